```python
import math
import jax, jax.numpy as jnp
from jax import lax
import numpy as np

D_MODEL = 1024
BATCH = 8
SEQ = 8192
DEPTH = 1

GRID_W = 64
CTX_LEN = 256
MIX_WIDTH = D_MODEL
POOL_WIDTH = MIX_WIDTH // 2
S5_WIDTH = MIX_WIDTH - POOL_WIDTH
POOL_WINDOWS = (2, 4, 8, 16)
N_POOL_GROUPS = len(POOL_WINDOWS)
POOL_CH = POOL_WIDTH // N_POOL_GROUPS
S5_GROUP = 16
S5_GROUPS = S5_WIDTH // S5_GROUP
S5_STATE = 64
D_FF = 2816
N_MOD = 9
EPS = 1e-6
DT_MIN = 1e-3
DT_MAX = 1e-1

kernel_name = "hybrid_pool_s5_macaron_dit_layer"


def rmsnorm(x, g):
    xf = x.astype(jnp.float32)
    y = xf * lax.rsqrt(jnp.mean(xf * xf, axis=-1, keepdims=True) + EPS)
    return (y * g.astype(jnp.float32)).astype(x.dtype)


def modulate(h, shift, scale):
    return h * (1 + scale) + shift


def swiglu(h, w1, w3, w2):
    return (jax.nn.silu(h @ w1) * (h @ w3)) @ w2


def ffn_sublayer(x, g, shift, scale, gate, w1, w3, w2):
    h = modulate(rmsnorm(x, g), shift, scale)
    return x + 0.5 * gate * swiglu(h, w1, w3, w2)


def window_mean(v, w, axis):
    n = v.shape[axis]
    lo = w // 2
    hi = w - 1 - lo
    t = np.arange(n)
    i_hi = np.clip(t + hi + 1, 0, n)
    i_lo = np.clip(t - lo, 0, n)
    vf = jnp.moveaxis(v.astype(jnp.float32), axis, 0)
    s = jnp.concatenate([jnp.zeros_like(vf[:1]), jnp.cumsum(vf, axis=0)], axis=0)
    cnt = (i_hi - i_lo).astype(np.float32).reshape((n,) + (1,) * (vf.ndim - 1))
    m = (s[i_hi] - s[i_lo]) / cnt
    return jnp.moveaxis(m, 0, axis).astype(v.dtype)


def pool_mixer(u, w_pool, scale, rows):
    bn, length, _ = u.shape
    ug = u.reshape(bn, length, N_POOL_GROUPS, POOL_CH)
    outs = []
    for k, w in enumerate(POOL_WINDOWS):
        v = ug[:, :, k]
        if rows is None:
            p = window_mean(v, w, 1)
        else:
            v2 = v.reshape(bn, rows, GRID_W, POOL_CH)
            p = window_mean(window_mean(v2, w, 1), w, 2).reshape(bn, length, POOL_CH)
        outs.append(p - v)
    d = jnp.stack(outs, axis=2)
    y = jnp.einsum('blkc,kcd->blkd', d, w_pool).reshape(bn, length, POOL_WIDTH)
    return y * scale


def s5_discretise(a_re, a_im, log_dt, b):
    lam = lax.complex(a_re.astype(jnp.float32), a_im.astype(jnp.float32))
    dt = jnp.exp(log_dt.astype(jnp.float32))[:, None]
    abar = jnp.exp(lam * dt)
    bbar = ((abar - 1) / lam)[..., None] * b
    return abar, bbar


def _ssm_combine(e1, e2):
    a1, b1 = e1
    a2, b2 = e2
    return a2 * a1, a2 * b1 + b2


def s5_scan(abar, bbar, u, h0, reverse):
    bu = jnp.einsum('gpc,blgc->blgp', bbar, u.astype(jnp.float32).astype(jnp.complex64))
    if h0 is not None:
        edge = -1 if reverse else 0
        bu = bu.at[:, edge].add(abar * h0)
    a = jnp.broadcast_to(abar, bu.shape)
    _, h = lax.associative_scan(_ssm_combine, (a, bu), axis=1, reverse=reverse)
    return h


def s5_readout(h_f, h_b, u, c_f, c_b, d_skip, w_glu):
    bn, length = u.shape[:2]
    y = (jnp.einsum('gcp,blgp->blgc', c_f, h_f).real
         + jnp.einsum('gcp,blgp->blgc', c_b, h_b).real)
    y = y.reshape(bn, length, S5_WIDTH) + d_skip.astype(jnp.float32) * u.reshape(bn, length, S5_WIDTH).astype(jnp.float32)
    y = jax.nn.gelu(y)
    y = y * jax.nn.sigmoid(y @ w_glu.astype(jnp.float32))
    return y.astype(u.dtype)


def _fwd_setup_inputs(seed: int = 0) -> dict:
    key = jax.random.key(seed)
    ks = jax.random.split(key, 24)
    f32 = jnp.float32
    D, F, G, P = D_MODEL, D_FF, S5_GROUPS, S5_STATE
    nrm = lambda k, shape, s: jax.random.normal(k, shape, f32) * s
    n_idx = jnp.arange(P, dtype=f32)
    log_dt = jax.random.uniform(ks[14], (DEPTH, 2, G), f32, math.log(DT_MIN), math.log(DT_MAX))
    return {
        "x": nrm(ks[0], (BATCH, SEQ, D), 1.0),
        "c": nrm(ks[1], (BATCH, D), 1.0),
        "ctx": nrm(ks[2], (BATCH, CTX_LEN, D), 1.0),
        "c_ctx": nrm(ks[3], (D,), 1.0),
        "norm_g": 1.0 + nrm(ks[4], (DEPTH, 3, D), 0.02),
        "w_ada": nrm(ks[5], (DEPTH, D, N_MOD * D), 0.5 * D ** -0.5),
        "b_ada": nrm(ks[6], (DEPTH, N_MOD * D), 0.01),
        "ffn_w1": nrm(ks[7], (DEPTH, 2, D, F), D ** -0.5),
        "ffn_w3": nrm(ks[8], (DEPTH, 2, D, F), D ** -0.5),
        "ffn_w2": nrm(ks[9], (DEPTH, 2, F, D), F ** -0.5),
        "w_in": nrm(ks[10], (DEPTH, D, MIX_WIDTH), D ** -0.5),
        "pool_w": nrm(ks[11], (DEPTH, N_POOL_GROUPS, POOL_CH, POOL_CH), POOL_CH ** -0.5),
        "pool_scale": 1.0 + nrm(ks[12], (DEPTH, POOL_WIDTH), 0.02),
        "s5_a_re": -0.5 + nrm(ks[13], (DEPTH, 2, G, P), 0.01),
        "s5_a_im": math.pi * n_idx + nrm(ks[15], (DEPTH, 2, G, P), 0.01),
        "s5_log_dt": log_dt,
        "s5_b_re": nrm(ks[16], (DEPTH, G, P, S5_GROUP), (2.0 * S5_GROUP) ** -0.5),
        "s5_b_im": nrm(ks[17], (DEPTH, G, P, S5_GROUP), (2.0 * S5_GROUP) ** -0.5),
        "s5_c_re": nrm(ks[18], (DEPTH, 2, G, S5_GROUP, P), (2.0 * P) ** -0.5),
        "s5_c_im": nrm(ks[19], (DEPTH, 2, G, S5_GROUP, P), (2.0 * P) ** -0.5),
        "s5_d": nrm(ks[20], (DEPTH, S5_WIDTH), 1.0),
        "s5_w_glu": nrm(ks[21], (DEPTH, S5_WIDTH, S5_WIDTH), S5_WIDTH ** -0.5),
        "w_out": nrm(ks[22], (DEPTH, MIX_WIDTH, D), MIX_WIDTH ** -0.5),
        "final_g": 1.0 + nrm(ks[23], (D,), 0.02),
    }


def _fwd_reference(x, c, ctx, c_ctx, norm_g, w_ada, b_ada, ffn_w1, ffn_w3, ffn_w2, w_in,
              pool_w, pool_scale, s5_a_re, s5_a_im, s5_log_dt, s5_b_re, s5_b_im,
              s5_c_re, s5_c_im, s5_d, s5_w_glu, w_out, final_g):
    f32 = jnp.float32
    bn, length, _ = x.shape
    rows = length // GRID_W
    lc = ctx.shape[1]
    xl, xc = x, ctx
    for l in range(DEPTH):
        last = l == DEPTH - 1
        ml = jnp.split((jax.nn.silu(c) @ w_ada[l] + b_ada[l])[:, None, :], N_MOD, axis=-1)
        mc = jnp.split((jax.nn.silu(c_ctx) @ w_ada[l] + b_ada[l])[None, None, :], N_MOD, axis=-1)

        xl = ffn_sublayer(xl, norm_g[l, 0], ml[0], ml[1], ml[2], ffn_w1[l, 0], ffn_w3[l, 0], ffn_w2[l, 0])
        xc = ffn_sublayer(xc, norm_g[l, 0], mc[0], mc[1], mc[2], ffn_w1[l, 0], ffn_w3[l, 0], ffn_w2[l, 0])

        hl = modulate(rmsnorm(xl, norm_g[l, 1]), ml[3], ml[4]) @ w_in[l]
        hc = modulate(rmsnorm(xc, norm_g[l, 1]), mc[3], mc[4]) @ w_in[l]
        pool_in_l, s5_in_l = hl[..., :POOL_WIDTH], hl[..., POOL_WIDTH:]
        pool_in_c, s5_in_c = hc[..., :POOL_WIDTH], hc[..., POOL_WIDTH:]

        b_cplx = lax.complex(s5_b_re[l].astype(f32), s5_b_im[l].astype(f32))
        abar_f, bbar_f = s5_discretise(s5_a_re[l, 0], s5_a_im[l, 0], s5_log_dt[l, 0], b_cplx)
        abar_b, bbar_b = s5_discretise(s5_a_re[l, 1], s5_a_im[l, 1], s5_log_dt[l, 1], b_cplx)
        c_f = lax.complex(s5_c_re[l, 0].astype(f32), s5_c_im[l, 0].astype(f32))
        c_b = lax.complex(s5_c_re[l, 1].astype(f32), s5_c_im[l, 1].astype(f32))

        uc = s5_in_c.reshape(bn, lc, S5_GROUPS, S5_GROUP)
        ul = s5_in_l.reshape(bn, length, S5_GROUPS, S5_GROUP)
        hc_f = s5_scan(abar_f, bbar_f, uc, None, False)
        hc_b = s5_scan(abar_b, bbar_b, uc, None, True)
        hl_f = s5_scan(abar_f, bbar_f, ul, hc_f[:, -1], False)
        hl_b = s5_scan(abar_b, bbar_b, ul, hc_b[:, 0], True)

        mix_l = jnp.concatenate([
            pool_mixer(pool_in_l, pool_w[l], pool_scale[l], rows),
            s5_readout(hl_f, hl_b, ul, c_f, c_b, s5_d[l], s5_w_glu[l]),
        ], axis=-1) @ w_out[l]
        xl = xl + ml[5] * mix_l
        if not last:
            mix_c = jnp.concatenate([
                pool_mixer(pool_in_c, pool_w[l], pool_scale[l], None),
                s5_readout(hc_f, hc_b, uc, c_f, c_b, s5_d[l], s5_w_glu[l]),
            ], axis=-1) @ w_out[l]
            xc = xc + mc[5] * mix_c

        xl = ffn_sublayer(xl, norm_g[l, 2], ml[6], ml[7], ml[8], ffn_w1[l, 1], ffn_w3[l, 1], ffn_w2[l, 1])
        if not last:
            xc = ffn_sublayer(xc, norm_g[l, 2], mc[6], mc[7], mc[8], ffn_w1[l, 1], ffn_w3[l, 1], ffn_w2[l, 1])
    return rmsnorm(xl, final_g)


import jax as _jax
import jax.numpy as _jnp

TWIN_FORMAT = 'train_step'
FWD_PARAMS = ['x', 'c', 'ctx', 'c_ctx', 'norm_g', 'w_ada', 'b_ada', 'ffn_w1', 'ffn_w3', 'ffn_w2', 'w_in', 'pool_w', 'pool_scale', 's5_a_re', 's5_a_im', 's5_log_dt', 's5_b_re', 's5_b_im', 's5_c_re', 's5_c_im', 's5_d', 's5_w_glu', 'w_out', 'final_g']
TWIN_WEIGHTS = ['c_ctx', 'norm_g', 'w_ada', 'b_ada', 'ffn_w1', 'ffn_w3', 'ffn_w2', 'w_in', 'pool_w', 'pool_scale', 's5_a_re', 's5_a_im', 's5_log_dt', 's5_b_re', 's5_b_im', 's5_c_re', 's5_c_im', 's5_d', 's5_w_glu', 'w_out', 'final_g']
TWIN_DIFF_INPUT = 'x'
TWIN_INPUTS = ['x', 'c', 'ctx', 'c_ctx', 'norm_g', 'w_ada', 'b_ada', 'ffn_w1', 'ffn_w3', 'ffn_w2', 'w_in', 'pool_w', 'pool_scale', 's5_a_re', 's5_a_im', 's5_log_dt', 's5_b_re', 's5_b_im', 's5_c_re', 's5_c_im', 's5_d', 's5_w_glu', 'w_out', 'final_g', 'loss_target', 'm_c_ctx', 'm_norm_g', 'm_w_ada', 'm_b_ada', 'm_ffn_w1', 'm_ffn_w3', 'm_ffn_w2', 'm_w_in', 'm_pool_w', 'm_pool_scale', 'm_s5_a_re', 'm_s5_a_im', 'm_s5_log_dt', 'm_s5_b_re', 'm_s5_b_im', 'm_s5_c_re', 'm_s5_c_im', 'm_s5_d', 'm_s5_w_glu', 'm_w_out', 'm_final_g', 'v_c_ctx', 'v_norm_g', 'v_w_ada', 'v_b_ada', 'v_ffn_w1', 'v_ffn_w3', 'v_ffn_w2', 'v_w_in', 'v_pool_w', 'v_pool_scale', 'v_s5_a_re', 'v_s5_a_im', 'v_s5_log_dt', 'v_s5_b_re', 'v_s5_b_im', 'v_s5_c_re', 'v_s5_c_im', 'v_s5_d', 'v_s5_w_glu', 'v_w_out', 'v_final_g']
TWIN_OUTPUTS = ['loss', 'grad_x', 'grad_c_ctx', 'grad_norm_g', 'grad_w_ada', 'grad_b_ada', 'grad_ffn_w1', 'grad_ffn_w3', 'grad_ffn_w2', 'grad_w_in', 'grad_pool_w', 'grad_pool_scale', 'grad_s5_a_re', 'grad_s5_a_im', 'grad_s5_log_dt', 'grad_s5_b_re', 'grad_s5_b_im', 'grad_s5_c_re', 'grad_s5_c_im', 'grad_s5_d', 'grad_s5_w_glu', 'grad_w_out', 'grad_final_g', 'delta_c_ctx', 'delta_norm_g', 'delta_w_ada', 'delta_b_ada', 'delta_ffn_w1', 'delta_ffn_w3', 'delta_ffn_w2', 'delta_w_in', 'delta_pool_w', 'delta_pool_scale', 'delta_s5_a_re', 'delta_s5_a_im', 'delta_s5_log_dt', 'delta_s5_b_re', 'delta_s5_b_im', 'delta_s5_c_re', 'delta_s5_c_im', 'delta_s5_d', 'delta_s5_w_glu', 'delta_w_out', 'delta_final_g', 'new_m_c_ctx', 'new_m_norm_g', 'new_m_w_ada', 'new_m_b_ada', 'new_m_ffn_w1', 'new_m_ffn_w3', 'new_m_ffn_w2', 'new_m_w_in', 'new_m_pool_w', 'new_m_pool_scale', 'new_m_s5_a_re', 'new_m_s5_a_im', 'new_m_s5_log_dt', 'new_m_s5_b_re', 'new_m_s5_b_im', 'new_m_s5_c_re', 'new_m_s5_c_im', 'new_m_s5_d', 'new_m_s5_w_glu', 'new_m_w_out', 'new_m_final_g', 'new_v_c_ctx', 'new_v_norm_g', 'new_v_w_ada', 'new_v_b_ada', 'new_v_ffn_w1', 'new_v_ffn_w3', 'new_v_ffn_w2', 'new_v_w_in', 'new_v_pool_w', 'new_v_pool_scale', 'new_v_s5_a_re', 'new_v_s5_a_im', 'new_v_s5_log_dt', 'new_v_s5_b_re', 'new_v_s5_b_im', 'new_v_s5_c_re', 'new_v_s5_c_im', 'new_v_s5_d', 'new_v_s5_w_glu', 'new_v_w_out', 'new_v_final_g']
TWIN_LEAF_KINDS = {'loss': 'loss', 'grad_x': 'grad_x', 'grad_c_ctx': 'grad_w', 'grad_norm_g': 'grad_w', 'grad_w_ada': 'grad_w', 'grad_b_ada': 'grad_w', 'grad_ffn_w1': 'grad_w', 'grad_ffn_w3': 'grad_w', 'grad_ffn_w2': 'grad_w', 'grad_w_in': 'grad_w', 'grad_pool_w': 'grad_w', 'grad_pool_scale': 'grad_w', 'grad_s5_a_re': 'grad_w', 'grad_s5_a_im': 'grad_w', 'grad_s5_log_dt': 'grad_w', 'grad_s5_b_re': 'grad_w', 'grad_s5_b_im': 'grad_w', 'grad_s5_c_re': 'grad_w', 'grad_s5_c_im': 'grad_w', 'grad_s5_d': 'grad_w', 'grad_s5_w_glu': 'grad_w', 'grad_w_out': 'grad_w', 'grad_final_g': 'grad_w', 'delta_c_ctx': 'delta_w', 'delta_norm_g': 'delta_w', 'delta_w_ada': 'delta_w', 'delta_b_ada': 'delta_w', 'delta_ffn_w1': 'delta_w', 'delta_ffn_w3': 'delta_w', 'delta_ffn_w2': 'delta_w', 'delta_w_in': 'delta_w', 'delta_pool_w': 'delta_w', 'delta_pool_scale': 'delta_w', 'delta_s5_a_re': 'delta_w', 'delta_s5_a_im': 'delta_w', 'delta_s5_log_dt': 'delta_w', 'delta_s5_b_re': 'delta_w', 'delta_s5_b_im': 'delta_w', 'delta_s5_c_re': 'delta_w', 'delta_s5_c_im': 'delta_w', 'delta_s5_d': 'delta_w', 'delta_s5_w_glu': 'delta_w', 'delta_w_out': 'delta_w', 'delta_final_g': 'delta_w', 'new_m_c_ctx': 'new_m', 'new_m_norm_g': 'new_m', 'new_m_w_ada': 'new_m', 'new_m_b_ada': 'new_m', 'new_m_ffn_w1': 'new_m', 'new_m_ffn_w3': 'new_m', 'new_m_ffn_w2': 'new_m', 'new_m_w_in': 'new_m', 'new_m_pool_w': 'new_m', 'new_m_pool_scale': 'new_m', 'new_m_s5_a_re': 'new_m', 'new_m_s5_a_im': 'new_m', 'new_m_s5_log_dt': 'new_m', 'new_m_s5_b_re': 'new_m', 'new_m_s5_b_im': 'new_m', 'new_m_s5_c_re': 'new_m', 'new_m_s5_c_im': 'new_m', 'new_m_s5_d': 'new_m', 'new_m_s5_w_glu': 'new_m', 'new_m_w_out': 'new_m', 'new_m_final_g': 'new_m', 'new_v_c_ctx': 'new_v', 'new_v_norm_g': 'new_v', 'new_v_w_ada': 'new_v', 'new_v_b_ada': 'new_v', 'new_v_ffn_w1': 'new_v', 'new_v_ffn_w3': 'new_v', 'new_v_ffn_w2': 'new_v', 'new_v_w_in': 'new_v', 'new_v_pool_w': 'new_v', 'new_v_pool_scale': 'new_v', 'new_v_s5_a_re': 'new_v', 'new_v_s5_a_im': 'new_v', 'new_v_s5_log_dt': 'new_v', 'new_v_s5_b_re': 'new_v', 'new_v_s5_b_im': 'new_v', 'new_v_s5_c_re': 'new_v', 'new_v_s5_c_im': 'new_v', 'new_v_s5_d': 'new_v', 'new_v_s5_w_glu': 'new_v', 'new_v_w_out': 'new_v', 'new_v_final_g': 'new_v'}


def _forward(args):
    return _fwd_reference(*[args[k] for k in FWD_PARAMS])


def _output_shape():
    def fwd():
        inp = _fwd_setup_inputs(0)
        return _fwd_reference(*[inp[k] for k in FWD_PARAMS])
    out = _jax.eval_shape(fwd)
    return out.shape, out.dtype

N_MICROBATCH = 1
ADAM_LR = 0.001
ADAM_B1 = 0.9
ADAM_B2 = 0.999
ADAM_EPS = 1e-08
ADAM_WD = 0.01
ADAM_STEP = 10
PER_EXAMPLE_BATCH_AXIS = {'x': 0, 'c': 0, 'ctx': 0, 'loss_target': 0}
SHARED_INPUTS = []
_WEIGHT_DTYPES = {'c_ctx': _jnp.float32, 'norm_g': _jnp.float32, 'w_ada': _jnp.float32, 'b_ada': _jnp.float32, 'ffn_w1': _jnp.float32, 'ffn_w3': _jnp.float32, 'ffn_w2': _jnp.float32, 'w_in': _jnp.float32, 'pool_w': _jnp.float32, 'pool_scale': _jnp.float32, 's5_a_re': _jnp.float32, 's5_a_im': _jnp.float32, 's5_log_dt': _jnp.float32, 's5_b_re': _jnp.float32, 's5_b_im': _jnp.float32, 's5_c_re': _jnp.float32, 's5_c_im': _jnp.float32, 's5_d': _jnp.float32, 's5_w_glu': _jnp.float32, 'w_out': _jnp.float32, 'final_g': _jnp.float32}
MOMENT_SCALE = {'c_ctx': 5.213032e-04, 'norm_g': 4.676807e-02, 'w_ada': 5.105604e-02, 'b_ada': 8.747791e-02, 'ffn_w1': 1.656934e-02, 'ffn_w3': 1.605926e-02, 'ffn_w2': 2.662707e-02, 'w_in': 5.756897e-02, 'pool_w': 7.739175e-02, 'pool_scale': 7.553450e-02, 's5_a_re': 1.676120e-03, 's5_a_im': 2.069891e-03, 's5_log_dt': 8.400565e-01, 's5_b_re': 1.633608e-03, 's5_b_im': 1.605477e-03, 's5_c_re': 2.129253e-03, 's5_c_im': 2.367310e-03, 's5_d': 2.994112e-02, 's5_w_glu': 8.430632e-03, 'w_out': 5.732557e-02, 'final_g': 6.399352e+01}


def _to_microbatches(a, axis):
    t = _jnp.moveaxis(a, axis, 0)
    t = t.reshape((N_MICROBATCH, t.shape[0] // N_MICROBATCH) + t.shape[1:])
    return _jnp.moveaxis(t, 1, axis + 1)


def setup_inputs(seed: int = 0) -> dict:
    inp = _fwd_setup_inputs(seed)
    key = _jax.random.fold_in(_jax.random.key(seed), 7919)
    shape, _ = _output_shape()
    out = dict(inp)
    out["loss_target"] = _jax.random.normal(_jax.random.fold_in(key, 0), shape, _jnp.float32)
    for i, name in enumerate(TWIN_WEIGHTS):
        w = inp[name].astype(_jnp.float32)
        if MOMENT_SCALE is None:
            s = _jnp.sqrt(_jnp.mean(_jnp.square(w)) + 1e-30)
        else:
            s = MOMENT_SCALE[name]
        km, kv = _jax.random.split(_jax.random.fold_in(key, i + 1))
        out[name] = w
        out["m_" + name] = s * _jax.random.normal(km, w.shape, _jnp.float32)
        out["v_" + name] = (s * s) * _jax.random.uniform(kv, w.shape, _jnp.float32, 0.5, 1.5)
    if N_MICROBATCH > 1:
        for name, axis in PER_EXAMPLE_BATCH_AXIS.items():
            out[name] = _to_microbatches(out[name], axis)
    return {'x': out['x'], 'c': out['c'], 'ctx': out['ctx'], 'c_ctx': out['c_ctx'], 'norm_g': out['norm_g'], 'w_ada': out['w_ada'], 'b_ada': out['b_ada'], 'ffn_w1': out['ffn_w1'], 'ffn_w3': out['ffn_w3'], 'ffn_w2': out['ffn_w2'], 'w_in': out['w_in'], 'pool_w': out['pool_w'], 'pool_scale': out['pool_scale'], 's5_a_re': out['s5_a_re'], 's5_a_im': out['s5_a_im'], 's5_log_dt': out['s5_log_dt'], 's5_b_re': out['s5_b_re'], 's5_b_im': out['s5_b_im'], 's5_c_re': out['s5_c_re'], 's5_c_im': out['s5_c_im'], 's5_d': out['s5_d'], 's5_w_glu': out['s5_w_glu'], 'w_out': out['w_out'], 'final_g': out['final_g'], 'loss_target': out['loss_target'], 'm_c_ctx': out['m_c_ctx'], 'm_norm_g': out['m_norm_g'], 'm_w_ada': out['m_w_ada'], 'm_b_ada': out['m_b_ada'], 'm_ffn_w1': out['m_ffn_w1'], 'm_ffn_w3': out['m_ffn_w3'], 'm_ffn_w2': out['m_ffn_w2'], 'm_w_in': out['m_w_in'], 'm_pool_w': out['m_pool_w'], 'm_pool_scale': out['m_pool_scale'], 'm_s5_a_re': out['m_s5_a_re'], 'm_s5_a_im': out['m_s5_a_im'], 'm_s5_log_dt': out['m_s5_log_dt'], 'm_s5_b_re': out['m_s5_b_re'], 'm_s5_b_im': out['m_s5_b_im'], 'm_s5_c_re': out['m_s5_c_re'], 'm_s5_c_im': out['m_s5_c_im'], 'm_s5_d': out['m_s5_d'], 'm_s5_w_glu': out['m_s5_w_glu'], 'm_w_out': out['m_w_out'], 'm_final_g': out['m_final_g'], 'v_c_ctx': out['v_c_ctx'], 'v_norm_g': out['v_norm_g'], 'v_w_ada': out['v_w_ada'], 'v_b_ada': out['v_b_ada'], 'v_ffn_w1': out['v_ffn_w1'], 'v_ffn_w3': out['v_ffn_w3'], 'v_ffn_w2': out['v_ffn_w2'], 'v_w_in': out['v_w_in'], 'v_pool_w': out['v_pool_w'], 'v_pool_scale': out['v_pool_scale'], 'v_s5_a_re': out['v_s5_a_re'], 'v_s5_a_im': out['v_s5_a_im'], 'v_s5_log_dt': out['v_s5_log_dt'], 'v_s5_b_re': out['v_s5_b_re'], 'v_s5_b_im': out['v_s5_b_im'], 'v_s5_c_re': out['v_s5_c_re'], 'v_s5_c_im': out['v_s5_c_im'], 'v_s5_d': out['v_s5_d'], 'v_s5_w_glu': out['v_s5_w_glu'], 'v_w_out': out['v_w_out'], 'v_final_g': out['v_final_g']}


def _loss(weights, diff, rest, loss_target):
    with _jax.named_scope("forward"):
        args = {**rest, TWIN_DIFF_INPUT: diff, **{k: w.astype(_WEIGHT_DTYPES[k]) for k, w in weights.items()}}
        y = _forward(args)
    with _jax.named_scope("loss_head"):
        err = _jnp.square(y.astype(_jnp.float32) - loss_target)
        return 0.5 * _jnp.sum(_jnp.mean(err, axis=-1)) if err.ndim else 0.5 * err


def _adamw(w, g, m, v):
    m = ADAM_B1 * m + (1.0 - ADAM_B1) * g
    v = ADAM_B2 * v + (1.0 - ADAM_B2) * _jnp.square(g)
    m_hat = m / (1.0 - ADAM_B1 ** ADAM_STEP)
    v_hat = v / (1.0 - ADAM_B2 ** ADAM_STEP)
    delta = -ADAM_LR * (m_hat / (_jnp.sqrt(v_hat) + ADAM_EPS) + ADAM_WD * w)
    return delta, m, v


def reference(x, c, ctx, c_ctx, norm_g, w_ada, b_ada, ffn_w1, ffn_w3, ffn_w2, w_in, pool_w, pool_scale, s5_a_re, s5_a_im, s5_log_dt, s5_b_re, s5_b_im, s5_c_re, s5_c_im, s5_d, s5_w_glu, w_out, final_g, loss_target, m_c_ctx, m_norm_g, m_w_ada, m_b_ada, m_ffn_w1, m_ffn_w3, m_ffn_w2, m_w_in, m_pool_w, m_pool_scale, m_s5_a_re, m_s5_a_im, m_s5_log_dt, m_s5_b_re, m_s5_b_im, m_s5_c_re, m_s5_c_im, m_s5_d, m_s5_w_glu, m_w_out, m_final_g, v_c_ctx, v_norm_g, v_w_ada, v_b_ada, v_ffn_w1, v_ffn_w3, v_ffn_w2, v_w_in, v_pool_w, v_pool_scale, v_s5_a_re, v_s5_a_im, v_s5_log_dt, v_s5_b_re, v_s5_b_im, v_s5_c_re, v_s5_c_im, v_s5_d, v_s5_w_glu, v_w_out, v_final_g):
    given = dict(x=x, c=c, ctx=ctx, c_ctx=c_ctx, norm_g=norm_g, w_ada=w_ada, b_ada=b_ada, ffn_w1=ffn_w1, ffn_w3=ffn_w3, ffn_w2=ffn_w2, w_in=w_in, pool_w=pool_w, pool_scale=pool_scale, s5_a_re=s5_a_re, s5_a_im=s5_a_im, s5_log_dt=s5_log_dt, s5_b_re=s5_b_re, s5_b_im=s5_b_im, s5_c_re=s5_c_re, s5_c_im=s5_c_im, s5_d=s5_d, s5_w_glu=s5_w_glu, w_out=w_out, final_g=final_g, loss_target=loss_target, m_c_ctx=m_c_ctx, m_norm_g=m_norm_g, m_w_ada=m_w_ada, m_b_ada=m_b_ada, m_ffn_w1=m_ffn_w1, m_ffn_w3=m_ffn_w3, m_ffn_w2=m_ffn_w2, m_w_in=m_w_in, m_pool_w=m_pool_w, m_pool_scale=m_pool_scale, m_s5_a_re=m_s5_a_re, m_s5_a_im=m_s5_a_im, m_s5_log_dt=m_s5_log_dt, m_s5_b_re=m_s5_b_re, m_s5_b_im=m_s5_b_im, m_s5_c_re=m_s5_c_re, m_s5_c_im=m_s5_c_im, m_s5_d=m_s5_d, m_s5_w_glu=m_s5_w_glu, m_w_out=m_w_out, m_final_g=m_final_g, v_c_ctx=v_c_ctx, v_norm_g=v_norm_g, v_w_ada=v_w_ada, v_b_ada=v_b_ada, v_ffn_w1=v_ffn_w1, v_ffn_w3=v_ffn_w3, v_ffn_w2=v_ffn_w2, v_w_in=v_w_in, v_pool_w=v_pool_w, v_pool_scale=v_pool_scale, v_s5_a_re=v_s5_a_re, v_s5_a_im=v_s5_a_im, v_s5_log_dt=v_s5_log_dt, v_s5_b_re=v_s5_b_re, v_s5_b_im=v_s5_b_im, v_s5_c_re=v_s5_c_re, v_s5_c_im=v_s5_c_im, v_s5_d=v_s5_d, v_s5_w_glu=v_s5_w_glu, v_w_out=v_w_out, v_final_g=v_final_g)
    weights = {n: given[n] for n in TWIN_WEIGHTS}
    shared = {n: given[n] for n in SHARED_INPUTS}
    per_example = {n: given[n] for n in ['x', 'c', 'ctx']}
    grad_fn = _jax.value_and_grad(_loss, argnums=(0, 1))

    def one_microbatch(ex, loss_target):
        ex = dict(ex)
        diff = ex.pop(TWIN_DIFF_INPUT)
        return grad_fn(weights, diff, {**shared, **ex}, loss_target)

    if N_MICROBATCH == 1:
        loss, (grad_w, grad_x) = one_microbatch(per_example, given["loss_target"])
    else:
        def body(carry, xs):
            loss_sum, grad_sum = carry
            l_k, (gw_k, gx_k) = one_microbatch(xs[0], xs[1])
            with _jax.named_scope("update"):
                return (loss_sum + l_k, _jax.tree.map(_jnp.add, grad_sum, gw_k)), gx_k

        init = (_jnp.zeros((), _jnp.float32), _jax.tree.map(_jnp.zeros_like, weights))
        (loss, grad_w), grad_x = _jax.lax.scan(body, init, (per_example, given["loss_target"]))
    with _jax.named_scope("update"):
        delta_w, new_m, new_v = {}, {}, {}
        for n in TWIN_WEIGHTS:
            delta_w[n], new_m[n], new_v[n] = _adamw(weights[n], grad_w[n], given["m_" + n], given["v_" + n])
    return (loss, grad_x, *[grad_w[n] for n in TWIN_WEIGHTS], *[delta_w[n] for n in TWIN_WEIGHTS],
            *[new_m[n] for n in TWIN_WEIGHTS], *[new_v[n] for n in TWIN_WEIGHTS])
```

```python
import functools
import math

import numpy as np
import jax
import jax.numpy as jnp
from jax import lax
from jax.experimental import pallas as pl
from jax.experimental.pallas import tpu as pltpu

F32 = jnp.float32
BF16 = jnp.bfloat16
AXES = ("x", "y", "c")
NDEV = 8
EPS = 1e-6
TM = 256
TC = 256
SEG = TC // 8
NTAB = SEG + 16
GRID_W = 64
POOL_WINDOWS = (2, 4, 8, 16)
LANE = 128
FPAD = 384
VMEM_LIMIT = 56 * 1024 * 1024
ADAM_LR, ADAM_B1, ADAM_B2, ADAM_EPS, ADAM_WD, ADAM_STEP = 0.001, 0.9, 0.999, 1e-08, 0.01, 10


def _pcall(body, **kw):
    return pl.pallas_call(body, **kw)


def _params(sem):
    return pltpu.CompilerParams(dimension_semantics=sem, vmem_limit_bytes=VMEM_LIMIT)


def _dot(a, b):
    return jnp.dot(a, b, preferred_element_type=F32)


def _dot_nt(a, b):
    return lax.dot_general(a, b, (((1,), (1,)), ((), ())), preferred_element_type=F32)


def _dot_tn(a, b):
    return lax.dot_general(a, b, (((0,), (0,)), ((), ())), preferred_element_type=F32)


def _sigmoid(x):
    return 1.0 / (1.0 + jnp.exp(-x))


def _colsum(a):
    return jnp.sum(a, axis=0, keepdims=True)


def _resident(shape):
    nd = len(shape)
    return pl.BlockSpec(shape, lambda *_: (0,) * nd, pipeline_mode=pl.Buffered(1))


def _const(shape):
    nd = len(shape)
    return pl.BlockSpec(shape, lambda *_: (0,) * nd)


def _exchange(gathers, a2as, name):
    ng, na = len(gathers), len(a2as)
    n = ng + na
    ins = list(gathers) + list(a2as)
    out_shape = [jax.ShapeDtypeStruct((NDEV,) + tuple(a.shape), a.dtype) for a in gathers]
    out_shape += [jax.ShapeDtypeStruct(tuple(a.shape), a.dtype) for a in a2as]

    def body(*refs):
        in_refs, out_refs = refs[:n], refs[n:2 * n]
        send_sems, recv_sems, loc_sems = refs[2 * n:]
        x, y, c = (lax.axis_index(a) for a in AXES)
        me = 4 * x + 2 * y + c
        started = []
        for k in range(n):
            own = in_refs[k] if k < ng else in_refs[k].at[me]
            loc = pltpu.make_async_copy(own, out_refs[k].at[me], loc_sems.at[k])
            loc.start()
            started.append((loc, None))
            for p in range(1, NDEV):
                px = 1 - x if p & 4 else x
                py = 1 - y if p & 2 else y
                pc = 1 - c if p & 1 else c
                pid = 4 * px + 2 * py + pc
                src = in_refs[k] if k < ng else in_refs[k].at[pid]
                sem = k * (NDEV - 1) + p - 1
                send = pltpu.make_async_remote_copy(
                    src_ref=src, dst_ref=out_refs[k].at[me], send_sem=send_sems.at[sem], recv_sem=recv_sems.at[sem],
                    device_id=(px, py, pc), device_id_type=pl.DeviceIdType.MESH)
                send.start()
                recv = pltpu.make_async_remote_copy(
                    src_ref=src, dst_ref=out_refs[k].at[pid], send_sem=send_sems.at[sem], recv_sem=recv_sems.at[sem],
                    device_id=(px, py, pc), device_id_type=pl.DeviceIdType.MESH)
                started.append((send, recv))
        for send, recv in started:
            if recv is None:
                send.wait()
            else:
                recv.wait_recv()
                send.wait_send()

    any_spec = pl.BlockSpec(memory_space=pl.ANY)
    outs = _pcall(
        body, name=name, out_shape=out_shape, in_specs=[any_spec] * n, out_specs=[any_spec] * n,
        scratch_shapes=[pltpu.SemaphoreType.DMA((n * (NDEV - 1),)), pltpu.SemaphoreType.DMA((n * (NDEV - 1),)),
                        pltpu.SemaphoreType.DMA((n,))],
    )(*ins)
    return list(outs[:ng]), list(outs[ng:])


def _ada_fwd(c16, w_blk, b_blk):
    def body(c_ref, w_ref, b_ref, o_ref):
        cc = c_ref[...]
        s = (cc * _sigmoid(cc)).astype(BF16)
        o_ref[...] = _dot(s, w_ref[...].astype(BF16)) + b_ref[...]

    return _pcall(body, name="ada_fwd", out_shape=jax.ShapeDtypeStruct((16, w_blk.shape[1]), F32),
                  compiler_params=_params(None))(c16, w_blk, b_blk)


def _ada_bwd(c16, w_blk, dl, dc):
    d, w = w_blk.shape

    def body(c_ref, w_ref, dl_ref, dc_ref, gw_ref, gc_ref):
        cc = c_ref[...]
        sg = _sigmoid(cc)
        s = (cc * sg).astype(BF16)
        dctot = _colsum(dc_ref[...])
        dm = jnp.concatenate([dl_ref[...], jnp.broadcast_to(dctot, (8, w))], axis=0)
        rows = lax.broadcasted_iota(jnp.int32, (16, w), 0)
        dm = jnp.where(rows <= 8, dm, 0.0).astype(BF16)
        gw_ref[...] = _dot_tn(s, dm)
        t = _dot_nt(jnp.broadcast_to(dctot, (8, w)).astype(BF16), w_ref[...].astype(BF16))[0:1]
        c8, s8 = cc[8:9], sg[8:9]
        gc_ref[...] = t * (s8 * (1.0 + c8 * (1.0 - s8)))

    return _pcall(body, name="ada_bwd",
                  out_shape=(jax.ShapeDtypeStruct((d, w), F32), jax.ShapeDtypeStruct((1, d), F32)),
                  compiler_params=_params(None))(c16, w_blk, dl, dc)


def _norm_fwd(x, g, shift, scale):
    rstd = lax.rsqrt(jnp.mean(x * x, axis=-1, keepdims=True) + EPS)
    xhat = x * rstd
    xn = xhat * g
    return rstd, xhat, xn, xn * (1.0 + scale) + shift


def _norm_bwd(dh, rstd, xhat, xn, g, scale):
    dxn = dh * (1.0 + scale)
    dxhat = dxn * g
    dx = rstd * (dxhat - xhat * jnp.mean(dxhat * xhat, axis=-1, keepdims=True))
    return dx, _colsum(dh), _colsum(dh * xn), _colsum(dxn * xhat)


def _mod_spec(d, has_ctx):
    if has_ctx:
        return pl.BlockSpec((1, 9, d), lambda *ids: (jnp.minimum(ids[-1], 1), 0, 0))
    return pl.BlockSpec((1, 9, d), lambda *ids: (1, 0, 0))


def _dmod_spec(nrow, d, has_ctx):
    if has_ctx:
        return pl.BlockSpec((1, nrow, d), lambda i: (jnp.minimum(i, 1), 0, 0))
    return pl.BlockSpec((1, nrow, d), lambda i: (0, 0, 0))


def _ffn_fwd(x, modtab, g, w13, w2, k0, has_ctx, name):
    t, d = x.shape
    nb, _, w = w13.shape
    fp = w // 2

    def body(x_ref, mod_ref, g_ref, w13_ref, w2_ref, xo_ref, h_ref, ab_ref, o_ref):
        xx = x_ref[...]
        shift, scale, gate = mod_ref[0, k0:k0 + 1, :], mod_ref[0, k0 + 1:k0 + 2, :], mod_ref[0, k0 + 2:k0 + 3, :]
        _, _, _, h = _norm_fwd(xx, g_ref[...], shift, scale)
        hb = h.astype(BF16)
        h_ref[...] = hb
        acc = jnp.zeros((TM, d), F32)
        for p in range(nb // 2):
            zs = []
            for q in range(2):
                blk = 2 * p + q
                ab = _dot(hb, w13_ref[blk])
                ab_ref[:, blk * w:(blk + 1) * w] = ab.astype(BF16)
                a, b = ab[:, :fp], ab[:, fp:]
                zs.append((a * _sigmoid(a) * b).astype(BF16))
            acc = acc + _dot(jnp.concatenate(zs, axis=1), w2_ref[p])
        o_ref[...] = acc.astype(BF16)
        xo_ref[...] = xx + (0.5 * gate) * acc

    row = lambda i: (i, 0)
    return _pcall(
        body, name=name, grid=(t // TM,),
        out_shape=(jax.ShapeDtypeStruct((t, d), F32), jax.ShapeDtypeStruct((t, d), BF16),
                   jax.ShapeDtypeStruct((t, nb * w), BF16), jax.ShapeDtypeStruct((t, d), BF16)),
        in_specs=[pl.BlockSpec((TM, d), row), _mod_spec(d, has_ctx), _const((1, d)),
                  _resident(w13.shape), _resident(w2.shape)],
        out_specs=(pl.BlockSpec((TM, d), row), pl.BlockSpec((TM, d), row), pl.BlockSpec((TM, nb * w), row),
                   pl.BlockSpec((TM, d), row)),
        compiler_params=_params(("arbitrary",)),
    )(x, modtab, g, w13, w2)


def _ffn_bwd(dy, modtab, h, ab, w13, w2, k0, has_ctx, name):
    t, d = dy.shape
    nb, _, w = w13.shape
    fp = w // 2
    npair = nb // 2
    nt = t // TM

    def body(dy_ref, mod_ref, h_ref, ab_ref, w13_ref, w2_ref, dh_ref, d13_ref, d2_ref, acc13, acc2):
        i = pl.program_id(1)

        @pl.when(i == 0)
        def _():
            acc13[...] = jnp.zeros_like(acc13)
            acc2[...] = jnp.zeros_like(acc2)

        gate = mod_ref[0, k0 + 2:k0 + 3, :]
        do = (dy_ref[...] * (0.5 * gate)).astype(BF16)
        dz = _dot_nt(do, w2_ref[0])
        hb = h_ref[...]
        dh = jnp.zeros((TM, d), F32)
        zs = []
        for q in range(2):
            ab = ab_ref[:, q * w:(q + 1) * w].astype(F32)
            a, b = ab[:, :fp], ab[:, fp:]
            sg = _sigmoid(a)
            sa = a * sg
            dzq = dz[:, q * fp:(q + 1) * fp]
            da = dzq * b * (sg * (1.0 + a * (1.0 - sg)))
            db = dzq * sa
            dab = jnp.concatenate([da, db], axis=1).astype(BF16)
            dh = dh + _dot_nt(dab, w13_ref[q])
            acc13[q] += _dot_tn(hb, dab)
            zs.append((sa * b).astype(BF16))
        acc2[...] += _dot_tn(jnp.concatenate(zs, axis=1), do)
        dh_ref[0] = dh

        @pl.when(i == nt - 1)
        def _():
            d13_ref[...] = acc13[...].astype(BF16)
            d2_ref[0] = acc2[...].astype(BF16)

    mod_spec = _mod_spec(d, has_ctx)
    return _pcall(
        body, name=name, grid=(npair, nt),
        out_shape=(jax.ShapeDtypeStruct((npair, t, d), F32), jax.ShapeDtypeStruct((nb, d, w), BF16),
                   jax.ShapeDtypeStruct((npair, 2 * fp, d), BF16)),
        in_specs=[pl.BlockSpec((TM, d), lambda p, i: (i, 0)), mod_spec, pl.BlockSpec((TM, d), lambda p, i: (i, 0)),
                  pl.BlockSpec((TM, 2 * w), lambda p, i: (i, p)), pl.BlockSpec((2, d, w), lambda p, i: (p, 0, 0)),
                  pl.BlockSpec((1, 2 * fp, d), lambda p, i: (p, 0, 0))],
        out_specs=(pl.BlockSpec((1, TM, d), lambda p, i: (p, i, 0)), pl.BlockSpec((2, d, w), lambda p, i: (p, 0, 0)),
                   pl.BlockSpec((1, 2 * fp, d), lambda p, i: (p, 0, 0))),
        scratch_shapes=[pltpu.VMEM((2, d, w), F32), pltpu.VMEM((2 * fp, d), F32)],
        compiler_params=_params(("arbitrary", "arbitrary")),
    )(dy, modtab, h, ab, w13, w2)


def _ffn_bwd_norm(dy, dhp, x, o, modtab, g, k0, has_ctx, name):
    t, d = dy.shape
    npair = dhp.shape[0]
    ngrp = 2 if has_ctx else 1

    def body(dy_ref, dhp_ref, x_ref, o_ref, mod_ref, g_ref, dx_ref, dmod_ref, dg_ref):
        i = pl.program_id(0)

        @pl.when(i == 0)
        def _():
            dg_ref[...] = jnp.zeros_like(dg_ref)

        @pl.when((i == 0) | (i == ngrp - 1))
        def _():
            dmod_ref[...] = jnp.zeros_like(dmod_ref)

        dh = dhp_ref[0]
        for p in range(1, npair):
            dh = dh + dhp_ref[p]
        scale = mod_ref[0, k0 + 1:k0 + 2, :]
        gg = g_ref[...]
        rstd, xhat, xn, _ = _norm_fwd(x_ref[...], gg, 0.0, scale)
        dxn, dshift, dscale, dg = _norm_bwd(dh, rstd, xhat, xn, gg, scale)
        dyv = dy_ref[...]
        dx_ref[...] = dyv + dxn
        dmod_ref[0, 0:1, :] += dshift
        dmod_ref[0, 1:2, :] += dscale
        dmod_ref[0, 2:3, :] += _colsum(0.5 * dyv * o_ref[...].astype(F32))
        dg_ref[...] += dg

    row = lambda i: (i, 0)
    return _pcall(
        body, name=name, grid=(t // TM,),
        out_shape=(jax.ShapeDtypeStruct((t, d), F32), jax.ShapeDtypeStruct((ngrp, 3, d), F32),
                   jax.ShapeDtypeStruct((1, d), F32)),
        in_specs=[pl.BlockSpec((TM, d), row), pl.BlockSpec((npair, TM, d), lambda i: (0, i, 0)),
                  pl.BlockSpec((TM, d), row), pl.BlockSpec((TM, d), row), _mod_spec(d, has_ctx), _const((1, d))],
        out_specs=(pl.BlockSpec((TM, d), row), _dmod_spec(3, d, has_ctx), _const((1, d))),
        compiler_params=_params(("arbitrary",)),
    )(dy, dhp, x, o, modtab, g)


def _mix_in_fwd(x1, modtab, g, w_in):
    t, d = x1.shape

    def body(x_ref, mod_ref, g_ref, w_ref, h_ref, u_ref):
        _, _, _, h = _norm_fwd(x_ref[...], g_ref[...], mod_ref[0, 3:4, :], mod_ref[0, 4:5, :])
        hb = h.astype(BF16)
        h_ref[...] = hb
        u_ref[...] = _dot(hb, w_ref[...])

    row = lambda i: (i, 0)
    return _pcall(
        body, name="mix_in_fwd", grid=(t // TM,),
        out_shape=(jax.ShapeDtypeStruct((t, d), BF16), jax.ShapeDtypeStruct((t, w_in.shape[1]), F32)),
        in_specs=[pl.BlockSpec((TM, d), row), _mod_spec(d, True), _const((1, d)), _resident(w_in.shape)],
        out_specs=(pl.BlockSpec((TM, d), row), pl.BlockSpec((TM, w_in.shape[1]), row)),
        compiler_params=_params(("arbitrary",)),
    )(x1, modtab, g, w_in)


def _mix_in_bwd(du_pool, du_f, du_b, dx2, h1, x1, modtab, g, w_in):
    t, d = x1.shape
    m = w_in.shape[1]
    hm = m // 2
    nt = t // TM

    def body(dup_ref, duf_ref, dub_ref, dx2_ref, h_ref, x_ref, mod_ref, g_ref, w_ref, dx_ref, dw_ref, dmod_ref, dg_ref, acc):
        i = pl.program_id(0)

        @pl.when(i == 0)
        def _():
            acc[...] = jnp.zeros_like(acc)
            dg_ref[...] = jnp.zeros_like(dg_ref)

        @pl.when(i <= 1)
        def _():
            dmod_ref[...] = jnp.zeros_like(dmod_ref)

        lat = (i > 0).astype(F32)
        du = jnp.concatenate([dup_ref[...] * lat, duf_ref[...] + dub_ref[...]], axis=1).astype(BF16)
        dh = _dot_nt(du, w_ref[...])
        acc[...] += _dot_tn(h_ref[...], du)
        scale = mod_ref[0, 4:5, :]
        gg = g_ref[...]
        rstd, xhat, xn, _ = _norm_fwd(x_ref[...], gg, 0.0, scale)
        dxn, dshift, dscale, dg = _norm_bwd(dh, rstd, xhat, xn, gg, scale)
        dx_ref[...] = dx2_ref[...] * lat + dxn
        dmod_ref[0, 0:1, :] += dshift
        dmod_ref[0, 1:2, :] += dscale
        dg_ref[...] += dg

        @pl.when(i == nt - 1)
        def _():
            dw_ref[...] = acc[...].astype(BF16)

    row = lambda i: (i, 0)
    lrow = lambda i: (jnp.maximum(i - 1, 0), 0)
    return _pcall(
        body, name="mix_in_bwd", grid=(nt,),
        out_shape=(jax.ShapeDtypeStruct((t, d), F32), jax.ShapeDtypeStruct((d, m), BF16),
                   jax.ShapeDtypeStruct((2, 2, d), F32), jax.ShapeDtypeStruct((1, d), F32)),
        in_specs=[pl.BlockSpec((TM, hm), lrow), pl.BlockSpec((TM, hm), row), pl.BlockSpec((TM, hm), row),
                  pl.BlockSpec((TM, d), lrow), pl.BlockSpec((TM, d), row), pl.BlockSpec((TM, d), row),
                  _mod_spec(d, True), _const((1, d)), _resident(w_in.shape)],
        out_specs=(pl.BlockSpec((TM, d), row), _const((d, m)), _dmod_spec(2, d, True), _const((1, d))),
        scratch_shapes=[pltpu.VMEM((d, m), F32)],
        compiler_params=_params(("arbitrary",)),
    )(du_pool, du_f, du_b, dx2, h1, x1, modtab, g, w_in)


def _pool(v, transpose, name):
    ngrp, rows, n = v.shape
    nchunk = 4
    cw = n // nchunk

    def rowsum(val, lo, hi):
        ri = lax.broadcasted_iota(jnp.int32, (rows, rows), 0)
        ci = lax.broadcasted_iota(jnp.int32, (rows, rows), 1)
        sel = (((ci - ri) >= -lo) & ((ci - ri) <= hi)).astype(BF16)
        v1 = val.astype(BF16)
        r1 = val - v1.astype(F32)
        v2 = r1.astype(BF16)
        v3 = (r1 - v2.astype(F32)).astype(BF16)
        return _dot(sel, v1) + _dot(sel, v2) + _dot(sel, v3)

    def one(v_ref, o_ref, r_scr, a_scr, win):
        lo = win // 2
        hi = win - 1 - lo
        rlo, rhi = (hi, lo) if transpose else (lo, hi)
        ridx = lax.broadcasted_iota(jnp.int32, (rows, 1), 0)
        cnt_r = (jnp.minimum(ridx + hi + 1, rows) - jnp.maximum(ridx - lo, 0)).astype(F32)
        cidx = lax.broadcasted_iota(jnp.int32, (1, n), 1) // LANE
        cnt_c = (jnp.minimum(cidx + hi + 1, GRID_W) - jnp.maximum(cidx - lo, 0)).astype(F32)
        if not transpose:
            for k in range(nchunk):
                sl = slice(k * cw, (k + 1) * cw)
                r_scr[:, sl] = rowsum(v_ref[0, :, sl], rlo, rhi) / cnt_r
        else:
            r_scr[...] = v_ref[0] / cnt_c
        a_scr[...] = r_scr[...]
        for j in range(-rlo, rhi + 1):
            if j == 0:
                continue
            c0, c1 = max(0, -j), min(GRID_W, GRID_W - j)
            a_scr[:, c0 * LANE:c1 * LANE] += r_scr[:, (c0 + j) * LANE:(c1 + j) * LANE]
        if not transpose:
            o_ref[0] = a_scr[...] / cnt_c - v_ref[0]
        else:
            for k in range(nchunk):
                sl = slice(k * cw, (k + 1) * cw)
                o_ref[0, :, sl] = rowsum(a_scr[:, sl] / cnt_r, rlo, rhi) - v_ref[0, :, sl]

    def body(v_ref, o_ref, r_scr, a_scr):
        grp = pl.program_id(0)
        for k, win in enumerate(POOL_WINDOWS):
            @pl.when(grp == k)
            def _(win=win):
                one(v_ref, o_ref, r_scr, a_scr, win)

    spec = pl.BlockSpec((1, rows, n), lambda k: (k, 0, 0))
    return _pcall(
        body, name=name, grid=(ngrp,), out_shape=jax.ShapeDtypeStruct(v.shape, F32),
        in_specs=[spec], out_specs=spec,
        scratch_shapes=[pltpu.VMEM((rows, n), F32), pltpu.VMEM((rows, n), F32)],
        compiler_params=_params(("arbitrary",)),
    )(v)


def _cmul(ar, ai, br, bi):
    return ar * br - ai * bi, ar * bi + ai * br


def _s5_prep(a_re, a_im, log_dt, c_re, c_im, kvec):
    q, nc, p = c_re.shape

    def body(ar_ref, ai_ref, ldt_ref, cr_ref, ci_ref, k_ref, cpr_ref, cpi_ref, tr_ref, ti_ref):
        lr, li = ar_ref[...], ai_ref[...]
        dt = jnp.exp(ldt_ref[...])
        kk = k_ref[...]
        mag = jnp.exp(kk * (lr * dt))
        ph = kk * (li * dt)
        tr_ref[...] = mag * jnp.cos(ph)
        ti_ref[...] = mag * jnp.sin(ph)
        m1 = jnp.exp(lr * dt)
        abr, abi = m1 * jnp.cos(li * dt), m1 * jnp.sin(li * dt)
        den = lr * lr + li * li
        xr, xi = abr - 1.0, abi
        cfr, cfi = (xr * lr + xi * li) / den, (xi * lr - xr * li) / den
        pr, pi_ = _cmul(cr_ref[...], ci_ref[...], cfr, cfi)
        cpr_ref[...] = pr
        cpi_ref[...] = pi_

    return _pcall(
        body, name="s5_prep",
        out_shape=(jax.ShapeDtypeStruct((q, nc, p), F32), jax.ShapeDtypeStruct((q, nc, p), F32),
                   jax.ShapeDtypeStruct((q, NTAB, p), F32), jax.ShapeDtypeStruct((q, NTAB, p), F32)),
        compiler_params=_params(None),
    )(a_re, a_im, log_dt, c_re, c_im, kvec)


def _s5_param_bwd(a_re, a_im, log_dt, c_re, c_im, dab_r, dab_i, dcp_r, dcp_i):
    q, nc, p = c_re.shape

    def body(ar_ref, ai_ref, ldt_ref, cr_ref, ci_ref, dar_ref, dai_ref, dcr_ref, dci_ref,
             gar_ref, gai_ref, gdt_ref, gcr_ref, gci_ref):
        lr, li = ar_ref[...], ai_ref[...]
        dt = jnp.exp(ldt_ref[...])
        m1 = jnp.exp(lr * dt)
        abr, abi = m1 * jnp.cos(li * dt), m1 * jnp.sin(li * dt)
        den = lr * lr + li * li
        xr, xi = abr - 1.0, abi
        cfr, cfi = (xr * lr + xi * li) / den, (xi * lr - xr * li) / den
        cr, ci = cr_ref[...], ci_ref[...]
        dcr, dci = dcr_ref[...], dci_ref[...]
        gcr, gci = _cmul(dcr, dci, cfr, -cfi)
        gcr_ref[...] = gcr
        gci_ref[...] = gci
        t_r, t_i = _cmul(cr, -ci, dcr, dci)
        dcf_r = jnp.sum(t_r, axis=1, keepdims=True)
        dcf_i = jnp.sum(t_i, axis=1, keepdims=True)
        ilr, ili = lr / den, -li / den
        u_r, u_i = _cmul(dcf_r, dcf_i, ilr, -ili)
        dab_r_, dab_i_ = dar_ref[...] + u_r, dai_ref[...] + u_i
        v_r, v_i = _cmul(dab_r_, dab_i_, abr, -abi)
        cl_r, cl_i = _cmul(cfr, cfi, ilr, ili)
        w_r, w_i = _cmul(dcf_r, dcf_i, cl_r, -cl_i)
        gar_ref[...] = v_r * dt - w_r
        gai_ref[...] = v_i * dt - w_i
        la_r, la_i = _cmul(lr, li, abr, abi)
        ddt = la_r * dab_r_ + la_i * dab_i_
        gdt_ref[...] = dt * jnp.sum(ddt, axis=2, keepdims=True)

    return _pcall(
        body, name="s5_param_bwd",
        out_shape=(jax.ShapeDtypeStruct((q, 1, p), F32), jax.ShapeDtypeStruct((q, 1, p), F32),
                   jax.ShapeDtypeStruct((q, 1, p), F32), jax.ShapeDtypeStruct((q, nc, p), F32),
                   jax.ShapeDtypeStruct((q, nc, p), F32)),
        compiler_params=_params(None),
    )(a_re, a_im, log_dt, c_re, c_im, dab_r, dab_i, dcp_r, dcp_i)


def _bcast8(row, c):
    return jnp.broadcast_to(row, (8, c))


def _s5_fwd(up, bre, bim, cmr, cmi, tre, tim, dskip, order, with_skip, name):
    t, cu = up.shape
    nsb = bre.shape[0]
    cb = cu // nsb
    cs = bre.shape[2]
    nch = t // TC

    def body(u_ref, bre_ref, bim_ref, cmr_ref, cmi_ref, tre_ref, tim_ref, dsk_ref,
             y_ref, hr_ref, hi_ref, cinr_ref, cini_ref, car_r, car_i, bur, bui):
        i = pl.program_id(1)

        @pl.when(i == 0)
        def _():
            car_r[...] = jnp.zeros_like(car_r)
            car_i[...] = jnp.zeros_like(car_i)

        u = u_ref[...]
        ub = u.astype(BF16)
        bur[...] = _dot(ub, bre_ref[0])
        bui[...] = _dot(ub, bim_ref[0])
        ar = _bcast8(tre_ref[0:1, :], cs)
        ai = _bcast8(tim_ref[0:1, :], cs)

        def p1(j, carry):
            hr, hi = carry
            off = pl.multiple_of(j * 8, 8)
            pr, pi_ = _cmul(ar, ai, hr, hi)
            nr = pr + bur[pl.ds(off, 8), :]
            ni = pi_ + bui[pl.ds(off, 8), :]
            hr_ref[pl.ds(off, 8), :] = nr
            hi_ref[pl.ds(off, 8), :] = ni
            return nr, ni

        zero = jnp.zeros((8, cs), F32)
        ir, ii = lax.fori_loop(0, SEG, p1, (zero, zero), unroll=4)
        cin_r, cin_i = car_r[...], car_i[...]
        cinr_ref[...] = cin_r
        cini_ref[...] = cin_i
        rows = lax.broadcasted_iota(jnp.int32, (8, cs), 0)
        for s, krow in ((1, SEG), (2, SEG + 1), (4, SEG + 3)):
            sr = jnp.where(rows >= s, pltpu.roll(ir, s, 0), 0.0)
            si = jnp.where(rows >= s, pltpu.roll(ii, s, 0), 0.0)
            pr, pi_ = _cmul(tre_ref[krow:krow + 1, :], tim_ref[krow:krow + 1, :], sr, si)
            ir, ii = ir + pr, ii + pi_
        pr, pi_ = _cmul(tre_ref[SEG:SEG + 8, :], tim_ref[SEG:SEG + 8, :], cin_r, cin_i)
        fr, fi = ir + pr, ii + pi_
        cs_r = jnp.where(rows >= 1, pltpu.roll(fr, 1, 0), cin_r)
        cs_i = jnp.where(rows >= 1, pltpu.roll(fi, 1, 0), cin_i)
        car_r[...] = _bcast8(fr[7:8, :], cs)
        car_i[...] = _bcast8(fi[7:8, :], cs)

        def p2(j, _):
            off = pl.multiple_of(j * 8, 8)
            pr, pi_ = _cmul(_bcast8(tre_ref[pl.ds(j, 1), :], cs), _bcast8(tim_ref[pl.ds(j, 1), :], cs), cs_r, cs_i)
            hr_ref[pl.ds(off, 8), :] = hr_ref[pl.ds(off, 8), :] + pr
            hi_ref[pl.ds(off, 8), :] = hi_ref[pl.ds(off, 8), :] + pi_
            return 0

        lax.fori_loop(0, SEG, p2, 0, unroll=4)
        y = _dot(hr_ref[...].astype(BF16), cmr_ref[0]) - _dot(hi_ref[...].astype(BF16), cmi_ref[0])
        if with_skip:
            y = y + dsk_ref[...] * u
        y_ref[...] = y

    blk = lambda sb, i: (order(i), sb)
    mat = lambda sb, i: (sb, 0, 0)
    return _pcall(
        body, name=name, grid=(nsb, nch),
        out_shape=(jax.ShapeDtypeStruct((t, cu), F32), jax.ShapeDtypeStruct((t, nsb * cs), F32),
                   jax.ShapeDtypeStruct((t, nsb * cs), F32), jax.ShapeDtypeStruct((nch * 8, nsb * cs), F32),
                   jax.ShapeDtypeStruct((nch * 8, nsb * cs), F32)),
        in_specs=[pl.BlockSpec((TC, cb), blk), pl.BlockSpec((1, cb, cs), mat), pl.BlockSpec((1, cb, cs), mat),
                  pl.BlockSpec((1, cs, cb), mat), pl.BlockSpec((1, cs, cb), mat),
                  pl.BlockSpec((NTAB, cs), lambda sb, i: (0, sb)), pl.BlockSpec((NTAB, cs), lambda sb, i: (0, sb)),
                  pl.BlockSpec((1, cb), lambda sb, i: (0, sb))],
        out_specs=(pl.BlockSpec((TC, cb), blk), pl.BlockSpec((TC, cs), blk), pl.BlockSpec((TC, cs), blk),
                   pl.BlockSpec((8, cs), blk), pl.BlockSpec((8, cs), blk)),
        scratch_shapes=[pltpu.VMEM((8, cs), F32), pltpu.VMEM((8, cs), F32), pltpu.VMEM((TC, cs), F32),
                        pltpu.VMEM((TC, cs), F32)],
        compiler_params=_params(("arbitrary", "arbitrary")),
    )(up, bre, bim, cmr, cmi, tre, tim, dskip)


def _s5_bwd(dyp, up, hr, hi, cinr, cini, bre, bim, ctr, cti, tre, tim, dskip, order, with_skip, name):
    t, cu = up.shape
    nsb = bre.shape[0]
    cb = cu // nsb
    cs = bre.shape[2]
    nch = t // TC

    def body(dy_ref, u_ref, hr_ref, hi_ref, cinr_ref, cini_ref, bre_ref, bim_ref, ctr_ref, cti_ref, tre_ref, tim_ref,
             dsk_ref, du_ref, dar_ref, dai_ref, dcr_ref, dci_ref, dbr_ref, dbi_ref, dd_ref,
             car_r, car_i, mr, mi, acc_r, acc_i):
        i = pl.program_id(1)

        @pl.when(i == 0)
        def _():
            for ref in (car_r, car_i, acc_r, acc_i, dcr_ref, dci_ref, dbr_ref, dbi_ref, dd_ref):
                ref[...] = jnp.zeros_like(ref)

        dy = dy_ref[...]
        dyb = dy.astype(BF16)
        u = u_ref[...]
        ub = u.astype(BF16)
        mr[...] = _dot(dyb, ctr_ref[0])
        mi[...] = -_dot(dyb, cti_ref[0])
        ar = _bcast8(tre_ref[0:1, :], cs)
        ai = -_bcast8(tim_ref[0:1, :], cs)

        def p1(jj, carry):
            m_r, m_i = carry
            off = pl.multiple_of((SEG - 1 - jj) * 8, 8)
            pr, pi_ = _cmul(ar, ai, m_r, m_i)
            nr = pr + mr[pl.ds(off, 8), :]
            ni = pi_ + mi[pl.ds(off, 8), :]
            mr[pl.ds(off, 8), :] = nr
            mi[pl.ds(off, 8), :] = ni
            return nr, ni

        zero = jnp.zeros((8, cs), F32)
        ir, ii = lax.fori_loop(0, SEG, p1, (zero, zero), unroll=4)
        cin_r, cin_i = car_r[...], car_i[...]
        rows = lax.broadcasted_iota(jnp.int32, (8, cs), 0)
        for s, krow in ((1, SEG), (2, SEG + 1), (4, SEG + 3)):
            sr = jnp.where(rows < 8 - s, pltpu.roll(ir, 8 - s, 0), 0.0)
            si = jnp.where(rows < 8 - s, pltpu.roll(ii, 8 - s, 0), 0.0)
            pr, pi_ = _cmul(tre_ref[krow:krow + 1, :], -tim_ref[krow:krow + 1, :], sr, si)
            ir, ii = ir + pr, ii + pi_
        pr, pi_ = _cmul(tre_ref[SEG + 8:SEG + 16, :], -tim_ref[SEG + 8:SEG + 16, :], cin_r, cin_i)
        fr, fi = ir + pr, ii + pi_
        cs_r = jnp.where(rows < 7, pltpu.roll(fr, 7, 0), cin_r)
        cs_i = jnp.where(rows < 7, pltpu.roll(fi, 7, 0), cin_i)
        car_r[...] = _bcast8(fr[0:1, :], cs)
        car_i[...] = _bcast8(fi[0:1, :], cs)

        def fix(j, hp_r, hp_i, acc):
            off = pl.multiple_of(j * 8, 8)
            k = SEG - 1 - j
            pr, pi_ = _cmul(_bcast8(tre_ref[pl.ds(k, 1), :], cs), -_bcast8(tim_ref[pl.ds(k, 1), :], cs), cs_r, cs_i)
            m_r = mr[pl.ds(off, 8), :] + pr
            m_i = mi[pl.ds(off, 8), :] + pi_
            mr[pl.ds(off, 8), :] = m_r
            mi[pl.ds(off, 8), :] = m_i
            a_r, a_i = acc
            return a_r + hp_r * m_r + hp_i * m_i, a_i + hp_r * m_i - hp_i * m_r

        last = (SEG - 1) * 8
        h0r = jnp.where(rows >= 1, pltpu.roll(hr_ref[last:last + 8, :], 1, 0), cinr_ref[...])
        h0i = jnp.where(rows >= 1, pltpu.roll(hi_ref[last:last + 8, :], 1, 0), cini_ref[...])
        acc = fix(0, h0r, h0i, (acc_r[...], acc_i[...]))

        def p2(j, acc):
            offp = pl.multiple_of((j - 1) * 8, 8)
            return fix(j, hr_ref[pl.ds(offp, 8), :], hi_ref[pl.ds(offp, 8), :], acc)

        a_r, a_i = lax.fori_loop(1, SEG, p2, acc, unroll=4)
        acc_r[...] = a_r
        acc_i[...] = a_i

        mrb, mib = mr[...].astype(BF16), mi[...].astype(BF16)
        du = _dot_nt(mrb, bre_ref[0]) + _dot_nt(mib, bim_ref[0])
        if with_skip:
            du = du + dsk_ref[...] * dy
            dd_ref[...] += _colsum(dy * u)
        du_ref[...] = du
        dbr_ref[0] += _dot_tn(ub, mrb)
        dbi_ref[0] += _dot_tn(ub, mib)
        dcr_ref[0] += _dot_tn(dyb, hr_ref[...].astype(BF16))
        dci_ref[0] -= _dot_tn(dyb, hi_ref[...].astype(BF16))

        @pl.when(i == nch - 1)
        def _():
            dar_ref[0] = _colsum(a_r)
            dai_ref[0] = _colsum(a_i)

    blk = lambda sb, i: (order(i), sb)
    mat = lambda sb, i: (sb, 0, 0)
    tab = pl.BlockSpec((NTAB, cs), lambda sb, i: (0, sb))
    vec = pl.BlockSpec((1, cb), lambda sb, i: (0, sb))
    return _pcall(
        body, name=name, grid=(nsb, nch),
        out_shape=(jax.ShapeDtypeStruct((t, cu), F32),
                   jax.ShapeDtypeStruct((nsb, 1, cs), F32), jax.ShapeDtypeStruct((nsb, 1, cs), F32),
                   jax.ShapeDtypeStruct((nsb, cb, cs), F32), jax.ShapeDtypeStruct((nsb, cb, cs), F32),
                   jax.ShapeDtypeStruct((nsb, cb, cs), F32), jax.ShapeDtypeStruct((nsb, cb, cs), F32),
                   jax.ShapeDtypeStruct((1, cu), F32)),
        in_specs=[pl.BlockSpec((TC, cb), blk), pl.BlockSpec((TC, cb), blk), pl.BlockSpec((TC, cs), blk),
                  pl.BlockSpec((TC, cs), blk), pl.BlockSpec((8, cs), blk), pl.BlockSpec((8, cs), blk),
                  pl.BlockSpec((1, cb, cs), mat), pl.BlockSpec((1, cb, cs), mat), pl.BlockSpec((1, cb, cs), mat),
                  pl.BlockSpec((1, cb, cs), mat), tab, tab, vec],
        out_specs=(pl.BlockSpec((TC, cb), blk), pl.BlockSpec((1, 1, cs), mat), pl.BlockSpec((1, 1, cs), mat),
                   pl.BlockSpec((1, cb, cs), mat), pl.BlockSpec((1, cb, cs), mat), pl.BlockSpec((1, cb, cs), mat),
                   pl.BlockSpec((1, cb, cs), mat), vec),
        scratch_shapes=[pltpu.VMEM((8, cs), F32), pltpu.VMEM((8, cs), F32), pltpu.VMEM((TC, cs), F32),
                        pltpu.VMEM((TC, cs), F32), pltpu.VMEM((8, cs), F32), pltpu.VMEM((8, cs), F32)],
        compiler_params=_params(("arbitrary", "arbitrary")),
    )(dyp, up, hr, hi, cinr, cini, bre, bim, ctr, cti, tre, tim, dskip)


def _gelu(x):
    k = math.sqrt(2.0 / math.pi)
    return 0.5 * x * (1.0 + jnp.tanh(k * (x + 0.044715 * (x * x * x))))


def _gelu_grad(x):
    k = math.sqrt(2.0 / math.pi)
    th = jnp.tanh(k * (x + 0.044715 * (x * x * x)))
    return 0.5 * (1.0 + th) + 0.5 * x * (1.0 - th * th) * (k * (1.0 + 3.0 * 0.044715 * (x * x)))


def _mix_out_fwd(yf, yb, dpool, x1, modtab, w_pool, pscale, w_glu, w_out):
    l, hm = dpool.shape
    d = x1.shape[1]

    def body(yf_ref, yb_ref, dp_ref, x_ref, mod_ref, wp_ref, ps_ref, wg_ref, wo_ref, x2_ref, yp_ref, cat_ref, mix_ref):
        ypre = yf_ref[...] + yb_ref[...]
        yp_ref[...] = ypre
        yg = _gelu(ypre)
        y2 = yg * _sigmoid(_dot(yg.astype(BF16), wg_ref[...]))
        po = _dot(dp_ref[...].astype(BF16), wp_ref[...]) * ps_ref[...]
        cat = jnp.concatenate([po, y2], axis=1).astype(BF16)
        cat_ref[...] = cat
        mix = _dot(cat, wo_ref[...])
        mix_ref[...] = mix.astype(BF16)
        x2_ref[...] = x_ref[...] + mod_ref[0, 5:6, :] * mix

    row = lambda i: (i, 0)
    lrow = lambda i: (i + 1, 0)
    return _pcall(
        body, name="mix_out_fwd", grid=(l // TM,),
        out_shape=(jax.ShapeDtypeStruct((l, d), F32), jax.ShapeDtypeStruct((l, hm), F32),
                   jax.ShapeDtypeStruct((l, d), BF16), jax.ShapeDtypeStruct((l, d), BF16)),
        in_specs=[pl.BlockSpec((TM, hm), lrow), pl.BlockSpec((TM, hm), lrow), pl.BlockSpec((TM, hm), row),
                  pl.BlockSpec((TM, d), lrow), _mod_spec(d, False), _const(w_pool.shape), _const((1, hm)),
                  _const(w_glu.shape), _resident(w_out.shape)],
        out_specs=(pl.BlockSpec((TM, d), row), pl.BlockSpec((TM, hm), row), pl.BlockSpec((TM, d), row),
                   pl.BlockSpec((TM, d), row)),
        compiler_params=_params(("arbitrary",)),
    )(yf, yb, dpool, x1, modtab, w_pool, pscale, w_glu, w_out)


def _mix_out_bwd(dx2, mix, cat, ypre, dpool, modtab, w_pool, pscale, w_glu, w_out):
    l, hm = dpool.shape
    d = dx2.shape[1]
    nt = l // TM

    def body(dx_ref, mix_ref, cat_ref, yp_ref, dp_ref, mod_ref, wp_ref, ps_ref, wg_ref, wo_ref,
             dyp_ref, ddp_ref, dwo_ref, dwg_ref, dwp_ref, dps_ref, dgate_ref, acc_o, acc_g):
        i = pl.program_id(0)

        @pl.when(i == 0)
        def _():
            for ref in (acc_o, acc_g, dwp_ref, dps_ref, dgate_ref):
                ref[...] = jnp.zeros_like(ref)

        dx = dx_ref[...]
        dgate_ref[...] += _colsum(dx * mix_ref[...].astype(F32))
        dmix = (dx * mod_ref[0, 5:6, :]).astype(BF16)
        dcat = _dot_nt(dmix, wo_ref[...])
        acc_o[...] += _dot_tn(cat_ref[...], dmix)
        dpo, dy2 = dcat[:, :hm], dcat[:, hm:]
        dpb = dp_ref[...].astype(BF16)
        pp = _dot(dpb, wp_ref[...])
        dps_ref[...] += _colsum(dpo * pp)
        dpp = (dpo * ps_ref[...]).astype(BF16)
        ddp_ref[...] = _dot_nt(dpp, wp_ref[...])
        dwp_ref[...] += _dot_tn(dpb, dpp)
        ypre = yp_ref[...]
        yg = _gelu(ypre)
        ygb = yg.astype(BF16)
        s = _sigmoid(_dot(ygb, wg_ref[...]))
        dq = (dy2 * yg * s * (1.0 - s)).astype(BF16)
        dyg = dy2 * s + _dot_nt(dq, wg_ref[...])
        acc_g[...] += _dot_tn(ygb, dq)
        dyp_ref[...] = dyg * _gelu_grad(ypre)

        @pl.when(i == nt - 1)
        def _():
            dwo_ref[...] = acc_o[...].astype(BF16)
            dwg_ref[...] = acc_g[...].astype(BF16)

    row = lambda i: (i, 0)
    return _pcall(
        body, name="mix_out_bwd", grid=(nt,),
        out_shape=(jax.ShapeDtypeStruct((l, hm), F32), jax.ShapeDtypeStruct((l, hm), F32),
                   jax.ShapeDtypeStruct((d, d), BF16), jax.ShapeDtypeStruct((hm, hm), BF16),
                   jax.ShapeDtypeStruct((hm, hm), F32), jax.ShapeDtypeStruct((1, hm), F32),
                   jax.ShapeDtypeStruct((1, d), F32)),
        in_specs=[pl.BlockSpec((TM, d), row), pl.BlockSpec((TM, d), row), pl.BlockSpec((TM, d), row),
                  pl.BlockSpec((TM, hm), row), pl.BlockSpec((TM, hm), row), _mod_spec(d, False),
                  _const(w_pool.shape), _const((1, hm)), _const(w_glu.shape), _resident(w_out.shape)],
        out_specs=(pl.BlockSpec((TM, hm), row), pl.BlockSpec((TM, hm), row), _const((d, d)), _const((hm, hm)),
                   _const((hm, hm)), _const((1, hm)), _const((1, d))),
        scratch_shapes=[pltpu.VMEM((d, d), F32), pltpu.VMEM((hm, hm), F32)],
        compiler_params=_params(("arbitrary",)),
    )(dx2, mix, cat, ypre, dpool, modtab, w_pool, pscale, w_glu, w_out)


def _loss_head(x3, target, g):
    l, d = x3.shape

    def body(x_ref, t_ref, g_ref, dx_ref, loss_ref, dg_ref):
        i = pl.program_id(0)

        @pl.when(i == 0)
        def _():
            loss_ref[...] = jnp.zeros_like(loss_ref)
            dg_ref[...] = jnp.zeros_like(dg_ref)

        xx = x_ref[...]
        gg = g_ref[...]
        rstd = lax.rsqrt(jnp.mean(xx * xx, axis=-1, keepdims=True) + EPS)
        xhat = xx * rstd
        err = xhat * gg - t_ref[...]
        row_loss = jnp.mean(err * err, axis=-1, keepdims=True)
        loss_ref[...] += 0.5 * jnp.sum(row_loss, axis=0, keepdims=True)
        dout = err * (1.0 / d)
        dg_ref[...] += _colsum(dout * xhat)
        dxhat = dout * gg
        dx_ref[...] = rstd * (dxhat - xhat * jnp.mean(dxhat * xhat, axis=-1, keepdims=True))

    row = lambda i: (i, 0)
    return _pcall(
        body, name="loss_head", grid=(l // TM,),
        out_shape=(jax.ShapeDtypeStruct((l, d), F32), jax.ShapeDtypeStruct((1, LANE), F32),
                   jax.ShapeDtypeStruct((1, d), F32)),
        in_specs=[pl.BlockSpec((TM, d), row), pl.BlockSpec((TM, d), row), _const((1, d))],
        out_specs=(pl.BlockSpec((TM, d), row), _const((1, LANE)), _const((1, d))),
        compiler_params=_params(("arbitrary",)),
    )(x3, target, g)


def _adamw(parts, w, m, v, name):
    shape = w.shape
    n = int(np.prod(shape))
    r, c = (n // LANE, LANE) if n % LANE == 0 else (1, n)
    npart = parts.shape[0]
    tr = r
    for cand in (1024, 512, 256, 128, 64, 32, 16, 8):
        if r % cand == 0:
            tr = cand
            break
    c1 = 1.0 - ADAM_B1
    c2 = 1.0 - ADAM_B2
    bc1 = 1.0 - ADAM_B1 ** ADAM_STEP
    bc2 = 1.0 - ADAM_B2 ** ADAM_STEP

    def body(p_ref, w_ref, m_ref, v_ref, g_ref, d_ref, mo_ref, vo_ref):
        g = p_ref[0].astype(F32)
        for k in range(1, npart):
            g = g + p_ref[k].astype(F32)
        mn = ADAM_B1 * m_ref[...] + c1 * g
        vn = ADAM_B2 * v_ref[...] + c2 * (g * g)
        m_hat = mn / bc1
        v_hat = vn / bc2
        g_ref[...] = g
        mo_ref[...] = mn
        vo_ref[...] = vn
        d_ref[...] = -ADAM_LR * (m_hat / (jnp.sqrt(v_hat) + ADAM_EPS) + ADAM_WD * w_ref[...])

    spec = pl.BlockSpec((tr, c), lambda i: (i, 0))
    outs = _pcall(
        body, name=name, grid=(r // tr,),
        out_shape=tuple(jax.ShapeDtypeStruct((r, c), F32) for _ in range(4)),
        in_specs=[pl.BlockSpec((npart, tr, c), lambda i: (0, i, 0)), spec, spec, spec],
        out_specs=(spec, spec, spec, spec),
        compiler_params=_params(("arbitrary",)),
    )(parts.reshape(npart, r, c), w.reshape(r, c), m.reshape(r, c), v.reshape(r, c))
    return tuple(o.reshape(shape) for o in outs)


def _perm(a, flip):
    t, c = a.shape
    a3 = a.reshape(t // TC, TC, c)
    if flip:
        a3 = a3[:, ::-1]
    return a3.reshape(t // TC, 8, SEG, c).transpose(0, 2, 1, 3).reshape(t, c)


def _unperm(a, flip):
    t, c = a.shape
    a3 = a.reshape(t // TC, SEG, 8, c).transpose(0, 2, 1, 3).reshape(t // TC, TC, c)
    if flip:
        a3 = a3[:, ::-1]
    return a3.reshape(t, c)


def _blockdiag(xb):
    n, a, b = xb.shape[-3:]
    eye = jnp.eye(n, dtype=xb.dtype)
    out = xb[..., :, :, None, :] * eye[:, None, :, None]
    return out.reshape(xb.shape[:-3] + (n * a, n * b))


def _diagblocks(mat, n):
    lead = mat.shape[:-2]
    a, b = mat.shape[-2] // n, mat.shape[-1] // n
    m5 = mat.reshape(lead + (n, a, n, b))
    nl = len(lead)
    dg = jnp.diagonal(m5, axis1=nl, axis2=nl + 2)
    return jnp.moveaxis(dg, -1, nl)


def _to_grid(a, rows):
    return a.reshape(rows, GRID_W, 4, LANE).transpose(2, 0, 1, 3).reshape(4, rows, GRID_W * LANE)


def _from_grid(a, rows):
    return a.reshape(4, rows, GRID_W, LANE).transpose(1, 2, 0, 3).reshape(rows * GRID_W, 4 * LANE)


def _pad128(a):
    flat = a.reshape(-1)
    pad = (-flat.shape[0]) % LANE
    return jnp.pad(flat, (0, pad)) if pad else flat


def kernel(x, c, ctx, c_ctx, norm_g, w_ada, b_ada, ffn_w1, ffn_w3, ffn_w2, w_in, pool_w, pool_scale, s5_a_re, s5_a_im, s5_log_dt, s5_b_re, s5_b_im, s5_c_re, s5_c_im, s5_d, s5_w_glu, w_out, final_g, loss_target, m_c_ctx, m_norm_g, m_w_ada, m_b_ada, m_ffn_w1, m_ffn_w3, m_ffn_w2, m_w_in, m_pool_w, m_pool_scale, m_s5_a_re, m_s5_a_im, m_s5_log_dt, m_s5_b_re, m_s5_b_im, m_s5_c_re, m_s5_c_im, m_s5_d, m_s5_w_glu, m_w_out, m_final_g, v_c_ctx, v_norm_g, v_w_ada, v_b_ada, v_ffn_w1, v_ffn_w3, v_ffn_w2, v_w_in, v_pool_w, v_pool_scale, v_s5_a_re, v_s5_a_im, v_s5_log_dt, v_s5_b_re, v_s5_b_im, v_s5_c_re, v_s5_c_im, v_s5_d, v_s5_w_glu, v_w_out, v_final_g):
    weights = dict(c_ctx=c_ctx, norm_g=norm_g, w_ada=w_ada, b_ada=b_ada, ffn_w1=ffn_w1, ffn_w3=ffn_w3, ffn_w2=ffn_w2,
                   w_in=w_in, pool_w=pool_w, pool_scale=pool_scale, s5_a_re=s5_a_re, s5_a_im=s5_a_im,
                   s5_log_dt=s5_log_dt, s5_b_re=s5_b_re, s5_b_im=s5_b_im, s5_c_re=s5_c_re, s5_c_im=s5_c_im, s5_d=s5_d,
                   s5_w_glu=s5_w_glu, w_out=w_out, final_g=final_g)
    mom_m = dict(c_ctx=m_c_ctx, norm_g=m_norm_g, w_ada=m_w_ada, b_ada=m_b_ada, ffn_w1=m_ffn_w1, ffn_w3=m_ffn_w3,
                 ffn_w2=m_ffn_w2, w_in=m_w_in, pool_w=m_pool_w, pool_scale=m_pool_scale, s5_a_re=m_s5_a_re,
                 s5_a_im=m_s5_a_im, s5_log_dt=m_s5_log_dt, s5_b_re=m_s5_b_re, s5_b_im=m_s5_b_im, s5_c_re=m_s5_c_re,
                 s5_c_im=m_s5_c_im, s5_d=m_s5_d, s5_w_glu=m_s5_w_glu, w_out=m_w_out, final_g=m_final_g)
    mom_v = dict(c_ctx=v_c_ctx, norm_g=v_norm_g, w_ada=v_w_ada, b_ada=v_b_ada, ffn_w1=v_ffn_w1, ffn_w3=v_ffn_w3,
                 ffn_w2=v_ffn_w2, w_in=v_w_in, pool_w=v_pool_w, pool_scale=v_pool_scale, s5_a_re=v_s5_a_re,
                 s5_a_im=v_s5_a_im, s5_log_dt=v_s5_log_dt, s5_b_re=v_s5_b_re, s5_b_im=v_s5_b_im, s5_c_re=v_s5_c_re,
                 s5_c_im=v_s5_c_im, s5_d=v_s5_d, s5_w_glu=v_s5_w_glu, w_out=v_w_out, final_g=v_final_g)

    l, d = x.shape[1], x.shape[2]
    lc = ctx.shape[1]
    t = l + lc
    rows = l // GRID_W
    fblk = ffn_w1.shape[-1]
    ngrp, gp = s5_a_re.shape[2], s5_a_re.shape[3]
    gc = s5_b_re.shape[3]
    hm = ngrp * gc
    nsb = 4
    gsb = ngrp // nsb
    assert lc == TM == TC and l % TM == 0 and hm == 4 * LANE and ngrp * gp == nsb * 512
    me = 4 * lax.axis_index("x") + 2 * lax.axis_index("y") + lax.axis_index("c")

    padc = ((0, 0), (0, 0), (0, FPAD - fblk))
    w13_loc = [jnp.concatenate([jnp.pad(ffn_w1[0, k], padc[1:]), jnp.pad(ffn_w3[0, k], padc[1:])], axis=-1).astype(BF16)
               for k in range(2)]
    w2_loc = [jnp.pad(ffn_w2[0, k], ((0, FPAD - fblk), (0, 0))).astype(BF16) for k in range(2)]
    gath, _ = _exchange(
        [w13_loc[0], w13_loc[1], w2_loc[0], w2_loc[1], w_in[0].astype(BF16), w_out[0].astype(BF16),
         s5_w_glu[0].astype(BF16), c, norm_g[0]], [], "exchange_weights")
    w13 = gath[0:2]
    w2 = [g_.reshape(NDEV // 2, 2 * FPAD, d) for g_ in gath[2:4]]
    w_in_f = gath[4].reshape(d, -1)
    w_out_f = gath[5].reshape(-1, d)
    w_glu_f = gath[6].reshape(hm, hm)
    c_all = gath[7].reshape(NDEV, d)
    g_all = gath[8].transpose(1, 0, 2).reshape(3, d)

    wa = w_ada.shape[-1]
    c16 = jnp.concatenate([c_all, c_ctx.reshape(1, d), jnp.zeros((7, d), F32)], axis=0)
    b_blk = lax.dynamic_slice_in_dim(b_ada, me * wa, wa, axis=1)
    mod_sl = _ada_fwd(c16, w_ada[0], b_blk)
    gath, _ = _exchange([mod_sl], [], "exchange_mod")
    mod_full = gath[0].transpose(1, 0, 2).reshape(16, NDEV * wa)
    mod_l = lax.dynamic_index_in_dim(mod_full, me, axis=0, keepdims=False).reshape(9, d)
    modtab = jnp.stack([mod_full[8].reshape(9, d), mod_l])

    x0 = jnp.concatenate([ctx[0], x[0]], axis=0)
    g0, g1, g2 = g_all[0:1], g_all[1:2], g_all[2:3]
    x1, h0, ab0, o0 = _ffn_fwd(x0, modtab, g0, w13[0], w2[0], 0, True, "ffn0_fwd")
    h1, u = _mix_in_fwd(x1, modtab, g1, w_in_f)

    v_grid = _to_grid(u[lc:, :hm], rows)
    dpool = _from_grid(_pool(v_grid, False, "pool_fwd"), rows)
    w_pool_bd = _blockdiag(pool_w[0]).astype(BF16)

    q = 2 * ngrp
    kvec = np.concatenate([np.arange(1, SEG + 1), SEG * np.arange(1, 9), SEG * np.arange(8, 0, -1)]).astype(np.float32)
    a_re3, a_im3 = s5_a_re[0].reshape(q, 1, gp), s5_a_im[0].reshape(q, 1, gp)
    ldt3 = jnp.broadcast_to(s5_log_dt[0].reshape(q, 1, 1), (q, 1, gp))
    c_re3, c_im3 = s5_c_re[0].reshape(q, gc, gp), s5_c_im[0].reshape(q, gc, gp)
    cp_re, cp_im, tab_re, tab_im = _s5_prep(a_re3, a_im3, ldt3, c_re3, c_im3, jnp.broadcast_to(jnp.asarray(kvec).reshape(1, NTAB, 1), (1, NTAB, gp)))
    tabs = [(tab_re.reshape(2, ngrp, NTAB, gp)[k].transpose(1, 0, 2).reshape(NTAB, ngrp * gp),
             tab_im.reshape(2, ngrp, NTAB, gp)[k].transpose(1, 0, 2).reshape(NTAB, ngrp * gp)) for k in range(2)]
    b_t = lambda b: _blockdiag(b[0].reshape(nsb, gsb, gp, gc).transpose(0, 1, 3, 2)).astype(BF16)
    bre, bim = b_t(s5_b_re), b_t(s5_b_im)
    cp_t = lambda cp: _blockdiag(cp.reshape(2, nsb, gsb, gc, gp)).astype(BF16)
    ctr, cti = cp_t(cp_re), cp_t(cp_im)
    cmr, cmi = jnp.swapaxes(ctr, -1, -2), jnp.swapaxes(cti, -1, -2)
    dskip = s5_d

    nch = t // TC
    order_f = lambda i: i
    order_b = lambda i: jnp.where(i == 0, 0, nch - i)
    rorder_f = lambda i: nch - 1 - i
    rorder_b = lambda i: jnp.where(i == nch - 1, 0, i + 1)
    u_s5 = u[:, hm:]
    up_f, up_b = _perm(u_s5, False), _perm(u_s5, True)
    yp_f, hr_f, hi_f, cinr_f, cini_f = _s5_fwd(up_f, bre, bim, cmr[0], cmi[0], tabs[0][0], tabs[0][1], dskip,
                                               order_f, True, "s5_fwd_f")
    yp_b, hr_b, hi_b, cinr_b, cini_b = _s5_fwd(up_b, bre, bim, cmr[1], cmi[1], tabs[1][0], tabs[1][1], dskip,
                                               order_b, False, "s5_fwd_b")
    y_f, y_b = _unperm(yp_f, False), _unperm(yp_b, True)

    x2, ypre, cat, mix = _mix_out_fwd(y_f, y_b, dpool, x1, modtab, w_pool_bd, pool_scale, w_glu_f, w_out_f)
    x3, h2, ab2, o2 = _ffn_fwd(x2, modtab, g2, w13[1], w2[1], 6, False, "ffn1_fwd")
    dx3, loss_part, dfinal_g = _loss_head(x3, loss_target[0], final_g.reshape(1, d))
    loss = lax.psum(loss_part[0, 0], AXES)

    dhp2, d13_1, d2_1 = _ffn_bwd(dx3, modtab, h2, ab2, w13[1], w2[1], 6, False, "ffn1_bwd")
    dx2, dmod_678, dg2 = _ffn_bwd_norm(dx3, dhp2, x2, o2, modtab, g2, 6, False, "ffn1_bwd_norm")
    dypre, ddpool, dw_out, dw_glu, dw_pool_bd, dpscale, dgate5 = _mix_out_bwd(
        dx2, mix, cat, ypre, dpool, modtab, w_pool_bd, pool_scale, w_glu_f, w_out_f)

    dy_t = jnp.concatenate([jnp.zeros((lc, hm), F32), dypre], axis=0)
    dyp_f, dyp_b = _perm(dy_t, False), _perm(dy_t, True)
    dup_f, dar_f, dai_f, dcr_f, dci_f, dbr_f, dbi_f, dd_skip = _s5_bwd(
        dyp_f, up_f, hr_f, hi_f, cinr_f, cini_f, bre, bim, ctr[0], cti[0], tabs[0][0], tabs[0][1], dskip,
        rorder_f, True, "s5_bwd_f")
    dup_b, dar_b, dai_b, dcr_b, dci_b, dbr_b, dbi_b, _ = _s5_bwd(
        dyp_b, up_b, hr_b, hi_b, cinr_b, cini_b, bre, bim, ctr[1], cti[1], tabs[1][0], tabs[1][1], dskip,
        rorder_b, False, "s5_bwd_b")
    du_f, du_b = _unperm(dup_f, False), _unperm(dup_b, True)

    dab_r = jnp.stack([dar_f, dar_b]).reshape(q, 1, gp)
    dab_i = jnp.stack([dai_f, dai_b]).reshape(q, 1, gp)
    dcp_r = _diagblocks(jnp.stack([dcr_f, dcr_b]), gsb).reshape(q, gc, gp)
    dcp_i = _diagblocks(jnp.stack([dci_f, dci_b]), gsb).reshape(q, gc, gp)
    ga_re, ga_im, gldt, gc_re, gc_im = _s5_param_bwd(a_re3, a_im3, ldt3, c_re3, c_im3, dab_r, dab_i, dcp_r, dcp_i)
    gb_re = (_diagblocks(dbr_f, gsb) + _diagblocks(dbr_b, gsb)).transpose(0, 1, 3, 2).reshape(ngrp, gp, gc)
    gb_im = (_diagblocks(dbi_f, gsb) + _diagblocks(dbi_b, gsb)).transpose(0, 1, 3, 2).reshape(ngrp, gp, gc)

    du_pool = _from_grid(_pool(_to_grid(ddpool, rows), True, "pool_bwd"), rows)
    dx1, dw_in, dmod_34, dg1 = _mix_in_bwd(du_pool, du_f, du_b, dx2, h1, x1, modtab, g1, w_in_f)
    dhp0, d13_0, d2_0 = _ffn_bwd(dx1, modtab, h0, ab0, w13[0], w2[0], 0, True, "ffn0_bwd")
    dx0, dmod_012, dg0 = _ffn_bwd_norm(dx1, dhp0, x0, o0, modtab, g0, 0, True, "ffn0_bwd_norm")
    grad_x = dx0[lc:].reshape(1, l, d)

    dmod_c = jnp.concatenate([dmod_012[0], dmod_34[0], jnp.zeros((4, d), F32)], axis=0)
    dmod_l = jnp.concatenate([dmod_012[1], dmod_34[1], dgate5, dmod_678[0]], axis=0)

    gw_pool = _diagblocks(dw_pool_bd, 4)
    small = dict(norm_g=jnp.concatenate([dg0, dg1, dg2], axis=0), pool_w=gw_pool, pool_scale=dpscale,
                 s5_a_re=ga_re, s5_a_im=ga_im, s5_log_dt=gldt[:, 0, 0], s5_b_re=gb_re, s5_b_im=gb_im, s5_c_re=gc_re,
                 s5_c_im=gc_im, s5_d=dd_skip, final_g=dfinal_g, dmod_l=dmod_l, dmod_c=dmod_c)
    offs, pieces, off = {}, [], 0
    for k_, a_ in small.items():
        p_ = _pad128(a_.astype(F32))
        offs[k_] = (off, int(np.prod(a_.shape)))
        off += p_.shape[0]
        pieces.append(p_)
    bundle = jnp.concatenate(pieces).reshape(1, off)
    gath, recv = _exchange(
        [bundle],
        [d13_0, d13_1, d2_0.reshape(NDEV, FPAD, d), d2_1.reshape(NDEV, FPAD, d), dw_in.reshape(NDEV, d // NDEV, -1),
         dw_out.reshape(NDEV, -1, d), dw_glu.reshape(NDEV, hm // NDEV, hm)], "exchange_grads")
    bund = gath[0].reshape(NDEV, off)
    r13, r2, r_in, r_out, r_glu = recv[0:2], recv[2:4], recv[4], recv[5], recv[6]

    def piece(name):
        o_, n_ = offs[name]
        return bund[:, o_:o_ + n_]

    dl_all = lax.dynamic_slice_in_dim(piece("dmod_l"), me * wa, wa, axis=1)
    dc_all = lax.dynamic_slice_in_dim(piece("dmod_c"), me * wa, wa, axis=1)
    g_wada, gc_part = _ada_bwd(c16, w_ada[0], dl_all, dc_all)
    gath, _ = _exchange([gc_part], [], "exchange_cctx")

    parts = {
        "c_ctx": gath[0].reshape(NDEV, d),
        "norm_g": lax.dynamic_slice_in_dim(piece("norm_g").reshape(NDEV, 3, d), me * (d // NDEV), d // NDEV,
                                           axis=2).reshape((NDEV,) + norm_g.shape),
        "w_ada": g_wada.reshape((1,) + w_ada.shape),
        "b_ada": jnp.concatenate([piece("dmod_l"), piece("dmod_c")], axis=0).reshape((2 * NDEV,) + b_ada.shape),
        "ffn_w1": jnp.stack([r13[0][:, :, :fblk], r13[1][:, :, :fblk]], axis=1).reshape((NDEV,) + ffn_w1.shape),
        "ffn_w3": jnp.stack([r13[0][:, :, FPAD:FPAD + fblk], r13[1][:, :, FPAD:FPAD + fblk]],
                            axis=1).reshape((NDEV,) + ffn_w3.shape),
        "ffn_w2": jnp.stack([r2[0][:, :fblk, :], r2[1][:, :fblk, :]], axis=1).reshape((NDEV,) + ffn_w2.shape),
        "w_in": r_in.reshape((NDEV,) + w_in.shape),
        "s5_w_glu": r_glu.reshape((NDEV,) + s5_w_glu.shape),
        "w_out": r_out.reshape((NDEV,) + w_out.shape),
    }
    for k_ in ("pool_w", "pool_scale", "s5_a_re", "s5_a_im", "s5_log_dt", "s5_b_re", "s5_b_im", "s5_c_re", "s5_c_im",
               "s5_d", "final_g"):
        parts[k_] = piece(k_).reshape((NDEV,) + weights[k_].shape)

    grads, deltas, new_m, new_v = [], [], [], []
    for k_ in weights:
        g_, d_, m_, v_ = _adamw(parts[k_], weights[k_], mom_m[k_], mom_v[k_], "adamw_" + k_)
        grads.append(g_)
        deltas.append(d_)
        new_m.append(m_)
        new_v.append(v_)
    return (loss, grad_x, *grads, *deltas, *new_m, *new_v)
```

```python
import functools
import math

import numpy as np
import jax
import jax.numpy as jnp
from jax import lax
from jax.experimental import pallas as pl
from jax.experimental.pallas import tpu as pltpu

F32 = jnp.float32
BF16 = jnp.bfloat16
AXES = ("x", "y", "c")
NDEV = 8
EPS = 1e-6
TM = 256
TC = 256
SEG = TC // 8
NTAB = SEG + 16
GRID_W = 64
POOL_WINDOWS = (2, 4, 8, 16)
LANE = 128
FPAD = 384
VMEM_LIMIT = 56 * 1024 * 1024
ADAM_BLOCK_BYTES = 512 * 1024
ADAM_LR, ADAM_B1, ADAM_B2, ADAM_EPS, ADAM_WD, ADAM_STEP = 0.001, 0.9, 0.999, 1e-08, 0.01, 10


def _raw_call(body, kw):
    return pl.pallas_call(body, **kw)


def _pcall(body, xchg=None, edges=None, **kw):
    extra = ()
    if xchg is not None:
        gathers, a2as = xchg
        ng, n = len(gathers), len(gathers) + len(a2as)
        n_in, n_out, n_scr = len(kw["in_specs"]), len(kw["out_shape"]), len(kw.get("scratch_shapes", ()))
        inner = body

        def hosted(*refs):
            a, b = n_in, n_in + n
            c_, d_ = b + n_out, b + n_out + n
            e = d_ + n_scr
            first, last = edges()

            @pl.when(first)
            def _():
                _xchg_start(refs[a:b], refs[c_:d_], refs[e:], ng)

            inner(*refs[:a], *refs[b:c_], *refs[d_:e])

            @pl.when(last)
            def _():
                _xchg_wait(refs[a:b], refs[c_:d_], refs[e:], ng)

        body = hosted
        any_spec = pl.BlockSpec(memory_space=pl.ANY)
        kw = dict(kw, in_specs=list(kw["in_specs"]) + [any_spec] * n,
                  out_shape=tuple(kw["out_shape"]) + _xchg_shapes(gathers, a2as),
                  out_specs=tuple(kw["out_specs"]) + (any_spec,) * n,
                  scratch_shapes=list(kw.get("scratch_shapes", ())) + _xchg_sems(n))
        extra = tuple(gathers) + tuple(a2as)
    call = _raw_call(body, kw)
    return (lambda *args: call(*args, *extra)) if extra else call


def _params(sem):
    return pltpu.CompilerParams(dimension_semantics=sem, vmem_limit_bytes=VMEM_LIMIT)


def _dot(a, b):
    return jnp.dot(a, b, preferred_element_type=F32)


def _dot_nt(a, b):
    return lax.dot_general(a, b, (((1,), (1,)), ((), ())), preferred_element_type=F32)


def _dot_tn(a, b):
    return lax.dot_general(a, b, (((0,), (0,)), ((), ())), preferred_element_type=F32)


def _sigmoid(x):
    return 1.0 / (1.0 + jnp.exp(-x))


def _colsum(a):
    return jnp.sum(a, axis=0, keepdims=True)


def _resident(shape):
    nd = len(shape)
    return pl.BlockSpec(shape, lambda *_: (0,) * nd, pipeline_mode=pl.Buffered(1))


def _const(shape):
    nd = len(shape)
    return pl.BlockSpec(shape, lambda *_: (0,) * nd)


def _exchange(gathers, a2as, name):
    ng = len(gathers)
    n = ng + len(a2as)

    def body(*refs):
        _xchg_start(refs[:n], refs[n:2 * n], refs[2 * n:], ng)
        _xchg_wait(refs[:n], refs[n:2 * n], refs[2 * n:], ng)

    any_spec = pl.BlockSpec(memory_space=pl.ANY)
    outs = _pcall(
        body, name=name, out_shape=_xchg_shapes(gathers, a2as), in_specs=[any_spec] * n, out_specs=[any_spec] * n,
        scratch_shapes=_xchg_sems(n),
    )(*gathers, *a2as)
    return list(outs[:ng]), list(outs[ng:])


def _edges1(n0):
    return lambda: (pl.program_id(0) == 0, pl.program_id(0) == n0 - 1)


def _edges2(n0, n1):
    return lambda: ((pl.program_id(0) == 0) & (pl.program_id(1) == 0),
                    (pl.program_id(0) == n0 - 1) & (pl.program_id(1) == n1 - 1))


def _xchg_shapes(gathers, a2as):
    return (tuple(jax.ShapeDtypeStruct((NDEV,) + tuple(a.shape), a.dtype) for a in gathers)
            + tuple(jax.ShapeDtypeStruct(tuple(a.shape), a.dtype) for a in a2as))


def _xchg_sems(n):
    return [pltpu.SemaphoreType.DMA((n * (NDEV - 1),)), pltpu.SemaphoreType.DMA((n * (NDEV - 1),)),
            pltpu.SemaphoreType.DMA((n,))]


def _xchg_copies(in_refs, out_refs, sems, ng):
    send_sems, recv_sems, loc_sems = sems
    x, y, c = (lax.axis_index(a) for a in AXES)
    me = 4 * x + 2 * y + c
    copies = []
    for k in range(len(in_refs)):
        for p in range(1, NDEV):
            px = 1 - x if p & 4 else x
            py = 1 - y if p & 2 else y
            pc = 1 - c if p & 1 else c
            pid = 4 * px + 2 * py + pc
            src = in_refs[k] if k < ng else in_refs[k].at[pid]
            sem = k * (NDEV - 1) + p - 1
            send = pltpu.make_async_remote_copy(
                src_ref=src, dst_ref=out_refs[k].at[me], send_sem=send_sems.at[sem], recv_sem=recv_sems.at[sem],
                device_id=(px, py, pc), device_id_type=pl.DeviceIdType.MESH)
            recv = pltpu.make_async_remote_copy(
                src_ref=src, dst_ref=out_refs[k].at[pid], send_sem=send_sems.at[sem], recv_sem=recv_sems.at[sem],
                device_id=(px, py, pc), device_id_type=pl.DeviceIdType.MESH)
            copies.append((send, recv))
        own = in_refs[k] if k < ng else in_refs[k].at[me]
        copies.append((pltpu.make_async_copy(own, out_refs[k].at[me], loc_sems.at[k]), None))
    return copies


def _xchg_start(in_refs, out_refs, sems, ng):
    for send, _ in _xchg_copies(in_refs, out_refs, sems, ng):
        send.start()


def _xchg_wait(in_refs, out_refs, sems, ng):
    for send, recv in _xchg_copies(in_refs, out_refs, sems, ng):
        if recv is None:
            send.wait()
        else:
            recv.wait_recv()
            send.wait_send()


def _ada_fwd(c16, w_blk, b_blk):
    def body(c_ref, w_ref, b_ref, o_ref):
        cc = c_ref[...]
        s = (cc * _sigmoid(cc)).astype(BF16)
        o_ref[...] = _dot(s, w_ref[...].astype(BF16)) + b_ref[...]

    return _pcall(body, name="ada_fwd", out_shape=jax.ShapeDtypeStruct((16, w_blk.shape[1]), F32),
                  compiler_params=_params(None))(c16, w_blk, b_blk)


def _ada_bwd(c16, w_blk, dl, dc):
    d, w = w_blk.shape

    def body(c_ref, w_ref, dl_ref, dc_ref, gw_ref, gc_ref):
        cc = c_ref[...]
        sg = _sigmoid(cc)
        s = (cc * sg).astype(BF16)
        dctot = _colsum(dc_ref[...])
        dm = jnp.concatenate([dl_ref[...], jnp.broadcast_to(dctot, (8, w))], axis=0)
        rows = lax.broadcasted_iota(jnp.int32, (16, w), 0)
        dm = jnp.where(rows <= 8, dm, 0.0).astype(BF16)
        gw_ref[...] = _dot_tn(s, dm)
        t = _dot_nt(jnp.broadcast_to(dctot, (8, w)).astype(BF16), w_ref[...].astype(BF16))[0:1]
        c8, s8 = cc[8:9], sg[8:9]
        gc_ref[...] = t * (s8 * (1.0 + c8 * (1.0 - s8)))

    return _pcall(body, name="ada_bwd",
                  out_shape=(jax.ShapeDtypeStruct((d, w), F32), jax.ShapeDtypeStruct((1, d), F32)),
                  compiler_params=_params(None))(c16, w_blk, dl, dc)


def _norm_fwd(x, g, shift, scale):
    rstd = lax.rsqrt(jnp.mean(x * x, axis=-1, keepdims=True) + EPS)
    xhat = x * rstd
    xn = xhat * g
    return rstd, xhat, xn, xn * (1.0 + scale) + shift


def _norm_bwd(dh, rstd, xhat, xn, g, scale):
    dxn = dh * (1.0 + scale)
    dxhat = dxn * g
    dx = rstd * (dxhat - xhat * jnp.mean(dxhat * xhat, axis=-1, keepdims=True))
    return dx, _colsum(dh), _colsum(dh * xn), _colsum(dxn * xhat)


def _mod_spec(d, has_ctx):
    if has_ctx:
        return pl.BlockSpec((1, 9, d), lambda *ids: (jnp.minimum(ids[-1], 1), 0, 0))
    return pl.BlockSpec((1, 9, d), lambda *ids: (1, 0, 0))


def _dmod_spec(nrow, d, has_ctx):
    if has_ctx:
        return pl.BlockSpec((1, nrow, d), lambda i: (jnp.minimum(i, 1), 0, 0))
    return pl.BlockSpec((1, nrow, d), lambda i: (0, 0, 0))


def _ffn_fwd(x, modtab, g, w13, w2, k0, has_ctx, name, xchg=None):
    t, d = x.shape
    nb, _, w = w13.shape
    fp = w // 2

    def body(x_ref, mod_ref, g_ref, w13_ref, w2_ref, xo_ref, h_ref, ab_ref, o_ref):
        xx = x_ref[...]
        shift, scale, gate = mod_ref[0, k0:k0 + 1, :], mod_ref[0, k0 + 1:k0 + 2, :], mod_ref[0, k0 + 2:k0 + 3, :]
        _, _, _, h = _norm_fwd(xx, g_ref[...], shift, scale)
        hb = h.astype(BF16)
        h_ref[...] = hb
        acc = jnp.zeros((TM, d), F32)
        for p in range(nb // 2):
            zs = []
            for q in range(2):
                blk = 2 * p + q
                ab = _dot(hb, w13_ref[blk])
                ab_ref[:, blk * w:(blk + 1) * w] = ab.astype(BF16)
                a, b = ab[:, :fp], ab[:, fp:]
                zs.append((a * _sigmoid(a) * b).astype(BF16))
            acc = acc + _dot(jnp.concatenate(zs, axis=1), w2_ref[p])
        o_ref[...] = acc.astype(BF16)
        xo_ref[...] = xx + (0.5 * gate) * acc

    row = lambda i: (i, 0)
    return _pcall(
        body, name=name, grid=(t // TM,), xchg=xchg, edges=_edges1(t // TM),
        out_shape=(jax.ShapeDtypeStruct((t, d), F32), jax.ShapeDtypeStruct((t, d), BF16),
                   jax.ShapeDtypeStruct((t, nb * w), BF16), jax.ShapeDtypeStruct((t, d), BF16)),
        in_specs=[pl.BlockSpec((TM, d), row), _mod_spec(d, has_ctx), _const((1, d)),
                  _resident(w13.shape), _resident(w2.shape)],
        out_specs=(pl.BlockSpec((TM, d), row), pl.BlockSpec((TM, d), row), pl.BlockSpec((TM, nb * w), row),
                   pl.BlockSpec((TM, d), row)),
        compiler_params=_params(("arbitrary",)),
    )(x, modtab, g, w13, w2)


def _ffn_bwd(dy, modtab, h, ab, w13, w2, k0, has_ctx, name, xchg=None):
    t, d = dy.shape
    nb, _, w = w13.shape
    fp = w // 2
    npair = nb // 2
    nt = t // TM

    def body(dy_ref, mod_ref, h_ref, ab_ref, w13_ref, w2_ref, dh_ref, d13_ref, d2_ref, acc13, acc2):
        i = pl.program_id(1)

        @pl.when(i == 0)
        def _():
            acc13[...] = jnp.zeros_like(acc13)
            acc2[...] = jnp.zeros_like(acc2)

        gate = mod_ref[0, k0 + 2:k0 + 3, :]
        do = (dy_ref[...] * (0.5 * gate)).astype(BF16)
        dz = _dot_nt(do, w2_ref[0])
        hb = h_ref[...]
        dh = jnp.zeros((TM, d), F32)
        zs = []
        for q in range(2):
            ab = ab_ref[:, q * w:(q + 1) * w].astype(F32)
            a, b = ab[:, :fp], ab[:, fp:]
            sg = _sigmoid(a)
            sa = a * sg
            dzq = dz[:, q * fp:(q + 1) * fp]
            da = dzq * b * (sg * (1.0 + a * (1.0 - sg)))
            db = dzq * sa
            dab = jnp.concatenate([da, db], axis=1).astype(BF16)
            dh = dh + _dot_nt(dab, w13_ref[q])
            acc13[q] += _dot_tn(hb, dab)
            zs.append((sa * b).astype(BF16))
        acc2[...] += _dot_tn(jnp.concatenate(zs, axis=1), do)
        dh_ref[0] = dh

        @pl.when(i == nt - 1)
        def _():
            d13_ref[...] = acc13[...].astype(BF16)
            d2_ref[0] = acc2[...].astype(BF16)

    mod_spec = _mod_spec(d, has_ctx)
    return _pcall(
        body, name=name, grid=(npair, nt), xchg=xchg, edges=_edges2(npair, nt),
        out_shape=(jax.ShapeDtypeStruct((npair, t, d), F32), jax.ShapeDtypeStruct((nb, d, w), BF16),
                   jax.ShapeDtypeStruct((npair, 2 * fp, d), BF16)),
        in_specs=[pl.BlockSpec((TM, d), lambda p, i: (i, 0)), mod_spec, pl.BlockSpec((TM, d), lambda p, i: (i, 0)),
                  pl.BlockSpec((TM, 2 * w), lambda p, i: (i, p)), pl.BlockSpec((2, d, w), lambda p, i: (p, 0, 0)),
                  pl.BlockSpec((1, 2 * fp, d), lambda p, i: (p, 0, 0))],
        out_specs=(pl.BlockSpec((1, TM, d), lambda p, i: (p, i, 0)), pl.BlockSpec((2, d, w), lambda p, i: (p, 0, 0)),
                   pl.BlockSpec((1, 2 * fp, d), lambda p, i: (p, 0, 0))),
        scratch_shapes=[pltpu.VMEM((2, d, w), F32), pltpu.VMEM((2 * fp, d), F32)],
        compiler_params=_params(("arbitrary", "arbitrary")),
    )(dy, modtab, h, ab, w13, w2)


def _ffn_bwd_norm(dy, dhp, x, o, modtab, g, k0, has_ctx, name, xchg=None):
    t, d = dy.shape
    npair = dhp.shape[0]
    ngrp = 2 if has_ctx else 1

    def body(dy_ref, dhp_ref, x_ref, o_ref, mod_ref, g_ref, dx_ref, dmod_ref, dg_ref):
        i = pl.program_id(0)

        @pl.when(i == 0)
        def _():
            dg_ref[...] = jnp.zeros_like(dg_ref)

        @pl.when((i == 0) | (i == ngrp - 1))
        def _():
            dmod_ref[...] = jnp.zeros_like(dmod_ref)

        dh = dhp_ref[0]
        for p in range(1, npair):
            dh = dh + dhp_ref[p]
        scale = mod_ref[0, k0 + 1:k0 + 2, :]
        gg = g_ref[...]
        rstd, xhat, xn, _ = _norm_fwd(x_ref[...], gg, 0.0, scale)
        dxn, dshift, dscale, dg = _norm_bwd(dh, rstd, xhat, xn, gg, scale)
        dyv = dy_ref[...]
        dx_ref[...] = dyv + dxn
        dmod_ref[0, 0:1, :] += dshift
        dmod_ref[0, 1:2, :] += dscale
        dmod_ref[0, 2:3, :] += _colsum(0.5 * dyv * o_ref[...].astype(F32))
        dg_ref[...] += dg

    row = lambda i: (i, 0)
    return _pcall(
        body, name=name, grid=(t // TM,), xchg=xchg, edges=_edges1(t // TM),
        out_shape=(jax.ShapeDtypeStruct((t, d), F32), jax.ShapeDtypeStruct((ngrp, 3, d), F32),
                   jax.ShapeDtypeStruct((1, d), F32)),
        in_specs=[pl.BlockSpec((TM, d), row), pl.BlockSpec((npair, TM, d), lambda i: (0, i, 0)),
                  pl.BlockSpec((TM, d), row), pl.BlockSpec((TM, d), row), _mod_spec(d, has_ctx), _const((1, d))],
        out_specs=(pl.BlockSpec((TM, d), row), _dmod_spec(3, d, has_ctx), _const((1, d))),
        compiler_params=_params(("arbitrary",)),
    )(dy, dhp, x, o, modtab, g)


def _mix_in_fwd(x1, modtab, g, w_in):
    t, d = x1.shape

    def body(x_ref, mod_ref, g_ref, w_ref, h_ref, u_ref):
        _, _, _, h = _norm_fwd(x_ref[...], g_ref[...], mod_ref[0, 3:4, :], mod_ref[0, 4:5, :])
        hb = h.astype(BF16)
        h_ref[...] = hb
        u_ref[...] = _dot(hb, w_ref[...])

    row = lambda i: (i, 0)
    return _pcall(
        body, name="mix_in_fwd", grid=(t // TM,),
        out_shape=(jax.ShapeDtypeStruct((t, d), BF16), jax.ShapeDtypeStruct((t, w_in.shape[1]), F32)),
        in_specs=[pl.BlockSpec((TM, d), row), _mod_spec(d, True), _const((1, d)), _resident(w_in.shape)],
        out_specs=(pl.BlockSpec((TM, d), row), pl.BlockSpec((TM, w_in.shape[1]), row)),
        compiler_params=_params(("arbitrary",)),
    )(x1, modtab, g, w_in)


def _mix_in_bwd(du_pool, du_f, du_b, dx2, h1, x1, modtab, g, w_in):
    t, d = x1.shape
    m = w_in.shape[1]
    hm = m // 2
    nt = t // TM

    def body(dup_ref, duf_ref, dub_ref, dx2_ref, h_ref, x_ref, mod_ref, g_ref, w_ref, dx_ref, dw_ref, dmod_ref, dg_ref, acc):
        i = pl.program_id(0)

        @pl.when(i == 0)
        def _():
            acc[...] = jnp.zeros_like(acc)
            dg_ref[...] = jnp.zeros_like(dg_ref)

        @pl.when(i <= 1)
        def _():
            dmod_ref[...] = jnp.zeros_like(dmod_ref)

        lat = (i > 0).astype(F32)
        du = jnp.concatenate([dup_ref[...] * lat, duf_ref[...] + dub_ref[...]], axis=1).astype(BF16)
        dh = _dot_nt(du, w_ref[...])
        acc[...] += _dot_tn(h_ref[...], du)
        scale = mod_ref[0, 4:5, :]
        gg = g_ref[...]
        rstd, xhat, xn, _ = _norm_fwd(x_ref[...], gg, 0.0, scale)
        dxn, dshift, dscale, dg = _norm_bwd(dh, rstd, xhat, xn, gg, scale)
        dx_ref[...] = dx2_ref[...] * lat + dxn
        dmod_ref[0, 0:1, :] += dshift
        dmod_ref[0, 1:2, :] += dscale
        dg_ref[...] += dg

        @pl.when(i == nt - 1)
        def _():
            dw_ref[...] = acc[...].astype(BF16)

    row = lambda i: (i, 0)
    lrow = lambda i: (jnp.maximum(i - 1, 0), 0)
    return _pcall(
        body, name="mix_in_bwd", grid=(nt,),
        out_shape=(jax.ShapeDtypeStruct((t, d), F32), jax.ShapeDtypeStruct((d, m), BF16),
                   jax.ShapeDtypeStruct((2, 2, d), F32), jax.ShapeDtypeStruct((1, d), F32)),
        in_specs=[pl.BlockSpec((TM, hm), lrow), pl.BlockSpec((TM, hm), row), pl.BlockSpec((TM, hm), row),
                  pl.BlockSpec((TM, d), lrow), pl.BlockSpec((TM, d), row), pl.BlockSpec((TM, d), row),
                  _mod_spec(d, True), _const((1, d)), _resident(w_in.shape)],
        out_specs=(pl.BlockSpec((TM, d), row), _const((d, m)), _dmod_spec(2, d, True), _const((1, d))),
        scratch_shapes=[pltpu.VMEM((d, m), F32)],
        compiler_params=_params(("arbitrary",)),
    )(du_pool, du_f, du_b, dx2, h1, x1, modtab, g, w_in)


def _pool(v, transpose, name):
    ngrp, rows, n = v.shape
    nchunk = 4
    cw = n // nchunk

    def rowsum(val, lo, hi):
        ri = lax.broadcasted_iota(jnp.int32, (rows, rows), 0)
        ci = lax.broadcasted_iota(jnp.int32, (rows, rows), 1)
        sel = (((ci - ri) >= -lo) & ((ci - ri) <= hi)).astype(BF16)
        v1 = val.astype(BF16)
        r1 = val - v1.astype(F32)
        v2 = r1.astype(BF16)
        v3 = (r1 - v2.astype(F32)).astype(BF16)
        return _dot(sel, v1) + _dot(sel, v2) + _dot(sel, v3)

    def one(v_ref, o_ref, r_scr, a_scr, win):
        lo = win // 2
        hi = win - 1 - lo
        rlo, rhi = (hi, lo) if transpose else (lo, hi)
        ridx = lax.broadcasted_iota(jnp.int32, (rows, 1), 0)
        cnt_r = (jnp.minimum(ridx + hi + 1, rows) - jnp.maximum(ridx - lo, 0)).astype(F32)
        cidx = lax.broadcasted_iota(jnp.int32, (1, n), 1) // LANE
        cnt_c = (jnp.minimum(cidx + hi + 1, GRID_W) - jnp.maximum(cidx - lo, 0)).astype(F32)
        if not transpose:
            for k in range(nchunk):
                sl = slice(k * cw, (k + 1) * cw)
                r_scr[:, sl] = rowsum(v_ref[0, :, sl], rlo, rhi) / cnt_r
        else:
            r_scr[...] = v_ref[0] / cnt_c
        a_scr[...] = r_scr[...]
        for j in range(-rlo, rhi + 1):
            if j == 0:
                continue
            c0, c1 = max(0, -j), min(GRID_W, GRID_W - j)
            a_scr[:, c0 * LANE:c1 * LANE] += r_scr[:, (c0 + j) * LANE:(c1 + j) * LANE]
        if not transpose:
            o_ref[0] = a_scr[...] / cnt_c - v_ref[0]
        else:
            for k in range(nchunk):
                sl = slice(k * cw, (k + 1) * cw)
                o_ref[0, :, sl] = rowsum(a_scr[:, sl] / cnt_r, rlo, rhi) - v_ref[0, :, sl]

    def body(v_ref, o_ref, r_scr, a_scr):
        grp = pl.program_id(0)
        for k, win in enumerate(POOL_WINDOWS):
            @pl.when(grp == k)
            def _(win=win):
                one(v_ref, o_ref, r_scr, a_scr, win)

    spec = pl.BlockSpec((1, rows, n), lambda k: (k, 0, 0))
    return _pcall(
        body, name=name, grid=(ngrp,), out_shape=jax.ShapeDtypeStruct(v.shape, F32),
        in_specs=[spec], out_specs=spec,
        scratch_shapes=[pltpu.VMEM((rows, n), F32), pltpu.VMEM((rows, n), F32)],
        compiler_params=_params(("arbitrary",)),
    )(v)


def _cmul(ar, ai, br, bi):
    return ar * br - ai * bi, ar * bi + ai * br


def _s5_prep(a_re, a_im, log_dt, c_re, c_im, kvec):
    q, nc, p = c_re.shape

    def body(ar_ref, ai_ref, ldt_ref, cr_ref, ci_ref, k_ref, cpr_ref, cpi_ref, tr_ref, ti_ref):
        lr, li = ar_ref[...], ai_ref[...]
        dt = jnp.exp(ldt_ref[...])
        kk = k_ref[...]
        mag = jnp.exp(kk * (lr * dt))
        ph = kk * (li * dt)
        tr_ref[...] = mag * jnp.cos(ph)
        ti_ref[...] = mag * jnp.sin(ph)
        m1 = jnp.exp(lr * dt)
        abr, abi = m1 * jnp.cos(li * dt), m1 * jnp.sin(li * dt)
        den = lr * lr + li * li
        xr, xi = abr - 1.0, abi
        cfr, cfi = (xr * lr + xi * li) / den, (xi * lr - xr * li) / den
        pr, pi_ = _cmul(cr_ref[...], ci_ref[...], cfr, cfi)
        cpr_ref[...] = pr
        cpi_ref[...] = pi_

    return _pcall(
        body, name="s5_prep",
        out_shape=(jax.ShapeDtypeStruct((q, nc, p), F32), jax.ShapeDtypeStruct((q, nc, p), F32),
                   jax.ShapeDtypeStruct((q, NTAB, p), F32), jax.ShapeDtypeStruct((q, NTAB, p), F32)),
        compiler_params=_params(None),
    )(a_re, a_im, log_dt, c_re, c_im, kvec)


def _s5_param_bwd(a_re, a_im, log_dt, c_re, c_im, dab_r, dab_i, dcp_r, dcp_i):
    q, nc, p = c_re.shape

    def body(ar_ref, ai_ref, ldt_ref, cr_ref, ci_ref, dar_ref, dai_ref, dcr_ref, dci_ref,
             gar_ref, gai_ref, gdt_ref, gcr_ref, gci_ref):
        lr, li = ar_ref[...], ai_ref[...]
        dt = jnp.exp(ldt_ref[...])
        m1 = jnp.exp(lr * dt)
        abr, abi = m1 * jnp.cos(li * dt), m1 * jnp.sin(li * dt)
        den = lr * lr + li * li
        xr, xi = abr - 1.0, abi
        cfr, cfi = (xr * lr + xi * li) / den, (xi * lr - xr * li) / den
        cr, ci = cr_ref[...], ci_ref[...]
        dcr, dci = dcr_ref[...], dci_ref[...]
        gcr, gci = _cmul(dcr, dci, cfr, -cfi)
        gcr_ref[...] = gcr
        gci_ref[...] = gci
        t_r, t_i = _cmul(cr, -ci, dcr, dci)
        dcf_r = jnp.sum(t_r, axis=1, keepdims=True)
        dcf_i = jnp.sum(t_i, axis=1, keepdims=True)
        ilr, ili = lr / den, -li / den
        u_r, u_i = _cmul(dcf_r, dcf_i, ilr, -ili)
        dab_r_, dab_i_ = dar_ref[...] + u_r, dai_ref[...] + u_i
        v_r, v_i = _cmul(dab_r_, dab_i_, abr, -abi)
        cl_r, cl_i = _cmul(cfr, cfi, ilr, ili)
        w_r, w_i = _cmul(dcf_r, dcf_i, cl_r, -cl_i)
        gar_ref[...] = v_r * dt - w_r
        gai_ref[...] = v_i * dt - w_i
        la_r, la_i = _cmul(lr, li, abr, abi)
        ddt = la_r * dab_r_ + la_i * dab_i_
        gdt_ref[...] = dt * jnp.sum(ddt, axis=2, keepdims=True)

    return _pcall(
        body, name="s5_param_bwd",
        out_shape=(jax.ShapeDtypeStruct((q, 1, p), F32), jax.ShapeDtypeStruct((q, 1, p), F32),
                   jax.ShapeDtypeStruct((q, 1, p), F32), jax.ShapeDtypeStruct((q, nc, p), F32),
                   jax.ShapeDtypeStruct((q, nc, p), F32)),
        compiler_params=_params(None),
    )(a_re, a_im, log_dt, c_re, c_im, dab_r, dab_i, dcp_r, dcp_i)


def _bcast8(row, c):
    return jnp.broadcast_to(row, (8, c))


def _slab(j):
    return j * 8 if isinstance(j, int) else pl.multiple_of(j * 8, 8)


def _scan_head(xr, xi, tre_ref, tim_ref, car_r, car_i, desc, conj, cs):
    sg = -1.0 if conj else 1.0
    ar = _bcast8(tre_ref[0:1, :], cs)
    ai = sg * _bcast8(tim_ref[0:1, :], cs)

    def p1(jj, carry):
        hr, hi = carry
        off = _slab(SEG - 1 - jj if desc else jj)
        pr, pi_ = _cmul(ar, ai, hr, hi)
        nr = pr + xr[pl.ds(off, 8), :]
        ni = pi_ + xi[pl.ds(off, 8), :]
        xr[pl.ds(off, 8), :] = nr
        xi[pl.ds(off, 8), :] = ni
        return nr, ni

    zero = jnp.zeros((8, cs), F32)
    ir, ii = lax.fori_loop(0, SEG, p1, (zero, zero), unroll=4)
    cin_r, cin_i = car_r[...], car_i[...]
    rows = lax.broadcasted_iota(jnp.int32, (8, cs), 0)
    for s, krow in ((1, SEG), (2, SEG + 1), (4, SEG + 3)):
        keep, sh = (rows < 8 - s, 8 - s) if desc else (rows >= s, s)
        sr = jnp.where(keep, pltpu.roll(ir, sh, 0), 0.0)
        si = jnp.where(keep, pltpu.roll(ii, sh, 0), 0.0)
        pr, pi_ = _cmul(tre_ref[krow:krow + 1, :], sg * tim_ref[krow:krow + 1, :], sr, si)
        ir, ii = ir + pr, ii + pi_
    q0 = SEG + 8 if desc else SEG
    pr, pi_ = _cmul(tre_ref[q0:q0 + 8, :], sg * tim_ref[q0:q0 + 8, :], cin_r, cin_i)
    fr, fi = ir + pr, ii + pi_
    keep, sh, edge = (rows < 7, 7, 0) if desc else (rows >= 1, 1, 7)
    cs_r = jnp.where(keep, pltpu.roll(fr, sh, 0), cin_r)
    cs_i = jnp.where(keep, pltpu.roll(fi, sh, 0), cin_i)
    car_r[...] = _bcast8(fr[edge:edge + 1, :], cs)
    car_i[...] = _bcast8(fi[edge:edge + 1, :], cs)
    return cs_r, cs_i, cin_r, cin_i


def _pow_row(tre_ref, tim_ref, j, desc, conj, cs):
    k = SEG - 1 - j if desc else j
    sg = -1.0 if conj else 1.0
    return _bcast8(tre_ref[pl.ds(k, 1), :], cs), sg * _bcast8(tim_ref[pl.ds(k, 1), :], cs)


def _s5_fwd(up, bre, bim, cmr, cmi, tre, tim, dskip, order, desc, with_skip, name):
    t, cu = up.shape
    nsb = bre.shape[0]
    cb = cu // nsb
    cs = bre.shape[2]
    nch = t // TC

    def body(u_ref, bre_ref, bim_ref, cmr_ref, cmi_ref, tre_ref, tim_ref, dsk_ref,
             y_ref, hr_ref, hi_ref, cinr_ref, cini_ref, car_r, car_i):
        i = pl.program_id(1)

        @pl.when(i == 0)
        def _():
            car_r[...] = jnp.zeros_like(car_r)
            car_i[...] = jnp.zeros_like(car_i)

        u = u_ref[...]
        ub = u.astype(BF16)
        hr_ref[...] = _dot(ub, bre_ref[0])
        hi_ref[...] = _dot(ub, bim_ref[0])
        cs_r, cs_i, cin_r, cin_i = _scan_head(hr_ref, hi_ref, tre_ref, tim_ref, car_r, car_i, desc, False, cs)
        cinr_ref[...] = cin_r
        cini_ref[...] = cin_i

        def p2(j, _):
            off = _slab(j)
            pw_r, pw_i = _pow_row(tre_ref, tim_ref, j, desc, False, cs)
            pr, pi_ = _cmul(pw_r, pw_i, cs_r, cs_i)
            hr_ref[pl.ds(off, 8), :] = hr_ref[pl.ds(off, 8), :] + pr
            hi_ref[pl.ds(off, 8), :] = hi_ref[pl.ds(off, 8), :] + pi_
            return 0

        lax.fori_loop(0, SEG, p2, 0, unroll=4)
        y = _dot(hr_ref[...].astype(BF16), cmr_ref[0]) - _dot(hi_ref[...].astype(BF16), cmi_ref[0])
        if with_skip:
            y = y + dsk_ref[...] * u
        y_ref[...] = y

    blk = lambda sb, i: (order(i), sb)
    mat = lambda sb, i: (sb, 0, 0)
    return _pcall(
        body, name=name, grid=(nsb, nch),
        out_shape=(jax.ShapeDtypeStruct((t, cu), F32), jax.ShapeDtypeStruct((t, nsb * cs), F32),
                   jax.ShapeDtypeStruct((t, nsb * cs), F32), jax.ShapeDtypeStruct((nch * 8, nsb * cs), F32),
                   jax.ShapeDtypeStruct((nch * 8, nsb * cs), F32)),
        in_specs=[pl.BlockSpec((TC, cb), blk), pl.BlockSpec((1, cb, cs), mat), pl.BlockSpec((1, cb, cs), mat),
                  pl.BlockSpec((1, cs, cb), mat), pl.BlockSpec((1, cs, cb), mat),
                  pl.BlockSpec((NTAB, cs), lambda sb, i: (0, sb)), pl.BlockSpec((NTAB, cs), lambda sb, i: (0, sb)),
                  pl.BlockSpec((1, cb), lambda sb, i: (0, sb))],
        out_specs=(pl.BlockSpec((TC, cb), blk), pl.BlockSpec((TC, cs), blk), pl.BlockSpec((TC, cs), blk),
                   pl.BlockSpec((8, cs), blk), pl.BlockSpec((8, cs), blk)),
        scratch_shapes=[pltpu.VMEM((8, cs), F32), pltpu.VMEM((8, cs), F32)],
        compiler_params=_params(("arbitrary", "arbitrary")),
    )(up, bre, bim, cmr, cmi, tre, tim, dskip)


def _s5_bwd(dyp, up, hr, hi, cinr, cini, bre, bim, ctr, cti, tre, tim, dskip, order, desc, with_skip, name, xchg=None):
    t, cu = up.shape
    nsb = bre.shape[0]
    cb = cu // nsb
    cs = bre.shape[2]
    nch = t // TC
    adesc = not desc

    def body(dy_ref, u_ref, hr_ref, hi_ref, cinr_ref, cini_ref, bre_ref, bim_ref, ctr_ref, cti_ref, tre_ref, tim_ref,
             dsk_ref, du_ref, dar_ref, dai_ref, dcr_ref, dci_ref, dbr_ref, dbi_ref, dd_ref,
             car_r, car_i, mr, mi, acc_r, acc_i):
        i = pl.program_id(1)

        @pl.when(i == 0)
        def _():
            for ref in (car_r, car_i, acc_r, acc_i, dcr_ref, dci_ref, dbr_ref, dbi_ref, dd_ref):
                ref[...] = jnp.zeros_like(ref)

        dy = dy_ref[...]
        dyb = dy.astype(BF16)
        u = u_ref[...]
        ub = u.astype(BF16)
        mr[...] = _dot(dyb, ctr_ref[0])
        mi[...] = -_dot(dyb, cti_ref[0])
        cs_r, cs_i, _, _ = _scan_head(mr, mi, tre_ref, tim_ref, car_r, car_i, adesc, True, cs)
        rows = lax.broadcasted_iota(jnp.int32, (8, cs), 0)

        def fix(j, hp_r, hp_i, acc):
            off = _slab(j)
            pw_r, pw_i = _pow_row(tre_ref, tim_ref, j, adesc, True, cs)
            pr, pi_ = _cmul(pw_r, pw_i, cs_r, cs_i)
            m_r = mr[pl.ds(off, 8), :] + pr
            m_i = mi[pl.ds(off, 8), :] + pi_
            mr[pl.ds(off, 8), :] = m_r
            mi[pl.ds(off, 8), :] = m_i
            a_r, a_i = acc
            return a_r + hp_r * m_r + hp_i * m_i, a_i + hp_r * m_i - hp_i * m_r

        edge_j, src, keep, sh = (SEG - 1, 0, rows < 7, 7) if desc else (0, (SEG - 1) * 8, rows >= 1, 1)
        h0r = jnp.where(keep, pltpu.roll(hr_ref[src:src + 8, :], sh, 0), cinr_ref[...])
        h0i = jnp.where(keep, pltpu.roll(hi_ref[src:src + 8, :], sh, 0), cini_ref[...])
        acc = fix(edge_j, h0r, h0i, (acc_r[...], acc_i[...]))

        def p2(jj, acc):
            j = jj if desc else jj + 1
            offp = _slab(j + 1 if desc else j - 1)
            return fix(j, hr_ref[pl.ds(offp, 8), :], hi_ref[pl.ds(offp, 8), :], acc)

        a_r, a_i = lax.fori_loop(0, SEG - 1, p2, acc, unroll=4)
        acc_r[...] = a_r
        acc_i[...] = a_i

        mrb, mib = mr[...].astype(BF16), mi[...].astype(BF16)
        du = _dot_nt(mrb, bre_ref[0]) + _dot_nt(mib, bim_ref[0])
        if with_skip:
            du = du + dsk_ref[...] * dy
            dd_ref[...] += _colsum(dy * u)
        du_ref[...] = du
        dbr_ref[0] += _dot_tn(ub, mrb)
        dbi_ref[0] += _dot_tn(ub, mib)
        dcr_ref[0] += _dot_tn(dyb, hr_ref[...].astype(BF16))
        dci_ref[0] -= _dot_tn(dyb, hi_ref[...].astype(BF16))

        @pl.when(i == nch - 1)
        def _():
            dar_ref[0] = _colsum(a_r)
            dai_ref[0] = _colsum(a_i)

    blk = lambda sb, i: (order(i), sb)
    mat = lambda sb, i: (sb, 0, 0)
    tab = pl.BlockSpec((NTAB, cs), lambda sb, i: (0, sb))
    vec = pl.BlockSpec((1, cb), lambda sb, i: (0, sb))
    return _pcall(
        body, name=name, grid=(nsb, nch), xchg=xchg, edges=_edges2(nsb, nch),
        out_shape=(jax.ShapeDtypeStruct((t, cu), F32),
                   jax.ShapeDtypeStruct((nsb, 1, cs), F32), jax.ShapeDtypeStruct((nsb, 1, cs), F32),
                   jax.ShapeDtypeStruct((nsb, cb, cs), F32), jax.ShapeDtypeStruct((nsb, cb, cs), F32),
                   jax.ShapeDtypeStruct((nsb, cb, cs), F32), jax.ShapeDtypeStruct((nsb, cb, cs), F32),
                   jax.ShapeDtypeStruct((1, cu), F32)),
        in_specs=[pl.BlockSpec((TC, cb), blk), pl.BlockSpec((TC, cb), blk), pl.BlockSpec((TC, cs), blk),
                  pl.BlockSpec((TC, cs), blk), pl.BlockSpec((8, cs), blk), pl.BlockSpec((8, cs), blk),
                  pl.BlockSpec((1, cb, cs), mat), pl.BlockSpec((1, cb, cs), mat), pl.BlockSpec((1, cb, cs), mat),
                  pl.BlockSpec((1, cb, cs), mat), tab, tab, vec],
        out_specs=(pl.BlockSpec((TC, cb), blk), pl.BlockSpec((1, 1, cs), mat), pl.BlockSpec((1, 1, cs), mat),
                   pl.BlockSpec((1, cb, cs), mat), pl.BlockSpec((1, cb, cs), mat), pl.BlockSpec((1, cb, cs), mat),
                   pl.BlockSpec((1, cb, cs), mat), vec),
        scratch_shapes=[pltpu.VMEM((8, cs), F32), pltpu.VMEM((8, cs), F32), pltpu.VMEM((TC, cs), F32),
                        pltpu.VMEM((TC, cs), F32), pltpu.VMEM((8, cs), F32), pltpu.VMEM((8, cs), F32)],
        compiler_params=_params(("arbitrary", "arbitrary")),
    )(dyp, up, hr, hi, cinr, cini, bre, bim, ctr, cti, tre, tim, dskip)


def _gelu(x):
    k = math.sqrt(2.0 / math.pi)
    return 0.5 * x * (1.0 + jnp.tanh(k * (x + 0.044715 * (x * x * x))))


def _gelu_grad(x):
    k = math.sqrt(2.0 / math.pi)
    th = jnp.tanh(k * (x + 0.044715 * (x * x * x)))
    return 0.5 * (1.0 + th) + 0.5 * x * (1.0 - th * th) * (k * (1.0 + 3.0 * 0.044715 * (x * x)))


def _mix_out_fwd(yf, yb, dpool, x1, modtab, w_pool, pscale, w_glu, w_out):
    l, hm = dpool.shape
    d = x1.shape[1]

    def body(yf_ref, yb_ref, dp_ref, x_ref, mod_ref, wp_ref, ps_ref, wg_ref, wo_ref, x2_ref, yp_ref, cat_ref, mix_ref):
        ypre = yf_ref[...] + yb_ref[...]
        yp_ref[...] = ypre
        yg = _gelu(ypre)
        y2 = yg * _sigmoid(_dot(yg.astype(BF16), wg_ref[...]))
        po = _dot(dp_ref[...].astype(BF16), wp_ref[...]) * ps_ref[...]
        cat = jnp.concatenate([po, y2], axis=1).astype(BF16)
        cat_ref[...] = cat
        mix = _dot(cat, wo_ref[...])
        mix_ref[...] = mix.astype(BF16)
        x2_ref[...] = x_ref[...] + mod_ref[0, 5:6, :] * mix

    row = lambda i: (i, 0)
    lrow = lambda i: (i + 1, 0)
    return _pcall(
        body, name="mix_out_fwd", grid=(l // TM,),
        out_shape=(jax.ShapeDtypeStruct((l, d), F32), jax.ShapeDtypeStruct((l, hm), F32),
                   jax.ShapeDtypeStruct((l, d), BF16), jax.ShapeDtypeStruct((l, d), BF16)),
        in_specs=[pl.BlockSpec((TM, hm), lrow), pl.BlockSpec((TM, hm), lrow), pl.BlockSpec((TM, hm), row),
                  pl.BlockSpec((TM, d), lrow), _mod_spec(d, False), _const(w_pool.shape), _const((1, hm)),
                  _const(w_glu.shape), _resident(w_out.shape)],
        out_specs=(pl.BlockSpec((TM, d), row), pl.BlockSpec((TM, hm), row), pl.BlockSpec((TM, d), row),
                   pl.BlockSpec((TM, d), row)),
        compiler_params=_params(("arbitrary",)),
    )(yf, yb, dpool, x1, modtab, w_pool, pscale, w_glu, w_out)


def _mix_out_bwd(dx2, mix, cat, ypre, dpool, modtab, w_pool, pscale, w_glu, w_out):
    l, hm = dpool.shape
    d = dx2.shape[1]
    nt = l // TM

    def body(dx_ref, mix_ref, cat_ref, yp_ref, dp_ref, mod_ref, wp_ref, ps_ref, wg_ref, wo_ref,
             dyp_ref, ddp_ref, dwo_ref, dwg_ref, dwp_ref, dps_ref, dgate_ref, acc_o, acc_g):
        i = pl.program_id(0)

        @pl.when(i == 0)
        def _():
            for ref in (acc_o, acc_g, dwp_ref, dps_ref, dgate_ref):
                ref[...] = jnp.zeros_like(ref)

        dx = dx_ref[...]
        dgate_ref[...] += _colsum(dx * mix_ref[...].astype(F32))
        dmix = (dx * mod_ref[0, 5:6, :]).astype(BF16)
        dcat = _dot_nt(dmix, wo_ref[...])
        acc_o[...] += _dot_tn(cat_ref[...], dmix)
        dpo, dy2 = dcat[:, :hm], dcat[:, hm:]
        dpb = dp_ref[...].astype(BF16)
        pp = _dot(dpb, wp_ref[...])
        dps_ref[...] += _colsum(dpo * pp)
        dpp = (dpo * ps_ref[...]).astype(BF16)
        ddp_ref[...] = _dot_nt(dpp, wp_ref[...])
        dwp_ref[...] += _dot_tn(dpb, dpp)
        ypre = yp_ref[...]
        yg = _gelu(ypre)
        ygb = yg.astype(BF16)
        s = _sigmoid(_dot(ygb, wg_ref[...]))
        dq = (dy2 * yg * s * (1.0 - s)).astype(BF16)
        dyg = dy2 * s + _dot_nt(dq, wg_ref[...])
        acc_g[...] += _dot_tn(ygb, dq)
        dyp_ref[...] = dyg * _gelu_grad(ypre)

        @pl.when(i == nt - 1)
        def _():
            dwo_ref[...] = acc_o[...].astype(BF16)
            dwg_ref[...] = acc_g[...].astype(BF16)

    row = lambda i: (i, 0)
    return _pcall(
        body, name="mix_out_bwd", grid=(nt,),
        out_shape=(jax.ShapeDtypeStruct((l, hm), F32), jax.ShapeDtypeStruct((l, hm), F32),
                   jax.ShapeDtypeStruct((d, d), BF16), jax.ShapeDtypeStruct((hm, hm), BF16),
                   jax.ShapeDtypeStruct((hm, hm), F32), jax.ShapeDtypeStruct((1, hm), F32),
                   jax.ShapeDtypeStruct((1, d), F32)),
        in_specs=[pl.BlockSpec((TM, d), row), pl.BlockSpec((TM, d), row), pl.BlockSpec((TM, d), row),
                  pl.BlockSpec((TM, hm), row), pl.BlockSpec((TM, hm), row), _mod_spec(d, False),
                  _const(w_pool.shape), _const((1, hm)), _const(w_glu.shape), _resident(w_out.shape)],
        out_specs=(pl.BlockSpec((TM, hm), row), pl.BlockSpec((TM, hm), row), _const((d, d)), _const((hm, hm)),
                   _const((hm, hm)), _const((1, hm)), _const((1, d))),
        scratch_shapes=[pltpu.VMEM((d, d), F32), pltpu.VMEM((hm, hm), F32)],
        compiler_params=_params(("arbitrary",)),
    )(dx2, mix, cat, ypre, dpool, modtab, w_pool, pscale, w_glu, w_out)


def _loss_head(x3, target, g):
    l, d = x3.shape

    def body(x_ref, t_ref, g_ref, dx_ref, loss_ref, dg_ref):
        i = pl.program_id(0)

        @pl.when(i == 0)
        def _():
            loss_ref[...] = jnp.zeros_like(loss_ref)
            dg_ref[...] = jnp.zeros_like(dg_ref)

        xx = x_ref[...]
        gg = g_ref[...]
        rstd = lax.rsqrt(jnp.mean(xx * xx, axis=-1, keepdims=True) + EPS)
        xhat = xx * rstd
        err = xhat * gg - t_ref[...]
        row_loss = jnp.mean(err * err, axis=-1, keepdims=True)
        loss_ref[...] += 0.5 * jnp.sum(row_loss, axis=0, keepdims=True)
        dout = err * (1.0 / d)
        dg_ref[...] += _colsum(dout * xhat)
        dxhat = dout * gg
        dx_ref[...] = rstd * (dxhat - xhat * jnp.mean(dxhat * xhat, axis=-1, keepdims=True))

    row = lambda i: (i, 0)
    return _pcall(
        body, name="loss_head", grid=(l // TM,),
        out_shape=(jax.ShapeDtypeStruct((l, d), F32), jax.ShapeDtypeStruct((1, LANE), F32),
                   jax.ShapeDtypeStruct((1, d), F32)),
        in_specs=[pl.BlockSpec((TM, d), row), pl.BlockSpec((TM, d), row), _const((1, d))],
        out_specs=(pl.BlockSpec((TM, d), row), _const((1, LANE)), _const((1, d))),
        compiler_params=_params(("arbitrary",)),
    )(x3, target, g)


def _adamw(parts, w, m, v, name):
    shape = w.shape
    c = shape[-1]
    r = int(np.prod(shape)) // c
    npart = parts.shape[0]
    tr = r
    for cand in (1024, 512, 256, 128, 64, 32, 16):
        if r % cand == 0 and cand * max(c, LANE) * 4 <= ADAM_BLOCK_BYTES:
            tr = cand
            break
    c1 = 1.0 - ADAM_B1
    c2 = 1.0 - ADAM_B2
    bc1 = 1.0 - ADAM_B1 ** ADAM_STEP
    bc2 = 1.0 - ADAM_B2 ** ADAM_STEP

    def body(p_ref, w_ref, m_ref, v_ref, g_ref, d_ref, mo_ref, vo_ref):
        g = p_ref[0].astype(F32)
        for k in range(1, npart):
            g = g + p_ref[k].astype(F32)
        mn = ADAM_B1 * m_ref[...] + c1 * g
        vn = ADAM_B2 * v_ref[...] + c2 * (g * g)
        m_hat = mn / bc1
        v_hat = vn / bc2
        g_ref[...] = g
        mo_ref[...] = mn
        vo_ref[...] = vn
        d_ref[...] = -ADAM_LR * (m_hat / (jnp.sqrt(v_hat) + ADAM_EPS) + ADAM_WD * w_ref[...])

    spec = pl.BlockSpec((tr, c), lambda i: (i, 0))
    outs = _pcall(
        body, name=name, grid=(r // tr,),
        out_shape=tuple(jax.ShapeDtypeStruct((r, c), F32) for _ in range(4)),
        in_specs=[pl.BlockSpec((npart, tr, c), lambda i: (0, i, 0)), spec, spec, spec],
        out_specs=(spec, spec, spec, spec),
        compiler_params=_params(("arbitrary",)),
    )(parts.reshape(npart, r, c), w.reshape(r, c), m.reshape(r, c), v.reshape(r, c))
    return tuple(o.reshape(shape) for o in outs)


def _perm(a):
    t, c = a.shape
    return a.reshape(t // TC, 8, SEG, c).transpose(0, 2, 1, 3).reshape(t, c)


def _unperm(a):
    t, c = a.shape
    return a.reshape(t // TC, SEG, 8, c).transpose(0, 2, 1, 3).reshape(t, c)


def _blockdiag(xb):
    n, a, b = xb.shape[-3:]
    eye = jnp.eye(n, dtype=xb.dtype)
    out = xb[..., :, :, None, :] * eye[:, None, :, None]
    return out.reshape(xb.shape[:-3] + (n * a, n * b))


def _diagblocks(mat, n):
    lead = mat.shape[:-2]
    a, b = mat.shape[-2] // n, mat.shape[-1] // n
    m5 = mat.reshape(lead + (n, a, n, b))
    nl = len(lead)
    dg = jnp.diagonal(m5, axis1=nl, axis2=nl + 2)
    return jnp.moveaxis(dg, -1, nl)


def _to_grid(a, rows):
    return a.reshape(rows, GRID_W, 4, LANE).transpose(2, 0, 1, 3).reshape(4, rows, GRID_W * LANE)


def _from_grid(a, rows):
    return a.reshape(4, rows, GRID_W, LANE).transpose(1, 2, 0, 3).reshape(rows * GRID_W, 4 * LANE)


def _pad128(a):
    flat = a.reshape(-1)
    pad = (-flat.shape[0]) % LANE
    return jnp.pad(flat, (0, pad)) if pad else flat


def kernel(x, c, ctx, c_ctx, norm_g, w_ada, b_ada, ffn_w1, ffn_w3, ffn_w2, w_in, pool_w, pool_scale, s5_a_re, s5_a_im, s5_log_dt, s5_b_re, s5_b_im, s5_c_re, s5_c_im, s5_d, s5_w_glu, w_out, final_g, loss_target, m_c_ctx, m_norm_g, m_w_ada, m_b_ada, m_ffn_w1, m_ffn_w3, m_ffn_w2, m_w_in, m_pool_w, m_pool_scale, m_s5_a_re, m_s5_a_im, m_s5_log_dt, m_s5_b_re, m_s5_b_im, m_s5_c_re, m_s5_c_im, m_s5_d, m_s5_w_glu, m_w_out, m_final_g, v_c_ctx, v_norm_g, v_w_ada, v_b_ada, v_ffn_w1, v_ffn_w3, v_ffn_w2, v_w_in, v_pool_w, v_pool_scale, v_s5_a_re, v_s5_a_im, v_s5_log_dt, v_s5_b_re, v_s5_b_im, v_s5_c_re, v_s5_c_im, v_s5_d, v_s5_w_glu, v_w_out, v_final_g):
    weights = dict(c_ctx=c_ctx, norm_g=norm_g, w_ada=w_ada, b_ada=b_ada, ffn_w1=ffn_w1, ffn_w3=ffn_w3, ffn_w2=ffn_w2,
                   w_in=w_in, pool_w=pool_w, pool_scale=pool_scale, s5_a_re=s5_a_re, s5_a_im=s5_a_im,
                   s5_log_dt=s5_log_dt, s5_b_re=s5_b_re, s5_b_im=s5_b_im, s5_c_re=s5_c_re, s5_c_im=s5_c_im, s5_d=s5_d,
                   s5_w_glu=s5_w_glu, w_out=w_out, final_g=final_g)
    mom_m = dict(c_ctx=m_c_ctx, norm_g=m_norm_g, w_ada=m_w_ada, b_ada=m_b_ada, ffn_w1=m_ffn_w1, ffn_w3=m_ffn_w3,
                 ffn_w2=m_ffn_w2, w_in=m_w_in, pool_w=m_pool_w, pool_scale=m_pool_scale, s5_a_re=m_s5_a_re,
                 s5_a_im=m_s5_a_im, s5_log_dt=m_s5_log_dt, s5_b_re=m_s5_b_re, s5_b_im=m_s5_b_im, s5_c_re=m_s5_c_re,
                 s5_c_im=m_s5_c_im, s5_d=m_s5_d, s5_w_glu=m_s5_w_glu, w_out=m_w_out, final_g=m_final_g)
    mom_v = dict(c_ctx=v_c_ctx, norm_g=v_norm_g, w_ada=v_w_ada, b_ada=v_b_ada, ffn_w1=v_ffn_w1, ffn_w3=v_ffn_w3,
                 ffn_w2=v_ffn_w2, w_in=v_w_in, pool_w=v_pool_w, pool_scale=v_pool_scale, s5_a_re=v_s5_a_re,
                 s5_a_im=v_s5_a_im, s5_log_dt=v_s5_log_dt, s5_b_re=v_s5_b_re, s5_b_im=v_s5_b_im, s5_c_re=v_s5_c_re,
                 s5_c_im=v_s5_c_im, s5_d=v_s5_d, s5_w_glu=v_s5_w_glu, w_out=v_w_out, final_g=v_final_g)

    l, d = x.shape[1], x.shape[2]
    lc = ctx.shape[1]
    t = l + lc
    rows = l // GRID_W
    fblk = ffn_w1.shape[-1]
    ngrp, gp = s5_a_re.shape[2], s5_a_re.shape[3]
    gc = s5_b_re.shape[3]
    hm = ngrp * gc
    nsb = 4
    gsb = ngrp // nsb
    assert lc == TM == TC and l % TM == 0 and hm == 4 * LANE and ngrp * gp == nsb * 512
    me = 4 * lax.axis_index("x") + 2 * lax.axis_index("y") + lax.axis_index("c")

    padc = ((0, 0), (0, 0), (0, FPAD - fblk))
    w13_loc = [jnp.concatenate([jnp.pad(ffn_w1[0, k], padc[1:]), jnp.pad(ffn_w3[0, k], padc[1:])], axis=-1).astype(BF16)
               for k in range(2)]
    w2_loc = [jnp.pad(ffn_w2[0, k], ((0, FPAD - fblk), (0, 0))).astype(BF16) for k in range(2)]
    gath, _ = _exchange([w13_loc[0], w2_loc[0], w_in[0].astype(BF16), c, norm_g[0]], [], "exchange_weights")
    w13_0 = gath[0]
    w2_0 = gath[1].reshape(NDEV // 2, 2 * FPAD, d)
    w_in_f = gath[2].reshape(d, -1)
    c_all = gath[3].reshape(NDEV, d)
    g_all = gath[4].transpose(1, 0, 2).reshape(3, d)

    wa = w_ada.shape[-1]
    c16 = jnp.concatenate([c_all, c_ctx.reshape(1, d), jnp.zeros((7, d), F32)], axis=0)
    b_blk = lax.dynamic_slice_in_dim(b_ada, me * wa, wa, axis=1)
    mod_sl = _ada_fwd(c16, w_ada[0], b_blk)
    gath, _ = _exchange([mod_sl], [], "exchange_mod")
    mod_full = gath[0].transpose(1, 0, 2).reshape(16, NDEV * wa)
    mod_l = lax.dynamic_index_in_dim(mod_full, me, axis=0, keepdims=False).reshape(9, d)
    modtab = jnp.stack([mod_full[8].reshape(9, d), mod_l])

    x0 = jnp.concatenate([ctx[0], x[0]], axis=0)
    g0, g1, g2 = g_all[0:1], g_all[1:2], g_all[2:3]
    x1, h0, ab0, o0, w13_1, w2_1, w_out_f, w_glu_f = _ffn_fwd(
        x0, modtab, g0, w13_0, w2_0, 0, True, "ffn0_fwd",
        xchg=([w13_loc[1], w2_loc[1], w_out[0].astype(BF16), s5_w_glu[0].astype(BF16)], []))
    w2_1 = w2_1.reshape(NDEV // 2, 2 * FPAD, d)
    w_out_f = w_out_f.reshape(-1, d)
    w_glu_f = w_glu_f.reshape(hm, hm)
    h1, u = _mix_in_fwd(x1, modtab, g1, w_in_f)

    v_grid = _to_grid(u[lc:, :hm], rows)
    dpool = _from_grid(_pool(v_grid, False, "pool_fwd"), rows)
    w_pool_bd = _blockdiag(pool_w[0]).astype(BF16)

    q = 2 * ngrp
    kvec = np.concatenate([np.arange(1, SEG + 1), SEG * np.arange(1, 9), SEG * np.arange(8, 0, -1)]).astype(np.float32)
    a_re3, a_im3 = s5_a_re[0].reshape(q, 1, gp), s5_a_im[0].reshape(q, 1, gp)
    ldt3 = jnp.broadcast_to(s5_log_dt[0].reshape(q, 1, 1), (q, 1, gp))
    c_re3, c_im3 = s5_c_re[0].reshape(q, gc, gp), s5_c_im[0].reshape(q, gc, gp)
    cp_re, cp_im, tab_re, tab_im = _s5_prep(a_re3, a_im3, ldt3, c_re3, c_im3, jnp.broadcast_to(jnp.asarray(kvec).reshape(1, NTAB, 1), (1, NTAB, gp)))
    tabs = [(tab_re.reshape(2, ngrp, NTAB, gp)[k].transpose(1, 0, 2).reshape(NTAB, ngrp * gp),
             tab_im.reshape(2, ngrp, NTAB, gp)[k].transpose(1, 0, 2).reshape(NTAB, ngrp * gp)) for k in range(2)]
    b_t = lambda b: _blockdiag(b[0].reshape(nsb, gsb, gp, gc).transpose(0, 1, 3, 2)).astype(BF16)
    bre, bim = b_t(s5_b_re), b_t(s5_b_im)
    cp_t = lambda cp: _blockdiag(cp.reshape(2, nsb, gsb, gc, gp)).astype(BF16)
    ctr, cti = cp_t(cp_re), cp_t(cp_im)
    cmr, cmi = jnp.swapaxes(ctr, -1, -2), jnp.swapaxes(cti, -1, -2)
    dskip = s5_d

    nch = t // TC
    order_f = lambda i: i
    order_b = lambda i: jnp.where(i == 0, 0, nch - i)
    rorder_f = lambda i: nch - 1 - i
    rorder_b = lambda i: jnp.where(i == nch - 1, 0, i + 1)
    up = _perm(u[:, hm:])
    yp_f, hr_f, hi_f, cinr_f, cini_f = _s5_fwd(up, bre, bim, cmr[0], cmi[0], tabs[0][0], tabs[0][1], dskip,
                                               order_f, False, True, "s5_fwd_f")
    yp_b, hr_b, hi_b, cinr_b, cini_b = _s5_fwd(up, bre, bim, cmr[1], cmi[1], tabs[1][0], tabs[1][1], dskip,
                                               order_b, True, False, "s5_fwd_b")
    y_f, y_b = _unperm(yp_f), _unperm(yp_b)

    x2, ypre, cat, mix = _mix_out_fwd(y_f, y_b, dpool, x1, modtab, w_pool_bd, pool_scale, w_glu_f, w_out_f)
    x3, h2, ab2, o2 = _ffn_fwd(x2, modtab, g2, w13_1, w2_1, 6, False, "ffn1_fwd")
    dx3, loss_part, dfinal_g = _loss_head(x3, loss_target[0], final_g.reshape(1, d))
    loss = lax.psum(loss_part[0, 0], AXES)

    dhp2, d13_1, d2_1 = _ffn_bwd(dx3, modtab, h2, ab2, w13_1, w2_1, 6, False, "ffn1_bwd")
    dx2, dmod_678, dg2 = _ffn_bwd_norm(dx3, dhp2, x2, o2, modtab, g2, 6, False, "ffn1_bwd_norm")
    dypre, ddpool, dw_out, dw_glu, dw_pool_bd, dpscale, dgate5 = _mix_out_bwd(
        dx2, mix, cat, ypre, dpool, modtab, w_pool_bd, pool_scale, w_glu_f, w_out_f)

    dyp = _perm(jnp.concatenate([jnp.zeros((lc, hm), F32), dypre], axis=0))
    dup_f, dar_f, dai_f, dcr_f, dci_f, dbr_f, dbi_f, dd_skip, r13_1, r2_1 = _s5_bwd(
        dyp, up, hr_f, hi_f, cinr_f, cini_f, bre, bim, ctr[0], cti[0], tabs[0][0], tabs[0][1], dskip,
        rorder_f, False, True, "s5_bwd_f", xchg=([], [d13_1, d2_1.reshape(NDEV, FPAD, d)]))
    dup_b, dar_b, dai_b, dcr_b, dci_b, dbr_b, dbi_b, _, r_out, r_glu = _s5_bwd(
        dyp, up, hr_b, hi_b, cinr_b, cini_b, bre, bim, ctr[1], cti[1], tabs[1][0], tabs[1][1], dskip,
        rorder_b, True, False, "s5_bwd_b",
        xchg=([], [dw_out.reshape(NDEV, -1, d), dw_glu.reshape(NDEV, hm // NDEV, hm)]))
    du_f, du_b = _unperm(dup_f), _unperm(dup_b)

    dab_r = jnp.stack([dar_f, dar_b]).reshape(q, 1, gp)
    dab_i = jnp.stack([dai_f, dai_b]).reshape(q, 1, gp)
    dcp_r = _diagblocks(jnp.stack([dcr_f, dcr_b]), gsb).reshape(q, gc, gp)
    dcp_i = _diagblocks(jnp.stack([dci_f, dci_b]), gsb).reshape(q, gc, gp)
    ga_re, ga_im, gldt, gc_re, gc_im = _s5_param_bwd(a_re3, a_im3, ldt3, c_re3, c_im3, dab_r, dab_i, dcp_r, dcp_i)
    gb_re = (_diagblocks(dbr_f, gsb) + _diagblocks(dbr_b, gsb)).transpose(0, 1, 3, 2).reshape(ngrp, gp, gc)
    gb_im = (_diagblocks(dbi_f, gsb) + _diagblocks(dbi_b, gsb)).transpose(0, 1, 3, 2).reshape(ngrp, gp, gc)

    def pack(small):
        offs, pieces, off = {}, [], 0
        for k_, a_ in small.items():
            p_ = _pad128(a_.astype(F32))
            offs[k_] = (off, int(np.prod(a_.shape)))
            off += p_.shape[0]
            pieces.append(p_)
        return jnp.concatenate(pieces).reshape(1, off), offs

    gw_pool = _diagblocks(dw_pool_bd, 4)
    bundle_a, offs = pack(dict(pool_w=gw_pool, pool_scale=dpscale, s5_a_re=ga_re, s5_a_im=ga_im,
                               s5_log_dt=gldt[:, 0, 0], s5_b_re=gb_re, s5_b_im=gb_im, s5_c_re=gc_re, s5_c_im=gc_im,
                               s5_d=dd_skip, final_g=dfinal_g))

    du_pool = _from_grid(_pool(_to_grid(ddpool, rows), True, "pool_bwd"), rows)
    dx1, dw_in, dmod_34, dg1 = _mix_in_bwd(du_pool, du_f, du_b, dx2, h1, x1, modtab, g1, w_in_f)
    dhp0, d13_0, d2_0, bund_a, r_in = _ffn_bwd(dx1, modtab, h0, ab0, w13_0, w2_0, 0, True, "ffn0_bwd",
                                               xchg=([bundle_a], [dw_in.reshape(NDEV, d // NDEV, -1)]))
    dx0, dmod_012, dg0, r13_0, r2_0 = _ffn_bwd_norm(dx1, dhp0, x0, o0, modtab, g0, 0, True, "ffn0_bwd_norm",
                                                    xchg=([], [d13_0, d2_0.reshape(NDEV, FPAD, d)]))
    grad_x = dx0[lc:].reshape(1, l, d)

    dmod_c = jnp.concatenate([dmod_012[0], dmod_34[0], jnp.zeros((4, d), F32)], axis=0)
    dmod_l = jnp.concatenate([dmod_012[1], dmod_34[1], dgate5, dmod_678[0]], axis=0)
    bundle_b, offs_b = pack(dict(norm_g=jnp.concatenate([dg0, dg1, dg2], axis=0), dmod_l=dmod_l, dmod_c=dmod_c))
    gath, _ = _exchange([bundle_b], [], "exchange_small")
    bunds = {**{k_: (bund_a.reshape(NDEV, -1), v_) for k_, v_ in offs.items()},
             **{k_: (gath[0].reshape(NDEV, -1), v_) for k_, v_ in offs_b.items()}}
    r13, r2 = (r13_0, r13_1), (r2_0, r2_1)

    def piece(name):
        b_, (o_, n_) = bunds[name]
        return b_[:, o_:o_ + n_]

    dl_all = lax.dynamic_slice_in_dim(piece("dmod_l"), me * wa, wa, axis=1)
    dc_all = lax.dynamic_slice_in_dim(piece("dmod_c"), me * wa, wa, axis=1)
    g_wada, gc_part = _ada_bwd(c16, w_ada[0], dl_all, dc_all)
    gath, _ = _exchange([gc_part], [], "exchange_cctx")

    parts = {
        "c_ctx": gath[0].reshape(NDEV, d),
        "norm_g": lax.dynamic_slice_in_dim(piece("norm_g").reshape(NDEV, 3, d), me * (d // NDEV), d // NDEV,
                                           axis=2).reshape((NDEV,) + norm_g.shape),
        "w_ada": g_wada.reshape((1,) + w_ada.shape),
        "b_ada": jnp.concatenate([piece("dmod_l"), piece("dmod_c")], axis=0).reshape((2 * NDEV,) + b_ada.shape),
        "ffn_w1": jnp.stack([r13[0][:, :, :fblk], r13[1][:, :, :fblk]], axis=1).reshape((NDEV,) + ffn_w1.shape),
        "ffn_w3": jnp.stack([r13[0][:, :, FPAD:FPAD + fblk], r13[1][:, :, FPAD:FPAD + fblk]],
                            axis=1).reshape((NDEV,) + ffn_w3.shape),
        "ffn_w2": jnp.stack([r2[0][:, :fblk, :], r2[1][:, :fblk, :]], axis=1).reshape((NDEV,) + ffn_w2.shape),
        "w_in": r_in.reshape((NDEV,) + w_in.shape),
        "s5_w_glu": r_glu.reshape((NDEV,) + s5_w_glu.shape),
        "w_out": r_out.reshape((NDEV,) + w_out.shape),
    }
    for k_ in ("pool_w", "pool_scale", "s5_a_re", "s5_a_im", "s5_log_dt", "s5_b_re", "s5_b_im", "s5_c_re", "s5_c_im",
               "s5_d", "final_g"):
        parts[k_] = piece(k_).reshape((NDEV,) + weights[k_].shape)

    grads, deltas, new_m, new_v = [], [], [], []
    for k_ in weights:
        g_, d_, m_, v_ = _adamw(parts[k_], weights[k_], mom_m[k_], mom_v[k_], "adamw_" + k_)
        grads.append(g_)
        deltas.append(d_)
        new_m.append(m_)
        new_v.append(v_)
    return (loss, grad_x, *grads, *deltas, *new_m, *new_v)
```

```python
import functools
import math

import numpy as np
import jax
import jax.numpy as jnp
from jax import lax
from jax.experimental import pallas as pl
from jax.experimental.pallas import tpu as pltpu

F32 = jnp.float32
BF16 = jnp.bfloat16
AXES = ("x", "y", "c")
NDEV = 8
EPS = 1e-6
TM = 256
TML = 512
TC = 256
SEG = TC // 8
NTAB = SEG + 16
GRID_W = 64
POOL_WINDOWS = (2, 4, 8, 16)
LANE = 128
FPAD = 384
VMEM_LIMIT = 56 * 1024 * 1024
ADAM_BLOCK_BYTES = 512 * 1024
ADAM_LR, ADAM_B1, ADAM_B2, ADAM_EPS, ADAM_WD, ADAM_STEP = 0.001, 0.9, 0.999, 1e-08, 0.01, 10


def _raw_call(body, kw):
    return pl.pallas_call(body, **kw)


def _pcall(body, xchg=None, edges=None, **kw):
    extra = ()
    if xchg:
        arrs, kinds = [a for a, _ in xchg], [k for _, k in xchg]
        n = len(arrs)
        n_in, n_out, n_scr = len(kw["in_specs"]), len(kw["out_shape"]), len(kw.get("scratch_shapes", ()))
        inner = body

        def hosted(*refs):
            a, b = n_in, n_in + n
            c_, d_ = b + n_out, b + n_out + n
            e = d_ + n_scr
            first, last = edges()

            @pl.when(first)
            def _():
                _xchg_start(refs[a:b], refs[c_:d_], refs[e:], kinds)

            inner(*refs[:a], *refs[b:c_], *refs[d_:e])

            @pl.when(last)
            def _():
                _xchg_wait(refs[a:b], refs[c_:d_], refs[e:], kinds)

        body = hosted
        any_spec = pl.BlockSpec(memory_space=pl.ANY)
        kw = dict(kw, in_specs=list(kw["in_specs"]) + [any_spec] * n,
                  out_shape=tuple(kw["out_shape"]) + _xchg_shapes(arrs, kinds),
                  out_specs=tuple(kw["out_specs"]) + (any_spec,) * n,
                  scratch_shapes=list(kw.get("scratch_shapes", ())) + _xchg_sems(n))
        extra = tuple(arrs)
    call = _raw_call(body, kw)
    return (lambda *args: call(*args, *extra)) if extra else call


def _params(sem):
    return pltpu.CompilerParams(dimension_semantics=sem, vmem_limit_bytes=VMEM_LIMIT)


def _dot(a, b):
    return jnp.dot(a, b, preferred_element_type=F32)


def _dot_nt(a, b):
    return lax.dot_general(a, b, (((1,), (1,)), ((), ())), preferred_element_type=F32)


def _dot_tn(a, b):
    return lax.dot_general(a, b, (((0,), (0,)), ((), ())), preferred_element_type=F32)


def _sigmoid(x):
    return 1.0 / (1.0 + jnp.exp(-x))


def _colsum(a):
    return jnp.sum(a, axis=0, keepdims=True)


def _resident(shape):
    nd = len(shape)
    return pl.BlockSpec(shape, lambda *_: (0,) * nd, pipeline_mode=pl.Buffered(1))


def _const(shape):
    nd = len(shape)
    return pl.BlockSpec(shape, lambda *_: (0,) * nd)


GATHER = "gather"
GATHER_REL = "gather_rel"
A2A = "a2a"
A2A_REL = "a2a_rel"
A2A_SAME = "a2a_same"
A2A_OTHER = "a2a_other"


def _exchange(items, name):
    arrs, kinds = [a for a, _ in items], [k for _, k in items]
    n = len(arrs)

    def body(*refs):
        _xchg_start(refs[:n], refs[n:2 * n], refs[2 * n:], kinds)
        _xchg_wait(refs[:n], refs[n:2 * n], refs[2 * n:], kinds)

    any_spec = pl.BlockSpec(memory_space=pl.ANY)
    outs = _pcall(
        body, name=name, out_shape=_xchg_shapes(arrs, kinds), in_specs=[any_spec] * n, out_specs=[any_spec] * n,
        scratch_shapes=_xchg_sems(n),
    )(*arrs)
    return list(outs)


def _edges1(n0):
    return lambda: (pl.program_id(0) == 0, pl.program_id(0) == n0 - 1)


def _edges2(n0, n1):
    return lambda: ((pl.program_id(0) == 0) & (pl.program_id(1) == 0),
                    (pl.program_id(0) == n0 - 1) & (pl.program_id(1) == n1 - 1))


def _xchg_shapes(arrs, kinds):
    return tuple(jax.ShapeDtypeStruct(((NDEV,) if k in (GATHER, GATHER_REL) else ()) + tuple(a.shape), a.dtype)
                 for a, k in zip(arrs, kinds))


def _xchg_sems(n):
    return [pltpu.SemaphoreType.DMA((n * (NDEV - 1),)), pltpu.SemaphoreType.DMA((n * (NDEV - 1),)),
            pltpu.SemaphoreType.DMA((n,))]


def _xchg_plan(in_refs, out_refs, sems, kinds):
    send_sems, recv_sems, loc_sems = sems
    x, y, c = (lax.axis_index(a) for a in AXES)
    my_abs, my_chip = 4 * x + 2 * y + c, 2 * x + y

    def peer(p):
        px = 1 - x if p & 4 else x
        py = 1 - y if p & 2 else y
        pc = 1 - c if p & 1 else c
        return (px, py, pc), 4 * px + 2 * py + pc, 2 * px + py

    starts, relays, recvs = [], [], []
    for k, kind in enumerate(kinds):
        src, dst = in_refs[k], out_refs[k]

        def remote(src_ref, row, p, pair, dst=dst, k=k):
            dev, sem = peer(p)[0], k * (NDEV - 1) + pair - 1
            return lambda: pltpu.make_async_remote_copy(
                src_ref=src_ref, dst_ref=dst.at[row], send_sem=send_sems.at[sem], recv_sem=recv_sems.at[sem],
                device_id=dev, device_id_type=pl.DeviceIdType.MESH)

        def local(src_ref, row, dst=dst, k=k):
            return lambda: pltpu.make_async_copy(src_ref, dst.at[row], loc_sems.at[k])

        if kind == GATHER:
            for p in range(1, NDEV):
                starts.append((remote(src, my_abs, p, p), False))
                recvs.append(remote(src, peer(p)[1], p, p))
            starts.append((local(src, my_abs), True))
        elif kind == GATHER_REL:
            for p in (1, 4, 2, 6):
                starts.append((remote(src, 4 * (p & 1) + my_chip, p, p), False))
            for q in (4, 2, 6):
                pchip = peer(q)[2]
                relays.append((remote(src, pchip, q, q), remote(dst.at[pchip], 4 + pchip, 1, q | 1)))
                recvs.append(remote(src, 4 + pchip, 1, q | 1))
            recvs.append(remote(src, 4 + my_chip, 1, 1))
            starts.append((local(src, my_chip), True))
        elif kind in (A2A, A2A_REL):
            for p in range(1, NDEV):
                _, pabs, pchip = peer(p)
                theirs, mine = (pabs, my_abs) if kind == A2A else (4 * (p & 1) + pchip, 4 * (p & 1) + my_chip)
                starts.append((remote(src.at[theirs], mine, p, p), False))
                recvs.append(remote(src.at[theirs], theirs, p, p))
            own = my_abs if kind == A2A else my_chip
            starts.append((local(src.at[own], own), True))
        else:
            for p in ((4, 2, 6) if kind == A2A_SAME else (1, 5, 3, 7)):
                pchip = peer(p)[2]
                starts.append((remote(src.at[pchip], my_chip, p, p), False))
                recvs.append(remote(src.at[pchip], pchip, p, p))
            if kind == A2A_SAME:
                starts.append((local(src.at[my_chip], my_chip), True))
    return starts, relays, recvs


def _xchg_start(in_refs, out_refs, sems, kinds):
    for make, _ in _xchg_plan(in_refs, out_refs, sems, kinds)[0]:
        make().start()


def _xchg_wait(in_refs, out_refs, sems, kinds):
    starts, relays, recvs = _xchg_plan(in_refs, out_refs, sems, kinds)
    forwards = []
    for arrival, forward in relays:
        arrival().wait_recv()
        fwd = forward()
        fwd.start()
        forwards.append(fwd)
    for make in recvs:
        make().wait_recv()
    for make, is_local in starts:
        if is_local:
            make().wait()
        else:
            make().wait_send()
    for fwd in forwards:
        fwd.wait_send()


def _ada_fwd(c16, w_blk, b_blk):
    def body(c_ref, w_ref, b_ref, o_ref):
        cc = c_ref[...]
        s = (cc * _sigmoid(cc)).astype(BF16)
        o_ref[...] = _dot(s, w_ref[...].astype(BF16)) + b_ref[...]

    return _pcall(body, name="ada_fwd", out_shape=jax.ShapeDtypeStruct((16, w_blk.shape[1]), F32),
                  compiler_params=_params(None))(c16, w_blk, b_blk)


def _ada_bwd(c16, w_blk, dl, dc):
    d, w = w_blk.shape

    def body(c_ref, w_ref, dl_ref, dc_ref, gw_ref, gc_ref):
        cc = c_ref[...]
        sg = _sigmoid(cc)
        s = (cc * sg).astype(BF16)
        dctot = _colsum(dc_ref[...])
        dm = jnp.concatenate([dl_ref[...], jnp.broadcast_to(dctot, (8, w))], axis=0)
        rows = lax.broadcasted_iota(jnp.int32, (16, w), 0)
        dm = jnp.where(rows <= 8, dm, 0.0).astype(BF16)
        gw_ref[...] = _dot_tn(s, dm)
        t = _dot_nt(jnp.broadcast_to(dctot, (8, w)).astype(BF16), w_ref[...].astype(BF16))[0:1]
        c8, s8 = cc[8:9], sg[8:9]
        gc_ref[...] = t * (s8 * (1.0 + c8 * (1.0 - s8)))

    return _pcall(body, name="ada_bwd",
                  out_shape=(jax.ShapeDtypeStruct((d, w), F32), jax.ShapeDtypeStruct((1, d), F32)),
                  compiler_params=_params(None))(c16, w_blk, dl, dc)


def _norm_fwd(x, g, shift, scale):
    rstd = lax.rsqrt(jnp.mean(x * x, axis=-1, keepdims=True) + EPS)
    xhat = x * rstd
    xn = xhat * g
    return rstd, xhat, xn, xn * (1.0 + scale) + shift


def _norm_bwd(dh, rstd, xhat, xn, g, scale):
    dxn = dh * (1.0 + scale)
    dxhat = dxn * g
    dx = rstd * (dxhat - xhat * jnp.mean(dxhat * xhat, axis=-1, keepdims=True))
    return dx, _colsum(dh), _colsum(dh * xn), _colsum(dxn * xhat)


def _mod_spec(d, has_ctx):
    if has_ctx:
        return pl.BlockSpec((1, 9, d), lambda *ids: (jnp.minimum(ids[-1], 1), 0, 0))
    return pl.BlockSpec((1, 9, d), lambda *ids: (1, 0, 0))


def _tile(has_ctx):
    return TM if has_ctx else TML


def _dmod_spec(nrow, d, has_ctx):
    if has_ctx:
        return pl.BlockSpec((1, nrow, d), lambda i: (jnp.minimum(i, 1), 0, 0))
    return pl.BlockSpec((1, nrow, d), lambda i: (0, 0, 0))


def _ffn_fwd(x, modtab, g, w13, w2, k0, has_ctx, name, xchg=None):
    t, d = x.shape
    nb, _, w = w13.shape
    fp = w // 2
    tm = _tile(has_ctx)

    def body(x_ref, mod_ref, g_ref, w13_ref, w2_ref, xo_ref, h_ref, ab_ref, o_ref):
        xx = x_ref[...]
        shift, scale, gate = mod_ref[0, k0:k0 + 1, :], mod_ref[0, k0 + 1:k0 + 2, :], mod_ref[0, k0 + 2:k0 + 3, :]
        _, _, _, h = _norm_fwd(xx, g_ref[...], shift, scale)
        hb = h.astype(BF16)
        h_ref[...] = hb
        acc = jnp.zeros((tm, d), F32)
        for p in range(nb // 2):
            zs = []
            for q in range(2):
                blk = 2 * p + q
                ab = _dot(hb, w13_ref[blk])
                ab_ref[:, blk * w:(blk + 1) * w] = ab.astype(BF16)
                a, b = ab[:, :fp], ab[:, fp:]
                zs.append((a * _sigmoid(a) * b).astype(BF16))
            acc = acc + _dot(jnp.concatenate(zs, axis=1), w2_ref[p])
        o_ref[...] = acc.astype(BF16)
        xo_ref[...] = xx + (0.5 * gate) * acc

    row = lambda i: (i, 0)
    return _pcall(
        body, name=name, grid=(t // tm,), xchg=xchg, edges=_edges1(t // tm),
        out_shape=(jax.ShapeDtypeStruct((t, d), F32), jax.ShapeDtypeStruct((t, d), BF16),
                   jax.ShapeDtypeStruct((t, nb * w), BF16), jax.ShapeDtypeStruct((t, d), BF16)),
        in_specs=[pl.BlockSpec((tm, d), row), _mod_spec(d, has_ctx), _const((1, d)),
                  _resident(w13.shape), _resident(w2.shape)],
        out_specs=(pl.BlockSpec((tm, d), row), pl.BlockSpec((tm, d), row), pl.BlockSpec((tm, nb * w), row),
                   pl.BlockSpec((tm, d), row)),
        compiler_params=_params(("arbitrary",)),
    )(x, modtab, g, w13, w2)


def _ffn_bwd(dy, modtab, h, ab, w13, w2, k0, has_ctx, name, pair0=0, npair=None, xchg=None):
    t, d = dy.shape
    _, _, w = w13.shape
    fp = w // 2
    npair = w13.shape[0] // 2 if npair is None else npair
    nb = 2 * npair
    tm = _tile(has_ctx)
    nt = t // tm

    def body(dy_ref, mod_ref, h_ref, ab_ref, w13_ref, w2_ref, dh_ref, d13_ref, d2_ref, acc13, acc2):
        i = pl.program_id(1)

        @pl.when(i == 0)
        def _():
            acc13[...] = jnp.zeros_like(acc13)
            acc2[...] = jnp.zeros_like(acc2)

        gate = mod_ref[0, k0 + 2:k0 + 3, :]
        do = (dy_ref[...] * (0.5 * gate)).astype(BF16)
        dz = _dot_nt(do, w2_ref[0])
        hb = h_ref[...]
        dh = jnp.zeros((tm, d), F32)
        zs = []
        for q in range(2):
            ab = ab_ref[:, q * w:(q + 1) * w].astype(F32)
            a, b = ab[:, :fp], ab[:, fp:]
            sg = _sigmoid(a)
            sa = a * sg
            dzq = dz[:, q * fp:(q + 1) * fp]
            da = dzq * b * (sg * (1.0 + a * (1.0 - sg)))
            db = dzq * sa
            dab = jnp.concatenate([da, db], axis=1).astype(BF16)
            dh = dh + _dot_nt(dab, w13_ref[q])
            acc13[q] += _dot_tn(hb, dab)
            zs.append((sa * b).astype(BF16))
        acc2[...] += _dot_tn(jnp.concatenate(zs, axis=1), do)
        dh_ref[0] = dh

        @pl.when(i == nt - 1)
        def _():
            d13_ref[...] = acc13[...].astype(BF16)
            d2_ref[0] = acc2[...].astype(BF16)

    mod_spec = _mod_spec(d, has_ctx)
    return _pcall(
        body, name=name, grid=(npair, nt), xchg=xchg, edges=_edges2(npair, nt),
        out_shape=(jax.ShapeDtypeStruct((npair, t, d), F32), jax.ShapeDtypeStruct((nb, d, w), BF16),
                   jax.ShapeDtypeStruct((npair, 2 * fp, d), BF16)),
        in_specs=[pl.BlockSpec((tm, d), lambda p, i: (i, 0)), mod_spec, pl.BlockSpec((tm, d), lambda p, i: (i, 0)),
                  pl.BlockSpec((tm, 2 * w), lambda p, i: (i, pair0 + p)),
                  pl.BlockSpec((2, d, w), lambda p, i: (pair0 + p, 0, 0)),
                  pl.BlockSpec((1, 2 * fp, d), lambda p, i: (pair0 + p, 0, 0))],
        out_specs=(pl.BlockSpec((1, tm, d), lambda p, i: (p, i, 0)), pl.BlockSpec((2, d, w), lambda p, i: (p, 0, 0)),
                   pl.BlockSpec((1, 2 * fp, d), lambda p, i: (p, 0, 0))),
        scratch_shapes=[pltpu.VMEM((2, d, w), F32), pltpu.VMEM((2 * fp, d), F32)],
        compiler_params=_params(("arbitrary", "arbitrary")),
    )(dy, modtab, h, ab, w13, w2)


def _ffn_bwd_norm(dy, dhps, x, o, modtab, g, k0, has_ctx, name, xchg=None):
    t, d = dy.shape
    ngrp = 2 if has_ctx else 1
    tm = _tile(has_ctx)
    ndh = len(dhps)

    def body(dy_ref, *rest):
        dhp_refs = rest[:ndh]
        x_ref, o_ref, mod_ref, g_ref, dx_ref, dmod_ref, dg_ref = rest[ndh:]
        i = pl.program_id(0)

        @pl.when(i == 0)
        def _():
            dg_ref[...] = jnp.zeros_like(dg_ref)

        @pl.when((i == 0) | (i == ngrp - 1))
        def _():
            dmod_ref[...] = jnp.zeros_like(dmod_ref)

        dh = None
        for ref in dhp_refs:
            for p in range(ref.shape[0]):
                dh = ref[p] if dh is None else dh + ref[p]
        scale = mod_ref[0, k0 + 1:k0 + 2, :]
        gg = g_ref[...]
        rstd, xhat, xn, _ = _norm_fwd(x_ref[...], gg, 0.0, scale)
        dxn, dshift, dscale, dg = _norm_bwd(dh, rstd, xhat, xn, gg, scale)
        dyv = dy_ref[...]
        dx_ref[...] = dyv + dxn
        dmod_ref[0, 0:1, :] += dshift
        dmod_ref[0, 1:2, :] += dscale
        dmod_ref[0, 2:3, :] += _colsum(0.5 * dyv * o_ref[...].astype(F32))
        dg_ref[...] += dg

    row = lambda i: (i, 0)
    return _pcall(
        body, name=name, grid=(t // tm,), xchg=xchg, edges=_edges1(t // tm),
        out_shape=(jax.ShapeDtypeStruct((t, d), F32), jax.ShapeDtypeStruct((ngrp, 3, d), F32),
                   jax.ShapeDtypeStruct((1, d), F32)),
        in_specs=[pl.BlockSpec((tm, d), row)]
        + [pl.BlockSpec((a.shape[0], tm, d), lambda i: (0, i, 0)) for a in dhps]
        + [pl.BlockSpec((tm, d), row), pl.BlockSpec((tm, d), row), _mod_spec(d, has_ctx), _const((1, d))],
        out_specs=(pl.BlockSpec((tm, d), row), _dmod_spec(3, d, has_ctx), _const((1, d))),
        compiler_params=_params(("arbitrary",)),
    )(dy, *dhps, x, o, modtab, g)


def _mix_in_fwd(x1, modtab, g, w_in):
    t, d = x1.shape

    def body(x_ref, mod_ref, g_ref, w_ref, h_ref, u_ref):
        _, _, _, h = _norm_fwd(x_ref[...], g_ref[...], mod_ref[0, 3:4, :], mod_ref[0, 4:5, :])
        hb = h.astype(BF16)
        h_ref[...] = hb
        u_ref[...] = _dot(hb, w_ref[...])

    row = lambda i: (i, 0)
    return _pcall(
        body, name="mix_in_fwd", grid=(t // TM,),
        out_shape=(jax.ShapeDtypeStruct((t, d), BF16), jax.ShapeDtypeStruct((t, w_in.shape[1]), F32)),
        in_specs=[pl.BlockSpec((TM, d), row), _mod_spec(d, True), _const((1, d)), _resident(w_in.shape)],
        out_specs=(pl.BlockSpec((TM, d), row), pl.BlockSpec((TM, w_in.shape[1]), row)),
        compiler_params=_params(("arbitrary",)),
    )(x1, modtab, g, w_in)


def _mix_in_bwd(du_pool, du_f, du_b, dx2, h1, x1, modtab, g, w_in):
    t, d = x1.shape
    m = w_in.shape[1]
    hm = m // 2
    nt = t // TM

    def body(dup_ref, duf_ref, dub_ref, dx2_ref, h_ref, x_ref, mod_ref, g_ref, w_ref, dx_ref, dw_ref, dmod_ref, dg_ref, acc):
        i = pl.program_id(0)

        @pl.when(i == 0)
        def _():
            acc[...] = jnp.zeros_like(acc)
            dg_ref[...] = jnp.zeros_like(dg_ref)

        @pl.when(i <= 1)
        def _():
            dmod_ref[...] = jnp.zeros_like(dmod_ref)

        lat = (i > 0).astype(F32)
        du = jnp.concatenate([dup_ref[...] * lat, duf_ref[...] + dub_ref[...]], axis=1).astype(BF16)
        dh = _dot_nt(du, w_ref[...])
        acc[...] += _dot_tn(h_ref[...], du)
        scale = mod_ref[0, 4:5, :]
        gg = g_ref[...]
        rstd, xhat, xn, _ = _norm_fwd(x_ref[...], gg, 0.0, scale)
        dxn, dshift, dscale, dg = _norm_bwd(dh, rstd, xhat, xn, gg, scale)
        dx_ref[...] = dx2_ref[...] * lat + dxn
        dmod_ref[0, 0:1, :] += dshift
        dmod_ref[0, 1:2, :] += dscale
        dg_ref[...] += dg

        @pl.when(i == nt - 1)
        def _():
            dw_ref[...] = acc[...].astype(BF16)

    row = lambda i: (i, 0)
    lrow = lambda i: (jnp.maximum(i - 1, 0), 0)
    return _pcall(
        body, name="mix_in_bwd", grid=(nt,),
        out_shape=(jax.ShapeDtypeStruct((t, d), F32), jax.ShapeDtypeStruct((d, m), BF16),
                   jax.ShapeDtypeStruct((2, 2, d), F32), jax.ShapeDtypeStruct((1, d), F32)),
        in_specs=[pl.BlockSpec((TM, hm), lrow), pl.BlockSpec((TM, hm), row), pl.BlockSpec((TM, hm), row),
                  pl.BlockSpec((TM, d), lrow), pl.BlockSpec((TM, d), row), pl.BlockSpec((TM, d), row),
                  _mod_spec(d, True), _const((1, d)), _resident(w_in.shape)],
        out_specs=(pl.BlockSpec((TM, d), row), _const((d, m)), _dmod_spec(2, d, True), _const((1, d))),
        scratch_shapes=[pltpu.VMEM((d, m), F32)],
        compiler_params=_params(("arbitrary",)),
    )(du_pool, du_f, du_b, dx2, h1, x1, modtab, g, w_in)


def _pool(v, transpose, name):
    ngrp, rows, n = v.shape
    nchunk = 4
    cw = n // nchunk

    def rowsum(val, lo, hi):
        ri = lax.broadcasted_iota(jnp.int32, (rows, rows), 0)
        ci = lax.broadcasted_iota(jnp.int32, (rows, rows), 1)
        sel = (((ci - ri) >= -lo) & ((ci - ri) <= hi)).astype(BF16)
        v1 = val.astype(BF16)
        r1 = val - v1.astype(F32)
        v2 = r1.astype(BF16)
        v3 = (r1 - v2.astype(F32)).astype(BF16)
        return _dot(sel, v1) + _dot(sel, v2) + _dot(sel, v3)

    def one(v_ref, o_ref, r_scr, a_scr, win):
        lo = win // 2
        hi = win - 1 - lo
        rlo, rhi = (hi, lo) if transpose else (lo, hi)
        ridx = lax.broadcasted_iota(jnp.int32, (rows, 1), 0)
        cnt_r = (jnp.minimum(ridx + hi + 1, rows) - jnp.maximum(ridx - lo, 0)).astype(F32)
        cidx = lax.broadcasted_iota(jnp.int32, (1, n), 1) // LANE
        cnt_c = (jnp.minimum(cidx + hi + 1, GRID_W) - jnp.maximum(cidx - lo, 0)).astype(F32)
        if not transpose:
            for k in range(nchunk):
                sl = slice(k * cw, (k + 1) * cw)
                r_scr[:, sl] = rowsum(v_ref[0, :, sl], rlo, rhi) / cnt_r
        else:
            r_scr[...] = v_ref[0] / cnt_c
        a_scr[...] = r_scr[...]
        for j in range(-rlo, rhi + 1):
            if j == 0:
                continue
            c0, c1 = max(0, -j), min(GRID_W, GRID_W - j)
            a_scr[:, c0 * LANE:c1 * LANE] += r_scr[:, (c0 + j) * LANE:(c1 + j) * LANE]
        if not transpose:
            o_ref[0] = a_scr[...] / cnt_c - v_ref[0]
        else:
            for k in range(nchunk):
                sl = slice(k * cw, (k + 1) * cw)
                o_ref[0, :, sl] = rowsum(a_scr[:, sl] / cnt_r, rlo, rhi) - v_ref[0, :, sl]

    def body(v_ref, o_ref, r_scr, a_scr):
        grp = pl.program_id(0)
        for k, win in enumerate(POOL_WINDOWS):
            @pl.when(grp == k)
            def _(win=win):
                one(v_ref, o_ref, r_scr, a_scr, win)

    spec = pl.BlockSpec((1, rows, n), lambda k: (k, 0, 0))
    return _pcall(
        body, name=name, grid=(ngrp,), out_shape=jax.ShapeDtypeStruct(v.shape, F32),
        in_specs=[spec], out_specs=spec,
        scratch_shapes=[pltpu.VMEM((rows, n), F32), pltpu.VMEM((rows, n), F32)],
        compiler_params=_params(("arbitrary",)),
    )(v)


def _cmul(ar, ai, br, bi):
    return ar * br - ai * bi, ar * bi + ai * br


def _s5_prep(a_re, a_im, log_dt, c_re, c_im, kvec):
    q, nc, p = c_re.shape

    def body(ar_ref, ai_ref, ldt_ref, cr_ref, ci_ref, k_ref, cpr_ref, cpi_ref, tr_ref, ti_ref):
        lr, li = ar_ref[...], ai_ref[...]
        dt = jnp.exp(ldt_ref[...])
        kk = k_ref[...]
        mag = jnp.exp(kk * (lr * dt))
        ph = kk * (li * dt)
        tr_ref[...] = mag * jnp.cos(ph)
        ti_ref[...] = mag * jnp.sin(ph)
        m1 = jnp.exp(lr * dt)
        abr, abi = m1 * jnp.cos(li * dt), m1 * jnp.sin(li * dt)
        den = lr * lr + li * li
        xr, xi = abr - 1.0, abi
        cfr, cfi = (xr * lr + xi * li) / den, (xi * lr - xr * li) / den
        pr, pi_ = _cmul(cr_ref[...], ci_ref[...], cfr, cfi)
        cpr_ref[...] = pr
        cpi_ref[...] = pi_

    return _pcall(
        body, name="s5_prep",
        out_shape=(jax.ShapeDtypeStruct((q, nc, p), F32), jax.ShapeDtypeStruct((q, nc, p), F32),
                   jax.ShapeDtypeStruct((q, NTAB, p), F32), jax.ShapeDtypeStruct((q, NTAB, p), F32)),
        compiler_params=_params(None),
    )(a_re, a_im, log_dt, c_re, c_im, kvec)


def _s5_param_bwd(a_re, a_im, log_dt, c_re, c_im, dab_r, dab_i, dcp_r, dcp_i):
    q, nc, p = c_re.shape

    def body(ar_ref, ai_ref, ldt_ref, cr_ref, ci_ref, dar_ref, dai_ref, dcr_ref, dci_ref,
             gar_ref, gai_ref, gdt_ref, gcr_ref, gci_ref):
        lr, li = ar_ref[...], ai_ref[...]
        dt = jnp.exp(ldt_ref[...])
        m1 = jnp.exp(lr * dt)
        abr, abi = m1 * jnp.cos(li * dt), m1 * jnp.sin(li * dt)
        den = lr * lr + li * li
        xr, xi = abr - 1.0, abi
        cfr, cfi = (xr * lr + xi * li) / den, (xi * lr - xr * li) / den
        cr, ci = cr_ref[...], ci_ref[...]
        dcr, dci = dcr_ref[...], dci_ref[...]
        gcr, gci = _cmul(dcr, dci, cfr, -cfi)
        gcr_ref[...] = gcr
        gci_ref[...] = gci
        t_r, t_i = _cmul(cr, -ci, dcr, dci)
        dcf_r = jnp.sum(t_r, axis=1, keepdims=True)
        dcf_i = jnp.sum(t_i, axis=1, keepdims=True)
        ilr, ili = lr / den, -li / den
        u_r, u_i = _cmul(dcf_r, dcf_i, ilr, -ili)
        dab_r_, dab_i_ = dar_ref[...] + u_r, dai_ref[...] + u_i
        v_r, v_i = _cmul(dab_r_, dab_i_, abr, -abi)
        cl_r, cl_i = _cmul(cfr, cfi, ilr, ili)
        w_r, w_i = _cmul(dcf_r, dcf_i, cl_r, -cl_i)
        gar_ref[...] = v_r * dt - w_r
        gai_ref[...] = v_i * dt - w_i
        la_r, la_i = _cmul(lr, li, abr, abi)
        ddt = la_r * dab_r_ + la_i * dab_i_
        gdt_ref[...] = dt * jnp.sum(ddt, axis=2, keepdims=True)

    return _pcall(
        body, name="s5_param_bwd",
        out_shape=(jax.ShapeDtypeStruct((q, 1, p), F32), jax.ShapeDtypeStruct((q, 1, p), F32),
                   jax.ShapeDtypeStruct((q, 1, p), F32), jax.ShapeDtypeStruct((q, nc, p), F32),
                   jax.ShapeDtypeStruct((q, nc, p), F32)),
        compiler_params=_params(None),
    )(a_re, a_im, log_dt, c_re, c_im, dab_r, dab_i, dcp_r, dcp_i)


def _bcast8(row, c):
    return jnp.broadcast_to(row, (8, c))


def _slab(j):
    return j * 8 if isinstance(j, int) else pl.multiple_of(j * 8, 8)


def _scan_head(xr, xi, tre_ref, tim_ref, car_r, car_i, desc, conj, cs):
    sg = -1.0 if conj else 1.0
    ar = _bcast8(tre_ref[0:1, :], cs)
    ai = sg * _bcast8(tim_ref[0:1, :], cs)

    def p1(jj, carry):
        hr, hi = carry
        off = _slab(SEG - 1 - jj if desc else jj)
        pr, pi_ = _cmul(ar, ai, hr, hi)
        nr = pr + xr[pl.ds(off, 8), :]
        ni = pi_ + xi[pl.ds(off, 8), :]
        xr[pl.ds(off, 8), :] = nr
        xi[pl.ds(off, 8), :] = ni
        return nr, ni

    zero = jnp.zeros((8, cs), F32)
    ir, ii = lax.fori_loop(0, SEG, p1, (zero, zero), unroll=4)
    cin_r, cin_i = car_r[...], car_i[...]
    rows = lax.broadcasted_iota(jnp.int32, (8, cs), 0)
    for s, krow in ((1, SEG), (2, SEG + 1), (4, SEG + 3)):
        keep, sh = (rows < 8 - s, 8 - s) if desc else (rows >= s, s)
        sr = jnp.where(keep, pltpu.roll(ir, sh, 0), 0.0)
        si = jnp.where(keep, pltpu.roll(ii, sh, 0), 0.0)
        pr, pi_ = _cmul(tre_ref[krow:krow + 1, :], sg * tim_ref[krow:krow + 1, :], sr, si)
        ir, ii = ir + pr, ii + pi_
    q0 = SEG + 8 if desc else SEG
    pr, pi_ = _cmul(tre_ref[q0:q0 + 8, :], sg * tim_ref[q0:q0 + 8, :], cin_r, cin_i)
    fr, fi = ir + pr, ii + pi_
    keep, sh, edge = (rows < 7, 7, 0) if desc else (rows >= 1, 1, 7)
    cs_r = jnp.where(keep, pltpu.roll(fr, sh, 0), cin_r)
    cs_i = jnp.where(keep, pltpu.roll(fi, sh, 0), cin_i)
    car_r[...] = _bcast8(fr[edge:edge + 1, :], cs)
    car_i[...] = _bcast8(fi[edge:edge + 1, :], cs)
    return cs_r, cs_i, cin_r, cin_i


def _pow_row(tre_ref, tim_ref, j, desc, conj, cs):
    k = SEG - 1 - j if desc else j
    sg = -1.0 if conj else 1.0
    return _bcast8(tre_ref[pl.ds(k, 1), :], cs), sg * _bcast8(tim_ref[pl.ds(k, 1), :], cs)


def _s5_fwd(up, bre, bim, cmr, cmi, tre, tim, dskip, order, desc, with_skip, name):
    t, cu = up.shape
    nsb = bre.shape[0]
    cb = cu // nsb
    cs = bre.shape[2]
    nch = t // TC

    def body(u_ref, bre_ref, bim_ref, cmr_ref, cmi_ref, tre_ref, tim_ref, dsk_ref,
             y_ref, hr_ref, hi_ref, cinr_ref, cini_ref, car_r, car_i):
        i = pl.program_id(1)

        @pl.when(i == 0)
        def _():
            car_r[...] = jnp.zeros_like(car_r)
            car_i[...] = jnp.zeros_like(car_i)

        u = u_ref[...]
        ub = u.astype(BF16)
        hr_ref[...] = _dot(ub, bre_ref[0])
        hi_ref[...] = _dot(ub, bim_ref[0])
        cs_r, cs_i, cin_r, cin_i = _scan_head(hr_ref, hi_ref, tre_ref, tim_ref, car_r, car_i, desc, False, cs)
        cinr_ref[...] = cin_r
        cini_ref[...] = cin_i

        def p2(j, _):
            off = _slab(j)
            pw_r, pw_i = _pow_row(tre_ref, tim_ref, j, desc, False, cs)
            pr, pi_ = _cmul(pw_r, pw_i, cs_r, cs_i)
            hr_ref[pl.ds(off, 8), :] = hr_ref[pl.ds(off, 8), :] + pr
            hi_ref[pl.ds(off, 8), :] = hi_ref[pl.ds(off, 8), :] + pi_
            return 0

        lax.fori_loop(0, SEG, p2, 0, unroll=4)
        y = _dot(hr_ref[...].astype(BF16), cmr_ref[0]) - _dot(hi_ref[...].astype(BF16), cmi_ref[0])
        if with_skip:
            y = y + dsk_ref[...] * u
        y_ref[...] = y

    blk = lambda sb, i: (order(i), sb)
    mat = lambda sb, i: (sb, 0, 0)
    return _pcall(
        body, name=name, grid=(nsb, nch),
        out_shape=(jax.ShapeDtypeStruct((t, cu), F32), jax.ShapeDtypeStruct((t, nsb * cs), F32),
                   jax.ShapeDtypeStruct((t, nsb * cs), F32), jax.ShapeDtypeStruct((nch * 8, nsb * cs), F32),
                   jax.ShapeDtypeStruct((nch * 8, nsb * cs), F32)),
        in_specs=[pl.BlockSpec((TC, cb), blk), pl.BlockSpec((1, cb, cs), mat), pl.BlockSpec((1, cb, cs), mat),
                  pl.BlockSpec((1, cs, cb), mat), pl.BlockSpec((1, cs, cb), mat),
                  pl.BlockSpec((NTAB, cs), lambda sb, i: (0, sb)), pl.BlockSpec((NTAB, cs), lambda sb, i: (0, sb)),
                  pl.BlockSpec((1, cb), lambda sb, i: (0, sb))],
        out_specs=(pl.BlockSpec((TC, cb), blk), pl.BlockSpec((TC, cs), blk), pl.BlockSpec((TC, cs), blk),
                   pl.BlockSpec((8, cs), blk), pl.BlockSpec((8, cs), blk)),
        scratch_shapes=[pltpu.VMEM((8, cs), F32), pltpu.VMEM((8, cs), F32)],
        compiler_params=_params(("arbitrary", "arbitrary")),
    )(up, bre, bim, cmr, cmi, tre, tim, dskip)


def _s5_bwd(dyp, up, hr, hi, cinr, cini, bre, bim, ctr, cti, tre, tim, dskip, order, desc, with_skip, name, xchg=None):
    t, cu = up.shape
    nsb = bre.shape[0]
    cb = cu // nsb
    cs = bre.shape[2]
    nch = t // TC
    adesc = not desc

    def body(dy_ref, u_ref, hr_ref, hi_ref, cinr_ref, cini_ref, bre_ref, bim_ref, ctr_ref, cti_ref, tre_ref, tim_ref,
             dsk_ref, du_ref, dar_ref, dai_ref, dcr_ref, dci_ref, dbr_ref, dbi_ref, dd_ref,
             car_r, car_i, mr, mi, acc_r, acc_i):
        i = pl.program_id(1)

        @pl.when(i == 0)
        def _():
            for ref in (car_r, car_i, acc_r, acc_i, dcr_ref, dci_ref, dbr_ref, dbi_ref, dd_ref):
                ref[...] = jnp.zeros_like(ref)

        dy = dy_ref[...]
        dyb = dy.astype(BF16)
        u = u_ref[...]
        ub = u.astype(BF16)
        mr[...] = _dot(dyb, ctr_ref[0])
        mi[...] = -_dot(dyb, cti_ref[0])
        cs_r, cs_i, _, _ = _scan_head(mr, mi, tre_ref, tim_ref, car_r, car_i, adesc, True, cs)
        rows = lax.broadcasted_iota(jnp.int32, (8, cs), 0)

        def fix(j, hp_r, hp_i, acc):
            off = _slab(j)
            pw_r, pw_i = _pow_row(tre_ref, tim_ref, j, adesc, True, cs)
            pr, pi_ = _cmul(pw_r, pw_i, cs_r, cs_i)
            m_r = mr[pl.ds(off, 8), :] + pr
            m_i = mi[pl.ds(off, 8), :] + pi_
            mr[pl.ds(off, 8), :] = m_r
            mi[pl.ds(off, 8), :] = m_i
            a_r, a_i = acc
            return a_r + hp_r * m_r + hp_i * m_i, a_i + hp_r * m_i - hp_i * m_r

        edge_j, src, keep, sh = (SEG - 1, 0, rows < 7, 7) if desc else (0, (SEG - 1) * 8, rows >= 1, 1)
        h0r = jnp.where(keep, pltpu.roll(hr_ref[src:src + 8, :], sh, 0), cinr_ref[...])
        h0i = jnp.where(keep, pltpu.roll(hi_ref[src:src + 8, :], sh, 0), cini_ref[...])
        acc = fix(edge_j, h0r, h0i, (acc_r[...], acc_i[...]))

        def p2(jj, acc):
            j = jj if desc else jj + 1
            offp = _slab(j + 1 if desc else j - 1)
            return fix(j, hr_ref[pl.ds(offp, 8), :], hi_ref[pl.ds(offp, 8), :], acc)

        a_r, a_i = lax.fori_loop(0, SEG - 1, p2, acc, unroll=4)
        acc_r[...] = a_r
        acc_i[...] = a_i

        mrb, mib = mr[...].astype(BF16), mi[...].astype(BF16)
        du = _dot_nt(mrb, bre_ref[0]) + _dot_nt(mib, bim_ref[0])
        if with_skip:
            du = du + dsk_ref[...] * dy
            dd_ref[...] += _colsum(dy * u)
        du_ref[...] = du
        dbr_ref[0] += _dot_tn(ub, mrb)
        dbi_ref[0] += _dot_tn(ub, mib)
        dcr_ref[0] += _dot_tn(dyb, hr_ref[...].astype(BF16))
        dci_ref[0] -= _dot_tn(dyb, hi_ref[...].astype(BF16))

        @pl.when(i == nch - 1)
        def _():
            dar_ref[0] = _colsum(a_r)
            dai_ref[0] = _colsum(a_i)

    blk = lambda sb, i: (order(i), sb)
    mat = lambda sb, i: (sb, 0, 0)
    tab = pl.BlockSpec((NTAB, cs), lambda sb, i: (0, sb))
    vec = pl.BlockSpec((1, cb), lambda sb, i: (0, sb))
    return _pcall(
        body, name=name, grid=(nsb, nch), xchg=xchg, edges=_edges2(nsb, nch),
        out_shape=(jax.ShapeDtypeStruct((t, cu), F32),
                   jax.ShapeDtypeStruct((nsb, 1, cs), F32), jax.ShapeDtypeStruct((nsb, 1, cs), F32),
                   jax.ShapeDtypeStruct((nsb, cb, cs), F32), jax.ShapeDtypeStruct((nsb, cb, cs), F32),
                   jax.ShapeDtypeStruct((nsb, cb, cs), F32), jax.ShapeDtypeStruct((nsb, cb, cs), F32),
                   jax.ShapeDtypeStruct((1, cu), F32)),
        in_specs=[pl.BlockSpec((TC, cb), blk), pl.BlockSpec((TC, cb), blk), pl.BlockSpec((TC, cs), blk),
                  pl.BlockSpec((TC, cs), blk), pl.BlockSpec((8, cs), blk), pl.BlockSpec((8, cs), blk),
                  pl.BlockSpec((1, cb, cs), mat), pl.BlockSpec((1, cb, cs), mat), pl.BlockSpec((1, cb, cs), mat),
                  pl.BlockSpec((1, cb, cs), mat), tab, tab, vec],
        out_specs=(pl.BlockSpec((TC, cb), blk), pl.BlockSpec((1, 1, cs), mat), pl.BlockSpec((1, 1, cs), mat),
                   pl.BlockSpec((1, cb, cs), mat), pl.BlockSpec((1, cb, cs), mat), pl.BlockSpec((1, cb, cs), mat),
                   pl.BlockSpec((1, cb, cs), mat), vec),
        scratch_shapes=[pltpu.VMEM((8, cs), F32), pltpu.VMEM((8, cs), F32), pltpu.VMEM((TC, cs), F32),
                        pltpu.VMEM((TC, cs), F32), pltpu.VMEM((8, cs), F32), pltpu.VMEM((8, cs), F32)],
        compiler_params=_params(("arbitrary", "arbitrary")),
    )(dyp, up, hr, hi, cinr, cini, bre, bim, ctr, cti, tre, tim, dskip)


def _gelu(x):
    k = math.sqrt(2.0 / math.pi)
    return 0.5 * x * (1.0 + jnp.tanh(k * (x + 0.044715 * (x * x * x))))


def _gelu_grad(x):
    k = math.sqrt(2.0 / math.pi)
    th = jnp.tanh(k * (x + 0.044715 * (x * x * x)))
    return 0.5 * (1.0 + th) + 0.5 * x * (1.0 - th * th) * (k * (1.0 + 3.0 * 0.044715 * (x * x)))


def _mix_out_fwd(yf, yb, dpool, x1, modtab, w_pool, pscale, w_glu, w_out):
    l, hm = dpool.shape
    d = x1.shape[1]

    def body(yf_ref, yb_ref, dp_ref, x_ref, mod_ref, wp_ref, ps_ref, wg_ref, wo_ref, x2_ref, yp_ref, cat_ref, mix_ref):
        ypre = yf_ref[...] + yb_ref[...]
        yp_ref[...] = ypre
        yg = _gelu(ypre)
        y2 = yg * _sigmoid(_dot(yg.astype(BF16), wg_ref[...]))
        po = _dot(dp_ref[...].astype(BF16), wp_ref[...]) * ps_ref[...]
        cat = jnp.concatenate([po, y2], axis=1).astype(BF16)
        cat_ref[...] = cat
        mix = _dot(cat, wo_ref[...])
        mix_ref[...] = mix.astype(BF16)
        x2_ref[...] = x_ref[...] + mod_ref[0, 5:6, :] * mix

    row = lambda i: (i, 0)
    lrow = lambda i: (i + 1, 0)
    return _pcall(
        body, name="mix_out_fwd", grid=(l // TM,),
        out_shape=(jax.ShapeDtypeStruct((l, d), F32), jax.ShapeDtypeStruct((l, hm), F32),
                   jax.ShapeDtypeStruct((l, d), BF16), jax.ShapeDtypeStruct((l, d), BF16)),
        in_specs=[pl.BlockSpec((TM, hm), lrow), pl.BlockSpec((TM, hm), lrow), pl.BlockSpec((TM, hm), row),
                  pl.BlockSpec((TM, d), lrow), _mod_spec(d, False), _const(w_pool.shape), _const((1, hm)),
                  _const(w_glu.shape), _resident(w_out.shape)],
        out_specs=(pl.BlockSpec((TM, d), row), pl.BlockSpec((TM, hm), row), pl.BlockSpec((TM, d), row),
                   pl.BlockSpec((TM, d), row)),
        compiler_params=_params(("arbitrary",)),
    )(yf, yb, dpool, x1, modtab, w_pool, pscale, w_glu, w_out)


def _mix_out_bwd(dx2, mix, cat, ypre, dpool, modtab, w_pool, pscale, w_glu, w_out):
    l, hm = dpool.shape
    d = dx2.shape[1]
    nt = l // TML

    def body(dx_ref, mix_ref, cat_ref, yp_ref, dp_ref, mod_ref, wp_ref, ps_ref, wg_ref, wo_ref,
             dyp_ref, ddp_ref, dwo_ref, dwg_ref, dwp_ref, dps_ref, dgate_ref, acc_o, acc_g):
        i = pl.program_id(0)

        @pl.when(i == 0)
        def _():
            for ref in (acc_o, acc_g, dwp_ref, dps_ref, dgate_ref):
                ref[...] = jnp.zeros_like(ref)

        dx = dx_ref[...]
        dgate_ref[...] += _colsum(dx * mix_ref[...].astype(F32))
        dmix = (dx * mod_ref[0, 5:6, :]).astype(BF16)
        dcat = _dot_nt(dmix, wo_ref[...])
        acc_o[...] += _dot_tn(cat_ref[...], dmix)
        dpo, dy2 = dcat[:, :hm], dcat[:, hm:]
        dpb = dp_ref[...].astype(BF16)
        pp = _dot(dpb, wp_ref[...])
        dps_ref[...] += _colsum(dpo * pp)
        dpp = (dpo * ps_ref[...]).astype(BF16)
        ddp_ref[...] = _dot_nt(dpp, wp_ref[...])
        dwp_ref[...] += _dot_tn(dpb, dpp)
        ypre = yp_ref[...]
        yg = _gelu(ypre)
        ygb = yg.astype(BF16)
        s = _sigmoid(_dot(ygb, wg_ref[...]))
        dq = (dy2 * yg * s * (1.0 - s)).astype(BF16)
        dyg = dy2 * s + _dot_nt(dq, wg_ref[...])
        acc_g[...] += _dot_tn(ygb, dq)
        dyp_ref[...] = dyg * _gelu_grad(ypre)

        @pl.when(i == nt - 1)
        def _():
            dwo_ref[...] = acc_o[...].astype(BF16)
            dwg_ref[...] = acc_g[...].astype(BF16)

    row = lambda i: (i, 0)
    return _pcall(
        body, name="mix_out_bwd", grid=(nt,),
        out_shape=(jax.ShapeDtypeStruct((l, hm), F32), jax.ShapeDtypeStruct((l, hm), F32),
                   jax.ShapeDtypeStruct((d, d), BF16), jax.ShapeDtypeStruct((hm, hm), BF16),
                   jax.ShapeDtypeStruct((hm, hm), F32), jax.ShapeDtypeStruct((1, hm), F32),
                   jax.ShapeDtypeStruct((1, d), F32)),
        in_specs=[pl.BlockSpec((TML, d), row), pl.BlockSpec((TML, d), row), pl.BlockSpec((TML, d), row),
                  pl.BlockSpec((TML, hm), row), pl.BlockSpec((TML, hm), row), _mod_spec(d, False),
                  _const(w_pool.shape), _const((1, hm)), _const(w_glu.shape), _resident(w_out.shape)],
        out_specs=(pl.BlockSpec((TML, hm), row), pl.BlockSpec((TML, hm), row), _const((d, d)), _const((hm, hm)),
                   _const((hm, hm)), _const((1, hm)), _const((1, d))),
        scratch_shapes=[pltpu.VMEM((d, d), F32), pltpu.VMEM((hm, hm), F32)],
        compiler_params=_params(("arbitrary",)),
    )(dx2, mix, cat, ypre, dpool, modtab, w_pool, pscale, w_glu, w_out)


def _loss_head(x3, target, g):
    l, d = x3.shape

    def body(x_ref, t_ref, g_ref, dx_ref, loss_ref, dg_ref):
        i = pl.program_id(0)

        @pl.when(i == 0)
        def _():
            loss_ref[...] = jnp.zeros_like(loss_ref)
            dg_ref[...] = jnp.zeros_like(dg_ref)

        xx = x_ref[...]
        gg = g_ref[...]
        rstd = lax.rsqrt(jnp.mean(xx * xx, axis=-1, keepdims=True) + EPS)
        xhat = xx * rstd
        err = xhat * gg - t_ref[...]
        row_loss = jnp.mean(err * err, axis=-1, keepdims=True)
        loss_ref[...] += 0.5 * jnp.sum(row_loss, axis=0, keepdims=True)
        dout = err * (1.0 / d)
        dg_ref[...] += _colsum(dout * xhat)
        dxhat = dout * gg
        dx_ref[...] = rstd * (dxhat - xhat * jnp.mean(dxhat * xhat, axis=-1, keepdims=True))

    row = lambda i: (i, 0)
    return _pcall(
        body, name="loss_head", grid=(l // TML,),
        out_shape=(jax.ShapeDtypeStruct((l, d), F32), jax.ShapeDtypeStruct((1, LANE), F32),
                   jax.ShapeDtypeStruct((1, d), F32)),
        in_specs=[pl.BlockSpec((TML, d), row), pl.BlockSpec((TML, d), row), _const((1, d))],
        out_specs=(pl.BlockSpec((TML, d), row), _const((1, LANE)), _const((1, d))),
        compiler_params=_params(("arbitrary",)),
    )(x3, target, g)


def _adamw(parts, w, m, v, name):
    shape = w.shape
    c = shape[-1]
    r = int(np.prod(shape)) // c
    npart = parts.shape[0]
    tr = r
    for cand in (1024, 512, 256, 128, 64, 32, 16):
        if r % cand == 0 and cand * max(c, LANE) * 4 <= ADAM_BLOCK_BYTES:
            tr = cand
            break
    c1 = 1.0 - ADAM_B1
    c2 = 1.0 - ADAM_B2
    bc1 = 1.0 - ADAM_B1 ** ADAM_STEP
    bc2 = 1.0 - ADAM_B2 ** ADAM_STEP

    def body(p_ref, w_ref, m_ref, v_ref, g_ref, d_ref, mo_ref, vo_ref):
        g = p_ref[0].astype(F32)
        for k in range(1, npart):
            g = g + p_ref[k].astype(F32)
        mn = ADAM_B1 * m_ref[...] + c1 * g
        vn = ADAM_B2 * v_ref[...] + c2 * (g * g)
        m_hat = mn / bc1
        v_hat = vn / bc2
        g_ref[...] = g
        mo_ref[...] = mn
        vo_ref[...] = vn
        d_ref[...] = -ADAM_LR * (m_hat / (jnp.sqrt(v_hat) + ADAM_EPS) + ADAM_WD * w_ref[...])

    spec = pl.BlockSpec((tr, c), lambda i: (i, 0))
    outs = _pcall(
        body, name=name, grid=(r // tr,),
        out_shape=tuple(jax.ShapeDtypeStruct((r, c), F32) for _ in range(4)),
        in_specs=[pl.BlockSpec((npart, tr, c), lambda i: (0, i, 0)), spec, spec, spec],
        out_specs=(spec, spec, spec, spec),
        compiler_params=_params(("arbitrary",)),
    )(parts.reshape(npart, r, c), w.reshape(r, c), m.reshape(r, c), v.reshape(r, c))
    return tuple(o.reshape(shape) for o in outs)


def _perm(a):
    t, c = a.shape
    return a.reshape(t // TC, 8, SEG, c).transpose(0, 2, 1, 3).reshape(t, c)


def _unperm(a):
    t, c = a.shape
    return a.reshape(t // TC, SEG, 8, c).transpose(0, 2, 1, 3).reshape(t, c)


def _blockdiag(xb):
    n, a, b = xb.shape[-3:]
    eye = jnp.eye(n, dtype=xb.dtype)
    out = xb[..., :, :, None, :] * eye[:, None, :, None]
    return out.reshape(xb.shape[:-3] + (n * a, n * b))


def _diagblocks(mat, n):
    lead = mat.shape[:-2]
    a, b = mat.shape[-2] // n, mat.shape[-1] // n
    m5 = mat.reshape(lead + (n, a, n, b))
    nl = len(lead)
    dg = jnp.diagonal(m5, axis1=nl, axis2=nl + 2)
    return jnp.moveaxis(dg, -1, nl)


def _to_grid(a, rows):
    return a.reshape(rows, GRID_W, 4, LANE).transpose(2, 0, 1, 3).reshape(4, rows, GRID_W * LANE)


def _from_grid(a, rows):
    return a.reshape(4, rows, GRID_W, LANE).transpose(1, 2, 0, 3).reshape(rows * GRID_W, 4 * LANE)


def _pad128(a):
    flat = a.reshape(-1)
    pad = (-flat.shape[0]) % LANE
    return jnp.pad(flat, (0, pad)) if pad else flat


def kernel(x, c, ctx, c_ctx, norm_g, w_ada, b_ada, ffn_w1, ffn_w3, ffn_w2, w_in, pool_w, pool_scale, s5_a_re, s5_a_im, s5_log_dt, s5_b_re, s5_b_im, s5_c_re, s5_c_im, s5_d, s5_w_glu, w_out, final_g, loss_target, m_c_ctx, m_norm_g, m_w_ada, m_b_ada, m_ffn_w1, m_ffn_w3, m_ffn_w2, m_w_in, m_pool_w, m_pool_scale, m_s5_a_re, m_s5_a_im, m_s5_log_dt, m_s5_b_re, m_s5_b_im, m_s5_c_re, m_s5_c_im, m_s5_d, m_s5_w_glu, m_w_out, m_final_g, v_c_ctx, v_norm_g, v_w_ada, v_b_ada, v_ffn_w1, v_ffn_w3, v_ffn_w2, v_w_in, v_pool_w, v_pool_scale, v_s5_a_re, v_s5_a_im, v_s5_log_dt, v_s5_b_re, v_s5_b_im, v_s5_c_re, v_s5_c_im, v_s5_d, v_s5_w_glu, v_w_out, v_final_g):
    weights = dict(c_ctx=c_ctx, norm_g=norm_g, w_ada=w_ada, b_ada=b_ada, ffn_w1=ffn_w1, ffn_w3=ffn_w3, ffn_w2=ffn_w2,
                   w_in=w_in, pool_w=pool_w, pool_scale=pool_scale, s5_a_re=s5_a_re, s5_a_im=s5_a_im,
                   s5_log_dt=s5_log_dt, s5_b_re=s5_b_re, s5_b_im=s5_b_im, s5_c_re=s5_c_re, s5_c_im=s5_c_im, s5_d=s5_d,
                   s5_w_glu=s5_w_glu, w_out=w_out, final_g=final_g)
    mom_m = dict(c_ctx=m_c_ctx, norm_g=m_norm_g, w_ada=m_w_ada, b_ada=m_b_ada, ffn_w1=m_ffn_w1, ffn_w3=m_ffn_w3,
                 ffn_w2=m_ffn_w2, w_in=m_w_in, pool_w=m_pool_w, pool_scale=m_pool_scale, s5_a_re=m_s5_a_re,
                 s5_a_im=m_s5_a_im, s5_log_dt=m_s5_log_dt, s5_b_re=m_s5_b_re, s5_b_im=m_s5_b_im, s5_c_re=m_s5_c_re,
                 s5_c_im=m_s5_c_im, s5_d=m_s5_d, s5_w_glu=m_s5_w_glu, w_out=m_w_out, final_g=m_final_g)
    mom_v = dict(c_ctx=v_c_ctx, norm_g=v_norm_g, w_ada=v_w_ada, b_ada=v_b_ada, ffn_w1=v_ffn_w1, ffn_w3=v_ffn_w3,
                 ffn_w2=v_ffn_w2, w_in=v_w_in, pool_w=v_pool_w, pool_scale=v_pool_scale, s5_a_re=v_s5_a_re,
                 s5_a_im=v_s5_a_im, s5_log_dt=v_s5_log_dt, s5_b_re=v_s5_b_re, s5_b_im=v_s5_b_im, s5_c_re=v_s5_c_re,
                 s5_c_im=v_s5_c_im, s5_d=v_s5_d, s5_w_glu=v_s5_w_glu, w_out=v_w_out, final_g=v_final_g)

    l, d = x.shape[1], x.shape[2]
    lc = ctx.shape[1]
    t = l + lc
    rows = l // GRID_W
    fblk = ffn_w1.shape[-1]
    ngrp, gp = s5_a_re.shape[2], s5_a_re.shape[3]
    gc = s5_b_re.shape[3]
    hm = ngrp * gc
    nsb = 4
    gsb = ngrp // nsb
    assert lc == TM == TC and l % TML == 0 and hm == 4 * LANE and ngrp * gp == nsb * 512
    me = 4 * lax.axis_index("x") + 2 * lax.axis_index("y") + lax.axis_index("c")

    padc = ((0, 0), (0, 0), (0, FPAD - fblk))
    w13_loc = [jnp.concatenate([jnp.pad(ffn_w1[0, k], padc[1:]), jnp.pad(ffn_w3[0, k], padc[1:])], axis=-1).astype(BF16)
               for k in range(2)]
    w2_loc = [jnp.pad(ffn_w2[0, k], ((0, FPAD - fblk), (0, 0))).astype(BF16) for k in range(2)]
    gath = _exchange([(w13_loc[0], GATHER_REL), (w2_loc[0], GATHER_REL), (w_in[0].astype(BF16), GATHER),
                      (c, GATHER), (norm_g[0], GATHER)], "exchange_weights")
    w13_0 = gath[0]
    w2_0 = gath[1].reshape(NDEV // 2, 2 * FPAD, d)
    w_in_f = gath[2].reshape(d, -1)
    c_all = gath[3].reshape(NDEV, d)
    g_all = gath[4].transpose(1, 0, 2).reshape(3, d)

    wa = w_ada.shape[-1]
    c16 = jnp.concatenate([c_all, c_ctx.reshape(1, d), jnp.zeros((7, d), F32)], axis=0)
    b_blk = lax.dynamic_slice_in_dim(b_ada, me * wa, wa, axis=1)
    mod_sl = _ada_fwd(c16, w_ada[0], b_blk)
    gath = _exchange([(mod_sl, GATHER)], "exchange_mod")
    mod_full = gath[0].transpose(1, 0, 2).reshape(16, NDEV * wa)
    mod_l = lax.dynamic_index_in_dim(mod_full, me, axis=0, keepdims=False).reshape(9, d)
    modtab = jnp.stack([mod_full[8].reshape(9, d), mod_l])

    x0 = jnp.concatenate([ctx[0], x[0]], axis=0)
    g0, g1, g2 = g_all[0:1], g_all[1:2], g_all[2:3]
    x1, h0, ab0, o0, w13_1, w2_1, w_out_f, w_glu_f = _ffn_fwd(
        x0, modtab, g0, w13_0, w2_0, 0, True, "ffn0_fwd",
        xchg=[(w13_loc[1], GATHER_REL), (w2_loc[1], GATHER_REL), (w_out[0].astype(BF16), GATHER),
              (s5_w_glu[0].astype(BF16), GATHER)])
    w2_1 = w2_1.reshape(NDEV // 2, 2 * FPAD, d)
    w_out_f = w_out_f.reshape(-1, d)
    w_glu_f = w_glu_f.reshape(hm, hm)
    h1, u = _mix_in_fwd(x1, modtab, g1, w_in_f)

    v_grid = _to_grid(u[lc:, :hm], rows)
    dpool = _from_grid(_pool(v_grid, False, "pool_fwd"), rows)
    w_pool_bd = _blockdiag(pool_w[0]).astype(BF16)

    q = 2 * ngrp
    kvec = np.concatenate([np.arange(1, SEG + 1), SEG * np.arange(1, 9), SEG * np.arange(8, 0, -1)]).astype(np.float32)
    a_re3, a_im3 = s5_a_re[0].reshape(q, 1, gp), s5_a_im[0].reshape(q, 1, gp)
    ldt3 = jnp.broadcast_to(s5_log_dt[0].reshape(q, 1, 1), (q, 1, gp))
    c_re3, c_im3 = s5_c_re[0].reshape(q, gc, gp), s5_c_im[0].reshape(q, gc, gp)
    cp_re, cp_im, tab_re, tab_im = _s5_prep(a_re3, a_im3, ldt3, c_re3, c_im3, jnp.broadcast_to(jnp.asarray(kvec).reshape(1, NTAB, 1), (1, NTAB, gp)))
    tabs = [(tab_re.reshape(2, ngrp, NTAB, gp)[k].transpose(1, 0, 2).reshape(NTAB, ngrp * gp),
             tab_im.reshape(2, ngrp, NTAB, gp)[k].transpose(1, 0, 2).reshape(NTAB, ngrp * gp)) for k in range(2)]
    b_t = lambda b: _blockdiag(b[0].reshape(nsb, gsb, gp, gc).transpose(0, 1, 3, 2)).astype(BF16)
    bre, bim = b_t(s5_b_re), b_t(s5_b_im)
    cp_t = lambda cp: _blockdiag(cp.reshape(2, nsb, gsb, gc, gp)).astype(BF16)
    ctr, cti = cp_t(cp_re), cp_t(cp_im)
    cmr, cmi = jnp.swapaxes(ctr, -1, -2), jnp.swapaxes(cti, -1, -2)
    dskip = s5_d

    nch = t // TC
    order_f = lambda i: i
    order_b = lambda i: jnp.where(i == 0, 0, nch - i)
    rorder_f = lambda i: nch - 1 - i
    rorder_b = lambda i: jnp.where(i == nch - 1, 0, i + 1)
    up = _perm(u[:, hm:])
    yp_f, hr_f, hi_f, cinr_f, cini_f = _s5_fwd(up, bre, bim, cmr[0], cmi[0], tabs[0][0], tabs[0][1], dskip,
                                               order_f, False, True, "s5_fwd_f")
    yp_b, hr_b, hi_b, cinr_b, cini_b = _s5_fwd(up, bre, bim, cmr[1], cmi[1], tabs[1][0], tabs[1][1], dskip,
                                               order_b, True, False, "s5_fwd_b")
    y_f, y_b = _unperm(yp_f), _unperm(yp_b)

    x2, ypre, cat, mix = _mix_out_fwd(y_f, y_b, dpool, x1, modtab, w_pool_bd, pool_scale, w_glu_f, w_out_f)
    x3, h2, ab2, o2 = _ffn_fwd(x2, modtab, g2, w13_1, w2_1, 6, False, "ffn1_fwd")
    dx3, loss_part, dfinal_g = _loss_head(x3, loss_target[0], final_g.reshape(1, d))
    loss = lax.psum(loss_part[0, 0], AXES)

    dhp2, d13_1, d2_1 = _ffn_bwd(dx3, modtab, h2, ab2, w13_1, w2_1, 6, False, "ffn1_bwd")
    dx2, dmod_678, dg2 = _ffn_bwd_norm(dx3, [dhp2], x2, o2, modtab, g2, 6, False, "ffn1_bwd_norm")
    dypre, ddpool, dw_out, dw_glu, dw_pool_bd, dpscale, dgate5 = _mix_out_bwd(
        dx2, mix, cat, ypre, dpool, modtab, w_pool_bd, pool_scale, w_glu_f, w_out_f)

    dyp = _perm(jnp.concatenate([jnp.zeros((lc, hm), F32), dypre], axis=0))
    dup_f, dar_f, dai_f, dcr_f, dci_f, dbr_f, dbi_f, dd_skip, r13_1, r2_1 = _s5_bwd(
        dyp, up, hr_f, hi_f, cinr_f, cini_f, bre, bim, ctr[0], cti[0], tabs[0][0], tabs[0][1], dskip,
        rorder_f, False, True, "s5_bwd_f", xchg=[(d13_1, A2A_REL), (d2_1.reshape(NDEV, FPAD, d), A2A_REL)])
    dup_b, dar_b, dai_b, dcr_b, dci_b, dbr_b, dbi_b, _, r_out, r_glu = _s5_bwd(
        dyp, up, hr_b, hi_b, cinr_b, cini_b, bre, bim, ctr[1], cti[1], tabs[1][0], tabs[1][1], dskip,
        rorder_b, True, False, "s5_bwd_b",
        xchg=[(dw_out.reshape(NDEV, -1, d), A2A), (dw_glu.reshape(NDEV, hm // NDEV, hm), A2A)])
    du_f, du_b = _unperm(dup_f), _unperm(dup_b)

    dab_r = jnp.stack([dar_f, dar_b]).reshape(q, 1, gp)
    dab_i = jnp.stack([dai_f, dai_b]).reshape(q, 1, gp)
    dcp_r = _diagblocks(jnp.stack([dcr_f, dcr_b]), gsb).reshape(q, gc, gp)
    dcp_i = _diagblocks(jnp.stack([dci_f, dci_b]), gsb).reshape(q, gc, gp)
    ga_re, ga_im, gldt, gc_re, gc_im = _s5_param_bwd(a_re3, a_im3, ldt3, c_re3, c_im3, dab_r, dab_i, dcp_r, dcp_i)
    gb_re = (_diagblocks(dbr_f, gsb) + _diagblocks(dbr_b, gsb)).transpose(0, 1, 3, 2).reshape(ngrp, gp, gc)
    gb_im = (_diagblocks(dbi_f, gsb) + _diagblocks(dbi_b, gsb)).transpose(0, 1, 3, 2).reshape(ngrp, gp, gc)

    def pack(small):
        offs, pieces, off = {}, [], 0
        for k_, a_ in small.items():
            p_ = _pad128(a_.astype(F32))
            offs[k_] = (off, int(np.prod(a_.shape)))
            off += p_.shape[0]
            pieces.append(p_)
        return jnp.concatenate(pieces).reshape(1, off), offs

    gw_pool = _diagblocks(dw_pool_bd, 4)
    bundle_a, offs = pack(dict(pool_w=gw_pool, pool_scale=dpscale, s5_a_re=ga_re, s5_a_im=ga_im,
                               s5_log_dt=gldt[:, 0, 0], s5_b_re=gb_re, s5_b_im=gb_im, s5_c_re=gc_re, s5_c_im=gc_im,
                               s5_d=dd_skip, final_g=dfinal_g))

    du_pool = _from_grid(_pool(_to_grid(ddpool, rows), True, "pool_bwd"), rows)
    dx1, dw_in, dmod_34, dg1 = _mix_in_bwd(du_pool, du_f, du_b, dx2, h1, x1, modtab, g1, w_in_f)
    dhp0a, d13_0a, d2_0a, bund_a, r_in = _ffn_bwd(
        dx1, modtab, h0, ab0, w13_0, w2_0, 0, True, "ffn0_bwd_a", pair0=0, npair=2,
        xchg=[(bundle_a, GATHER), (dw_in.reshape(NDEV, d // NDEV, -1), A2A)])
    dhp0b, d13_0b, d2_0b, r13_0a, r2_0a = _ffn_bwd(
        dx1, modtab, h0, ab0, w13_0, w2_0, 0, True, "ffn0_bwd_b", pair0=2, npair=2,
        xchg=[(d13_0a, A2A_SAME), (d2_0a.reshape(NDEV // 2, FPAD, d), A2A_SAME)])
    dx0, dmod_012, dg0, r13_0b, r2_0b = _ffn_bwd_norm(
        dx1, [dhp0a, dhp0b], x0, o0, modtab, g0, 0, True, "ffn0_bwd_norm",
        xchg=[(d13_0b, A2A_OTHER), (d2_0b.reshape(NDEV // 2, FPAD, d), A2A_OTHER)])
    r13_0 = jnp.concatenate([r13_0a, r13_0b], axis=0)
    r2_0 = jnp.concatenate([r2_0a, r2_0b], axis=0)
    grad_x = dx0[lc:].reshape(1, l, d)

    dmod_c = jnp.concatenate([dmod_012[0], dmod_34[0], jnp.zeros((4, d), F32)], axis=0)
    dmod_l = jnp.concatenate([dmod_012[1], dmod_34[1], dgate5, dmod_678[0]], axis=0)
    bundle_b, offs_b = pack(dict(norm_g=jnp.concatenate([dg0, dg1, dg2], axis=0), dmod_l=dmod_l, dmod_c=dmod_c))
    bund_b = _exchange([(bundle_b, GATHER)], "exchange_small")[0]
    bunds = {**{k_: (bund_a.reshape(NDEV, -1), v_) for k_, v_ in offs.items()},
             **{k_: (bund_b.reshape(NDEV, -1), v_) for k_, v_ in offs_b.items()}}
    r13, r2 = (r13_0, r13_1), (r2_0, r2_1)

    def piece(name):
        b_, (o_, n_) = bunds[name]
        return b_[:, o_:o_ + n_]

    dl_all = lax.dynamic_slice_in_dim(piece("dmod_l"), me * wa, wa, axis=1)
    dc_all = lax.dynamic_slice_in_dim(piece("dmod_c"), me * wa, wa, axis=1)
    g_wada, gc_part = _ada_bwd(c16, w_ada[0], dl_all, dc_all)
    gc_all = _exchange([(gc_part, GATHER)], "exchange_cctx")[0]

    parts = {
        "c_ctx": gc_all.reshape(NDEV, d),
        "norm_g": lax.dynamic_slice_in_dim(piece("norm_g").reshape(NDEV, 3, d), me * (d // NDEV), d // NDEV,
                                           axis=2).reshape((NDEV,) + norm_g.shape),
        "w_ada": g_wada.reshape((1,) + w_ada.shape),
        "b_ada": jnp.concatenate([piece("dmod_l"), piece("dmod_c")], axis=0).reshape((2 * NDEV,) + b_ada.shape),
        "ffn_w1": jnp.stack([r13[0][:, :, :fblk], r13[1][:, :, :fblk]], axis=1).reshape((NDEV,) + ffn_w1.shape),
        "ffn_w3": jnp.stack([r13[0][:, :, FPAD:FPAD + fblk], r13[1][:, :, FPAD:FPAD + fblk]],
                            axis=1).reshape((NDEV,) + ffn_w3.shape),
        "ffn_w2": jnp.stack([r2[0][:, :fblk, :], r2[1][:, :fblk, :]], axis=1).reshape((NDEV,) + ffn_w2.shape),
        "w_in": r_in.reshape((NDEV,) + w_in.shape),
        "s5_w_glu": r_glu.reshape((NDEV,) + s5_w_glu.shape),
        "w_out": r_out.reshape((NDEV,) + w_out.shape),
    }
    for k_ in ("pool_w", "pool_scale", "s5_a_re", "s5_a_im", "s5_log_dt", "s5_b_re", "s5_b_im", "s5_c_re", "s5_c_im",
               "s5_d", "final_g"):
        parts[k_] = piece(k_).reshape((NDEV,) + weights[k_].shape)

    grads, deltas, new_m, new_v = [], [], [], []
    for k_ in weights:
        g_, d_, m_, v_ = _adamw(parts[k_], weights[k_], mom_m[k_], mom_v[k_], "adamw_" + k_)
        grads.append(g_)
        deltas.append(d_)
        new_m.append(m_)
        new_v.append(v_)
    return (loss, grad_x, *grads, *deltas, *new_m, *new_v)
```

```python
import functools
import math

import numpy as np
import jax
import jax.numpy as jnp
from jax import lax
from jax.experimental import pallas as pl
from jax.experimental.pallas import tpu as pltpu

F32 = jnp.float32
BF16 = jnp.bfloat16
AXES = ("x", "y", "c")
NDEV = 8
EPS = 1e-6
TM = 256
TML = 512
TC = 256
SEG = TC // 8
NTAB = SEG + 16
S5_UNROLL = True
GRID_W = 64
POOL_WINDOWS = (2, 4, 8, 16)
LANE = 128
FPAD = 384
VMEM_LIMIT = 56 * 1024 * 1024
ADAM_BLOCK_BYTES = 512 * 1024
ADAM_LR, ADAM_B1, ADAM_B2, ADAM_EPS, ADAM_WD, ADAM_STEP = 0.001, 0.9, 0.999, 1e-08, 0.01, 10


def _raw_call(body, kw):
    return pl.pallas_call(body, **kw)


def _pcall(body, xchg=None, edges=None, **kw):
    extra = ()
    if xchg:
        arrs, kinds = [a for a, _ in xchg], [k for _, k in xchg]
        n = len(arrs)
        n_in, n_out, n_scr = len(kw["in_specs"]), len(kw["out_shape"]), len(kw.get("scratch_shapes", ()))
        inner = body

        def hosted(*refs):
            a, b = n_in, n_in + n
            c_, d_ = b + n_out, b + n_out + n
            e = d_ + n_scr
            first, last = edges()

            @pl.when(first)
            def _():
                _xchg_start(refs[a:b], refs[c_:d_], refs[e:], kinds)

            inner(*refs[:a], *refs[b:c_], *refs[d_:e])

            @pl.when(last)
            def _():
                _xchg_wait(refs[a:b], refs[c_:d_], refs[e:], kinds)

        body = hosted
        any_spec = pl.BlockSpec(memory_space=pl.ANY)
        kw = dict(kw, in_specs=list(kw["in_specs"]) + [any_spec] * n,
                  out_shape=tuple(kw["out_shape"]) + _xchg_shapes(arrs, kinds),
                  out_specs=tuple(kw["out_specs"]) + (any_spec,) * n,
                  scratch_shapes=list(kw.get("scratch_shapes", ())) + _xchg_sems(n))
        extra = tuple(arrs)
    call = _raw_call(body, kw)
    return (lambda *args: call(*args, *extra)) if extra else call


def _params(sem):
    return pltpu.CompilerParams(dimension_semantics=sem, vmem_limit_bytes=VMEM_LIMIT)


def _dot(a, b):
    return jnp.dot(a, b, preferred_element_type=F32)


def _dot_nt(a, b):
    return lax.dot_general(a, b, (((1,), (1,)), ((), ())), preferred_element_type=F32)


def _dot_tn(a, b):
    return lax.dot_general(a, b, (((0,), (0,)), ((), ())), preferred_element_type=F32)


def _sigmoid(x):
    return 1.0 / (1.0 + jnp.exp(-x))


def _colsum(a):
    return jnp.sum(a, axis=0, keepdims=True)


def _resident(shape):
    nd = len(shape)
    return pl.BlockSpec(shape, lambda *_: (0,) * nd, pipeline_mode=pl.Buffered(1))


def _const(shape):
    nd = len(shape)
    return pl.BlockSpec(shape, lambda *_: (0,) * nd)


GATHER = "gather"
GATHER_REL = "gather_rel"
A2A = "a2a"
A2A_REL = "a2a_rel"
A2A_SAME = "a2a_same"
A2A_OTHER = "a2a_other"


def _exchange(items, name):
    arrs, kinds = [a for a, _ in items], [k for _, k in items]
    n = len(arrs)

    def body(*refs):
        _xchg_start(refs[:n], refs[n:2 * n], refs[2 * n:], kinds)
        _xchg_wait(refs[:n], refs[n:2 * n], refs[2 * n:], kinds)

    any_spec = pl.BlockSpec(memory_space=pl.ANY)
    outs = _pcall(
        body, name=name, out_shape=_xchg_shapes(arrs, kinds), in_specs=[any_spec] * n, out_specs=[any_spec] * n,
        scratch_shapes=_xchg_sems(n),
    )(*arrs)
    return list(outs)


def _edges1(n0):
    return lambda: (pl.program_id(0) == 0, pl.program_id(0) == n0 - 1)


def _edges2(n0, n1):
    return lambda: ((pl.program_id(0) == 0) & (pl.program_id(1) == 0),
                    (pl.program_id(0) == n0 - 1) & (pl.program_id(1) == n1 - 1))


def _xchg_shapes(arrs, kinds):
    return tuple(jax.ShapeDtypeStruct(((NDEV,) if k in (GATHER, GATHER_REL) else ()) + tuple(a.shape), a.dtype)
                 for a, k in zip(arrs, kinds))


def _xchg_sems(n):
    return [pltpu.SemaphoreType.DMA((n * (NDEV - 1),)), pltpu.SemaphoreType.DMA((n * (NDEV - 1),)),
            pltpu.SemaphoreType.DMA((n,))]


def _xchg_plan(in_refs, out_refs, sems, kinds):
    send_sems, recv_sems, loc_sems = sems
    x, y, c = (lax.axis_index(a) for a in AXES)
    my_abs, my_chip = 4 * x + 2 * y + c, 2 * x + y

    def peer(p):
        px = 1 - x if p & 4 else x
        py = 1 - y if p & 2 else y
        pc = 1 - c if p & 1 else c
        return (px, py, pc), 4 * px + 2 * py + pc, 2 * px + py

    starts, relays, recvs = [], [], []
    for k, kind in enumerate(kinds):
        src, dst = in_refs[k], out_refs[k]

        def remote(src_ref, row, p, pair, dst=dst, k=k):
            dev, sem = peer(p)[0], k * (NDEV - 1) + pair - 1
            return lambda: pltpu.make_async_remote_copy(
                src_ref=src_ref, dst_ref=dst.at[row], send_sem=send_sems.at[sem], recv_sem=recv_sems.at[sem],
                device_id=dev, device_id_type=pl.DeviceIdType.MESH)

        def local(src_ref, row, dst=dst, k=k):
            return lambda: pltpu.make_async_copy(src_ref, dst.at[row], loc_sems.at[k])

        if kind == GATHER:
            for p in range(1, NDEV):
                starts.append((remote(src, my_abs, p, p), False))
                recvs.append(remote(src, peer(p)[1], p, p))
            starts.append((local(src, my_abs), True))
        elif kind == GATHER_REL:
            for p in (1, 4, 2, 6):
                starts.append((remote(src, 4 * (p & 1) + my_chip, p, p), False))
            for q in (4, 2, 6):
                pchip = peer(q)[2]
                relays.append((remote(src, pchip, q, q), remote(dst.at[pchip], 4 + pchip, 1, q | 1)))
                recvs.append(remote(src, 4 + pchip, 1, q | 1))
            recvs.append(remote(src, 4 + my_chip, 1, 1))
            starts.append((local(src, my_chip), True))
        elif kind in (A2A, A2A_REL):
            for p in range(1, NDEV):
                _, pabs, pchip = peer(p)
                theirs, mine = (pabs, my_abs) if kind == A2A else (4 * (p & 1) + pchip, 4 * (p & 1) + my_chip)
                starts.append((remote(src.at[theirs], mine, p, p), False))
                recvs.append(remote(src.at[theirs], theirs, p, p))
            own = my_abs if kind == A2A else my_chip
            starts.append((local(src.at[own], own), True))
        else:
            for p in ((4, 2, 6) if kind == A2A_SAME else (1, 5, 3, 7)):
                pchip = peer(p)[2]
                starts.append((remote(src.at[pchip], my_chip, p, p), False))
                recvs.append(remote(src.at[pchip], pchip, p, p))
            if kind == A2A_SAME:
                starts.append((local(src.at[my_chip], my_chip), True))
    return starts, relays, recvs


def _xchg_start(in_refs, out_refs, sems, kinds):
    for make, _ in _xchg_plan(in_refs, out_refs, sems, kinds)[0]:
        make().start()


def _xchg_wait(in_refs, out_refs, sems, kinds):
    starts, relays, recvs = _xchg_plan(in_refs, out_refs, sems, kinds)
    forwards = []
    for arrival, forward in relays:
        arrival().wait_recv()
        fwd = forward()
        fwd.start()
        forwards.append(fwd)
    for make in recvs:
        make().wait_recv()
    for make, is_local in starts:
        if is_local:
            make().wait()
        else:
            make().wait_send()
    for fwd in forwards:
        fwd.wait_send()


def _ada_fwd(c16, w_blk, b_blk):
    def body(c_ref, w_ref, b_ref, o_ref):
        cc = c_ref[...]
        s = (cc * _sigmoid(cc)).astype(BF16)
        o_ref[...] = _dot(s, w_ref[...].astype(BF16)) + b_ref[...]

    return _pcall(body, name="ada_fwd", out_shape=jax.ShapeDtypeStruct((16, w_blk.shape[1]), F32),
                  compiler_params=_params(None))(c16, w_blk, b_blk)


def _ada_bwd(c16, w_blk, dl, dc):
    d, w = w_blk.shape

    def body(c_ref, w_ref, dl_ref, dc_ref, gw_ref, gc_ref):
        cc = c_ref[...]
        sg = _sigmoid(cc)
        s = (cc * sg).astype(BF16)
        dctot = _colsum(dc_ref[...])
        dm = jnp.concatenate([dl_ref[...], jnp.broadcast_to(dctot, (8, w))], axis=0)
        rows = lax.broadcasted_iota(jnp.int32, (16, w), 0)
        dm = jnp.where(rows <= 8, dm, 0.0).astype(BF16)
        gw_ref[...] = _dot_tn(s, dm)
        t = _dot_nt(jnp.broadcast_to(dctot, (8, w)).astype(BF16), w_ref[...].astype(BF16))[0:1]
        c8, s8 = cc[8:9], sg[8:9]
        gc_ref[...] = t * (s8 * (1.0 + c8 * (1.0 - s8)))

    return _pcall(body, name="ada_bwd",
                  out_shape=(jax.ShapeDtypeStruct((d, w), F32), jax.ShapeDtypeStruct((1, d), F32)),
                  compiler_params=_params(None))(c16, w_blk, dl, dc)


def _norm_fwd(x, g, shift, scale):
    rstd = lax.rsqrt(jnp.mean(x * x, axis=-1, keepdims=True) + EPS)
    xhat = x * rstd
    xn = xhat * g
    return rstd, xhat, xn, xn * (1.0 + scale) + shift


def _norm_bwd(dh, rstd, xhat, xn, g, scale):
    dxn = dh * (1.0 + scale)
    dxhat = dxn * g
    dx = rstd * (dxhat - xhat * jnp.mean(dxhat * xhat, axis=-1, keepdims=True))
    return dx, _colsum(dh), _colsum(dh * xn), _colsum(dxn * xhat)


def _mod_spec(d, has_ctx):
    if has_ctx:
        return pl.BlockSpec((1, 9, d), lambda *ids: (jnp.minimum(ids[-1], 1), 0, 0))
    return pl.BlockSpec((1, 9, d), lambda *ids: (1, 0, 0))


def _tile(has_ctx):
    return TM if has_ctx else TML


def _dmod_spec(nrow, d, has_ctx):
    if has_ctx:
        return pl.BlockSpec((1, nrow, d), lambda i: (jnp.minimum(i, 1), 0, 0))
    return pl.BlockSpec((1, nrow, d), lambda i: (0, 0, 0))


def _ffn_fwd(x, modtab, g, w13, w2, k0, has_ctx, name, xchg=None):
    t, d = x.shape
    nb, _, w = w13.shape
    fp = w // 2
    tm = _tile(has_ctx)

    def body(x_ref, mod_ref, g_ref, w13_ref, w2_ref, xo_ref, h_ref, ab_ref, o_ref):
        xx = x_ref[...]
        shift, scale, gate = mod_ref[0, k0:k0 + 1, :], mod_ref[0, k0 + 1:k0 + 2, :], mod_ref[0, k0 + 2:k0 + 3, :]
        _, _, _, h = _norm_fwd(xx, g_ref[...], shift, scale)
        hb = h.astype(BF16)
        h_ref[...] = hb
        acc = jnp.zeros((tm, d), F32)
        for p in range(nb // 2):
            zs = []
            for q in range(2):
                blk = 2 * p + q
                ab = _dot(hb, w13_ref[blk])
                ab_ref[:, blk * w:(blk + 1) * w] = ab.astype(BF16)
                a, b = ab[:, :fp], ab[:, fp:]
                zs.append((a * _sigmoid(a) * b).astype(BF16))
            acc = acc + _dot(jnp.concatenate(zs, axis=1), w2_ref[p])
        o_ref[...] = acc.astype(BF16)
        xo_ref[...] = xx + (0.5 * gate) * acc

    row = lambda i: (i, 0)
    return _pcall(
        body, name=name, grid=(t // tm,), xchg=xchg, edges=_edges1(t // tm),
        out_shape=(jax.ShapeDtypeStruct((t, d), F32), jax.ShapeDtypeStruct((t, d), BF16),
                   jax.ShapeDtypeStruct((t, nb * w), BF16), jax.ShapeDtypeStruct((t, d), BF16)),
        in_specs=[pl.BlockSpec((tm, d), row), _mod_spec(d, has_ctx), _const((1, d)),
                  _resident(w13.shape), _resident(w2.shape)],
        out_specs=(pl.BlockSpec((tm, d), row), pl.BlockSpec((tm, d), row), pl.BlockSpec((tm, nb * w), row),
                   pl.BlockSpec((tm, d), row)),
        compiler_params=_params(("arbitrary",)),
    )(x, modtab, g, w13, w2)


def _ffn_bwd(dy, modtab, h, ab, w13, w2, k0, has_ctx, name, pair0=0, npair=None, xchg=None):
    t, d = dy.shape
    _, _, w = w13.shape
    fp = w // 2
    npair = w13.shape[0] // 2 if npair is None else npair
    nb = 2 * npair
    tm = _tile(has_ctx)
    nt = t // tm

    def body(dy_ref, mod_ref, h_ref, ab_ref, w13_ref, w2_ref, dh_ref, d13_ref, d2_ref, acc13, acc2):
        i = pl.program_id(1)

        @pl.when(i == 0)
        def _():
            acc13[...] = jnp.zeros_like(acc13)
            acc2[...] = jnp.zeros_like(acc2)

        gate = mod_ref[0, k0 + 2:k0 + 3, :]
        do = (dy_ref[...] * (0.5 * gate)).astype(BF16)
        dz = _dot_nt(do, w2_ref[0])
        hb = h_ref[...]
        dh = jnp.zeros((tm, d), F32)
        zs = []
        for q in range(2):
            ab = ab_ref[:, q * w:(q + 1) * w].astype(F32)
            a, b = ab[:, :fp], ab[:, fp:]
            sg = _sigmoid(a)
            sa = a * sg
            dzq = dz[:, q * fp:(q + 1) * fp]
            da = dzq * b * (sg * (1.0 + a * (1.0 - sg)))
            db = dzq * sa
            dab = jnp.concatenate([da, db], axis=1).astype(BF16)
            dh = dh + _dot_nt(dab, w13_ref[q])
            acc13[q] += _dot_tn(hb, dab)
            zs.append((sa * b).astype(BF16))
        acc2[...] += _dot_tn(jnp.concatenate(zs, axis=1), do)
        dh_ref[0] = dh

        @pl.when(i == nt - 1)
        def _():
            d13_ref[...] = acc13[...].astype(BF16)
            d2_ref[0] = acc2[...].astype(BF16)

    mod_spec = _mod_spec(d, has_ctx)
    return _pcall(
        body, name=name, grid=(npair, nt), xchg=xchg, edges=_edges2(npair, nt),
        out_shape=(jax.ShapeDtypeStruct((npair, t, d), F32), jax.ShapeDtypeStruct((nb, d, w), BF16),
                   jax.ShapeDtypeStruct((npair, 2 * fp, d), BF16)),
        in_specs=[pl.BlockSpec((tm, d), lambda p, i: (i, 0)), mod_spec, pl.BlockSpec((tm, d), lambda p, i: (i, 0)),
                  pl.BlockSpec((tm, 2 * w), lambda p, i: (i, pair0 + p)),
                  pl.BlockSpec((2, d, w), lambda p, i: (pair0 + p, 0, 0)),
                  pl.BlockSpec((1, 2 * fp, d), lambda p, i: (pair0 + p, 0, 0))],
        out_specs=(pl.BlockSpec((1, tm, d), lambda p, i: (p, i, 0)), pl.BlockSpec((2, d, w), lambda p, i: (p, 0, 0)),
                   pl.BlockSpec((1, 2 * fp, d), lambda p, i: (p, 0, 0))),
        scratch_shapes=[pltpu.VMEM((2, d, w), F32), pltpu.VMEM((2 * fp, d), F32)],
        compiler_params=_params(("arbitrary", "arbitrary")),
    )(dy, modtab, h, ab, w13, w2)


def _ffn_bwd_norm(dy, dhps, x, o, modtab, g, k0, has_ctx, name, xchg=None):
    t, d = dy.shape
    ngrp = 2 if has_ctx else 1
    tm = _tile(has_ctx)
    ndh = len(dhps)

    def body(dy_ref, *rest):
        dhp_refs = rest[:ndh]
        x_ref, o_ref, mod_ref, g_ref, dx_ref, dmod_ref, dg_ref = rest[ndh:]
        i = pl.program_id(0)

        @pl.when(i == 0)
        def _():
            dg_ref[...] = jnp.zeros_like(dg_ref)

        @pl.when((i == 0) | (i == ngrp - 1))
        def _():
            dmod_ref[...] = jnp.zeros_like(dmod_ref)

        dh = None
        for ref in dhp_refs:
            for p in range(ref.shape[0]):
                dh = ref[p] if dh is None else dh + ref[p]
        scale = mod_ref[0, k0 + 1:k0 + 2, :]
        gg = g_ref[...]
        rstd, xhat, xn, _ = _norm_fwd(x_ref[...], gg, 0.0, scale)
        dxn, dshift, dscale, dg = _norm_bwd(dh, rstd, xhat, xn, gg, scale)
        dyv = dy_ref[...]
        dx_ref[...] = dyv + dxn
        dmod_ref[0, 0:1, :] += dshift
        dmod_ref[0, 1:2, :] += dscale
        dmod_ref[0, 2:3, :] += _colsum(0.5 * dyv * o_ref[...].astype(F32))
        dg_ref[...] += dg

    row = lambda i: (i, 0)
    return _pcall(
        body, name=name, grid=(t // tm,), xchg=xchg, edges=_edges1(t // tm),
        out_shape=(jax.ShapeDtypeStruct((t, d), F32), jax.ShapeDtypeStruct((ngrp, 3, d), F32),
                   jax.ShapeDtypeStruct((1, d), F32)),
        in_specs=[pl.BlockSpec((tm, d), row)]
        + [pl.BlockSpec((a.shape[0], tm, d), lambda i: (0, i, 0)) for a in dhps]
        + [pl.BlockSpec((tm, d), row), pl.BlockSpec((tm, d), row), _mod_spec(d, has_ctx), _const((1, d))],
        out_specs=(pl.BlockSpec((tm, d), row), _dmod_spec(3, d, has_ctx), _const((1, d))),
        compiler_params=_params(("arbitrary",)),
    )(dy, *dhps, x, o, modtab, g)


def _mix_in_fwd(x1, modtab, g, w_in):
    t, d = x1.shape

    def body(x_ref, mod_ref, g_ref, w_ref, h_ref, u_ref):
        _, _, _, h = _norm_fwd(x_ref[...], g_ref[...], mod_ref[0, 3:4, :], mod_ref[0, 4:5, :])
        hb = h.astype(BF16)
        h_ref[...] = hb
        u_ref[...] = _dot(hb, w_ref[...])

    row = lambda i: (i, 0)
    return _pcall(
        body, name="mix_in_fwd", grid=(t // TM,),
        out_shape=(jax.ShapeDtypeStruct((t, d), BF16), jax.ShapeDtypeStruct((t, w_in.shape[1]), F32)),
        in_specs=[pl.BlockSpec((TM, d), row), _mod_spec(d, True), _const((1, d)), _resident(w_in.shape)],
        out_specs=(pl.BlockSpec((TM, d), row), pl.BlockSpec((TM, w_in.shape[1]), row)),
        compiler_params=_params(("arbitrary",)),
    )(x1, modtab, g, w_in)


def _mix_in_bwd(du_pool, du_f, du_b, dx2, h1, x1, modtab, g, w_in):
    t, d = x1.shape
    m = w_in.shape[1]
    hm = m // 2
    nt = t // TM

    def body(dup_ref, duf_ref, dub_ref, dx2_ref, h_ref, x_ref, mod_ref, g_ref, w_ref, dx_ref, dw_ref, dmod_ref, dg_ref, acc):
        i = pl.program_id(0)

        @pl.when(i == 0)
        def _():
            acc[...] = jnp.zeros_like(acc)
            dg_ref[...] = jnp.zeros_like(dg_ref)

        @pl.when(i <= 1)
        def _():
            dmod_ref[...] = jnp.zeros_like(dmod_ref)

        lat = (i > 0).astype(F32)
        du = jnp.concatenate([dup_ref[...] * lat, duf_ref[...] + dub_ref[...]], axis=1).astype(BF16)
        dh = _dot_nt(du, w_ref[...])
        acc[...] += _dot_tn(h_ref[...], du)
        scale = mod_ref[0, 4:5, :]
        gg = g_ref[...]
        rstd, xhat, xn, _ = _norm_fwd(x_ref[...], gg, 0.0, scale)
        dxn, dshift, dscale, dg = _norm_bwd(dh, rstd, xhat, xn, gg, scale)
        dx_ref[...] = dx2_ref[...] * lat + dxn
        dmod_ref[0, 0:1, :] += dshift
        dmod_ref[0, 1:2, :] += dscale
        dg_ref[...] += dg

        @pl.when(i == nt - 1)
        def _():
            dw_ref[...] = acc[...].astype(BF16)

    row = lambda i: (i, 0)
    lrow = lambda i: (jnp.maximum(i - 1, 0), 0)
    return _pcall(
        body, name="mix_in_bwd", grid=(nt,),
        out_shape=(jax.ShapeDtypeStruct((t, d), F32), jax.ShapeDtypeStruct((d, m), BF16),
                   jax.ShapeDtypeStruct((2, 2, d), F32), jax.ShapeDtypeStruct((1, d), F32)),
        in_specs=[pl.BlockSpec((TM, hm), lrow), pl.BlockSpec((TM, hm), row), pl.BlockSpec((TM, hm), row),
                  pl.BlockSpec((TM, d), lrow), pl.BlockSpec((TM, d), row), pl.BlockSpec((TM, d), row),
                  _mod_spec(d, True), _const((1, d)), _resident(w_in.shape)],
        out_specs=(pl.BlockSpec((TM, d), row), _const((d, m)), _dmod_spec(2, d, True), _const((1, d))),
        scratch_shapes=[pltpu.VMEM((d, m), F32)],
        compiler_params=_params(("arbitrary",)),
    )(du_pool, du_f, du_b, dx2, h1, x1, modtab, g, w_in)


def _pool(v, transpose, name):
    ngrp, rows, n = v.shape
    nchunk = 4
    cw = n // nchunk

    def rowsum(val, lo, hi):
        ri = lax.broadcasted_iota(jnp.int32, (rows, rows), 0)
        ci = lax.broadcasted_iota(jnp.int32, (rows, rows), 1)
        sel = (((ci - ri) >= -lo) & ((ci - ri) <= hi)).astype(BF16)
        v1 = val.astype(BF16)
        r1 = val - v1.astype(F32)
        v2 = r1.astype(BF16)
        v3 = (r1 - v2.astype(F32)).astype(BF16)
        return _dot(sel, v1) + _dot(sel, v2) + _dot(sel, v3)

    def one(v_ref, o_ref, r_scr, a_scr, win):
        lo = win // 2
        hi = win - 1 - lo
        rlo, rhi = (hi, lo) if transpose else (lo, hi)
        ridx = lax.broadcasted_iota(jnp.int32, (rows, 1), 0)
        cnt_r = (jnp.minimum(ridx + hi + 1, rows) - jnp.maximum(ridx - lo, 0)).astype(F32)
        cidx = lax.broadcasted_iota(jnp.int32, (1, n), 1) // LANE
        cnt_c = (jnp.minimum(cidx + hi + 1, GRID_W) - jnp.maximum(cidx - lo, 0)).astype(F32)
        if not transpose:
            for k in range(nchunk):
                sl = slice(k * cw, (k + 1) * cw)
                r_scr[:, sl] = rowsum(v_ref[0, :, sl], rlo, rhi) / cnt_r
        else:
            r_scr[...] = v_ref[0] / cnt_c
        a_scr[...] = r_scr[...]
        for j in range(-rlo, rhi + 1):
            if j == 0:
                continue
            c0, c1 = max(0, -j), min(GRID_W, GRID_W - j)
            a_scr[:, c0 * LANE:c1 * LANE] += r_scr[:, (c0 + j) * LANE:(c1 + j) * LANE]
        if not transpose:
            o_ref[0] = a_scr[...] / cnt_c - v_ref[0]
        else:
            for k in range(nchunk):
                sl = slice(k * cw, (k + 1) * cw)
                o_ref[0, :, sl] = rowsum(a_scr[:, sl] / cnt_r, rlo, rhi) - v_ref[0, :, sl]

    def body(v_ref, o_ref, r_scr, a_scr):
        grp = pl.program_id(0)
        for k, win in enumerate(POOL_WINDOWS):
            @pl.when(grp == k)
            def _(win=win):
                one(v_ref, o_ref, r_scr, a_scr, win)

    spec = pl.BlockSpec((1, rows, n), lambda k: (k, 0, 0))
    return _pcall(
        body, name=name, grid=(ngrp,), out_shape=jax.ShapeDtypeStruct(v.shape, F32),
        in_specs=[spec], out_specs=spec,
        scratch_shapes=[pltpu.VMEM((rows, n), F32), pltpu.VMEM((rows, n), F32)],
        compiler_params=_params(("arbitrary",)),
    )(v)


def _cmul(ar, ai, br, bi):
    return ar * br - ai * bi, ar * bi + ai * br


def _s5_prep(a_re, a_im, log_dt, c_re, c_im, kvec):
    q, nc, p = c_re.shape

    def body(ar_ref, ai_ref, ldt_ref, cr_ref, ci_ref, k_ref, cpr_ref, cpi_ref, tr_ref, ti_ref):
        lr, li = ar_ref[...], ai_ref[...]
        dt = jnp.exp(ldt_ref[...])
        kk = k_ref[...]
        mag = jnp.exp(kk * (lr * dt))
        ph = kk * (li * dt)
        tr_ref[...] = mag * jnp.cos(ph)
        ti_ref[...] = mag * jnp.sin(ph)
        m1 = jnp.exp(lr * dt)
        abr, abi = m1 * jnp.cos(li * dt), m1 * jnp.sin(li * dt)
        den = lr * lr + li * li
        xr, xi = abr - 1.0, abi
        cfr, cfi = (xr * lr + xi * li) / den, (xi * lr - xr * li) / den
        pr, pi_ = _cmul(cr_ref[...], ci_ref[...], cfr, cfi)
        cpr_ref[...] = pr
        cpi_ref[...] = pi_

    return _pcall(
        body, name="s5_prep",
        out_shape=(jax.ShapeDtypeStruct((q, nc, p), F32), jax.ShapeDtypeStruct((q, nc, p), F32),
                   jax.ShapeDtypeStruct((q, NTAB, p), F32), jax.ShapeDtypeStruct((q, NTAB, p), F32)),
        compiler_params=_params(None),
    )(a_re, a_im, log_dt, c_re, c_im, kvec)


def _s5_param_bwd(a_re, a_im, log_dt, c_re, c_im, dab_r, dab_i, dcp_r, dcp_i):
    q, nc, p = c_re.shape

    def body(ar_ref, ai_ref, ldt_ref, cr_ref, ci_ref, dar_ref, dai_ref, dcr_ref, dci_ref,
             gar_ref, gai_ref, gdt_ref, gcr_ref, gci_ref):
        lr, li = ar_ref[...], ai_ref[...]
        dt = jnp.exp(ldt_ref[...])
        m1 = jnp.exp(lr * dt)
        abr, abi = m1 * jnp.cos(li * dt), m1 * jnp.sin(li * dt)
        den = lr * lr + li * li
        xr, xi = abr - 1.0, abi
        cfr, cfi = (xr * lr + xi * li) / den, (xi * lr - xr * li) / den
        cr, ci = cr_ref[...], ci_ref[...]
        dcr, dci = dcr_ref[...], dci_ref[...]
        gcr, gci = _cmul(dcr, dci, cfr, -cfi)
        gcr_ref[...] = gcr
        gci_ref[...] = gci
        t_r, t_i = _cmul(cr, -ci, dcr, dci)
        dcf_r = jnp.sum(t_r, axis=1, keepdims=True)
        dcf_i = jnp.sum(t_i, axis=1, keepdims=True)
        ilr, ili = lr / den, -li / den
        u_r, u_i = _cmul(dcf_r, dcf_i, ilr, -ili)
        dab_r_, dab_i_ = dar_ref[...] + u_r, dai_ref[...] + u_i
        v_r, v_i = _cmul(dab_r_, dab_i_, abr, -abi)
        cl_r, cl_i = _cmul(cfr, cfi, ilr, ili)
        w_r, w_i = _cmul(dcf_r, dcf_i, cl_r, -cl_i)
        gar_ref[...] = v_r * dt - w_r
        gai_ref[...] = v_i * dt - w_i
        la_r, la_i = _cmul(lr, li, abr, abi)
        ddt = la_r * dab_r_ + la_i * dab_i_
        gdt_ref[...] = dt * jnp.sum(ddt, axis=2, keepdims=True)

    return _pcall(
        body, name="s5_param_bwd",
        out_shape=(jax.ShapeDtypeStruct((q, 1, p), F32), jax.ShapeDtypeStruct((q, 1, p), F32),
                   jax.ShapeDtypeStruct((q, 1, p), F32), jax.ShapeDtypeStruct((q, nc, p), F32),
                   jax.ShapeDtypeStruct((q, nc, p), F32)),
        compiler_params=_params(None),
    )(a_re, a_im, log_dt, c_re, c_im, dab_r, dab_i, dcp_r, dcp_i)


def _bcast8(row, c):
    return jnp.broadcast_to(row, (8, c))


def _slab(j):
    return j * 8 if isinstance(j, int) else pl.multiple_of(j * 8, 8)


def _scan_head(xr, xi, tre_ref, tim_ref, car_r, car_i, desc, conj, cs):
    sg = -1.0 if conj else 1.0
    ar = _bcast8(tre_ref[0:1, :], cs)
    ai = sg * _bcast8(tim_ref[0:1, :], cs)

    def p1(jj, carry):
        hr, hi = carry
        off = _slab(SEG - 1 - jj if desc else jj)
        pr, pi_ = _cmul(ar, ai, hr, hi)
        nr = pr + xr[pl.ds(off, 8), :]
        ni = pi_ + xi[pl.ds(off, 8), :]
        xr[pl.ds(off, 8), :] = nr
        xi[pl.ds(off, 8), :] = ni
        return nr, ni

    zero = jnp.zeros((8, cs), F32)
    ir, ii = lax.fori_loop(0, SEG, p1, (zero, zero), unroll=S5_UNROLL)
    cin_r, cin_i = car_r[...], car_i[...]
    rows = lax.broadcasted_iota(jnp.int32, (8, cs), 0)
    for s, krow in ((1, SEG), (2, SEG + 1), (4, SEG + 3)):
        keep, sh = (rows < 8 - s, 8 - s) if desc else (rows >= s, s)
        sr = jnp.where(keep, pltpu.roll(ir, sh, 0), 0.0)
        si = jnp.where(keep, pltpu.roll(ii, sh, 0), 0.0)
        pr, pi_ = _cmul(tre_ref[krow:krow + 1, :], sg * tim_ref[krow:krow + 1, :], sr, si)
        ir, ii = ir + pr, ii + pi_
    q0 = SEG + 8 if desc else SEG
    pr, pi_ = _cmul(tre_ref[q0:q0 + 8, :], sg * tim_ref[q0:q0 + 8, :], cin_r, cin_i)
    fr, fi = ir + pr, ii + pi_
    keep, sh, edge = (rows < 7, 7, 0) if desc else (rows >= 1, 1, 7)
    cs_r = jnp.where(keep, pltpu.roll(fr, sh, 0), cin_r)
    cs_i = jnp.where(keep, pltpu.roll(fi, sh, 0), cin_i)
    car_r[...] = _bcast8(fr[edge:edge + 1, :], cs)
    car_i[...] = _bcast8(fi[edge:edge + 1, :], cs)
    return cs_r, cs_i, cin_r, cin_i


def _pow_row(tre_ref, tim_ref, j, desc, conj, cs):
    k = SEG - 1 - j if desc else j
    sg = -1.0 if conj else 1.0
    return _bcast8(tre_ref[pl.ds(k, 1), :], cs), sg * _bcast8(tim_ref[pl.ds(k, 1), :], cs)


def _s5_fwd(up, bre, bim, cmr, cmi, tre, tim, dskip, order, desc, with_skip, name):
    t, cu = up.shape
    nsb = bre.shape[0]
    cb = cu // nsb
    cs = bre.shape[2]
    nch = t // TC

    def body(u_ref, bre_ref, bim_ref, cmr_ref, cmi_ref, tre_ref, tim_ref, dsk_ref,
             y_ref, hr_ref, hi_ref, cinr_ref, cini_ref, car_r, car_i):
        i = pl.program_id(0)

        @pl.when(i == 0)
        def _():
            car_r[...] = jnp.zeros_like(car_r)
            car_i[...] = jnp.zeros_like(car_i)

        for sb in range(nsb):
            col, ch = pl.ds(sb * cs, cs), slice(sb * cb, (sb + 1) * cb)
            hr_v, hi_v = hr_ref.at[:, col], hi_ref.at[:, col]
            tre_v, tim_v = tre_ref.at[:, col], tim_ref.at[:, col]
            u = u_ref[:, ch]
            ub = u.astype(BF16)
            hr_v[...] = _dot(ub, bre_ref[sb])
            hi_v[...] = _dot(ub, bim_ref[sb])
            cs_r, cs_i, cin_r, cin_i = _scan_head(hr_v, hi_v, tre_v, tim_v, car_r.at[sb], car_i.at[sb], desc, False, cs)
            cinr_ref[:, col] = cin_r
            cini_ref[:, col] = cin_i

            def p2(j, _, hr_v=hr_v, hi_v=hi_v, tre_v=tre_v, tim_v=tim_v, cs_r=cs_r, cs_i=cs_i):
                off = _slab(j)
                pw_r, pw_i = _pow_row(tre_v, tim_v, j, desc, False, cs)
                pr, pi_ = _cmul(pw_r, pw_i, cs_r, cs_i)
                hr_v[pl.ds(off, 8), :] = hr_v[pl.ds(off, 8), :] + pr
                hi_v[pl.ds(off, 8), :] = hi_v[pl.ds(off, 8), :] + pi_
                return 0

            lax.fori_loop(0, SEG, p2, 0, unroll=S5_UNROLL)
            y = _dot(hr_v[...].astype(BF16), cmr_ref[sb]) - _dot(hi_v[...].astype(BF16), cmi_ref[sb])
            if with_skip:
                y = y + dsk_ref[:, ch] * u
            y_ref[:, ch] = y

    blk = lambda i: (order(i), 0)
    return _pcall(
        body, name=name, grid=(nch,),
        out_shape=(jax.ShapeDtypeStruct((t, cu), F32), jax.ShapeDtypeStruct((t, nsb * cs), F32),
                   jax.ShapeDtypeStruct((t, nsb * cs), F32), jax.ShapeDtypeStruct((nch * 8, nsb * cs), F32),
                   jax.ShapeDtypeStruct((nch * 8, nsb * cs), F32)),
        in_specs=[pl.BlockSpec((TC, cu), blk), _const(bre.shape), _const(bim.shape), _const(cmr.shape),
                  _const(cmi.shape), _const(tre.shape), _const(tim.shape), _const(dskip.shape)],
        out_specs=(pl.BlockSpec((TC, cu), blk), pl.BlockSpec((TC, nsb * cs), blk), pl.BlockSpec((TC, nsb * cs), blk),
                   pl.BlockSpec((8, nsb * cs), blk), pl.BlockSpec((8, nsb * cs), blk)),
        scratch_shapes=[pltpu.VMEM((nsb, 8, cs), F32), pltpu.VMEM((nsb, 8, cs), F32)],
        compiler_params=_params(("arbitrary",)),
    )(up, bre, bim, cmr, cmi, tre, tim, dskip)


def _s5_bwd(dyp, up, hr, hi, cinr, cini, bre, bim, ctr, cti, tre, tim, dskip, order, desc, with_skip, name, xchg=None):
    t, cu = up.shape
    nsb = bre.shape[0]
    cb = cu // nsb
    cs = bre.shape[2]
    nch = t // TC
    adesc = not desc

    def body(dy_ref, u_ref, hr_ref, hi_ref, cinr_ref, cini_ref, bre_ref, bim_ref, ctr_ref, cti_ref, tre_ref, tim_ref,
             dsk_ref, du_ref, dar_ref, dai_ref, dcr_ref, dci_ref, dbr_ref, dbi_ref, dd_ref,
             car_r, car_i, mr_all, mi_all, acc_r, acc_i):
        i = pl.program_id(0)

        @pl.when(i == 0)
        def _():
            for ref in (car_r, car_i, acc_r, acc_i, dcr_ref, dci_ref, dbr_ref, dbi_ref, dd_ref):
                ref[...] = jnp.zeros_like(ref)

        rows = lax.broadcasted_iota(jnp.int32, (8, cs), 0)
        for sb in range(nsb):
            col, ch = pl.ds(sb * cs, cs), slice(sb * cb, (sb + 1) * cb)
            hr_v, hi_v = hr_ref.at[:, col], hi_ref.at[:, col]
            tre_v, tim_v = tre_ref.at[:, col], tim_ref.at[:, col]
            mr, mi = mr_all.at[sb % 2], mi_all.at[sb % 2]
            dy = dy_ref[:, ch]
            dyb = dy.astype(BF16)
            u = u_ref[:, ch]
            ub = u.astype(BF16)
            mr[...] = _dot(dyb, ctr_ref[sb])
            mi[...] = -_dot(dyb, cti_ref[sb])
            cs_r, cs_i, _, _ = _scan_head(mr, mi, tre_v, tim_v, car_r.at[sb], car_i.at[sb], adesc, True, cs)

            def fix(j, hp_r, hp_i, acc, mr=mr, mi=mi, tre_v=tre_v, tim_v=tim_v, cs_r=cs_r, cs_i=cs_i):
                off = _slab(j)
                pw_r, pw_i = _pow_row(tre_v, tim_v, j, adesc, True, cs)
                pr, pi_ = _cmul(pw_r, pw_i, cs_r, cs_i)
                m_r = mr[pl.ds(off, 8), :] + pr
                m_i = mi[pl.ds(off, 8), :] + pi_
                mr[pl.ds(off, 8), :] = m_r
                mi[pl.ds(off, 8), :] = m_i
                a_r, a_i = acc
                return a_r + hp_r * m_r + hp_i * m_i, a_i + hp_r * m_i - hp_i * m_r

            edge_j, src, keep, sh = (SEG - 1, 0, rows < 7, 7) if desc else (0, (SEG - 1) * 8, rows >= 1, 1)
            h0r = jnp.where(keep, pltpu.roll(hr_v[src:src + 8, :], sh, 0), cinr_ref[:, col])
            h0i = jnp.where(keep, pltpu.roll(hi_v[src:src + 8, :], sh, 0), cini_ref[:, col])
            acc = fix(edge_j, h0r, h0i, (acc_r[sb], acc_i[sb]))

            def p2(jj, acc, fix=fix, hr_v=hr_v, hi_v=hi_v):
                j = jj if desc else jj + 1
                offp = _slab(j + 1 if desc else j - 1)
                return fix(j, hr_v[pl.ds(offp, 8), :], hi_v[pl.ds(offp, 8), :], acc)

            a_r, a_i = lax.fori_loop(0, SEG - 1, p2, acc, unroll=S5_UNROLL)
            acc_r[sb] = a_r
            acc_i[sb] = a_i

            mrb, mib = mr[...].astype(BF16), mi[...].astype(BF16)
            du = _dot_nt(mrb, bre_ref[sb]) + _dot_nt(mib, bim_ref[sb])
            if with_skip:
                du = du + dsk_ref[:, ch] * dy
                dd_ref[:, ch] += _colsum(dy * u)
            du_ref[:, ch] = du
            dbr_ref[sb] += _dot_tn(ub, mrb)
            dbi_ref[sb] += _dot_tn(ub, mib)
            dcr_ref[sb] += _dot_tn(dyb, hr_v[...].astype(BF16))
            dci_ref[sb] -= _dot_tn(dyb, hi_v[...].astype(BF16))

            @pl.when(i == nch - 1)
            def _(sb=sb, a_r=a_r, a_i=a_i):
                dar_ref[sb] = _colsum(a_r)
                dai_ref[sb] = _colsum(a_i)

    blk = lambda i: (order(i), 0)
    mat = (nsb, cb, cs)
    return _pcall(
        body, name=name, grid=(nch,), xchg=xchg, edges=_edges1(nch),
        out_shape=(jax.ShapeDtypeStruct((t, cu), F32),
                   jax.ShapeDtypeStruct((nsb, 1, cs), F32), jax.ShapeDtypeStruct((nsb, 1, cs), F32),
                   jax.ShapeDtypeStruct(mat, F32), jax.ShapeDtypeStruct(mat, F32),
                   jax.ShapeDtypeStruct(mat, F32), jax.ShapeDtypeStruct(mat, F32),
                   jax.ShapeDtypeStruct((1, cu), F32)),
        in_specs=[pl.BlockSpec((TC, cu), blk), pl.BlockSpec((TC, cu), blk), pl.BlockSpec((TC, nsb * cs), blk),
                  pl.BlockSpec((TC, nsb * cs), blk), pl.BlockSpec((8, nsb * cs), blk), pl.BlockSpec((8, nsb * cs), blk),
                  _const(mat), _const(mat), _const(mat), _const(mat), _const(tre.shape), _const(tim.shape),
                  _const((1, cu))],
        out_specs=(pl.BlockSpec((TC, cu), blk), _const((nsb, 1, cs)), _const((nsb, 1, cs)),
                   _const(mat), _const(mat), _const(mat), _const(mat), _const((1, cu))),
        scratch_shapes=[pltpu.VMEM((nsb, 8, cs), F32), pltpu.VMEM((nsb, 8, cs), F32), pltpu.VMEM((2, TC, cs), F32),
                        pltpu.VMEM((2, TC, cs), F32), pltpu.VMEM((nsb, 8, cs), F32), pltpu.VMEM((nsb, 8, cs), F32)],
        compiler_params=_params(("arbitrary",)),
    )(dyp, up, hr, hi, cinr, cini, bre, bim, ctr, cti, tre, tim, dskip)


def _gelu(x):
    k = math.sqrt(2.0 / math.pi)
    return 0.5 * x * (1.0 + jnp.tanh(k * (x + 0.044715 * (x * x * x))))


def _gelu_grad(x):
    k = math.sqrt(2.0 / math.pi)
    th = jnp.tanh(k * (x + 0.044715 * (x * x * x)))
    return 0.5 * (1.0 + th) + 0.5 * x * (1.0 - th * th) * (k * (1.0 + 3.0 * 0.044715 * (x * x)))


def _mix_out_fwd(yf, yb, dpool, x1, modtab, w_pool, pscale, w_glu, w_out):
    l, hm = dpool.shape
    d = x1.shape[1]

    def body(yf_ref, yb_ref, dp_ref, x_ref, mod_ref, wp_ref, ps_ref, wg_ref, wo_ref, x2_ref, yp_ref, cat_ref, mix_ref):
        ypre = yf_ref[...] + yb_ref[...]
        yp_ref[...] = ypre
        yg = _gelu(ypre)
        y2 = yg * _sigmoid(_dot(yg.astype(BF16), wg_ref[...]))
        po = _dot(dp_ref[...].astype(BF16), wp_ref[...]) * ps_ref[...]
        cat = jnp.concatenate([po, y2], axis=1).astype(BF16)
        cat_ref[...] = cat
        mix = _dot(cat, wo_ref[...])
        mix_ref[...] = mix.astype(BF16)
        x2_ref[...] = x_ref[...] + mod_ref[0, 5:6, :] * mix

    row = lambda i: (i, 0)
    lrow = lambda i: (i + 1, 0)
    return _pcall(
        body, name="mix_out_fwd", grid=(l // TM,),
        out_shape=(jax.ShapeDtypeStruct((l, d), F32), jax.ShapeDtypeStruct((l, hm), F32),
                   jax.ShapeDtypeStruct((l, d), BF16), jax.ShapeDtypeStruct((l, d), BF16)),
        in_specs=[pl.BlockSpec((TM, hm), lrow), pl.BlockSpec((TM, hm), lrow), pl.BlockSpec((TM, hm), row),
                  pl.BlockSpec((TM, d), lrow), _mod_spec(d, False), _const(w_pool.shape), _const((1, hm)),
                  _const(w_glu.shape), _resident(w_out.shape)],
        out_specs=(pl.BlockSpec((TM, d), row), pl.BlockSpec((TM, hm), row), pl.BlockSpec((TM, d), row),
                   pl.BlockSpec((TM, d), row)),
        compiler_params=_params(("arbitrary",)),
    )(yf, yb, dpool, x1, modtab, w_pool, pscale, w_glu, w_out)


def _mix_out_bwd(dx2, mix, cat, ypre, dpool, modtab, w_pool, pscale, w_glu, w_out):
    l, hm = dpool.shape
    d = dx2.shape[1]
    nt = l // TML

    def body(dx_ref, mix_ref, cat_ref, yp_ref, dp_ref, mod_ref, wp_ref, ps_ref, wg_ref, wo_ref,
             dyp_ref, ddp_ref, dwo_ref, dwg_ref, dwp_ref, dps_ref, dgate_ref, acc_o, acc_g):
        i = pl.program_id(0)

        @pl.when(i == 0)
        def _():
            for ref in (acc_o, acc_g, dwp_ref, dps_ref, dgate_ref):
                ref[...] = jnp.zeros_like(ref)

        dx = dx_ref[...]
        dgate_ref[...] += _colsum(dx * mix_ref[...].astype(F32))
        dmix = (dx * mod_ref[0, 5:6, :]).astype(BF16)
        dcat = _dot_nt(dmix, wo_ref[...])
        acc_o[...] += _dot_tn(cat_ref[...], dmix)
        dpo, dy2 = dcat[:, :hm], dcat[:, hm:]
        dpb = dp_ref[...].astype(BF16)
        pp = _dot(dpb, wp_ref[...])
        dps_ref[...] += _colsum(dpo * pp)
        dpp = (dpo * ps_ref[...]).astype(BF16)
        ddp_ref[...] = _dot_nt(dpp, wp_ref[...])
        dwp_ref[...] += _dot_tn(dpb, dpp)
        ypre = yp_ref[...]
        yg = _gelu(ypre)
        ygb = yg.astype(BF16)
        s = _sigmoid(_dot(ygb, wg_ref[...]))
        dq = (dy2 * yg * s * (1.0 - s)).astype(BF16)
        dyg = dy2 * s + _dot_nt(dq, wg_ref[...])
        acc_g[...] += _dot_tn(ygb, dq)
        dyp_ref[...] = dyg * _gelu_grad(ypre)

        @pl.when(i == nt - 1)
        def _():
            dwo_ref[...] = acc_o[...].astype(BF16)
            dwg_ref[...] = acc_g[...].astype(BF16)

    row = lambda i: (i, 0)
    return _pcall(
        body, name="mix_out_bwd", grid=(nt,),
        out_shape=(jax.ShapeDtypeStruct((l, hm), F32), jax.ShapeDtypeStruct((l, hm), F32),
                   jax.ShapeDtypeStruct((d, d), BF16), jax.ShapeDtypeStruct((hm, hm), BF16),
                   jax.ShapeDtypeStruct((hm, hm), F32), jax.ShapeDtypeStruct((1, hm), F32),
                   jax.ShapeDtypeStruct((1, d), F32)),
        in_specs=[pl.BlockSpec((TML, d), row), pl.BlockSpec((TML, d), row), pl.BlockSpec((TML, d), row),
                  pl.BlockSpec((TML, hm), row), pl.BlockSpec((TML, hm), row), _mod_spec(d, False),
                  _const(w_pool.shape), _const((1, hm)), _const(w_glu.shape), _resident(w_out.shape)],
        out_specs=(pl.BlockSpec((TML, hm), row), pl.BlockSpec((TML, hm), row), _const((d, d)), _const((hm, hm)),
                   _const((hm, hm)), _const((1, hm)), _const((1, d))),
        scratch_shapes=[pltpu.VMEM((d, d), F32), pltpu.VMEM((hm, hm), F32)],
        compiler_params=_params(("arbitrary",)),
    )(dx2, mix, cat, ypre, dpool, modtab, w_pool, pscale, w_glu, w_out)


def _loss_head(x3, target, g):
    l, d = x3.shape

    def body(x_ref, t_ref, g_ref, dx_ref, loss_ref, dg_ref):
        i = pl.program_id(0)

        @pl.when(i == 0)
        def _():
            loss_ref[...] = jnp.zeros_like(loss_ref)
            dg_ref[...] = jnp.zeros_like(dg_ref)

        xx = x_ref[...]
        gg = g_ref[...]
        rstd = lax.rsqrt(jnp.mean(xx * xx, axis=-1, keepdims=True) + EPS)
        xhat = xx * rstd
        err = xhat * gg - t_ref[...]
        row_loss = jnp.mean(err * err, axis=-1, keepdims=True)
        loss_ref[...] += 0.5 * jnp.sum(row_loss, axis=0, keepdims=True)
        dout = err * (1.0 / d)
        dg_ref[...] += _colsum(dout * xhat)
        dxhat = dout * gg
        dx_ref[...] = rstd * (dxhat - xhat * jnp.mean(dxhat * xhat, axis=-1, keepdims=True))

    row = lambda i: (i, 0)
    return _pcall(
        body, name="loss_head", grid=(l // TML,),
        out_shape=(jax.ShapeDtypeStruct((l, d), F32), jax.ShapeDtypeStruct((1, LANE), F32),
                   jax.ShapeDtypeStruct((1, d), F32)),
        in_specs=[pl.BlockSpec((TML, d), row), pl.BlockSpec((TML, d), row), _const((1, d))],
        out_specs=(pl.BlockSpec((TML, d), row), _const((1, LANE)), _const((1, d))),
        compiler_params=_params(("arbitrary",)),
    )(x3, target, g)


def _adamw(parts, w, m, v, name):
    shape = w.shape
    c = shape[-1]
    r = int(np.prod(shape)) // c
    npart = parts.shape[0]
    tr = r
    for cand in (1024, 512, 256, 128, 64, 32, 16):
        if r % cand == 0 and cand * max(c, LANE) * 4 <= ADAM_BLOCK_BYTES:
            tr = cand
            break
    c1 = 1.0 - ADAM_B1
    c2 = 1.0 - ADAM_B2
    bc1 = 1.0 - ADAM_B1 ** ADAM_STEP
    bc2 = 1.0 - ADAM_B2 ** ADAM_STEP

    def body(p_ref, w_ref, m_ref, v_ref, g_ref, d_ref, mo_ref, vo_ref):
        g = p_ref[0].astype(F32)
        for k in range(1, npart):
            g = g + p_ref[k].astype(F32)
        mn = ADAM_B1 * m_ref[...] + c1 * g
        vn = ADAM_B2 * v_ref[...] + c2 * (g * g)
        m_hat = mn / bc1
        v_hat = vn / bc2
        g_ref[...] = g
        mo_ref[...] = mn
        vo_ref[...] = vn
        d_ref[...] = -ADAM_LR * (m_hat / (jnp.sqrt(v_hat) + ADAM_EPS) + ADAM_WD * w_ref[...])

    spec = pl.BlockSpec((tr, c), lambda i: (i, 0))
    outs = _pcall(
        body, name=name, grid=(r // tr,),
        out_shape=tuple(jax.ShapeDtypeStruct((r, c), F32) for _ in range(4)),
        in_specs=[pl.BlockSpec((npart, tr, c), lambda i: (0, i, 0)), spec, spec, spec],
        out_specs=(spec, spec, spec, spec),
        compiler_params=_params(("arbitrary",)),
    )(parts.reshape(npart, r, c), w.reshape(r, c), m.reshape(r, c), v.reshape(r, c))
    return tuple(o.reshape(shape) for o in outs)


def _perm(a):
    t, c = a.shape
    return a.reshape(t // TC, 8, SEG, c).transpose(0, 2, 1, 3).reshape(t, c)


def _unperm(a):
    t, c = a.shape
    return a.reshape(t // TC, SEG, 8, c).transpose(0, 2, 1, 3).reshape(t, c)


def _blockdiag(xb):
    n, a, b = xb.shape[-3:]
    eye = jnp.eye(n, dtype=xb.dtype)
    out = xb[..., :, :, None, :] * eye[:, None, :, None]
    return out.reshape(xb.shape[:-3] + (n * a, n * b))


def _diagblocks(mat, n):
    lead = mat.shape[:-2]
    a, b = mat.shape[-2] // n, mat.shape[-1] // n
    m5 = mat.reshape(lead + (n, a, n, b))
    nl = len(lead)
    dg = jnp.diagonal(m5, axis1=nl, axis2=nl + 2)
    return jnp.moveaxis(dg, -1, nl)


def _to_grid(a, rows):
    return a.reshape(rows, GRID_W, 4, LANE).transpose(2, 0, 1, 3).reshape(4, rows, GRID_W * LANE)


def _from_grid(a, rows):
    return a.reshape(4, rows, GRID_W, LANE).transpose(1, 2, 0, 3).reshape(rows * GRID_W, 4 * LANE)


def _pad128(a):
    flat = a.reshape(-1)
    pad = (-flat.shape[0]) % LANE
    return jnp.pad(flat, (0, pad)) if pad else flat


def kernel(x, c, ctx, c_ctx, norm_g, w_ada, b_ada, ffn_w1, ffn_w3, ffn_w2, w_in, pool_w, pool_scale, s5_a_re, s5_a_im, s5_log_dt, s5_b_re, s5_b_im, s5_c_re, s5_c_im, s5_d, s5_w_glu, w_out, final_g, loss_target, m_c_ctx, m_norm_g, m_w_ada, m_b_ada, m_ffn_w1, m_ffn_w3, m_ffn_w2, m_w_in, m_pool_w, m_pool_scale, m_s5_a_re, m_s5_a_im, m_s5_log_dt, m_s5_b_re, m_s5_b_im, m_s5_c_re, m_s5_c_im, m_s5_d, m_s5_w_glu, m_w_out, m_final_g, v_c_ctx, v_norm_g, v_w_ada, v_b_ada, v_ffn_w1, v_ffn_w3, v_ffn_w2, v_w_in, v_pool_w, v_pool_scale, v_s5_a_re, v_s5_a_im, v_s5_log_dt, v_s5_b_re, v_s5_b_im, v_s5_c_re, v_s5_c_im, v_s5_d, v_s5_w_glu, v_w_out, v_final_g):
    weights = dict(c_ctx=c_ctx, norm_g=norm_g, w_ada=w_ada, b_ada=b_ada, ffn_w1=ffn_w1, ffn_w3=ffn_w3, ffn_w2=ffn_w2,
                   w_in=w_in, pool_w=pool_w, pool_scale=pool_scale, s5_a_re=s5_a_re, s5_a_im=s5_a_im,
                   s5_log_dt=s5_log_dt, s5_b_re=s5_b_re, s5_b_im=s5_b_im, s5_c_re=s5_c_re, s5_c_im=s5_c_im, s5_d=s5_d,
                   s5_w_glu=s5_w_glu, w_out=w_out, final_g=final_g)
    mom_m = dict(c_ctx=m_c_ctx, norm_g=m_norm_g, w_ada=m_w_ada, b_ada=m_b_ada, ffn_w1=m_ffn_w1, ffn_w3=m_ffn_w3,
                 ffn_w2=m_ffn_w2, w_in=m_w_in, pool_w=m_pool_w, pool_scale=m_pool_scale, s5_a_re=m_s5_a_re,
                 s5_a_im=m_s5_a_im, s5_log_dt=m_s5_log_dt, s5_b_re=m_s5_b_re, s5_b_im=m_s5_b_im, s5_c_re=m_s5_c_re,
                 s5_c_im=m_s5_c_im, s5_d=m_s5_d, s5_w_glu=m_s5_w_glu, w_out=m_w_out, final_g=m_final_g)
    mom_v = dict(c_ctx=v_c_ctx, norm_g=v_norm_g, w_ada=v_w_ada, b_ada=v_b_ada, ffn_w1=v_ffn_w1, ffn_w3=v_ffn_w3,
                 ffn_w2=v_ffn_w2, w_in=v_w_in, pool_w=v_pool_w, pool_scale=v_pool_scale, s5_a_re=v_s5_a_re,
                 s5_a_im=v_s5_a_im, s5_log_dt=v_s5_log_dt, s5_b_re=v_s5_b_re, s5_b_im=v_s5_b_im, s5_c_re=v_s5_c_re,
                 s5_c_im=v_s5_c_im, s5_d=v_s5_d, s5_w_glu=v_s5_w_glu, w_out=v_w_out, final_g=v_final_g)

    l, d = x.shape[1], x.shape[2]
    lc = ctx.shape[1]
    t = l + lc
    rows = l // GRID_W
    fblk = ffn_w1.shape[-1]
    ngrp, gp = s5_a_re.shape[2], s5_a_re.shape[3]
    gc = s5_b_re.shape[3]
    hm = ngrp * gc
    nsb = 4
    gsb = ngrp // nsb
    assert lc == TM == TC and l % TML == 0 and hm == 4 * LANE and ngrp * gp == nsb * 512
    me = 4 * lax.axis_index("x") + 2 * lax.axis_index("y") + lax.axis_index("c")

    padc = ((0, 0), (0, 0), (0, FPAD - fblk))
    w13_loc = [jnp.concatenate([jnp.pad(ffn_w1[0, k], padc[1:]), jnp.pad(ffn_w3[0, k], padc[1:])], axis=-1).astype(BF16)
               for k in range(2)]
    w2_loc = [jnp.pad(ffn_w2[0, k], ((0, FPAD - fblk), (0, 0))).astype(BF16) for k in range(2)]
    gath = _exchange([(w13_loc[0], GATHER_REL), (w2_loc[0], GATHER_REL), (w_in[0].astype(BF16), GATHER),
                      (c, GATHER), (norm_g[0], GATHER)], "exchange_weights")
    w13_0 = gath[0]
    w2_0 = gath[1].reshape(NDEV // 2, 2 * FPAD, d)
    w_in_f = gath[2].reshape(d, -1)
    c_all = gath[3].reshape(NDEV, d)
    g_all = gath[4].transpose(1, 0, 2).reshape(3, d)

    wa = w_ada.shape[-1]
    c16 = jnp.concatenate([c_all, c_ctx.reshape(1, d), jnp.zeros((7, d), F32)], axis=0)
    b_blk = lax.dynamic_slice_in_dim(b_ada, me * wa, wa, axis=1)
    mod_sl = _ada_fwd(c16, w_ada[0], b_blk)
    gath = _exchange([(mod_sl, GATHER)], "exchange_mod")
    mod_full = gath[0].transpose(1, 0, 2).reshape(16, NDEV * wa)
    mod_l = lax.dynamic_index_in_dim(mod_full, me, axis=0, keepdims=False).reshape(9, d)
    modtab = jnp.stack([mod_full[8].reshape(9, d), mod_l])

    x0 = jnp.concatenate([ctx[0], x[0]], axis=0)
    g0, g1, g2 = g_all[0:1], g_all[1:2], g_all[2:3]
    x1, h0, ab0, o0, w13_1, w2_1, w_out_f, w_glu_f = _ffn_fwd(
        x0, modtab, g0, w13_0, w2_0, 0, True, "ffn0_fwd",
        xchg=[(w13_loc[1], GATHER_REL), (w2_loc[1], GATHER_REL), (w_out[0].astype(BF16), GATHER),
              (s5_w_glu[0].astype(BF16), GATHER)])
    w2_1 = w2_1.reshape(NDEV // 2, 2 * FPAD, d)
    w_out_f = w_out_f.reshape(-1, d)
    w_glu_f = w_glu_f.reshape(hm, hm)
    h1, u = _mix_in_fwd(x1, modtab, g1, w_in_f)

    v_grid = _to_grid(u[lc:, :hm], rows)
    dpool = _from_grid(_pool(v_grid, False, "pool_fwd"), rows)
    w_pool_bd = _blockdiag(pool_w[0]).astype(BF16)

    q = 2 * ngrp
    kvec = np.concatenate([np.arange(1, SEG + 1), SEG * np.arange(1, 9), SEG * np.arange(8, 0, -1)]).astype(np.float32)
    a_re3, a_im3 = s5_a_re[0].reshape(q, 1, gp), s5_a_im[0].reshape(q, 1, gp)
    ldt3 = jnp.broadcast_to(s5_log_dt[0].reshape(q, 1, 1), (q, 1, gp))
    c_re3, c_im3 = s5_c_re[0].reshape(q, gc, gp), s5_c_im[0].reshape(q, gc, gp)
    cp_re, cp_im, tab_re, tab_im = _s5_prep(a_re3, a_im3, ldt3, c_re3, c_im3, jnp.broadcast_to(jnp.asarray(kvec).reshape(1, NTAB, 1), (1, NTAB, gp)))
    tabs = [(tab_re.reshape(2, ngrp, NTAB, gp)[k].transpose(1, 0, 2).reshape(NTAB, ngrp * gp),
             tab_im.reshape(2, ngrp, NTAB, gp)[k].transpose(1, 0, 2).reshape(NTAB, ngrp * gp)) for k in range(2)]
    b_t = lambda b: _blockdiag(b[0].reshape(nsb, gsb, gp, gc).transpose(0, 1, 3, 2)).astype(BF16)
    bre, bim = b_t(s5_b_re), b_t(s5_b_im)
    cp_t = lambda cp: _blockdiag(cp.reshape(2, nsb, gsb, gc, gp)).astype(BF16)
    ctr, cti = cp_t(cp_re), cp_t(cp_im)
    cmr, cmi = jnp.swapaxes(ctr, -1, -2), jnp.swapaxes(cti, -1, -2)
    dskip = s5_d

    nch = t // TC
    order_f = lambda i: i
    order_b = lambda i: jnp.where(i == 0, 0, nch - i)
    rorder_f = lambda i: nch - 1 - i
    rorder_b = lambda i: jnp.where(i == nch - 1, 0, i + 1)
    up = _perm(u[:, hm:])
    yp_f, hr_f, hi_f, cinr_f, cini_f = _s5_fwd(up, bre, bim, cmr[0], cmi[0], tabs[0][0], tabs[0][1], dskip,
                                               order_f, False, True, "s5_fwd_f")
    yp_b, hr_b, hi_b, cinr_b, cini_b = _s5_fwd(up, bre, bim, cmr[1], cmi[1], tabs[1][0], tabs[1][1], dskip,
                                               order_b, True, False, "s5_fwd_b")
    y_f, y_b = _unperm(yp_f), _unperm(yp_b)

    x2, ypre, cat, mix = _mix_out_fwd(y_f, y_b, dpool, x1, modtab, w_pool_bd, pool_scale, w_glu_f, w_out_f)
    x3, h2, ab2, o2 = _ffn_fwd(x2, modtab, g2, w13_1, w2_1, 6, False, "ffn1_fwd")
    dx3, loss_part, dfinal_g = _loss_head(x3, loss_target[0], final_g.reshape(1, d))
    loss = lax.psum(loss_part[0, 0], AXES)

    dhp2, d13_1, d2_1 = _ffn_bwd(dx3, modtab, h2, ab2, w13_1, w2_1, 6, False, "ffn1_bwd")
    dx2, dmod_678, dg2 = _ffn_bwd_norm(dx3, [dhp2], x2, o2, modtab, g2, 6, False, "ffn1_bwd_norm")
    dypre, ddpool, dw_out, dw_glu, dw_pool_bd, dpscale, dgate5 = _mix_out_bwd(
        dx2, mix, cat, ypre, dpool, modtab, w_pool_bd, pool_scale, w_glu_f, w_out_f)

    dyp = _perm(jnp.concatenate([jnp.zeros((lc, hm), F32), dypre], axis=0))
    dup_f, dar_f, dai_f, dcr_f, dci_f, dbr_f, dbi_f, dd_skip, r13_1, r2_1 = _s5_bwd(
        dyp, up, hr_f, hi_f, cinr_f, cini_f, bre, bim, ctr[0], cti[0], tabs[0][0], tabs[0][1], dskip,
        rorder_f, False, True, "s5_bwd_f", xchg=[(d13_1, A2A_REL), (d2_1.reshape(NDEV, FPAD, d), A2A_REL)])
    dup_b, dar_b, dai_b, dcr_b, dci_b, dbr_b, dbi_b, _, r_out, r_glu = _s5_bwd(
        dyp, up, hr_b, hi_b, cinr_b, cini_b, bre, bim, ctr[1], cti[1], tabs[1][0], tabs[1][1], dskip,
        rorder_b, True, False, "s5_bwd_b",
        xchg=[(dw_out.reshape(NDEV, -1, d), A2A), (dw_glu.reshape(NDEV, hm // NDEV, hm), A2A)])
    du_f, du_b = _unperm(dup_f), _unperm(dup_b)

    dab_r = jnp.stack([dar_f, dar_b]).reshape(q, 1, gp)
    dab_i = jnp.stack([dai_f, dai_b]).reshape(q, 1, gp)
    dcp_r = _diagblocks(jnp.stack([dcr_f, dcr_b]), gsb).reshape(q, gc, gp)
    dcp_i = _diagblocks(jnp.stack([dci_f, dci_b]), gsb).reshape(q, gc, gp)
    ga_re, ga_im, gldt, gc_re, gc_im = _s5_param_bwd(a_re3, a_im3, ldt3, c_re3, c_im3, dab_r, dab_i, dcp_r, dcp_i)
    gb_re = (_diagblocks(dbr_f, gsb) + _diagblocks(dbr_b, gsb)).transpose(0, 1, 3, 2).reshape(ngrp, gp, gc)
    gb_im = (_diagblocks(dbi_f, gsb) + _diagblocks(dbi_b, gsb)).transpose(0, 1, 3, 2).reshape(ngrp, gp, gc)

    def pack(small):
        offs, pieces, off = {}, [], 0
        for k_, a_ in small.items():
            p_ = _pad128(a_.astype(F32))
            offs[k_] = (off, int(np.prod(a_.shape)))
            off += p_.shape[0]
            pieces.append(p_)
        return jnp.concatenate(pieces).reshape(1, off), offs

    gw_pool = _diagblocks(dw_pool_bd, 4)
    bundle_a, offs = pack(dict(pool_w=gw_pool, pool_scale=dpscale, s5_a_re=ga_re, s5_a_im=ga_im,
                               s5_log_dt=gldt[:, 0, 0], s5_b_re=gb_re, s5_b_im=gb_im, s5_c_re=gc_re, s5_c_im=gc_im,
                               s5_d=dd_skip, final_g=dfinal_g))

    du_pool = _from_grid(_pool(_to_grid(ddpool, rows), True, "pool_bwd"), rows)
    dx1, dw_in, dmod_34, dg1 = _mix_in_bwd(du_pool, du_f, du_b, dx2, h1, x1, modtab, g1, w_in_f)
    dhp0a, d13_0a, d2_0a, bund_a, r_in = _ffn_bwd(
        dx1, modtab, h0, ab0, w13_0, w2_0, 0, True, "ffn0_bwd_a", pair0=0, npair=2,
        xchg=[(bundle_a, GATHER), (dw_in.reshape(NDEV, d // NDEV, -1), A2A)])
    dhp0b, d13_0b, d2_0b, r13_0a, r2_0a = _ffn_bwd(
        dx1, modtab, h0, ab0, w13_0, w2_0, 0, True, "ffn0_bwd_b", pair0=2, npair=2,
        xchg=[(d13_0a, A2A_SAME), (d2_0a.reshape(NDEV // 2, FPAD, d), A2A_SAME)])
    dx0, dmod_012, dg0, r13_0b, r2_0b = _ffn_bwd_norm(
        dx1, [dhp0a, dhp0b], x0, o0, modtab, g0, 0, True, "ffn0_bwd_norm",
        xchg=[(d13_0b, A2A_OTHER), (d2_0b.reshape(NDEV // 2, FPAD, d), A2A_OTHER)])
    r13_0 = jnp.concatenate([r13_0a, r13_0b], axis=0)
    r2_0 = jnp.concatenate([r2_0a, r2_0b], axis=0)
    grad_x = dx0[lc:].reshape(1, l, d)

    dmod_c = jnp.concatenate([dmod_012[0], dmod_34[0], jnp.zeros((4, d), F32)], axis=0)
    dmod_l = jnp.concatenate([dmod_012[1], dmod_34[1], dgate5, dmod_678[0]], axis=0)
    bundle_b, offs_b = pack(dict(norm_g=jnp.concatenate([dg0, dg1, dg2], axis=0), dmod_l=dmod_l, dmod_c=dmod_c))
    bund_b = _exchange([(bundle_b, GATHER)], "exchange_small")[0]
    bunds = {**{k_: (bund_a.reshape(NDEV, -1), v_) for k_, v_ in offs.items()},
             **{k_: (bund_b.reshape(NDEV, -1), v_) for k_, v_ in offs_b.items()}}
    r13, r2 = (r13_0, r13_1), (r2_0, r2_1)

    def piece(name):
        b_, (o_, n_) = bunds[name]
        return b_[:, o_:o_ + n_]

    dl_all = lax.dynamic_slice_in_dim(piece("dmod_l"), me * wa, wa, axis=1)
    dc_all = lax.dynamic_slice_in_dim(piece("dmod_c"), me * wa, wa, axis=1)
    g_wada, gc_part = _ada_bwd(c16, w_ada[0], dl_all, dc_all)
    gc_all = _exchange([(gc_part, GATHER)], "exchange_cctx")[0]

    parts = {
        "c_ctx": gc_all.reshape(NDEV, d),
        "norm_g": lax.dynamic_slice_in_dim(piece("norm_g").reshape(NDEV, 3, d), me * (d // NDEV), d // NDEV,
                                           axis=2).reshape((NDEV,) + norm_g.shape),
        "w_ada": g_wada.reshape((1,) + w_ada.shape),
        "b_ada": jnp.concatenate([piece("dmod_l"), piece("dmod_c")], axis=0).reshape((2 * NDEV,) + b_ada.shape),
        "ffn_w1": jnp.stack([r13[0][:, :, :fblk], r13[1][:, :, :fblk]], axis=1).reshape((NDEV,) + ffn_w1.shape),
        "ffn_w3": jnp.stack([r13[0][:, :, FPAD:FPAD + fblk], r13[1][:, :, FPAD:FPAD + fblk]],
                            axis=1).reshape((NDEV,) + ffn_w3.shape),
        "ffn_w2": jnp.stack([r2[0][:, :fblk, :], r2[1][:, :fblk, :]], axis=1).reshape((NDEV,) + ffn_w2.shape),
        "w_in": r_in.reshape((NDEV,) + w_in.shape),
        "s5_w_glu": r_glu.reshape((NDEV,) + s5_w_glu.shape),
        "w_out": r_out.reshape((NDEV,) + w_out.shape),
    }
    for k_ in ("pool_w", "pool_scale", "s5_a_re", "s5_a_im", "s5_log_dt", "s5_b_re", "s5_b_im", "s5_c_re", "s5_c_im",
               "s5_d", "final_g"):
        parts[k_] = piece(k_).reshape((NDEV,) + weights[k_].shape)

    grads, deltas, new_m, new_v = [], [], [], []
    for k_ in weights:
        g_, d_, m_, v_ = _adamw(parts[k_], weights[k_], mom_m[k_], mom_v[k_], "adamw_" + k_)
        grads.append(g_)
        deltas.append(d_)
        new_m.append(m_)
        new_v.append(v_)
    return (loss, grad_x, *grads, *deltas, *new_m, *new_v)
```

```python
import functools
import math

import numpy as np
import jax
import jax.numpy as jnp
from jax import lax
from jax.experimental import pallas as pl
from jax.experimental.pallas import tpu as pltpu

F32 = jnp.float32
BF16 = jnp.bfloat16
AXES = ("x", "y", "c")
NDEV = 8
EPS = 1e-6
TM = 256
TML = 512
TC = 256
SEG = TC // 8
NTAB = SEG + 16
S5_UNROLL = True
GRID_W = 64
POOL_WINDOWS = (2, 4, 8, 16)
LANE = 128
FPAD = 384
VMEM_LIMIT = 56 * 1024 * 1024
ADAM_BLOCK_BYTES = 512 * 1024
ADAM_LR, ADAM_B1, ADAM_B2, ADAM_EPS, ADAM_WD, ADAM_STEP = 0.001, 0.9, 0.999, 1e-08, 0.01, 10


def _raw_call(body, kw):
    return pl.pallas_call(body, **kw)


def _pcall(body, xchg=None, edges=None, **kw):
    extra = ()
    if xchg:
        arrs, kinds = [a for a, _ in xchg], [k for _, k in xchg]
        n = len(arrs)
        n_in, n_out, n_scr = len(kw["in_specs"]), len(kw["out_shape"]), len(kw.get("scratch_shapes", ()))
        inner = body

        def hosted(*refs):
            a, b = n_in, n_in + n
            c_, d_ = b + n_out, b + n_out + n
            e = d_ + n_scr
            first, mid, last = edges()

            @pl.when(first)
            def _():
                _xchg_start(refs[a:b], refs[c_:d_], refs[e:], kinds)

            inner(*refs[:a], *refs[b:c_], *refs[d_:e])

            if any(k == GATHER_REL for k in kinds):
                @pl.when(mid)
                def _():
                    _xchg_relay(refs[a:b], refs[c_:d_], refs[e:], kinds)

            @pl.when(last)
            def _():
                _xchg_finish(refs[a:b], refs[c_:d_], refs[e:], kinds)

        body = hosted
        any_spec = pl.BlockSpec(memory_space=pl.ANY)
        kw = dict(kw, in_specs=list(kw["in_specs"]) + [any_spec] * n,
                  out_shape=tuple(kw["out_shape"]) + _xchg_shapes(arrs, kinds),
                  out_specs=tuple(kw["out_specs"]) + (any_spec,) * n,
                  scratch_shapes=list(kw.get("scratch_shapes", ())) + _xchg_sems(n))
        extra = tuple(arrs)
    call = _raw_call(body, kw)
    return (lambda *args: call(*args, *extra)) if extra else call


def _params(sem):
    return pltpu.CompilerParams(dimension_semantics=sem, vmem_limit_bytes=VMEM_LIMIT)


def _dot(a, b):
    return jnp.dot(a, b, preferred_element_type=F32)


def _dot_nt(a, b):
    return lax.dot_general(a, b, (((1,), (1,)), ((), ())), preferred_element_type=F32)


def _dot_tn(a, b):
    return lax.dot_general(a, b, (((0,), (0,)), ((), ())), preferred_element_type=F32)


def _sigmoid(x):
    return 1.0 / (1.0 + jnp.exp(-x))


def _colsum(a):
    return jnp.sum(a, axis=0, keepdims=True)


def _resident(shape):
    nd = len(shape)
    return pl.BlockSpec(shape, lambda *_: (0,) * nd, pipeline_mode=pl.Buffered(1))


def _const(shape):
    nd = len(shape)
    return pl.BlockSpec(shape, lambda *_: (0,) * nd)


GATHER = "gather"
GATHER_REL = "gather_rel"
A2A = "a2a"
A2A_REL = "a2a_rel"
A2A_SAME = "a2a_same"
A2A_OTHER = "a2a_other"


def _exchange(items, name):
    arrs, kinds = [a for a, _ in items], [k for _, k in items]
    n = len(arrs)

    def body(*refs):
        _xchg_start(refs[:n], refs[n:2 * n], refs[2 * n:], kinds)
        _xchg_relay(refs[:n], refs[n:2 * n], refs[2 * n:], kinds)
        _xchg_finish(refs[:n], refs[n:2 * n], refs[2 * n:], kinds)

    any_spec = pl.BlockSpec(memory_space=pl.ANY)
    outs = _pcall(
        body, name=name, out_shape=_xchg_shapes(arrs, kinds), in_specs=[any_spec] * n, out_specs=[any_spec] * n,
        scratch_shapes=_xchg_sems(n),
    )(*arrs)
    return list(outs)


def _edges1(n0):
    return lambda: (pl.program_id(0) == 0, pl.program_id(0) == (3 * n0) // 4, pl.program_id(0) == n0 - 1)


def _edges2(n0, n1):
    def edges():
        step = pl.program_id(0) * n1 + pl.program_id(1)
        return step == 0, step == (3 * n0 * n1) // 4, step == n0 * n1 - 1
    return edges


def _xchg_shapes(arrs, kinds):
    return tuple(jax.ShapeDtypeStruct(((NDEV,) if k in (GATHER, GATHER_REL) else ()) + tuple(a.shape), a.dtype)
                 for a, k in zip(arrs, kinds))


def _xchg_sems(n):
    return [pltpu.SemaphoreType.DMA((n * (NDEV - 1),)), pltpu.SemaphoreType.DMA((n * (NDEV - 1),)),
            pltpu.SemaphoreType.DMA((n,))]


def _xchg_plan(in_refs, out_refs, sems, kinds):
    send_sems, recv_sems, loc_sems = sems
    x, y, c = (lax.axis_index(a) for a in AXES)
    my_abs, my_chip = 4 * x + 2 * y + c, 2 * x + y

    def peer(p):
        px = 1 - x if p & 4 else x
        py = 1 - y if p & 2 else y
        pc = 1 - c if p & 1 else c
        return (px, py, pc), 4 * px + 2 * py + pc, 2 * px + py

    starts, relays, recvs = [], [], []
    for k, kind in enumerate(kinds):
        src, dst = in_refs[k], out_refs[k]

        def remote(src_ref, row, p, pair, dst=dst, k=k):
            dev, sem = peer(p)[0], k * (NDEV - 1) + pair - 1
            return lambda: pltpu.make_async_remote_copy(
                src_ref=src_ref, dst_ref=dst.at[row], send_sem=send_sems.at[sem], recv_sem=recv_sems.at[sem],
                device_id=dev, device_id_type=pl.DeviceIdType.MESH)

        def local(src_ref, row, dst=dst, k=k):
            return lambda: pltpu.make_async_copy(src_ref, dst.at[row], loc_sems.at[k])

        if kind == GATHER:
            for p in range(1, NDEV):
                starts.append((remote(src, my_abs, p, p), False))
                recvs.append(remote(src, peer(p)[1], p, p))
            starts.append((local(src, my_abs), True))
        elif kind == GATHER_REL:
            for p in (1, 4, 2, 6):
                starts.append((remote(src, 4 * (p & 1) + my_chip, p, p), False))
            for q in (4, 2, 6):
                pchip = peer(q)[2]
                relays.append((remote(src, pchip, q, q), remote(dst.at[pchip], 4 + pchip, 1, q | 1)))
                recvs.append(remote(src, 4 + pchip, 1, q | 1))
            recvs.append(remote(src, 4 + my_chip, 1, 1))
            starts.append((local(src, my_chip), True))
        elif kind in (A2A, A2A_REL):
            for p in range(1, NDEV):
                _, pabs, pchip = peer(p)
                theirs, mine = (pabs, my_abs) if kind == A2A else (4 * (p & 1) + pchip, 4 * (p & 1) + my_chip)
                starts.append((remote(src.at[theirs], mine, p, p), False))
                recvs.append(remote(src.at[theirs], theirs, p, p))
            own = my_abs if kind == A2A else my_chip
            starts.append((local(src.at[own], own), True))
        else:
            for p in ((4, 2, 6) if kind == A2A_SAME else (1, 5, 3, 7)):
                pchip = peer(p)[2]
                starts.append((remote(src.at[pchip], my_chip, p, p), False))
                recvs.append(remote(src.at[pchip], pchip, p, p))
            if kind == A2A_SAME:
                starts.append((local(src.at[my_chip], my_chip), True))
    return starts, relays, recvs


def _xchg_start(in_refs, out_refs, sems, kinds):
    for make, _ in _xchg_plan(in_refs, out_refs, sems, kinds)[0]:
        make().start()


def _xchg_relay(in_refs, out_refs, sems, kinds):
    for arrival, forward in _xchg_plan(in_refs, out_refs, sems, kinds)[1]:
        arrival().wait_recv()
        forward().start()


def _xchg_finish(in_refs, out_refs, sems, kinds):
    starts, relays, recvs = _xchg_plan(in_refs, out_refs, sems, kinds)
    for make in recvs:
        make().wait_recv()
    for make, is_local in starts:
        if is_local:
            make().wait()
        else:
            make().wait_send()
    for _, forward in relays:
        forward().wait_send()


def _ada_fwd(c16, w_blk, b_blk):
    def body(c_ref, w_ref, b_ref, o_ref):
        cc = c_ref[...]
        s = (cc * _sigmoid(cc)).astype(BF16)
        o_ref[...] = _dot(s, w_ref[...].astype(BF16)) + b_ref[...]

    return _pcall(body, name="ada_fwd", out_shape=jax.ShapeDtypeStruct((16, w_blk.shape[1]), F32),
                  compiler_params=_params(None))(c16, w_blk, b_blk)


def _ada_bwd(c16, w_blk, dl, dc):
    d, w = w_blk.shape

    def body(c_ref, w_ref, dl_ref, dc_ref, gw_ref, gc_ref):
        cc = c_ref[...]
        sg = _sigmoid(cc)
        s = (cc * sg).astype(BF16)
        dctot = _colsum(dc_ref[...])
        dm = jnp.concatenate([dl_ref[...], jnp.broadcast_to(dctot, (8, w))], axis=0)
        rows = lax.broadcasted_iota(jnp.int32, (16, w), 0)
        dm = jnp.where(rows <= 8, dm, 0.0).astype(BF16)
        gw_ref[...] = _dot_tn(s, dm)
        t = _dot_nt(jnp.broadcast_to(dctot, (8, w)).astype(BF16), w_ref[...].astype(BF16))[0:1]
        c8, s8 = cc[8:9], sg[8:9]
        gc_ref[...] = t * (s8 * (1.0 + c8 * (1.0 - s8)))

    return _pcall(body, name="ada_bwd",
                  out_shape=(jax.ShapeDtypeStruct((d, w), F32), jax.ShapeDtypeStruct((1, d), F32)),
                  compiler_params=_params(None))(c16, w_blk, dl, dc)


def _norm_fwd(x, g, shift, scale):
    rstd = lax.rsqrt(jnp.mean(x * x, axis=-1, keepdims=True) + EPS)
    xhat = x * rstd
    xn = xhat * g
    return rstd, xhat, xn, xn * (1.0 + scale) + shift


def _norm_bwd(dh, rstd, xhat, xn, g, scale):
    dxn = dh * (1.0 + scale)
    dxhat = dxn * g
    dx = rstd * (dxhat - xhat * jnp.mean(dxhat * xhat, axis=-1, keepdims=True))
    return dx, _colsum(dh), _colsum(dh * xn), _colsum(dxn * xhat)


def _mod_spec(d, has_ctx):
    if has_ctx:
        return pl.BlockSpec((1, 9, d), lambda *ids: (jnp.minimum(ids[-1], 1), 0, 0))
    return pl.BlockSpec((1, 9, d), lambda *ids: (1, 0, 0))


def _row_specs(nx, tm, d):
    if nx == 1:
        return [pl.BlockSpec((tm, d), lambda i: (i, 0))]
    return [pl.BlockSpec((tm, d), lambda i: (0, 0)), pl.BlockSpec((tm, d), lambda i: (jnp.maximum(i - 1, 0), 0))]


def _rows(refs):
    if len(refs) == 1:
        return refs[0][...]
    return jnp.where(pl.program_id(0) == 0, refs[0][...], refs[1][...])


def _tile(has_ctx):
    return TM if has_ctx else TML


def _dmod_spec(nrow, d, has_ctx):
    if has_ctx:
        return pl.BlockSpec((1, nrow, d), lambda i: (jnp.minimum(i, 1), 0, 0))
    return pl.BlockSpec((1, nrow, d), lambda i: (0, 0, 0))


def _ffn_fwd(x, modtab, g, w13, w2, k0, has_ctx, name, xchg=None):
    t, d = sum(a.shape[0] for a in x), x[0].shape[1]
    nb, _, w = w13.shape
    fp = w // 2
    tm = _tile(has_ctx)
    nx = len(x)

    def body(*refs):
        mod_ref, g_ref, w13_ref, w2_ref, xo_ref, h_ref, ab_ref, o_ref = refs[nx:]
        xx = _rows(refs[:nx])
        shift, scale, gate = mod_ref[0, k0:k0 + 1, :], mod_ref[0, k0 + 1:k0 + 2, :], mod_ref[0, k0 + 2:k0 + 3, :]
        _, _, _, h = _norm_fwd(xx, g_ref[...], shift, scale)
        hb = h.astype(BF16)
        h_ref[...] = hb
        acc = jnp.zeros((tm, d), F32)
        for p in range(nb // 2):
            zs = []
            for q in range(2):
                blk = 2 * p + q
                ab = _dot(hb, w13_ref[blk])
                ab_ref[:, blk * w:(blk + 1) * w] = ab.astype(BF16)
                a, b = ab[:, :fp], ab[:, fp:]
                zs.append((a * _sigmoid(a) * b).astype(BF16))
            acc = acc + _dot(jnp.concatenate(zs, axis=1), w2_ref[p])
        o_ref[...] = acc.astype(BF16)
        xo_ref[...] = xx + (0.5 * gate) * acc

    row = lambda i: (i, 0)
    return _pcall(
        body, name=name, grid=(t // tm,), xchg=xchg, edges=_edges1(t // tm),
        out_shape=(jax.ShapeDtypeStruct((t, d), F32), jax.ShapeDtypeStruct((t, d), BF16),
                   jax.ShapeDtypeStruct((t, nb * w), BF16), jax.ShapeDtypeStruct((t, d), BF16)),
        in_specs=_row_specs(nx, tm, d) + [_mod_spec(d, has_ctx), _const((1, d)),
                                          _resident(w13.shape), _resident(w2.shape)],
        out_specs=(pl.BlockSpec((tm, d), row), pl.BlockSpec((tm, d), row), pl.BlockSpec((tm, nb * w), row),
                   pl.BlockSpec((tm, d), row)),
        compiler_params=_params(("arbitrary",)),
    )(*x, modtab, g, w13, w2)


def _ffn_bwd(dy, modtab, h, ab, w13, w2, k0, has_ctx, name, pair0=0, npair=None, xchg=None):
    t, d = dy.shape
    _, _, w = w13.shape
    fp = w // 2
    npair = w13.shape[0] // 2 if npair is None else npair
    nb = 2 * npair
    tm = _tile(has_ctx)
    nt = t // tm

    def body(dy_ref, mod_ref, h_ref, ab_ref, w13_ref, w2_ref, dh_ref, d13_ref, d2_ref, acc13, acc2):
        i = pl.program_id(1)

        @pl.when(i == 0)
        def _():
            acc13[...] = jnp.zeros_like(acc13)
            acc2[...] = jnp.zeros_like(acc2)

        gate = mod_ref[0, k0 + 2:k0 + 3, :]
        do = (dy_ref[...] * (0.5 * gate)).astype(BF16)
        dz = _dot_nt(do, w2_ref[0])
        hb = h_ref[...]
        dh = jnp.zeros((tm, d), F32)
        zs = []
        for q in range(2):
            ab = ab_ref[:, q * w:(q + 1) * w].astype(F32)
            a, b = ab[:, :fp], ab[:, fp:]
            sg = _sigmoid(a)
            sa = a * sg
            dzq = dz[:, q * fp:(q + 1) * fp]
            da = dzq * b * (sg * (1.0 + a * (1.0 - sg)))
            db = dzq * sa
            dab = jnp.concatenate([da, db], axis=1).astype(BF16)
            dh = dh + _dot_nt(dab, w13_ref[q])
            acc13[q] += _dot_tn(hb, dab)
            zs.append((sa * b).astype(BF16))
        acc2[...] += _dot_tn(jnp.concatenate(zs, axis=1), do)
        dh_ref[0] = dh.astype(BF16)

        @pl.when(i == nt - 1)
        def _():
            d13_ref[...] = acc13[...].astype(BF16)
            d2_ref[0] = acc2[...].astype(BF16)

    mod_spec = _mod_spec(d, has_ctx)
    return _pcall(
        body, name=name, grid=(npair, nt), xchg=xchg, edges=_edges2(npair, nt),
        out_shape=(jax.ShapeDtypeStruct((npair, t, d), BF16), jax.ShapeDtypeStruct((nb, d, w), BF16),
                   jax.ShapeDtypeStruct((npair, 2 * fp, d), BF16)),
        in_specs=[pl.BlockSpec((tm, d), lambda p, i: (i, 0)), mod_spec, pl.BlockSpec((tm, d), lambda p, i: (i, 0)),
                  pl.BlockSpec((tm, 2 * w), lambda p, i: (i, pair0 + p)),
                  pl.BlockSpec((2, d, w), lambda p, i: (pair0 + p, 0, 0)),
                  pl.BlockSpec((1, 2 * fp, d), lambda p, i: (pair0 + p, 0, 0))],
        out_specs=(pl.BlockSpec((1, tm, d), lambda p, i: (p, i, 0)), pl.BlockSpec((2, d, w), lambda p, i: (p, 0, 0)),
                   pl.BlockSpec((1, 2 * fp, d), lambda p, i: (p, 0, 0))),
        scratch_shapes=[pltpu.VMEM((2, d, w), F32), pltpu.VMEM((2 * fp, d), F32)],
        compiler_params=_params(("arbitrary", "arbitrary")),
    )(dy, modtab, h, ab, w13, w2)


def _ffn_bwd_norm(dy, dhps, x, o, modtab, g, k0, has_ctx, name, xchg=None):
    t, d = dy.shape
    ngrp = 2 if has_ctx else 1
    tm = _tile(has_ctx)
    ndh, nx = len(dhps), len(x)
    lat0 = ngrp - 1

    def body(dy_ref, *rest):
        dhp_refs, x_refs = rest[:ndh], rest[ndh:ndh + nx]
        o_ref, mod_ref, g_ref, dx_ref, dmod_ref, dg_ref = rest[ndh + nx:]
        i = pl.program_id(0)

        @pl.when(i == 0)
        def _():
            dg_ref[...] = jnp.zeros_like(dg_ref)

        @pl.when((i == 0) | (i == ngrp - 1))
        def _():
            dmod_ref[...] = jnp.zeros_like(dmod_ref)

        dh = None
        for ref in dhp_refs:
            for p in range(ref.shape[0]):
                dh = ref[p].astype(F32) if dh is None else dh + ref[p].astype(F32)
        scale = mod_ref[0, k0 + 1:k0 + 2, :]
        gg = g_ref[...]
        rstd, xhat, xn, _ = _norm_fwd(_rows(x_refs), gg, 0.0, scale)
        dxn, dshift, dscale, dg = _norm_bwd(dh, rstd, xhat, xn, gg, scale)
        dyv = dy_ref[...]
        dx_ref[...] = dyv + dxn
        dmod_ref[0, 0:1, :] += dshift
        dmod_ref[0, 1:2, :] += dscale
        dmod_ref[0, 2:3, :] += _colsum(0.5 * dyv * o_ref[...].astype(F32))
        dg_ref[...] += dg

    row = lambda i: (i, 0)
    return _pcall(
        body, name=name, grid=(t // tm,), xchg=xchg, edges=_edges1(t // tm),
        out_shape=(jax.ShapeDtypeStruct((t - lat0 * tm, d), F32), jax.ShapeDtypeStruct((ngrp, 3, d), F32),
                   jax.ShapeDtypeStruct((1, d), F32)),
        in_specs=[pl.BlockSpec((tm, d), row)]
        + [pl.BlockSpec((a.shape[0], tm, d), lambda i: (0, i, 0)) for a in dhps]
        + _row_specs(nx, tm, d) + [pl.BlockSpec((tm, d), row), _mod_spec(d, has_ctx), _const((1, d))],
        out_specs=(pl.BlockSpec((tm, d), lambda i: (jnp.maximum(i - lat0, 0), 0)), _dmod_spec(3, d, has_ctx),
                   _const((1, d))),
        compiler_params=_params(("arbitrary",)),
    )(dy, *dhps, *x, o, modtab, g)


def _mix_in_fwd(x1, modtab, g, w_in):
    t, d = x1.shape

    def body(x_ref, mod_ref, g_ref, w_ref, h_ref, u_ref):
        _, _, _, h = _norm_fwd(x_ref[...], g_ref[...], mod_ref[0, 3:4, :], mod_ref[0, 4:5, :])
        hb = h.astype(BF16)
        h_ref[...] = hb
        u_ref[...] = _dot(hb, w_ref[...])

    row = lambda i: (i, 0)
    return _pcall(
        body, name="mix_in_fwd", grid=(t // TM,),
        out_shape=(jax.ShapeDtypeStruct((t, d), BF16), jax.ShapeDtypeStruct((t, w_in.shape[1]), F32)),
        in_specs=[pl.BlockSpec((TM, d), row), _mod_spec(d, True), _const((1, d)), _resident(w_in.shape)],
        out_specs=(pl.BlockSpec((TM, d), row), pl.BlockSpec((TM, w_in.shape[1]), row)),
        compiler_params=_params(("arbitrary",)),
    )(x1, modtab, g, w_in)


def _mix_in_bwd(du_pool, du_f, du_b, dx2, h1, x1, modtab, g, w_in):
    t, d = x1.shape
    m = w_in.shape[1]
    hm = m // 2
    nt = t // TM

    def body(dup_ref, duf_ref, dub_ref, dx2_ref, h_ref, x_ref, mod_ref, g_ref, w_ref, dx_ref, dw_ref, dmod_ref, dg_ref, acc):
        i = pl.program_id(0)

        @pl.when(i == 0)
        def _():
            acc[...] = jnp.zeros_like(acc)
            dg_ref[...] = jnp.zeros_like(dg_ref)

        @pl.when(i <= 1)
        def _():
            dmod_ref[...] = jnp.zeros_like(dmod_ref)

        lat = (i > 0).astype(F32)
        du = jnp.concatenate([dup_ref[...] * lat, duf_ref[...] + dub_ref[...]], axis=1).astype(BF16)
        dh = _dot_nt(du, w_ref[...])
        acc[...] += _dot_tn(h_ref[...], du)
        scale = mod_ref[0, 4:5, :]
        gg = g_ref[...]
        rstd, xhat, xn, _ = _norm_fwd(x_ref[...], gg, 0.0, scale)
        dxn, dshift, dscale, dg = _norm_bwd(dh, rstd, xhat, xn, gg, scale)
        dx_ref[...] = dx2_ref[...] * lat + dxn
        dmod_ref[0, 0:1, :] += dshift
        dmod_ref[0, 1:2, :] += dscale
        dg_ref[...] += dg

        @pl.when(i == nt - 1)
        def _():
            dw_ref[...] = acc[...].astype(BF16)

    row = lambda i: (i, 0)
    lrow = lambda i: (jnp.maximum(i - 1, 0), 0)
    return _pcall(
        body, name="mix_in_bwd", grid=(nt,),
        out_shape=(jax.ShapeDtypeStruct((t, d), F32), jax.ShapeDtypeStruct((d, m), BF16),
                   jax.ShapeDtypeStruct((2, 2, d), F32), jax.ShapeDtypeStruct((1, d), F32)),
        in_specs=[pl.BlockSpec((TM, hm), lrow), pl.BlockSpec((TM, hm), row), pl.BlockSpec((TM, hm), row),
                  pl.BlockSpec((TM, d), lrow), pl.BlockSpec((TM, d), row), pl.BlockSpec((TM, d), row),
                  _mod_spec(d, True), _const((1, d)), _resident(w_in.shape)],
        out_specs=(pl.BlockSpec((TM, d), row), _const((d, m)), _dmod_spec(2, d, True), _const((1, d))),
        scratch_shapes=[pltpu.VMEM((d, m), F32)],
        compiler_params=_params(("arbitrary",)),
    )(du_pool, du_f, du_b, dx2, h1, x1, modtab, g, w_in)


def _pool(v, transpose, name):
    ngrp, rows, n = v.shape
    nchunk = 4
    cw = n // nchunk

    def rowsum(val, lo, hi):
        ri = lax.broadcasted_iota(jnp.int32, (rows, rows), 0)
        ci = lax.broadcasted_iota(jnp.int32, (rows, rows), 1)
        sel = (((ci - ri) >= -lo) & ((ci - ri) <= hi)).astype(BF16)
        v1 = val.astype(BF16)
        r1 = val - v1.astype(F32)
        v2 = r1.astype(BF16)
        v3 = (r1 - v2.astype(F32)).astype(BF16)
        return _dot(sel, v1) + _dot(sel, v2) + _dot(sel, v3)

    def one(v_ref, o_ref, r_scr, a_scr, win):
        lo = win // 2
        hi = win - 1 - lo
        rlo, rhi = (hi, lo) if transpose else (lo, hi)
        ridx = lax.broadcasted_iota(jnp.int32, (rows, 1), 0)
        cnt_r = (jnp.minimum(ridx + hi + 1, rows) - jnp.maximum(ridx - lo, 0)).astype(F32)
        cidx = lax.broadcasted_iota(jnp.int32, (1, n), 1) // LANE
        cnt_c = (jnp.minimum(cidx + hi + 1, GRID_W) - jnp.maximum(cidx - lo, 0)).astype(F32)
        if not transpose:
            for k in range(nchunk):
                sl = slice(k * cw, (k + 1) * cw)
                r_scr[:, sl] = rowsum(v_ref[0, :, sl], rlo, rhi) / cnt_r
        else:
            r_scr[...] = v_ref[0] / cnt_c
        a_scr[...] = r_scr[...]
        for j in range(-rlo, rhi + 1):
            if j == 0:
                continue
            c0, c1 = max(0, -j), min(GRID_W, GRID_W - j)
            a_scr[:, c0 * LANE:c1 * LANE] += r_scr[:, (c0 + j) * LANE:(c1 + j) * LANE]
        if not transpose:
            o_ref[0] = a_scr[...] / cnt_c - v_ref[0]
        else:
            for k in range(nchunk):
                sl = slice(k * cw, (k + 1) * cw)
                o_ref[0, :, sl] = rowsum(a_scr[:, sl] / cnt_r, rlo, rhi) - v_ref[0, :, sl]

    def body(v_ref, o_ref, r_scr, a_scr):
        grp = pl.program_id(0)
        for k, win in enumerate(POOL_WINDOWS):
            @pl.when(grp == k)
            def _(win=win):
                one(v_ref, o_ref, r_scr, a_scr, win)

    spec = pl.BlockSpec((1, rows, n), lambda k: (k, 0, 0))
    return _pcall(
        body, name=name, grid=(ngrp,), out_shape=jax.ShapeDtypeStruct(v.shape, F32),
        in_specs=[spec], out_specs=spec,
        scratch_shapes=[pltpu.VMEM((rows, n), F32), pltpu.VMEM((rows, n), F32)],
        compiler_params=_params(("arbitrary",)),
    )(v)


def _cmul(ar, ai, br, bi):
    return ar * br - ai * bi, ar * bi + ai * br


def _s5_prep(a_re, a_im, log_dt, c_re, c_im, kvec):
    q, nc, p = c_re.shape

    def body(ar_ref, ai_ref, ldt_ref, cr_ref, ci_ref, k_ref, cpr_ref, cpi_ref, tr_ref, ti_ref):
        lr, li = ar_ref[...], ai_ref[...]
        dt = jnp.exp(ldt_ref[...])
        kk = k_ref[...]
        mag = jnp.exp(kk * (lr * dt))
        ph = kk * (li * dt)
        tr_ref[...] = mag * jnp.cos(ph)
        ti_ref[...] = mag * jnp.sin(ph)
        m1 = jnp.exp(lr * dt)
        abr, abi = m1 * jnp.cos(li * dt), m1 * jnp.sin(li * dt)
        den = lr * lr + li * li
        xr, xi = abr - 1.0, abi
        cfr, cfi = (xr * lr + xi * li) / den, (xi * lr - xr * li) / den
        pr, pi_ = _cmul(cr_ref[...], ci_ref[...], cfr, cfi)
        cpr_ref[...] = pr
        cpi_ref[...] = pi_

    return _pcall(
        body, name="s5_prep",
        out_shape=(jax.ShapeDtypeStruct((q, nc, p), F32), jax.ShapeDtypeStruct((q, nc, p), F32),
                   jax.ShapeDtypeStruct((q, NTAB, p), F32), jax.ShapeDtypeStruct((q, NTAB, p), F32)),
        compiler_params=_params(None),
    )(a_re, a_im, log_dt, c_re, c_im, kvec)


def _s5_param_bwd(a_re, a_im, log_dt, c_re, c_im, dab_r, dab_i, dcp_r, dcp_i):
    q, nc, p = c_re.shape

    def body(ar_ref, ai_ref, ldt_ref, cr_ref, ci_ref, dar_ref, dai_ref, dcr_ref, dci_ref,
             gar_ref, gai_ref, gdt_ref, gcr_ref, gci_ref):
        lr, li = ar_ref[...], ai_ref[...]
        dt = jnp.exp(ldt_ref[...])
        m1 = jnp.exp(lr * dt)
        abr, abi = m1 * jnp.cos(li * dt), m1 * jnp.sin(li * dt)
        den = lr * lr + li * li
        xr, xi = abr - 1.0, abi
        cfr, cfi = (xr * lr + xi * li) / den, (xi * lr - xr * li) / den
        cr, ci = cr_ref[...], ci_ref[...]
        dcr, dci = dcr_ref[...], dci_ref[...]
        gcr, gci = _cmul(dcr, dci, cfr, -cfi)
        gcr_ref[...] = gcr
        gci_ref[...] = gci
        t_r, t_i = _cmul(cr, -ci, dcr, dci)
        dcf_r = jnp.sum(t_r, axis=1, keepdims=True)
        dcf_i = jnp.sum(t_i, axis=1, keepdims=True)
        ilr, ili = lr / den, -li / den
        u_r, u_i = _cmul(dcf_r, dcf_i, ilr, -ili)
        dab_r_, dab_i_ = dar_ref[...] + u_r, dai_ref[...] + u_i
        v_r, v_i = _cmul(dab_r_, dab_i_, abr, -abi)
        cl_r, cl_i = _cmul(cfr, cfi, ilr, ili)
        w_r, w_i = _cmul(dcf_r, dcf_i, cl_r, -cl_i)
        gar_ref[...] = v_r * dt - w_r
        gai_ref[...] = v_i * dt - w_i
        la_r, la_i = _cmul(lr, li, abr, abi)
        ddt = la_r * dab_r_ + la_i * dab_i_
        gdt_ref[...] = dt * jnp.sum(ddt, axis=2, keepdims=True)

    return _pcall(
        body, name="s5_param_bwd",
        out_shape=(jax.ShapeDtypeStruct((q, 1, p), F32), jax.ShapeDtypeStruct((q, 1, p), F32),
                   jax.ShapeDtypeStruct((q, 1, p), F32), jax.ShapeDtypeStruct((q, nc, p), F32),
                   jax.ShapeDtypeStruct((q, nc, p), F32)),
        compiler_params=_params(None),
    )(a_re, a_im, log_dt, c_re, c_im, dab_r, dab_i, dcp_r, dcp_i)


def _bcast8(row, c):
    return jnp.broadcast_to(row, (8, c))


def _slab(j):
    return j * 8 if isinstance(j, int) else pl.multiple_of(j * 8, 8)


def _scan_head(xr, xi, tre_ref, tim_ref, car_r, car_i, desc, conj, cs):
    sg = -1.0 if conj else 1.0
    ar = _bcast8(tre_ref[0:1, :], cs)
    ai = sg * _bcast8(tim_ref[0:1, :], cs)

    def p1(jj, carry):
        hr, hi = carry
        off = _slab(SEG - 1 - jj if desc else jj)
        pr, pi_ = _cmul(ar, ai, hr, hi)
        nr = pr + xr[pl.ds(off, 8), :]
        ni = pi_ + xi[pl.ds(off, 8), :]
        xr[pl.ds(off, 8), :] = nr
        xi[pl.ds(off, 8), :] = ni
        return nr, ni

    zero = jnp.zeros((8, cs), F32)
    ir, ii = lax.fori_loop(0, SEG, p1, (zero, zero), unroll=S5_UNROLL)
    cin_r, cin_i = car_r[...], car_i[...]
    rows = lax.broadcasted_iota(jnp.int32, (8, cs), 0)
    for s, krow in ((1, SEG), (2, SEG + 1), (4, SEG + 3)):
        keep, sh = (rows < 8 - s, 8 - s) if desc else (rows >= s, s)
        sr = jnp.where(keep, pltpu.roll(ir, sh, 0), 0.0)
        si = jnp.where(keep, pltpu.roll(ii, sh, 0), 0.0)
        pr, pi_ = _cmul(tre_ref[krow:krow + 1, :], sg * tim_ref[krow:krow + 1, :], sr, si)
        ir, ii = ir + pr, ii + pi_
    q0 = SEG + 8 if desc else SEG
    pr, pi_ = _cmul(tre_ref[q0:q0 + 8, :], sg * tim_ref[q0:q0 + 8, :], cin_r, cin_i)
    fr, fi = ir + pr, ii + pi_
    keep, sh, edge = (rows < 7, 7, 0) if desc else (rows >= 1, 1, 7)
    cs_r = jnp.where(keep, pltpu.roll(fr, sh, 0), cin_r)
    cs_i = jnp.where(keep, pltpu.roll(fi, sh, 0), cin_i)
    car_r[...] = _bcast8(fr[edge:edge + 1, :], cs)
    car_i[...] = _bcast8(fi[edge:edge + 1, :], cs)
    return cs_r, cs_i, cin_r, cin_i


def _pow_row(tre_ref, tim_ref, j, desc, conj, cs):
    k = SEG - 1 - j if desc else j
    sg = -1.0 if conj else 1.0
    return _bcast8(tre_ref[pl.ds(k, 1), :], cs), sg * _bcast8(tim_ref[pl.ds(k, 1), :], cs)


def _s5_fwd(up, bre, bim, cmr, cmi, tre, tim, dskip, order, desc, with_skip, name):
    t, cu = up.shape
    nsb = bre.shape[0]
    cb = cu // nsb
    cs = bre.shape[2]
    nch = t // TC

    def body(u_ref, bre_ref, bim_ref, cmr_ref, cmi_ref, tre_ref, tim_ref, dsk_ref,
             y_ref, hr_ref, hi_ref, cinr_ref, cini_ref, car_r, car_i):
        i = pl.program_id(0)

        @pl.when(i == 0)
        def _():
            car_r[...] = jnp.zeros_like(car_r)
            car_i[...] = jnp.zeros_like(car_i)

        for sb in range(nsb):
            col, ch = pl.ds(sb * cs, cs), slice(sb * cb, (sb + 1) * cb)
            hr_v, hi_v = hr_ref.at[:, col], hi_ref.at[:, col]
            tre_v, tim_v = tre_ref.at[:, col], tim_ref.at[:, col]
            u = u_ref[:, ch]
            ub = u.astype(BF16)
            hr_v[...] = _dot(ub, bre_ref[sb])
            hi_v[...] = _dot(ub, bim_ref[sb])
            cs_r, cs_i, cin_r, cin_i = _scan_head(hr_v, hi_v, tre_v, tim_v, car_r.at[sb], car_i.at[sb], desc, False, cs)
            cinr_ref[:, col] = cin_r
            cini_ref[:, col] = cin_i

            def p2(j, _, hr_v=hr_v, hi_v=hi_v, tre_v=tre_v, tim_v=tim_v, cs_r=cs_r, cs_i=cs_i):
                off = _slab(j)
                pw_r, pw_i = _pow_row(tre_v, tim_v, j, desc, False, cs)
                pr, pi_ = _cmul(pw_r, pw_i, cs_r, cs_i)
                hr_v[pl.ds(off, 8), :] = hr_v[pl.ds(off, 8), :] + pr
                hi_v[pl.ds(off, 8), :] = hi_v[pl.ds(off, 8), :] + pi_
                return 0

            lax.fori_loop(0, SEG, p2, 0, unroll=S5_UNROLL)
            y = _dot(hr_v[...].astype(BF16), cmr_ref[sb]) - _dot(hi_v[...].astype(BF16), cmi_ref[sb])
            if with_skip:
                y = y + dsk_ref[:, ch] * u
            y_ref[:, ch] = y

    blk = lambda i: (order(i), 0)
    return _pcall(
        body, name=name, grid=(nch,),
        out_shape=(jax.ShapeDtypeStruct((t, cu), F32), jax.ShapeDtypeStruct((t, nsb * cs), F32),
                   jax.ShapeDtypeStruct((t, nsb * cs), F32), jax.ShapeDtypeStruct((nch * 8, nsb * cs), F32),
                   jax.ShapeDtypeStruct((nch * 8, nsb * cs), F32)),
        in_specs=[pl.BlockSpec((TC, cu), blk), _const(bre.shape), _const(bim.shape), _const(cmr.shape),
                  _const(cmi.shape), _const(tre.shape), _const(tim.shape), _const(dskip.shape)],
        out_specs=(pl.BlockSpec((TC, cu), blk), pl.BlockSpec((TC, nsb * cs), blk), pl.BlockSpec((TC, nsb * cs), blk),
                   pl.BlockSpec((8, nsb * cs), blk), pl.BlockSpec((8, nsb * cs), blk)),
        scratch_shapes=[pltpu.VMEM((nsb, 8, cs), F32), pltpu.VMEM((nsb, 8, cs), F32)],
        compiler_params=_params(("arbitrary",)),
    )(up, bre, bim, cmr, cmi, tre, tim, dskip)


def _s5_bwd(dyp, up, hr, hi, cinr, cini, bre, bim, ctr, cti, tre, tim, dskip, order, desc, with_skip, name, xchg=None):
    t, cu = up.shape
    nsb = bre.shape[0]
    cb = cu // nsb
    cs = bre.shape[2]
    nch = t // TC
    adesc = not desc

    def body(dy_ref, u_ref, hr_ref, hi_ref, cinr_ref, cini_ref, bre_ref, bim_ref, ctr_ref, cti_ref, tre_ref, tim_ref,
             dsk_ref, du_ref, dar_ref, dai_ref, dcr_ref, dci_ref, dbr_ref, dbi_ref, dd_ref,
             car_r, car_i, mr_all, mi_all, acc_r, acc_i):
        i = pl.program_id(0)

        @pl.when(i == 0)
        def _():
            for ref in (car_r, car_i, acc_r, acc_i, dcr_ref, dci_ref, dbr_ref, dbi_ref, dd_ref):
                ref[...] = jnp.zeros_like(ref)

        rows = lax.broadcasted_iota(jnp.int32, (8, cs), 0)
        for sb in range(nsb):
            col, ch = pl.ds(sb * cs, cs), slice(sb * cb, (sb + 1) * cb)
            hr_v, hi_v = hr_ref.at[:, col], hi_ref.at[:, col]
            tre_v, tim_v = tre_ref.at[:, col], tim_ref.at[:, col]
            mr, mi = mr_all.at[sb % 2], mi_all.at[sb % 2]
            dy = dy_ref[:, ch]
            dyb = dy.astype(BF16)
            u = u_ref[:, ch]
            ub = u.astype(BF16)
            mr[...] = _dot(dyb, ctr_ref[sb])
            mi[...] = -_dot(dyb, cti_ref[sb])
            cs_r, cs_i, _, _ = _scan_head(mr, mi, tre_v, tim_v, car_r.at[sb], car_i.at[sb], adesc, True, cs)

            def fix(j, hp_r, hp_i, acc, mr=mr, mi=mi, tre_v=tre_v, tim_v=tim_v, cs_r=cs_r, cs_i=cs_i):
                off = _slab(j)
                pw_r, pw_i = _pow_row(tre_v, tim_v, j, adesc, True, cs)
                pr, pi_ = _cmul(pw_r, pw_i, cs_r, cs_i)
                m_r = mr[pl.ds(off, 8), :] + pr
                m_i = mi[pl.ds(off, 8), :] + pi_
                mr[pl.ds(off, 8), :] = m_r
                mi[pl.ds(off, 8), :] = m_i
                a_r, a_i = acc
                return a_r + hp_r * m_r + hp_i * m_i, a_i + hp_r * m_i - hp_i * m_r

            edge_j, src, keep, sh = (SEG - 1, 0, rows < 7, 7) if desc else (0, (SEG - 1) * 8, rows >= 1, 1)
            h0r = jnp.where(keep, pltpu.roll(hr_v[src:src + 8, :], sh, 0), cinr_ref[:, col])
            h0i = jnp.where(keep, pltpu.roll(hi_v[src:src + 8, :], sh, 0), cini_ref[:, col])
            acc = fix(edge_j, h0r, h0i, (acc_r[sb], acc_i[sb]))

            def p2(jj, acc, fix=fix, hr_v=hr_v, hi_v=hi_v):
                j = jj if desc else jj + 1
                offp = _slab(j + 1 if desc else j - 1)
                return fix(j, hr_v[pl.ds(offp, 8), :], hi_v[pl.ds(offp, 8), :], acc)

            a_r, a_i = lax.fori_loop(0, SEG - 1, p2, acc, unroll=S5_UNROLL)
            acc_r[sb] = a_r
            acc_i[sb] = a_i

            mrb, mib = mr[...].astype(BF16), mi[...].astype(BF16)
            du = _dot_nt(mrb, bre_ref[sb]) + _dot_nt(mib, bim_ref[sb])
            if with_skip:
                du = du + dsk_ref[:, ch] * dy
                dd_ref[:, ch] += _colsum(dy * u)
            du_ref[:, ch] = du
            dbr_ref[sb] += _dot_tn(ub, mrb)
            dbi_ref[sb] += _dot_tn(ub, mib)
            dcr_ref[sb] += _dot_tn(dyb, hr_v[...].astype(BF16))
            dci_ref[sb] -= _dot_tn(dyb, hi_v[...].astype(BF16))

            @pl.when(i == nch - 1)
            def _(sb=sb, a_r=a_r, a_i=a_i):
                dar_ref[sb] = _colsum(a_r)
                dai_ref[sb] = _colsum(a_i)

    blk = lambda i: (order(i), 0)
    mat = (nsb, cb, cs)
    return _pcall(
        body, name=name, grid=(nch,), xchg=xchg, edges=_edges1(nch),
        out_shape=(jax.ShapeDtypeStruct((t, cu), F32),
                   jax.ShapeDtypeStruct((nsb, 1, cs), F32), jax.ShapeDtypeStruct((nsb, 1, cs), F32),
                   jax.ShapeDtypeStruct(mat, F32), jax.ShapeDtypeStruct(mat, F32),
                   jax.ShapeDtypeStruct(mat, F32), jax.ShapeDtypeStruct(mat, F32),
                   jax.ShapeDtypeStruct((1, cu), F32)),
        in_specs=[pl.BlockSpec((TC, cu), blk), pl.BlockSpec((TC, cu), blk), pl.BlockSpec((TC, nsb * cs), blk),
                  pl.BlockSpec((TC, nsb * cs), blk), pl.BlockSpec((8, nsb * cs), blk), pl.BlockSpec((8, nsb * cs), blk),
                  _const(mat), _const(mat), _const(mat), _const(mat), _const(tre.shape), _const(tim.shape),
                  _const((1, cu))],
        out_specs=(pl.BlockSpec((TC, cu), blk), _const((nsb, 1, cs)), _const((nsb, 1, cs)),
                   _const(mat), _const(mat), _const(mat), _const(mat), _const((1, cu))),
        scratch_shapes=[pltpu.VMEM((nsb, 8, cs), F32), pltpu.VMEM((nsb, 8, cs), F32), pltpu.VMEM((2, TC, cs), F32),
                        pltpu.VMEM((2, TC, cs), F32), pltpu.VMEM((nsb, 8, cs), F32), pltpu.VMEM((nsb, 8, cs), F32)],
        compiler_params=_params(("arbitrary",)),
    )(dyp, up, hr, hi, cinr, cini, bre, bim, ctr, cti, tre, tim, dskip)


def _gelu(x):
    k = math.sqrt(2.0 / math.pi)
    return 0.5 * x * (1.0 + jnp.tanh(k * (x + 0.044715 * (x * x * x))))


def _gelu_grad(x):
    k = math.sqrt(2.0 / math.pi)
    th = jnp.tanh(k * (x + 0.044715 * (x * x * x)))
    return 0.5 * (1.0 + th) + 0.5 * x * (1.0 - th * th) * (k * (1.0 + 3.0 * 0.044715 * (x * x)))


def _mix_out_fwd(yf, yb, dpool, x1, modtab, w_pool, pscale, w_glu, w_out):
    l, hm = dpool.shape
    d = x1.shape[1]

    def body(yf_ref, yb_ref, dp_ref, x_ref, mod_ref, wp_ref, ps_ref, wg_ref, wo_ref, x2_ref, yp_ref, cat_ref, mix_ref):
        ypre = yf_ref[...] + yb_ref[...]
        yp_ref[...] = ypre
        yg = _gelu(ypre)
        y2 = yg * _sigmoid(_dot(yg.astype(BF16), wg_ref[...]))
        po = _dot(dp_ref[...].astype(BF16), wp_ref[...]) * ps_ref[...]
        cat = jnp.concatenate([po, y2], axis=1).astype(BF16)
        cat_ref[...] = cat
        mix = _dot(cat, wo_ref[...])
        mix_ref[...] = mix.astype(BF16)
        x2_ref[...] = x_ref[...] + mod_ref[0, 5:6, :] * mix

    row = lambda i: (i, 0)
    lrow = lambda i: (i + 1, 0)
    return _pcall(
        body, name="mix_out_fwd", grid=(l // TM,),
        out_shape=(jax.ShapeDtypeStruct((l, d), F32), jax.ShapeDtypeStruct((l, hm), F32),
                   jax.ShapeDtypeStruct((l, d), BF16), jax.ShapeDtypeStruct((l, d), BF16)),
        in_specs=[pl.BlockSpec((TM, hm), lrow), pl.BlockSpec((TM, hm), lrow), pl.BlockSpec((TM, hm), row),
                  pl.BlockSpec((TM, d), lrow), _mod_spec(d, False), _const(w_pool.shape), _const((1, hm)),
                  _const(w_glu.shape), _resident(w_out.shape)],
        out_specs=(pl.BlockSpec((TM, d), row), pl.BlockSpec((TM, hm), row), pl.BlockSpec((TM, d), row),
                   pl.BlockSpec((TM, d), row)),
        compiler_params=_params(("arbitrary",)),
    )(yf, yb, dpool, x1, modtab, w_pool, pscale, w_glu, w_out)


def _mix_out_bwd(dx2, mix, cat, ypre, dpool, modtab, w_pool, pscale, w_glu, w_out):
    l, hm = dpool.shape
    d = dx2.shape[1]
    nt = l // TML

    def body(dx_ref, mix_ref, cat_ref, yp_ref, dp_ref, mod_ref, wp_ref, ps_ref, wg_ref, wo_ref,
             dyp_ref, ddp_ref, dwo_ref, dwg_ref, dwp_ref, dps_ref, dgate_ref, acc_o, acc_g):
        i = pl.program_id(0)

        @pl.when(i == 0)
        def _():
            for ref in (acc_o, acc_g, dwp_ref, dps_ref, dgate_ref):
                ref[...] = jnp.zeros_like(ref)

        dx = dx_ref[...]
        dgate_ref[...] += _colsum(dx * mix_ref[...].astype(F32))
        dmix = (dx * mod_ref[0, 5:6, :]).astype(BF16)
        dcat = _dot_nt(dmix, wo_ref[...])
        acc_o[...] += _dot_tn(cat_ref[...], dmix)
        dpo, dy2 = dcat[:, :hm], dcat[:, hm:]
        dpb = dp_ref[...].astype(BF16)
        pp = _dot(dpb, wp_ref[...])
        dps_ref[...] += _colsum(dpo * pp)
        dpp = (dpo * ps_ref[...]).astype(BF16)
        ddp_ref[...] = _dot_nt(dpp, wp_ref[...])
        dwp_ref[...] += _dot_tn(dpb, dpp)
        ypre = yp_ref[...]
        yg = _gelu(ypre)
        ygb = yg.astype(BF16)
        s = _sigmoid(_dot(ygb, wg_ref[...]))
        dq = (dy2 * yg * s * (1.0 - s)).astype(BF16)
        dyg = dy2 * s + _dot_nt(dq, wg_ref[...])
        acc_g[...] += _dot_tn(ygb, dq)
        dyp_ref[...] = dyg * _gelu_grad(ypre)

        @pl.when(i == nt - 1)
        def _():
            dwo_ref[...] = acc_o[...].astype(BF16)
            dwg_ref[...] = acc_g[...].astype(BF16)

    row = lambda i: (i, 0)
    return _pcall(
        body, name="mix_out_bwd", grid=(nt,),
        out_shape=(jax.ShapeDtypeStruct((l, hm), F32), jax.ShapeDtypeStruct((l, hm), F32),
                   jax.ShapeDtypeStruct((d, d), BF16), jax.ShapeDtypeStruct((hm, hm), BF16),
                   jax.ShapeDtypeStruct((hm, hm), F32), jax.ShapeDtypeStruct((1, hm), F32),
                   jax.ShapeDtypeStruct((1, d), F32)),
        in_specs=[pl.BlockSpec((TML, d), row), pl.BlockSpec((TML, d), row), pl.BlockSpec((TML, d), row),
                  pl.BlockSpec((TML, hm), row), pl.BlockSpec((TML, hm), row), _mod_spec(d, False),
                  _const(w_pool.shape), _const((1, hm)), _const(w_glu.shape), _resident(w_out.shape)],
        out_specs=(pl.BlockSpec((TML, hm), row), pl.BlockSpec((TML, hm), row), _const((d, d)), _const((hm, hm)),
                   _const((hm, hm)), _const((1, hm)), _const((1, d))),
        scratch_shapes=[pltpu.VMEM((d, d), F32), pltpu.VMEM((hm, hm), F32)],
        compiler_params=_params(("arbitrary",)),
    )(dx2, mix, cat, ypre, dpool, modtab, w_pool, pscale, w_glu, w_out)


def _loss_head(x3, target, g):
    l, d = x3.shape

    def body(x_ref, t_ref, g_ref, dx_ref, loss_ref, dg_ref):
        i = pl.program_id(0)

        @pl.when(i == 0)
        def _():
            loss_ref[...] = jnp.zeros_like(loss_ref)
            dg_ref[...] = jnp.zeros_like(dg_ref)

        xx = x_ref[...]
        gg = g_ref[...]
        rstd = lax.rsqrt(jnp.mean(xx * xx, axis=-1, keepdims=True) + EPS)
        xhat = xx * rstd
        err = xhat * gg - t_ref[...]
        row_loss = jnp.mean(err * err, axis=-1, keepdims=True)
        loss_ref[...] += 0.5 * jnp.sum(row_loss, axis=0, keepdims=True)
        dout = err * (1.0 / d)
        dg_ref[...] += _colsum(dout * xhat)
        dxhat = dout * gg
        dx_ref[...] = rstd * (dxhat - xhat * jnp.mean(dxhat * xhat, axis=-1, keepdims=True))

    row = lambda i: (i, 0)
    return _pcall(
        body, name="loss_head", grid=(l // TML,),
        out_shape=(jax.ShapeDtypeStruct((l, d), F32), jax.ShapeDtypeStruct((1, LANE), F32),
                   jax.ShapeDtypeStruct((1, d), F32)),
        in_specs=[pl.BlockSpec((TML, d), row), pl.BlockSpec((TML, d), row), _const((1, d))],
        out_specs=(pl.BlockSpec((TML, d), row), _const((1, LANE)), _const((1, d))),
        compiler_params=_params(("arbitrary",)),
    )(x3, target, g)


def _adamw(parts, w, m, v, name):
    shape = w.shape
    c = shape[-1]
    r = int(np.prod(shape)) // c
    npart = parts.shape[0]
    tr = r
    for cand in (1024, 512, 256, 128, 64, 32, 16):
        if r % cand == 0 and cand * max(c, LANE) * 4 <= ADAM_BLOCK_BYTES:
            tr = cand
            break
    c1 = 1.0 - ADAM_B1
    c2 = 1.0 - ADAM_B2
    bc1 = 1.0 - ADAM_B1 ** ADAM_STEP
    bc2 = 1.0 - ADAM_B2 ** ADAM_STEP

    def body(p_ref, w_ref, m_ref, v_ref, g_ref, d_ref, mo_ref, vo_ref):
        g = p_ref[0].astype(F32)
        for k in range(1, npart):
            g = g + p_ref[k].astype(F32)
        mn = ADAM_B1 * m_ref[...] + c1 * g
        vn = ADAM_B2 * v_ref[...] + c2 * (g * g)
        m_hat = mn / bc1
        v_hat = vn / bc2
        g_ref[...] = g
        mo_ref[...] = mn
        vo_ref[...] = vn
        d_ref[...] = -ADAM_LR * (m_hat / (jnp.sqrt(v_hat) + ADAM_EPS) + ADAM_WD * w_ref[...])

    spec = pl.BlockSpec((tr, c), lambda i: (i, 0))
    outs = _pcall(
        body, name=name, grid=(r // tr,),
        out_shape=tuple(jax.ShapeDtypeStruct((r, c), F32) for _ in range(4)),
        in_specs=[pl.BlockSpec((npart, tr, c), lambda i: (0, i, 0)), spec, spec, spec],
        out_specs=(spec, spec, spec, spec),
        compiler_params=_params(("arbitrary",)),
    )(parts.reshape(npart, r, c), w.reshape(r, c), m.reshape(r, c), v.reshape(r, c))
    return tuple(o.reshape(shape) for o in outs)


def _perm(a):
    t, c = a.shape
    return a.reshape(t // TC, 8, SEG, c).transpose(0, 2, 1, 3).reshape(t, c)


def _unperm(a):
    t, c = a.shape
    return a.reshape(t // TC, SEG, 8, c).transpose(0, 2, 1, 3).reshape(t, c)


def _blockdiag(xb):
    n, a, b = xb.shape[-3:]
    eye = jnp.eye(n, dtype=xb.dtype)
    out = xb[..., :, :, None, :] * eye[:, None, :, None]
    return out.reshape(xb.shape[:-3] + (n * a, n * b))


def _diagblocks(mat, n):
    lead = mat.shape[:-2]
    a, b = mat.shape[-2] // n, mat.shape[-1] // n
    m5 = mat.reshape(lead + (n, a, n, b))
    nl = len(lead)
    dg = jnp.diagonal(m5, axis1=nl, axis2=nl + 2)
    return jnp.moveaxis(dg, -1, nl)


def _to_grid(a, rows):
    return a.reshape(rows, GRID_W, 4, LANE).transpose(2, 0, 1, 3).reshape(4, rows, GRID_W * LANE)


def _from_grid(a, rows):
    return a.reshape(4, rows, GRID_W, LANE).transpose(1, 2, 0, 3).reshape(rows * GRID_W, 4 * LANE)


def _pad128(a):
    flat = a.reshape(-1)
    pad = (-flat.shape[0]) % LANE
    return jnp.pad(flat, (0, pad)) if pad else flat


def kernel(x, c, ctx, c_ctx, norm_g, w_ada, b_ada, ffn_w1, ffn_w3, ffn_w2, w_in, pool_w, pool_scale, s5_a_re, s5_a_im, s5_log_dt, s5_b_re, s5_b_im, s5_c_re, s5_c_im, s5_d, s5_w_glu, w_out, final_g, loss_target, m_c_ctx, m_norm_g, m_w_ada, m_b_ada, m_ffn_w1, m_ffn_w3, m_ffn_w2, m_w_in, m_pool_w, m_pool_scale, m_s5_a_re, m_s5_a_im, m_s5_log_dt, m_s5_b_re, m_s5_b_im, m_s5_c_re, m_s5_c_im, m_s5_d, m_s5_w_glu, m_w_out, m_final_g, v_c_ctx, v_norm_g, v_w_ada, v_b_ada, v_ffn_w1, v_ffn_w3, v_ffn_w2, v_w_in, v_pool_w, v_pool_scale, v_s5_a_re, v_s5_a_im, v_s5_log_dt, v_s5_b_re, v_s5_b_im, v_s5_c_re, v_s5_c_im, v_s5_d, v_s5_w_glu, v_w_out, v_final_g):
    weights = dict(c_ctx=c_ctx, norm_g=norm_g, w_ada=w_ada, b_ada=b_ada, ffn_w1=ffn_w1, ffn_w3=ffn_w3, ffn_w2=ffn_w2,
                   w_in=w_in, pool_w=pool_w, pool_scale=pool_scale, s5_a_re=s5_a_re, s5_a_im=s5_a_im,
                   s5_log_dt=s5_log_dt, s5_b_re=s5_b_re, s5_b_im=s5_b_im, s5_c_re=s5_c_re, s5_c_im=s5_c_im, s5_d=s5_d,
                   s5_w_glu=s5_w_glu, w_out=w_out, final_g=final_g)
    mom_m = dict(c_ctx=m_c_ctx, norm_g=m_norm_g, w_ada=m_w_ada, b_ada=m_b_ada, ffn_w1=m_ffn_w1, ffn_w3=m_ffn_w3,
                 ffn_w2=m_ffn_w2, w_in=m_w_in, pool_w=m_pool_w, pool_scale=m_pool_scale, s5_a_re=m_s5_a_re,
                 s5_a_im=m_s5_a_im, s5_log_dt=m_s5_log_dt, s5_b_re=m_s5_b_re, s5_b_im=m_s5_b_im, s5_c_re=m_s5_c_re,
                 s5_c_im=m_s5_c_im, s5_d=m_s5_d, s5_w_glu=m_s5_w_glu, w_out=m_w_out, final_g=m_final_g)
    mom_v = dict(c_ctx=v_c_ctx, norm_g=v_norm_g, w_ada=v_w_ada, b_ada=v_b_ada, ffn_w1=v_ffn_w1, ffn_w3=v_ffn_w3,
                 ffn_w2=v_ffn_w2, w_in=v_w_in, pool_w=v_pool_w, pool_scale=v_pool_scale, s5_a_re=v_s5_a_re,
                 s5_a_im=v_s5_a_im, s5_log_dt=v_s5_log_dt, s5_b_re=v_s5_b_re, s5_b_im=v_s5_b_im, s5_c_re=v_s5_c_re,
                 s5_c_im=v_s5_c_im, s5_d=v_s5_d, s5_w_glu=v_s5_w_glu, w_out=v_w_out, final_g=v_final_g)

    l, d = x.shape[1], x.shape[2]
    lc = ctx.shape[1]
    t = l + lc
    rows = l // GRID_W
    fblk = ffn_w1.shape[-1]
    ngrp, gp = s5_a_re.shape[2], s5_a_re.shape[3]
    gc = s5_b_re.shape[3]
    hm = ngrp * gc
    nsb = 4
    gsb = ngrp // nsb
    assert lc == TM == TC and l % TML == 0 and hm == 4 * LANE and ngrp * gp == nsb * 512
    me = 4 * lax.axis_index("x") + 2 * lax.axis_index("y") + lax.axis_index("c")

    padc = ((0, 0), (0, 0), (0, FPAD - fblk))
    w13_loc = [jnp.concatenate([jnp.pad(ffn_w1[0, k], padc[1:]), jnp.pad(ffn_w3[0, k], padc[1:])], axis=-1).astype(BF16)
               for k in range(2)]
    w2_loc = [jnp.pad(ffn_w2[0, k], ((0, FPAD - fblk), (0, 0))).astype(BF16) for k in range(2)]
    gath = _exchange([(w13_loc[0], GATHER_REL), (w2_loc[0], GATHER_REL), (c, GATHER), (norm_g[0], GATHER)],
                     "exchange_weights")
    w13_0 = gath[0]
    w2_0 = gath[1].reshape(NDEV // 2, 2 * FPAD, d)
    c_all = gath[2].reshape(NDEV, d)
    g_all = gath[3].transpose(1, 0, 2).reshape(3, d)

    wa = w_ada.shape[-1]
    c16 = jnp.concatenate([c_all, c_ctx.reshape(1, d), jnp.zeros((7, d), F32)], axis=0)
    b_blk = lax.dynamic_slice_in_dim(b_ada, me * wa, wa, axis=1)
    mod_sl = _ada_fwd(c16, w_ada[0], b_blk)
    gath = _exchange([(mod_sl, GATHER)], "exchange_mod")
    mod_full = gath[0].transpose(1, 0, 2).reshape(16, NDEV * wa)
    mod_l = lax.dynamic_index_in_dim(mod_full, me, axis=0, keepdims=False).reshape(9, d)
    modtab = jnp.stack([mod_full[8].reshape(9, d), mod_l])

    x0 = [ctx[0], x[0]]
    g0, g1, g2 = g_all[0:1], g_all[1:2], g_all[2:3]
    x1, h0, ab0, o0, w13_1, w2_1, w_in_f, w_out_f, w_glu_f = _ffn_fwd(
        x0, modtab, g0, w13_0, w2_0, 0, True, "ffn0_fwd",
        xchg=[(w13_loc[1], GATHER_REL), (w2_loc[1], GATHER_REL), (w_in[0].astype(BF16), GATHER),
              (w_out[0].astype(BF16), GATHER), (s5_w_glu[0].astype(BF16), GATHER)])
    w2_1 = w2_1.reshape(NDEV // 2, 2 * FPAD, d)
    w_in_f = w_in_f.reshape(d, -1)
    w_out_f = w_out_f.reshape(-1, d)
    w_glu_f = w_glu_f.reshape(hm, hm)
    h1, u = _mix_in_fwd(x1, modtab, g1, w_in_f)

    v_grid = _to_grid(u[lc:, :hm], rows)
    dpool = _from_grid(_pool(v_grid, False, "pool_fwd"), rows)
    w_pool_bd = _blockdiag(pool_w[0]).astype(BF16)

    q = 2 * ngrp
    kvec = np.concatenate([np.arange(1, SEG + 1), SEG * np.arange(1, 9), SEG * np.arange(8, 0, -1)]).astype(np.float32)
    a_re3, a_im3 = s5_a_re[0].reshape(q, 1, gp), s5_a_im[0].reshape(q, 1, gp)
    ldt3 = jnp.broadcast_to(s5_log_dt[0].reshape(q, 1, 1), (q, 1, gp))
    c_re3, c_im3 = s5_c_re[0].reshape(q, gc, gp), s5_c_im[0].reshape(q, gc, gp)
    cp_re, cp_im, tab_re, tab_im = _s5_prep(a_re3, a_im3, ldt3, c_re3, c_im3, jnp.broadcast_to(jnp.asarray(kvec).reshape(1, NTAB, 1), (1, NTAB, gp)))
    tabs = [(tab_re.reshape(2, ngrp, NTAB, gp)[k].transpose(1, 0, 2).reshape(NTAB, ngrp * gp),
             tab_im.reshape(2, ngrp, NTAB, gp)[k].transpose(1, 0, 2).reshape(NTAB, ngrp * gp)) for k in range(2)]
    b_t = lambda b: _blockdiag(b[0].reshape(nsb, gsb, gp, gc).transpose(0, 1, 3, 2)).astype(BF16)
    bre, bim = b_t(s5_b_re), b_t(s5_b_im)
    cp_t = lambda cp: _blockdiag(cp.reshape(2, nsb, gsb, gc, gp)).astype(BF16)
    ctr, cti = cp_t(cp_re), cp_t(cp_im)
    cmr, cmi = jnp.swapaxes(ctr, -1, -2), jnp.swapaxes(cti, -1, -2)
    dskip = s5_d

    nch = t // TC
    order_f = lambda i: i
    order_b = lambda i: jnp.where(i == 0, 0, nch - i)
    rorder_f = lambda i: nch - 1 - i
    rorder_b = lambda i: jnp.where(i == nch - 1, 0, i + 1)
    up = _perm(u[:, hm:])
    yp_f, hr_f, hi_f, cinr_f, cini_f = _s5_fwd(up, bre, bim, cmr[0], cmi[0], tabs[0][0], tabs[0][1], dskip,
                                               order_f, False, True, "s5_fwd_f")
    yp_b, hr_b, hi_b, cinr_b, cini_b = _s5_fwd(up, bre, bim, cmr[1], cmi[1], tabs[1][0], tabs[1][1], dskip,
                                               order_b, True, False, "s5_fwd_b")
    y_f, y_b = _unperm(yp_f), _unperm(yp_b)

    x2, ypre, cat, mix = _mix_out_fwd(y_f, y_b, dpool, x1, modtab, w_pool_bd, pool_scale, w_glu_f, w_out_f)
    x3, h2, ab2, o2 = _ffn_fwd([x2], modtab, g2, w13_1, w2_1, 6, False, "ffn1_fwd")
    dx3, loss_part, dfinal_g = _loss_head(x3, loss_target[0], final_g.reshape(1, d))

    dhp2, d13_1, d2_1 = _ffn_bwd(dx3, modtab, h2, ab2, w13_1, w2_1, 6, False, "ffn1_bwd")
    dx2, dmod_678, dg2 = _ffn_bwd_norm(dx3, [dhp2], [x2], o2, modtab, g2, 6, False, "ffn1_bwd_norm")
    dypre, ddpool, dw_out, dw_glu, dw_pool_bd, dpscale, dgate5 = _mix_out_bwd(
        dx2, mix, cat, ypre, dpool, modtab, w_pool_bd, pool_scale, w_glu_f, w_out_f)

    dyp = _perm(jnp.concatenate([jnp.zeros((lc, hm), F32), dypre], axis=0))
    dup_f, dar_f, dai_f, dcr_f, dci_f, dbr_f, dbi_f, dd_skip, r13_1, r2_1 = _s5_bwd(
        dyp, up, hr_f, hi_f, cinr_f, cini_f, bre, bim, ctr[0], cti[0], tabs[0][0], tabs[0][1], dskip,
        rorder_f, False, True, "s5_bwd_f", xchg=[(d13_1, A2A_REL), (d2_1.reshape(NDEV, FPAD, d), A2A_REL)])
    dup_b, dar_b, dai_b, dcr_b, dci_b, dbr_b, dbi_b, _, r_out, r_glu = _s5_bwd(
        dyp, up, hr_b, hi_b, cinr_b, cini_b, bre, bim, ctr[1], cti[1], tabs[1][0], tabs[1][1], dskip,
        rorder_b, True, False, "s5_bwd_b",
        xchg=[(dw_out.reshape(NDEV, -1, d), A2A), (dw_glu.reshape(NDEV, hm // NDEV, hm), A2A)])
    du_f, du_b = _unperm(dup_f), _unperm(dup_b)

    dab_r = jnp.stack([dar_f, dar_b]).reshape(q, 1, gp)
    dab_i = jnp.stack([dai_f, dai_b]).reshape(q, 1, gp)
    dcp_r = _diagblocks(jnp.stack([dcr_f, dcr_b]), gsb).reshape(q, gc, gp)
    dcp_i = _diagblocks(jnp.stack([dci_f, dci_b]), gsb).reshape(q, gc, gp)
    ga_re, ga_im, gldt, gc_re, gc_im = _s5_param_bwd(a_re3, a_im3, ldt3, c_re3, c_im3, dab_r, dab_i, dcp_r, dcp_i)
    gb_re = (_diagblocks(dbr_f, gsb) + _diagblocks(dbr_b, gsb)).transpose(0, 1, 3, 2).reshape(ngrp, gp, gc)
    gb_im = (_diagblocks(dbi_f, gsb) + _diagblocks(dbi_b, gsb)).transpose(0, 1, 3, 2).reshape(ngrp, gp, gc)

    def pack(small):
        offs, pieces, off = {}, [], 0
        for k_, a_ in small.items():
            p_ = _pad128(a_.astype(F32))
            offs[k_] = (off, int(np.prod(a_.shape)))
            off += p_.shape[0]
            pieces.append(p_)
        return jnp.concatenate(pieces).reshape(1, off), offs

    gw_pool = _diagblocks(dw_pool_bd, 4)
    bundle_a, offs = pack(dict(pool_w=gw_pool, pool_scale=dpscale, s5_a_re=ga_re, s5_a_im=ga_im,
                               s5_log_dt=gldt[:, 0, 0], s5_b_re=gb_re, s5_b_im=gb_im, s5_c_re=gc_re, s5_c_im=gc_im,
                               s5_d=dd_skip, final_g=dfinal_g))

    du_pool = _from_grid(_pool(_to_grid(ddpool, rows), True, "pool_bwd"), rows)
    dx1, dw_in, dmod_34, dg1 = _mix_in_bwd(du_pool, du_f, du_b, dx2, h1, x1, modtab, g1, w_in_f)
    dhp0a, d13_0a, d2_0a, bund_a, r_in = _ffn_bwd(
        dx1, modtab, h0, ab0, w13_0, w2_0, 0, True, "ffn0_bwd_a", pair0=0, npair=2,
        xchg=[(bundle_a, GATHER), (dw_in.reshape(NDEV, d // NDEV, -1), A2A)])
    dhp0b, d13_0b, d2_0b, r13_0a, r2_0a = _ffn_bwd(
        dx1, modtab, h0, ab0, w13_0, w2_0, 0, True, "ffn0_bwd_b", pair0=2, npair=2,
        xchg=[(d13_0a, A2A_SAME), (d2_0a.reshape(NDEV // 2, FPAD, d), A2A_SAME)])
    dx0, dmod_012, dg0, r13_0b, r2_0b = _ffn_bwd_norm(
        dx1, [dhp0a, dhp0b], x0, o0, modtab, g0, 0, True, "ffn0_bwd_norm",
        xchg=[(d13_0b, A2A_OTHER), (d2_0b.reshape(NDEV // 2, FPAD, d), A2A_OTHER)])
    r13_0 = jnp.concatenate([r13_0a, r13_0b], axis=0)
    r2_0 = jnp.concatenate([r2_0a, r2_0b], axis=0)
    grad_x = dx0.reshape(1, l, d)

    dmod_c = jnp.concatenate([dmod_012[0], dmod_34[0], jnp.zeros((4, d), F32)], axis=0)
    dmod_l = jnp.concatenate([dmod_012[1], dmod_34[1], dgate5, dmod_678[0]], axis=0)
    bundle_b, offs_b = pack(dict(norm_g=jnp.concatenate([dg0, dg1, dg2], axis=0), dmod_l=dmod_l, dmod_c=dmod_c,
                                 loss=loss_part[:, :1]))
    bund_b = _exchange([(bundle_b, GATHER)], "exchange_small")[0]
    bunds = {**{k_: (bund_a.reshape(NDEV, -1), v_) for k_, v_ in offs.items()},
             **{k_: (bund_b.reshape(NDEV, -1), v_) for k_, v_ in offs_b.items()}}
    r13, r2 = (r13_0, r13_1), (r2_0, r2_1)

    def piece(name):
        b_, (o_, n_) = bunds[name]
        return b_[:, o_:o_ + n_]

    loss = jnp.sum(piece("loss")[:, 0])

    dl_all = lax.dynamic_slice_in_dim(piece("dmod_l"), me * wa, wa, axis=1)
    dc_all = lax.dynamic_slice_in_dim(piece("dmod_c"), me * wa, wa, axis=1)
    g_wada, gc_part = _ada_bwd(c16, w_ada[0], dl_all, dc_all)
    gc_all = _exchange([(gc_part, GATHER)], "exchange_cctx")[0]

    parts = {
        "c_ctx": gc_all.reshape(NDEV, d),
        "norm_g": lax.dynamic_slice_in_dim(piece("norm_g").reshape(NDEV, 3, d), me * (d // NDEV), d // NDEV,
                                           axis=2).reshape((NDEV,) + norm_g.shape),
        "w_ada": g_wada.reshape((1,) + w_ada.shape),
        "b_ada": jnp.concatenate([piece("dmod_l"), piece("dmod_c")], axis=0).reshape((2 * NDEV,) + b_ada.shape),
        "ffn_w1": jnp.stack([r13[0][:, :, :fblk], r13[1][:, :, :fblk]], axis=1).reshape((NDEV,) + ffn_w1.shape),
        "ffn_w3": jnp.stack([r13[0][:, :, FPAD:FPAD + fblk], r13[1][:, :, FPAD:FPAD + fblk]],
                            axis=1).reshape((NDEV,) + ffn_w3.shape),
        "ffn_w2": jnp.stack([r2[0][:, :fblk, :], r2[1][:, :fblk, :]], axis=1).reshape((NDEV,) + ffn_w2.shape),
        "w_in": r_in.reshape((NDEV,) + w_in.shape),
        "s5_w_glu": r_glu.reshape((NDEV,) + s5_w_glu.shape),
        "w_out": r_out.reshape((NDEV,) + w_out.shape),
    }
    for k_ in ("pool_w", "pool_scale", "s5_a_re", "s5_a_im", "s5_log_dt", "s5_b_re", "s5_b_im", "s5_c_re", "s5_c_im",
               "s5_d", "final_g"):
        parts[k_] = piece(k_).reshape((NDEV,) + weights[k_].shape)

    grads, deltas, new_m, new_v = [], [], [], []
    for k_ in weights:
        g_, d_, m_, v_ = _adamw(parts[k_], weights[k_], mom_m[k_], mom_v[k_], "adamw_" + k_)
        grads.append(g_)
        deltas.append(d_)
        new_m.append(m_)
        new_v.append(v_)
    return (loss, grad_x, *grads, *deltas, *new_m, *new_v)
```

```python
import functools
import math

import numpy as np
import jax
import jax.numpy as jnp
from jax import lax
from jax.experimental import pallas as pl
from jax.experimental.pallas import tpu as pltpu

F32 = jnp.float32
BF16 = jnp.bfloat16
AXES = ("x", "y", "c")
NDEV = 8
EPS = 1e-6
TM = 512
TML = 512
TC = 256
SEG = TC // 8
NTAB = SEG + 16
S5_UNROLL = True
GRID_W = 64
POOL_WINDOWS = (2, 4, 8, 16)
LANE = 128
FPAD = 384
VMEM_LIMIT = 56 * 1024 * 1024
ADAM_BLOCK_BYTES = 512 * 1024
ADAM_LR, ADAM_B1, ADAM_B2, ADAM_EPS, ADAM_WD, ADAM_STEP = 0.001, 0.9, 0.999, 1e-08, 0.01, 10


def _raw_call(body, kw):
    return pl.pallas_call(body, **kw)


def _pcall(body, xchg=None, edges=None, **kw):
    extra = ()
    if xchg:
        arrs, kinds = [a for a, _ in xchg], [k for _, k in xchg]
        n = len(arrs)
        n_in, n_out, n_scr = len(kw["in_specs"]), len(kw["out_shape"]), len(kw.get("scratch_shapes", ()))
        inner = body

        def hosted(*refs):
            a, b = n_in, n_in + n
            c_, d_ = b + n_out, b + n_out + n
            e = d_ + n_scr
            first, mid, last = edges()

            @pl.when(first)
            def _():
                _xchg_start(refs[a:b], refs[c_:d_], refs[e:], kinds)

            inner(*refs[:a], *refs[b:c_], *refs[d_:e])

            if any(k == GATHER_REL for k in kinds):
                @pl.when(mid)
                def _():
                    _xchg_relay(refs[a:b], refs[c_:d_], refs[e:], kinds)

            @pl.when(last)
            def _():
                _xchg_finish(refs[a:b], refs[c_:d_], refs[e:], kinds)

        body = hosted
        any_spec = pl.BlockSpec(memory_space=pl.ANY)
        kw = dict(kw, in_specs=list(kw["in_specs"]) + [any_spec] * n,
                  out_shape=tuple(kw["out_shape"]) + _xchg_shapes(arrs, kinds),
                  out_specs=tuple(kw["out_specs"]) + (any_spec,) * n,
                  scratch_shapes=list(kw.get("scratch_shapes", ())) + _xchg_sems(n))
        extra = tuple(arrs)
    call = _raw_call(body, kw)
    return (lambda *args: call(*args, *extra)) if extra else call


def _params(sem):
    return pltpu.CompilerParams(dimension_semantics=sem, vmem_limit_bytes=VMEM_LIMIT)


def _dot(a, b):
    return jnp.dot(a, b, preferred_element_type=F32)


def _dot_nt(a, b):
    return lax.dot_general(a, b, (((1,), (1,)), ((), ())), preferred_element_type=F32)


def _dot_tn(a, b):
    return lax.dot_general(a, b, (((0,), (0,)), ((), ())), preferred_element_type=F32)


def _sigmoid(x):
    return 1.0 / (1.0 + jnp.exp(-x))


def _colsum(a):
    return jnp.sum(a, axis=0, keepdims=True)


def _resident(shape):
    nd = len(shape)
    return pl.BlockSpec(shape, lambda *_: (0,) * nd, pipeline_mode=pl.Buffered(1))


def _const(shape):
    nd = len(shape)
    return pl.BlockSpec(shape, lambda *_: (0,) * nd)


GATHER = "gather"
GATHER_REL = "gather_rel"
A2A = "a2a"
A2A_REL = "a2a_rel"
A2A_SAME = "a2a_same"
A2A_OTHER = "a2a_other"


def _exchange(items, name):
    arrs, kinds = [a for a, _ in items], [k for _, k in items]
    n = len(arrs)

    def body(*refs):
        _xchg_start(refs[:n], refs[n:2 * n], refs[2 * n:], kinds)
        _xchg_relay(refs[:n], refs[n:2 * n], refs[2 * n:], kinds)
        _xchg_finish(refs[:n], refs[n:2 * n], refs[2 * n:], kinds)

    any_spec = pl.BlockSpec(memory_space=pl.ANY)
    outs = _pcall(
        body, name=name, out_shape=_xchg_shapes(arrs, kinds), in_specs=[any_spec] * n, out_specs=[any_spec] * n,
        scratch_shapes=_xchg_sems(n),
    )(*arrs)
    return list(outs)


def _edges1(n0):
    return lambda: (pl.program_id(0) == 0, pl.program_id(0) == (3 * n0) // 4, pl.program_id(0) == n0 - 1)


def _edges2(n0, n1):
    def edges():
        step = pl.program_id(0) * n1 + pl.program_id(1)
        return step == 0, step == (3 * n0 * n1) // 4, step == n0 * n1 - 1
    return edges


def _xchg_shapes(arrs, kinds):
    return tuple(jax.ShapeDtypeStruct(((NDEV,) if k in (GATHER, GATHER_REL) else ()) + tuple(a.shape), a.dtype)
                 for a, k in zip(arrs, kinds))


def _xchg_sems(n):
    return [pltpu.SemaphoreType.DMA((n * (NDEV - 1),)), pltpu.SemaphoreType.DMA((n * (NDEV - 1),)),
            pltpu.SemaphoreType.DMA((n,))]


def _xchg_plan(in_refs, out_refs, sems, kinds):
    send_sems, recv_sems, loc_sems = sems
    x, y, c = (lax.axis_index(a) for a in AXES)
    my_abs, my_chip = 4 * x + 2 * y + c, 2 * x + y

    def peer(p):
        px = 1 - x if p & 4 else x
        py = 1 - y if p & 2 else y
        pc = 1 - c if p & 1 else c
        return (px, py, pc), 4 * px + 2 * py + pc, 2 * px + py

    starts, relays, recvs = [], [], []
    for k, kind in enumerate(kinds):
        src, dst = in_refs[k], out_refs[k]

        def remote(src_ref, row, p, pair, dst=dst, k=k):
            dev, sem = peer(p)[0], k * (NDEV - 1) + pair - 1
            return lambda: pltpu.make_async_remote_copy(
                src_ref=src_ref, dst_ref=dst.at[row], send_sem=send_sems.at[sem], recv_sem=recv_sems.at[sem],
                device_id=dev, device_id_type=pl.DeviceIdType.MESH)

        def local(src_ref, row, dst=dst, k=k):
            return lambda: pltpu.make_async_copy(src_ref, dst.at[row], loc_sems.at[k])

        if kind == GATHER:
            for p in range(1, NDEV):
                starts.append((remote(src, my_abs, p, p), False))
                recvs.append(remote(src, peer(p)[1], p, p))
            starts.append((local(src, my_abs), True))
        elif kind == GATHER_REL:
            for p in (1, 4, 2, 6):
                starts.append((remote(src, 4 * (p & 1) + my_chip, p, p), False))
            for q in (4, 2, 6):
                pchip = peer(q)[2]
                relays.append((remote(src, pchip, q, q), remote(dst.at[pchip], 4 + pchip, 1, q | 1)))
                recvs.append(remote(src, 4 + pchip, 1, q | 1))
            recvs.append(remote(src, 4 + my_chip, 1, 1))
            starts.append((local(src, my_chip), True))
        elif kind in (A2A, A2A_REL):
            for p in range(1, NDEV):
                _, pabs, pchip = peer(p)
                theirs, mine = (pabs, my_abs) if kind == A2A else (4 * (p & 1) + pchip, 4 * (p & 1) + my_chip)
                starts.append((remote(src.at[theirs], mine, p, p), False))
                recvs.append(remote(src.at[theirs], theirs, p, p))
            own = my_abs if kind == A2A else my_chip
            starts.append((local(src.at[own], own), True))
        else:
            for p in ((4, 2, 6) if kind == A2A_SAME else (1, 5, 3, 7)):
                pchip = peer(p)[2]
                starts.append((remote(src.at[pchip], my_chip, p, p), False))
                recvs.append(remote(src.at[pchip], pchip, p, p))
            if kind == A2A_SAME:
                starts.append((local(src.at[my_chip], my_chip), True))
    return starts, relays, recvs


def _xchg_start(in_refs, out_refs, sems, kinds):
    for make, _ in _xchg_plan(in_refs, out_refs, sems, kinds)[0]:
        make().start()


def _xchg_relay(in_refs, out_refs, sems, kinds):
    for arrival, forward in _xchg_plan(in_refs, out_refs, sems, kinds)[1]:
        arrival().wait_recv()
        forward().start()


def _xchg_finish(in_refs, out_refs, sems, kinds):
    starts, relays, recvs = _xchg_plan(in_refs, out_refs, sems, kinds)
    for make in recvs:
        make().wait_recv()
    for make, is_local in starts:
        if is_local:
            make().wait()
        else:
            make().wait_send()
    for _, forward in relays:
        forward().wait_send()


def _ada_fwd(c16, w_blk, b_blk):
    def body(c_ref, w_ref, b_ref, o_ref):
        cc = c_ref[...]
        s = (cc * _sigmoid(cc)).astype(BF16)
        o_ref[...] = _dot(s, w_ref[...].astype(BF16)) + b_ref[...]

    return _pcall(body, name="ada_fwd", out_shape=jax.ShapeDtypeStruct((16, w_blk.shape[1]), F32),
                  compiler_params=_params(None))(c16, w_blk, b_blk)


def _ada_bwd(c16, w_blk, dl, dc):
    d, w = w_blk.shape

    def body(c_ref, w_ref, dl_ref, dc_ref, gw_ref, gc_ref):
        cc = c_ref[...]
        sg = _sigmoid(cc)
        s = (cc * sg).astype(BF16)
        dctot = _colsum(dc_ref[...])
        dm = jnp.concatenate([dl_ref[...], jnp.broadcast_to(dctot, (8, w))], axis=0)
        rows = lax.broadcasted_iota(jnp.int32, (16, w), 0)
        dm = jnp.where(rows <= 8, dm, 0.0).astype(BF16)
        gw_ref[...] = _dot_tn(s, dm)
        t = _dot_nt(jnp.broadcast_to(dctot, (8, w)).astype(BF16), w_ref[...].astype(BF16))[0:1]
        c8, s8 = cc[8:9], sg[8:9]
        gc_ref[...] = t * (s8 * (1.0 + c8 * (1.0 - s8)))

    return _pcall(body, name="ada_bwd",
                  out_shape=(jax.ShapeDtypeStruct((d, w), F32), jax.ShapeDtypeStruct((1, d), F32)),
                  compiler_params=_params(None))(c16, w_blk, dl, dc)


def _norm_fwd(x, g, shift, scale):
    rstd = lax.rsqrt(jnp.mean(x * x, axis=-1, keepdims=True) + EPS)
    xhat = x * rstd
    xn = xhat * g
    return rstd, xhat, xn, xn * (1.0 + scale) + shift


def _norm_bwd(dh, rstd, xhat, xn, g, scale):
    dxn = dh * (1.0 + scale)
    dxhat = dxn * g
    dx = rstd * (dxhat - xhat * jnp.mean(dxhat * xhat, axis=-1, keepdims=True))
    return dx, _colsum(dh), _colsum(dh * xn), _colsum(dxn * xhat)


def _mod_spec(d, has_ctx):
    if has_ctx:
        return pl.BlockSpec((1, 9, d), lambda *ids: (jnp.minimum(ids[-1], 1), 0, 0))
    return pl.BlockSpec((1, 9, d), lambda *ids: (1, 0, 0))


def _row_specs(nx, tm, d):
    if nx == 1:
        return [pl.BlockSpec((tm, d), lambda i: (i, 0))]
    return [pl.BlockSpec((tm, d), lambda i: (0, 0)), pl.BlockSpec((tm, d), lambda i: (jnp.maximum(i - 1, 0), 0))]


def _rows(refs):
    if len(refs) == 1:
        return refs[0][...]
    return jnp.where(pl.program_id(0) == 0, refs[0][...], refs[1][...])


def _tile(has_ctx):
    return TM if has_ctx else TML


def _tile_of(nt, has_ctx):
    if has_ctx:
        return lambda i: jnp.where(i == 0, nt - 1, i - 1)
    return lambda i: i


def _dmod_spec(nrow, d, has_ctx):
    if has_ctx:
        return pl.BlockSpec((1, nrow, d), lambda i: (jnp.minimum(i, 1), 0, 0))
    return pl.BlockSpec((1, nrow, d), lambda i: (0, 0, 0))


def _ffn_fwd(x, modtab, g, w13, w2, k0, has_ctx, name, xchg=None):
    t, d = sum(a.shape[0] for a in x), x[0].shape[1]
    nb, _, w = w13.shape
    fp = w // 2
    tm = _tile(has_ctx)
    nx = len(x)

    def body(*refs):
        mod_ref, g_ref, w13_ref, w2_ref, xo_ref, h_ref, ab_ref, o_ref = refs[nx:]
        xx = _rows(refs[:nx])
        shift, scale, gate = mod_ref[0, k0:k0 + 1, :], mod_ref[0, k0 + 1:k0 + 2, :], mod_ref[0, k0 + 2:k0 + 3, :]
        _, _, _, h = _norm_fwd(xx, g_ref[...], shift, scale)
        hb = h.astype(BF16)
        h_ref[...] = hb
        acc = jnp.zeros((tm, d), F32)
        for p in range(nb // 2):
            zs = []
            for q in range(2):
                blk = 2 * p + q
                ab = _dot(hb, w13_ref[blk])
                ab_ref[:, blk * w:(blk + 1) * w] = ab.astype(BF16)
                a, b = ab[:, :fp], ab[:, fp:]
                zs.append((a * _sigmoid(a) * b).astype(BF16))
            acc = acc + _dot(jnp.concatenate(zs, axis=1), w2_ref[p])
        o_ref[...] = acc.astype(BF16)
        xo_ref[...] = xx + (0.5 * gate) * acc

    tile = _tile_of(t // tm, has_ctx)
    row = lambda i: (tile(i), 0)
    return _pcall(
        body, name=name, grid=(t // tm,), xchg=xchg, edges=_edges1(t // tm),
        out_shape=(jax.ShapeDtypeStruct((t, d), F32), jax.ShapeDtypeStruct((t, d), BF16),
                   jax.ShapeDtypeStruct((t, nb * w), BF16), jax.ShapeDtypeStruct((t, d), BF16)),
        in_specs=_row_specs(nx, tm, d) + [_mod_spec(d, has_ctx), _const((1, d)),
                                          _resident(w13.shape), _resident(w2.shape)],
        out_specs=(pl.BlockSpec((tm, d), row), pl.BlockSpec((tm, d), row), pl.BlockSpec((tm, nb * w), row),
                   pl.BlockSpec((tm, d), row)),
        compiler_params=_params(("arbitrary",)),
    )(*x, modtab, g, w13, w2)


def _ffn_bwd(dy, modtab, h, ab, w13, w2, k0, has_ctx, name, pair0=0, npair=None, xchg=None):
    t, d = dy.shape
    _, _, w = w13.shape
    fp = w // 2
    npair = w13.shape[0] // 2 if npair is None else npair
    nb = 2 * npair
    tm = _tile(has_ctx)
    nt = t // tm

    def body(dy_ref, mod_ref, h_ref, ab_ref, w13_ref, w2_ref, dh_ref, d13_ref, d2_ref, acc13, acc2):
        i = pl.program_id(1)

        @pl.when(i == 0)
        def _():
            acc13[...] = jnp.zeros_like(acc13)
            acc2[...] = jnp.zeros_like(acc2)

        gate = mod_ref[0, k0 + 2:k0 + 3, :]
        do = (dy_ref[...] * (0.5 * gate)).astype(BF16)
        dz = _dot_nt(do, w2_ref[0])
        hb = h_ref[...]
        dh = jnp.zeros((tm, d), F32)
        zs = []
        for q in range(2):
            ab = ab_ref[:, q * w:(q + 1) * w].astype(F32)
            a, b = ab[:, :fp], ab[:, fp:]
            sg = _sigmoid(a)
            sa = a * sg
            dzq = dz[:, q * fp:(q + 1) * fp]
            da = dzq * b * (sg * (1.0 + a * (1.0 - sg)))
            db = dzq * sa
            dab = jnp.concatenate([da, db], axis=1).astype(BF16)
            dh = dh + _dot_nt(dab, w13_ref[q])
            acc13[q] += _dot_tn(hb, dab)
            zs.append((sa * b).astype(BF16))
        acc2[...] += _dot_tn(jnp.concatenate(zs, axis=1), do)
        dh_ref[0] = dh.astype(BF16)

        @pl.when(i == nt - 1)
        def _():
            d13_ref[...] = acc13[...].astype(BF16)
            d2_ref[0] = acc2[...].astype(BF16)

    mod_spec = _mod_spec(d, has_ctx)
    tile = _tile_of(nt, has_ctx)
    return _pcall(
        body, name=name, grid=(npair, nt), xchg=xchg, edges=_edges2(npair, nt),
        out_shape=(jax.ShapeDtypeStruct((npair, t, d), BF16), jax.ShapeDtypeStruct((nb, d, w), BF16),
                   jax.ShapeDtypeStruct((npair, 2 * fp, d), BF16)),
        in_specs=[pl.BlockSpec((tm, d), lambda p, i: (tile(i), 0)), mod_spec,
                  pl.BlockSpec((tm, d), lambda p, i: (tile(i), 0)),
                  pl.BlockSpec((tm, 2 * w), lambda p, i: (tile(i), pair0 + p)),
                  pl.BlockSpec((2, d, w), lambda p, i: (pair0 + p, 0, 0)),
                  pl.BlockSpec((1, 2 * fp, d), lambda p, i: (pair0 + p, 0, 0))],
        out_specs=(pl.BlockSpec((1, tm, d), lambda p, i: (p, tile(i), 0)),
                   pl.BlockSpec((2, d, w), lambda p, i: (p, 0, 0)),
                   pl.BlockSpec((1, 2 * fp, d), lambda p, i: (p, 0, 0))),
        scratch_shapes=[pltpu.VMEM((2, d, w), F32), pltpu.VMEM((2 * fp, d), F32)],
        compiler_params=_params(("arbitrary", "arbitrary")),
    )(dy, modtab, h, ab, w13, w2)


def _ffn_bwd_norm(dy, dhps, x, o, modtab, g, k0, has_ctx, name, xchg=None):
    t, d = dy.shape
    ngrp = 2 if has_ctx else 1
    tm = _tile(has_ctx)
    ndh, nx = len(dhps), len(x)
    lat0 = ngrp - 1

    def body(dy_ref, *rest):
        dhp_refs, x_refs = rest[:ndh], rest[ndh:ndh + nx]
        o_ref, mod_ref, g_ref, dx_ref, dmod_ref, dg_ref = rest[ndh + nx:]
        i = pl.program_id(0)

        @pl.when(i == 0)
        def _():
            dg_ref[...] = jnp.zeros_like(dg_ref)

        @pl.when((i == 0) | (i == ngrp - 1))
        def _():
            dmod_ref[...] = jnp.zeros_like(dmod_ref)

        dh = None
        for ref in dhp_refs:
            for p in range(ref.shape[0]):
                dh = ref[p].astype(F32) if dh is None else dh + ref[p].astype(F32)
        scale = mod_ref[0, k0 + 1:k0 + 2, :]
        gg = g_ref[...]
        rstd, xhat, xn, _ = _norm_fwd(_rows(x_refs), gg, 0.0, scale)
        dxn, dshift, dscale, dg = _norm_bwd(dh, rstd, xhat, xn, gg, scale)
        dyv = dy_ref[...]
        dx_ref[...] = dyv + dxn
        dmod_ref[0, 0:1, :] += dshift
        dmod_ref[0, 1:2, :] += dscale
        dmod_ref[0, 2:3, :] += _colsum(0.5 * dyv * o_ref[...].astype(F32))
        dg_ref[...] += dg

    tile = _tile_of(t // tm, has_ctx)
    row = lambda i: (tile(i), 0)
    return _pcall(
        body, name=name, grid=(t // tm,), xchg=xchg, edges=_edges1(t // tm),
        out_shape=(jax.ShapeDtypeStruct((t - lat0 * tm, d), F32), jax.ShapeDtypeStruct((ngrp, 3, d), F32),
                   jax.ShapeDtypeStruct((1, d), F32)),
        in_specs=[pl.BlockSpec((tm, d), row)]
        + [pl.BlockSpec((a.shape[0], tm, d), lambda i: (0, tile(i), 0)) for a in dhps]
        + _row_specs(nx, tm, d) + [pl.BlockSpec((tm, d), row), _mod_spec(d, has_ctx), _const((1, d))],
        out_specs=(pl.BlockSpec((tm, d), lambda i: (jnp.maximum(i - lat0, 0), 0)), _dmod_spec(3, d, has_ctx),
                   _const((1, d))),
        compiler_params=_params(("arbitrary",)),
    )(dy, *dhps, *x, o, modtab, g)


def _mix_in_fwd(x1, modtab, g, w_in):
    t, d = x1.shape

    def body(x_ref, mod_ref, g_ref, w_ref, h_ref, u_ref):
        _, _, _, h = _norm_fwd(x_ref[...], g_ref[...], mod_ref[0, 3:4, :], mod_ref[0, 4:5, :])
        hb = h.astype(BF16)
        h_ref[...] = hb
        u_ref[...] = _dot(hb, w_ref[...])

    tile = _tile_of(t // TM, True)
    row = lambda i: (tile(i), 0)
    return _pcall(
        body, name="mix_in_fwd", grid=(t // TM,),
        out_shape=(jax.ShapeDtypeStruct((t, d), BF16), jax.ShapeDtypeStruct((t, w_in.shape[1]), F32)),
        in_specs=[pl.BlockSpec((TM, d), row), _mod_spec(d, True), _const((1, d)), _resident(w_in.shape)],
        out_specs=(pl.BlockSpec((TM, d), row), pl.BlockSpec((TM, w_in.shape[1]), row)),
        compiler_params=_params(("arbitrary",)),
    )(x1, modtab, g, w_in)


def _mix_in_bwd(du_pool, du_f, du_b, dx2, h1, x1, modtab, g, w_in, lc):
    t, d = x1.shape
    m = w_in.shape[1]
    hm = m // 2
    nt = t // TM

    def body(dup_ref, duf_ref, dub_ref, dx2_ref, h_ref, x_ref, mod_ref, g_ref, w_ref, dx_ref, dw_ref, dmod_ref, dg_ref, acc):
        i = pl.program_id(0)

        @pl.when(i == 0)
        def _():
            acc[...] = jnp.zeros_like(acc)
            dg_ref[...] = jnp.zeros_like(dg_ref)

        @pl.when(i <= 1)
        def _():
            dmod_ref[...] = jnp.zeros_like(dmod_ref)

        lat = (i > 0).astype(F32)
        valid = (i > 0) | (lax.broadcasted_iota(jnp.int32, (TM, 1), 0) < lc)
        du_s5 = jnp.where(valid, duf_ref[...] + dub_ref[...], 0.0)
        du = jnp.concatenate([dup_ref[...] * lat, du_s5], axis=1).astype(BF16)
        dh = _dot_nt(du, w_ref[...])
        acc[...] += _dot_tn(h_ref[...], du)
        scale = mod_ref[0, 4:5, :]
        gg = g_ref[...]
        rstd, xhat, xn, _ = _norm_fwd(x_ref[...], gg, 0.0, scale)
        dxn, dshift, dscale, dg = _norm_bwd(dh, rstd, xhat, xn, gg, scale)
        dx_ref[...] = dx2_ref[...] * lat + dxn
        dmod_ref[0, 0:1, :] += dshift
        dmod_ref[0, 1:2, :] += dscale
        dg_ref[...] += dg

        @pl.when(i == nt - 1)
        def _():
            dw_ref[...] = acc[...].astype(BF16)

    tile = _tile_of(nt, True)
    row = lambda i: (tile(i), 0)
    lrow = lambda i: (jnp.maximum(i - 1, 0), 0)
    return _pcall(
        body, name="mix_in_bwd", grid=(nt,),
        out_shape=(jax.ShapeDtypeStruct((t, d), F32), jax.ShapeDtypeStruct((d, m), BF16),
                   jax.ShapeDtypeStruct((2, 2, d), F32), jax.ShapeDtypeStruct((1, d), F32)),
        in_specs=[pl.BlockSpec((TM, hm), lrow), pl.BlockSpec((TM, hm), row), pl.BlockSpec((TM, hm), row),
                  pl.BlockSpec((TM, d), lrow), pl.BlockSpec((TM, d), row), pl.BlockSpec((TM, d), row),
                  _mod_spec(d, True), _const((1, d)), _resident(w_in.shape)],
        out_specs=(pl.BlockSpec((TM, d), row), _const((d, m)), _dmod_spec(2, d, True), _const((1, d))),
        scratch_shapes=[pltpu.VMEM((d, m), F32)],
        compiler_params=_params(("arbitrary",)),
    )(du_pool, du_f, du_b, dx2, h1, x1, modtab, g, w_in)


def _pool(v, transpose, name):
    ngrp, rows, n = v.shape
    nchunk = 4
    cw = n // nchunk

    def rowsum(val, lo, hi):
        ri = lax.broadcasted_iota(jnp.int32, (rows, rows), 0)
        ci = lax.broadcasted_iota(jnp.int32, (rows, rows), 1)
        sel = (((ci - ri) >= -lo) & ((ci - ri) <= hi)).astype(BF16)
        v1 = val.astype(BF16)
        r1 = val - v1.astype(F32)
        v2 = r1.astype(BF16)
        v3 = (r1 - v2.astype(F32)).astype(BF16)
        return _dot(sel, v1) + _dot(sel, v2) + _dot(sel, v3)

    def one(v_ref, o_ref, r_scr, a_scr, win):
        lo = win // 2
        hi = win - 1 - lo
        rlo, rhi = (hi, lo) if transpose else (lo, hi)
        ridx = lax.broadcasted_iota(jnp.int32, (rows, 1), 0)
        cnt_r = (jnp.minimum(ridx + hi + 1, rows) - jnp.maximum(ridx - lo, 0)).astype(F32)
        cidx = lax.broadcasted_iota(jnp.int32, (1, n), 1) // LANE
        cnt_c = (jnp.minimum(cidx + hi + 1, GRID_W) - jnp.maximum(cidx - lo, 0)).astype(F32)
        if not transpose:
            for k in range(nchunk):
                sl = slice(k * cw, (k + 1) * cw)
                r_scr[:, sl] = rowsum(v_ref[0, :, sl], rlo, rhi) / cnt_r
        else:
            r_scr[...] = v_ref[0] / cnt_c
        a_scr[...] = r_scr[...]
        for j in range(-rlo, rhi + 1):
            if j == 0:
                continue
            c0, c1 = max(0, -j), min(GRID_W, GRID_W - j)
            a_scr[:, c0 * LANE:c1 * LANE] += r_scr[:, (c0 + j) * LANE:(c1 + j) * LANE]
        if not transpose:
            o_ref[0] = a_scr[...] / cnt_c - v_ref[0]
        else:
            for k in range(nchunk):
                sl = slice(k * cw, (k + 1) * cw)
                o_ref[0, :, sl] = rowsum(a_scr[:, sl] / cnt_r, rlo, rhi) - v_ref[0, :, sl]

    def body(v_ref, o_ref, r_scr, a_scr):
        grp = pl.program_id(0)
        for k, win in enumerate(POOL_WINDOWS):
            @pl.when(grp == k)
            def _(win=win):
                one(v_ref, o_ref, r_scr, a_scr, win)

    spec = pl.BlockSpec((1, rows, n), lambda k: (k, 0, 0))
    return _pcall(
        body, name=name, grid=(ngrp,), out_shape=jax.ShapeDtypeStruct(v.shape, F32),
        in_specs=[spec], out_specs=spec,
        scratch_shapes=[pltpu.VMEM((rows, n), F32), pltpu.VMEM((rows, n), F32)],
        compiler_params=_params(("arbitrary",)),
    )(v)


def _cmul(ar, ai, br, bi):
    return ar * br - ai * bi, ar * bi + ai * br


def _s5_prep(a_re, a_im, log_dt, c_re, c_im, kvec):
    q, nc, p = c_re.shape

    def body(ar_ref, ai_ref, ldt_ref, cr_ref, ci_ref, k_ref, cpr_ref, cpi_ref, tr_ref, ti_ref):
        lr, li = ar_ref[...], ai_ref[...]
        dt = jnp.exp(ldt_ref[...])
        kk = k_ref[...]
        mag = jnp.exp(kk * (lr * dt))
        ph = kk * (li * dt)
        tr_ref[...] = mag * jnp.cos(ph)
        ti_ref[...] = mag * jnp.sin(ph)
        m1 = jnp.exp(lr * dt)
        abr, abi = m1 * jnp.cos(li * dt), m1 * jnp.sin(li * dt)
        den = lr * lr + li * li
        xr, xi = abr - 1.0, abi
        cfr, cfi = (xr * lr + xi * li) / den, (xi * lr - xr * li) / den
        pr, pi_ = _cmul(cr_ref[...], ci_ref[...], cfr, cfi)
        cpr_ref[...] = pr
        cpi_ref[...] = pi_

    return _pcall(
        body, name="s5_prep",
        out_shape=(jax.ShapeDtypeStruct((q, nc, p), F32), jax.ShapeDtypeStruct((q, nc, p), F32),
                   jax.ShapeDtypeStruct((q, NTAB, p), F32), jax.ShapeDtypeStruct((q, NTAB, p), F32)),
        compiler_params=_params(None),
    )(a_re, a_im, log_dt, c_re, c_im, kvec)


def _s5_param_bwd(a_re, a_im, log_dt, c_re, c_im, dab_r, dab_i, dcp_r, dcp_i):
    q, nc, p = c_re.shape

    def body(ar_ref, ai_ref, ldt_ref, cr_ref, ci_ref, dar_ref, dai_ref, dcr_ref, dci_ref,
             gar_ref, gai_ref, gdt_ref, gcr_ref, gci_ref):
        lr, li = ar_ref[...], ai_ref[...]
        dt = jnp.exp(ldt_ref[...])
        m1 = jnp.exp(lr * dt)
        abr, abi = m1 * jnp.cos(li * dt), m1 * jnp.sin(li * dt)
        den = lr * lr + li * li
        xr, xi = abr - 1.0, abi
        cfr, cfi = (xr * lr + xi * li) / den, (xi * lr - xr * li) / den
        cr, ci = cr_ref[...], ci_ref[...]
        dcr, dci = dcr_ref[...], dci_ref[...]
        gcr, gci = _cmul(dcr, dci, cfr, -cfi)
        gcr_ref[...] = gcr
        gci_ref[...] = gci
        t_r, t_i = _cmul(cr, -ci, dcr, dci)
        dcf_r = jnp.sum(t_r, axis=1, keepdims=True)
        dcf_i = jnp.sum(t_i, axis=1, keepdims=True)
        ilr, ili = lr / den, -li / den
        u_r, u_i = _cmul(dcf_r, dcf_i, ilr, -ili)
        dab_r_, dab_i_ = dar_ref[...] + u_r, dai_ref[...] + u_i
        v_r, v_i = _cmul(dab_r_, dab_i_, abr, -abi)
        cl_r, cl_i = _cmul(cfr, cfi, ilr, ili)
        w_r, w_i = _cmul(dcf_r, dcf_i, cl_r, -cl_i)
        gar_ref[...] = v_r * dt - w_r
        gai_ref[...] = v_i * dt - w_i
        la_r, la_i = _cmul(lr, li, abr, abi)
        ddt = la_r * dab_r_ + la_i * dab_i_
        gdt_ref[...] = dt * jnp.sum(ddt, axis=2, keepdims=True)

    return _pcall(
        body, name="s5_param_bwd",
        out_shape=(jax.ShapeDtypeStruct((q, 1, p), F32), jax.ShapeDtypeStruct((q, 1, p), F32),
                   jax.ShapeDtypeStruct((q, 1, p), F32), jax.ShapeDtypeStruct((q, nc, p), F32),
                   jax.ShapeDtypeStruct((q, nc, p), F32)),
        compiler_params=_params(None),
    )(a_re, a_im, log_dt, c_re, c_im, dab_r, dab_i, dcp_r, dcp_i)


def _bcast8(row, c):
    return jnp.broadcast_to(row, (8, c))


def _slab(j):
    return j * 8 if isinstance(j, int) else pl.multiple_of(j * 8, 8)


def _scan_head(xr, xi, tre_ref, tim_ref, car_r, car_i, desc, conj, cs):
    sg = -1.0 if conj else 1.0
    ar = _bcast8(tre_ref[0:1, :], cs)
    ai = sg * _bcast8(tim_ref[0:1, :], cs)

    def p1(jj, carry):
        hr, hi = carry
        off = _slab(SEG - 1 - jj if desc else jj)
        pr, pi_ = _cmul(ar, ai, hr, hi)
        nr = pr + xr[pl.ds(off, 8), :]
        ni = pi_ + xi[pl.ds(off, 8), :]
        xr[pl.ds(off, 8), :] = nr
        xi[pl.ds(off, 8), :] = ni
        return nr, ni

    zero = jnp.zeros((8, cs), F32)
    ir, ii = lax.fori_loop(0, SEG, p1, (zero, zero), unroll=S5_UNROLL)
    cin_r, cin_i = car_r[...], car_i[...]
    rows = lax.broadcasted_iota(jnp.int32, (8, cs), 0)
    for s, krow in ((1, SEG), (2, SEG + 1), (4, SEG + 3)):
        keep, sh = (rows < 8 - s, 8 - s) if desc else (rows >= s, s)
        sr = jnp.where(keep, pltpu.roll(ir, sh, 0), 0.0)
        si = jnp.where(keep, pltpu.roll(ii, sh, 0), 0.0)
        pr, pi_ = _cmul(tre_ref[krow:krow + 1, :], sg * tim_ref[krow:krow + 1, :], sr, si)
        ir, ii = ir + pr, ii + pi_
    q0 = SEG + 8 if desc else SEG
    pr, pi_ = _cmul(tre_ref[q0:q0 + 8, :], sg * tim_ref[q0:q0 + 8, :], cin_r, cin_i)
    fr, fi = ir + pr, ii + pi_
    keep, sh, edge = (rows < 7, 7, 0) if desc else (rows >= 1, 1, 7)
    cs_r = jnp.where(keep, pltpu.roll(fr, sh, 0), cin_r)
    cs_i = jnp.where(keep, pltpu.roll(fi, sh, 0), cin_i)
    car_r[...] = _bcast8(fr[edge:edge + 1, :], cs)
    car_i[...] = _bcast8(fi[edge:edge + 1, :], cs)
    return cs_r, cs_i, cin_r, cin_i


def _pow_row(tre_ref, tim_ref, j, desc, conj, cs):
    k = SEG - 1 - j if desc else j
    sg = -1.0 if conj else 1.0
    return _bcast8(tre_ref[pl.ds(k, 1), :], cs), sg * _bcast8(tim_ref[pl.ds(k, 1), :], cs)


def _s5_fwd(up, bre, bim, cmr, cmi, tre, tim, dskip, order, nproc, desc, with_skip, name):
    t, cu = up.shape
    nsb = bre.shape[0]
    cb = cu // nsb
    cs = bre.shape[2]
    nch = t // TC

    def body(u_ref, bre_ref, bim_ref, cmr_ref, cmi_ref, tre_ref, tim_ref, dsk_ref,
             y_ref, hr_ref, hi_ref, cinr_ref, cini_ref, car_r, car_i):
        i = pl.program_id(0)

        @pl.when(i == 0)
        def _():
            car_r[...] = jnp.zeros_like(car_r)
            car_i[...] = jnp.zeros_like(car_i)

        for sb in range(nsb):
            col, ch = pl.ds(sb * cs, cs), slice(sb * cb, (sb + 1) * cb)
            hr_v, hi_v = hr_ref.at[:, col], hi_ref.at[:, col]
            tre_v, tim_v = tre_ref.at[:, col], tim_ref.at[:, col]
            u = u_ref[:, ch]
            ub = u.astype(BF16)
            hr_v[...] = _dot(ub, bre_ref[sb])
            hi_v[...] = _dot(ub, bim_ref[sb])
            cs_r, cs_i, cin_r, cin_i = _scan_head(hr_v, hi_v, tre_v, tim_v, car_r.at[sb], car_i.at[sb], desc, False, cs)
            cinr_ref[:, col] = cin_r
            cini_ref[:, col] = cin_i

            def p2(j, _, hr_v=hr_v, hi_v=hi_v, tre_v=tre_v, tim_v=tim_v, cs_r=cs_r, cs_i=cs_i):
                off = _slab(j)
                pw_r, pw_i = _pow_row(tre_v, tim_v, j, desc, False, cs)
                pr, pi_ = _cmul(pw_r, pw_i, cs_r, cs_i)
                hr_v[pl.ds(off, 8), :] = hr_v[pl.ds(off, 8), :] + pr
                hi_v[pl.ds(off, 8), :] = hi_v[pl.ds(off, 8), :] + pi_
                return 0

            lax.fori_loop(0, SEG, p2, 0, unroll=S5_UNROLL)
            y = _dot(hr_v[...].astype(BF16), cmr_ref[sb]) - _dot(hi_v[...].astype(BF16), cmi_ref[sb])
            if with_skip:
                y = y + dsk_ref[:, ch] * u
            y_ref[:, ch] = y

    blk = lambda i: (order(i), 0)
    return _pcall(
        body, name=name, grid=(nproc,),
        out_shape=(jax.ShapeDtypeStruct((t, cu), F32), jax.ShapeDtypeStruct((t, nsb * cs), F32),
                   jax.ShapeDtypeStruct((t, nsb * cs), F32), jax.ShapeDtypeStruct((nch * 8, nsb * cs), F32),
                   jax.ShapeDtypeStruct((nch * 8, nsb * cs), F32)),
        in_specs=[pl.BlockSpec((TC, cu), blk), _const(bre.shape), _const(bim.shape), _const(cmr.shape),
                  _const(cmi.shape), _const(tre.shape), _const(tim.shape), _const(dskip.shape)],
        out_specs=(pl.BlockSpec((TC, cu), blk), pl.BlockSpec((TC, nsb * cs), blk), pl.BlockSpec((TC, nsb * cs), blk),
                   pl.BlockSpec((8, nsb * cs), blk), pl.BlockSpec((8, nsb * cs), blk)),
        scratch_shapes=[pltpu.VMEM((nsb, 8, cs), F32), pltpu.VMEM((nsb, 8, cs), F32)],
        compiler_params=_params(("arbitrary",)),
    )(up, bre, bim, cmr, cmi, tre, tim, dskip)


def _s5_bwd(dyp, up, hr, hi, cinr, cini, bre, bim, ctr, cti, tre, tim, dskip, order, nproc, desc, with_skip, name,
            xchg=None):
    t, cu = up.shape
    nsb = bre.shape[0]
    cb = cu // nsb
    cs = bre.shape[2]
    nch = t // TC
    adesc = not desc

    def body(dy_ref, u_ref, hr_ref, hi_ref, cinr_ref, cini_ref, bre_ref, bim_ref, ctr_ref, cti_ref, tre_ref, tim_ref,
             dsk_ref, du_ref, dar_ref, dai_ref, dcr_ref, dci_ref, dbr_ref, dbi_ref, dd_ref,
             car_r, car_i, mr_all, mi_all, acc_r, acc_i):
        i = pl.program_id(0)

        @pl.when(i == 0)
        def _():
            for ref in (car_r, car_i, acc_r, acc_i, dcr_ref, dci_ref, dbr_ref, dbi_ref, dd_ref):
                ref[...] = jnp.zeros_like(ref)

        rows = lax.broadcasted_iota(jnp.int32, (8, cs), 0)
        for sb in range(nsb):
            col, ch = pl.ds(sb * cs, cs), slice(sb * cb, (sb + 1) * cb)
            hr_v, hi_v = hr_ref.at[:, col], hi_ref.at[:, col]
            tre_v, tim_v = tre_ref.at[:, col], tim_ref.at[:, col]
            mr, mi = mr_all.at[sb % 2], mi_all.at[sb % 2]
            dy = dy_ref[:, ch]
            dyb = dy.astype(BF16)
            u = u_ref[:, ch]
            ub = u.astype(BF16)
            mr[...] = _dot(dyb, ctr_ref[sb])
            mi[...] = -_dot(dyb, cti_ref[sb])
            cs_r, cs_i, _, _ = _scan_head(mr, mi, tre_v, tim_v, car_r.at[sb], car_i.at[sb], adesc, True, cs)

            def fix(j, hp_r, hp_i, acc, mr=mr, mi=mi, tre_v=tre_v, tim_v=tim_v, cs_r=cs_r, cs_i=cs_i):
                off = _slab(j)
                pw_r, pw_i = _pow_row(tre_v, tim_v, j, adesc, True, cs)
                pr, pi_ = _cmul(pw_r, pw_i, cs_r, cs_i)
                m_r = mr[pl.ds(off, 8), :] + pr
                m_i = mi[pl.ds(off, 8), :] + pi_
                mr[pl.ds(off, 8), :] = m_r
                mi[pl.ds(off, 8), :] = m_i
                a_r, a_i = acc
                return a_r + hp_r * m_r + hp_i * m_i, a_i + hp_r * m_i - hp_i * m_r

            edge_j, src, keep, sh = (SEG - 1, 0, rows < 7, 7) if desc else (0, (SEG - 1) * 8, rows >= 1, 1)
            h0r = jnp.where(keep, pltpu.roll(hr_v[src:src + 8, :], sh, 0), cinr_ref[:, col])
            h0i = jnp.where(keep, pltpu.roll(hi_v[src:src + 8, :], sh, 0), cini_ref[:, col])
            acc = fix(edge_j, h0r, h0i, (acc_r[sb], acc_i[sb]))

            def p2(jj, acc, fix=fix, hr_v=hr_v, hi_v=hi_v):
                j = jj if desc else jj + 1
                offp = _slab(j + 1 if desc else j - 1)
                return fix(j, hr_v[pl.ds(offp, 8), :], hi_v[pl.ds(offp, 8), :], acc)

            a_r, a_i = lax.fori_loop(0, SEG - 1, p2, acc, unroll=S5_UNROLL)
            acc_r[sb] = a_r
            acc_i[sb] = a_i

            mrb, mib = mr[...].astype(BF16), mi[...].astype(BF16)
            du = _dot_nt(mrb, bre_ref[sb]) + _dot_nt(mib, bim_ref[sb])
            if with_skip:
                du = du + dsk_ref[:, ch] * dy
                dd_ref[:, ch] += _colsum(dy * u)
            du_ref[:, ch] = du
            dbr_ref[sb] += _dot_tn(ub, mrb)
            dbi_ref[sb] += _dot_tn(ub, mib)
            dcr_ref[sb] += _dot_tn(dyb, hr_v[...].astype(BF16))
            dci_ref[sb] -= _dot_tn(dyb, hi_v[...].astype(BF16))

            @pl.when(i == nproc - 1)
            def _(sb=sb, a_r=a_r, a_i=a_i):
                dar_ref[sb] = _colsum(a_r)
                dai_ref[sb] = _colsum(a_i)

    blk = lambda i: (order(i), 0)
    mat = (nsb, cb, cs)
    return _pcall(
        body, name=name, grid=(nproc,), xchg=xchg, edges=_edges1(nproc),
        out_shape=(jax.ShapeDtypeStruct((t, cu), F32),
                   jax.ShapeDtypeStruct((nsb, 1, cs), F32), jax.ShapeDtypeStruct((nsb, 1, cs), F32),
                   jax.ShapeDtypeStruct(mat, F32), jax.ShapeDtypeStruct(mat, F32),
                   jax.ShapeDtypeStruct(mat, F32), jax.ShapeDtypeStruct(mat, F32),
                   jax.ShapeDtypeStruct((1, cu), F32)),
        in_specs=[pl.BlockSpec((TC, cu), blk), pl.BlockSpec((TC, cu), blk), pl.BlockSpec((TC, nsb * cs), blk),
                  pl.BlockSpec((TC, nsb * cs), blk), pl.BlockSpec((8, nsb * cs), blk), pl.BlockSpec((8, nsb * cs), blk),
                  _const(mat), _const(mat), _const(mat), _const(mat), _const(tre.shape), _const(tim.shape),
                  _const((1, cu))],
        out_specs=(pl.BlockSpec((TC, cu), blk), _const((nsb, 1, cs)), _const((nsb, 1, cs)),
                   _const(mat), _const(mat), _const(mat), _const(mat), _const((1, cu))),
        scratch_shapes=[pltpu.VMEM((nsb, 8, cs), F32), pltpu.VMEM((nsb, 8, cs), F32), pltpu.VMEM((2, TC, cs), F32),
                        pltpu.VMEM((2, TC, cs), F32), pltpu.VMEM((nsb, 8, cs), F32), pltpu.VMEM((nsb, 8, cs), F32)],
        compiler_params=_params(("arbitrary",)),
    )(dyp, up, hr, hi, cinr, cini, bre, bim, ctr, cti, tre, tim, dskip)


def _gelu(x):
    k = math.sqrt(2.0 / math.pi)
    return 0.5 * x * (1.0 + jnp.tanh(k * (x + 0.044715 * (x * x * x))))


def _gelu_grad(x):
    k = math.sqrt(2.0 / math.pi)
    th = jnp.tanh(k * (x + 0.044715 * (x * x * x)))
    return 0.5 * (1.0 + th) + 0.5 * x * (1.0 - th * th) * (k * (1.0 + 3.0 * 0.044715 * (x * x)))


def _mix_out_fwd(yf, yb, dpool, x1, modtab, w_pool, pscale, w_glu, w_out):
    l, hm = dpool.shape
    d = x1.shape[1]

    def body(yf_ref, yb_ref, dp_ref, x_ref, mod_ref, wp_ref, ps_ref, wg_ref, wo_ref, x2_ref, yp_ref, cat_ref, mix_ref):
        ypre = yf_ref[...] + yb_ref[...]
        yp_ref[...] = ypre
        yg = _gelu(ypre)
        y2 = yg * _sigmoid(_dot(yg.astype(BF16), wg_ref[...]))
        po = _dot(dp_ref[...].astype(BF16), wp_ref[...]) * ps_ref[...]
        cat = jnp.concatenate([po, y2], axis=1).astype(BF16)
        cat_ref[...] = cat
        mix = _dot(cat, wo_ref[...])
        mix_ref[...] = mix.astype(BF16)
        x2_ref[...] = x_ref[...] + mod_ref[0, 5:6, :] * mix

    row = lambda i: (i, 0)
    lrow = row
    return _pcall(
        body, name="mix_out_fwd", grid=(l // TM,),
        out_shape=(jax.ShapeDtypeStruct((l, d), F32), jax.ShapeDtypeStruct((l, hm), F32),
                   jax.ShapeDtypeStruct((l, d), BF16), jax.ShapeDtypeStruct((l, d), BF16)),
        in_specs=[pl.BlockSpec((TM, hm), lrow), pl.BlockSpec((TM, hm), lrow), pl.BlockSpec((TM, hm), row),
                  pl.BlockSpec((TM, d), lrow), _mod_spec(d, False), _const(w_pool.shape), _const((1, hm)),
                  _const(w_glu.shape), _resident(w_out.shape)],
        out_specs=(pl.BlockSpec((TM, d), row), pl.BlockSpec((TM, hm), row), pl.BlockSpec((TM, d), row),
                   pl.BlockSpec((TM, d), row)),
        compiler_params=_params(("arbitrary",)),
    )(yf, yb, dpool, x1, modtab, w_pool, pscale, w_glu, w_out)


def _mix_out_bwd(dx2, mix, cat, ypre, dpool, modtab, w_pool, pscale, w_glu, w_out):
    l, hm = dpool.shape
    d = dx2.shape[1]
    nt = l // TML

    def body(dx_ref, mix_ref, cat_ref, yp_ref, dp_ref, mod_ref, wp_ref, ps_ref, wg_ref, wo_ref,
             dyp_ref, ddp_ref, dwo_ref, dwg_ref, dwp_ref, dps_ref, dgate_ref, acc_o, acc_g):
        i = pl.program_id(0)

        @pl.when(i == 0)
        def _():
            for ref in (acc_o, acc_g, dwp_ref, dps_ref, dgate_ref):
                ref[...] = jnp.zeros_like(ref)

        dx = dx_ref[...]
        dgate_ref[...] += _colsum(dx * mix_ref[...].astype(F32))
        dmix = (dx * mod_ref[0, 5:6, :]).astype(BF16)
        dcat = _dot_nt(dmix, wo_ref[...])
        acc_o[...] += _dot_tn(cat_ref[...], dmix)
        dpo, dy2 = dcat[:, :hm], dcat[:, hm:]
        dpb = dp_ref[...].astype(BF16)
        pp = _dot(dpb, wp_ref[...])
        dps_ref[...] += _colsum(dpo * pp)
        dpp = (dpo * ps_ref[...]).astype(BF16)
        ddp_ref[...] = _dot_nt(dpp, wp_ref[...])
        dwp_ref[...] += _dot_tn(dpb, dpp)
        ypre = yp_ref[...]
        yg = _gelu(ypre)
        ygb = yg.astype(BF16)
        s = _sigmoid(_dot(ygb, wg_ref[...]))
        dq = (dy2 * yg * s * (1.0 - s)).astype(BF16)
        dyg = dy2 * s + _dot_nt(dq, wg_ref[...])
        acc_g[...] += _dot_tn(ygb, dq)
        dyp_ref[...] = dyg * _gelu_grad(ypre)

        @pl.when(i == nt - 1)
        def _():
            dwo_ref[...] = acc_o[...].astype(BF16)
            dwg_ref[...] = acc_g[...].astype(BF16)

    row = lambda i: (i, 0)
    return _pcall(
        body, name="mix_out_bwd", grid=(nt,),
        out_shape=(jax.ShapeDtypeStruct((l, hm), F32), jax.ShapeDtypeStruct((l, hm), F32),
                   jax.ShapeDtypeStruct((d, d), BF16), jax.ShapeDtypeStruct((hm, hm), BF16),
                   jax.ShapeDtypeStruct((hm, hm), F32), jax.ShapeDtypeStruct((1, hm), F32),
                   jax.ShapeDtypeStruct((1, d), F32)),
        in_specs=[pl.BlockSpec((TML, d), row), pl.BlockSpec((TML, d), row), pl.BlockSpec((TML, d), row),
                  pl.BlockSpec((TML, hm), row), pl.BlockSpec((TML, hm), row), _mod_spec(d, False),
                  _const(w_pool.shape), _const((1, hm)), _const(w_glu.shape), _resident(w_out.shape)],
        out_specs=(pl.BlockSpec((TML, hm), row), pl.BlockSpec((TML, hm), row), _const((d, d)), _const((hm, hm)),
                   _const((hm, hm)), _const((1, hm)), _const((1, d))),
        scratch_shapes=[pltpu.VMEM((d, d), F32), pltpu.VMEM((hm, hm), F32)],
        compiler_params=_params(("arbitrary",)),
    )(dx2, mix, cat, ypre, dpool, modtab, w_pool, pscale, w_glu, w_out)


def _loss_head(x3, target, g):
    l, d = x3.shape

    def body(x_ref, t_ref, g_ref, dx_ref, loss_ref, dg_ref):
        i = pl.program_id(0)

        @pl.when(i == 0)
        def _():
            loss_ref[...] = jnp.zeros_like(loss_ref)
            dg_ref[...] = jnp.zeros_like(dg_ref)

        xx = x_ref[...]
        gg = g_ref[...]
        rstd = lax.rsqrt(jnp.mean(xx * xx, axis=-1, keepdims=True) + EPS)
        xhat = xx * rstd
        err = xhat * gg - t_ref[...]
        row_loss = jnp.mean(err * err, axis=-1, keepdims=True)
        loss_ref[...] += 0.5 * jnp.sum(row_loss, axis=0, keepdims=True)
        dout = err * (1.0 / d)
        dg_ref[...] += _colsum(dout * xhat)
        dxhat = dout * gg
        dx_ref[...] = rstd * (dxhat - xhat * jnp.mean(dxhat * xhat, axis=-1, keepdims=True))

    row = lambda i: (i, 0)
    return _pcall(
        body, name="loss_head", grid=(l // TML,),
        out_shape=(jax.ShapeDtypeStruct((l, d), F32), jax.ShapeDtypeStruct((1, LANE), F32),
                   jax.ShapeDtypeStruct((1, d), F32)),
        in_specs=[pl.BlockSpec((TML, d), row), pl.BlockSpec((TML, d), row), _const((1, d))],
        out_specs=(pl.BlockSpec((TML, d), row), _const((1, LANE)), _const((1, d))),
        compiler_params=_params(("arbitrary",)),
    )(x3, target, g)


def _adamw(parts, w, m, v, name):
    shape = w.shape
    c = shape[-1]
    r = int(np.prod(shape)) // c
    npart = parts.shape[0]
    tr = r
    for cand in (1024, 512, 256, 128, 64, 32, 16):
        if r % cand == 0 and cand * max(c, LANE) * 4 <= ADAM_BLOCK_BYTES:
            tr = cand
            break
    c1 = 1.0 - ADAM_B1
    c2 = 1.0 - ADAM_B2
    bc1 = 1.0 - ADAM_B1 ** ADAM_STEP
    bc2 = 1.0 - ADAM_B2 ** ADAM_STEP

    def body(p_ref, w_ref, m_ref, v_ref, g_ref, d_ref, mo_ref, vo_ref):
        g = p_ref[0].astype(F32)
        for k in range(1, npart):
            g = g + p_ref[k].astype(F32)
        mn = ADAM_B1 * m_ref[...] + c1 * g
        vn = ADAM_B2 * v_ref[...] + c2 * (g * g)
        m_hat = mn / bc1
        v_hat = vn / bc2
        g_ref[...] = g
        mo_ref[...] = mn
        vo_ref[...] = vn
        d_ref[...] = -ADAM_LR * (m_hat / (jnp.sqrt(v_hat) + ADAM_EPS) + ADAM_WD * w_ref[...])

    spec = pl.BlockSpec((tr, c), lambda i: (i, 0))
    outs = _pcall(
        body, name=name, grid=(r // tr,),
        out_shape=tuple(jax.ShapeDtypeStruct((r, c), F32) for _ in range(4)),
        in_specs=[pl.BlockSpec((npart, tr, c), lambda i: (0, i, 0)), spec, spec, spec],
        out_specs=(spec, spec, spec, spec),
        compiler_params=_params(("arbitrary",)),
    )(parts.reshape(npart, r, c), w.reshape(r, c), m.reshape(r, c), v.reshape(r, c))
    return tuple(o.reshape(shape) for o in outs)


def _perm(a):
    t, c = a.shape
    return a.reshape(t // TC, 8, SEG, c).transpose(0, 2, 1, 3).reshape(t, c)


def _unperm(a):
    t, c = a.shape
    return a.reshape(t // TC, SEG, 8, c).transpose(0, 2, 1, 3).reshape(t, c)


def _blockdiag(xb):
    n, a, b = xb.shape[-3:]
    eye = jnp.eye(n, dtype=xb.dtype)
    out = xb[..., :, :, None, :] * eye[:, None, :, None]
    return out.reshape(xb.shape[:-3] + (n * a, n * b))


def _diagblocks(mat, n):
    lead = mat.shape[:-2]
    a, b = mat.shape[-2] // n, mat.shape[-1] // n
    m5 = mat.reshape(lead + (n, a, n, b))
    nl = len(lead)
    dg = jnp.diagonal(m5, axis1=nl, axis2=nl + 2)
    return jnp.moveaxis(dg, -1, nl)


def _to_grid(a, rows):
    return a.reshape(rows, GRID_W, 4, LANE).transpose(2, 0, 1, 3).reshape(4, rows, GRID_W * LANE)


def _from_grid(a, rows):
    return a.reshape(4, rows, GRID_W, LANE).transpose(1, 2, 0, 3).reshape(rows * GRID_W, 4 * LANE)


def _pad128(a):
    flat = a.reshape(-1)
    pad = (-flat.shape[0]) % LANE
    return jnp.pad(flat, (0, pad)) if pad else flat


def kernel(x, c, ctx, c_ctx, norm_g, w_ada, b_ada, ffn_w1, ffn_w3, ffn_w2, w_in, pool_w, pool_scale, s5_a_re, s5_a_im, s5_log_dt, s5_b_re, s5_b_im, s5_c_re, s5_c_im, s5_d, s5_w_glu, w_out, final_g, loss_target, m_c_ctx, m_norm_g, m_w_ada, m_b_ada, m_ffn_w1, m_ffn_w3, m_ffn_w2, m_w_in, m_pool_w, m_pool_scale, m_s5_a_re, m_s5_a_im, m_s5_log_dt, m_s5_b_re, m_s5_b_im, m_s5_c_re, m_s5_c_im, m_s5_d, m_s5_w_glu, m_w_out, m_final_g, v_c_ctx, v_norm_g, v_w_ada, v_b_ada, v_ffn_w1, v_ffn_w3, v_ffn_w2, v_w_in, v_pool_w, v_pool_scale, v_s5_a_re, v_s5_a_im, v_s5_log_dt, v_s5_b_re, v_s5_b_im, v_s5_c_re, v_s5_c_im, v_s5_d, v_s5_w_glu, v_w_out, v_final_g):
    weights = dict(c_ctx=c_ctx, norm_g=norm_g, w_ada=w_ada, b_ada=b_ada, ffn_w1=ffn_w1, ffn_w3=ffn_w3, ffn_w2=ffn_w2,
                   w_in=w_in, pool_w=pool_w, pool_scale=pool_scale, s5_a_re=s5_a_re, s5_a_im=s5_a_im,
                   s5_log_dt=s5_log_dt, s5_b_re=s5_b_re, s5_b_im=s5_b_im, s5_c_re=s5_c_re, s5_c_im=s5_c_im, s5_d=s5_d,
                   s5_w_glu=s5_w_glu, w_out=w_out, final_g=final_g)
    mom_m = dict(c_ctx=m_c_ctx, norm_g=m_norm_g, w_ada=m_w_ada, b_ada=m_b_ada, ffn_w1=m_ffn_w1, ffn_w3=m_ffn_w3,
                 ffn_w2=m_ffn_w2, w_in=m_w_in, pool_w=m_pool_w, pool_scale=m_pool_scale, s5_a_re=m_s5_a_re,
                 s5_a_im=m_s5_a_im, s5_log_dt=m_s5_log_dt, s5_b_re=m_s5_b_re, s5_b_im=m_s5_b_im, s5_c_re=m_s5_c_re,
                 s5_c_im=m_s5_c_im, s5_d=m_s5_d, s5_w_glu=m_s5_w_glu, w_out=m_w_out, final_g=m_final_g)
    mom_v = dict(c_ctx=v_c_ctx, norm_g=v_norm_g, w_ada=v_w_ada, b_ada=v_b_ada, ffn_w1=v_ffn_w1, ffn_w3=v_ffn_w3,
                 ffn_w2=v_ffn_w2, w_in=v_w_in, pool_w=v_pool_w, pool_scale=v_pool_scale, s5_a_re=v_s5_a_re,
                 s5_a_im=v_s5_a_im, s5_log_dt=v_s5_log_dt, s5_b_re=v_s5_b_re, s5_b_im=v_s5_b_im, s5_c_re=v_s5_c_re,
                 s5_c_im=v_s5_c_im, s5_d=v_s5_d, s5_w_glu=v_s5_w_glu, w_out=v_w_out, final_g=v_final_g)

    l, d = x.shape[1], x.shape[2]
    lc = ctx.shape[1]
    t = l + TM
    rows = l // GRID_W
    fblk = ffn_w1.shape[-1]
    ngrp, gp = s5_a_re.shape[2], s5_a_re.shape[3]
    gc = s5_b_re.shape[3]
    hm = ngrp * gc
    nsb = 4
    gsb = ngrp // nsb
    assert lc == TC and TM % TC == 0 and l % TM == 0 and TM == TML and hm == 4 * LANE and ngrp * gp == nsb * 512
    me = 4 * lax.axis_index("x") + 2 * lax.axis_index("y") + lax.axis_index("c")

    padc = ((0, 0), (0, 0), (0, FPAD - fblk))
    w13_loc = [jnp.concatenate([jnp.pad(ffn_w1[0, k], padc[1:]), jnp.pad(ffn_w3[0, k], padc[1:])], axis=-1).astype(BF16)
               for k in range(2)]
    w2_loc = [jnp.pad(ffn_w2[0, k], ((0, FPAD - fblk), (0, 0))).astype(BF16) for k in range(2)]
    gath = _exchange([(w13_loc[0], GATHER_REL), (w2_loc[0], GATHER_REL), (c, GATHER), (norm_g[0], GATHER)],
                     "exchange_weights")
    w13_0 = gath[0]
    w2_0 = gath[1].reshape(NDEV // 2, 2 * FPAD, d)
    c_all = gath[2].reshape(NDEV, d)
    g_all = gath[3].transpose(1, 0, 2).reshape(3, d)

    wa = w_ada.shape[-1]
    c16 = jnp.concatenate([c_all, c_ctx.reshape(1, d), jnp.zeros((7, d), F32)], axis=0)
    b_blk = lax.dynamic_slice_in_dim(b_ada, me * wa, wa, axis=1)
    mod_sl = _ada_fwd(c16, w_ada[0], b_blk)
    gath = _exchange([(mod_sl, GATHER)], "exchange_mod")
    mod_full = gath[0].transpose(1, 0, 2).reshape(16, NDEV * wa)
    mod_l = lax.dynamic_index_in_dim(mod_full, me, axis=0, keepdims=False).reshape(9, d)
    modtab = jnp.stack([mod_full[8].reshape(9, d), mod_l])

    x0 = [jnp.pad(ctx[0], ((0, TM - lc), (0, 0))), x[0]]
    g0, g1, g2 = g_all[0:1], g_all[1:2], g_all[2:3]
    x1, h0, ab0, o0, w13_1, w2_1, w_in_f, w_out_f, w_glu_f = _ffn_fwd(
        x0, modtab, g0, w13_0, w2_0, 0, True, "ffn0_fwd",
        xchg=[(w13_loc[1], GATHER_REL), (w2_loc[1], GATHER_REL), (w_in[0].astype(BF16), GATHER),
              (w_out[0].astype(BF16), GATHER), (s5_w_glu[0].astype(BF16), GATHER)])
    w2_1 = w2_1.reshape(NDEV // 2, 2 * FPAD, d)
    w_in_f = w_in_f.reshape(d, -1)
    w_out_f = w_out_f.reshape(-1, d)
    w_glu_f = w_glu_f.reshape(hm, hm)
    h1, u = _mix_in_fwd(x1, modtab, g1, w_in_f)

    v_grid = _to_grid(u[:l, :hm], rows)
    dpool = _from_grid(_pool(v_grid, False, "pool_fwd"), rows)
    w_pool_bd = _blockdiag(pool_w[0]).astype(BF16)

    q = 2 * ngrp
    kvec = np.concatenate([np.arange(1, SEG + 1), SEG * np.arange(1, 9), SEG * np.arange(8, 0, -1)]).astype(np.float32)
    a_re3, a_im3 = s5_a_re[0].reshape(q, 1, gp), s5_a_im[0].reshape(q, 1, gp)
    ldt3 = jnp.broadcast_to(s5_log_dt[0].reshape(q, 1, 1), (q, 1, gp))
    c_re3, c_im3 = s5_c_re[0].reshape(q, gc, gp), s5_c_im[0].reshape(q, gc, gp)
    cp_re, cp_im, tab_re, tab_im = _s5_prep(a_re3, a_im3, ldt3, c_re3, c_im3, jnp.broadcast_to(jnp.asarray(kvec).reshape(1, NTAB, 1), (1, NTAB, gp)))
    tabs = [(tab_re.reshape(2, ngrp, NTAB, gp)[k].transpose(1, 0, 2).reshape(NTAB, ngrp * gp),
             tab_im.reshape(2, ngrp, NTAB, gp)[k].transpose(1, 0, 2).reshape(NTAB, ngrp * gp)) for k in range(2)]
    b_t = lambda b: _blockdiag(b[0].reshape(nsb, gsb, gp, gc).transpose(0, 1, 3, 2)).astype(BF16)
    bre, bim = b_t(s5_b_re), b_t(s5_b_im)
    cp_t = lambda cp: _blockdiag(cp.reshape(2, nsb, gsb, gc, gp)).astype(BF16)
    ctr, cti = cp_t(cp_re), cp_t(cp_im)
    cmr, cmi = jnp.swapaxes(ctr, -1, -2), jnp.swapaxes(cti, -1, -2)
    dskip = s5_d

    lch = l // TC
    nproc = lch + 1
    order_f = lambda i: jnp.where(i == 0, lch, i - 1)
    order_b = lambda i: jnp.where(i == 0, lch, lch - i)
    rorder_f = lambda i: jnp.where(i == lch, lch, lch - 1 - i)
    rorder_b = lambda i: i
    up = _perm(u[:, hm:])
    yp_f, hr_f, hi_f, cinr_f, cini_f = _s5_fwd(up, bre, bim, cmr[0], cmi[0], tabs[0][0], tabs[0][1], dskip,
                                               order_f, nproc, False, True, "s5_fwd_f")
    yp_b, hr_b, hi_b, cinr_b, cini_b = _s5_fwd(up, bre, bim, cmr[1], cmi[1], tabs[1][0], tabs[1][1], dskip,
                                               order_b, nproc, True, False, "s5_fwd_b")
    y_f, y_b = _unperm(yp_f), _unperm(yp_b)

    x2, ypre, cat, mix = _mix_out_fwd(y_f, y_b, dpool, x1, modtab, w_pool_bd, pool_scale, w_glu_f, w_out_f)
    x3, h2, ab2, o2 = _ffn_fwd([x2], modtab, g2, w13_1, w2_1, 6, False, "ffn1_fwd")
    dx3, loss_part, dfinal_g = _loss_head(x3, loss_target[0], final_g.reshape(1, d))

    dhp2, d13_1, d2_1 = _ffn_bwd(dx3, modtab, h2, ab2, w13_1, w2_1, 6, False, "ffn1_bwd")
    dx2, dmod_678, dg2 = _ffn_bwd_norm(dx3, [dhp2], [x2], o2, modtab, g2, 6, False, "ffn1_bwd_norm")
    dypre, ddpool, dw_out, dw_glu, dw_pool_bd, dpscale, dgate5 = _mix_out_bwd(
        dx2, mix, cat, ypre, dpool, modtab, w_pool_bd, pool_scale, w_glu_f, w_out_f)

    dyp = _perm(jnp.concatenate([dypre, jnp.zeros((t - l, hm), F32)], axis=0))
    dup_f, dar_f, dai_f, dcr_f, dci_f, dbr_f, dbi_f, dd_skip, r13_1, r2_1 = _s5_bwd(
        dyp, up, hr_f, hi_f, cinr_f, cini_f, bre, bim, ctr[0], cti[0], tabs[0][0], tabs[0][1], dskip,
        rorder_f, nproc, False, True, "s5_bwd_f",
        xchg=[(d13_1, A2A_REL), (d2_1.reshape(NDEV, FPAD, d), A2A_REL)])
    dup_b, dar_b, dai_b, dcr_b, dci_b, dbr_b, dbi_b, _, r_out, r_glu = _s5_bwd(
        dyp, up, hr_b, hi_b, cinr_b, cini_b, bre, bim, ctr[1], cti[1], tabs[1][0], tabs[1][1], dskip,
        rorder_b, nproc, True, False, "s5_bwd_b",
        xchg=[(dw_out.reshape(NDEV, -1, d), A2A), (dw_glu.reshape(NDEV, hm // NDEV, hm), A2A)])
    du_f, du_b = _unperm(dup_f), _unperm(dup_b)

    dab_r = jnp.stack([dar_f, dar_b]).reshape(q, 1, gp)
    dab_i = jnp.stack([dai_f, dai_b]).reshape(q, 1, gp)
    dcp_r = _diagblocks(jnp.stack([dcr_f, dcr_b]), gsb).reshape(q, gc, gp)
    dcp_i = _diagblocks(jnp.stack([dci_f, dci_b]), gsb).reshape(q, gc, gp)
    ga_re, ga_im, gldt, gc_re, gc_im = _s5_param_bwd(a_re3, a_im3, ldt3, c_re3, c_im3, dab_r, dab_i, dcp_r, dcp_i)
    gb_re = (_diagblocks(dbr_f, gsb) + _diagblocks(dbr_b, gsb)).transpose(0, 1, 3, 2).reshape(ngrp, gp, gc)
    gb_im = (_diagblocks(dbi_f, gsb) + _diagblocks(dbi_b, gsb)).transpose(0, 1, 3, 2).reshape(ngrp, gp, gc)

    def pack(small):
        offs, pieces, off = {}, [], 0
        for k_, a_ in small.items():
            p_ = _pad128(a_.astype(F32))
            offs[k_] = (off, int(np.prod(a_.shape)))
            off += p_.shape[0]
            pieces.append(p_)
        return jnp.concatenate(pieces).reshape(1, off), offs

    gw_pool = _diagblocks(dw_pool_bd, 4)
    bundle_a, offs = pack(dict(pool_w=gw_pool, pool_scale=dpscale, s5_a_re=ga_re, s5_a_im=ga_im,
                               s5_log_dt=gldt[:, 0, 0], s5_b_re=gb_re, s5_b_im=gb_im, s5_c_re=gc_re, s5_c_im=gc_im,
                               s5_d=dd_skip, final_g=dfinal_g))

    du_pool = _from_grid(_pool(_to_grid(ddpool, rows), True, "pool_bwd"), rows)
    dx1, dw_in, dmod_34, dg1 = _mix_in_bwd(du_pool, du_f, du_b, dx2, h1, x1, modtab, g1, w_in_f, lc)
    dhp0a, d13_0a, d2_0a, bund_a, r_in = _ffn_bwd(
        dx1, modtab, h0, ab0, w13_0, w2_0, 0, True, "ffn0_bwd_a", pair0=0, npair=2,
        xchg=[(bundle_a, GATHER), (dw_in.reshape(NDEV, d // NDEV, -1), A2A)])
    dhp0b, d13_0b, d2_0b, r13_0a, r2_0a = _ffn_bwd(
        dx1, modtab, h0, ab0, w13_0, w2_0, 0, True, "ffn0_bwd_b", pair0=2, npair=2,
        xchg=[(d13_0a, A2A_SAME), (d2_0a.reshape(NDEV // 2, FPAD, d), A2A_SAME)])
    dx0, dmod_012, dg0, r13_0b, r2_0b = _ffn_bwd_norm(
        dx1, [dhp0a, dhp0b], x0, o0, modtab, g0, 0, True, "ffn0_bwd_norm",
        xchg=[(d13_0b, A2A_OTHER), (d2_0b.reshape(NDEV // 2, FPAD, d), A2A_OTHER)])
    r13_0 = jnp.concatenate([r13_0a, r13_0b], axis=0)
    r2_0 = jnp.concatenate([r2_0a, r2_0b], axis=0)
    grad_x = dx0.reshape(1, l, d)

    dmod_c = jnp.concatenate([dmod_012[0], dmod_34[0], jnp.zeros((4, d), F32)], axis=0)
    dmod_l = jnp.concatenate([dmod_012[1], dmod_34[1], dgate5, dmod_678[0]], axis=0)
    bundle_b, offs_b = pack(dict(norm_g=jnp.concatenate([dg0, dg1, dg2], axis=0), dmod_l=dmod_l, dmod_c=dmod_c,
                                 loss=loss_part[:, :1]))
    bund_b = _exchange([(bundle_b, GATHER)], "exchange_small")[0]
    bunds = {**{k_: (bund_a.reshape(NDEV, -1), v_) for k_, v_ in offs.items()},
             **{k_: (bund_b.reshape(NDEV, -1), v_) for k_, v_ in offs_b.items()}}
    r13, r2 = (r13_0, r13_1), (r2_0, r2_1)

    def piece(name):
        b_, (o_, n_) = bunds[name]
        return b_[:, o_:o_ + n_]

    loss = jnp.sum(piece("loss")[:, 0])

    dl_all = lax.dynamic_slice_in_dim(piece("dmod_l"), me * wa, wa, axis=1)
    dc_all = lax.dynamic_slice_in_dim(piece("dmod_c"), me * wa, wa, axis=1)
    g_wada, gc_part = _ada_bwd(c16, w_ada[0], dl_all, dc_all)
    gc_all = _exchange([(gc_part, GATHER)], "exchange_cctx")[0]

    parts = {
        "c_ctx": gc_all.reshape(NDEV, d),
        "norm_g": lax.dynamic_slice_in_dim(piece("norm_g").reshape(NDEV, 3, d), me * (d // NDEV), d // NDEV,
                                           axis=2).reshape((NDEV,) + norm_g.shape),
        "w_ada": g_wada.reshape((1,) + w_ada.shape),
        "b_ada": jnp.concatenate([piece("dmod_l"), piece("dmod_c")], axis=0).reshape((2 * NDEV,) + b_ada.shape),
        "ffn_w1": jnp.stack([r13[0][:, :, :fblk], r13[1][:, :, :fblk]], axis=1).reshape((NDEV,) + ffn_w1.shape),
        "ffn_w3": jnp.stack([r13[0][:, :, FPAD:FPAD + fblk], r13[1][:, :, FPAD:FPAD + fblk]],
                            axis=1).reshape((NDEV,) + ffn_w3.shape),
        "ffn_w2": jnp.stack([r2[0][:, :fblk, :], r2[1][:, :fblk, :]], axis=1).reshape((NDEV,) + ffn_w2.shape),
        "w_in": r_in.reshape((NDEV,) + w_in.shape),
        "s5_w_glu": r_glu.reshape((NDEV,) + s5_w_glu.shape),
        "w_out": r_out.reshape((NDEV,) + w_out.shape),
    }
    for k_ in ("pool_w", "pool_scale", "s5_a_re", "s5_a_im", "s5_log_dt", "s5_b_re", "s5_b_im", "s5_c_re", "s5_c_im",
               "s5_d", "final_g"):
        parts[k_] = piece(k_).reshape((NDEV,) + weights[k_].shape)

    grads, deltas, new_m, new_v = [], [], [], []
    for k_ in weights:
        g_, d_, m_, v_ = _adamw(parts[k_], weights[k_], mom_m[k_], mom_v[k_], "adamw_" + k_)
        grads.append(g_)
        deltas.append(d_)
        new_m.append(m_)
        new_v.append(v_)
    return (loss, grad_x, *grads, *deltas, *new_m, *new_v)
```

```python
import functools
import math

import numpy as np
import jax
import jax.numpy as jnp
from jax import lax
from jax.experimental import pallas as pl
from jax.experimental.pallas import tpu as pltpu

F32 = jnp.float32
BF16 = jnp.bfloat16
AXES = ("x", "y", "c")
NDEV = 8
EPS = 1e-6
TM = 512
TML = 512
TC = 256
SEG = TC // 8
NTAB = SEG + 16
S5_UNROLL = True
GRID_W = 64
POOL_WINDOWS = (2, 4, 8, 16)
LANE = 128
FPAD = 384
VMEM_LIMIT = 56 * 1024 * 1024
ADAM_BLOCK_BYTES = 512 * 1024
ADAM_LR, ADAM_B1, ADAM_B2, ADAM_EPS, ADAM_WD, ADAM_STEP = 0.001, 0.9, 0.999, 1e-08, 0.01, 10


def _raw_call(body, kw):
    return pl.pallas_call(body, **kw)


def _pcall(body, xchg=None, edges=None, **kw):
    extra = ()
    if xchg:
        arrs, kinds = [a for a, _ in xchg], [k for _, k in xchg]
        n = len(arrs)
        n_in, n_out, n_scr = len(kw["in_specs"]), len(kw["out_shape"]), len(kw.get("scratch_shapes", ()))
        inner = body

        def hosted(*refs):
            a, b = n_in, n_in + n
            c_, d_ = b + n_out, b + n_out + n
            e = d_ + n_scr
            first, mid, last = edges()

            @pl.when(first)
            def _():
                _xchg_start(refs[a:b], refs[c_:d_], refs[e:], kinds)

            inner(*refs[:a], *refs[b:c_], *refs[d_:e])

            if any(k == GATHER_REL for k in kinds):
                @pl.when(mid)
                def _():
                    _xchg_relay(refs[a:b], refs[c_:d_], refs[e:], kinds)

            @pl.when(last)
            def _():
                _xchg_finish(refs[a:b], refs[c_:d_], refs[e:], kinds)

        body = hosted
        any_spec = pl.BlockSpec(memory_space=pl.ANY)
        kw = dict(kw, in_specs=list(kw["in_specs"]) + [any_spec] * n,
                  out_shape=tuple(kw["out_shape"]) + _xchg_shapes(arrs, kinds),
                  out_specs=tuple(kw["out_specs"]) + (any_spec,) * n,
                  scratch_shapes=list(kw.get("scratch_shapes", ())) + _xchg_sems(n))
        extra = tuple(arrs)
    call = _raw_call(body, kw)
    return (lambda *args: call(*args, *extra)) if extra else call


def _params(sem):
    return pltpu.CompilerParams(dimension_semantics=sem, vmem_limit_bytes=VMEM_LIMIT)


def _dot(a, b):
    return jnp.dot(a, b, preferred_element_type=F32)


def _dot_nt(a, b):
    return lax.dot_general(a, b, (((1,), (1,)), ((), ())), preferred_element_type=F32)


def _dot_tn(a, b):
    return lax.dot_general(a, b, (((0,), (0,)), ((), ())), preferred_element_type=F32)


def _sigmoid(x):
    return 1.0 / (1.0 + jnp.exp(-x))


def _colsum(a):
    return jnp.sum(a, axis=0, keepdims=True)


def _resident(shape):
    nd = len(shape)
    return pl.BlockSpec(shape, lambda *_: (0,) * nd, pipeline_mode=pl.Buffered(1))


def _const(shape):
    nd = len(shape)
    return pl.BlockSpec(shape, lambda *_: (0,) * nd)


GATHER = "gather"
GATHER_REL = "gather_rel"
A2A = "a2a"
A2A_REL = "a2a_rel"
A2A_SAME = "a2a_same"
A2A_OTHER = "a2a_other"


def _exchange(items, name):
    arrs, kinds = [a for a, _ in items], [k for _, k in items]
    n = len(arrs)

    def body(*refs):
        _xchg_start(refs[:n], refs[n:2 * n], refs[2 * n:], kinds)
        _xchg_relay(refs[:n], refs[n:2 * n], refs[2 * n:], kinds)
        _xchg_finish(refs[:n], refs[n:2 * n], refs[2 * n:], kinds)

    any_spec = pl.BlockSpec(memory_space=pl.ANY)
    outs = _pcall(
        body, name=name, out_shape=_xchg_shapes(arrs, kinds), in_specs=[any_spec] * n, out_specs=[any_spec] * n,
        scratch_shapes=_xchg_sems(n),
    )(*arrs)
    return list(outs)


def _edges1(n0):
    return lambda: (pl.program_id(0) == 0, pl.program_id(0) == (3 * n0) // 4, pl.program_id(0) == n0 - 1)


def _edges2(n0, n1):
    def edges():
        step = pl.program_id(0) * n1 + pl.program_id(1)
        return step == 0, step == (3 * n0 * n1) // 4, step == n0 * n1 - 1
    return edges


def _xchg_shapes(arrs, kinds):
    return tuple(jax.ShapeDtypeStruct(((NDEV,) if k in (GATHER, GATHER_REL) else ()) + tuple(a.shape), a.dtype)
                 for a, k in zip(arrs, kinds))


def _xchg_sems(n):
    return [pltpu.SemaphoreType.DMA((n * (NDEV - 1),)), pltpu.SemaphoreType.DMA((n * (NDEV - 1),)),
            pltpu.SemaphoreType.DMA((n,))]


def _xchg_plan(in_refs, out_refs, sems, kinds):
    send_sems, recv_sems, loc_sems = sems
    x, y, c = (lax.axis_index(a) for a in AXES)
    my_abs, my_chip = 4 * x + 2 * y + c, 2 * x + y

    def peer(p):
        px = 1 - x if p & 4 else x
        py = 1 - y if p & 2 else y
        pc = 1 - c if p & 1 else c
        return (px, py, pc), 4 * px + 2 * py + pc, 2 * px + py

    starts, relays, recvs = [], [], []
    for k, kind in enumerate(kinds):
        src, dst = in_refs[k], out_refs[k]

        def remote(src_ref, row, p, pair, dst=dst, k=k):
            dev, sem = peer(p)[0], k * (NDEV - 1) + pair - 1
            return lambda: pltpu.make_async_remote_copy(
                src_ref=src_ref, dst_ref=dst.at[row], send_sem=send_sems.at[sem], recv_sem=recv_sems.at[sem],
                device_id=dev, device_id_type=pl.DeviceIdType.MESH)

        def local(src_ref, row, dst=dst, k=k):
            return lambda: pltpu.make_async_copy(src_ref, dst.at[row], loc_sems.at[k])

        if kind == GATHER:
            for p in range(1, NDEV):
                starts.append((remote(src, my_abs, p, p), False))
                recvs.append(remote(src, peer(p)[1], p, p))
            starts.append((local(src, my_abs), True))
        elif kind == GATHER_REL:
            for p in (1, 4, 2, 6):
                starts.append((remote(src, 4 * (p & 1) + my_chip, p, p), False))
            for q in (4, 2, 6):
                pchip = peer(q)[2]
                relays.append((remote(src, pchip, q, q), remote(dst.at[pchip], 4 + pchip, 1, q | 1)))
                recvs.append(remote(src, 4 + pchip, 1, q | 1))
            recvs.append(remote(src, 4 + my_chip, 1, 1))
            starts.append((local(src, my_chip), True))
        elif kind in (A2A, A2A_REL):
            for p in range(1, NDEV):
                _, pabs, pchip = peer(p)
                theirs, mine = (pabs, my_abs) if kind == A2A else (4 * (p & 1) + pchip, 4 * (p & 1) + my_chip)
                starts.append((remote(src.at[theirs], mine, p, p), False))
                recvs.append(remote(src.at[theirs], theirs, p, p))
            own = my_abs if kind == A2A else my_chip
            starts.append((local(src.at[own], own), True))
        else:
            for p in ((4, 2, 6) if kind == A2A_SAME else (1, 5, 3, 7)):
                pchip = peer(p)[2]
                starts.append((remote(src.at[pchip], my_chip, p, p), False))
                recvs.append(remote(src.at[pchip], pchip, p, p))
            if kind == A2A_SAME:
                starts.append((local(src.at[my_chip], my_chip), True))
    return starts, relays, recvs


def _xchg_start(in_refs, out_refs, sems, kinds):
    for make, _ in _xchg_plan(in_refs, out_refs, sems, kinds)[0]:
        make().start()


def _xchg_relay(in_refs, out_refs, sems, kinds):
    for arrival, forward in _xchg_plan(in_refs, out_refs, sems, kinds)[1]:
        arrival().wait_recv()
        forward().start()


def _xchg_finish(in_refs, out_refs, sems, kinds):
    starts, relays, recvs = _xchg_plan(in_refs, out_refs, sems, kinds)
    for make in recvs:
        make().wait_recv()
    for make, is_local in starts:
        if is_local:
            make().wait()
        else:
            make().wait_send()
    for _, forward in relays:
        forward().wait_send()


def _ada_fwd(c16, w_blk, b_blk):
    def body(c_ref, w_ref, b_ref, o_ref):
        cc = c_ref[...]
        s = (cc * _sigmoid(cc)).astype(BF16)
        o_ref[...] = _dot(s, w_ref[...].astype(BF16)) + b_ref[...]

    return _pcall(body, name="ada_fwd", out_shape=jax.ShapeDtypeStruct((16, w_blk.shape[1]), F32),
                  compiler_params=_params(None))(c16, w_blk, b_blk)


def _ada_bwd(c16, w_blk, dl, dc):
    d, w = w_blk.shape

    def body(c_ref, w_ref, dl_ref, dc_ref, gw_ref, gc_ref):
        cc = c_ref[...]
        sg = _sigmoid(cc)
        s = (cc * sg).astype(BF16)
        dctot = _colsum(dc_ref[...])
        dm = jnp.concatenate([dl_ref[...], jnp.broadcast_to(dctot, (8, w))], axis=0)
        rows = lax.broadcasted_iota(jnp.int32, (16, w), 0)
        dm = jnp.where(rows <= 8, dm, 0.0).astype(BF16)
        gw_ref[...] = _dot_tn(s, dm)
        t = _dot_nt(jnp.broadcast_to(dctot, (8, w)).astype(BF16), w_ref[...].astype(BF16))[0:1]
        c8, s8 = cc[8:9], sg[8:9]
        gc_ref[...] = t * (s8 * (1.0 + c8 * (1.0 - s8)))

    return _pcall(body, name="ada_bwd",
                  out_shape=(jax.ShapeDtypeStruct((d, w), F32), jax.ShapeDtypeStruct((1, d), F32)),
                  compiler_params=_params(None))(c16, w_blk, dl, dc)


def _norm_fwd(x, g, shift, scale):
    rstd = lax.rsqrt(jnp.mean(x * x, axis=-1, keepdims=True) + EPS)
    xhat = x * rstd
    xn = xhat * g
    return rstd, xhat, xn, xn * (1.0 + scale) + shift


def _norm_bwd(dh, rstd, xhat, xn, g, scale):
    dxn = dh * (1.0 + scale)
    dxhat = dxn * g
    dx = rstd * (dxhat - xhat * jnp.mean(dxhat * xhat, axis=-1, keepdims=True))
    return dx, _colsum(dh), _colsum(dh * xn), _colsum(dxn * xhat)


def _mod_spec(d, has_ctx):
    if has_ctx:
        return pl.BlockSpec((1, 9, d), lambda *ids: (jnp.minimum(ids[-1], 1), 0, 0))
    return pl.BlockSpec((1, 9, d), lambda *ids: (1, 0, 0))


def _row_specs(nx, tm, d):
    if nx == 1:
        return [pl.BlockSpec((tm, d), lambda i: (i, 0))]
    return [pl.BlockSpec((tm, d), lambda i: (0, 0)), pl.BlockSpec((tm, d), lambda i: (jnp.maximum(i - 1, 0), 0))]


def _rows(refs):
    if len(refs) == 1:
        return refs[0][...]
    return jnp.where(pl.program_id(0) == 0, refs[0][...], refs[1][...])


def _tile(has_ctx):
    return TM if has_ctx else TML


def _tile_of(nt, has_ctx):
    if has_ctx:
        return lambda i: jnp.where(i == 0, nt - 1, i - 1)
    return lambda i: i


def _dmod_spec(nrow, d, has_ctx):
    if has_ctx:
        return pl.BlockSpec((1, nrow, d), lambda i: (jnp.minimum(i, 1), 0, 0))
    return pl.BlockSpec((1, nrow, d), lambda i: (0, 0, 0))


def _ffn_fwd(x, modtab, g, w13, w2, k0, has_ctx, name, xchg=None):
    t, d = sum(a.shape[0] for a in x), x[0].shape[1]
    nb, _, w = w13.shape
    fp = w // 2
    tm = _tile(has_ctx)
    nx = len(x)

    def body(*refs):
        mod_ref, g_ref, w13_ref, w2_ref, xo_ref, h_ref, ab_ref, o_ref = refs[nx:]
        xx = _rows(refs[:nx])
        shift, scale, gate = mod_ref[0, k0:k0 + 1, :], mod_ref[0, k0 + 1:k0 + 2, :], mod_ref[0, k0 + 2:k0 + 3, :]
        _, _, _, h = _norm_fwd(xx, g_ref[...], shift, scale)
        hb = h.astype(BF16)
        h_ref[...] = hb
        acc = jnp.zeros((tm, d), F32)
        for p in range(nb // 2):
            zs = []
            for q in range(2):
                blk = 2 * p + q
                ab = _dot(hb, w13_ref[blk])
                ab_ref[:, blk * w:(blk + 1) * w] = ab.astype(BF16)
                a, b = ab[:, :fp], ab[:, fp:]
                zs.append((a * _sigmoid(a) * b).astype(BF16))
            acc = acc + _dot(jnp.concatenate(zs, axis=1), w2_ref[p])
        o_ref[...] = acc.astype(BF16)
        xo_ref[...] = xx + (0.5 * gate) * acc

    tile = _tile_of(t // tm, has_ctx)
    row = lambda i: (tile(i), 0)
    return _pcall(
        body, name=name, grid=(t // tm,), xchg=xchg, edges=_edges1(t // tm),
        out_shape=(jax.ShapeDtypeStruct((t, d), F32), jax.ShapeDtypeStruct((t, d), BF16),
                   jax.ShapeDtypeStruct((t, nb * w), BF16), jax.ShapeDtypeStruct((t, d), BF16)),
        in_specs=_row_specs(nx, tm, d) + [_mod_spec(d, has_ctx), _const((1, d)),
                                          _resident(w13.shape), _resident(w2.shape)],
        out_specs=(pl.BlockSpec((tm, d), row), pl.BlockSpec((tm, d), row), pl.BlockSpec((tm, nb * w), row),
                   pl.BlockSpec((tm, d), row)),
        compiler_params=_params(("arbitrary",)),
    )(*x, modtab, g, w13, w2)


def _ffn_bwd(dy, modtab, h, ab, w13, w2, k0, has_ctx, name, pair0=0, npair=None, xchg=None):
    t, d = dy.shape
    _, _, w = w13.shape
    fp = w // 2
    npair = w13.shape[0] // 2 if npair is None else npair
    nb = 2 * npair
    tm = _tile(has_ctx)
    nt = t // tm

    def body(dy_ref, mod_ref, h_ref, ab_ref, w13_ref, w2_ref, dh_ref, d13_ref, d2_ref, acc13, acc2):
        i = pl.program_id(1)

        @pl.when(i == 0)
        def _():
            acc13[...] = jnp.zeros_like(acc13)
            acc2[...] = jnp.zeros_like(acc2)

        gate = mod_ref[0, k0 + 2:k0 + 3, :]
        do = (dy_ref[...] * (0.5 * gate)).astype(BF16)
        dz = _dot_nt(do, w2_ref[0])
        hb = h_ref[...]
        dh = jnp.zeros((tm, d), F32)
        zs = []
        for q in range(2):
            ab = ab_ref[:, q * w:(q + 1) * w].astype(F32)
            a, b = ab[:, :fp], ab[:, fp:]
            sg = _sigmoid(a)
            sa = a * sg
            dzq = dz[:, q * fp:(q + 1) * fp]
            da = dzq * b * (sg * (1.0 + a * (1.0 - sg)))
            db = dzq * sa
            dab = jnp.concatenate([da, db], axis=1).astype(BF16)
            dh = dh + _dot_nt(dab, w13_ref[q])
            acc13[q] += _dot_tn(hb, dab)
            zs.append((sa * b).astype(BF16))
        acc2[...] += _dot_tn(jnp.concatenate(zs, axis=1), do)
        dh_ref[0] = dh.astype(BF16)

        @pl.when(i == nt - 1)
        def _():
            d13_ref[...] = acc13[...].astype(BF16)
            d2_ref[0] = acc2[...].astype(BF16)

    mod_spec = _mod_spec(d, has_ctx)
    tile = _tile_of(nt, has_ctx)
    return _pcall(
        body, name=name, grid=(npair, nt), xchg=xchg, edges=_edges2(npair, nt),
        out_shape=(jax.ShapeDtypeStruct((npair, t, d), BF16), jax.ShapeDtypeStruct((nb, d, w), BF16),
                   jax.ShapeDtypeStruct((npair, 2 * fp, d), BF16)),
        in_specs=[pl.BlockSpec((tm, d), lambda p, i: (tile(i), 0)), mod_spec,
                  pl.BlockSpec((tm, d), lambda p, i: (tile(i), 0)),
                  pl.BlockSpec((tm, 2 * w), lambda p, i: (tile(i), pair0 + p)),
                  pl.BlockSpec((2, d, w), lambda p, i: (pair0 + p, 0, 0)),
                  pl.BlockSpec((1, 2 * fp, d), lambda p, i: (pair0 + p, 0, 0))],
        out_specs=(pl.BlockSpec((1, tm, d), lambda p, i: (p, tile(i), 0)),
                   pl.BlockSpec((2, d, w), lambda p, i: (p, 0, 0)),
                   pl.BlockSpec((1, 2 * fp, d), lambda p, i: (p, 0, 0))),
        scratch_shapes=[pltpu.VMEM((2, d, w), F32), pltpu.VMEM((2 * fp, d), F32)],
        compiler_params=_params(("arbitrary", "arbitrary")),
    )(dy, modtab, h, ab, w13, w2)


def _ffn_bwd_norm(dy, dhps, x, o, modtab, g, k0, has_ctx, name, xchg=None):
    t, d = dy.shape
    ngrp = 2 if has_ctx else 1
    tm = _tile(has_ctx)
    ndh, nx = len(dhps), len(x)
    lat0 = ngrp - 1

    def body(dy_ref, *rest):
        dhp_refs, x_refs = rest[:ndh], rest[ndh:ndh + nx]
        o_ref, mod_ref, g_ref, dx_ref, dmod_ref, dg_ref = rest[ndh + nx:]
        i = pl.program_id(0)

        @pl.when(i == 0)
        def _():
            dg_ref[...] = jnp.zeros_like(dg_ref)

        @pl.when((i == 0) | (i == ngrp - 1))
        def _():
            dmod_ref[...] = jnp.zeros_like(dmod_ref)

        dh = None
        for ref in dhp_refs:
            for p in range(ref.shape[0]):
                dh = ref[p].astype(F32) if dh is None else dh + ref[p].astype(F32)
        scale = mod_ref[0, k0 + 1:k0 + 2, :]
        gg = g_ref[...]
        rstd, xhat, xn, _ = _norm_fwd(_rows(x_refs), gg, 0.0, scale)
        dxn, dshift, dscale, dg = _norm_bwd(dh, rstd, xhat, xn, gg, scale)
        dyv = dy_ref[...]
        dx_ref[...] = dyv + dxn
        dmod_ref[0, 0:1, :] += dshift
        dmod_ref[0, 1:2, :] += dscale
        dmod_ref[0, 2:3, :] += _colsum(0.5 * dyv * o_ref[...].astype(F32))
        dg_ref[...] += dg

    tile = _tile_of(t // tm, has_ctx)
    row = lambda i: (tile(i), 0)
    return _pcall(
        body, name=name, grid=(t // tm,), xchg=xchg, edges=_edges1(t // tm),
        out_shape=(jax.ShapeDtypeStruct((t - lat0 * tm, d), F32), jax.ShapeDtypeStruct((ngrp, 3, d), F32),
                   jax.ShapeDtypeStruct((1, d), F32)),
        in_specs=[pl.BlockSpec((tm, d), row)]
        + [pl.BlockSpec((a.shape[0], tm, d), lambda i: (0, tile(i), 0)) for a in dhps]
        + _row_specs(nx, tm, d) + [pl.BlockSpec((tm, d), row), _mod_spec(d, has_ctx), _const((1, d))],
        out_specs=(pl.BlockSpec((tm, d), lambda i: (jnp.maximum(i - lat0, 0), 0)), _dmod_spec(3, d, has_ctx),
                   _const((1, d))),
        compiler_params=_params(("arbitrary",)),
    )(dy, *dhps, *x, o, modtab, g)


def _mix_in_fwd(x1, modtab, g, w_in):
    t, d = x1.shape
    hm = w_in.shape[1] // 2

    def body(x_ref, mod_ref, g_ref, w_ref, h_ref, up_ref, us_ref):
        _, _, _, h = _norm_fwd(x_ref[...], g_ref[...], mod_ref[0, 3:4, :], mod_ref[0, 4:5, :])
        hb = h.astype(BF16)
        h_ref[...] = hb
        u = _dot(hb, w_ref[...])
        up_ref[...] = u[:, :hm]
        us_ref[...] = u[:, hm:]

    tile = _tile_of(t // TM, True)
    row = lambda i: (tile(i), 0)
    return _pcall(
        body, name="mix_in_fwd", grid=(t // TM,),
        out_shape=(jax.ShapeDtypeStruct((t, d), BF16), jax.ShapeDtypeStruct((t, hm), F32),
                   jax.ShapeDtypeStruct((t, hm), F32)),
        in_specs=[pl.BlockSpec((TM, d), row), _mod_spec(d, True), _const((1, d)), _resident(w_in.shape)],
        out_specs=(pl.BlockSpec((TM, d), row), pl.BlockSpec((TM, hm), row), pl.BlockSpec((TM, hm), row)),
        compiler_params=_params(("arbitrary",)),
    )(x1, modtab, g, w_in)


def _mix_in_bwd(du_pool, du_f, du_b, dx2, h1, x1, modtab, g, w_in, lc):
    t, d = x1.shape
    m = w_in.shape[1]
    hm = m // 2
    nt = t // TM

    def body(dup_ref, duf_ref, dub_ref, dx2_ref, h_ref, x_ref, mod_ref, g_ref, w_ref, dx_ref, dw_ref, dmod_ref, dg_ref, acc):
        i = pl.program_id(0)

        @pl.when(i == 0)
        def _():
            acc[...] = jnp.zeros_like(acc)
            dg_ref[...] = jnp.zeros_like(dg_ref)

        @pl.when(i <= 1)
        def _():
            dmod_ref[...] = jnp.zeros_like(dmod_ref)

        lat = (i > 0).astype(F32)
        valid = (i > 0) | (lax.broadcasted_iota(jnp.int32, (TM, 1), 0) < lc)
        du_s5 = jnp.where(valid, duf_ref[...] + dub_ref[...], 0.0)
        du = jnp.concatenate([dup_ref[...] * lat, du_s5], axis=1).astype(BF16)
        dh = _dot_nt(du, w_ref[...])
        acc[...] += _dot_tn(h_ref[...], du)
        scale = mod_ref[0, 4:5, :]
        gg = g_ref[...]
        rstd, xhat, xn, _ = _norm_fwd(x_ref[...], gg, 0.0, scale)
        dxn, dshift, dscale, dg = _norm_bwd(dh, rstd, xhat, xn, gg, scale)
        dx_ref[...] = dx2_ref[...] * lat + dxn
        dmod_ref[0, 0:1, :] += dshift
        dmod_ref[0, 1:2, :] += dscale
        dg_ref[...] += dg

        @pl.when(i == nt - 1)
        def _():
            dw_ref[...] = acc[...].astype(BF16)

    tile = _tile_of(nt, True)
    row = lambda i: (tile(i), 0)
    lrow = lambda i: (jnp.maximum(i - 1, 0), 0)
    return _pcall(
        body, name="mix_in_bwd", grid=(nt,),
        out_shape=(jax.ShapeDtypeStruct((t, d), F32), jax.ShapeDtypeStruct((d, m), BF16),
                   jax.ShapeDtypeStruct((2, 2, d), F32), jax.ShapeDtypeStruct((1, d), F32)),
        in_specs=[pl.BlockSpec((TM, hm), lrow), pl.BlockSpec((TM, hm), row), pl.BlockSpec((TM, hm), row),
                  pl.BlockSpec((TM, d), lrow), pl.BlockSpec((TM, d), row), pl.BlockSpec((TM, d), row),
                  _mod_spec(d, True), _const((1, d)), _resident(w_in.shape)],
        out_specs=(pl.BlockSpec((TM, d), row), _const((d, m)), _dmod_spec(2, d, True), _const((1, d))),
        scratch_shapes=[pltpu.VMEM((d, m), F32)],
        compiler_params=_params(("arbitrary",)),
    )(du_pool, du_f, du_b, dx2, h1, x1, modtab, g, w_in)


def _pool(v, transpose, name):
    ngrp, rows, n = v.shape
    nchunk = 4
    cw = n // nchunk

    def rowsum(val, lo, hi):
        ri = lax.broadcasted_iota(jnp.int32, (rows, rows), 0)
        ci = lax.broadcasted_iota(jnp.int32, (rows, rows), 1)
        sel = (((ci - ri) >= -lo) & ((ci - ri) <= hi)).astype(BF16)
        v1 = val.astype(BF16)
        r1 = val - v1.astype(F32)
        v2 = r1.astype(BF16)
        v3 = (r1 - v2.astype(F32)).astype(BF16)
        return _dot(sel, v1) + _dot(sel, v2) + _dot(sel, v3)

    def one(v_ref, o_ref, r_scr, a_scr, win):
        lo = win // 2
        hi = win - 1 - lo
        rlo, rhi = (hi, lo) if transpose else (lo, hi)
        ridx = lax.broadcasted_iota(jnp.int32, (rows, 1), 0)
        cnt_r = (jnp.minimum(ridx + hi + 1, rows) - jnp.maximum(ridx - lo, 0)).astype(F32)
        cidx = lax.broadcasted_iota(jnp.int32, (1, n), 1) // LANE
        cnt_c = (jnp.minimum(cidx + hi + 1, GRID_W) - jnp.maximum(cidx - lo, 0)).astype(F32)
        if not transpose:
            for k in range(nchunk):
                sl = slice(k * cw, (k + 1) * cw)
                r_scr[:, sl] = rowsum(v_ref[0, :, sl], rlo, rhi) / cnt_r
        else:
            r_scr[...] = v_ref[0] / cnt_c
        a_scr[...] = r_scr[...]
        for j in range(-rlo, rhi + 1):
            if j == 0:
                continue
            c0, c1 = max(0, -j), min(GRID_W, GRID_W - j)
            a_scr[:, c0 * LANE:c1 * LANE] += r_scr[:, (c0 + j) * LANE:(c1 + j) * LANE]
        if not transpose:
            o_ref[0] = a_scr[...] / cnt_c - v_ref[0]
        else:
            for k in range(nchunk):
                sl = slice(k * cw, (k + 1) * cw)
                o_ref[0, :, sl] = rowsum(a_scr[:, sl] / cnt_r, rlo, rhi) - v_ref[0, :, sl]

    def body(v_ref, o_ref, r_scr, a_scr):
        grp = pl.program_id(0)
        for k, win in enumerate(POOL_WINDOWS):
            @pl.when(grp == k)
            def _(win=win):
                one(v_ref, o_ref, r_scr, a_scr, win)

    spec = pl.BlockSpec((1, rows, n), lambda k: (k, 0, 0))
    return _pcall(
        body, name=name, grid=(ngrp,), out_shape=jax.ShapeDtypeStruct(v.shape, F32),
        in_specs=[spec], out_specs=spec,
        scratch_shapes=[pltpu.VMEM((rows, n), F32), pltpu.VMEM((rows, n), F32)],
        compiler_params=_params(("arbitrary",)),
    )(v)


def _cmul(ar, ai, br, bi):
    return ar * br - ai * bi, ar * bi + ai * br


def _s5_prep(a_re, a_im, log_dt, c_re, c_im, kvec):
    q, nc, p = c_re.shape

    def body(ar_ref, ai_ref, ldt_ref, cr_ref, ci_ref, k_ref, cpr_ref, cpi_ref, tr_ref, ti_ref):
        lr, li = ar_ref[...], ai_ref[...]
        dt = jnp.exp(ldt_ref[...])
        kk = k_ref[...]
        mag = jnp.exp(kk * (lr * dt))
        ph = kk * (li * dt)
        tr_ref[...] = mag * jnp.cos(ph)
        ti_ref[...] = mag * jnp.sin(ph)
        m1 = jnp.exp(lr * dt)
        abr, abi = m1 * jnp.cos(li * dt), m1 * jnp.sin(li * dt)
        den = lr * lr + li * li
        xr, xi = abr - 1.0, abi
        cfr, cfi = (xr * lr + xi * li) / den, (xi * lr - xr * li) / den
        pr, pi_ = _cmul(cr_ref[...], ci_ref[...], cfr, cfi)
        cpr_ref[...] = pr
        cpi_ref[...] = pi_

    return _pcall(
        body, name="s5_prep",
        out_shape=(jax.ShapeDtypeStruct((q, nc, p), F32), jax.ShapeDtypeStruct((q, nc, p), F32),
                   jax.ShapeDtypeStruct((q, NTAB, p), F32), jax.ShapeDtypeStruct((q, NTAB, p), F32)),
        compiler_params=_params(None),
    )(a_re, a_im, log_dt, c_re, c_im, kvec)


def _s5_param_bwd(a_re, a_im, log_dt, c_re, c_im, dab_r, dab_i, dcp_r, dcp_i):
    q, nc, p = c_re.shape

    def body(ar_ref, ai_ref, ldt_ref, cr_ref, ci_ref, dar_ref, dai_ref, dcr_ref, dci_ref,
             gar_ref, gai_ref, gdt_ref, gcr_ref, gci_ref):
        lr, li = ar_ref[...], ai_ref[...]
        dt = jnp.exp(ldt_ref[...])
        m1 = jnp.exp(lr * dt)
        abr, abi = m1 * jnp.cos(li * dt), m1 * jnp.sin(li * dt)
        den = lr * lr + li * li
        xr, xi = abr - 1.0, abi
        cfr, cfi = (xr * lr + xi * li) / den, (xi * lr - xr * li) / den
        cr, ci = cr_ref[...], ci_ref[...]
        dcr, dci = dcr_ref[...], dci_ref[...]
        gcr, gci = _cmul(dcr, dci, cfr, -cfi)
        gcr_ref[...] = gcr
        gci_ref[...] = gci
        t_r, t_i = _cmul(cr, -ci, dcr, dci)
        dcf_r = jnp.sum(t_r, axis=1, keepdims=True)
        dcf_i = jnp.sum(t_i, axis=1, keepdims=True)
        ilr, ili = lr / den, -li / den
        u_r, u_i = _cmul(dcf_r, dcf_i, ilr, -ili)
        dab_r_, dab_i_ = dar_ref[...] + u_r, dai_ref[...] + u_i
        v_r, v_i = _cmul(dab_r_, dab_i_, abr, -abi)
        cl_r, cl_i = _cmul(cfr, cfi, ilr, ili)
        w_r, w_i = _cmul(dcf_r, dcf_i, cl_r, -cl_i)
        gar_ref[...] = v_r * dt - w_r
        gai_ref[...] = v_i * dt - w_i
        la_r, la_i = _cmul(lr, li, abr, abi)
        ddt = la_r * dab_r_ + la_i * dab_i_
        gdt_ref[...] = dt * jnp.sum(ddt, axis=2, keepdims=True)

    return _pcall(
        body, name="s5_param_bwd",
        out_shape=(jax.ShapeDtypeStruct((q, 1, p), F32), jax.ShapeDtypeStruct((q, 1, p), F32),
                   jax.ShapeDtypeStruct((q, 1, p), F32), jax.ShapeDtypeStruct((q, nc, p), F32),
                   jax.ShapeDtypeStruct((q, nc, p), F32)),
        compiler_params=_params(None),
    )(a_re, a_im, log_dt, c_re, c_im, dab_r, dab_i, dcp_r, dcp_i)


def _bcast8(row, c):
    return jnp.broadcast_to(row, (8, c))


def _slab(j):
    return j * 8 if isinstance(j, int) else pl.multiple_of(j * 8, 8)


def _scan_head(xr, xi, tre_ref, tim_ref, car_r, car_i, desc, conj, cs):
    sg = -1.0 if conj else 1.0
    ar = _bcast8(tre_ref[0:1, :], cs)
    ai = sg * _bcast8(tim_ref[0:1, :], cs)

    def p1(jj, carry):
        hr, hi = carry
        off = _slab(SEG - 1 - jj if desc else jj)
        pr, pi_ = _cmul(ar, ai, hr, hi)
        nr = pr + xr[pl.ds(off, 8), :]
        ni = pi_ + xi[pl.ds(off, 8), :]
        xr[pl.ds(off, 8), :] = nr
        xi[pl.ds(off, 8), :] = ni
        return nr, ni

    zero = jnp.zeros((8, cs), F32)
    ir, ii = lax.fori_loop(0, SEG, p1, (zero, zero), unroll=S5_UNROLL)
    cin_r, cin_i = car_r[...], car_i[...]
    rows = lax.broadcasted_iota(jnp.int32, (8, cs), 0)
    for s, krow in ((1, SEG), (2, SEG + 1), (4, SEG + 3)):
        keep, sh = (rows < 8 - s, 8 - s) if desc else (rows >= s, s)
        sr = jnp.where(keep, pltpu.roll(ir, sh, 0), 0.0)
        si = jnp.where(keep, pltpu.roll(ii, sh, 0), 0.0)
        pr, pi_ = _cmul(tre_ref[krow:krow + 1, :], sg * tim_ref[krow:krow + 1, :], sr, si)
        ir, ii = ir + pr, ii + pi_
    q0 = SEG + 8 if desc else SEG
    pr, pi_ = _cmul(tre_ref[q0:q0 + 8, :], sg * tim_ref[q0:q0 + 8, :], cin_r, cin_i)
    fr, fi = ir + pr, ii + pi_
    keep, sh, edge = (rows < 7, 7, 0) if desc else (rows >= 1, 1, 7)
    cs_r = jnp.where(keep, pltpu.roll(fr, sh, 0), cin_r)
    cs_i = jnp.where(keep, pltpu.roll(fi, sh, 0), cin_i)
    car_r[...] = _bcast8(fr[edge:edge + 1, :], cs)
    car_i[...] = _bcast8(fi[edge:edge + 1, :], cs)
    return cs_r, cs_i, cin_r, cin_i


def _pow_row(tre_ref, tim_ref, j, desc, conj, cs):
    k = SEG - 1 - j if desc else j
    sg = -1.0 if conj else 1.0
    return _bcast8(tre_ref[pl.ds(k, 1), :], cs), sg * _bcast8(tim_ref[pl.ds(k, 1), :], cs)


def _to_segments(val, dst):
    for r in range(8):
        dst[pl.ds(r, SEG, stride=8), :] = val[r * SEG:(r + 1) * SEG, :]


def _from_segments(src):
    return jnp.concatenate([src[pl.ds(r, SEG, stride=8), :] for r in range(8)], axis=0)


def _s5_fwd(up, bre, bim, cmr, cmi, tre, tim, dskip, order, nproc, desc, with_skip, name):
    t, cu = up.shape
    nsb = bre.shape[0]
    cb = cu // nsb
    cs = bre.shape[2]
    nch = t // TC

    def body(u_ref, bre_ref, bim_ref, cmr_ref, cmi_ref, tre_ref, tim_ref, dsk_ref,
             y_ref, hr_ref, hi_ref, cinr_ref, cini_ref, car_r, car_i, seg_scr):
        i = pl.program_id(0)

        @pl.when(i == 0)
        def _():
            car_r[...] = jnp.zeros_like(car_r)
            car_i[...] = jnp.zeros_like(car_i)

        for sb in range(nsb):
            col, ch = pl.ds(sb * cs, cs), slice(sb * cb, (sb + 1) * cb)
            hr_v, hi_v = hr_ref.at[:, col], hi_ref.at[:, col]
            tre_v, tim_v = tre_ref.at[:, col], tim_ref.at[:, col]
            useg, yseg = seg_scr.at[0, sb], seg_scr.at[1, sb]
            _to_segments(u_ref[:, ch], useg)
            u = useg[...]
            ub = u.astype(BF16)
            hr_v[...] = _dot(ub, bre_ref[sb])
            hi_v[...] = _dot(ub, bim_ref[sb])
            cs_r, cs_i, cin_r, cin_i = _scan_head(hr_v, hi_v, tre_v, tim_v, car_r.at[sb], car_i.at[sb], desc, False, cs)
            cinr_ref[:, col] = cin_r
            cini_ref[:, col] = cin_i

            def p2(j, _, hr_v=hr_v, hi_v=hi_v, tre_v=tre_v, tim_v=tim_v, cs_r=cs_r, cs_i=cs_i):
                off = _slab(j)
                pw_r, pw_i = _pow_row(tre_v, tim_v, j, desc, False, cs)
                pr, pi_ = _cmul(pw_r, pw_i, cs_r, cs_i)
                hr_v[pl.ds(off, 8), :] = hr_v[pl.ds(off, 8), :] + pr
                hi_v[pl.ds(off, 8), :] = hi_v[pl.ds(off, 8), :] + pi_
                return 0

            lax.fori_loop(0, SEG, p2, 0, unroll=S5_UNROLL)
            y = _dot(hr_v[...].astype(BF16), cmr_ref[sb]) - _dot(hi_v[...].astype(BF16), cmi_ref[sb])
            if with_skip:
                y = y + dsk_ref[:, ch] * u
            yseg[...] = y
            y_ref[:, ch] = _from_segments(yseg)

    blk = lambda i: (order(i), 0)
    return _pcall(
        body, name=name, grid=(nproc,),
        out_shape=(jax.ShapeDtypeStruct((t, cu), F32), jax.ShapeDtypeStruct((t, nsb * cs), F32),
                   jax.ShapeDtypeStruct((t, nsb * cs), F32), jax.ShapeDtypeStruct((nch * 8, nsb * cs), F32),
                   jax.ShapeDtypeStruct((nch * 8, nsb * cs), F32)),
        in_specs=[pl.BlockSpec((TC, cu), blk), _const(bre.shape), _const(bim.shape), _const(cmr.shape),
                  _const(cmi.shape), _const(tre.shape), _const(tim.shape), _const(dskip.shape)],
        out_specs=(pl.BlockSpec((TC, cu), blk), pl.BlockSpec((TC, nsb * cs), blk), pl.BlockSpec((TC, nsb * cs), blk),
                   pl.BlockSpec((8, nsb * cs), blk), pl.BlockSpec((8, nsb * cs), blk)),
        scratch_shapes=[pltpu.VMEM((nsb, 8, cs), F32), pltpu.VMEM((nsb, 8, cs), F32),
                        pltpu.VMEM((2, nsb, TC, cb), F32)],
        compiler_params=_params(("arbitrary",)),
    )(up, bre, bim, cmr, cmi, tre, tim, dskip)


def _s5_bwd(dy_lat, up, hr, hi, cinr, cini, bre, bim, ctr, cti, tre, tim, dskip, order, nproc, desc, with_skip, name,
            xchg=None):
    t, cu = up.shape
    nsb = bre.shape[0]
    cb = cu // nsb
    cs = bre.shape[2]
    lch = dy_lat.shape[0] // TC
    adesc = not desc

    def body(dy_ref, u_ref, hr_ref, hi_ref, cinr_ref, cini_ref, bre_ref, bim_ref, ctr_ref, cti_ref, tre_ref, tim_ref,
             dsk_ref, du_ref, dar_ref, dai_ref, dcr_ref, dci_ref, dbr_ref, dbi_ref, dd_ref,
             car_r, car_i, mr_all, mi_all, acc_r, acc_i, seg_scr):
        i = pl.program_id(0)
        latent = (order(i) < lch).astype(F32)

        @pl.when(i == 0)
        def _():
            for ref in (car_r, car_i, acc_r, acc_i, dcr_ref, dci_ref, dbr_ref, dbi_ref, dd_ref):
                ref[...] = jnp.zeros_like(ref)

        rows = lax.broadcasted_iota(jnp.int32, (8, cs), 0)
        for sb in range(nsb):
            col, ch = pl.ds(sb * cs, cs), slice(sb * cb, (sb + 1) * cb)
            hr_v, hi_v = hr_ref.at[:, col], hi_ref.at[:, col]
            tre_v, tim_v = tre_ref.at[:, col], tim_ref.at[:, col]
            mr, mi = mr_all.at[sb % 2], mi_all.at[sb % 2]
            useg, dyseg, duseg = seg_scr.at[0, sb], seg_scr.at[1, sb], seg_scr.at[2, sb]
            _to_segments(dy_ref[:, ch] * latent, dyseg)
            _to_segments(u_ref[:, ch], useg)
            dy = dyseg[...]
            dyb = dy.astype(BF16)
            u = useg[...]
            ub = u.astype(BF16)
            mr[...] = _dot(dyb, ctr_ref[sb])
            mi[...] = -_dot(dyb, cti_ref[sb])
            cs_r, cs_i, _, _ = _scan_head(mr, mi, tre_v, tim_v, car_r.at[sb], car_i.at[sb], adesc, True, cs)

            def fix(j, hp_r, hp_i, acc, mr=mr, mi=mi, tre_v=tre_v, tim_v=tim_v, cs_r=cs_r, cs_i=cs_i):
                off = _slab(j)
                pw_r, pw_i = _pow_row(tre_v, tim_v, j, adesc, True, cs)
                pr, pi_ = _cmul(pw_r, pw_i, cs_r, cs_i)
                m_r = mr[pl.ds(off, 8), :] + pr
                m_i = mi[pl.ds(off, 8), :] + pi_
                mr[pl.ds(off, 8), :] = m_r
                mi[pl.ds(off, 8), :] = m_i
                a_r, a_i = acc
                return a_r + hp_r * m_r + hp_i * m_i, a_i + hp_r * m_i - hp_i * m_r

            edge_j, src, keep, sh = (SEG - 1, 0, rows < 7, 7) if desc else (0, (SEG - 1) * 8, rows >= 1, 1)
            h0r = jnp.where(keep, pltpu.roll(hr_v[src:src + 8, :], sh, 0), cinr_ref[:, col])
            h0i = jnp.where(keep, pltpu.roll(hi_v[src:src + 8, :], sh, 0), cini_ref[:, col])
            acc = fix(edge_j, h0r, h0i, (acc_r[sb], acc_i[sb]))

            def p2(jj, acc, fix=fix, hr_v=hr_v, hi_v=hi_v):
                j = jj if desc else jj + 1
                offp = _slab(j + 1 if desc else j - 1)
                return fix(j, hr_v[pl.ds(offp, 8), :], hi_v[pl.ds(offp, 8), :], acc)

            a_r, a_i = lax.fori_loop(0, SEG - 1, p2, acc, unroll=S5_UNROLL)
            acc_r[sb] = a_r
            acc_i[sb] = a_i

            mrb, mib = mr[...].astype(BF16), mi[...].astype(BF16)
            du = _dot_nt(mrb, bre_ref[sb]) + _dot_nt(mib, bim_ref[sb])
            if with_skip:
                du = du + dsk_ref[:, ch] * dy
                dd_ref[:, ch] += _colsum(dy * u)
            duseg[...] = du
            du_ref[:, ch] = _from_segments(duseg)
            dbr_ref[sb] += _dot_tn(ub, mrb)
            dbi_ref[sb] += _dot_tn(ub, mib)
            dcr_ref[sb] += _dot_tn(dyb, hr_v[...].astype(BF16))
            dci_ref[sb] -= _dot_tn(dyb, hi_v[...].astype(BF16))

            @pl.when(i == nproc - 1)
            def _(sb=sb, a_r=a_r, a_i=a_i):
                dar_ref[sb] = _colsum(a_r)
                dai_ref[sb] = _colsum(a_i)

    blk = lambda i: (order(i), 0)
    blk_dy = lambda i: (jnp.minimum(order(i), lch - 1), 0)
    mat = (nsb, cb, cs)
    return _pcall(
        body, name=name, grid=(nproc,), xchg=xchg, edges=_edges1(nproc),
        out_shape=(jax.ShapeDtypeStruct((t, cu), F32),
                   jax.ShapeDtypeStruct((nsb, 1, cs), F32), jax.ShapeDtypeStruct((nsb, 1, cs), F32),
                   jax.ShapeDtypeStruct(mat, F32), jax.ShapeDtypeStruct(mat, F32),
                   jax.ShapeDtypeStruct(mat, F32), jax.ShapeDtypeStruct(mat, F32),
                   jax.ShapeDtypeStruct((1, cu), F32)),
        in_specs=[pl.BlockSpec((TC, cu), blk_dy), pl.BlockSpec((TC, cu), blk), pl.BlockSpec((TC, nsb * cs), blk),
                  pl.BlockSpec((TC, nsb * cs), blk), pl.BlockSpec((8, nsb * cs), blk), pl.BlockSpec((8, nsb * cs), blk),
                  _const(mat), _const(mat), _const(mat), _const(mat), _const(tre.shape), _const(tim.shape),
                  _const((1, cu))],
        out_specs=(pl.BlockSpec((TC, cu), blk), _const((nsb, 1, cs)), _const((nsb, 1, cs)),
                   _const(mat), _const(mat), _const(mat), _const(mat), _const((1, cu))),
        scratch_shapes=[pltpu.VMEM((nsb, 8, cs), F32), pltpu.VMEM((nsb, 8, cs), F32), pltpu.VMEM((2, TC, cs), F32),
                        pltpu.VMEM((2, TC, cs), F32), pltpu.VMEM((nsb, 8, cs), F32), pltpu.VMEM((nsb, 8, cs), F32),
                        pltpu.VMEM((3, nsb, TC, cb), F32)],
        compiler_params=_params(("arbitrary",)),
    )(dy_lat, up, hr, hi, cinr, cini, bre, bim, ctr, cti, tre, tim, dskip)


def _gelu(x):
    k = math.sqrt(2.0 / math.pi)
    return 0.5 * x * (1.0 + jnp.tanh(k * (x + 0.044715 * (x * x * x))))


def _gelu_grad(x):
    k = math.sqrt(2.0 / math.pi)
    th = jnp.tanh(k * (x + 0.044715 * (x * x * x)))
    return 0.5 * (1.0 + th) + 0.5 * x * (1.0 - th * th) * (k * (1.0 + 3.0 * 0.044715 * (x * x)))


def _mix_out_fwd(yf, yb, dpool, x1, modtab, w_pool, pscale, w_glu, w_out):
    l, hm = dpool.shape
    d = x1.shape[1]

    def body(yf_ref, yb_ref, dp_ref, x_ref, mod_ref, wp_ref, ps_ref, wg_ref, wo_ref, x2_ref, yp_ref, cat_ref, mix_ref):
        ypre = yf_ref[...] + yb_ref[...]
        yp_ref[...] = ypre
        yg = _gelu(ypre)
        y2 = yg * _sigmoid(_dot(yg.astype(BF16), wg_ref[...]))
        po = _dot(dp_ref[...].astype(BF16), wp_ref[...]) * ps_ref[...]
        cat = jnp.concatenate([po, y2], axis=1).astype(BF16)
        cat_ref[...] = cat
        mix = _dot(cat, wo_ref[...])
        mix_ref[...] = mix.astype(BF16)
        x2_ref[...] = x_ref[...] + mod_ref[0, 5:6, :] * mix

    row = lambda i: (i, 0)
    lrow = row
    return _pcall(
        body, name="mix_out_fwd", grid=(l // TM,),
        out_shape=(jax.ShapeDtypeStruct((l, d), F32), jax.ShapeDtypeStruct((l, hm), F32),
                   jax.ShapeDtypeStruct((l, d), BF16), jax.ShapeDtypeStruct((l, d), BF16)),
        in_specs=[pl.BlockSpec((TM, hm), lrow), pl.BlockSpec((TM, hm), lrow), pl.BlockSpec((TM, hm), row),
                  pl.BlockSpec((TM, d), lrow), _mod_spec(d, False), _const(w_pool.shape), _const((1, hm)),
                  _const(w_glu.shape), _resident(w_out.shape)],
        out_specs=(pl.BlockSpec((TM, d), row), pl.BlockSpec((TM, hm), row), pl.BlockSpec((TM, d), row),
                   pl.BlockSpec((TM, d), row)),
        compiler_params=_params(("arbitrary",)),
    )(yf, yb, dpool, x1, modtab, w_pool, pscale, w_glu, w_out)


def _mix_out_bwd(dx2, mix, cat, ypre, dpool, modtab, w_pool, pscale, w_glu, w_out):
    l, hm = dpool.shape
    d = dx2.shape[1]
    nt = l // TML

    def body(dx_ref, mix_ref, cat_ref, yp_ref, dp_ref, mod_ref, wp_ref, ps_ref, wg_ref, wo_ref,
             dyp_ref, ddp_ref, dwo_ref, dwg_ref, dwp_ref, dps_ref, dgate_ref, acc_o, acc_g):
        i = pl.program_id(0)

        @pl.when(i == 0)
        def _():
            for ref in (acc_o, acc_g, dwp_ref, dps_ref, dgate_ref):
                ref[...] = jnp.zeros_like(ref)

        dx = dx_ref[...]
        dgate_ref[...] += _colsum(dx * mix_ref[...].astype(F32))
        dmix = (dx * mod_ref[0, 5:6, :]).astype(BF16)
        dcat = _dot_nt(dmix, wo_ref[...])
        acc_o[...] += _dot_tn(cat_ref[...], dmix)
        dpo, dy2 = dcat[:, :hm], dcat[:, hm:]
        dpb = dp_ref[...].astype(BF16)
        pp = _dot(dpb, wp_ref[...])
        dps_ref[...] += _colsum(dpo * pp)
        dpp = (dpo * ps_ref[...]).astype(BF16)
        ddp_ref[...] = _dot_nt(dpp, wp_ref[...])
        dwp_ref[...] += _dot_tn(dpb, dpp)
        ypre = yp_ref[...]
        yg = _gelu(ypre)
        ygb = yg.astype(BF16)
        s = _sigmoid(_dot(ygb, wg_ref[...]))
        dq = (dy2 * yg * s * (1.0 - s)).astype(BF16)
        dyg = dy2 * s + _dot_nt(dq, wg_ref[...])
        acc_g[...] += _dot_tn(ygb, dq)
        dyp_ref[...] = dyg * _gelu_grad(ypre)

        @pl.when(i == nt - 1)
        def _():
            dwo_ref[...] = acc_o[...].astype(BF16)
            dwg_ref[...] = acc_g[...].astype(BF16)

    row = lambda i: (i, 0)
    return _pcall(
        body, name="mix_out_bwd", grid=(nt,),
        out_shape=(jax.ShapeDtypeStruct((l, hm), F32), jax.ShapeDtypeStruct((l, hm), F32),
                   jax.ShapeDtypeStruct((d, d), BF16), jax.ShapeDtypeStruct((hm, hm), BF16),
                   jax.ShapeDtypeStruct((hm, hm), F32), jax.ShapeDtypeStruct((1, hm), F32),
                   jax.ShapeDtypeStruct((1, d), F32)),
        in_specs=[pl.BlockSpec((TML, d), row), pl.BlockSpec((TML, d), row), pl.BlockSpec((TML, d), row),
                  pl.BlockSpec((TML, hm), row), pl.BlockSpec((TML, hm), row), _mod_spec(d, False),
                  _const(w_pool.shape), _const((1, hm)), _const(w_glu.shape), _resident(w_out.shape)],
        out_specs=(pl.BlockSpec((TML, hm), row), pl.BlockSpec((TML, hm), row), _const((d, d)), _const((hm, hm)),
                   _const((hm, hm)), _const((1, hm)), _const((1, d))),
        scratch_shapes=[pltpu.VMEM((d, d), F32), pltpu.VMEM((hm, hm), F32)],
        compiler_params=_params(("arbitrary",)),
    )(dx2, mix, cat, ypre, dpool, modtab, w_pool, pscale, w_glu, w_out)


def _loss_head(x3, target, g):
    l, d = x3.shape

    def body(x_ref, t_ref, g_ref, dx_ref, loss_ref, dg_ref):
        i = pl.program_id(0)

        @pl.when(i == 0)
        def _():
            loss_ref[...] = jnp.zeros_like(loss_ref)
            dg_ref[...] = jnp.zeros_like(dg_ref)

        xx = x_ref[...]
        gg = g_ref[...]
        rstd = lax.rsqrt(jnp.mean(xx * xx, axis=-1, keepdims=True) + EPS)
        xhat = xx * rstd
        err = xhat * gg - t_ref[...]
        row_loss = jnp.mean(err * err, axis=-1, keepdims=True)
        loss_ref[...] += 0.5 * jnp.sum(row_loss, axis=0, keepdims=True)
        dout = err * (1.0 / d)
        dg_ref[...] += _colsum(dout * xhat)
        dxhat = dout * gg
        dx_ref[...] = rstd * (dxhat - xhat * jnp.mean(dxhat * xhat, axis=-1, keepdims=True))

    row = lambda i: (i, 0)
    return _pcall(
        body, name="loss_head", grid=(l // TML,),
        out_shape=(jax.ShapeDtypeStruct((l, d), F32), jax.ShapeDtypeStruct((1, LANE), F32),
                   jax.ShapeDtypeStruct((1, d), F32)),
        in_specs=[pl.BlockSpec((TML, d), row), pl.BlockSpec((TML, d), row), _const((1, d))],
        out_specs=(pl.BlockSpec((TML, d), row), _const((1, LANE)), _const((1, d))),
        compiler_params=_params(("arbitrary",)),
    )(x3, target, g)


def _adamw(parts, w, m, v, name):
    shape = w.shape
    c = shape[-1]
    r = int(np.prod(shape)) // c
    npart = parts.shape[0]
    tr = r
    for cand in (1024, 512, 256, 128, 64, 32, 16):
        if r % cand == 0 and cand * max(c, LANE) * 4 <= ADAM_BLOCK_BYTES:
            tr = cand
            break
    c1 = 1.0 - ADAM_B1
    c2 = 1.0 - ADAM_B2
    bc1 = 1.0 - ADAM_B1 ** ADAM_STEP
    bc2 = 1.0 - ADAM_B2 ** ADAM_STEP

    def body(p_ref, w_ref, m_ref, v_ref, g_ref, d_ref, mo_ref, vo_ref):
        g = p_ref[0].astype(F32)
        for k in range(1, npart):
            g = g + p_ref[k].astype(F32)
        mn = ADAM_B1 * m_ref[...] + c1 * g
        vn = ADAM_B2 * v_ref[...] + c2 * (g * g)
        m_hat = mn / bc1
        v_hat = vn / bc2
        g_ref[...] = g
        mo_ref[...] = mn
        vo_ref[...] = vn
        d_ref[...] = -ADAM_LR * (m_hat / (jnp.sqrt(v_hat) + ADAM_EPS) + ADAM_WD * w_ref[...])

    spec = pl.BlockSpec((tr, c), lambda i: (i, 0))
    outs = _pcall(
        body, name=name, grid=(r // tr,),
        out_shape=tuple(jax.ShapeDtypeStruct((r, c), F32) for _ in range(4)),
        in_specs=[pl.BlockSpec((npart, tr, c), lambda i: (0, i, 0)), spec, spec, spec],
        out_specs=(spec, spec, spec, spec),
        compiler_params=_params(("arbitrary",)),
    )(parts.reshape(npart, r, c), w.reshape(r, c), m.reshape(r, c), v.reshape(r, c))
    return tuple(o.reshape(shape) for o in outs)


def _blockdiag(xb):
    n, a, b = xb.shape[-3:]
    eye = jnp.eye(n, dtype=xb.dtype)
    out = xb[..., :, :, None, :] * eye[:, None, :, None]
    return out.reshape(xb.shape[:-3] + (n * a, n * b))


def _diagblocks(mat, n):
    lead = mat.shape[:-2]
    a, b = mat.shape[-2] // n, mat.shape[-1] // n
    m5 = mat.reshape(lead + (n, a, n, b))
    nl = len(lead)
    dg = jnp.diagonal(m5, axis1=nl, axis2=nl + 2)
    return jnp.moveaxis(dg, -1, nl)


def _to_grid(a, rows):
    return a.reshape(rows, GRID_W, 4, LANE).transpose(2, 0, 1, 3).reshape(4, rows, GRID_W * LANE)


def _from_grid(a, rows):
    return a.reshape(4, rows, GRID_W, LANE).transpose(1, 2, 0, 3).reshape(rows * GRID_W, 4 * LANE)


def _pad128(a):
    flat = a.reshape(-1)
    pad = (-flat.shape[0]) % LANE
    return jnp.pad(flat, (0, pad)) if pad else flat


def kernel(x, c, ctx, c_ctx, norm_g, w_ada, b_ada, ffn_w1, ffn_w3, ffn_w2, w_in, pool_w, pool_scale, s5_a_re, s5_a_im, s5_log_dt, s5_b_re, s5_b_im, s5_c_re, s5_c_im, s5_d, s5_w_glu, w_out, final_g, loss_target, m_c_ctx, m_norm_g, m_w_ada, m_b_ada, m_ffn_w1, m_ffn_w3, m_ffn_w2, m_w_in, m_pool_w, m_pool_scale, m_s5_a_re, m_s5_a_im, m_s5_log_dt, m_s5_b_re, m_s5_b_im, m_s5_c_re, m_s5_c_im, m_s5_d, m_s5_w_glu, m_w_out, m_final_g, v_c_ctx, v_norm_g, v_w_ada, v_b_ada, v_ffn_w1, v_ffn_w3, v_ffn_w2, v_w_in, v_pool_w, v_pool_scale, v_s5_a_re, v_s5_a_im, v_s5_log_dt, v_s5_b_re, v_s5_b_im, v_s5_c_re, v_s5_c_im, v_s5_d, v_s5_w_glu, v_w_out, v_final_g):
    weights = dict(c_ctx=c_ctx, norm_g=norm_g, w_ada=w_ada, b_ada=b_ada, ffn_w1=ffn_w1, ffn_w3=ffn_w3, ffn_w2=ffn_w2,
                   w_in=w_in, pool_w=pool_w, pool_scale=pool_scale, s5_a_re=s5_a_re, s5_a_im=s5_a_im,
                   s5_log_dt=s5_log_dt, s5_b_re=s5_b_re, s5_b_im=s5_b_im, s5_c_re=s5_c_re, s5_c_im=s5_c_im, s5_d=s5_d,
                   s5_w_glu=s5_w_glu, w_out=w_out, final_g=final_g)
    mom_m = dict(c_ctx=m_c_ctx, norm_g=m_norm_g, w_ada=m_w_ada, b_ada=m_b_ada, ffn_w1=m_ffn_w1, ffn_w3=m_ffn_w3,
                 ffn_w2=m_ffn_w2, w_in=m_w_in, pool_w=m_pool_w, pool_scale=m_pool_scale, s5_a_re=m_s5_a_re,
                 s5_a_im=m_s5_a_im, s5_log_dt=m_s5_log_dt, s5_b_re=m_s5_b_re, s5_b_im=m_s5_b_im, s5_c_re=m_s5_c_re,
                 s5_c_im=m_s5_c_im, s5_d=m_s5_d, s5_w_glu=m_s5_w_glu, w_out=m_w_out, final_g=m_final_g)
    mom_v = dict(c_ctx=v_c_ctx, norm_g=v_norm_g, w_ada=v_w_ada, b_ada=v_b_ada, ffn_w1=v_ffn_w1, ffn_w3=v_ffn_w3,
                 ffn_w2=v_ffn_w2, w_in=v_w_in, pool_w=v_pool_w, pool_scale=v_pool_scale, s5_a_re=v_s5_a_re,
                 s5_a_im=v_s5_a_im, s5_log_dt=v_s5_log_dt, s5_b_re=v_s5_b_re, s5_b_im=v_s5_b_im, s5_c_re=v_s5_c_re,
                 s5_c_im=v_s5_c_im, s5_d=v_s5_d, s5_w_glu=v_s5_w_glu, w_out=v_w_out, final_g=v_final_g)

    l, d = x.shape[1], x.shape[2]
    lc = ctx.shape[1]
    t = l + TM
    rows = l // GRID_W
    fblk = ffn_w1.shape[-1]
    ngrp, gp = s5_a_re.shape[2], s5_a_re.shape[3]
    gc = s5_b_re.shape[3]
    hm = ngrp * gc
    nsb = 4
    gsb = ngrp // nsb
    assert lc == TC and TM % TC == 0 and l % TM == 0 and TM == TML and hm == 4 * LANE and ngrp * gp == nsb * 512
    me = 4 * lax.axis_index("x") + 2 * lax.axis_index("y") + lax.axis_index("c")

    padc = ((0, 0), (0, 0), (0, FPAD - fblk))
    w13_loc = [jnp.concatenate([jnp.pad(ffn_w1[0, k], padc[1:]), jnp.pad(ffn_w3[0, k], padc[1:])], axis=-1).astype(BF16)
               for k in range(2)]
    w2_loc = [jnp.pad(ffn_w2[0, k], ((0, FPAD - fblk), (0, 0))).astype(BF16) for k in range(2)]
    gath = _exchange([(w13_loc[0], GATHER_REL), (w2_loc[0], GATHER_REL), (c, GATHER), (norm_g[0], GATHER)],
                     "exchange_weights")
    w13_0 = gath[0]
    w2_0 = gath[1].reshape(NDEV // 2, 2 * FPAD, d)
    c_all = gath[2].reshape(NDEV, d)
    g_all = gath[3].transpose(1, 0, 2).reshape(3, d)

    wa = w_ada.shape[-1]
    c16 = jnp.concatenate([c_all, c_ctx.reshape(1, d), jnp.zeros((7, d), F32)], axis=0)
    b_blk = lax.dynamic_slice_in_dim(b_ada, me * wa, wa, axis=1)
    mod_sl = _ada_fwd(c16, w_ada[0], b_blk)
    gath = _exchange([(mod_sl, GATHER)], "exchange_mod")
    mod_full = gath[0].transpose(1, 0, 2).reshape(16, NDEV * wa)
    mod_l = lax.dynamic_index_in_dim(mod_full, me, axis=0, keepdims=False).reshape(9, d)
    modtab = jnp.stack([mod_full[8].reshape(9, d), mod_l])

    x0 = [jnp.pad(ctx[0], ((0, TM - lc), (0, 0))), x[0]]
    g0, g1, g2 = g_all[0:1], g_all[1:2], g_all[2:3]
    x1, h0, ab0, o0, w13_1, w2_1, w_in_f, w_out_f, w_glu_f = _ffn_fwd(
        x0, modtab, g0, w13_0, w2_0, 0, True, "ffn0_fwd",
        xchg=[(w13_loc[1], GATHER_REL), (w2_loc[1], GATHER_REL), (w_in[0].astype(BF16), GATHER),
              (w_out[0].astype(BF16), GATHER), (s5_w_glu[0].astype(BF16), GATHER)])
    w2_1 = w2_1.reshape(NDEV // 2, 2 * FPAD, d)
    w_in_f = w_in_f.reshape(d, -1)
    w_out_f = w_out_f.reshape(-1, d)
    w_glu_f = w_glu_f.reshape(hm, hm)
    h1, u_pool, up = _mix_in_fwd(x1, modtab, g1, w_in_f)

    v_grid = _to_grid(u_pool[:l], rows)
    dpool = _from_grid(_pool(v_grid, False, "pool_fwd"), rows)
    w_pool_bd = _blockdiag(pool_w[0]).astype(BF16)

    q = 2 * ngrp
    kvec = np.concatenate([np.arange(1, SEG + 1), SEG * np.arange(1, 9), SEG * np.arange(8, 0, -1)]).astype(np.float32)
    a_re3, a_im3 = s5_a_re[0].reshape(q, 1, gp), s5_a_im[0].reshape(q, 1, gp)
    ldt3 = jnp.broadcast_to(s5_log_dt[0].reshape(q, 1, 1), (q, 1, gp))
    c_re3, c_im3 = s5_c_re[0].reshape(q, gc, gp), s5_c_im[0].reshape(q, gc, gp)
    cp_re, cp_im, tab_re, tab_im = _s5_prep(a_re3, a_im3, ldt3, c_re3, c_im3, jnp.broadcast_to(jnp.asarray(kvec).reshape(1, NTAB, 1), (1, NTAB, gp)))
    tabs = [(tab_re.reshape(2, ngrp, NTAB, gp)[k].transpose(1, 0, 2).reshape(NTAB, ngrp * gp),
             tab_im.reshape(2, ngrp, NTAB, gp)[k].transpose(1, 0, 2).reshape(NTAB, ngrp * gp)) for k in range(2)]
    b_t = lambda b: _blockdiag(b[0].reshape(nsb, gsb, gp, gc).transpose(0, 1, 3, 2)).astype(BF16)
    bre, bim = b_t(s5_b_re), b_t(s5_b_im)
    cp_t = lambda cp: _blockdiag(cp.reshape(2, nsb, gsb, gc, gp)).astype(BF16)
    ctr, cti = cp_t(cp_re), cp_t(cp_im)
    cmr, cmi = jnp.swapaxes(ctr, -1, -2), jnp.swapaxes(cti, -1, -2)
    dskip = s5_d

    lch = l // TC
    nproc = lch + 1
    order_f = lambda i: jnp.where(i == 0, lch, i - 1)
    order_b = lambda i: jnp.where(i == 0, lch, lch - i)
    rorder_f = lambda i: jnp.where(i == lch, lch, lch - 1 - i)
    rorder_b = lambda i: i
    y_f, hr_f, hi_f, cinr_f, cini_f = _s5_fwd(up, bre, bim, cmr[0], cmi[0], tabs[0][0], tabs[0][1], dskip,
                                               order_f, nproc, False, True, "s5_fwd_f")
    y_b, hr_b, hi_b, cinr_b, cini_b = _s5_fwd(up, bre, bim, cmr[1], cmi[1], tabs[1][0], tabs[1][1], dskip,
                                              order_b, nproc, True, False, "s5_fwd_b")

    x2, ypre, cat, mix = _mix_out_fwd(y_f, y_b, dpool, x1, modtab, w_pool_bd, pool_scale, w_glu_f, w_out_f)
    x3, h2, ab2, o2 = _ffn_fwd([x2], modtab, g2, w13_1, w2_1, 6, False, "ffn1_fwd")
    dx3, loss_part, dfinal_g = _loss_head(x3, loss_target[0], final_g.reshape(1, d))

    dhp2, d13_1, d2_1 = _ffn_bwd(dx3, modtab, h2, ab2, w13_1, w2_1, 6, False, "ffn1_bwd")
    dx2, dmod_678, dg2 = _ffn_bwd_norm(dx3, [dhp2], [x2], o2, modtab, g2, 6, False, "ffn1_bwd_norm")
    dypre, ddpool, dw_out, dw_glu, dw_pool_bd, dpscale, dgate5 = _mix_out_bwd(
        dx2, mix, cat, ypre, dpool, modtab, w_pool_bd, pool_scale, w_glu_f, w_out_f)

    du_f, dar_f, dai_f, dcr_f, dci_f, dbr_f, dbi_f, dd_skip, r13_1, r2_1 = _s5_bwd(
        dypre, up, hr_f, hi_f, cinr_f, cini_f, bre, bim, ctr[0], cti[0], tabs[0][0], tabs[0][1], dskip,
        rorder_f, nproc, False, True, "s5_bwd_f",
        xchg=[(d13_1, A2A_REL), (d2_1.reshape(NDEV, FPAD, d), A2A_REL)])
    du_b, dar_b, dai_b, dcr_b, dci_b, dbr_b, dbi_b, _, r_out, r_glu = _s5_bwd(
        dypre, up, hr_b, hi_b, cinr_b, cini_b, bre, bim, ctr[1], cti[1], tabs[1][0], tabs[1][1], dskip,
        rorder_b, nproc, True, False, "s5_bwd_b",
        xchg=[(dw_out.reshape(NDEV, -1, d), A2A), (dw_glu.reshape(NDEV, hm // NDEV, hm), A2A)])

    dab_r = jnp.stack([dar_f, dar_b]).reshape(q, 1, gp)
    dab_i = jnp.stack([dai_f, dai_b]).reshape(q, 1, gp)
    dcp_r = _diagblocks(jnp.stack([dcr_f, dcr_b]), gsb).reshape(q, gc, gp)
    dcp_i = _diagblocks(jnp.stack([dci_f, dci_b]), gsb).reshape(q, gc, gp)
    ga_re, ga_im, gldt, gc_re, gc_im = _s5_param_bwd(a_re3, a_im3, ldt3, c_re3, c_im3, dab_r, dab_i, dcp_r, dcp_i)
    gb_re = (_diagblocks(dbr_f, gsb) + _diagblocks(dbr_b, gsb)).transpose(0, 1, 3, 2).reshape(ngrp, gp, gc)
    gb_im = (_diagblocks(dbi_f, gsb) + _diagblocks(dbi_b, gsb)).transpose(0, 1, 3, 2).reshape(ngrp, gp, gc)

    def pack(small):
        offs, pieces, off = {}, [], 0
        for k_, a_ in small.items():
            p_ = _pad128(a_.astype(F32))
            offs[k_] = (off, int(np.prod(a_.shape)))
            off += p_.shape[0]
            pieces.append(p_)
        return jnp.concatenate(pieces).reshape(1, off), offs

    gw_pool = _diagblocks(dw_pool_bd, 4)
    bundle_a, offs = pack(dict(pool_w=gw_pool, pool_scale=dpscale, s5_a_re=ga_re, s5_a_im=ga_im,
                               s5_log_dt=gldt[:, 0, 0], s5_b_re=gb_re, s5_b_im=gb_im, s5_c_re=gc_re, s5_c_im=gc_im,
                               s5_d=dd_skip, final_g=dfinal_g))

    du_pool = _from_grid(_pool(_to_grid(ddpool, rows), True, "pool_bwd"), rows)
    dx1, dw_in, dmod_34, dg1 = _mix_in_bwd(du_pool, du_f, du_b, dx2, h1, x1, modtab, g1, w_in_f, lc)
    dhp0a, d13_0a, d2_0a, bund_a, r_in = _ffn_bwd(
        dx1, modtab, h0, ab0, w13_0, w2_0, 0, True, "ffn0_bwd_a", pair0=0, npair=2,
        xchg=[(bundle_a, GATHER), (dw_in.reshape(NDEV, d // NDEV, -1), A2A)])
    dhp0b, d13_0b, d2_0b, r13_0a, r2_0a = _ffn_bwd(
        dx1, modtab, h0, ab0, w13_0, w2_0, 0, True, "ffn0_bwd_b", pair0=2, npair=2,
        xchg=[(d13_0a, A2A_SAME), (d2_0a.reshape(NDEV // 2, FPAD, d), A2A_SAME)])
    dx0, dmod_012, dg0, r13_0b, r2_0b = _ffn_bwd_norm(
        dx1, [dhp0a, dhp0b], x0, o0, modtab, g0, 0, True, "ffn0_bwd_norm",
        xchg=[(d13_0b, A2A_OTHER), (d2_0b.reshape(NDEV // 2, FPAD, d), A2A_OTHER)])
    r13_0 = jnp.concatenate([r13_0a, r13_0b], axis=0)
    r2_0 = jnp.concatenate([r2_0a, r2_0b], axis=0)
    grad_x = dx0.reshape(1, l, d)

    dmod_c = jnp.concatenate([dmod_012[0], dmod_34[0], jnp.zeros((4, d), F32)], axis=0)
    dmod_l = jnp.concatenate([dmod_012[1], dmod_34[1], dgate5, dmod_678[0]], axis=0)
    bundle_b, offs_b = pack(dict(norm_g=jnp.concatenate([dg0, dg1, dg2], axis=0), dmod_l=dmod_l, dmod_c=dmod_c,
                                 loss=loss_part[:, :1]))
    bund_b = _exchange([(bundle_b, GATHER)], "exchange_small")[0]
    bunds = {**{k_: (bund_a.reshape(NDEV, -1), v_) for k_, v_ in offs.items()},
             **{k_: (bund_b.reshape(NDEV, -1), v_) for k_, v_ in offs_b.items()}}
    r13, r2 = (r13_0, r13_1), (r2_0, r2_1)

    def piece(name):
        b_, (o_, n_) = bunds[name]
        return b_[:, o_:o_ + n_]

    loss = jnp.sum(piece("loss")[:, 0])

    dl_all = lax.dynamic_slice_in_dim(piece("dmod_l"), me * wa, wa, axis=1)
    dc_all = lax.dynamic_slice_in_dim(piece("dmod_c"), me * wa, wa, axis=1)
    g_wada, gc_part = _ada_bwd(c16, w_ada[0], dl_all, dc_all)
    gc_all = _exchange([(gc_part, GATHER)], "exchange_cctx")[0]

    parts = {
        "c_ctx": gc_all.reshape(NDEV, d),
        "norm_g": lax.dynamic_slice_in_dim(piece("norm_g").reshape(NDEV, 3, d), me * (d // NDEV), d // NDEV,
                                           axis=2).reshape((NDEV,) + norm_g.shape),
        "w_ada": g_wada.reshape((1,) + w_ada.shape),
        "b_ada": jnp.concatenate([piece("dmod_l"), piece("dmod_c")], axis=0).reshape((2 * NDEV,) + b_ada.shape),
        "ffn_w1": jnp.stack([r13[0][:, :, :fblk], r13[1][:, :, :fblk]], axis=1).reshape((NDEV,) + ffn_w1.shape),
        "ffn_w3": jnp.stack([r13[0][:, :, FPAD:FPAD + fblk], r13[1][:, :, FPAD:FPAD + fblk]],
                            axis=1).reshape((NDEV,) + ffn_w3.shape),
        "ffn_w2": jnp.stack([r2[0][:, :fblk, :], r2[1][:, :fblk, :]], axis=1).reshape((NDEV,) + ffn_w2.shape),
        "w_in": r_in.reshape((NDEV,) + w_in.shape),
        "s5_w_glu": r_glu.reshape((NDEV,) + s5_w_glu.shape),
        "w_out": r_out.reshape((NDEV,) + w_out.shape),
    }
    for k_ in ("pool_w", "pool_scale", "s5_a_re", "s5_a_im", "s5_log_dt", "s5_b_re", "s5_b_im", "s5_c_re", "s5_c_im",
               "s5_d", "final_g"):
        parts[k_] = piece(k_).reshape((NDEV,) + weights[k_].shape)

    grads, deltas, new_m, new_v = [], [], [], []
    for k_ in weights:
        g_, d_, m_, v_ = _adamw(parts[k_], weights[k_], mom_m[k_], mom_v[k_], "adamw_" + k_)
        grads.append(g_)
        deltas.append(d_)
        new_m.append(m_)
        new_v.append(v_)
    return (loss, grad_x, *grads, *deltas, *new_m, *new_v)
```

```python
import functools
import math

import numpy as np
import jax
import jax.numpy as jnp
from jax import lax
from jax.experimental import pallas as pl
from jax.experimental.pallas import tpu as pltpu

F32 = jnp.float32
BF16 = jnp.bfloat16
AXES = ("x", "y", "c")
NDEV = 8
EPS = 1e-6
TM = 512
TML = 512
TC = 256
SEG = TC // 8
NTAB = SEG + 16
S5_UNROLL = True
GRID_W = 64
POOL_WINDOWS = (2, 4, 8, 16)
LANE = 128
FPAD = 384
VMEM_LIMIT = 56 * 1024 * 1024
ADAM_BLOCK_BYTES = 512 * 1024
ADAM_LR, ADAM_B1, ADAM_B2, ADAM_EPS, ADAM_WD, ADAM_STEP = 0.001, 0.9, 0.999, 1e-08, 0.01, 10


def _raw_call(body, kw):
    return pl.pallas_call(body, **kw)


def _pcall(body, xchg=None, edges=None, **kw):
    extra = ()
    if xchg:
        arrs, kinds = [a for a, _ in xchg], [k for _, k in xchg]
        n = len(arrs)
        n_in, n_out, n_scr = len(kw["in_specs"]), len(kw["out_shape"]), len(kw.get("scratch_shapes", ()))
        inner = body

        def hosted(*refs):
            a, b = n_in, n_in + n
            c_, d_ = b + n_out, b + n_out + n
            e = d_ + n_scr
            first, mid, last = edges()

            @pl.when(first)
            def _():
                _xchg_start(refs[a:b], refs[c_:d_], refs[e:], kinds)

            inner(*refs[:a], *refs[b:c_], *refs[d_:e])

            if any(k == GATHER_REL for k in kinds):
                @pl.when(mid)
                def _():
                    _xchg_relay(refs[a:b], refs[c_:d_], refs[e:], kinds)

            @pl.when(last)
            def _():
                _xchg_finish(refs[a:b], refs[c_:d_], refs[e:], kinds)

        body = hosted
        any_spec = pl.BlockSpec(memory_space=pl.ANY)
        kw = dict(kw, in_specs=list(kw["in_specs"]) + [any_spec] * n,
                  out_shape=tuple(kw["out_shape"]) + _xchg_shapes(arrs, kinds),
                  out_specs=tuple(kw["out_specs"]) + (any_spec,) * n,
                  scratch_shapes=list(kw.get("scratch_shapes", ())) + _xchg_sems(n))
        extra = tuple(arrs)
    call = _raw_call(body, kw)
    return (lambda *args: call(*args, *extra)) if extra else call


def _params(sem):
    return pltpu.CompilerParams(dimension_semantics=sem, vmem_limit_bytes=VMEM_LIMIT)


def _dot(a, b):
    return jnp.dot(a, b, preferred_element_type=F32)


def _dot_nt(a, b):
    return lax.dot_general(a, b, (((1,), (1,)), ((), ())), preferred_element_type=F32)


def _dot_tn(a, b):
    return lax.dot_general(a, b, (((0,), (0,)), ((), ())), preferred_element_type=F32)


def _sigmoid(x):
    return 1.0 / (1.0 + jnp.exp(-x))


def _colsum(a):
    return jnp.sum(a, axis=0, keepdims=True)


def _resident(shape):
    nd = len(shape)
    return pl.BlockSpec(shape, lambda *_: (0,) * nd, pipeline_mode=pl.Buffered(1))


def _const(shape):
    nd = len(shape)
    return pl.BlockSpec(shape, lambda *_: (0,) * nd)


GATHER = "gather"
GATHER_REL = "gather_rel"
A2A = "a2a"
A2A_REL = "a2a_rel"
A2A_SAME = "a2a_same"
A2A_OTHER = "a2a_other"


def _exchange(items, name):
    arrs, kinds = [a for a, _ in items], [k for _, k in items]
    n = len(arrs)

    def body(*refs):
        _xchg_start(refs[:n], refs[n:2 * n], refs[2 * n:], kinds)
        _xchg_relay(refs[:n], refs[n:2 * n], refs[2 * n:], kinds)
        _xchg_finish(refs[:n], refs[n:2 * n], refs[2 * n:], kinds)

    any_spec = pl.BlockSpec(memory_space=pl.ANY)
    outs = _pcall(
        body, name=name, out_shape=_xchg_shapes(arrs, kinds), in_specs=[any_spec] * n, out_specs=[any_spec] * n,
        scratch_shapes=_xchg_sems(n),
    )(*arrs)
    return list(outs)


def _edges1(n0):
    return lambda: (pl.program_id(0) == 0, pl.program_id(0) == (3 * n0) // 4, pl.program_id(0) == n0 - 1)


def _edges2(n0, n1):
    def edges():
        step = pl.program_id(0) * n1 + pl.program_id(1)
        return step == 0, step == (3 * n0 * n1) // 4, step == n0 * n1 - 1
    return edges


def _xchg_shapes(arrs, kinds):
    return tuple(jax.ShapeDtypeStruct(((NDEV,) if k in (GATHER, GATHER_REL) else ()) + tuple(a.shape), a.dtype)
                 for a, k in zip(arrs, kinds))


def _xchg_sems(n):
    return [pltpu.SemaphoreType.DMA((n * (NDEV - 1),)), pltpu.SemaphoreType.DMA((n * (NDEV - 1),)),
            pltpu.SemaphoreType.DMA((n,))]


def _xchg_plan(in_refs, out_refs, sems, kinds):
    send_sems, recv_sems, loc_sems = sems
    x, y, c = (lax.axis_index(a) for a in AXES)
    my_abs, my_chip = 4 * x + 2 * y + c, 2 * x + y

    def peer(p):
        px = 1 - x if p & 4 else x
        py = 1 - y if p & 2 else y
        pc = 1 - c if p & 1 else c
        return (px, py, pc), 4 * px + 2 * py + pc, 2 * px + py

    starts, relays, recvs = [], [], []
    for k, kind in enumerate(kinds):
        src, dst = in_refs[k], out_refs[k]

        def remote(src_ref, row, p, pair, dst=dst, k=k):
            dev, sem = peer(p)[0], k * (NDEV - 1) + pair - 1
            return lambda: pltpu.make_async_remote_copy(
                src_ref=src_ref, dst_ref=dst.at[row], send_sem=send_sems.at[sem], recv_sem=recv_sems.at[sem],
                device_id=dev, device_id_type=pl.DeviceIdType.MESH)

        def local(src_ref, row, dst=dst, k=k):
            return lambda: pltpu.make_async_copy(src_ref, dst.at[row], loc_sems.at[k])

        if kind == GATHER:
            for p in range(1, NDEV):
                starts.append((remote(src, my_abs, p, p), False))
                recvs.append(remote(src, peer(p)[1], p, p))
            starts.append((local(src, my_abs), True))
        elif kind == GATHER_REL:
            for p in (1, 4, 2, 6):
                starts.append((remote(src, 4 * (p & 1) + my_chip, p, p), False))
            for q in (4, 2, 6):
                pchip = peer(q)[2]
                relays.append((remote(src, pchip, q, q), remote(dst.at[pchip], 4 + pchip, 1, q | 1)))
                recvs.append(remote(src, 4 + pchip, 1, q | 1))
            recvs.append(remote(src, 4 + my_chip, 1, 1))
            starts.append((local(src, my_chip), True))
        elif kind in (A2A, A2A_REL):
            for p in range(1, NDEV):
                _, pabs, pchip = peer(p)
                theirs, mine = (pabs, my_abs) if kind == A2A else (4 * (p & 1) + pchip, 4 * (p & 1) + my_chip)
                starts.append((remote(src.at[theirs], mine, p, p), False))
                recvs.append(remote(src.at[theirs], theirs, p, p))
            own = my_abs if kind == A2A else my_chip
            starts.append((local(src.at[own], own), True))
        else:
            for p in ((4, 2, 6) if kind == A2A_SAME else (1, 5, 3, 7)):
                pchip = peer(p)[2]
                starts.append((remote(src.at[pchip], my_chip, p, p), False))
                recvs.append(remote(src.at[pchip], pchip, p, p))
            if kind == A2A_SAME:
                starts.append((local(src.at[my_chip], my_chip), True))
    return starts, relays, recvs


def _xchg_start(in_refs, out_refs, sems, kinds):
    for make, _ in _xchg_plan(in_refs, out_refs, sems, kinds)[0]:
        make().start()


def _xchg_relay(in_refs, out_refs, sems, kinds):
    for arrival, forward in _xchg_plan(in_refs, out_refs, sems, kinds)[1]:
        arrival().wait_recv()
        forward().start()


def _xchg_finish(in_refs, out_refs, sems, kinds):
    starts, relays, recvs = _xchg_plan(in_refs, out_refs, sems, kinds)
    for make in recvs:
        make().wait_recv()
    for make, is_local in starts:
        if is_local:
            make().wait()
        else:
            make().wait_send()
    for _, forward in relays:
        forward().wait_send()


def _ada_fwd(c16, w_blk, b_blk):
    def body(c_ref, w_ref, b_ref, o_ref):
        cc = c_ref[...]
        s = (cc * _sigmoid(cc)).astype(BF16)
        o_ref[...] = _dot(s, w_ref[...].astype(BF16)) + b_ref[...]

    return _pcall(body, name="ada_fwd", out_shape=jax.ShapeDtypeStruct((16, w_blk.shape[1]), F32),
                  compiler_params=_params(None))(c16, w_blk, b_blk)


def _ada_bwd(c16, w_blk, dl, dc):
    d, w = w_blk.shape

    def body(c_ref, w_ref, dl_ref, dc_ref, gw_ref, gc_ref):
        cc = c_ref[...]
        sg = _sigmoid(cc)
        s = (cc * sg).astype(BF16)
        dctot = _colsum(dc_ref[...])
        dm = jnp.concatenate([dl_ref[...], jnp.broadcast_to(dctot, (8, w))], axis=0)
        rows = lax.broadcasted_iota(jnp.int32, (16, w), 0)
        dm = jnp.where(rows <= 8, dm, 0.0).astype(BF16)
        gw_ref[...] = _dot_tn(s, dm)
        t = _dot_nt(jnp.broadcast_to(dctot, (8, w)).astype(BF16), w_ref[...].astype(BF16))[0:1]
        c8, s8 = cc[8:9], sg[8:9]
        gc_ref[...] = t * (s8 * (1.0 + c8 * (1.0 - s8)))

    return _pcall(body, name="ada_bwd",
                  out_shape=(jax.ShapeDtypeStruct((d, w), F32), jax.ShapeDtypeStruct((1, d), F32)),
                  compiler_params=_params(None))(c16, w_blk, dl, dc)


def _norm_fwd(x, g, shift, scale):
    rstd = lax.rsqrt(jnp.mean(x * x, axis=-1, keepdims=True) + EPS)
    xhat = x * rstd
    xn = xhat * g
    return rstd, xhat, xn, xn * (1.0 + scale) + shift


def _norm_bwd(dh, rstd, xhat, xn, g, scale):
    dxn = dh * (1.0 + scale)
    dxhat = dxn * g
    dx = rstd * (dxhat - xhat * jnp.mean(dxhat * xhat, axis=-1, keepdims=True))
    return dx, _colsum(dh), _colsum(dh * xn), _colsum(dxn * xhat)


def _mod_spec(d, has_ctx):
    if has_ctx:
        return pl.BlockSpec((1, 9, d), lambda *ids: (jnp.minimum(ids[-1], 1), 0, 0))
    return pl.BlockSpec((1, 9, d), lambda *ids: (1, 0, 0))


def _row_specs(nx, tm, d):
    if nx == 1:
        return [pl.BlockSpec((tm, d), lambda i: (i, 0))]
    return [pl.BlockSpec((tm, d), lambda i: (0, 0)), pl.BlockSpec((tm, d), lambda i: (jnp.maximum(i - 1, 0), 0))]


def _rows(refs):
    if len(refs) == 1:
        return refs[0][...]
    return jnp.where(pl.program_id(0) == 0, refs[0][...], refs[1][...])


def _tile(has_ctx):
    return TM if has_ctx else TML


def _tile_of(nt, has_ctx):
    if has_ctx:
        return lambda i: jnp.where(i == 0, nt - 1, i - 1)
    return lambda i: i


def _dmod_spec(nrow, d, has_ctx):
    if has_ctx:
        return pl.BlockSpec((1, nrow, d), lambda i: (jnp.minimum(i, 1), 0, 0))
    return pl.BlockSpec((1, nrow, d), lambda i: (0, 0, 0))


def _ffn_fwd(x, modtab, g, w13, w2, k0, has_ctx, name, xchg=None):
    t, d = sum(a.shape[0] for a in x), x[0].shape[1]
    nb, _, w = w13.shape
    fp = w // 2
    tm = _tile(has_ctx)
    nx = len(x)

    def body(*refs):
        mod_ref, g_ref, w13_ref, w2_ref, xo_ref, h_ref, ab_ref, o_ref = refs[nx:]
        xx = _rows(refs[:nx])
        shift, scale, gate = mod_ref[0, k0:k0 + 1, :], mod_ref[0, k0 + 1:k0 + 2, :], mod_ref[0, k0 + 2:k0 + 3, :]
        _, _, _, h = _norm_fwd(xx, g_ref[...], shift, scale)
        hb = h.astype(BF16)
        h_ref[...] = hb
        acc = jnp.zeros((tm, d), F32)
        for p in range(nb // 2):
            zs = []
            for q in range(2):
                blk = 2 * p + q
                ab = _dot(hb, w13_ref[blk])
                ab_ref[:, blk * w:(blk + 1) * w] = ab.astype(BF16)
                a, b = ab[:, :fp], ab[:, fp:]
                zs.append((a * _sigmoid(a) * b).astype(BF16))
            acc = acc + _dot(jnp.concatenate(zs, axis=1), w2_ref[p])
        o_ref[...] = acc.astype(BF16)
        xo_ref[...] = xx + (0.5 * gate) * acc

    tile = _tile_of(t // tm, has_ctx)
    row = lambda i: (tile(i), 0)
    return _pcall(
        body, name=name, grid=(t // tm,), xchg=xchg, edges=_edges1(t // tm),
        out_shape=(jax.ShapeDtypeStruct((t, d), F32), jax.ShapeDtypeStruct((t, d), BF16),
                   jax.ShapeDtypeStruct((t, nb * w), BF16), jax.ShapeDtypeStruct((t, d), BF16)),
        in_specs=_row_specs(nx, tm, d) + [_mod_spec(d, has_ctx), _const((1, d)),
                                          _resident(w13.shape), _resident(w2.shape)],
        out_specs=(pl.BlockSpec((tm, d), row), pl.BlockSpec((tm, d), row), pl.BlockSpec((tm, nb * w), row),
                   pl.BlockSpec((tm, d), row)),
        compiler_params=_params(("arbitrary",)),
    )(*x, modtab, g, w13, w2)


def _ffn_bwd(dy, modtab, h, ab, w13, w2, k0, has_ctx, name, pair0=0, npair=None, xchg=None):
    t, d = dy.shape
    _, _, w = w13.shape
    fp = w // 2
    npair = w13.shape[0] // 2 if npair is None else npair
    nb = 2 * npair
    tm = _tile(has_ctx)
    nt = t // tm

    def body(dy_ref, mod_ref, h_ref, ab_ref, w13_ref, w2_ref, dh_ref, d13_ref, d2_ref, acc13, acc2):
        i = pl.program_id(1)

        @pl.when(i == 0)
        def _():
            acc13[...] = jnp.zeros_like(acc13)
            acc2[...] = jnp.zeros_like(acc2)

        gate = mod_ref[0, k0 + 2:k0 + 3, :]
        do = (dy_ref[...] * (0.5 * gate)).astype(BF16)
        dz = _dot_nt(do, w2_ref[0])
        hb = h_ref[...]
        dh = jnp.zeros((tm, d), F32)
        zs = []
        for q in range(2):
            ab = ab_ref[:, q * w:(q + 1) * w].astype(F32)
            a, b = ab[:, :fp], ab[:, fp:]
            sg = _sigmoid(a)
            sa = a * sg
            dzq = dz[:, q * fp:(q + 1) * fp]
            da = dzq * b * (sg * (1.0 + a * (1.0 - sg)))
            db = dzq * sa
            dab = jnp.concatenate([da, db], axis=1).astype(BF16)
            dh = dh + _dot_nt(dab, w13_ref[q])
            acc13[q] += _dot_tn(hb, dab)
            zs.append((sa * b).astype(BF16))
        acc2[...] += _dot_tn(jnp.concatenate(zs, axis=1), do)
        dh_ref[0] = dh.astype(BF16)

        @pl.when(i == nt - 1)
        def _():
            d13_ref[...] = acc13[...].astype(BF16)
            d2_ref[0] = acc2[...].astype(BF16)

    mod_spec = _mod_spec(d, has_ctx)
    tile = _tile_of(nt, has_ctx)
    return _pcall(
        body, name=name, grid=(npair, nt), xchg=xchg, edges=_edges2(npair, nt),
        out_shape=(jax.ShapeDtypeStruct((npair, t, d), BF16), jax.ShapeDtypeStruct((nb, d, w), BF16),
                   jax.ShapeDtypeStruct((npair, 2 * fp, d), BF16)),
        in_specs=[pl.BlockSpec((tm, d), lambda p, i: (tile(i), 0)), mod_spec,
                  pl.BlockSpec((tm, d), lambda p, i: (tile(i), 0)),
                  pl.BlockSpec((tm, 2 * w), lambda p, i: (tile(i), pair0 + p)),
                  pl.BlockSpec((2, d, w), lambda p, i: (pair0 + p, 0, 0)),
                  pl.BlockSpec((1, 2 * fp, d), lambda p, i: (pair0 + p, 0, 0))],
        out_specs=(pl.BlockSpec((1, tm, d), lambda p, i: (p, tile(i), 0)),
                   pl.BlockSpec((2, d, w), lambda p, i: (p, 0, 0)),
                   pl.BlockSpec((1, 2 * fp, d), lambda p, i: (p, 0, 0))),
        scratch_shapes=[pltpu.VMEM((2, d, w), F32), pltpu.VMEM((2 * fp, d), F32)],
        compiler_params=_params(("arbitrary", "arbitrary")),
    )(dy, modtab, h, ab, w13, w2)


def _ffn_bwd_norm(dy, dhps, x, o, modtab, g, k0, has_ctx, name, xchg=None):
    t, d = dy.shape
    ngrp = 2 if has_ctx else 1
    tm = _tile(has_ctx)
    ndh, nx = len(dhps), len(x)
    lat0 = ngrp - 1

    def body(dy_ref, *rest):
        dhp_refs, x_refs = rest[:ndh], rest[ndh:ndh + nx]
        o_ref, mod_ref, g_ref, dx_ref, dmod_ref, dg_ref = rest[ndh + nx:]
        i = pl.program_id(0)

        @pl.when(i == 0)
        def _():
            dg_ref[...] = jnp.zeros_like(dg_ref)

        @pl.when((i == 0) | (i == ngrp - 1))
        def _():
            dmod_ref[...] = jnp.zeros_like(dmod_ref)

        dh = None
        for ref in dhp_refs:
            for p in range(ref.shape[0]):
                dh = ref[p].astype(F32) if dh is None else dh + ref[p].astype(F32)
        scale = mod_ref[0, k0 + 1:k0 + 2, :]
        gg = g_ref[...]
        rstd, xhat, xn, _ = _norm_fwd(_rows(x_refs), gg, 0.0, scale)
        dxn, dshift, dscale, dg = _norm_bwd(dh, rstd, xhat, xn, gg, scale)
        dyv = dy_ref[...]
        dx_ref[...] = dyv + dxn
        dmod_ref[0, 0:1, :] += dshift
        dmod_ref[0, 1:2, :] += dscale
        dmod_ref[0, 2:3, :] += _colsum(0.5 * dyv * o_ref[...].astype(F32))
        dg_ref[...] += dg

    tile = _tile_of(t // tm, has_ctx)
    row = lambda i: (tile(i), 0)
    return _pcall(
        body, name=name, grid=(t // tm,), xchg=xchg, edges=_edges1(t // tm),
        out_shape=(jax.ShapeDtypeStruct((t - lat0 * tm, d), F32), jax.ShapeDtypeStruct((ngrp, 3, d), F32),
                   jax.ShapeDtypeStruct((1, d), F32)),
        in_specs=[pl.BlockSpec((tm, d), row)]
        + [pl.BlockSpec((a.shape[0], tm, d), lambda i: (0, tile(i), 0)) for a in dhps]
        + _row_specs(nx, tm, d) + [pl.BlockSpec((tm, d), row), _mod_spec(d, has_ctx), _const((1, d))],
        out_specs=(pl.BlockSpec((tm, d), lambda i: (jnp.maximum(i - lat0, 0), 0)), _dmod_spec(3, d, has_ctx),
                   _const((1, d))),
        compiler_params=_params(("arbitrary",)),
    )(dy, *dhps, *x, o, modtab, g)


def _mix_in_fwd(x1, modtab, g, w_in):
    t, d = x1.shape
    hm = w_in.shape[1] // 2

    def body(x_ref, mod_ref, g_ref, w_ref, h_ref, up_ref, us_ref):
        _, _, _, h = _norm_fwd(x_ref[...], g_ref[...], mod_ref[0, 3:4, :], mod_ref[0, 4:5, :])
        hb = h.astype(BF16)
        h_ref[...] = hb
        u = _dot(hb, w_ref[...])
        up_ref[...] = u[:, :hm]
        us_ref[...] = u[:, hm:]

    tile = _tile_of(t // TM, True)
    row = lambda i: (tile(i), 0)
    return _pcall(
        body, name="mix_in_fwd", grid=(t // TM,),
        out_shape=(jax.ShapeDtypeStruct((t, d), BF16), jax.ShapeDtypeStruct((t, hm), F32),
                   jax.ShapeDtypeStruct((t, hm), F32)),
        in_specs=[pl.BlockSpec((TM, d), row), _mod_spec(d, True), _const((1, d)), _resident(w_in.shape)],
        out_specs=(pl.BlockSpec((TM, d), row), pl.BlockSpec((TM, hm), row), pl.BlockSpec((TM, hm), row)),
        compiler_params=_params(("arbitrary",)),
    )(x1, modtab, g, w_in)


def _mix_in_bwd(du_pool, du_f, du_b, dx2, h1, x1, modtab, g, w_in, lc):
    t, d = x1.shape
    m = w_in.shape[1]
    hm = m // 2
    nt = t // TM

    def body(dup_ref, duf_ref, dub_ref, dx2_ref, h_ref, x_ref, mod_ref, g_ref, w_ref, dx_ref, dw_ref, dmod_ref, dg_ref, acc):
        i = pl.program_id(0)

        @pl.when(i == 0)
        def _():
            acc[...] = jnp.zeros_like(acc)
            dg_ref[...] = jnp.zeros_like(dg_ref)

        @pl.when(i <= 1)
        def _():
            dmod_ref[...] = jnp.zeros_like(dmod_ref)

        lat = (i > 0).astype(F32)
        valid = (i > 0) | (lax.broadcasted_iota(jnp.int32, (TM, 1), 0) < lc)
        du_s5 = jnp.where(valid, duf_ref[...] + dub_ref[...], 0.0)
        du = jnp.concatenate([dup_ref[...] * lat, du_s5], axis=1).astype(BF16)
        dh = _dot_nt(du, w_ref[...])
        acc[...] += _dot_tn(h_ref[...], du)
        scale = mod_ref[0, 4:5, :]
        gg = g_ref[...]
        rstd, xhat, xn, _ = _norm_fwd(x_ref[...], gg, 0.0, scale)
        dxn, dshift, dscale, dg = _norm_bwd(dh, rstd, xhat, xn, gg, scale)
        dx_ref[...] = dx2_ref[...] * lat + dxn
        dmod_ref[0, 0:1, :] += dshift
        dmod_ref[0, 1:2, :] += dscale
        dg_ref[...] += dg

        @pl.when(i == nt - 1)
        def _():
            dw_ref[...] = acc[...].astype(BF16)

    tile = _tile_of(nt, True)
    row = lambda i: (tile(i), 0)
    lrow = lambda i: (jnp.maximum(i - 1, 0), 0)
    return _pcall(
        body, name="mix_in_bwd", grid=(nt,),
        out_shape=(jax.ShapeDtypeStruct((t, d), F32), jax.ShapeDtypeStruct((d, m), BF16),
                   jax.ShapeDtypeStruct((2, 2, d), F32), jax.ShapeDtypeStruct((1, d), F32)),
        in_specs=[pl.BlockSpec((TM, hm), lrow), pl.BlockSpec((TM, hm), row), pl.BlockSpec((TM, hm), row),
                  pl.BlockSpec((TM, d), lrow), pl.BlockSpec((TM, d), row), pl.BlockSpec((TM, d), row),
                  _mod_spec(d, True), _const((1, d)), _resident(w_in.shape)],
        out_specs=(pl.BlockSpec((TM, d), row), _const((d, m)), _dmod_spec(2, d, True), _const((1, d))),
        scratch_shapes=[pltpu.VMEM((d, m), F32)],
        compiler_params=_params(("arbitrary",)),
    )(du_pool, du_f, du_b, dx2, h1, x1, modtab, g, w_in)


def _pool(v, rows, transpose, name):
    l, n = rows * GRID_W, GRID_W * LANE
    ngrp = v.shape[1] // LANE
    nchunk = 4
    cw = n // nchunk

    def rowsum(val, lo, hi):
        ri = lax.broadcasted_iota(jnp.int32, (rows, rows), 0)
        ci = lax.broadcasted_iota(jnp.int32, (rows, rows), 1)
        sel = (((ci - ri) >= -lo) & ((ci - ri) <= hi)).astype(BF16)
        v1 = val.astype(BF16)
        r1 = val - v1.astype(F32)
        v2 = r1.astype(BF16)
        v3 = (r1 - v2.astype(F32)).astype(BF16)
        return _dot(sel, v1) + _dot(sel, v2) + _dot(sel, v3)

    def one(v_ref, o_ref, g_scr, r_scr, a_scr, win):
        for c in range(GRID_W):
            g_scr[:, c * LANE:(c + 1) * LANE] = v_ref[pl.ds(c, rows, stride=GRID_W), :]
        lo = win // 2
        hi = win - 1 - lo
        rlo, rhi = (hi, lo) if transpose else (lo, hi)
        ridx = lax.broadcasted_iota(jnp.int32, (rows, 1), 0)
        cnt_r = (jnp.minimum(ridx + hi + 1, rows) - jnp.maximum(ridx - lo, 0)).astype(F32)
        cidx = lax.broadcasted_iota(jnp.int32, (1, n), 1) // LANE
        cnt_c = (jnp.minimum(cidx + hi + 1, GRID_W) - jnp.maximum(cidx - lo, 0)).astype(F32)
        if not transpose:
            for k in range(nchunk):
                sl = slice(k * cw, (k + 1) * cw)
                r_scr[:, sl] = rowsum(g_scr[:, sl], rlo, rhi) / cnt_r
        else:
            r_scr[...] = g_scr[...] / cnt_c
        a_scr[...] = r_scr[...]
        for j in range(-rlo, rhi + 1):
            if j == 0:
                continue
            c0, c1 = max(0, -j), min(GRID_W, GRID_W - j)
            a_scr[:, c0 * LANE:c1 * LANE] += r_scr[:, (c0 + j) * LANE:(c1 + j) * LANE]
        if not transpose:
            r_scr[...] = a_scr[...] / cnt_c - g_scr[...]
        else:
            for k in range(nchunk):
                sl = slice(k * cw, (k + 1) * cw)
                r_scr[:, sl] = rowsum(a_scr[:, sl] / cnt_r, rlo, rhi) - g_scr[:, sl]
        for c in range(GRID_W):
            o_ref[pl.ds(c, rows, stride=GRID_W), :] = r_scr[:, c * LANE:(c + 1) * LANE]

    def body(v_ref, o_ref, g_scr, r_scr, a_scr):
        grp = pl.program_id(0)
        for k, win in enumerate(POOL_WINDOWS):
            @pl.when(grp == k)
            def _(win=win):
                one(v_ref, o_ref, g_scr, r_scr, a_scr, win)

    spec = pl.BlockSpec((l, LANE), lambda k: (0, k))
    return _pcall(
        body, name=name, grid=(ngrp,), out_shape=jax.ShapeDtypeStruct((l, ngrp * LANE), F32),
        in_specs=[spec], out_specs=spec,
        scratch_shapes=[pltpu.VMEM((rows, n), F32), pltpu.VMEM((rows, n), F32), pltpu.VMEM((rows, n), F32)],
        compiler_params=_params(("arbitrary",)),
    )(v)


def _cmul(ar, ai, br, bi):
    return ar * br - ai * bi, ar * bi + ai * br


def _s5_prep(a_re, a_im, log_dt, c_re, c_im, kvec):
    q, nc, p = c_re.shape

    def body(ar_ref, ai_ref, ldt_ref, cr_ref, ci_ref, k_ref, cpr_ref, cpi_ref, tr_ref, ti_ref):
        lr, li = ar_ref[...], ai_ref[...]
        dt = jnp.exp(ldt_ref[...])
        kk = k_ref[...]
        mag = jnp.exp(kk * (lr * dt))
        ph = kk * (li * dt)
        tr_ref[...] = mag * jnp.cos(ph)
        ti_ref[...] = mag * jnp.sin(ph)
        m1 = jnp.exp(lr * dt)
        abr, abi = m1 * jnp.cos(li * dt), m1 * jnp.sin(li * dt)
        den = lr * lr + li * li
        xr, xi = abr - 1.0, abi
        cfr, cfi = (xr * lr + xi * li) / den, (xi * lr - xr * li) / den
        pr, pi_ = _cmul(cr_ref[...], ci_ref[...], cfr, cfi)
        cpr_ref[...] = pr
        cpi_ref[...] = pi_

    return _pcall(
        body, name="s5_prep",
        out_shape=(jax.ShapeDtypeStruct((q, nc, p), F32), jax.ShapeDtypeStruct((q, nc, p), F32),
                   jax.ShapeDtypeStruct((q, NTAB, p), F32), jax.ShapeDtypeStruct((q, NTAB, p), F32)),
        compiler_params=_params(None),
    )(a_re, a_im, log_dt, c_re, c_im, kvec)


def _s5_param_bwd(a_re, a_im, log_dt, c_re, c_im, dab_r, dab_i, dcp_r, dcp_i):
    q, nc, p = c_re.shape

    def body(ar_ref, ai_ref, ldt_ref, cr_ref, ci_ref, dar_ref, dai_ref, dcr_ref, dci_ref,
             gar_ref, gai_ref, gdt_ref, gcr_ref, gci_ref):
        lr, li = ar_ref[...], ai_ref[...]
        dt = jnp.exp(ldt_ref[...])
        m1 = jnp.exp(lr * dt)
        abr, abi = m1 * jnp.cos(li * dt), m1 * jnp.sin(li * dt)
        den = lr * lr + li * li
        xr, xi = abr - 1.0, abi
        cfr, cfi = (xr * lr + xi * li) / den, (xi * lr - xr * li) / den
        cr, ci = cr_ref[...], ci_ref[...]
        dcr, dci = dcr_ref[...], dci_ref[...]
        gcr, gci = _cmul(dcr, dci, cfr, -cfi)
        gcr_ref[...] = gcr
        gci_ref[...] = gci
        t_r, t_i = _cmul(cr, -ci, dcr, dci)
        dcf_r = jnp.sum(t_r, axis=1, keepdims=True)
        dcf_i = jnp.sum(t_i, axis=1, keepdims=True)
        ilr, ili = lr / den, -li / den
        u_r, u_i = _cmul(dcf_r, dcf_i, ilr, -ili)
        dab_r_, dab_i_ = dar_ref[...] + u_r, dai_ref[...] + u_i
        v_r, v_i = _cmul(dab_r_, dab_i_, abr, -abi)
        cl_r, cl_i = _cmul(cfr, cfi, ilr, ili)
        w_r, w_i = _cmul(dcf_r, dcf_i, cl_r, -cl_i)
        gar_ref[...] = v_r * dt - w_r
        gai_ref[...] = v_i * dt - w_i
        la_r, la_i = _cmul(lr, li, abr, abi)
        ddt = la_r * dab_r_ + la_i * dab_i_
        gdt_ref[...] = dt * jnp.sum(ddt, axis=2, keepdims=True)

    return _pcall(
        body, name="s5_param_bwd",
        out_shape=(jax.ShapeDtypeStruct((q, 1, p), F32), jax.ShapeDtypeStruct((q, 1, p), F32),
                   jax.ShapeDtypeStruct((q, 1, p), F32), jax.ShapeDtypeStruct((q, nc, p), F32),
                   jax.ShapeDtypeStruct((q, nc, p), F32)),
        compiler_params=_params(None),
    )(a_re, a_im, log_dt, c_re, c_im, dab_r, dab_i, dcp_r, dcp_i)


def _bcast8(row, c):
    return jnp.broadcast_to(row, (8, c))


def _slab(j):
    return j * 8 if isinstance(j, int) else pl.multiple_of(j * 8, 8)


def _scan_head(xr, xi, tre_ref, tim_ref, car_r, car_i, desc, conj, cs):
    sg = -1.0 if conj else 1.0
    ar = _bcast8(tre_ref[0:1, :], cs)
    ai = sg * _bcast8(tim_ref[0:1, :], cs)

    def p1(jj, carry):
        hr, hi = carry
        off = _slab(SEG - 1 - jj if desc else jj)
        pr, pi_ = _cmul(ar, ai, hr, hi)
        nr = pr + xr[pl.ds(off, 8), :]
        ni = pi_ + xi[pl.ds(off, 8), :]
        xr[pl.ds(off, 8), :] = nr
        xi[pl.ds(off, 8), :] = ni
        return nr, ni

    zero = jnp.zeros((8, cs), F32)
    ir, ii = lax.fori_loop(0, SEG, p1, (zero, zero), unroll=S5_UNROLL)
    cin_r, cin_i = car_r[...], car_i[...]
    rows = lax.broadcasted_iota(jnp.int32, (8, cs), 0)
    for s, krow in ((1, SEG), (2, SEG + 1), (4, SEG + 3)):
        keep, sh = (rows < 8 - s, 8 - s) if desc else (rows >= s, s)
        sr = jnp.where(keep, pltpu.roll(ir, sh, 0), 0.0)
        si = jnp.where(keep, pltpu.roll(ii, sh, 0), 0.0)
        pr, pi_ = _cmul(tre_ref[krow:krow + 1, :], sg * tim_ref[krow:krow + 1, :], sr, si)
        ir, ii = ir + pr, ii + pi_
    q0 = SEG + 8 if desc else SEG
    pr, pi_ = _cmul(tre_ref[q0:q0 + 8, :], sg * tim_ref[q0:q0 + 8, :], cin_r, cin_i)
    fr, fi = ir + pr, ii + pi_
    keep, sh, edge = (rows < 7, 7, 0) if desc else (rows >= 1, 1, 7)
    cs_r = jnp.where(keep, pltpu.roll(fr, sh, 0), cin_r)
    cs_i = jnp.where(keep, pltpu.roll(fi, sh, 0), cin_i)
    car_r[...] = _bcast8(fr[edge:edge + 1, :], cs)
    car_i[...] = _bcast8(fi[edge:edge + 1, :], cs)
    return cs_r, cs_i, cin_r, cin_i


def _pow_row(tre_ref, tim_ref, j, desc, conj, cs):
    k = SEG - 1 - j if desc else j
    sg = -1.0 if conj else 1.0
    return _bcast8(tre_ref[pl.ds(k, 1), :], cs), sg * _bcast8(tim_ref[pl.ds(k, 1), :], cs)


def _to_segments(val, dst):
    for r in range(8):
        dst[pl.ds(r, SEG, stride=8), :] = val[r * SEG:(r + 1) * SEG, :]


def _from_segments(src):
    return jnp.concatenate([src[pl.ds(r, SEG, stride=8), :] for r in range(8)], axis=0)


def _s5_fwd(up, bre, bim, cmr, cmi, tre, tim, dskip, order, nproc, desc, with_skip, name):
    t, cu = up.shape
    nsb = bre.shape[0]
    cb = cu // nsb
    cs = bre.shape[2]
    nch = t // TC

    def body(u_ref, bre_ref, bim_ref, cmr_ref, cmi_ref, tre_ref, tim_ref, dsk_ref,
             y_ref, hr_ref, hi_ref, cinr_ref, cini_ref, car_r, car_i, seg_scr):
        i = pl.program_id(0)

        @pl.when(i == 0)
        def _():
            car_r[...] = jnp.zeros_like(car_r)
            car_i[...] = jnp.zeros_like(car_i)

        for sb in range(nsb):
            col, ch = pl.ds(sb * cs, cs), slice(sb * cb, (sb + 1) * cb)
            hr_v, hi_v = hr_ref.at[:, col], hi_ref.at[:, col]
            tre_v, tim_v = tre_ref.at[:, col], tim_ref.at[:, col]
            useg, yseg = seg_scr.at[0, sb], seg_scr.at[1, sb]
            _to_segments(u_ref[:, ch], useg)
            u = useg[...]
            ub = u.astype(BF16)
            hr_v[...] = _dot(ub, bre_ref[sb])
            hi_v[...] = _dot(ub, bim_ref[sb])
            cs_r, cs_i, cin_r, cin_i = _scan_head(hr_v, hi_v, tre_v, tim_v, car_r.at[sb], car_i.at[sb], desc, False, cs)
            cinr_ref[:, col] = cin_r
            cini_ref[:, col] = cin_i

            def p2(j, _, hr_v=hr_v, hi_v=hi_v, tre_v=tre_v, tim_v=tim_v, cs_r=cs_r, cs_i=cs_i):
                off = _slab(j)
                pw_r, pw_i = _pow_row(tre_v, tim_v, j, desc, False, cs)
                pr, pi_ = _cmul(pw_r, pw_i, cs_r, cs_i)
                hr_v[pl.ds(off, 8), :] = hr_v[pl.ds(off, 8), :] + pr
                hi_v[pl.ds(off, 8), :] = hi_v[pl.ds(off, 8), :] + pi_
                return 0

            lax.fori_loop(0, SEG, p2, 0, unroll=S5_UNROLL)
            y = _dot(hr_v[...].astype(BF16), cmr_ref[sb]) - _dot(hi_v[...].astype(BF16), cmi_ref[sb])
            if with_skip:
                y = y + dsk_ref[:, ch] * u
            yseg[...] = y
            y_ref[:, ch] = _from_segments(yseg)

    blk = lambda i: (order(i), 0)
    return _pcall(
        body, name=name, grid=(nproc,),
        out_shape=(jax.ShapeDtypeStruct((t, cu), F32), jax.ShapeDtypeStruct((t, nsb * cs), F32),
                   jax.ShapeDtypeStruct((t, nsb * cs), F32), jax.ShapeDtypeStruct((nch * 8, nsb * cs), F32),
                   jax.ShapeDtypeStruct((nch * 8, nsb * cs), F32)),
        in_specs=[pl.BlockSpec((TC, cu), blk), _const(bre.shape), _const(bim.shape), _const(cmr.shape),
                  _const(cmi.shape), _const(tre.shape), _const(tim.shape), _const(dskip.shape)],
        out_specs=(pl.BlockSpec((TC, cu), blk), pl.BlockSpec((TC, nsb * cs), blk), pl.BlockSpec((TC, nsb * cs), blk),
                   pl.BlockSpec((8, nsb * cs), blk), pl.BlockSpec((8, nsb * cs), blk)),
        scratch_shapes=[pltpu.VMEM((nsb, 8, cs), F32), pltpu.VMEM((nsb, 8, cs), F32),
                        pltpu.VMEM((2, nsb, TC, cb), F32)],
        compiler_params=_params(("arbitrary",)),
    )(up, bre, bim, cmr, cmi, tre, tim, dskip)


def _s5_bwd(dy_lat, up, hr, hi, cinr, cini, bre, bim, ctr, cti, tre, tim, dskip, order, nproc, desc, with_skip, name,
            xchg=None):
    t, cu = up.shape
    nsb = bre.shape[0]
    cb = cu // nsb
    cs = bre.shape[2]
    lch = dy_lat.shape[0] // TC
    adesc = not desc

    def body(dy_ref, u_ref, hr_ref, hi_ref, cinr_ref, cini_ref, bre_ref, bim_ref, ctr_ref, cti_ref, tre_ref, tim_ref,
             dsk_ref, du_ref, dar_ref, dai_ref, dcr_ref, dci_ref, dbr_ref, dbi_ref, dd_ref,
             car_r, car_i, mr_all, mi_all, acc_r, acc_i, seg_scr):
        i = pl.program_id(0)
        latent = (order(i) < lch).astype(F32)

        @pl.when(i == 0)
        def _():
            for ref in (car_r, car_i, acc_r, acc_i, dcr_ref, dci_ref, dbr_ref, dbi_ref, dd_ref):
                ref[...] = jnp.zeros_like(ref)

        rows = lax.broadcasted_iota(jnp.int32, (8, cs), 0)
        for sb in range(nsb):
            col, ch = pl.ds(sb * cs, cs), slice(sb * cb, (sb + 1) * cb)
            hr_v, hi_v = hr_ref.at[:, col], hi_ref.at[:, col]
            tre_v, tim_v = tre_ref.at[:, col], tim_ref.at[:, col]
            mr, mi = mr_all.at[sb % 2], mi_all.at[sb % 2]
            useg, dyseg, duseg = seg_scr.at[0, sb], seg_scr.at[1, sb], seg_scr.at[2, sb]
            _to_segments(dy_ref[:, ch] * latent, dyseg)
            _to_segments(u_ref[:, ch], useg)
            dy = dyseg[...]
            dyb = dy.astype(BF16)
            u = useg[...]
            ub = u.astype(BF16)
            mr[...] = _dot(dyb, ctr_ref[sb])
            mi[...] = -_dot(dyb, cti_ref[sb])
            cs_r, cs_i, _, _ = _scan_head(mr, mi, tre_v, tim_v, car_r.at[sb], car_i.at[sb], adesc, True, cs)

            def fix(j, hp_r, hp_i, acc, mr=mr, mi=mi, tre_v=tre_v, tim_v=tim_v, cs_r=cs_r, cs_i=cs_i):
                off = _slab(j)
                pw_r, pw_i = _pow_row(tre_v, tim_v, j, adesc, True, cs)
                pr, pi_ = _cmul(pw_r, pw_i, cs_r, cs_i)
                m_r = mr[pl.ds(off, 8), :] + pr
                m_i = mi[pl.ds(off, 8), :] + pi_
                mr[pl.ds(off, 8), :] = m_r
                mi[pl.ds(off, 8), :] = m_i
                a_r, a_i = acc
                return a_r + hp_r * m_r + hp_i * m_i, a_i + hp_r * m_i - hp_i * m_r

            edge_j, src, keep, sh = (SEG - 1, 0, rows < 7, 7) if desc else (0, (SEG - 1) * 8, rows >= 1, 1)
            h0r = jnp.where(keep, pltpu.roll(hr_v[src:src + 8, :], sh, 0), cinr_ref[:, col])
            h0i = jnp.where(keep, pltpu.roll(hi_v[src:src + 8, :], sh, 0), cini_ref[:, col])
            acc = fix(edge_j, h0r, h0i, (acc_r[sb], acc_i[sb]))

            def p2(jj, acc, fix=fix, hr_v=hr_v, hi_v=hi_v):
                j = jj if desc else jj + 1
                offp = _slab(j + 1 if desc else j - 1)
                return fix(j, hr_v[pl.ds(offp, 8), :], hi_v[pl.ds(offp, 8), :], acc)

            a_r, a_i = lax.fori_loop(0, SEG - 1, p2, acc, unroll=S5_UNROLL)
            acc_r[sb] = a_r
            acc_i[sb] = a_i

            mrb, mib = mr[...].astype(BF16), mi[...].astype(BF16)
            du = _dot_nt(mrb, bre_ref[sb]) + _dot_nt(mib, bim_ref[sb])
            if with_skip:
                du = du + dsk_ref[:, ch] * dy
                dd_ref[:, ch] += _colsum(dy * u)
            duseg[...] = du
            du_ref[:, ch] = _from_segments(duseg)
            dbr_ref[sb] += _dot_tn(ub, mrb)
            dbi_ref[sb] += _dot_tn(ub, mib)
            dcr_ref[sb] += _dot_tn(dyb, hr_v[...].astype(BF16))
            dci_ref[sb] -= _dot_tn(dyb, hi_v[...].astype(BF16))

            @pl.when(i == nproc - 1)
            def _(sb=sb, a_r=a_r, a_i=a_i):
                dar_ref[sb] = _colsum(a_r)
                dai_ref[sb] = _colsum(a_i)

    blk = lambda i: (order(i), 0)
    blk_dy = lambda i: (jnp.minimum(order(i), lch - 1), 0)
    mat = (nsb, cb, cs)
    return _pcall(
        body, name=name, grid=(nproc,), xchg=xchg, edges=_edges1(nproc),
        out_shape=(jax.ShapeDtypeStruct((t, cu), F32),
                   jax.ShapeDtypeStruct((nsb, 1, cs), F32), jax.ShapeDtypeStruct((nsb, 1, cs), F32),
                   jax.ShapeDtypeStruct(mat, F32), jax.ShapeDtypeStruct(mat, F32),
                   jax.ShapeDtypeStruct(mat, F32), jax.ShapeDtypeStruct(mat, F32),
                   jax.ShapeDtypeStruct((1, cu), F32)),
        in_specs=[pl.BlockSpec((TC, cu), blk_dy), pl.BlockSpec((TC, cu), blk), pl.BlockSpec((TC, nsb * cs), blk),
                  pl.BlockSpec((TC, nsb * cs), blk), pl.BlockSpec((8, nsb * cs), blk), pl.BlockSpec((8, nsb * cs), blk),
                  _const(mat), _const(mat), _const(mat), _const(mat), _const(tre.shape), _const(tim.shape),
                  _const((1, cu))],
        out_specs=(pl.BlockSpec((TC, cu), blk), _const((nsb, 1, cs)), _const((nsb, 1, cs)),
                   _const(mat), _const(mat), _const(mat), _const(mat), _const((1, cu))),
        scratch_shapes=[pltpu.VMEM((nsb, 8, cs), F32), pltpu.VMEM((nsb, 8, cs), F32), pltpu.VMEM((2, TC, cs), F32),
                        pltpu.VMEM((2, TC, cs), F32), pltpu.VMEM((nsb, 8, cs), F32), pltpu.VMEM((nsb, 8, cs), F32),
                        pltpu.VMEM((3, nsb, TC, cb), F32)],
        compiler_params=_params(("arbitrary",)),
    )(dy_lat, up, hr, hi, cinr, cini, bre, bim, ctr, cti, tre, tim, dskip)


def _gelu(x):
    k = math.sqrt(2.0 / math.pi)
    return 0.5 * x * (1.0 + jnp.tanh(k * (x + 0.044715 * (x * x * x))))


def _gelu_grad(x):
    k = math.sqrt(2.0 / math.pi)
    th = jnp.tanh(k * (x + 0.044715 * (x * x * x)))
    return 0.5 * (1.0 + th) + 0.5 * x * (1.0 - th * th) * (k * (1.0 + 3.0 * 0.044715 * (x * x)))


def _mix_out_fwd(yf, yb, dpool, x1, modtab, w_pool, pscale, w_glu, w_out):
    l, hm = dpool.shape
    d = x1.shape[1]

    def body(yf_ref, yb_ref, dp_ref, x_ref, mod_ref, wp_ref, ps_ref, wg_ref, wo_ref, x2_ref, yp_ref, cat_ref, mix_ref):
        ypre = yf_ref[...] + yb_ref[...]
        yp_ref[...] = ypre
        yg = _gelu(ypre)
        y2 = yg * _sigmoid(_dot(yg.astype(BF16), wg_ref[...]))
        po = _dot(dp_ref[...].astype(BF16), wp_ref[...]) * ps_ref[...]
        cat = jnp.concatenate([po, y2], axis=1).astype(BF16)
        cat_ref[...] = cat
        mix = _dot(cat, wo_ref[...])
        mix_ref[...] = mix.astype(BF16)
        x2_ref[...] = x_ref[...] + mod_ref[0, 5:6, :] * mix

    row = lambda i: (i, 0)
    lrow = row
    return _pcall(
        body, name="mix_out_fwd", grid=(l // TM,),
        out_shape=(jax.ShapeDtypeStruct((l, d), F32), jax.ShapeDtypeStruct((l, hm), F32),
                   jax.ShapeDtypeStruct((l, d), BF16), jax.ShapeDtypeStruct((l, d), BF16)),
        in_specs=[pl.BlockSpec((TM, hm), lrow), pl.BlockSpec((TM, hm), lrow), pl.BlockSpec((TM, hm), row),
                  pl.BlockSpec((TM, d), lrow), _mod_spec(d, False), _const(w_pool.shape), _const((1, hm)),
                  _const(w_glu.shape), _resident(w_out.shape)],
        out_specs=(pl.BlockSpec((TM, d), row), pl.BlockSpec((TM, hm), row), pl.BlockSpec((TM, d), row),
                   pl.BlockSpec((TM, d), row)),
        compiler_params=_params(("arbitrary",)),
    )(yf, yb, dpool, x1, modtab, w_pool, pscale, w_glu, w_out)


def _mix_out_bwd(dx2, mix, cat, ypre, dpool, modtab, w_pool, pscale, w_glu, w_out):
    l, hm = dpool.shape
    d = dx2.shape[1]
    nt = l // TML

    def body(dx_ref, mix_ref, cat_ref, yp_ref, dp_ref, mod_ref, wp_ref, ps_ref, wg_ref, wo_ref,
             dyp_ref, ddp_ref, dwo_ref, dwg_ref, dwp_ref, dps_ref, dgate_ref, acc_o, acc_g):
        i = pl.program_id(0)

        @pl.when(i == 0)
        def _():
            for ref in (acc_o, acc_g, dwp_ref, dps_ref, dgate_ref):
                ref[...] = jnp.zeros_like(ref)

        dx = dx_ref[...]
        dgate_ref[...] += _colsum(dx * mix_ref[...].astype(F32))
        dmix = (dx * mod_ref[0, 5:6, :]).astype(BF16)
        dcat = _dot_nt(dmix, wo_ref[...])
        acc_o[...] += _dot_tn(cat_ref[...], dmix)
        dpo, dy2 = dcat[:, :hm], dcat[:, hm:]
        dpb = dp_ref[...].astype(BF16)
        pp = _dot(dpb, wp_ref[...])
        dps_ref[...] += _colsum(dpo * pp)
        dpp = (dpo * ps_ref[...]).astype(BF16)
        ddp_ref[...] = _dot_nt(dpp, wp_ref[...])
        dwp_ref[...] += _dot_tn(dpb, dpp)
        ypre = yp_ref[...]
        yg = _gelu(ypre)
        ygb = yg.astype(BF16)
        s = _sigmoid(_dot(ygb, wg_ref[...]))
        dq = (dy2 * yg * s * (1.0 - s)).astype(BF16)
        dyg = dy2 * s + _dot_nt(dq, wg_ref[...])
        acc_g[...] += _dot_tn(ygb, dq)
        dyp_ref[...] = dyg * _gelu_grad(ypre)

        @pl.when(i == nt - 1)
        def _():
            dwo_ref[...] = acc_o[...].astype(BF16)
            dwg_ref[...] = acc_g[...].astype(BF16)

    row = lambda i: (i, 0)
    return _pcall(
        body, name="mix_out_bwd", grid=(nt,),
        out_shape=(jax.ShapeDtypeStruct((l, hm), F32), jax.ShapeDtypeStruct((l, hm), F32),
                   jax.ShapeDtypeStruct((d, d), BF16), jax.ShapeDtypeStruct((hm, hm), BF16),
                   jax.ShapeDtypeStruct((hm, hm), F32), jax.ShapeDtypeStruct((1, hm), F32),
                   jax.ShapeDtypeStruct((1, d), F32)),
        in_specs=[pl.BlockSpec((TML, d), row), pl.BlockSpec((TML, d), row), pl.BlockSpec((TML, d), row),
                  pl.BlockSpec((TML, hm), row), pl.BlockSpec((TML, hm), row), _mod_spec(d, False),
                  _const(w_pool.shape), _const((1, hm)), _const(w_glu.shape), _resident(w_out.shape)],
        out_specs=(pl.BlockSpec((TML, hm), row), pl.BlockSpec((TML, hm), row), _const((d, d)), _const((hm, hm)),
                   _const((hm, hm)), _const((1, hm)), _const((1, d))),
        scratch_shapes=[pltpu.VMEM((d, d), F32), pltpu.VMEM((hm, hm), F32)],
        compiler_params=_params(("arbitrary",)),
    )(dx2, mix, cat, ypre, dpool, modtab, w_pool, pscale, w_glu, w_out)


def _loss_head(x3, target, g):
    l, d = x3.shape

    def body(x_ref, t_ref, g_ref, dx_ref, loss_ref, dg_ref):
        i = pl.program_id(0)

        @pl.when(i == 0)
        def _():
            loss_ref[...] = jnp.zeros_like(loss_ref)
            dg_ref[...] = jnp.zeros_like(dg_ref)

        xx = x_ref[...]
        gg = g_ref[...]
        rstd = lax.rsqrt(jnp.mean(xx * xx, axis=-1, keepdims=True) + EPS)
        xhat = xx * rstd
        err = xhat * gg - t_ref[...]
        row_loss = jnp.mean(err * err, axis=-1, keepdims=True)
        loss_ref[...] += 0.5 * jnp.sum(row_loss, axis=0, keepdims=True)
        dout = err * (1.0 / d)
        dg_ref[...] += _colsum(dout * xhat)
        dxhat = dout * gg
        dx_ref[...] = rstd * (dxhat - xhat * jnp.mean(dxhat * xhat, axis=-1, keepdims=True))

    row = lambda i: (i, 0)
    return _pcall(
        body, name="loss_head", grid=(l // TML,),
        out_shape=(jax.ShapeDtypeStruct((l, d), F32), jax.ShapeDtypeStruct((1, LANE), F32),
                   jax.ShapeDtypeStruct((1, d), F32)),
        in_specs=[pl.BlockSpec((TML, d), row), pl.BlockSpec((TML, d), row), _const((1, d))],
        out_specs=(pl.BlockSpec((TML, d), row), _const((1, LANE)), _const((1, d))),
        compiler_params=_params(("arbitrary",)),
    )(x3, target, g)


def _adamw(parts, w, m, v, name):
    shape = w.shape
    c = shape[-1]
    r = int(np.prod(shape)) // c
    npart = parts.shape[0]
    tr = r
    for cand in (1024, 512, 256, 128, 64, 32, 16):
        if r % cand == 0 and cand * max(c, LANE) * 4 <= ADAM_BLOCK_BYTES:
            tr = cand
            break
    c1 = 1.0 - ADAM_B1
    c2 = 1.0 - ADAM_B2
    bc1 = 1.0 - ADAM_B1 ** ADAM_STEP
    bc2 = 1.0 - ADAM_B2 ** ADAM_STEP

    def body(p_ref, w_ref, m_ref, v_ref, g_ref, d_ref, mo_ref, vo_ref):
        g = p_ref[0].astype(F32)
        for k in range(1, npart):
            g = g + p_ref[k].astype(F32)
        mn = ADAM_B1 * m_ref[...] + c1 * g
        vn = ADAM_B2 * v_ref[...] + c2 * (g * g)
        m_hat = mn / bc1
        v_hat = vn / bc2
        g_ref[...] = g
        mo_ref[...] = mn
        vo_ref[...] = vn
        d_ref[...] = -ADAM_LR * (m_hat / (jnp.sqrt(v_hat) + ADAM_EPS) + ADAM_WD * w_ref[...])

    spec = pl.BlockSpec((tr, c), lambda i: (i, 0))
    outs = _pcall(
        body, name=name, grid=(r // tr,),
        out_shape=tuple(jax.ShapeDtypeStruct((r, c), F32) for _ in range(4)),
        in_specs=[pl.BlockSpec((npart, tr, c), lambda i: (0, i, 0)), spec, spec, spec],
        out_specs=(spec, spec, spec, spec),
        compiler_params=_params(("arbitrary",)),
    )(parts.reshape(npart, r, c), w.reshape(r, c), m.reshape(r, c), v.reshape(r, c))
    return tuple(o.reshape(shape) for o in outs)


def _adamw_ffn(parts, lane_blk, w, m, v, name):
    _, nl, r, c = w.shape
    rp = parts[0][0].shape[1]
    if lane_blk is None:
        tc = 256
        grid = (c // tc,)
        p_spec = lambda a: pl.BlockSpec((a.shape[0], rp, tc), lambda i: (0, 0, i))
        w_spec = pl.BlockSpec((1, nl, r, tc), lambda i: (0, 0, 0, i))
        take = lambda ref, k: ref[k, 0:r, :]
    else:
        tr = 128
        grid = (r // tr,)
        p_spec = lambda a: pl.BlockSpec((a.shape[0], tr, FPAD), lambda i: (0, i, lane_blk))
        w_spec = pl.BlockSpec((1, nl, tr, c), lambda i: (0, 0, i, 0))
        take = lambda ref, k: ref[k, :, 0:c]
    flat = [a for layer in parts for a in layer]
    counts = [len(layer) for layer in parts]
    c1 = 1.0 - ADAM_B1
    c2 = 1.0 - ADAM_B2
    bc1 = 1.0 - ADAM_B1 ** ADAM_STEP
    bc2 = 1.0 - ADAM_B2 ** ADAM_STEP

    def body(*refs):
        w_ref, m_ref, v_ref, g_ref, d_ref, mo_ref, vo_ref = refs[len(flat):]
        first = 0
        for lyr in range(nl):
            g = None
            for ref in refs[first:first + counts[lyr]]:
                for k in range(ref.shape[0]):
                    term = take(ref, k).astype(F32)
                    g = term if g is None else g + term
            first += counts[lyr]
            mn = ADAM_B1 * m_ref[0, lyr] + c1 * g
            vn = ADAM_B2 * v_ref[0, lyr] + c2 * (g * g)
            g_ref[0, lyr] = g
            mo_ref[0, lyr] = mn
            vo_ref[0, lyr] = vn
            d_ref[0, lyr] = -ADAM_LR * ((mn / bc1) / (jnp.sqrt(vn / bc2) + ADAM_EPS) + ADAM_WD * w_ref[0, lyr])

    return _pcall(
        body, name=name, grid=grid, out_shape=tuple(jax.ShapeDtypeStruct(w.shape, F32) for _ in range(4)),
        in_specs=[p_spec(a) for a in flat] + [w_spec] * 3, out_specs=(w_spec,) * 4,
        compiler_params=_params(("arbitrary",)),
    )(*flat, w, m, v)


def _blockdiag(xb):
    n, a, b = xb.shape[-3:]
    eye = jnp.eye(n, dtype=xb.dtype)
    out = xb[..., :, :, None, :] * eye[:, None, :, None]
    return out.reshape(xb.shape[:-3] + (n * a, n * b))


def _diagblocks(mat, n):
    lead = mat.shape[:-2]
    a, b = mat.shape[-2] // n, mat.shape[-1] // n
    m5 = mat.reshape(lead + (n, a, n, b))
    nl = len(lead)
    dg = jnp.diagonal(m5, axis1=nl, axis2=nl + 2)
    return jnp.moveaxis(dg, -1, nl)


def _pad128(a):
    flat = a.reshape(-1)
    pad = (-flat.shape[0]) % LANE
    return jnp.pad(flat, (0, pad)) if pad else flat


def kernel(x, c, ctx, c_ctx, norm_g, w_ada, b_ada, ffn_w1, ffn_w3, ffn_w2, w_in, pool_w, pool_scale, s5_a_re, s5_a_im, s5_log_dt, s5_b_re, s5_b_im, s5_c_re, s5_c_im, s5_d, s5_w_glu, w_out, final_g, loss_target, m_c_ctx, m_norm_g, m_w_ada, m_b_ada, m_ffn_w1, m_ffn_w3, m_ffn_w2, m_w_in, m_pool_w, m_pool_scale, m_s5_a_re, m_s5_a_im, m_s5_log_dt, m_s5_b_re, m_s5_b_im, m_s5_c_re, m_s5_c_im, m_s5_d, m_s5_w_glu, m_w_out, m_final_g, v_c_ctx, v_norm_g, v_w_ada, v_b_ada, v_ffn_w1, v_ffn_w3, v_ffn_w2, v_w_in, v_pool_w, v_pool_scale, v_s5_a_re, v_s5_a_im, v_s5_log_dt, v_s5_b_re, v_s5_b_im, v_s5_c_re, v_s5_c_im, v_s5_d, v_s5_w_glu, v_w_out, v_final_g):
    weights = dict(c_ctx=c_ctx, norm_g=norm_g, w_ada=w_ada, b_ada=b_ada, ffn_w1=ffn_w1, ffn_w3=ffn_w3, ffn_w2=ffn_w2,
                   w_in=w_in, pool_w=pool_w, pool_scale=pool_scale, s5_a_re=s5_a_re, s5_a_im=s5_a_im,
                   s5_log_dt=s5_log_dt, s5_b_re=s5_b_re, s5_b_im=s5_b_im, s5_c_re=s5_c_re, s5_c_im=s5_c_im, s5_d=s5_d,
                   s5_w_glu=s5_w_glu, w_out=w_out, final_g=final_g)
    mom_m = dict(c_ctx=m_c_ctx, norm_g=m_norm_g, w_ada=m_w_ada, b_ada=m_b_ada, ffn_w1=m_ffn_w1, ffn_w3=m_ffn_w3,
                 ffn_w2=m_ffn_w2, w_in=m_w_in, pool_w=m_pool_w, pool_scale=m_pool_scale, s5_a_re=m_s5_a_re,
                 s5_a_im=m_s5_a_im, s5_log_dt=m_s5_log_dt, s5_b_re=m_s5_b_re, s5_b_im=m_s5_b_im, s5_c_re=m_s5_c_re,
                 s5_c_im=m_s5_c_im, s5_d=m_s5_d, s5_w_glu=m_s5_w_glu, w_out=m_w_out, final_g=m_final_g)
    mom_v = dict(c_ctx=v_c_ctx, norm_g=v_norm_g, w_ada=v_w_ada, b_ada=v_b_ada, ffn_w1=v_ffn_w1, ffn_w3=v_ffn_w3,
                 ffn_w2=v_ffn_w2, w_in=v_w_in, pool_w=v_pool_w, pool_scale=v_pool_scale, s5_a_re=v_s5_a_re,
                 s5_a_im=v_s5_a_im, s5_log_dt=v_s5_log_dt, s5_b_re=v_s5_b_re, s5_b_im=v_s5_b_im, s5_c_re=v_s5_c_re,
                 s5_c_im=v_s5_c_im, s5_d=v_s5_d, s5_w_glu=v_s5_w_glu, w_out=v_w_out, final_g=v_final_g)

    l, d = x.shape[1], x.shape[2]
    lc = ctx.shape[1]
    t = l + TM
    rows = l // GRID_W
    fblk = ffn_w1.shape[-1]
    ngrp, gp = s5_a_re.shape[2], s5_a_re.shape[3]
    gc = s5_b_re.shape[3]
    hm = ngrp * gc
    nsb = 4
    gsb = ngrp // nsb
    assert lc == TC and TM % TC == 0 and l % TM == 0 and TM == TML and hm == 4 * LANE and ngrp * gp == nsb * 512
    me = 4 * lax.axis_index("x") + 2 * lax.axis_index("y") + lax.axis_index("c")

    padc = ((0, 0), (0, 0), (0, FPAD - fblk))
    w13_loc = [jnp.concatenate([jnp.pad(ffn_w1[0, k], padc[1:]), jnp.pad(ffn_w3[0, k], padc[1:])], axis=-1).astype(BF16)
               for k in range(2)]
    w2_loc = [jnp.pad(ffn_w2[0, k], ((0, FPAD - fblk), (0, 0))).astype(BF16) for k in range(2)]
    gath = _exchange([(w13_loc[0], GATHER_REL), (w2_loc[0], GATHER_REL), (c, GATHER), (norm_g[0], GATHER)],
                     "exchange_weights")
    w13_0 = gath[0]
    w2_0 = gath[1].reshape(NDEV // 2, 2 * FPAD, d)
    c_all = gath[2].reshape(NDEV, d)
    g_all = gath[3].transpose(1, 0, 2).reshape(3, d)

    wa = w_ada.shape[-1]
    c16 = jnp.concatenate([c_all, c_ctx.reshape(1, d), jnp.zeros((7, d), F32)], axis=0)
    b_blk = lax.dynamic_slice_in_dim(b_ada, me * wa, wa, axis=1)
    mod_sl = _ada_fwd(c16, w_ada[0], b_blk)
    gath = _exchange([(mod_sl, GATHER)], "exchange_mod")
    mod_full = gath[0].transpose(1, 0, 2).reshape(16, NDEV * wa)
    mod_l = lax.dynamic_index_in_dim(mod_full, me, axis=0, keepdims=False).reshape(9, d)
    modtab = jnp.stack([mod_full[8].reshape(9, d), mod_l])

    x0 = [jnp.pad(ctx[0], ((0, TM - lc), (0, 0))), x[0]]
    g0, g1, g2 = g_all[0:1], g_all[1:2], g_all[2:3]
    x1, h0, ab0, o0, w13_1, w2_1, w_in_f, w_out_f, w_glu_f = _ffn_fwd(
        x0, modtab, g0, w13_0, w2_0, 0, True, "ffn0_fwd",
        xchg=[(w13_loc[1], GATHER_REL), (w2_loc[1], GATHER_REL), (w_in[0].astype(BF16), GATHER),
              (w_out[0].astype(BF16), GATHER), (s5_w_glu[0].astype(BF16), GATHER)])
    w2_1 = w2_1.reshape(NDEV // 2, 2 * FPAD, d)
    w_in_f = w_in_f.reshape(d, -1)
    w_out_f = w_out_f.reshape(-1, d)
    w_glu_f = w_glu_f.reshape(hm, hm)
    h1, u_pool, up = _mix_in_fwd(x1, modtab, g1, w_in_f)

    dpool = _pool(u_pool, rows, False, "pool_fwd")
    w_pool_bd = _blockdiag(pool_w[0]).astype(BF16)

    q = 2 * ngrp
    kvec = np.concatenate([np.arange(1, SEG + 1), SEG * np.arange(1, 9), SEG * np.arange(8, 0, -1)]).astype(np.float32)
    a_re3, a_im3 = s5_a_re[0].reshape(q, 1, gp), s5_a_im[0].reshape(q, 1, gp)
    ldt3 = jnp.broadcast_to(s5_log_dt[0].reshape(q, 1, 1), (q, 1, gp))
    c_re3, c_im3 = s5_c_re[0].reshape(q, gc, gp), s5_c_im[0].reshape(q, gc, gp)
    cp_re, cp_im, tab_re, tab_im = _s5_prep(a_re3, a_im3, ldt3, c_re3, c_im3, jnp.broadcast_to(jnp.asarray(kvec).reshape(1, NTAB, 1), (1, NTAB, gp)))
    tabs = [(tab_re.reshape(2, ngrp, NTAB, gp)[k].transpose(1, 0, 2).reshape(NTAB, ngrp * gp),
             tab_im.reshape(2, ngrp, NTAB, gp)[k].transpose(1, 0, 2).reshape(NTAB, ngrp * gp)) for k in range(2)]
    b_t = lambda b: _blockdiag(b[0].reshape(nsb, gsb, gp, gc).transpose(0, 1, 3, 2)).astype(BF16)
    bre, bim = b_t(s5_b_re), b_t(s5_b_im)
    cp_t = lambda cp: _blockdiag(cp.reshape(2, nsb, gsb, gc, gp)).astype(BF16)
    ctr, cti = cp_t(cp_re), cp_t(cp_im)
    cmr, cmi = jnp.swapaxes(ctr, -1, -2), jnp.swapaxes(cti, -1, -2)
    dskip = s5_d

    lch = l // TC
    nproc = lch + 1
    order_f = lambda i: jnp.where(i == 0, lch, i - 1)
    order_b = lambda i: jnp.where(i == 0, lch, lch - i)
    rorder_f = lambda i: jnp.where(i == lch, lch, lch - 1 - i)
    rorder_b = lambda i: i
    y_f, hr_f, hi_f, cinr_f, cini_f = _s5_fwd(up, bre, bim, cmr[0], cmi[0], tabs[0][0], tabs[0][1], dskip,
                                               order_f, nproc, False, True, "s5_fwd_f")
    y_b, hr_b, hi_b, cinr_b, cini_b = _s5_fwd(up, bre, bim, cmr[1], cmi[1], tabs[1][0], tabs[1][1], dskip,
                                              order_b, nproc, True, False, "s5_fwd_b")

    x2, ypre, cat, mix = _mix_out_fwd(y_f, y_b, dpool, x1, modtab, w_pool_bd, pool_scale, w_glu_f, w_out_f)
    x3, h2, ab2, o2 = _ffn_fwd([x2], modtab, g2, w13_1, w2_1, 6, False, "ffn1_fwd")
    dx3, loss_part, dfinal_g = _loss_head(x3, loss_target[0], final_g.reshape(1, d))

    dhp2, d13_1, d2_1 = _ffn_bwd(dx3, modtab, h2, ab2, w13_1, w2_1, 6, False, "ffn1_bwd")
    dx2, dmod_678, dg2 = _ffn_bwd_norm(dx3, [dhp2], [x2], o2, modtab, g2, 6, False, "ffn1_bwd_norm")
    dypre, ddpool, dw_out, dw_glu, dw_pool_bd, dpscale, dgate5 = _mix_out_bwd(
        dx2, mix, cat, ypre, dpool, modtab, w_pool_bd, pool_scale, w_glu_f, w_out_f)

    du_f, dar_f, dai_f, dcr_f, dci_f, dbr_f, dbi_f, dd_skip, r13_1, r2_1 = _s5_bwd(
        dypre, up, hr_f, hi_f, cinr_f, cini_f, bre, bim, ctr[0], cti[0], tabs[0][0], tabs[0][1], dskip,
        rorder_f, nproc, False, True, "s5_bwd_f",
        xchg=[(d13_1, A2A_REL), (d2_1.reshape(NDEV, FPAD, d), A2A_REL)])
    du_b, dar_b, dai_b, dcr_b, dci_b, dbr_b, dbi_b, _, r_out, r_glu = _s5_bwd(
        dypre, up, hr_b, hi_b, cinr_b, cini_b, bre, bim, ctr[1], cti[1], tabs[1][0], tabs[1][1], dskip,
        rorder_b, nproc, True, False, "s5_bwd_b",
        xchg=[(dw_out.reshape(NDEV, -1, d), A2A), (dw_glu.reshape(NDEV, hm // NDEV, hm), A2A)])

    dab_r = jnp.stack([dar_f, dar_b]).reshape(q, 1, gp)
    dab_i = jnp.stack([dai_f, dai_b]).reshape(q, 1, gp)
    dcp_r = _diagblocks(jnp.stack([dcr_f, dcr_b]), gsb).reshape(q, gc, gp)
    dcp_i = _diagblocks(jnp.stack([dci_f, dci_b]), gsb).reshape(q, gc, gp)
    ga_re, ga_im, gldt, gc_re, gc_im = _s5_param_bwd(a_re3, a_im3, ldt3, c_re3, c_im3, dab_r, dab_i, dcp_r, dcp_i)
    gb_re = (_diagblocks(dbr_f, gsb) + _diagblocks(dbr_b, gsb)).transpose(0, 1, 3, 2).reshape(ngrp, gp, gc)
    gb_im = (_diagblocks(dbi_f, gsb) + _diagblocks(dbi_b, gsb)).transpose(0, 1, 3, 2).reshape(ngrp, gp, gc)

    def pack(small):
        offs, pieces, off = {}, [], 0
        for k_, a_ in small.items():
            p_ = _pad128(a_.astype(F32))
            offs[k_] = (off, int(np.prod(a_.shape)))
            off += p_.shape[0]
            pieces.append(p_)
        return jnp.concatenate(pieces).reshape(1, off), offs

    gw_pool = _diagblocks(dw_pool_bd, 4)
    bundle_a, offs = pack(dict(pool_w=gw_pool, pool_scale=dpscale, s5_a_re=ga_re, s5_a_im=ga_im,
                               s5_log_dt=gldt[:, 0, 0], s5_b_re=gb_re, s5_b_im=gb_im, s5_c_re=gc_re, s5_c_im=gc_im,
                               s5_d=dd_skip, final_g=dfinal_g))

    du_pool = _pool(ddpool, rows, True, "pool_bwd")
    dx1, dw_in, dmod_34, dg1 = _mix_in_bwd(du_pool, du_f, du_b, dx2, h1, x1, modtab, g1, w_in_f, lc)
    dhp0a, d13_0a, d2_0a, bund_a, r_in = _ffn_bwd(
        dx1, modtab, h0, ab0, w13_0, w2_0, 0, True, "ffn0_bwd_a", pair0=0, npair=2,
        xchg=[(bundle_a, GATHER), (dw_in.reshape(NDEV, d // NDEV, -1), A2A)])
    dhp0b, d13_0b, d2_0b, r13_0a, r2_0a = _ffn_bwd(
        dx1, modtab, h0, ab0, w13_0, w2_0, 0, True, "ffn0_bwd_b", pair0=2, npair=2,
        xchg=[(d13_0a, A2A_SAME), (d2_0a.reshape(NDEV // 2, FPAD, d), A2A_SAME)])
    dx0, dmod_012, dg0, r13_0b, r2_0b = _ffn_bwd_norm(
        dx1, [dhp0a, dhp0b], x0, o0, modtab, g0, 0, True, "ffn0_bwd_norm",
        xchg=[(d13_0b, A2A_OTHER), (d2_0b.reshape(NDEV // 2, FPAD, d), A2A_OTHER)])
    grad_x = dx0.reshape(1, l, d)

    dmod_c = jnp.concatenate([dmod_012[0], dmod_34[0], jnp.zeros((4, d), F32)], axis=0)
    dmod_l = jnp.concatenate([dmod_012[1], dmod_34[1], dgate5, dmod_678[0]], axis=0)
    bundle_b, offs_b = pack(dict(norm_g=jnp.concatenate([dg0, dg1, dg2], axis=0), dmod_l=dmod_l, dmod_c=dmod_c,
                                 loss=loss_part[:, :1]))
    bund_b = _exchange([(bundle_b, GATHER)], "exchange_small")[0]
    bunds = {**{k_: (bund_a.reshape(NDEV, -1), v_) for k_, v_ in offs.items()},
             **{k_: (bund_b.reshape(NDEV, -1), v_) for k_, v_ in offs_b.items()}}
    r13, r2 = [(r13_0a, r13_0b), (r13_1,)], [(r2_0a, r2_0b), (r2_1,)]

    def piece(name):
        b_, (o_, n_) = bunds[name]
        return b_[:, o_:o_ + n_]

    loss = jnp.sum(piece("loss")[:, 0])

    dl_all = lax.dynamic_slice_in_dim(piece("dmod_l"), me * wa, wa, axis=1)
    dc_all = lax.dynamic_slice_in_dim(piece("dmod_c"), me * wa, wa, axis=1)
    g_wada, gc_part = _ada_bwd(c16, w_ada[0], dl_all, dc_all)
    gc_all = _exchange([(gc_part, GATHER)], "exchange_cctx")[0]

    parts = {
        "c_ctx": gc_all.reshape(NDEV, d),
        "norm_g": lax.dynamic_slice_in_dim(piece("norm_g").reshape(NDEV, 3, d), me * (d // NDEV), d // NDEV,
                                           axis=2).reshape((NDEV,) + norm_g.shape),
        "w_ada": g_wada.reshape((1,) + w_ada.shape),
        "b_ada": jnp.concatenate([piece("dmod_l"), piece("dmod_c")], axis=0).reshape((2 * NDEV,) + b_ada.shape),
        "w_in": r_in.reshape((NDEV,) + w_in.shape),
        "s5_w_glu": r_glu.reshape((NDEV,) + s5_w_glu.shape),
        "w_out": r_out.reshape((NDEV,) + w_out.shape),
    }
    for k_ in ("pool_w", "pool_scale", "s5_a_re", "s5_a_im", "s5_log_dt", "s5_b_re", "s5_b_im", "s5_c_re", "s5_c_im",
               "s5_d", "final_g"):
        parts[k_] = piece(k_).reshape((NDEV,) + weights[k_].shape)

    grads, deltas, new_m, new_v = [], [], [], []
    ffn_parts = {"ffn_w1": (r13, 0), "ffn_w3": (r13, 1), "ffn_w2": (r2, None)}
    for k_ in weights:
        if k_ in ffn_parts:
            g_, d_, m_, v_ = _adamw_ffn(*ffn_parts[k_], weights[k_], mom_m[k_], mom_v[k_], "adamw_" + k_)
        else:
            g_, d_, m_, v_ = _adamw(parts[k_], weights[k_], mom_m[k_], mom_v[k_], "adamw_" + k_)
        grads.append(g_)
        deltas.append(d_)
        new_m.append(m_)
        new_v.append(v_)
    return (loss, grad_x, *grads, *deltas, *new_m, *new_v)
```

```python
import functools
import math

import numpy as np
import jax
import jax.numpy as jnp
from jax import lax
from jax.experimental import pallas as pl
from jax.experimental.pallas import tpu as pltpu

F32 = jnp.float32
BF16 = jnp.bfloat16
AXES = ("x", "y", "c")
NDEV = 8
EPS = 1e-6
TM = 512
TML = 512
TC = 256
SEG = TC // 8
NTAB = SEG + 16
S5_UNROLL = True
GRID_W = 64
POOL_WINDOWS = (2, 4, 8, 16)
LANE = 128
FPAD = 384
VMEM_LIMIT = 56 * 1024 * 1024
ADAM_BLOCK_BYTES = 512 * 1024
ADAM_LR, ADAM_B1, ADAM_B2, ADAM_EPS, ADAM_WD, ADAM_STEP = 0.001, 0.9, 0.999, 1e-08, 0.01, 10


def _raw_call(body, kw):
    return pl.pallas_call(body, **kw)


def _pcall(body, xchg=None, edges=None, **kw):
    extra = ()
    if xchg:
        arrs, kinds = [a for a, _ in xchg], [k for _, k in xchg]
        n = len(arrs)
        n_in, n_out, n_scr = len(kw["in_specs"]), len(kw["out_shape"]), len(kw.get("scratch_shapes", ()))
        inner = body

        def hosted(*refs):
            a, b = n_in, n_in + n
            c_, d_ = b + n_out, b + n_out + n
            e = d_ + n_scr
            first, mid, last = edges()

            @pl.when(first)
            def _():
                _xchg_start(refs[a:b], refs[c_:d_], refs[e:], kinds)

            inner(*refs[:a], *refs[b:c_], *refs[d_:e])

            if any(k == GATHER_REL for k in kinds):
                @pl.when(mid)
                def _():
                    _xchg_relay(refs[a:b], refs[c_:d_], refs[e:], kinds)

            @pl.when(last)
            def _():
                _xchg_finish(refs[a:b], refs[c_:d_], refs[e:], kinds)

        body = hosted
        any_spec = pl.BlockSpec(memory_space=pl.ANY)
        kw = dict(kw, in_specs=list(kw["in_specs"]) + [any_spec] * n,
                  out_shape=tuple(kw["out_shape"]) + _xchg_shapes(arrs, kinds),
                  out_specs=tuple(kw["out_specs"]) + (any_spec,) * n,
                  scratch_shapes=list(kw.get("scratch_shapes", ())) + _xchg_sems(n))
        extra = tuple(arrs)
    call = _raw_call(body, kw)
    return (lambda *args: call(*args, *extra)) if extra else call


def _params(sem):
    return pltpu.CompilerParams(dimension_semantics=sem, vmem_limit_bytes=VMEM_LIMIT)


def _dot(a, b):
    return jnp.dot(a, b, preferred_element_type=F32)


def _dot_nt(a, b):
    return lax.dot_general(a, b, (((1,), (1,)), ((), ())), preferred_element_type=F32)


def _dot_tn(a, b):
    return lax.dot_general(a, b, (((0,), (0,)), ((), ())), preferred_element_type=F32)


def _sigmoid(x):
    return 1.0 / (1.0 + jnp.exp(-x))


def _colsum(a):
    return jnp.sum(a, axis=0, keepdims=True)


def _resident(shape):
    nd = len(shape)
    return pl.BlockSpec(shape, lambda *_: (0,) * nd, pipeline_mode=pl.Buffered(1))


def _const(shape):
    nd = len(shape)
    return pl.BlockSpec(shape, lambda *_: (0,) * nd)


GATHER = "gather"
GATHER_REL = "gather_rel"
A2A = "a2a"
A2A_REL = "a2a_rel"
A2A_SAME = "a2a_same"
A2A_OTHER = "a2a_other"


def _exchange(items, name):
    arrs, kinds = [a for a, _ in items], [k for _, k in items]
    n = len(arrs)

    def body(*refs):
        _xchg_start(refs[:n], refs[n:2 * n], refs[2 * n:], kinds)
        _xchg_relay(refs[:n], refs[n:2 * n], refs[2 * n:], kinds)
        _xchg_finish(refs[:n], refs[n:2 * n], refs[2 * n:], kinds)

    any_spec = pl.BlockSpec(memory_space=pl.ANY)
    outs = _pcall(
        body, name=name, out_shape=_xchg_shapes(arrs, kinds), in_specs=[any_spec] * n, out_specs=[any_spec] * n,
        scratch_shapes=_xchg_sems(n),
    )(*arrs)
    return list(outs)


def _edges1(n0):
    return lambda: (pl.program_id(0) == 0, pl.program_id(0) == (7 * n0) // 8, pl.program_id(0) == n0 - 1)


def _edges2(n0, n1):
    def edges():
        step = pl.program_id(0) * n1 + pl.program_id(1)
        return step == 0, step == (7 * n0 * n1) // 8, step == n0 * n1 - 1
    return edges


def _xchg_shapes(arrs, kinds):
    return tuple(jax.ShapeDtypeStruct(((NDEV,) if k in (GATHER, GATHER_REL) else ()) + tuple(a.shape), a.dtype)
                 for a, k in zip(arrs, kinds))


def _xchg_sems(n):
    return [pltpu.SemaphoreType.DMA((n * (NDEV - 1),)), pltpu.SemaphoreType.DMA((n * (NDEV - 1),)),
            pltpu.SemaphoreType.DMA((n,))]


def _xchg_plan(in_refs, out_refs, sems, kinds):
    send_sems, recv_sems, loc_sems = sems
    x, y, c = (lax.axis_index(a) for a in AXES)
    my_abs, my_chip = 4 * x + 2 * y + c, 2 * x + y

    def peer(p):
        px = 1 - x if p & 4 else x
        py = 1 - y if p & 2 else y
        pc = 1 - c if p & 1 else c
        return (px, py, pc), 4 * px + 2 * py + pc, 2 * px + py

    starts, relays, recvs = [], [], []
    for k, kind in enumerate(kinds):
        src, dst = in_refs[k], out_refs[k]

        def remote(src_ref, row, p, pair, dst=dst, k=k):
            dev, sem = peer(p)[0], k * (NDEV - 1) + pair - 1
            return lambda: pltpu.make_async_remote_copy(
                src_ref=src_ref, dst_ref=dst.at[row], send_sem=send_sems.at[sem], recv_sem=recv_sems.at[sem],
                device_id=dev, device_id_type=pl.DeviceIdType.MESH)

        def local(src_ref, row, dst=dst, k=k):
            return lambda: pltpu.make_async_copy(src_ref, dst.at[row], loc_sems.at[k])

        if kind == GATHER:
            for p in range(1, NDEV):
                starts.append((remote(src, my_abs, p, p), False))
                recvs.append(remote(src, peer(p)[1], p, p))
            starts.append((local(src, my_abs), True))
        elif kind == GATHER_REL:
            for p in (1, 4, 2, 6):
                starts.append((remote(src, 4 * (p & 1) + my_chip, p, p), False))
            for q in (4, 2, 6):
                pchip = peer(q)[2]
                relays.append((remote(src, pchip, q, q), remote(dst.at[pchip], 4 + pchip, 1, q | 1)))
                recvs.append(remote(src, 4 + pchip, 1, q | 1))
            recvs.append(remote(src, 4 + my_chip, 1, 1))
            starts.append((local(src, my_chip), True))
        elif kind in (A2A, A2A_REL):
            for p in range(1, NDEV):
                _, pabs, pchip = peer(p)
                theirs, mine = (pabs, my_abs) if kind == A2A else (4 * (p & 1) + pchip, 4 * (p & 1) + my_chip)
                starts.append((remote(src.at[theirs], mine, p, p), False))
                recvs.append(remote(src.at[theirs], theirs, p, p))
            own = my_abs if kind == A2A else my_chip
            starts.append((local(src.at[own], own), True))
        else:
            for p in ((4, 2, 6) if kind == A2A_SAME else (1, 5, 3, 7)):
                pchip = peer(p)[2]
                starts.append((remote(src.at[pchip], my_chip, p, p), False))
                recvs.append(remote(src.at[pchip], pchip, p, p))
            if kind == A2A_SAME:
                starts.append((local(src.at[my_chip], my_chip), True))
    return starts, relays, recvs


def _xchg_start(in_refs, out_refs, sems, kinds):
    for make, _ in _xchg_plan(in_refs, out_refs, sems, kinds)[0]:
        make().start()


def _xchg_relay(in_refs, out_refs, sems, kinds):
    for arrival, forward in _xchg_plan(in_refs, out_refs, sems, kinds)[1]:
        arrival().wait_recv()
        forward().start()


def _xchg_finish(in_refs, out_refs, sems, kinds):
    starts, relays, recvs = _xchg_plan(in_refs, out_refs, sems, kinds)
    for make in recvs:
        make().wait_recv()
    for make, is_local in starts:
        if is_local:
            make().wait()
        else:
            make().wait_send()
    for _, forward in relays:
        forward().wait_send()


def _ada_fwd(c16, w_blk, b_blk):
    def body(c_ref, w_ref, b_ref, o_ref):
        cc = c_ref[...]
        s = (cc * _sigmoid(cc)).astype(BF16)
        o_ref[...] = _dot(s, w_ref[...].astype(BF16)) + b_ref[...]

    return _pcall(body, name="ada_fwd", out_shape=jax.ShapeDtypeStruct((16, w_blk.shape[1]), F32),
                  compiler_params=_params(None))(c16, w_blk, b_blk)


def _ada_bwd(c16, w_blk, dl, dc):
    d, w = w_blk.shape

    def body(c_ref, w_ref, dl_ref, dc_ref, gw_ref, gc_ref):
        cc = c_ref[...]
        sg = _sigmoid(cc)
        s = (cc * sg).astype(BF16)
        dctot = _colsum(dc_ref[...])
        dm = jnp.concatenate([dl_ref[...], jnp.broadcast_to(dctot, (8, w))], axis=0)
        rows = lax.broadcasted_iota(jnp.int32, (16, w), 0)
        dm = jnp.where(rows <= 8, dm, 0.0).astype(BF16)
        gw_ref[...] = _dot_tn(s, dm)
        t = _dot_nt(jnp.broadcast_to(dctot, (8, w)).astype(BF16), w_ref[...].astype(BF16))[0:1]
        c8, s8 = cc[8:9], sg[8:9]
        gc_ref[...] = t * (s8 * (1.0 + c8 * (1.0 - s8)))

    return _pcall(body, name="ada_bwd",
                  out_shape=(jax.ShapeDtypeStruct((d, w), F32), jax.ShapeDtypeStruct((1, d), F32)),
                  compiler_params=_params(None))(c16, w_blk, dl, dc)


def _norm_fwd(x, g, shift, scale):
    rstd = lax.rsqrt(jnp.mean(x * x, axis=-1, keepdims=True) + EPS)
    xhat = x * rstd
    xn = xhat * g
    return rstd, xhat, xn, xn * (1.0 + scale) + shift


def _norm_bwd(dh, rstd, xhat, xn, g, scale):
    dxn = dh * (1.0 + scale)
    dxhat = dxn * g
    dx = rstd * (dxhat - xhat * jnp.mean(dxhat * xhat, axis=-1, keepdims=True))
    return dx, _colsum(dh), _colsum(dh * xn), _colsum(dxn * xhat)


def _mod_spec(d, has_ctx):
    if has_ctx:
        return pl.BlockSpec((1, 9, d), lambda *ids: (jnp.minimum(ids[-1], 1), 0, 0))
    return pl.BlockSpec((1, 9, d), lambda *ids: (1, 0, 0))


def _row_specs(nx, tm, d):
    if nx == 1:
        return [pl.BlockSpec((tm, d), lambda i: (i, 0))]
    return [pl.BlockSpec((tm, d), lambda i: (0, 0)), pl.BlockSpec((tm, d), lambda i: (jnp.maximum(i - 1, 0), 0))]


def _rows(refs):
    if len(refs) == 1:
        return refs[0][...]
    return jnp.where(pl.program_id(0) == 0, refs[0][...], refs[1][...])


def _tile(has_ctx):
    return TM if has_ctx else TML


def _tile_of(nt, has_ctx):
    if has_ctx:
        return lambda i: jnp.where(i == 0, nt - 1, i - 1)
    return lambda i: i


def _dmod_spec(nrow, d, has_ctx):
    if has_ctx:
        return pl.BlockSpec((1, nrow, d), lambda i: (jnp.minimum(i, 1), 0, 0))
    return pl.BlockSpec((1, nrow, d), lambda i: (0, 0, 0))


def _ffn_fwd(x, modtab, g, w13, w2, k0, has_ctx, name, xchg=None, head=None):
    t, d = sum(a.shape[0] for a in x), x[0].shape[1]
    nb, _, w = w13.shape
    fp = w // 2
    tm = _tile(has_ctx)
    nx = len(x)

    nhead = 0 if head is None else 2

    def body(*refs):
        mod_ref, g_ref, w13_ref, w2_ref = refs[nx:nx + 4]
        xo_ref, h_ref, ab_ref, o_ref = refs[nx + 4 + nhead:nx + 8 + nhead]
        xx = _rows(refs[:nx])
        shift, scale, gate = mod_ref[0, k0:k0 + 1, :], mod_ref[0, k0 + 1:k0 + 2, :], mod_ref[0, k0 + 2:k0 + 3, :]
        _, _, _, h = _norm_fwd(xx, g_ref[...], shift, scale)
        hb = h.astype(BF16)
        h_ref[...] = hb
        acc = jnp.zeros((tm, d), F32)
        for p in range(nb // 2):
            zs = []
            for q in range(2):
                blk = 2 * p + q
                ab = _dot(hb, w13_ref[blk])
                ab_ref[:, blk * w:(blk + 1) * w] = ab.astype(BF16)
                a, b = ab[:, :fp], ab[:, fp:]
                zs.append((a * _sigmoid(a) * b).astype(BF16))
            acc = acc + _dot(jnp.concatenate(zs, axis=1), w2_ref[p])
        o_ref[...] = acc.astype(BF16)
        xo = xx + (0.5 * gate) * acc
        if head is None:
            xo_ref[...] = xo
        else:
            tgt_ref, fg_ref = refs[nx + 4:nx + 6]
            loss_ref, dfg_ref = refs[nx + 8 + nhead:]

            @pl.when(pl.program_id(0) == 0)
            def _():
                loss_ref[...] = jnp.zeros_like(loss_ref)
                dfg_ref[...] = jnp.zeros_like(dfg_ref)

            xo_ref[...] = _loss_head(xo, tgt_ref[...], fg_ref[...], loss_ref, dfg_ref)

    tile = _tile_of(t // tm, has_ctx)
    row = lambda i: (tile(i), 0)
    head_in = [] if head is None else [pl.BlockSpec((tm, d), row), _const((1, d))]
    head_shape = () if head is None else (jax.ShapeDtypeStruct((1, LANE), F32), jax.ShapeDtypeStruct((1, d), F32))
    head_out = () if head is None else (_const((1, LANE)), _const((1, d)))
    return _pcall(
        body, name=name, grid=(t // tm,), xchg=xchg, edges=_edges1(t // tm),
        out_shape=(jax.ShapeDtypeStruct((t, d), F32), jax.ShapeDtypeStruct((t, d), BF16),
                   jax.ShapeDtypeStruct((t, nb * w), BF16), jax.ShapeDtypeStruct((t, d), BF16)) + head_shape,
        in_specs=_row_specs(nx, tm, d) + [_mod_spec(d, has_ctx), _const((1, d)),
                                          _resident(w13.shape), _resident(w2.shape)] + head_in,
        out_specs=(pl.BlockSpec((tm, d), row), pl.BlockSpec((tm, d), row), pl.BlockSpec((tm, nb * w), row),
                   pl.BlockSpec((tm, d), row)) + head_out,
        compiler_params=_params(("arbitrary",)),
    )(*x, modtab, g, w13, w2, *(head or ()))


def _ffn_bwd(dy, modtab, h, ab, w13, w2, k0, has_ctx, name, pair0=0, npair=None, xchg=None):
    t, d = dy.shape
    _, _, w = w13.shape
    fp = w // 2
    npair = w13.shape[0] // 2 if npair is None else npair
    nb = 2 * npair
    tm = _tile(has_ctx)
    nt = t // tm

    def body(dy_ref, mod_ref, h_ref, ab_ref, w13_ref, w2_ref, dh_ref, d13_ref, d2_ref, acc13, acc2):
        i = pl.program_id(1)

        @pl.when(i == 0)
        def _():
            acc13[...] = jnp.zeros_like(acc13)
            acc2[...] = jnp.zeros_like(acc2)

        gate = mod_ref[0, k0 + 2:k0 + 3, :]
        do = (dy_ref[...] * (0.5 * gate)).astype(BF16)
        dz = _dot_nt(do, w2_ref[0])
        hb = h_ref[...]
        dh = jnp.zeros((tm, d), F32)
        zs = []
        for q in range(2):
            ab = ab_ref[:, q * w:(q + 1) * w].astype(F32)
            a, b = ab[:, :fp], ab[:, fp:]
            sg = _sigmoid(a)
            sa = a * sg
            dzq = dz[:, q * fp:(q + 1) * fp]
            da = dzq * b * (sg * (1.0 + a * (1.0 - sg)))
            db = dzq * sa
            dab = jnp.concatenate([da, db], axis=1).astype(BF16)
            dh = dh + _dot_nt(dab, w13_ref[q])
            acc13[q] += _dot_tn(hb, dab)
            zs.append((sa * b).astype(BF16))
        acc2[...] += _dot_tn(jnp.concatenate(zs, axis=1), do)
        dh_ref[0] = dh.astype(BF16)

        @pl.when(i == nt - 1)
        def _():
            d13_ref[...] = acc13[...].astype(BF16)
            d2_ref[0] = acc2[...].astype(BF16)

    mod_spec = _mod_spec(d, has_ctx)
    tile = _tile_of(nt, has_ctx)
    return _pcall(
        body, name=name, grid=(npair, nt), xchg=xchg, edges=_edges2(npair, nt),
        out_shape=(jax.ShapeDtypeStruct((npair, t, d), BF16), jax.ShapeDtypeStruct((nb, d, w), BF16),
                   jax.ShapeDtypeStruct((npair, 2 * fp, d), BF16)),
        in_specs=[pl.BlockSpec((tm, d), lambda p, i: (tile(i), 0)), mod_spec,
                  pl.BlockSpec((tm, d), lambda p, i: (tile(i), 0)),
                  pl.BlockSpec((tm, 2 * w), lambda p, i: (tile(i), pair0 + p)),
                  pl.BlockSpec((2, d, w), lambda p, i: (pair0 + p, 0, 0)),
                  pl.BlockSpec((1, 2 * fp, d), lambda p, i: (pair0 + p, 0, 0))],
        out_specs=(pl.BlockSpec((1, tm, d), lambda p, i: (p, tile(i), 0)),
                   pl.BlockSpec((2, d, w), lambda p, i: (p, 0, 0)),
                   pl.BlockSpec((1, 2 * fp, d), lambda p, i: (p, 0, 0))),
        scratch_shapes=[pltpu.VMEM((2, d, w), F32), pltpu.VMEM((2 * fp, d), F32)],
        compiler_params=_params(("arbitrary", "arbitrary")),
    )(dy, modtab, h, ab, w13, w2)


def _ffn_bwd_norm(dy, dhps, x, o, modtab, g, k0, has_ctx, name, xchg=None):
    t, d = dy.shape
    ngrp = 2 if has_ctx else 1
    tm = _tile(has_ctx)
    ndh, nx = len(dhps), len(x)
    lat0 = ngrp - 1

    def body(dy_ref, *rest):
        dhp_refs, x_refs = rest[:ndh], rest[ndh:ndh + nx]
        o_ref, mod_ref, g_ref, dx_ref, dmod_ref, dg_ref = rest[ndh + nx:]
        i = pl.program_id(0)

        @pl.when(i == 0)
        def _():
            dg_ref[...] = jnp.zeros_like(dg_ref)

        @pl.when((i == 0) | (i == ngrp - 1))
        def _():
            dmod_ref[...] = jnp.zeros_like(dmod_ref)

        dh = None
        for ref in dhp_refs:
            for p in range(ref.shape[0]):
                dh = ref[p].astype(F32) if dh is None else dh + ref[p].astype(F32)
        scale = mod_ref[0, k0 + 1:k0 + 2, :]
        gg = g_ref[...]
        rstd, xhat, xn, _ = _norm_fwd(_rows(x_refs), gg, 0.0, scale)
        dxn, dshift, dscale, dg = _norm_bwd(dh, rstd, xhat, xn, gg, scale)
        dyv = dy_ref[...]
        dx_ref[...] = dyv + dxn
        dmod_ref[0, 0:1, :] += dshift
        dmod_ref[0, 1:2, :] += dscale
        dmod_ref[0, 2:3, :] += _colsum(0.5 * dyv * o_ref[...].astype(F32))
        dg_ref[...] += dg

    tile = _tile_of(t // tm, has_ctx)
    row = lambda i: (tile(i), 0)
    return _pcall(
        body, name=name, grid=(t // tm,), xchg=xchg, edges=_edges1(t // tm),
        out_shape=(jax.ShapeDtypeStruct((t - lat0 * tm, d), F32), jax.ShapeDtypeStruct((ngrp, 3, d), F32),
                   jax.ShapeDtypeStruct((1, d), F32)),
        in_specs=[pl.BlockSpec((tm, d), row)]
        + [pl.BlockSpec((a.shape[0], tm, d), lambda i: (0, tile(i), 0)) for a in dhps]
        + _row_specs(nx, tm, d) + [pl.BlockSpec((tm, d), row), _mod_spec(d, has_ctx), _const((1, d))],
        out_specs=(pl.BlockSpec((tm, d), lambda i: (jnp.maximum(i - lat0, 0), 0)), _dmod_spec(3, d, has_ctx),
                   _const((1, d))),
        compiler_params=_params(("arbitrary",)),
    )(dy, *dhps, *x, o, modtab, g)


def _mix_in_fwd(x1, modtab, g, w_in):
    t, d = x1.shape
    hm = w_in.shape[1] // 2

    def body(x_ref, mod_ref, g_ref, w_ref, h_ref, up_ref, us_ref):
        _, _, _, h = _norm_fwd(x_ref[...], g_ref[...], mod_ref[0, 3:4, :], mod_ref[0, 4:5, :])
        hb = h.astype(BF16)
        h_ref[...] = hb
        u = _dot(hb, w_ref[...])
        up_ref[...] = u[:, :hm]
        us_ref[...] = u[:, hm:]

    tile = _tile_of(t // TM, True)
    row = lambda i: (tile(i), 0)
    return _pcall(
        body, name="mix_in_fwd", grid=(t // TM,),
        out_shape=(jax.ShapeDtypeStruct((t, d), BF16), jax.ShapeDtypeStruct((t, hm), F32),
                   jax.ShapeDtypeStruct((t, hm), F32)),
        in_specs=[pl.BlockSpec((TM, d), row), _mod_spec(d, True), _const((1, d)), _resident(w_in.shape)],
        out_specs=(pl.BlockSpec((TM, d), row), pl.BlockSpec((TM, hm), row), pl.BlockSpec((TM, hm), row)),
        compiler_params=_params(("arbitrary",)),
    )(x1, modtab, g, w_in)


def _mix_in_bwd(du_pool, du_f, du_b, dx2, h1, x1, modtab, g, w_in, lc):
    t, d = x1.shape
    m = w_in.shape[1]
    hm = m // 2
    nt = t // TM

    def body(dup_ref, duf_ref, dub_ref, dx2_ref, h_ref, x_ref, mod_ref, g_ref, w_ref, dx_ref, dw_ref, dmod_ref, dg_ref, acc):
        i = pl.program_id(0)

        @pl.when(i == 0)
        def _():
            acc[...] = jnp.zeros_like(acc)
            dg_ref[...] = jnp.zeros_like(dg_ref)

        @pl.when(i <= 1)
        def _():
            dmod_ref[...] = jnp.zeros_like(dmod_ref)

        lat = (i > 0).astype(F32)
        valid = (i > 0) | (lax.broadcasted_iota(jnp.int32, (TM, 1), 0) < lc)
        du_s5 = jnp.where(valid, duf_ref[...] + dub_ref[...], 0.0)
        du = jnp.concatenate([dup_ref[...] * lat, du_s5], axis=1).astype(BF16)
        dh = _dot_nt(du, w_ref[...])
        acc[...] += _dot_tn(h_ref[...], du)
        scale = mod_ref[0, 4:5, :]
        gg = g_ref[...]
        rstd, xhat, xn, _ = _norm_fwd(x_ref[...], gg, 0.0, scale)
        dxn, dshift, dscale, dg = _norm_bwd(dh, rstd, xhat, xn, gg, scale)
        dx_ref[...] = dx2_ref[...] * lat + dxn
        dmod_ref[0, 0:1, :] += dshift
        dmod_ref[0, 1:2, :] += dscale
        dg_ref[...] += dg

        @pl.when(i == nt - 1)
        def _():
            dw_ref[...] = acc[...].astype(BF16)

    tile = _tile_of(nt, True)
    row = lambda i: (tile(i), 0)
    lrow = lambda i: (jnp.maximum(i - 1, 0), 0)
    return _pcall(
        body, name="mix_in_bwd", grid=(nt,),
        out_shape=(jax.ShapeDtypeStruct((t, d), F32), jax.ShapeDtypeStruct((d, m), BF16),
                   jax.ShapeDtypeStruct((2, 2, d), F32), jax.ShapeDtypeStruct((1, d), F32)),
        in_specs=[pl.BlockSpec((TM, hm), lrow), pl.BlockSpec((TM, hm), row), pl.BlockSpec((TM, hm), row),
                  pl.BlockSpec((TM, d), lrow), pl.BlockSpec((TM, d), row), pl.BlockSpec((TM, d), row),
                  _mod_spec(d, True), _const((1, d)), _resident(w_in.shape)],
        out_specs=(pl.BlockSpec((TM, d), row), _const((d, m)), _dmod_spec(2, d, True), _const((1, d))),
        scratch_shapes=[pltpu.VMEM((d, m), F32)],
        compiler_params=_params(("arbitrary",)),
    )(du_pool, du_f, du_b, dx2, h1, x1, modtab, g, w_in)


def _pool(v, rows, transpose, name):
    l, n = rows * GRID_W, GRID_W * LANE
    ngrp = v.shape[1] // LANE
    nchunk = 4
    cw = n // nchunk

    def rowsum(val, lo, hi):
        ri = lax.broadcasted_iota(jnp.int32, (rows, rows), 0)
        ci = lax.broadcasted_iota(jnp.int32, (rows, rows), 1)
        sel = (((ci - ri) >= -lo) & ((ci - ri) <= hi)).astype(BF16)
        v1 = val.astype(BF16)
        r1 = val - v1.astype(F32)
        v2 = r1.astype(BF16)
        v3 = (r1 - v2.astype(F32)).astype(BF16)
        return _dot(sel, v1) + _dot(sel, v2) + _dot(sel, v3)

    def one(v_ref, o_ref, g_scr, r_scr, a_scr, win):
        for c in range(GRID_W):
            g_scr[:, c * LANE:(c + 1) * LANE] = v_ref[pl.ds(c, rows, stride=GRID_W), :]
        lo = win // 2
        hi = win - 1 - lo
        rlo, rhi = (hi, lo) if transpose else (lo, hi)
        ridx = lax.broadcasted_iota(jnp.int32, (rows, 1), 0)
        cnt_r = (jnp.minimum(ridx + hi + 1, rows) - jnp.maximum(ridx - lo, 0)).astype(F32)
        cidx = lax.broadcasted_iota(jnp.int32, (1, n), 1) // LANE
        cnt_c = (jnp.minimum(cidx + hi + 1, GRID_W) - jnp.maximum(cidx - lo, 0)).astype(F32)
        if not transpose:
            for k in range(nchunk):
                sl = slice(k * cw, (k + 1) * cw)
                r_scr[:, sl] = rowsum(g_scr[:, sl], rlo, rhi) / cnt_r
        else:
            r_scr[...] = g_scr[...] / cnt_c
        a_scr[...] = r_scr[...]
        for j in range(-rlo, rhi + 1):
            if j == 0:
                continue
            c0, c1 = max(0, -j), min(GRID_W, GRID_W - j)
            a_scr[:, c0 * LANE:c1 * LANE] += r_scr[:, (c0 + j) * LANE:(c1 + j) * LANE]
        if not transpose:
            r_scr[...] = a_scr[...] / cnt_c - g_scr[...]
        else:
            for k in range(nchunk):
                sl = slice(k * cw, (k + 1) * cw)
                r_scr[:, sl] = rowsum(a_scr[:, sl] / cnt_r, rlo, rhi) - g_scr[:, sl]
        for c in range(GRID_W):
            o_ref[pl.ds(c, rows, stride=GRID_W), :] = r_scr[:, c * LANE:(c + 1) * LANE]

    def body(v_ref, o_ref, g_scr, r_scr, a_scr):
        grp = pl.program_id(0)
        for k, win in enumerate(POOL_WINDOWS):
            @pl.when(grp == k)
            def _(win=win):
                one(v_ref, o_ref, g_scr, r_scr, a_scr, win)

    spec = pl.BlockSpec((l, LANE), lambda k: (0, k))
    return _pcall(
        body, name=name, grid=(ngrp,), out_shape=jax.ShapeDtypeStruct((l, ngrp * LANE), F32),
        in_specs=[spec], out_specs=spec,
        scratch_shapes=[pltpu.VMEM((rows, n), F32), pltpu.VMEM((rows, n), F32), pltpu.VMEM((rows, n), F32)],
        compiler_params=_params(("arbitrary",)),
    )(v)


def _cmul(ar, ai, br, bi):
    return ar * br - ai * bi, ar * bi + ai * br


def _s5_prep(a_re, a_im, log_dt, c_re, c_im, kvec):
    q, nc, p = c_re.shape

    def body(ar_ref, ai_ref, ldt_ref, cr_ref, ci_ref, k_ref, cpr_ref, cpi_ref, tr_ref, ti_ref):
        lr, li = ar_ref[...], ai_ref[...]
        dt = jnp.exp(ldt_ref[...])
        kk = k_ref[...]
        mag = jnp.exp(kk * (lr * dt))
        ph = kk * (li * dt)
        tr_ref[...] = mag * jnp.cos(ph)
        ti_ref[...] = mag * jnp.sin(ph)
        m1 = jnp.exp(lr * dt)
        abr, abi = m1 * jnp.cos(li * dt), m1 * jnp.sin(li * dt)
        den = lr * lr + li * li
        xr, xi = abr - 1.0, abi
        cfr, cfi = (xr * lr + xi * li) / den, (xi * lr - xr * li) / den
        pr, pi_ = _cmul(cr_ref[...], ci_ref[...], cfr, cfi)
        cpr_ref[...] = pr
        cpi_ref[...] = pi_

    return _pcall(
        body, name="s5_prep",
        out_shape=(jax.ShapeDtypeStruct((q, nc, p), F32), jax.ShapeDtypeStruct((q, nc, p), F32),
                   jax.ShapeDtypeStruct((q, NTAB, p), F32), jax.ShapeDtypeStruct((q, NTAB, p), F32)),
        compiler_params=_params(None),
    )(a_re, a_im, log_dt, c_re, c_im, kvec)


def _s5_param_bwd(a_re, a_im, log_dt, c_re, c_im, dab_r, dab_i, dcp_r, dcp_i):
    q, nc, p = c_re.shape

    def body(ar_ref, ai_ref, ldt_ref, cr_ref, ci_ref, dar_ref, dai_ref, dcr_ref, dci_ref,
             gar_ref, gai_ref, gdt_ref, gcr_ref, gci_ref):
        lr, li = ar_ref[...], ai_ref[...]
        dt = jnp.exp(ldt_ref[...])
        m1 = jnp.exp(lr * dt)
        abr, abi = m1 * jnp.cos(li * dt), m1 * jnp.sin(li * dt)
        den = lr * lr + li * li
        xr, xi = abr - 1.0, abi
        cfr, cfi = (xr * lr + xi * li) / den, (xi * lr - xr * li) / den
        cr, ci = cr_ref[...], ci_ref[...]
        dcr, dci = dcr_ref[...], dci_ref[...]
        gcr, gci = _cmul(dcr, dci, cfr, -cfi)
        gcr_ref[...] = gcr
        gci_ref[...] = gci
        t_r, t_i = _cmul(cr, -ci, dcr, dci)
        dcf_r = jnp.sum(t_r, axis=1, keepdims=True)
        dcf_i = jnp.sum(t_i, axis=1, keepdims=True)
        ilr, ili = lr / den, -li / den
        u_r, u_i = _cmul(dcf_r, dcf_i, ilr, -ili)
        dab_r_, dab_i_ = dar_ref[...] + u_r, dai_ref[...] + u_i
        v_r, v_i = _cmul(dab_r_, dab_i_, abr, -abi)
        cl_r, cl_i = _cmul(cfr, cfi, ilr, ili)
        w_r, w_i = _cmul(dcf_r, dcf_i, cl_r, -cl_i)
        gar_ref[...] = v_r * dt - w_r
        gai_ref[...] = v_i * dt - w_i
        la_r, la_i = _cmul(lr, li, abr, abi)
        ddt = la_r * dab_r_ + la_i * dab_i_
        gdt_ref[...] = dt * jnp.sum(ddt, axis=2, keepdims=True)

    return _pcall(
        body, name="s5_param_bwd",
        out_shape=(jax.ShapeDtypeStruct((q, 1, p), F32), jax.ShapeDtypeStruct((q, 1, p), F32),
                   jax.ShapeDtypeStruct((q, 1, p), F32), jax.ShapeDtypeStruct((q, nc, p), F32),
                   jax.ShapeDtypeStruct((q, nc, p), F32)),
        compiler_params=_params(None),
    )(a_re, a_im, log_dt, c_re, c_im, dab_r, dab_i, dcp_r, dcp_i)


def _bcast8(row, c):
    return jnp.broadcast_to(row, (8, c))


def _slab(j):
    return j * 8 if isinstance(j, int) else pl.multiple_of(j * 8, 8)


def _scan_head(xr, xi, tre_ref, tim_ref, car_r, car_i, desc, conj, cs):
    sg = -1.0 if conj else 1.0
    ar = _bcast8(tre_ref[0:1, :], cs)
    ai = sg * _bcast8(tim_ref[0:1, :], cs)

    def p1(jj, carry):
        hr, hi = carry
        off = _slab(SEG - 1 - jj if desc else jj)
        pr, pi_ = _cmul(ar, ai, hr, hi)
        nr = pr + xr[pl.ds(off, 8), :]
        ni = pi_ + xi[pl.ds(off, 8), :]
        xr[pl.ds(off, 8), :] = nr
        xi[pl.ds(off, 8), :] = ni
        return nr, ni

    zero = jnp.zeros((8, cs), F32)
    ir, ii = lax.fori_loop(0, SEG, p1, (zero, zero), unroll=S5_UNROLL)
    cin_r, cin_i = car_r[...], car_i[...]
    rows = lax.broadcasted_iota(jnp.int32, (8, cs), 0)
    for s, krow in ((1, SEG), (2, SEG + 1), (4, SEG + 3)):
        keep, sh = (rows < 8 - s, 8 - s) if desc else (rows >= s, s)
        sr = jnp.where(keep, pltpu.roll(ir, sh, 0), 0.0)
        si = jnp.where(keep, pltpu.roll(ii, sh, 0), 0.0)
        pr, pi_ = _cmul(tre_ref[krow:krow + 1, :], sg * tim_ref[krow:krow + 1, :], sr, si)
        ir, ii = ir + pr, ii + pi_
    q0 = SEG + 8 if desc else SEG
    pr, pi_ = _cmul(tre_ref[q0:q0 + 8, :], sg * tim_ref[q0:q0 + 8, :], cin_r, cin_i)
    fr, fi = ir + pr, ii + pi_
    keep, sh, edge = (rows < 7, 7, 0) if desc else (rows >= 1, 1, 7)
    cs_r = jnp.where(keep, pltpu.roll(fr, sh, 0), cin_r)
    cs_i = jnp.where(keep, pltpu.roll(fi, sh, 0), cin_i)
    car_r[...] = _bcast8(fr[edge:edge + 1, :], cs)
    car_i[...] = _bcast8(fi[edge:edge + 1, :], cs)
    return cs_r, cs_i, cin_r, cin_i


def _pow_row(tre_ref, tim_ref, j, desc, conj, cs):
    k = SEG - 1 - j if desc else j
    sg = -1.0 if conj else 1.0
    return _bcast8(tre_ref[pl.ds(k, 1), :], cs), sg * _bcast8(tim_ref[pl.ds(k, 1), :], cs)


def _to_segments(val, dst):
    for r in range(8):
        dst[pl.ds(r, SEG, stride=8), :] = val[r * SEG:(r + 1) * SEG, :]


def _from_segments(src):
    return jnp.concatenate([src[pl.ds(r, SEG, stride=8), :] for r in range(8)], axis=0)


def _s5_fwd(up, bre, bim, cmr, cmi, tre, tim, dskip, order, nproc, desc, with_skip, name):
    t, cu = up.shape
    nsb = bre.shape[0]
    cb = cu // nsb
    cs = bre.shape[2]
    nch = t // TC

    def body(u_ref, bre_ref, bim_ref, cmr_ref, cmi_ref, tre_ref, tim_ref, dsk_ref,
             y_ref, hr_ref, hi_ref, cinr_ref, cini_ref, car_r, car_i, seg_scr):
        i = pl.program_id(0)

        @pl.when(i == 0)
        def _():
            car_r[...] = jnp.zeros_like(car_r)
            car_i[...] = jnp.zeros_like(car_i)

        for sb in range(nsb):
            col, ch = pl.ds(sb * cs, cs), slice(sb * cb, (sb + 1) * cb)
            hr_v, hi_v = hr_ref.at[:, col], hi_ref.at[:, col]
            tre_v, tim_v = tre_ref.at[:, col], tim_ref.at[:, col]
            useg, yseg = seg_scr.at[0, sb], seg_scr.at[1, sb]
            _to_segments(u_ref[:, ch], useg)
            u = useg[...]
            ub = u.astype(BF16)
            hr_v[...] = _dot(ub, bre_ref[sb])
            hi_v[...] = _dot(ub, bim_ref[sb])
            cs_r, cs_i, cin_r, cin_i = _scan_head(hr_v, hi_v, tre_v, tim_v, car_r.at[sb], car_i.at[sb], desc, False, cs)
            cinr_ref[:, col] = cin_r
            cini_ref[:, col] = cin_i

            def p2(j, _, hr_v=hr_v, hi_v=hi_v, tre_v=tre_v, tim_v=tim_v, cs_r=cs_r, cs_i=cs_i):
                off = _slab(j)
                pw_r, pw_i = _pow_row(tre_v, tim_v, j, desc, False, cs)
                pr, pi_ = _cmul(pw_r, pw_i, cs_r, cs_i)
                hr_v[pl.ds(off, 8), :] = hr_v[pl.ds(off, 8), :] + pr
                hi_v[pl.ds(off, 8), :] = hi_v[pl.ds(off, 8), :] + pi_
                return 0

            lax.fori_loop(0, SEG, p2, 0, unroll=S5_UNROLL)
            y = _dot(hr_v[...].astype(BF16), cmr_ref[sb]) - _dot(hi_v[...].astype(BF16), cmi_ref[sb])
            if with_skip:
                y = y + dsk_ref[:, ch] * u
            yseg[...] = y
            y_ref[:, ch] = _from_segments(yseg)

    blk = lambda i: (order(i), 0)
    return _pcall(
        body, name=name, grid=(nproc,),
        out_shape=(jax.ShapeDtypeStruct((t, cu), F32), jax.ShapeDtypeStruct((t, nsb * cs), F32),
                   jax.ShapeDtypeStruct((t, nsb * cs), F32), jax.ShapeDtypeStruct((nch * 8, nsb * cs), F32),
                   jax.ShapeDtypeStruct((nch * 8, nsb * cs), F32)),
        in_specs=[pl.BlockSpec((TC, cu), blk), _const(bre.shape), _const(bim.shape), _const(cmr.shape),
                  _const(cmi.shape), _const(tre.shape), _const(tim.shape), _const(dskip.shape)],
        out_specs=(pl.BlockSpec((TC, cu), blk), pl.BlockSpec((TC, nsb * cs), blk), pl.BlockSpec((TC, nsb * cs), blk),
                   pl.BlockSpec((8, nsb * cs), blk), pl.BlockSpec((8, nsb * cs), blk)),
        scratch_shapes=[pltpu.VMEM((nsb, 8, cs), F32), pltpu.VMEM((nsb, 8, cs), F32),
                        pltpu.VMEM((2, nsb, TC, cb), F32)],
        compiler_params=_params(("arbitrary",)),
    )(up, bre, bim, cmr, cmi, tre, tim, dskip)


def _s5_bwd(dy_lat, up, hr, hi, cinr, cini, bre, bim, ctr, cti, tre, tim, dskip, order, nproc, desc, with_skip, name,
            xchg=None):
    t, cu = up.shape
    nsb = bre.shape[0]
    cb = cu // nsb
    cs = bre.shape[2]
    lch = dy_lat.shape[0] // TC
    adesc = not desc

    def body(dy_ref, u_ref, hr_ref, hi_ref, cinr_ref, cini_ref, bre_ref, bim_ref, ctr_ref, cti_ref, tre_ref, tim_ref,
             dsk_ref, du_ref, dar_ref, dai_ref, dcr_ref, dci_ref, dbr_ref, dbi_ref, dd_ref,
             car_r, car_i, mr_all, mi_all, acc_r, acc_i, seg_scr):
        i = pl.program_id(0)
        latent = (order(i) < lch).astype(F32)

        @pl.when(i == 0)
        def _():
            for ref in (car_r, car_i, acc_r, acc_i, dcr_ref, dci_ref, dbr_ref, dbi_ref, dd_ref):
                ref[...] = jnp.zeros_like(ref)

        rows = lax.broadcasted_iota(jnp.int32, (8, cs), 0)
        for sb in range(nsb):
            col, ch = pl.ds(sb * cs, cs), slice(sb * cb, (sb + 1) * cb)
            hr_v, hi_v = hr_ref.at[:, col], hi_ref.at[:, col]
            tre_v, tim_v = tre_ref.at[:, col], tim_ref.at[:, col]
            mr, mi = mr_all.at[sb % 2], mi_all.at[sb % 2]
            useg, dyseg, duseg = seg_scr.at[0, sb], seg_scr.at[1, sb], seg_scr.at[2, sb]
            _to_segments(dy_ref[:, ch] * latent, dyseg)
            _to_segments(u_ref[:, ch], useg)
            dy = dyseg[...]
            dyb = dy.astype(BF16)
            u = useg[...]
            ub = u.astype(BF16)
            mr[...] = _dot(dyb, ctr_ref[sb])
            mi[...] = -_dot(dyb, cti_ref[sb])
            cs_r, cs_i, _, _ = _scan_head(mr, mi, tre_v, tim_v, car_r.at[sb], car_i.at[sb], adesc, True, cs)

            def fix(j, hp_r, hp_i, acc, mr=mr, mi=mi, tre_v=tre_v, tim_v=tim_v, cs_r=cs_r, cs_i=cs_i):
                off = _slab(j)
                pw_r, pw_i = _pow_row(tre_v, tim_v, j, adesc, True, cs)
                pr, pi_ = _cmul(pw_r, pw_i, cs_r, cs_i)
                m_r = mr[pl.ds(off, 8), :] + pr
                m_i = mi[pl.ds(off, 8), :] + pi_
                mr[pl.ds(off, 8), :] = m_r
                mi[pl.ds(off, 8), :] = m_i
                a_r, a_i = acc
                return a_r + hp_r * m_r + hp_i * m_i, a_i + hp_r * m_i - hp_i * m_r

            edge_j, src, keep, sh = (SEG - 1, 0, rows < 7, 7) if desc else (0, (SEG - 1) * 8, rows >= 1, 1)
            h0r = jnp.where(keep, pltpu.roll(hr_v[src:src + 8, :], sh, 0), cinr_ref[:, col])
            h0i = jnp.where(keep, pltpu.roll(hi_v[src:src + 8, :], sh, 0), cini_ref[:, col])
            acc = fix(edge_j, h0r, h0i, (acc_r[sb], acc_i[sb]))

            def p2(jj, acc, fix=fix, hr_v=hr_v, hi_v=hi_v):
                j = jj if desc else jj + 1
                offp = _slab(j + 1 if desc else j - 1)
                return fix(j, hr_v[pl.ds(offp, 8), :], hi_v[pl.ds(offp, 8), :], acc)

            a_r, a_i = lax.fori_loop(0, SEG - 1, p2, acc, unroll=S5_UNROLL)
            acc_r[sb] = a_r
            acc_i[sb] = a_i

            mrb, mib = mr[...].astype(BF16), mi[...].astype(BF16)
            du = _dot_nt(mrb, bre_ref[sb]) + _dot_nt(mib, bim_ref[sb])
            if with_skip:
                du = du + dsk_ref[:, ch] * dy
                dd_ref[:, ch] += _colsum(dy * u)
            duseg[...] = du
            du_ref[:, ch] = _from_segments(duseg)
            dbr_ref[sb] += _dot_tn(ub, mrb)
            dbi_ref[sb] += _dot_tn(ub, mib)
            dcr_ref[sb] += _dot_tn(dyb, hr_v[...].astype(BF16))
            dci_ref[sb] -= _dot_tn(dyb, hi_v[...].astype(BF16))

            @pl.when(i == nproc - 1)
            def _(sb=sb, a_r=a_r, a_i=a_i):
                dar_ref[sb] = _colsum(a_r)
                dai_ref[sb] = _colsum(a_i)

    blk = lambda i: (order(i), 0)
    blk_dy = lambda i: (jnp.minimum(order(i), lch - 1), 0)
    mat = (nsb, cb, cs)
    return _pcall(
        body, name=name, grid=(nproc,), xchg=xchg, edges=_edges1(nproc),
        out_shape=(jax.ShapeDtypeStruct((t, cu), F32),
                   jax.ShapeDtypeStruct((nsb, 1, cs), F32), jax.ShapeDtypeStruct((nsb, 1, cs), F32),
                   jax.ShapeDtypeStruct(mat, F32), jax.ShapeDtypeStruct(mat, F32),
                   jax.ShapeDtypeStruct(mat, F32), jax.ShapeDtypeStruct(mat, F32),
                   jax.ShapeDtypeStruct((1, cu), F32)),
        in_specs=[pl.BlockSpec((TC, cu), blk_dy), pl.BlockSpec((TC, cu), blk), pl.BlockSpec((TC, nsb * cs), blk),
                  pl.BlockSpec((TC, nsb * cs), blk), pl.BlockSpec((8, nsb * cs), blk), pl.BlockSpec((8, nsb * cs), blk),
                  _const(mat), _const(mat), _const(mat), _const(mat), _const(tre.shape), _const(tim.shape),
                  _const((1, cu))],
        out_specs=(pl.BlockSpec((TC, cu), blk), _const((nsb, 1, cs)), _const((nsb, 1, cs)),
                   _const(mat), _const(mat), _const(mat), _const(mat), _const((1, cu))),
        scratch_shapes=[pltpu.VMEM((nsb, 8, cs), F32), pltpu.VMEM((nsb, 8, cs), F32), pltpu.VMEM((2, TC, cs), F32),
                        pltpu.VMEM((2, TC, cs), F32), pltpu.VMEM((nsb, 8, cs), F32), pltpu.VMEM((nsb, 8, cs), F32),
                        pltpu.VMEM((3, nsb, TC, cb), F32)],
        compiler_params=_params(("arbitrary",)),
    )(dy_lat, up, hr, hi, cinr, cini, bre, bim, ctr, cti, tre, tim, dskip)


def _gelu(x):
    k = math.sqrt(2.0 / math.pi)
    return 0.5 * x * (1.0 + jnp.tanh(k * (x + 0.044715 * (x * x * x))))


def _gelu_grad(x):
    k = math.sqrt(2.0 / math.pi)
    th = jnp.tanh(k * (x + 0.044715 * (x * x * x)))
    return 0.5 * (1.0 + th) + 0.5 * x * (1.0 - th * th) * (k * (1.0 + 3.0 * 0.044715 * (x * x)))


def _mix_out_fwd(yf, yb, dpool, x1, modtab, w_pool, pscale, w_glu, w_out):
    l, hm = dpool.shape
    d = x1.shape[1]

    def body(yf_ref, yb_ref, dp_ref, x_ref, mod_ref, wp_ref, ps_ref, wg_ref, wo_ref, x2_ref, yp_ref, cat_ref, mix_ref):
        ypre = yf_ref[...] + yb_ref[...]
        yp_ref[...] = ypre
        yg = _gelu(ypre)
        y2 = yg * _sigmoid(_dot(yg.astype(BF16), wg_ref[...]))
        po = _dot(dp_ref[...].astype(BF16), wp_ref[...]) * ps_ref[...]
        cat = jnp.concatenate([po, y2], axis=1).astype(BF16)
        cat_ref[...] = cat
        mix = _dot(cat, wo_ref[...])
        mix_ref[...] = mix.astype(BF16)
        x2_ref[...] = x_ref[...] + mod_ref[0, 5:6, :] * mix

    row = lambda i: (i, 0)
    lrow = row
    return _pcall(
        body, name="mix_out_fwd", grid=(l // TM,),
        out_shape=(jax.ShapeDtypeStruct((l, d), F32), jax.ShapeDtypeStruct((l, hm), F32),
                   jax.ShapeDtypeStruct((l, d), BF16), jax.ShapeDtypeStruct((l, d), BF16)),
        in_specs=[pl.BlockSpec((TM, hm), lrow), pl.BlockSpec((TM, hm), lrow), pl.BlockSpec((TM, hm), row),
                  pl.BlockSpec((TM, d), lrow), _mod_spec(d, False), _const(w_pool.shape), _const((1, hm)),
                  _const(w_glu.shape), _resident(w_out.shape)],
        out_specs=(pl.BlockSpec((TM, d), row), pl.BlockSpec((TM, hm), row), pl.BlockSpec((TM, d), row),
                   pl.BlockSpec((TM, d), row)),
        compiler_params=_params(("arbitrary",)),
    )(yf, yb, dpool, x1, modtab, w_pool, pscale, w_glu, w_out)


def _mix_out_bwd(dx2, mix, cat, ypre, dpool, modtab, w_pool, pscale, w_glu, w_out):
    l, hm = dpool.shape
    d = dx2.shape[1]
    nt = l // TML

    def body(dx_ref, mix_ref, cat_ref, yp_ref, dp_ref, mod_ref, wp_ref, ps_ref, wg_ref, wo_ref,
             dyp_ref, ddp_ref, dwo_ref, dwg_ref, dwp_ref, dps_ref, dgate_ref, acc_o, acc_g):
        i = pl.program_id(0)

        @pl.when(i == 0)
        def _():
            for ref in (acc_o, acc_g, dwp_ref, dps_ref, dgate_ref):
                ref[...] = jnp.zeros_like(ref)

        dx = dx_ref[...]
        dgate_ref[...] += _colsum(dx * mix_ref[...].astype(F32))
        dmix = (dx * mod_ref[0, 5:6, :]).astype(BF16)
        dcat = _dot_nt(dmix, wo_ref[...])
        acc_o[...] += _dot_tn(cat_ref[...], dmix)
        dpo, dy2 = dcat[:, :hm], dcat[:, hm:]
        dpb = dp_ref[...].astype(BF16)
        pp = _dot(dpb, wp_ref[...])
        dps_ref[...] += _colsum(dpo * pp)
        dpp = (dpo * ps_ref[...]).astype(BF16)
        ddp_ref[...] = _dot_nt(dpp, wp_ref[...])
        dwp_ref[...] += _dot_tn(dpb, dpp)
        ypre = yp_ref[...]
        yg = _gelu(ypre)
        ygb = yg.astype(BF16)
        s = _sigmoid(_dot(ygb, wg_ref[...]))
        dq = (dy2 * yg * s * (1.0 - s)).astype(BF16)
        dyg = dy2 * s + _dot_nt(dq, wg_ref[...])
        acc_g[...] += _dot_tn(ygb, dq)
        dyp_ref[...] = dyg * _gelu_grad(ypre)

        @pl.when(i == nt - 1)
        def _():
            dwo_ref[...] = acc_o[...].astype(BF16)
            dwg_ref[...] = acc_g[...].astype(BF16)

    row = lambda i: (i, 0)
    return _pcall(
        body, name="mix_out_bwd", grid=(nt,),
        out_shape=(jax.ShapeDtypeStruct((l, hm), F32), jax.ShapeDtypeStruct((l, hm), F32),
                   jax.ShapeDtypeStruct((d, d), BF16), jax.ShapeDtypeStruct((hm, hm), BF16),
                   jax.ShapeDtypeStruct((hm, hm), F32), jax.ShapeDtypeStruct((1, hm), F32),
                   jax.ShapeDtypeStruct((1, d), F32)),
        in_specs=[pl.BlockSpec((TML, d), row), pl.BlockSpec((TML, d), row), pl.BlockSpec((TML, d), row),
                  pl.BlockSpec((TML, hm), row), pl.BlockSpec((TML, hm), row), _mod_spec(d, False),
                  _const(w_pool.shape), _const((1, hm)), _const(w_glu.shape), _resident(w_out.shape)],
        out_specs=(pl.BlockSpec((TML, hm), row), pl.BlockSpec((TML, hm), row), _const((d, d)), _const((hm, hm)),
                   _const((hm, hm)), _const((1, hm)), _const((1, d))),
        scratch_shapes=[pltpu.VMEM((d, d), F32), pltpu.VMEM((hm, hm), F32)],
        compiler_params=_params(("arbitrary",)),
    )(dx2, mix, cat, ypre, dpool, modtab, w_pool, pscale, w_glu, w_out)


def _loss_head(xx, target, gg, loss_ref, dg_ref):
    d = xx.shape[-1]
    rstd = lax.rsqrt(jnp.mean(xx * xx, axis=-1, keepdims=True) + EPS)
    xhat = xx * rstd
    err = xhat * gg - target
    row_loss = jnp.mean(err * err, axis=-1, keepdims=True)
    loss_ref[...] += 0.5 * jnp.sum(row_loss, axis=0, keepdims=True)
    dout = err * (1.0 / d)
    dg_ref[...] += _colsum(dout * xhat)
    dxhat = dout * gg
    return rstd * (dxhat - xhat * jnp.mean(dxhat * xhat, axis=-1, keepdims=True))


def _adamw(parts, w, m, v, name):
    shape = w.shape
    c = shape[-1]
    r = int(np.prod(shape)) // c
    npart = parts.shape[0]
    tr = r
    for cand in (1024, 512, 256, 128, 64, 32, 16):
        if r % cand == 0 and cand * max(c, LANE) * 4 <= ADAM_BLOCK_BYTES:
            tr = cand
            break
    c1 = 1.0 - ADAM_B1
    c2 = 1.0 - ADAM_B2
    bc1 = 1.0 - ADAM_B1 ** ADAM_STEP
    bc2 = 1.0 - ADAM_B2 ** ADAM_STEP

    def body(p_ref, w_ref, m_ref, v_ref, g_ref, d_ref, mo_ref, vo_ref):
        g = p_ref[0].astype(F32)
        for k in range(1, npart):
            g = g + p_ref[k].astype(F32)
        mn = ADAM_B1 * m_ref[...] + c1 * g
        vn = ADAM_B2 * v_ref[...] + c2 * (g * g)
        m_hat = mn / bc1
        v_hat = vn / bc2
        g_ref[...] = g
        mo_ref[...] = mn
        vo_ref[...] = vn
        d_ref[...] = -ADAM_LR * (m_hat / (jnp.sqrt(v_hat) + ADAM_EPS) + ADAM_WD * w_ref[...])

    spec = pl.BlockSpec((tr, c), lambda i: (i, 0))
    outs = _pcall(
        body, name=name, grid=(r // tr,),
        out_shape=tuple(jax.ShapeDtypeStruct((r, c), F32) for _ in range(4)),
        in_specs=[pl.BlockSpec((npart, tr, c), lambda i: (0, i, 0)), spec, spec, spec],
        out_specs=(spec, spec, spec, spec),
        compiler_params=_params(("arbitrary",)),
    )(parts.reshape(npart, r, c), w.reshape(r, c), m.reshape(r, c), v.reshape(r, c))
    return tuple(o.reshape(shape) for o in outs)


def _adamw_ffn(parts, lane_blk, w, m, v, name):
    _, nl, r, c = w.shape
    rp = parts[0][0].shape[1]
    if lane_blk is None:
        tc = 256
        grid = (c // tc,)
        p_spec = lambda a: pl.BlockSpec((a.shape[0], rp, tc), lambda i: (0, 0, i))
        w_spec = pl.BlockSpec((1, nl, r, tc), lambda i: (0, 0, 0, i))
        take = lambda ref, k: ref[k, 0:r, :]
    else:
        tr = 128
        grid = (r // tr,)
        p_spec = lambda a: pl.BlockSpec((a.shape[0], tr, FPAD), lambda i: (0, i, lane_blk))
        w_spec = pl.BlockSpec((1, nl, tr, c), lambda i: (0, 0, i, 0))
        take = lambda ref, k: ref[k, :, 0:c]
    flat = [a for layer in parts for a in layer]
    counts = [len(layer) for layer in parts]
    c1 = 1.0 - ADAM_B1
    c2 = 1.0 - ADAM_B2
    bc1 = 1.0 - ADAM_B1 ** ADAM_STEP
    bc2 = 1.0 - ADAM_B2 ** ADAM_STEP

    def body(*refs):
        w_ref, m_ref, v_ref, g_ref, d_ref, mo_ref, vo_ref = refs[len(flat):]
        first = 0
        for lyr in range(nl):
            g = None
            for ref in refs[first:first + counts[lyr]]:
                for k in range(ref.shape[0]):
                    term = take(ref, k).astype(F32)
                    g = term if g is None else g + term
            first += counts[lyr]
            mn = ADAM_B1 * m_ref[0, lyr] + c1 * g
            vn = ADAM_B2 * v_ref[0, lyr] + c2 * (g * g)
            g_ref[0, lyr] = g
            mo_ref[0, lyr] = mn
            vo_ref[0, lyr] = vn
            d_ref[0, lyr] = -ADAM_LR * ((mn / bc1) / (jnp.sqrt(vn / bc2) + ADAM_EPS) + ADAM_WD * w_ref[0, lyr])

    return _pcall(
        body, name=name, grid=grid, out_shape=tuple(jax.ShapeDtypeStruct(w.shape, F32) for _ in range(4)),
        in_specs=[p_spec(a) for a in flat] + [w_spec] * 3, out_specs=(w_spec,) * 4,
        compiler_params=_params(("arbitrary",)),
    )(*flat, w, m, v)


def _blockdiag(xb):
    n, a, b = xb.shape[-3:]
    eye = jnp.eye(n, dtype=xb.dtype)
    out = xb[..., :, :, None, :] * eye[:, None, :, None]
    return out.reshape(xb.shape[:-3] + (n * a, n * b))


def _diagblocks(mat, n):
    lead = mat.shape[:-2]
    a, b = mat.shape[-2] // n, mat.shape[-1] // n
    m5 = mat.reshape(lead + (n, a, n, b))
    nl = len(lead)
    dg = jnp.diagonal(m5, axis1=nl, axis2=nl + 2)
    return jnp.moveaxis(dg, -1, nl)


def _pad128(a):
    flat = a.reshape(-1)
    pad = (-flat.shape[0]) % LANE
    return jnp.pad(flat, (0, pad)) if pad else flat


def kernel(x, c, ctx, c_ctx, norm_g, w_ada, b_ada, ffn_w1, ffn_w3, ffn_w2, w_in, pool_w, pool_scale, s5_a_re, s5_a_im, s5_log_dt, s5_b_re, s5_b_im, s5_c_re, s5_c_im, s5_d, s5_w_glu, w_out, final_g, loss_target, m_c_ctx, m_norm_g, m_w_ada, m_b_ada, m_ffn_w1, m_ffn_w3, m_ffn_w2, m_w_in, m_pool_w, m_pool_scale, m_s5_a_re, m_s5_a_im, m_s5_log_dt, m_s5_b_re, m_s5_b_im, m_s5_c_re, m_s5_c_im, m_s5_d, m_s5_w_glu, m_w_out, m_final_g, v_c_ctx, v_norm_g, v_w_ada, v_b_ada, v_ffn_w1, v_ffn_w3, v_ffn_w2, v_w_in, v_pool_w, v_pool_scale, v_s5_a_re, v_s5_a_im, v_s5_log_dt, v_s5_b_re, v_s5_b_im, v_s5_c_re, v_s5_c_im, v_s5_d, v_s5_w_glu, v_w_out, v_final_g):
    weights = dict(c_ctx=c_ctx, norm_g=norm_g, w_ada=w_ada, b_ada=b_ada, ffn_w1=ffn_w1, ffn_w3=ffn_w3, ffn_w2=ffn_w2,
                   w_in=w_in, pool_w=pool_w, pool_scale=pool_scale, s5_a_re=s5_a_re, s5_a_im=s5_a_im,
                   s5_log_dt=s5_log_dt, s5_b_re=s5_b_re, s5_b_im=s5_b_im, s5_c_re=s5_c_re, s5_c_im=s5_c_im, s5_d=s5_d,
                   s5_w_glu=s5_w_glu, w_out=w_out, final_g=final_g)
    mom_m = dict(c_ctx=m_c_ctx, norm_g=m_norm_g, w_ada=m_w_ada, b_ada=m_b_ada, ffn_w1=m_ffn_w1, ffn_w3=m_ffn_w3,
                 ffn_w2=m_ffn_w2, w_in=m_w_in, pool_w=m_pool_w, pool_scale=m_pool_scale, s5_a_re=m_s5_a_re,
                 s5_a_im=m_s5_a_im, s5_log_dt=m_s5_log_dt, s5_b_re=m_s5_b_re, s5_b_im=m_s5_b_im, s5_c_re=m_s5_c_re,
                 s5_c_im=m_s5_c_im, s5_d=m_s5_d, s5_w_glu=m_s5_w_glu, w_out=m_w_out, final_g=m_final_g)
    mom_v = dict(c_ctx=v_c_ctx, norm_g=v_norm_g, w_ada=v_w_ada, b_ada=v_b_ada, ffn_w1=v_ffn_w1, ffn_w3=v_ffn_w3,
                 ffn_w2=v_ffn_w2, w_in=v_w_in, pool_w=v_pool_w, pool_scale=v_pool_scale, s5_a_re=v_s5_a_re,
                 s5_a_im=v_s5_a_im, s5_log_dt=v_s5_log_dt, s5_b_re=v_s5_b_re, s5_b_im=v_s5_b_im, s5_c_re=v_s5_c_re,
                 s5_c_im=v_s5_c_im, s5_d=v_s5_d, s5_w_glu=v_s5_w_glu, w_out=v_w_out, final_g=v_final_g)

    l, d = x.shape[1], x.shape[2]
    lc = ctx.shape[1]
    t = l + TM
    rows = l // GRID_W
    fblk = ffn_w1.shape[-1]
    ngrp, gp = s5_a_re.shape[2], s5_a_re.shape[3]
    gc = s5_b_re.shape[3]
    hm = ngrp * gc
    nsb = 4
    gsb = ngrp // nsb
    assert lc == TC and TM % TC == 0 and l % TM == 0 and TM == TML and hm == 4 * LANE and ngrp * gp == nsb * 512
    me = 4 * lax.axis_index("x") + 2 * lax.axis_index("y") + lax.axis_index("c")

    padc = ((0, 0), (0, 0), (0, FPAD - fblk))
    w13_loc = [jnp.concatenate([jnp.pad(ffn_w1[0, k], padc[1:]), jnp.pad(ffn_w3[0, k], padc[1:])], axis=-1).astype(BF16)
               for k in range(2)]
    w2_loc = [jnp.pad(ffn_w2[0, k], ((0, FPAD - fblk), (0, 0))).astype(BF16) for k in range(2)]
    gath = _exchange([(w13_loc[0], GATHER_REL), (w2_loc[0], GATHER_REL), (c, GATHER), (norm_g[0], GATHER)],
                     "exchange_weights")
    w13_0 = gath[0]
    w2_0 = gath[1].reshape(NDEV // 2, 2 * FPAD, d)
    c_all = gath[2].reshape(NDEV, d)
    g_all = gath[3].transpose(1, 0, 2).reshape(3, d)

    wa = w_ada.shape[-1]
    c16 = jnp.concatenate([c_all, c_ctx.reshape(1, d), jnp.zeros((7, d), F32)], axis=0)
    b_blk = lax.dynamic_slice_in_dim(b_ada, me * wa, wa, axis=1)
    mod_sl = _ada_fwd(c16, w_ada[0], b_blk)
    gath = _exchange([(mod_sl, GATHER)], "exchange_mod")
    mod_full = gath[0].transpose(1, 0, 2).reshape(16, NDEV * wa)
    mod_l = lax.dynamic_index_in_dim(mod_full, me, axis=0, keepdims=False).reshape(9, d)
    modtab = jnp.stack([mod_full[8].reshape(9, d), mod_l])

    x0 = [jnp.pad(ctx[0], ((0, TM - lc), (0, 0))), x[0]]
    g0, g1, g2 = g_all[0:1], g_all[1:2], g_all[2:3]
    x1, h0, ab0, o0, w13_1, w2_1, w_in_f, w_out_f, w_glu_f = _ffn_fwd(
        x0, modtab, g0, w13_0, w2_0, 0, True, "ffn0_fwd",
        xchg=[(w13_loc[1], GATHER_REL), (w2_loc[1], GATHER_REL), (w_in[0].astype(BF16), GATHER),
              (w_out[0].astype(BF16), GATHER), (s5_w_glu[0].astype(BF16), GATHER)])
    w2_1 = w2_1.reshape(NDEV // 2, 2 * FPAD, d)
    w_in_f = w_in_f.reshape(d, -1)
    w_out_f = w_out_f.reshape(-1, d)
    w_glu_f = w_glu_f.reshape(hm, hm)
    h1, u_pool, up = _mix_in_fwd(x1, modtab, g1, w_in_f)

    dpool = _pool(u_pool, rows, False, "pool_fwd")
    w_pool_bd = _blockdiag(pool_w[0]).astype(BF16)

    q = 2 * ngrp
    kvec = np.concatenate([np.arange(1, SEG + 1), SEG * np.arange(1, 9), SEG * np.arange(8, 0, -1)]).astype(np.float32)
    a_re3, a_im3 = s5_a_re[0].reshape(q, 1, gp), s5_a_im[0].reshape(q, 1, gp)
    ldt3 = jnp.broadcast_to(s5_log_dt[0].reshape(q, 1, 1), (q, 1, gp))
    c_re3, c_im3 = s5_c_re[0].reshape(q, gc, gp), s5_c_im[0].reshape(q, gc, gp)
    cp_re, cp_im, tab_re, tab_im = _s5_prep(a_re3, a_im3, ldt3, c_re3, c_im3, jnp.broadcast_to(jnp.asarray(kvec).reshape(1, NTAB, 1), (1, NTAB, gp)))
    tabs = [(tab_re.reshape(2, ngrp, NTAB, gp)[k].transpose(1, 0, 2).reshape(NTAB, ngrp * gp),
             tab_im.reshape(2, ngrp, NTAB, gp)[k].transpose(1, 0, 2).reshape(NTAB, ngrp * gp)) for k in range(2)]
    b_t = lambda b: _blockdiag(b[0].reshape(nsb, gsb, gp, gc).transpose(0, 1, 3, 2)).astype(BF16)
    bre, bim = b_t(s5_b_re), b_t(s5_b_im)
    cp_t = lambda cp: _blockdiag(cp.reshape(2, nsb, gsb, gc, gp)).astype(BF16)
    ctr, cti = cp_t(cp_re), cp_t(cp_im)
    cmr, cmi = jnp.swapaxes(ctr, -1, -2), jnp.swapaxes(cti, -1, -2)
    dskip = s5_d

    lch = l // TC
    nproc = lch + 1
    order_f = lambda i: jnp.where(i == 0, lch, i - 1)
    order_b = lambda i: jnp.where(i == 0, lch, lch - i)
    rorder_f = lambda i: jnp.where(i == lch, lch, lch - 1 - i)
    rorder_b = lambda i: i
    y_f, hr_f, hi_f, cinr_f, cini_f = _s5_fwd(up, bre, bim, cmr[0], cmi[0], tabs[0][0], tabs[0][1], dskip,
                                               order_f, nproc, False, True, "s5_fwd_f")
    y_b, hr_b, hi_b, cinr_b, cini_b = _s5_fwd(up, bre, bim, cmr[1], cmi[1], tabs[1][0], tabs[1][1], dskip,
                                              order_b, nproc, True, False, "s5_fwd_b")

    x2, ypre, cat, mix = _mix_out_fwd(y_f, y_b, dpool, x1, modtab, w_pool_bd, pool_scale, w_glu_f, w_out_f)
    dx3, h2, ab2, o2, loss_part, dfinal_g = _ffn_fwd([x2], modtab, g2, w13_1, w2_1, 6, False, "ffn1_fwd",
                                                     head=(loss_target[0], final_g.reshape(1, d)))

    dhp2, d13_1, d2_1 = _ffn_bwd(dx3, modtab, h2, ab2, w13_1, w2_1, 6, False, "ffn1_bwd")
    dx2, dmod_678, dg2 = _ffn_bwd_norm(dx3, [dhp2], [x2], o2, modtab, g2, 6, False, "ffn1_bwd_norm")
    dypre, ddpool, dw_out, dw_glu, dw_pool_bd, dpscale, dgate5 = _mix_out_bwd(
        dx2, mix, cat, ypre, dpool, modtab, w_pool_bd, pool_scale, w_glu_f, w_out_f)

    du_f, dar_f, dai_f, dcr_f, dci_f, dbr_f, dbi_f, dd_skip, r13_1, r2_1 = _s5_bwd(
        dypre, up, hr_f, hi_f, cinr_f, cini_f, bre, bim, ctr[0], cti[0], tabs[0][0], tabs[0][1], dskip,
        rorder_f, nproc, False, True, "s5_bwd_f",
        xchg=[(d13_1, A2A_REL), (d2_1.reshape(NDEV, FPAD, d), A2A_REL)])
    du_b, dar_b, dai_b, dcr_b, dci_b, dbr_b, dbi_b, _, r_out, r_glu = _s5_bwd(
        dypre, up, hr_b, hi_b, cinr_b, cini_b, bre, bim, ctr[1], cti[1], tabs[1][0], tabs[1][1], dskip,
        rorder_b, nproc, True, False, "s5_bwd_b",
        xchg=[(dw_out.reshape(NDEV, -1, d), A2A), (dw_glu.reshape(NDEV, hm // NDEV, hm), A2A)])

    dab_r = jnp.stack([dar_f, dar_b]).reshape(q, 1, gp)
    dab_i = jnp.stack([dai_f, dai_b]).reshape(q, 1, gp)
    dcp_r = _diagblocks(jnp.stack([dcr_f, dcr_b]), gsb).reshape(q, gc, gp)
    dcp_i = _diagblocks(jnp.stack([dci_f, dci_b]), gsb).reshape(q, gc, gp)
    ga_re, ga_im, gldt, gc_re, gc_im = _s5_param_bwd(a_re3, a_im3, ldt3, c_re3, c_im3, dab_r, dab_i, dcp_r, dcp_i)
    gb_re = (_diagblocks(dbr_f, gsb) + _diagblocks(dbr_b, gsb)).transpose(0, 1, 3, 2).reshape(ngrp, gp, gc)
    gb_im = (_diagblocks(dbi_f, gsb) + _diagblocks(dbi_b, gsb)).transpose(0, 1, 3, 2).reshape(ngrp, gp, gc)

    def pack(small):
        offs, pieces, off = {}, [], 0
        for k_, a_ in small.items():
            p_ = _pad128(a_.astype(F32))
            offs[k_] = (off, int(np.prod(a_.shape)))
            off += p_.shape[0]
            pieces.append(p_)
        return jnp.concatenate(pieces).reshape(1, off), offs

    gw_pool = _diagblocks(dw_pool_bd, 4)
    bundle_a, offs = pack(dict(pool_w=gw_pool, pool_scale=dpscale, s5_a_re=ga_re, s5_a_im=ga_im,
                               s5_log_dt=gldt[:, 0, 0], s5_b_re=gb_re, s5_b_im=gb_im, s5_c_re=gc_re, s5_c_im=gc_im,
                               s5_d=dd_skip, final_g=dfinal_g))

    du_pool = _pool(ddpool, rows, True, "pool_bwd")
    dx1, dw_in, dmod_34, dg1 = _mix_in_bwd(du_pool, du_f, du_b, dx2, h1, x1, modtab, g1, w_in_f, lc)
    dhp0a, d13_0a, d2_0a, bund_a, r_in = _ffn_bwd(
        dx1, modtab, h0, ab0, w13_0, w2_0, 0, True, "ffn0_bwd_a", pair0=0, npair=2,
        xchg=[(bundle_a, GATHER), (dw_in.reshape(NDEV, d // NDEV, -1), A2A)])
    dhp0b, d13_0b, d2_0b, r13_0a, r2_0a = _ffn_bwd(
        dx1, modtab, h0, ab0, w13_0, w2_0, 0, True, "ffn0_bwd_b", pair0=2, npair=2,
        xchg=[(d13_0a, A2A_SAME), (d2_0a.reshape(NDEV // 2, FPAD, d), A2A_SAME)])
    dx0, dmod_012, dg0, r13_0b, r2_0b = _ffn_bwd_norm(
        dx1, [dhp0a, dhp0b], x0, o0, modtab, g0, 0, True, "ffn0_bwd_norm",
        xchg=[(d13_0b, A2A_OTHER), (d2_0b.reshape(NDEV // 2, FPAD, d), A2A_OTHER)])
    grad_x = dx0.reshape(1, l, d)

    dmod_c = jnp.concatenate([dmod_012[0], dmod_34[0], jnp.zeros((4, d), F32)], axis=0)
    dmod_l = jnp.concatenate([dmod_012[1], dmod_34[1], dgate5, dmod_678[0]], axis=0)
    bundle_b, offs_b = pack(dict(norm_g=jnp.concatenate([dg0, dg1, dg2], axis=0), dmod_l=dmod_l, dmod_c=dmod_c,
                                 loss=loss_part[:, :1]))
    bund_b = _exchange([(bundle_b, GATHER)], "exchange_small")[0]
    bunds = {**{k_: (bund_a.reshape(NDEV, -1), v_) for k_, v_ in offs.items()},
             **{k_: (bund_b.reshape(NDEV, -1), v_) for k_, v_ in offs_b.items()}}
    r13, r2 = [(r13_0a, r13_0b), (r13_1,)], [(r2_0a, r2_0b), (r2_1,)]

    def piece(name):
        b_, (o_, n_) = bunds[name]
        return b_[:, o_:o_ + n_]

    loss = jnp.sum(piece("loss")[:, 0])

    dl_all = lax.dynamic_slice_in_dim(piece("dmod_l"), me * wa, wa, axis=1)
    dc_all = lax.dynamic_slice_in_dim(piece("dmod_c"), me * wa, wa, axis=1)
    g_wada, gc_part = _ada_bwd(c16, w_ada[0], dl_all, dc_all)
    gc_all = _exchange([(gc_part, GATHER)], "exchange_cctx")[0]

    parts = {
        "c_ctx": gc_all.reshape(NDEV, d),
        "norm_g": lax.dynamic_slice_in_dim(piece("norm_g").reshape(NDEV, 3, d), me * (d // NDEV), d // NDEV,
                                           axis=2).reshape((NDEV,) + norm_g.shape),
        "w_ada": g_wada.reshape((1,) + w_ada.shape),
        "b_ada": jnp.concatenate([piece("dmod_l"), piece("dmod_c")], axis=0).reshape((2 * NDEV,) + b_ada.shape),
        "w_in": r_in.reshape((NDEV,) + w_in.shape),
        "s5_w_glu": r_glu.reshape((NDEV,) + s5_w_glu.shape),
        "w_out": r_out.reshape((NDEV,) + w_out.shape),
    }
    for k_ in ("pool_w", "pool_scale", "s5_a_re", "s5_a_im", "s5_log_dt", "s5_b_re", "s5_b_im", "s5_c_re", "s5_c_im",
               "s5_d", "final_g"):
        parts[k_] = piece(k_).reshape((NDEV,) + weights[k_].shape)

    grads, deltas, new_m, new_v = [], [], [], []
    ffn_parts = {"ffn_w1": (r13, 0), "ffn_w3": (r13, 1), "ffn_w2": (r2, None)}
    for k_ in weights:
        if k_ in ffn_parts:
            g_, d_, m_, v_ = _adamw_ffn(*ffn_parts[k_], weights[k_], mom_m[k_], mom_v[k_], "adamw_" + k_)
        else:
            g_, d_, m_, v_ = _adamw(parts[k_], weights[k_], mom_m[k_], mom_v[k_], "adamw_" + k_)
        grads.append(g_)
        deltas.append(d_)
        new_m.append(m_)
        new_v.append(v_)
    return (loss, grad_x, *grads, *deltas, *new_m, *new_v)
```

```python
import functools
import math

import numpy as np
import jax
import jax.numpy as jnp
from jax import lax
from jax.experimental import pallas as pl
from jax.experimental.pallas import tpu as pltpu

F32 = jnp.float32
BF16 = jnp.bfloat16
AXES = ("x", "y", "c")
NDEV = 8
EPS = 1e-6
TM = 512
TML = 512
TC = 256
SEG = TC // 8
NTAB = SEG + 16
S5_UNROLL = True
GRID_W = 64
POOL_WINDOWS = (2, 4, 8, 16)
LANE = 128
FPAD = 384
VMEM_LIMIT = 56 * 1024 * 1024
ADAM_BLOCK_BYTES = 512 * 1024
ADAM_FFN_LANES = 256
ADAM_LR, ADAM_B1, ADAM_B2, ADAM_EPS, ADAM_WD, ADAM_STEP = 0.001, 0.9, 0.999, 1e-08, 0.01, 10


def _raw_call(body, kw):
    return pl.pallas_call(body, **kw)


def _pcall(body, xchg=None, edges=None, **kw):
    extra = ()
    if xchg:
        arrs, kinds = [a for a, _ in xchg], [k for _, k in xchg]
        n = len(arrs)
        n_in, n_out, n_scr = len(kw["in_specs"]), len(kw["out_shape"]), len(kw.get("scratch_shapes", ()))
        inner = body

        def hosted(*refs):
            a, b = n_in, n_in + n
            c_, d_ = b + n_out, b + n_out + n
            e = d_ + n_scr
            first, mid, last = edges()

            @pl.when(first)
            def _():
                _xchg_start(refs[a:b], refs[c_:d_], refs[e:], kinds)

            inner(*refs[:a], *refs[b:c_], *refs[d_:e])

            if any(k == GATHER_REL for k in kinds):
                @pl.when(mid)
                def _():
                    _xchg_relay(refs[a:b], refs[c_:d_], refs[e:], kinds)

            @pl.when(last)
            def _():
                _xchg_finish(refs[a:b], refs[c_:d_], refs[e:], kinds)

        body = hosted
        any_spec = pl.BlockSpec(memory_space=pl.ANY)
        kw = dict(kw, in_specs=list(kw["in_specs"]) + [any_spec] * n,
                  out_shape=tuple(kw["out_shape"]) + _xchg_shapes(arrs, kinds),
                  out_specs=tuple(kw["out_specs"]) + (any_spec,) * n,
                  scratch_shapes=list(kw.get("scratch_shapes", ())) + _xchg_sems(n))
        extra = tuple(arrs)
    call = _raw_call(body, kw)
    return (lambda *args: call(*args, *extra)) if extra else call


def _params(sem):
    return pltpu.CompilerParams(dimension_semantics=sem, vmem_limit_bytes=VMEM_LIMIT)


def _dot(a, b):
    return jnp.dot(a, b, preferred_element_type=F32)


def _dot_nt(a, b):
    return lax.dot_general(a, b, (((1,), (1,)), ((), ())), preferred_element_type=F32)


def _dot_tn(a, b):
    return lax.dot_general(a, b, (((0,), (0,)), ((), ())), preferred_element_type=F32)


def _sigmoid(x):
    return 1.0 / (1.0 + jnp.exp(-x))


def _colsum(a):
    return jnp.sum(a, axis=0, keepdims=True)


def _resident(shape):
    nd = len(shape)
    return pl.BlockSpec(shape, lambda *_: (0,) * nd, pipeline_mode=pl.Buffered(1))


def _const(shape):
    nd = len(shape)
    return pl.BlockSpec(shape, lambda *_: (0,) * nd)


GATHER = "gather"
GATHER_REL = "gather_rel"
A2A = "a2a"
A2A_REL = "a2a_rel"
A2A_SAME = "a2a_same"
A2A_OTHER = "a2a_other"


def _exchange(items, name):
    arrs, kinds = [a for a, _ in items], [k for _, k in items]
    n = len(arrs)

    def body(*refs):
        _xchg_start(refs[:n], refs[n:2 * n], refs[2 * n:], kinds)
        _xchg_relay(refs[:n], refs[n:2 * n], refs[2 * n:], kinds)
        _xchg_finish(refs[:n], refs[n:2 * n], refs[2 * n:], kinds)

    any_spec = pl.BlockSpec(memory_space=pl.ANY)
    outs = _pcall(
        body, name=name, out_shape=_xchg_shapes(arrs, kinds), in_specs=[any_spec] * n, out_specs=[any_spec] * n,
        scratch_shapes=_xchg_sems(n),
    )(*arrs)
    return list(outs)


def _edges1(n0):
    return lambda: (pl.program_id(0) == 0, pl.program_id(0) == (7 * n0) // 8, pl.program_id(0) == n0 - 1)


def _edges2(n0, n1):
    def edges():
        step = pl.program_id(0) * n1 + pl.program_id(1)
        return step == 0, step == (7 * n0 * n1) // 8, step == n0 * n1 - 1
    return edges


def _xchg_shapes(arrs, kinds):
    return tuple(jax.ShapeDtypeStruct(((NDEV,) if k in (GATHER, GATHER_REL) else ()) + tuple(a.shape), a.dtype)
                 for a, k in zip(arrs, kinds))


def _xchg_sems(n):
    return [pltpu.SemaphoreType.DMA((n * (NDEV - 1),)), pltpu.SemaphoreType.DMA((n * (NDEV - 1),)),
            pltpu.SemaphoreType.DMA((n,))]


def _xchg_plan(in_refs, out_refs, sems, kinds):
    send_sems, recv_sems, loc_sems = sems
    x, y, c = (lax.axis_index(a) for a in AXES)
    my_abs, my_chip = 4 * x + 2 * y + c, 2 * x + y

    def peer(p):
        px = 1 - x if p & 4 else x
        py = 1 - y if p & 2 else y
        pc = 1 - c if p & 1 else c
        return (px, py, pc), 4 * px + 2 * py + pc, 2 * px + py

    starts, relays, recvs = [], [], []
    for k, kind in enumerate(kinds):
        src, dst = in_refs[k], out_refs[k]

        def remote(src_ref, row, p, pair, dst=dst, k=k):
            dev, sem = peer(p)[0], k * (NDEV - 1) + pair - 1
            return lambda: pltpu.make_async_remote_copy(
                src_ref=src_ref, dst_ref=dst.at[row], send_sem=send_sems.at[sem], recv_sem=recv_sems.at[sem],
                device_id=dev, device_id_type=pl.DeviceIdType.MESH)

        def local(src_ref, row, dst=dst, k=k):
            return lambda: pltpu.make_async_copy(src_ref, dst.at[row], loc_sems.at[k])

        if kind == GATHER:
            for p in range(1, NDEV):
                starts.append((remote(src, my_abs, p, p), False))
                recvs.append(remote(src, peer(p)[1], p, p))
            starts.append((local(src, my_abs), True))
        elif kind == GATHER_REL:
            for p in (1, 4, 2, 6):
                starts.append((remote(src, 4 * (p & 1) + my_chip, p, p), False))
            for q in (4, 2, 6):
                pchip = peer(q)[2]
                relays.append((remote(src, pchip, q, q), remote(dst.at[pchip], 4 + pchip, 1, q | 1)))
                recvs.append(remote(src, 4 + pchip, 1, q | 1))
            recvs.append(remote(src, 4 + my_chip, 1, 1))
            starts.append((local(src, my_chip), True))
        elif kind in (A2A, A2A_REL):
            for p in range(1, NDEV):
                _, pabs, pchip = peer(p)
                theirs, mine = (pabs, my_abs) if kind == A2A else (4 * (p & 1) + pchip, 4 * (p & 1) + my_chip)
                starts.append((remote(src.at[theirs], mine, p, p), False))
                recvs.append(remote(src.at[theirs], theirs, p, p))
            own = my_abs if kind == A2A else my_chip
            starts.append((local(src.at[own], own), True))
        else:
            for p in ((4, 2, 6) if kind == A2A_SAME else (1, 5, 3, 7)):
                pchip = peer(p)[2]
                starts.append((remote(src.at[pchip], my_chip, p, p), False))
                recvs.append(remote(src.at[pchip], pchip, p, p))
            if kind == A2A_SAME:
                starts.append((local(src.at[my_chip], my_chip), True))
    return starts, relays, recvs


def _xchg_start(in_refs, out_refs, sems, kinds):
    for make, _ in _xchg_plan(in_refs, out_refs, sems, kinds)[0]:
        make().start()


def _xchg_relay(in_refs, out_refs, sems, kinds):
    for arrival, forward in _xchg_plan(in_refs, out_refs, sems, kinds)[1]:
        arrival().wait_recv()
        forward().start()


def _xchg_finish(in_refs, out_refs, sems, kinds):
    starts, relays, recvs = _xchg_plan(in_refs, out_refs, sems, kinds)
    for make in recvs:
        make().wait_recv()
    for make, is_local in starts:
        if is_local:
            make().wait()
        else:
            make().wait_send()
    for _, forward in relays:
        forward().wait_send()


def _ada_fwd(c16, w_blk, b_blk):
    def body(c_ref, w_ref, b_ref, o_ref):
        cc = c_ref[...]
        s = (cc * _sigmoid(cc)).astype(BF16)
        o_ref[...] = _dot(s, w_ref[...].astype(BF16)) + b_ref[...]

    return _pcall(body, name="ada_fwd", out_shape=jax.ShapeDtypeStruct((16, w_blk.shape[1]), F32),
                  compiler_params=_params(None))(c16, w_blk, b_blk)


def _ada_bwd(c16, w_blk, dl, dc):
    d, w = w_blk.shape

    def body(c_ref, w_ref, dl_ref, dc_ref, gw_ref, gc_ref):
        cc = c_ref[...]
        sg = _sigmoid(cc)
        s = (cc * sg).astype(BF16)
        dctot = _colsum(dc_ref[...])
        dm = jnp.concatenate([dl_ref[...], jnp.broadcast_to(dctot, (8, w))], axis=0)
        rows = lax.broadcasted_iota(jnp.int32, (16, w), 0)
        dm = jnp.where(rows <= 8, dm, 0.0).astype(BF16)
        gw_ref[...] = _dot_tn(s, dm)
        t = _dot_nt(jnp.broadcast_to(dctot, (8, w)).astype(BF16), w_ref[...].astype(BF16))[0:1]
        c8, s8 = cc[8:9], sg[8:9]
        gc_ref[...] = t * (s8 * (1.0 + c8 * (1.0 - s8)))

    return _pcall(body, name="ada_bwd",
                  out_shape=(jax.ShapeDtypeStruct((d, w), F32), jax.ShapeDtypeStruct((1, d), F32)),
                  compiler_params=_params(None))(c16, w_blk, dl, dc)


def _norm_fwd(x, g, shift, scale):
    rstd = lax.rsqrt(jnp.mean(x * x, axis=-1, keepdims=True) + EPS)
    xhat = x * rstd
    xn = xhat * g
    return rstd, xhat, xn, xn * (1.0 + scale) + shift


def _norm_bwd(dh, rstd, xhat, xn, g, scale):
    dxn = dh * (1.0 + scale)
    dxhat = dxn * g
    dx = rstd * (dxhat - xhat * jnp.mean(dxhat * xhat, axis=-1, keepdims=True))
    return dx, _colsum(dh), _colsum(dh * xn), _colsum(dxn * xhat)


def _mod_spec(d, has_ctx):
    if has_ctx:
        return pl.BlockSpec((1, 9, d), lambda *ids: (jnp.minimum(ids[-1], 1), 0, 0))
    return pl.BlockSpec((1, 9, d), lambda *ids: (1, 0, 0))


def _row_specs(nx, tm, d):
    if nx == 1:
        return [pl.BlockSpec((tm, d), lambda i: (i, 0))]
    return [pl.BlockSpec((tm, d), lambda i: (0, 0)), pl.BlockSpec((tm, d), lambda i: (jnp.maximum(i - 1, 0), 0))]


def _rows(refs):
    if len(refs) == 1:
        return refs[0][...]
    return jnp.where(pl.program_id(0) == 0, refs[0][...], refs[1][...])


def _tile(has_ctx):
    return TM if has_ctx else TML


def _tile_of(nt, has_ctx):
    if has_ctx:
        return lambda i: jnp.where(i == 0, nt - 1, i - 1)
    return lambda i: i


def _dmod_spec(nrow, d, has_ctx):
    if has_ctx:
        return pl.BlockSpec((1, nrow, d), lambda i: (jnp.minimum(i, 1), 0, 0))
    return pl.BlockSpec((1, nrow, d), lambda i: (0, 0, 0))


def _ffn_fwd(x, modtab, g, w13, w2, k0, has_ctx, name, xchg=None, head=None):
    t, d = sum(a.shape[0] for a in x), x[0].shape[1]
    nb, w, _ = w13.shape
    fp = w // 2
    tm = _tile(has_ctx)
    nx = len(x)

    nhead = 0 if head is None else 2

    def body(*refs):
        mod_ref, g_ref, w13_ref, w2_ref = refs[nx:nx + 4]
        xo_ref, h_ref, ab_ref, o_ref = refs[nx + 4 + nhead:nx + 8 + nhead]
        xx = _rows(refs[:nx])
        shift, scale, gate = mod_ref[0, k0:k0 + 1, :], mod_ref[0, k0 + 1:k0 + 2, :], mod_ref[0, k0 + 2:k0 + 3, :]
        _, _, _, h = _norm_fwd(xx, g_ref[...], shift, scale)
        hb = h.astype(BF16)
        h_ref[...] = hb
        acc = jnp.zeros((tm, d), F32)
        for p in range(nb // 2):
            zs = []
            for q in range(2):
                blk = 2 * p + q
                ab = _dot_nt(hb, w13_ref[blk])
                ab_ref[:, blk * w:(blk + 1) * w] = ab.astype(BF16)
                a, b = ab[:, :fp], ab[:, fp:]
                zs.append((a * _sigmoid(a) * b).astype(BF16))
            acc = acc + _dot(jnp.concatenate(zs, axis=1), w2_ref[p])
        o_ref[...] = acc.astype(BF16)
        xo = xx + (0.5 * gate) * acc
        if head is None:
            xo_ref[...] = xo
        else:
            tgt_ref, fg_ref = refs[nx + 4:nx + 6]
            loss_ref, dfg_ref = refs[nx + 8 + nhead:]

            @pl.when(pl.program_id(0) == 0)
            def _():
                loss_ref[...] = jnp.zeros_like(loss_ref)
                dfg_ref[...] = jnp.zeros_like(dfg_ref)

            xo_ref[...] = _loss_head(xo, tgt_ref[...], fg_ref[...], loss_ref, dfg_ref)

    tile = _tile_of(t // tm, has_ctx)
    row = lambda i: (tile(i), 0)
    head_in = [] if head is None else [pl.BlockSpec((tm, d), row), _const((1, d))]
    head_shape = () if head is None else (jax.ShapeDtypeStruct((1, LANE), F32), jax.ShapeDtypeStruct((1, d), F32))
    head_out = () if head is None else (_const((1, LANE)), _const((1, d)))
    return _pcall(
        body, name=name, grid=(t // tm,), xchg=xchg, edges=_edges1(t // tm),
        out_shape=(jax.ShapeDtypeStruct((t, d), F32), jax.ShapeDtypeStruct((t, d), BF16),
                   jax.ShapeDtypeStruct((t, nb * w), BF16), jax.ShapeDtypeStruct((t, d), BF16)) + head_shape,
        in_specs=_row_specs(nx, tm, d) + [_mod_spec(d, has_ctx), _const((1, d)),
                                          _resident(w13.shape), _resident(w2.shape)] + head_in,
        out_specs=(pl.BlockSpec((tm, d), row), pl.BlockSpec((tm, d), row), pl.BlockSpec((tm, nb * w), row),
                   pl.BlockSpec((tm, d), row)) + head_out,
        compiler_params=_params(("arbitrary",)),
    )(*x, modtab, g, w13, w2, *(head or ()))


def _ffn_bwd(dy, modtab, h, ab, w13, w2, k0, has_ctx, name, pair0=0, npair=None, xchg=None):
    t, d = dy.shape
    _, w, _ = w13.shape
    fp = w // 2
    npair = w13.shape[0] // 2 if npair is None else npair
    nb = 2 * npair
    tm = _tile(has_ctx)
    nt = t // tm

    def body(dy_ref, mod_ref, h_ref, ab_ref, w13_ref, w2_ref, dh_ref, d13_ref, d2_ref, acc13, acc2):
        i = pl.program_id(1)

        @pl.when(i == 0)
        def _():
            acc13[...] = jnp.zeros_like(acc13)
            acc2[...] = jnp.zeros_like(acc2)

        gate = mod_ref[0, k0 + 2:k0 + 3, :]
        do = (dy_ref[...] * (0.5 * gate)).astype(BF16)
        dz = _dot_nt(do, w2_ref[0])
        hb = h_ref[...]
        dh = jnp.zeros((tm, d), F32)
        zs = []
        for q in range(2):
            ab = ab_ref[:, q * w:(q + 1) * w].astype(F32)
            a, b = ab[:, :fp], ab[:, fp:]
            sg = _sigmoid(a)
            sa = a * sg
            dzq = dz[:, q * fp:(q + 1) * fp]
            da = dzq * b * (sg * (1.0 + a * (1.0 - sg)))
            db = dzq * sa
            dab = jnp.concatenate([da, db], axis=1).astype(BF16)
            dh = dh + _dot(dab, w13_ref[q])
            acc13[q] += _dot_tn(dab, hb)
            zs.append((sa * b).astype(BF16))
        acc2[...] += _dot_tn(jnp.concatenate(zs, axis=1), do)
        dh_ref[0] = dh.astype(BF16)

        @pl.when(i == nt - 1)
        def _():
            d13_ref[...] = acc13[...].astype(BF16)
            d2_ref[0] = acc2[...].astype(BF16)

    mod_spec = _mod_spec(d, has_ctx)
    tile = _tile_of(nt, has_ctx)
    return _pcall(
        body, name=name, grid=(npair, nt), xchg=xchg, edges=_edges2(npair, nt),
        out_shape=(jax.ShapeDtypeStruct((npair, t, d), BF16), jax.ShapeDtypeStruct((nb, w, d), BF16),
                   jax.ShapeDtypeStruct((npair, 2 * fp, d), BF16)),
        in_specs=[pl.BlockSpec((tm, d), lambda p, i: (tile(i), 0)), mod_spec,
                  pl.BlockSpec((tm, d), lambda p, i: (tile(i), 0)),
                  pl.BlockSpec((tm, 2 * w), lambda p, i: (tile(i), pair0 + p)),
                  pl.BlockSpec((2, w, d), lambda p, i: (pair0 + p, 0, 0)),
                  pl.BlockSpec((1, 2 * fp, d), lambda p, i: (pair0 + p, 0, 0))],
        out_specs=(pl.BlockSpec((1, tm, d), lambda p, i: (p, tile(i), 0)),
                   pl.BlockSpec((2, w, d), lambda p, i: (p, 0, 0)),
                   pl.BlockSpec((1, 2 * fp, d), lambda p, i: (p, 0, 0))),
        scratch_shapes=[pltpu.VMEM((2, w, d), F32), pltpu.VMEM((2 * fp, d), F32)],
        compiler_params=_params(("arbitrary", "arbitrary")),
    )(dy, modtab, h, ab, w13, w2)


def _ffn_bwd_norm(dy, dhps, x, o, modtab, g, k0, has_ctx, name, xchg=None):
    t, d = dy.shape
    ngrp = 2 if has_ctx else 1
    tm = _tile(has_ctx)
    ndh, nx = len(dhps), len(x)
    lat0 = ngrp - 1

    def body(dy_ref, *rest):
        dhp_refs, x_refs = rest[:ndh], rest[ndh:ndh + nx]
        o_ref, mod_ref, g_ref, dx_ref, dmod_ref, dg_ref = rest[ndh + nx:]
        i = pl.program_id(0)

        @pl.when(i == 0)
        def _():
            dg_ref[...] = jnp.zeros_like(dg_ref)

        @pl.when((i == 0) | (i == ngrp - 1))
        def _():
            dmod_ref[...] = jnp.zeros_like(dmod_ref)

        dh = None
        for ref in dhp_refs:
            for p in range(ref.shape[0]):
                dh = ref[p].astype(F32) if dh is None else dh + ref[p].astype(F32)
        scale = mod_ref[0, k0 + 1:k0 + 2, :]
        gg = g_ref[...]
        rstd, xhat, xn, _ = _norm_fwd(_rows(x_refs), gg, 0.0, scale)
        dxn, dshift, dscale, dg = _norm_bwd(dh, rstd, xhat, xn, gg, scale)
        dyv = dy_ref[...]
        dx_ref[...] = dyv + dxn
        dmod_ref[0, 0:1, :] += dshift
        dmod_ref[0, 1:2, :] += dscale
        dmod_ref[0, 2:3, :] += _colsum(0.5 * dyv * o_ref[...].astype(F32))
        dg_ref[...] += dg

    tile = _tile_of(t // tm, has_ctx)
    row = lambda i: (tile(i), 0)
    return _pcall(
        body, name=name, grid=(t // tm,), xchg=xchg, edges=_edges1(t // tm),
        out_shape=(jax.ShapeDtypeStruct((t - lat0 * tm, d), F32), jax.ShapeDtypeStruct((ngrp, 3, d), F32),
                   jax.ShapeDtypeStruct((1, d), F32)),
        in_specs=[pl.BlockSpec((tm, d), row)]
        + [pl.BlockSpec((a.shape[0], tm, d), lambda i: (0, tile(i), 0)) for a in dhps]
        + _row_specs(nx, tm, d) + [pl.BlockSpec((tm, d), row), _mod_spec(d, has_ctx), _const((1, d))],
        out_specs=(pl.BlockSpec((tm, d), lambda i: (jnp.maximum(i - lat0, 0), 0)), _dmod_spec(3, d, has_ctx),
                   _const((1, d))),
        compiler_params=_params(("arbitrary",)),
    )(dy, *dhps, *x, o, modtab, g)


def _mix_in_fwd(x1, modtab, g, w_in):
    t, d = x1.shape
    hm = w_in.shape[1] // 2

    def body(x_ref, mod_ref, g_ref, w_ref, h_ref, up_ref, us_ref):
        _, _, _, h = _norm_fwd(x_ref[...], g_ref[...], mod_ref[0, 3:4, :], mod_ref[0, 4:5, :])
        hb = h.astype(BF16)
        h_ref[...] = hb
        u = _dot(hb, w_ref[...])
        up_ref[...] = u[:, :hm]
        us_ref[...] = u[:, hm:]

    tile = _tile_of(t // TM, True)
    row = lambda i: (tile(i), 0)
    return _pcall(
        body, name="mix_in_fwd", grid=(t // TM,),
        out_shape=(jax.ShapeDtypeStruct((t, d), BF16), jax.ShapeDtypeStruct((t, hm), F32),
                   jax.ShapeDtypeStruct((t, hm), F32)),
        in_specs=[pl.BlockSpec((TM, d), row), _mod_spec(d, True), _const((1, d)), _resident(w_in.shape)],
        out_specs=(pl.BlockSpec((TM, d), row), pl.BlockSpec((TM, hm), row), pl.BlockSpec((TM, hm), row)),
        compiler_params=_params(("arbitrary",)),
    )(x1, modtab, g, w_in)


def _mix_in_bwd(du_pool, du_f, du_b, dx2, h1, x1, modtab, g, w_in, lc):
    t, d = x1.shape
    m = w_in.shape[1]
    hm = m // 2
    nt = t // TM

    def body(dup_ref, duf_ref, dub_ref, dx2_ref, h_ref, x_ref, mod_ref, g_ref, w_ref, dx_ref, dw_ref, dmod_ref, dg_ref, acc):
        i = pl.program_id(0)

        @pl.when(i == 0)
        def _():
            acc[...] = jnp.zeros_like(acc)
            dg_ref[...] = jnp.zeros_like(dg_ref)

        @pl.when(i <= 1)
        def _():
            dmod_ref[...] = jnp.zeros_like(dmod_ref)

        lat = (i > 0).astype(F32)
        valid = (i > 0) | (lax.broadcasted_iota(jnp.int32, (TM, 1), 0) < lc)
        du_s5 = jnp.where(valid, duf_ref[...] + dub_ref[...], 0.0)
        du = jnp.concatenate([dup_ref[...] * lat, du_s5], axis=1).astype(BF16)
        dh = _dot_nt(du, w_ref[...])
        acc[...] += _dot_tn(h_ref[...], du)
        scale = mod_ref[0, 4:5, :]
        gg = g_ref[...]
        rstd, xhat, xn, _ = _norm_fwd(x_ref[...], gg, 0.0, scale)
        dxn, dshift, dscale, dg = _norm_bwd(dh, rstd, xhat, xn, gg, scale)
        dx_ref[...] = dx2_ref[...] * lat + dxn
        dmod_ref[0, 0:1, :] += dshift
        dmod_ref[0, 1:2, :] += dscale
        dg_ref[...] += dg

        @pl.when(i == nt - 1)
        def _():
            dw_ref[...] = acc[...].astype(BF16)

    tile = _tile_of(nt, True)
    row = lambda i: (tile(i), 0)
    lrow = lambda i: (jnp.maximum(i - 1, 0), 0)
    return _pcall(
        body, name="mix_in_bwd", grid=(nt,),
        out_shape=(jax.ShapeDtypeStruct((t, d), F32), jax.ShapeDtypeStruct((d, m), BF16),
                   jax.ShapeDtypeStruct((2, 2, d), F32), jax.ShapeDtypeStruct((1, d), F32)),
        in_specs=[pl.BlockSpec((TM, hm), lrow), pl.BlockSpec((TM, hm), row), pl.BlockSpec((TM, hm), row),
                  pl.BlockSpec((TM, d), lrow), pl.BlockSpec((TM, d), row), pl.BlockSpec((TM, d), row),
                  _mod_spec(d, True), _const((1, d)), _resident(w_in.shape)],
        out_specs=(pl.BlockSpec((TM, d), row), _const((d, m)), _dmod_spec(2, d, True), _const((1, d))),
        scratch_shapes=[pltpu.VMEM((d, m), F32)],
        compiler_params=_params(("arbitrary",)),
    )(du_pool, du_f, du_b, dx2, h1, x1, modtab, g, w_in)


def _pool(v, rows, transpose, name):
    l, n = rows * GRID_W, GRID_W * LANE
    ngrp = v.shape[1] // LANE
    nchunk = 4
    cw = n // nchunk

    def rowsum(val, lo, hi):
        ri = lax.broadcasted_iota(jnp.int32, (rows, rows), 0)
        ci = lax.broadcasted_iota(jnp.int32, (rows, rows), 1)
        sel = (((ci - ri) >= -lo) & ((ci - ri) <= hi)).astype(BF16)
        v1 = val.astype(BF16)
        r1 = val - v1.astype(F32)
        v2 = r1.astype(BF16)
        v3 = (r1 - v2.astype(F32)).astype(BF16)
        return _dot(sel, v1) + _dot(sel, v2) + _dot(sel, v3)

    def one(v_ref, o_ref, g_scr, r_scr, a_scr, win):
        for c in range(GRID_W):
            g_scr[:, c * LANE:(c + 1) * LANE] = v_ref[pl.ds(c, rows, stride=GRID_W), :]
        lo = win // 2
        hi = win - 1 - lo
        rlo, rhi = (hi, lo) if transpose else (lo, hi)
        ridx = lax.broadcasted_iota(jnp.int32, (rows, 1), 0)
        cnt_r = (jnp.minimum(ridx + hi + 1, rows) - jnp.maximum(ridx - lo, 0)).astype(F32)
        cidx = lax.broadcasted_iota(jnp.int32, (1, n), 1) // LANE
        cnt_c = (jnp.minimum(cidx + hi + 1, GRID_W) - jnp.maximum(cidx - lo, 0)).astype(F32)
        if not transpose:
            for k in range(nchunk):
                sl = slice(k * cw, (k + 1) * cw)
                r_scr[:, sl] = rowsum(g_scr[:, sl], rlo, rhi) / cnt_r
        else:
            r_scr[...] = g_scr[...] / cnt_c
        a_scr[...] = r_scr[...]
        for j in range(-rlo, rhi + 1):
            if j == 0:
                continue
            c0, c1 = max(0, -j), min(GRID_W, GRID_W - j)
            a_scr[:, c0 * LANE:c1 * LANE] += r_scr[:, (c0 + j) * LANE:(c1 + j) * LANE]
        if not transpose:
            r_scr[...] = a_scr[...] / cnt_c - g_scr[...]
        else:
            for k in range(nchunk):
                sl = slice(k * cw, (k + 1) * cw)
                r_scr[:, sl] = rowsum(a_scr[:, sl] / cnt_r, rlo, rhi) - g_scr[:, sl]
        for c in range(GRID_W):
            o_ref[pl.ds(c, rows, stride=GRID_W), :] = r_scr[:, c * LANE:(c + 1) * LANE]

    def body(v_ref, o_ref, g_scr, r_scr, a_scr):
        grp = pl.program_id(0)
        for k, win in enumerate(POOL_WINDOWS):
            @pl.when(grp == k)
            def _(win=win):
                one(v_ref, o_ref, g_scr, r_scr, a_scr, win)

    spec = pl.BlockSpec((l, LANE), lambda k: (0, k))
    return _pcall(
        body, name=name, grid=(ngrp,), out_shape=jax.ShapeDtypeStruct((l, ngrp * LANE), F32),
        in_specs=[spec], out_specs=spec,
        scratch_shapes=[pltpu.VMEM((rows, n), F32), pltpu.VMEM((rows, n), F32), pltpu.VMEM((rows, n), F32)],
        compiler_params=_params(("arbitrary",)),
    )(v)


def _cmul(ar, ai, br, bi):
    return ar * br - ai * bi, ar * bi + ai * br


def _s5_prep(a_re, a_im, log_dt, c_re, c_im, kvec):
    q, nc, p = c_re.shape

    def body(ar_ref, ai_ref, ldt_ref, cr_ref, ci_ref, k_ref, cpr_ref, cpi_ref, tr_ref, ti_ref):
        lr, li = ar_ref[...], ai_ref[...]
        dt = jnp.exp(ldt_ref[...])
        kk = k_ref[...]
        mag = jnp.exp(kk * (lr * dt))
        ph = kk * (li * dt)
        tr_ref[...] = mag * jnp.cos(ph)
        ti_ref[...] = mag * jnp.sin(ph)
        m1 = jnp.exp(lr * dt)
        abr, abi = m1 * jnp.cos(li * dt), m1 * jnp.sin(li * dt)
        den = lr * lr + li * li
        xr, xi = abr - 1.0, abi
        cfr, cfi = (xr * lr + xi * li) / den, (xi * lr - xr * li) / den
        pr, pi_ = _cmul(cr_ref[...], ci_ref[...], cfr, cfi)
        cpr_ref[...] = pr
        cpi_ref[...] = pi_

    return _pcall(
        body, name="s5_prep",
        out_shape=(jax.ShapeDtypeStruct((q, nc, p), F32), jax.ShapeDtypeStruct((q, nc, p), F32),
                   jax.ShapeDtypeStruct((q, NTAB, p), F32), jax.ShapeDtypeStruct((q, NTAB, p), F32)),
        compiler_params=_params(None),
    )(a_re, a_im, log_dt, c_re, c_im, kvec)


def _s5_param_bwd(a_re, a_im, log_dt, c_re, c_im, dab_r, dab_i, dcp_r, dcp_i):
    q, nc, p = c_re.shape

    def body(ar_ref, ai_ref, ldt_ref, cr_ref, ci_ref, dar_ref, dai_ref, dcr_ref, dci_ref,
             gar_ref, gai_ref, gdt_ref, gcr_ref, gci_ref):
        lr, li = ar_ref[...], ai_ref[...]
        dt = jnp.exp(ldt_ref[...])
        m1 = jnp.exp(lr * dt)
        abr, abi = m1 * jnp.cos(li * dt), m1 * jnp.sin(li * dt)
        den = lr * lr + li * li
        xr, xi = abr - 1.0, abi
        cfr, cfi = (xr * lr + xi * li) / den, (xi * lr - xr * li) / den
        cr, ci = cr_ref[...], ci_ref[...]
        dcr, dci = dcr_ref[...], dci_ref[...]
        gcr, gci = _cmul(dcr, dci, cfr, -cfi)
        gcr_ref[...] = gcr
        gci_ref[...] = gci
        t_r, t_i = _cmul(cr, -ci, dcr, dci)
        dcf_r = jnp.sum(t_r, axis=1, keepdims=True)
        dcf_i = jnp.sum(t_i, axis=1, keepdims=True)
        ilr, ili = lr / den, -li / den
        u_r, u_i = _cmul(dcf_r, dcf_i, ilr, -ili)
        dab_r_, dab_i_ = dar_ref[...] + u_r, dai_ref[...] + u_i
        v_r, v_i = _cmul(dab_r_, dab_i_, abr, -abi)
        cl_r, cl_i = _cmul(cfr, cfi, ilr, ili)
        w_r, w_i = _cmul(dcf_r, dcf_i, cl_r, -cl_i)
        gar_ref[...] = v_r * dt - w_r
        gai_ref[...] = v_i * dt - w_i
        la_r, la_i = _cmul(lr, li, abr, abi)
        ddt = la_r * dab_r_ + la_i * dab_i_
        gdt_ref[...] = dt * jnp.sum(ddt, axis=2, keepdims=True)

    return _pcall(
        body, name="s5_param_bwd",
        out_shape=(jax.ShapeDtypeStruct((q, 1, p), F32), jax.ShapeDtypeStruct((q, 1, p), F32),
                   jax.ShapeDtypeStruct((q, 1, p), F32), jax.ShapeDtypeStruct((q, nc, p), F32),
                   jax.ShapeDtypeStruct((q, nc, p), F32)),
        compiler_params=_params(None),
    )(a_re, a_im, log_dt, c_re, c_im, dab_r, dab_i, dcp_r, dcp_i)


def _bcast8(row, c):
    return jnp.broadcast_to(row, (8, c))


def _slab(j):
    return j * 8 if isinstance(j, int) else pl.multiple_of(j * 8, 8)


def _scan_head(xr, xi, tre_ref, tim_ref, car_r, car_i, desc, conj, cs):
    sg = -1.0 if conj else 1.0
    ar = _bcast8(tre_ref[0:1, :], cs)
    ai = sg * _bcast8(tim_ref[0:1, :], cs)

    def p1(jj, carry):
        hr, hi = carry
        off = _slab(SEG - 1 - jj if desc else jj)
        pr, pi_ = _cmul(ar, ai, hr, hi)
        nr = pr + xr[pl.ds(off, 8), :]
        ni = pi_ + xi[pl.ds(off, 8), :]
        xr[pl.ds(off, 8), :] = nr
        xi[pl.ds(off, 8), :] = ni
        return nr, ni

    zero = jnp.zeros((8, cs), F32)
    ir, ii = lax.fori_loop(0, SEG, p1, (zero, zero), unroll=S5_UNROLL)
    cin_r, cin_i = car_r[...], car_i[...]
    rows = lax.broadcasted_iota(jnp.int32, (8, cs), 0)
    for s, krow in ((1, SEG), (2, SEG + 1), (4, SEG + 3)):
        keep, sh = (rows < 8 - s, 8 - s) if desc else (rows >= s, s)
        sr = jnp.where(keep, pltpu.roll(ir, sh, 0), 0.0)
        si = jnp.where(keep, pltpu.roll(ii, sh, 0), 0.0)
        pr, pi_ = _cmul(tre_ref[krow:krow + 1, :], sg * tim_ref[krow:krow + 1, :], sr, si)
        ir, ii = ir + pr, ii + pi_
    q0 = SEG + 8 if desc else SEG
    pr, pi_ = _cmul(tre_ref[q0:q0 + 8, :], sg * tim_ref[q0:q0 + 8, :], cin_r, cin_i)
    fr, fi = ir + pr, ii + pi_
    keep, sh, edge = (rows < 7, 7, 0) if desc else (rows >= 1, 1, 7)
    cs_r = jnp.where(keep, pltpu.roll(fr, sh, 0), cin_r)
    cs_i = jnp.where(keep, pltpu.roll(fi, sh, 0), cin_i)
    car_r[...] = _bcast8(fr[edge:edge + 1, :], cs)
    car_i[...] = _bcast8(fi[edge:edge + 1, :], cs)
    return cs_r, cs_i, cin_r, cin_i


def _pow_row(tre_ref, tim_ref, j, desc, conj, cs):
    k = SEG - 1 - j if desc else j
    sg = -1.0 if conj else 1.0
    return _bcast8(tre_ref[pl.ds(k, 1), :], cs), sg * _bcast8(tim_ref[pl.ds(k, 1), :], cs)


def _to_segments(val, dst):
    for r in range(8):
        dst[pl.ds(r, SEG, stride=8), :] = val[r * SEG:(r + 1) * SEG, :]


def _from_segments(src):
    return jnp.concatenate([src[pl.ds(r, SEG, stride=8), :] for r in range(8)], axis=0)


def _s5_fwd(up, bre, bim, cmr, cmi, tre, tim, dskip, order, nproc, desc, with_skip, name):
    t, cu = up.shape
    nsb = bre.shape[0]
    cb = cu // nsb
    cs = bre.shape[2]
    nch = t // TC

    def body(u_ref, bre_ref, bim_ref, cmr_ref, cmi_ref, tre_ref, tim_ref, dsk_ref,
             y_ref, hr_ref, hi_ref, cinr_ref, cini_ref, car_r, car_i, seg_scr):
        i = pl.program_id(0)

        @pl.when(i == 0)
        def _():
            car_r[...] = jnp.zeros_like(car_r)
            car_i[...] = jnp.zeros_like(car_i)

        for sb in range(nsb):
            col, ch = pl.ds(sb * cs, cs), slice(sb * cb, (sb + 1) * cb)
            hr_v, hi_v = hr_ref.at[:, col], hi_ref.at[:, col]
            tre_v, tim_v = tre_ref.at[:, col], tim_ref.at[:, col]
            useg, yseg = seg_scr.at[0, sb], seg_scr.at[1, sb]
            _to_segments(u_ref[:, ch], useg)
            u = useg[...]
            ub = u.astype(BF16)
            hr_v[...] = _dot(ub, bre_ref[sb])
            hi_v[...] = _dot(ub, bim_ref[sb])
            cs_r, cs_i, cin_r, cin_i = _scan_head(hr_v, hi_v, tre_v, tim_v, car_r.at[sb], car_i.at[sb], desc, False, cs)
            cinr_ref[:, col] = cin_r
            cini_ref[:, col] = cin_i

            def p2(j, _, hr_v=hr_v, hi_v=hi_v, tre_v=tre_v, tim_v=tim_v, cs_r=cs_r, cs_i=cs_i):
                off = _slab(j)
                pw_r, pw_i = _pow_row(tre_v, tim_v, j, desc, False, cs)
                pr, pi_ = _cmul(pw_r, pw_i, cs_r, cs_i)
                hr_v[pl.ds(off, 8), :] = hr_v[pl.ds(off, 8), :] + pr
                hi_v[pl.ds(off, 8), :] = hi_v[pl.ds(off, 8), :] + pi_
                return 0

            lax.fori_loop(0, SEG, p2, 0, unroll=S5_UNROLL)
            y = _dot(hr_v[...].astype(BF16), cmr_ref[sb]) - _dot(hi_v[...].astype(BF16), cmi_ref[sb])
            if with_skip:
                y = y + dsk_ref[:, ch] * u
            yseg[...] = y
            y_ref[:, ch] = _from_segments(yseg)

    blk = lambda i: (order(i), 0)
    return _pcall(
        body, name=name, grid=(nproc,),
        out_shape=(jax.ShapeDtypeStruct((t, cu), F32), jax.ShapeDtypeStruct((t, nsb * cs), F32),
                   jax.ShapeDtypeStruct((t, nsb * cs), F32), jax.ShapeDtypeStruct((nch * 8, nsb * cs), F32),
                   jax.ShapeDtypeStruct((nch * 8, nsb * cs), F32)),
        in_specs=[pl.BlockSpec((TC, cu), blk), _const(bre.shape), _const(bim.shape), _const(cmr.shape),
                  _const(cmi.shape), _const(tre.shape), _const(tim.shape), _const(dskip.shape)],
        out_specs=(pl.BlockSpec((TC, cu), blk), pl.BlockSpec((TC, nsb * cs), blk), pl.BlockSpec((TC, nsb * cs), blk),
                   pl.BlockSpec((8, nsb * cs), blk), pl.BlockSpec((8, nsb * cs), blk)),
        scratch_shapes=[pltpu.VMEM((nsb, 8, cs), F32), pltpu.VMEM((nsb, 8, cs), F32),
                        pltpu.VMEM((2, nsb, TC, cb), F32)],
        compiler_params=_params(("arbitrary",)),
    )(up, bre, bim, cmr, cmi, tre, tim, dskip)


def _s5_bwd(dy_lat, up, hr, hi, cinr, cini, bre, bim, ctr, cti, tre, tim, dskip, order, nproc, desc, with_skip, name,
            xchg=None):
    t, cu = up.shape
    nsb = bre.shape[0]
    cb = cu // nsb
    cs = bre.shape[2]
    lch = dy_lat.shape[0] // TC
    adesc = not desc

    def body(dy_ref, u_ref, hr_ref, hi_ref, cinr_ref, cini_ref, bre_ref, bim_ref, ctr_ref, cti_ref, tre_ref, tim_ref,
             dsk_ref, du_ref, dar_ref, dai_ref, dcr_ref, dci_ref, dbr_ref, dbi_ref, dd_ref,
             car_r, car_i, mr_all, mi_all, acc_r, acc_i, seg_scr):
        i = pl.program_id(0)
        latent = (order(i) < lch).astype(F32)

        @pl.when(i == 0)
        def _():
            for ref in (car_r, car_i, acc_r, acc_i, dcr_ref, dci_ref, dbr_ref, dbi_ref, dd_ref):
                ref[...] = jnp.zeros_like(ref)

        rows = lax.broadcasted_iota(jnp.int32, (8, cs), 0)
        for sb in range(nsb):
            col, ch = pl.ds(sb * cs, cs), slice(sb * cb, (sb + 1) * cb)
            hr_v, hi_v = hr_ref.at[:, col], hi_ref.at[:, col]
            tre_v, tim_v = tre_ref.at[:, col], tim_ref.at[:, col]
            mr, mi = mr_all.at[sb % 2], mi_all.at[sb % 2]
            useg, dyseg, duseg = seg_scr.at[0, sb], seg_scr.at[1, sb], seg_scr.at[2, sb]
            _to_segments(dy_ref[:, ch] * latent, dyseg)
            _to_segments(u_ref[:, ch], useg)
            dy = dyseg[...]
            dyb = dy.astype(BF16)
            u = useg[...]
            ub = u.astype(BF16)
            mr[...] = _dot(dyb, ctr_ref[sb])
            mi[...] = -_dot(dyb, cti_ref[sb])
            cs_r, cs_i, _, _ = _scan_head(mr, mi, tre_v, tim_v, car_r.at[sb], car_i.at[sb], adesc, True, cs)

            def fix(j, hp_r, hp_i, acc, mr=mr, mi=mi, tre_v=tre_v, tim_v=tim_v, cs_r=cs_r, cs_i=cs_i):
                off = _slab(j)
                pw_r, pw_i = _pow_row(tre_v, tim_v, j, adesc, True, cs)
                pr, pi_ = _cmul(pw_r, pw_i, cs_r, cs_i)
                m_r = mr[pl.ds(off, 8), :] + pr
                m_i = mi[pl.ds(off, 8), :] + pi_
                mr[pl.ds(off, 8), :] = m_r
                mi[pl.ds(off, 8), :] = m_i
                a_r, a_i = acc
                return a_r + hp_r * m_r + hp_i * m_i, a_i + hp_r * m_i - hp_i * m_r

            edge_j, src, keep, sh = (SEG - 1, 0, rows < 7, 7) if desc else (0, (SEG - 1) * 8, rows >= 1, 1)
            h0r = jnp.where(keep, pltpu.roll(hr_v[src:src + 8, :], sh, 0), cinr_ref[:, col])
            h0i = jnp.where(keep, pltpu.roll(hi_v[src:src + 8, :], sh, 0), cini_ref[:, col])
            acc = fix(edge_j, h0r, h0i, (acc_r[sb], acc_i[sb]))

            def p2(jj, acc, fix=fix, hr_v=hr_v, hi_v=hi_v):
                j = jj if desc else jj + 1
                offp = _slab(j + 1 if desc else j - 1)
                return fix(j, hr_v[pl.ds(offp, 8), :], hi_v[pl.ds(offp, 8), :], acc)

            a_r, a_i = lax.fori_loop(0, SEG - 1, p2, acc, unroll=S5_UNROLL)
            acc_r[sb] = a_r
            acc_i[sb] = a_i

            mrb, mib = mr[...].astype(BF16), mi[...].astype(BF16)
            du = _dot_nt(mrb, bre_ref[sb]) + _dot_nt(mib, bim_ref[sb])
            if with_skip:
                du = du + dsk_ref[:, ch] * dy
                dd_ref[:, ch] += _colsum(dy * u)
            duseg[...] = du
            du_ref[:, ch] = _from_segments(duseg)
            dbr_ref[sb] += _dot_tn(ub, mrb)
            dbi_ref[sb] += _dot_tn(ub, mib)
            dcr_ref[sb] += _dot_tn(dyb, hr_v[...].astype(BF16))
            dci_ref[sb] -= _dot_tn(dyb, hi_v[...].astype(BF16))

            @pl.when(i == nproc - 1)
            def _(sb=sb, a_r=a_r, a_i=a_i):
                dar_ref[sb] = _colsum(a_r)
                dai_ref[sb] = _colsum(a_i)

    blk = lambda i: (order(i), 0)
    blk_dy = lambda i: (jnp.minimum(order(i), lch - 1), 0)
    mat = (nsb, cb, cs)
    return _pcall(
        body, name=name, grid=(nproc,), xchg=xchg, edges=_edges1(nproc),
        out_shape=(jax.ShapeDtypeStruct((t, cu), F32),
                   jax.ShapeDtypeStruct((nsb, 1, cs), F32), jax.ShapeDtypeStruct((nsb, 1, cs), F32),
                   jax.ShapeDtypeStruct(mat, F32), jax.ShapeDtypeStruct(mat, F32),
                   jax.ShapeDtypeStruct(mat, F32), jax.ShapeDtypeStruct(mat, F32),
                   jax.ShapeDtypeStruct((1, cu), F32)),
        in_specs=[pl.BlockSpec((TC, cu), blk_dy), pl.BlockSpec((TC, cu), blk), pl.BlockSpec((TC, nsb * cs), blk),
                  pl.BlockSpec((TC, nsb * cs), blk), pl.BlockSpec((8, nsb * cs), blk), pl.BlockSpec((8, nsb * cs), blk),
                  _const(mat), _const(mat), _const(mat), _const(mat), _const(tre.shape), _const(tim.shape),
                  _const((1, cu))],
        out_specs=(pl.BlockSpec((TC, cu), blk), _const((nsb, 1, cs)), _const((nsb, 1, cs)),
                   _const(mat), _const(mat), _const(mat), _const(mat), _const((1, cu))),
        scratch_shapes=[pltpu.VMEM((nsb, 8, cs), F32), pltpu.VMEM((nsb, 8, cs), F32), pltpu.VMEM((2, TC, cs), F32),
                        pltpu.VMEM((2, TC, cs), F32), pltpu.VMEM((nsb, 8, cs), F32), pltpu.VMEM((nsb, 8, cs), F32),
                        pltpu.VMEM((3, nsb, TC, cb), F32)],
        compiler_params=_params(("arbitrary",)),
    )(dy_lat, up, hr, hi, cinr, cini, bre, bim, ctr, cti, tre, tim, dskip)


def _gelu(x):
    k = math.sqrt(2.0 / math.pi)
    return 0.5 * x * (1.0 + jnp.tanh(k * (x + 0.044715 * (x * x * x))))


def _gelu_grad(x):
    k = math.sqrt(2.0 / math.pi)
    th = jnp.tanh(k * (x + 0.044715 * (x * x * x)))
    return 0.5 * (1.0 + th) + 0.5 * x * (1.0 - th * th) * (k * (1.0 + 3.0 * 0.044715 * (x * x)))


def _mix_out_fwd(yf, yb, dpool, x1, modtab, w_pool, pscale, w_glu, w_out):
    l, hm = dpool.shape
    d = x1.shape[1]

    def body(yf_ref, yb_ref, dp_ref, x_ref, mod_ref, wp_ref, ps_ref, wg_ref, wo_ref, x2_ref, yp_ref, cat_ref, mix_ref):
        ypre = yf_ref[...] + yb_ref[...]
        yp_ref[...] = ypre
        yg = _gelu(ypre)
        y2 = yg * _sigmoid(_dot(yg.astype(BF16), wg_ref[...]))
        po = _dot(dp_ref[...].astype(BF16), wp_ref[...]) * ps_ref[...]
        cat = jnp.concatenate([po, y2], axis=1).astype(BF16)
        cat_ref[...] = cat
        mix = _dot(cat, wo_ref[...])
        mix_ref[...] = mix.astype(BF16)
        x2_ref[...] = x_ref[...] + mod_ref[0, 5:6, :] * mix

    row = lambda i: (i, 0)
    lrow = row
    return _pcall(
        body, name="mix_out_fwd", grid=(l // TM,),
        out_shape=(jax.ShapeDtypeStruct((l, d), F32), jax.ShapeDtypeStruct((l, hm), F32),
                   jax.ShapeDtypeStruct((l, d), BF16), jax.ShapeDtypeStruct((l, d), BF16)),
        in_specs=[pl.BlockSpec((TM, hm), lrow), pl.BlockSpec((TM, hm), lrow), pl.BlockSpec((TM, hm), row),
                  pl.BlockSpec((TM, d), lrow), _mod_spec(d, False), _const(w_pool.shape), _const((1, hm)),
                  _const(w_glu.shape), _resident(w_out.shape)],
        out_specs=(pl.BlockSpec((TM, d), row), pl.BlockSpec((TM, hm), row), pl.BlockSpec((TM, d), row),
                   pl.BlockSpec((TM, d), row)),
        compiler_params=_params(("arbitrary",)),
    )(yf, yb, dpool, x1, modtab, w_pool, pscale, w_glu, w_out)


def _mix_out_bwd(dx2, mix, cat, ypre, dpool, modtab, w_pool, pscale, w_glu, w_out):
    l, hm = dpool.shape
    d = dx2.shape[1]
    nt = l // TML

    def body(dx_ref, mix_ref, cat_ref, yp_ref, dp_ref, mod_ref, wp_ref, ps_ref, wg_ref, wo_ref,
             dyp_ref, ddp_ref, dwo_ref, dwg_ref, dwp_ref, dps_ref, dgate_ref, acc_o, acc_g):
        i = pl.program_id(0)

        @pl.when(i == 0)
        def _():
            for ref in (acc_o, acc_g, dwp_ref, dps_ref, dgate_ref):
                ref[...] = jnp.zeros_like(ref)

        dx = dx_ref[...]
        dgate_ref[...] += _colsum(dx * mix_ref[...].astype(F32))
        dmix = (dx * mod_ref[0, 5:6, :]).astype(BF16)
        dcat = _dot_nt(dmix, wo_ref[...])
        acc_o[...] += _dot_tn(cat_ref[...], dmix)
        dpo, dy2 = dcat[:, :hm], dcat[:, hm:]
        dpb = dp_ref[...].astype(BF16)
        pp = _dot(dpb, wp_ref[...])
        dps_ref[...] += _colsum(dpo * pp)
        dpp = (dpo * ps_ref[...]).astype(BF16)
        ddp_ref[...] = _dot_nt(dpp, wp_ref[...])
        dwp_ref[...] += _dot_tn(dpb, dpp)
        ypre = yp_ref[...]
        yg = _gelu(ypre)
        ygb = yg.astype(BF16)
        s = _sigmoid(_dot(ygb, wg_ref[...]))
        dq = (dy2 * yg * s * (1.0 - s)).astype(BF16)
        dyg = dy2 * s + _dot_nt(dq, wg_ref[...])
        acc_g[...] += _dot_tn(ygb, dq)
        dyp_ref[...] = dyg * _gelu_grad(ypre)

        @pl.when(i == nt - 1)
        def _():
            dwo_ref[...] = acc_o[...].astype(BF16)
            dwg_ref[...] = acc_g[...].astype(BF16)

    row = lambda i: (i, 0)
    return _pcall(
        body, name="mix_out_bwd", grid=(nt,),
        out_shape=(jax.ShapeDtypeStruct((l, hm), F32), jax.ShapeDtypeStruct((l, hm), F32),
                   jax.ShapeDtypeStruct((d, d), BF16), jax.ShapeDtypeStruct((hm, hm), BF16),
                   jax.ShapeDtypeStruct((hm, hm), F32), jax.ShapeDtypeStruct((1, hm), F32),
                   jax.ShapeDtypeStruct((1, d), F32)),
        in_specs=[pl.BlockSpec((TML, d), row), pl.BlockSpec((TML, d), row), pl.BlockSpec((TML, d), row),
                  pl.BlockSpec((TML, hm), row), pl.BlockSpec((TML, hm), row), _mod_spec(d, False),
                  _const(w_pool.shape), _const((1, hm)), _const(w_glu.shape), _resident(w_out.shape)],
        out_specs=(pl.BlockSpec((TML, hm), row), pl.BlockSpec((TML, hm), row), _const((d, d)), _const((hm, hm)),
                   _const((hm, hm)), _const((1, hm)), _const((1, d))),
        scratch_shapes=[pltpu.VMEM((d, d), F32), pltpu.VMEM((hm, hm), F32)],
        compiler_params=_params(("arbitrary",)),
    )(dx2, mix, cat, ypre, dpool, modtab, w_pool, pscale, w_glu, w_out)


def _loss_head(xx, target, gg, loss_ref, dg_ref):
    d = xx.shape[-1]
    rstd = lax.rsqrt(jnp.mean(xx * xx, axis=-1, keepdims=True) + EPS)
    xhat = xx * rstd
    err = xhat * gg - target
    row_loss = jnp.mean(err * err, axis=-1, keepdims=True)
    loss_ref[...] += 0.5 * jnp.sum(row_loss, axis=0, keepdims=True)
    dout = err * (1.0 / d)
    dg_ref[...] += _colsum(dout * xhat)
    dxhat = dout * gg
    return rstd * (dxhat - xhat * jnp.mean(dxhat * xhat, axis=-1, keepdims=True))


def _adamw(parts, w, m, v, name):
    shape = w.shape
    c = shape[-1]
    r = int(np.prod(shape)) // c
    npart = parts.shape[0]
    tr = r
    for cand in (1024, 512, 256, 128, 64, 32, 16):
        if r % cand == 0 and cand * max(c, LANE) * 4 <= ADAM_BLOCK_BYTES:
            tr = cand
            break
    c1 = 1.0 - ADAM_B1
    c2 = 1.0 - ADAM_B2
    bc1 = 1.0 - ADAM_B1 ** ADAM_STEP
    bc2 = 1.0 - ADAM_B2 ** ADAM_STEP

    def body(p_ref, w_ref, m_ref, v_ref, g_ref, d_ref, mo_ref, vo_ref):
        g = p_ref[0].astype(F32)
        for k in range(1, npart):
            g = g + p_ref[k].astype(F32)
        mn = ADAM_B1 * m_ref[...] + c1 * g
        vn = ADAM_B2 * v_ref[...] + c2 * (g * g)
        m_hat = mn / bc1
        v_hat = vn / bc2
        g_ref[...] = g
        mo_ref[...] = mn
        vo_ref[...] = vn
        d_ref[...] = -ADAM_LR * (m_hat / (jnp.sqrt(v_hat) + ADAM_EPS) + ADAM_WD * w_ref[...])

    spec = pl.BlockSpec((tr, c), lambda i: (i, 0))
    outs = _pcall(
        body, name=name, grid=(r // tr,),
        out_shape=tuple(jax.ShapeDtypeStruct((r, c), F32) for _ in range(4)),
        in_specs=[pl.BlockSpec((npart, tr, c), lambda i: (0, i, 0)), spec, spec, spec],
        out_specs=(spec, spec, spec, spec),
        compiler_params=_params(("arbitrary",)),
    )(parts.reshape(npart, r, c), w.reshape(r, c), m.reshape(r, c), v.reshape(r, c))
    return tuple(o.reshape(shape) for o in outs)


def _adamw_ffn(parts, row0, w, m, v, name):
    _, nl, r, c = w.shape
    rp = parts[0][0].shape[1]
    tc = ADAM_FFN_LANES
    grid = (c // tc,)
    p_spec = lambda a: pl.BlockSpec((a.shape[0], rp, tc), lambda i: (0, 0, i))
    w_spec = pl.BlockSpec((1, nl, r, tc), lambda i: (0, 0, 0, i))
    take = lambda ref, k: ref[k, row0:row0 + r, :]
    flat = [a for layer in parts for a in layer]
    counts = [len(layer) for layer in parts]
    c1 = 1.0 - ADAM_B1
    c2 = 1.0 - ADAM_B2
    bc1 = 1.0 - ADAM_B1 ** ADAM_STEP
    bc2 = 1.0 - ADAM_B2 ** ADAM_STEP

    def body(*refs):
        w_ref, m_ref, v_ref, g_ref, d_ref, mo_ref, vo_ref = refs[len(flat):]
        first = 0
        for lyr in range(nl):
            g = None
            for ref in refs[first:first + counts[lyr]]:
                for k in range(ref.shape[0]):
                    term = take(ref, k).astype(F32)
                    g = term if g is None else g + term
            first += counts[lyr]
            mn = ADAM_B1 * m_ref[0, lyr] + c1 * g
            vn = ADAM_B2 * v_ref[0, lyr] + c2 * (g * g)
            g_ref[0, lyr] = g
            mo_ref[0, lyr] = mn
            vo_ref[0, lyr] = vn
            d_ref[0, lyr] = -ADAM_LR * ((mn / bc1) / (jnp.sqrt(vn / bc2) + ADAM_EPS) + ADAM_WD * w_ref[0, lyr])

    return _pcall(
        body, name=name, grid=grid, out_shape=tuple(jax.ShapeDtypeStruct(w.shape, F32) for _ in range(4)),
        in_specs=[p_spec(a) for a in flat] + [w_spec] * 3, out_specs=(w_spec,) * 4,
        compiler_params=_params(("arbitrary",)),
    )(*flat, w, m, v)


def _blockdiag(xb):
    n, a, b = xb.shape[-3:]
    eye = jnp.eye(n, dtype=xb.dtype)
    out = xb[..., :, :, None, :] * eye[:, None, :, None]
    return out.reshape(xb.shape[:-3] + (n * a, n * b))


def _diagblocks(mat, n):
    lead = mat.shape[:-2]
    a, b = mat.shape[-2] // n, mat.shape[-1] // n
    m5 = mat.reshape(lead + (n, a, n, b))
    nl = len(lead)
    dg = jnp.diagonal(m5, axis1=nl, axis2=nl + 2)
    return jnp.moveaxis(dg, -1, nl)


def _pad128(a):
    flat = a.reshape(-1)
    pad = (-flat.shape[0]) % LANE
    return jnp.pad(flat, (0, pad)) if pad else flat


def kernel(x, c, ctx, c_ctx, norm_g, w_ada, b_ada, ffn_w1, ffn_w3, ffn_w2, w_in, pool_w, pool_scale, s5_a_re, s5_a_im, s5_log_dt, s5_b_re, s5_b_im, s5_c_re, s5_c_im, s5_d, s5_w_glu, w_out, final_g, loss_target, m_c_ctx, m_norm_g, m_w_ada, m_b_ada, m_ffn_w1, m_ffn_w3, m_ffn_w2, m_w_in, m_pool_w, m_pool_scale, m_s5_a_re, m_s5_a_im, m_s5_log_dt, m_s5_b_re, m_s5_b_im, m_s5_c_re, m_s5_c_im, m_s5_d, m_s5_w_glu, m_w_out, m_final_g, v_c_ctx, v_norm_g, v_w_ada, v_b_ada, v_ffn_w1, v_ffn_w3, v_ffn_w2, v_w_in, v_pool_w, v_pool_scale, v_s5_a_re, v_s5_a_im, v_s5_log_dt, v_s5_b_re, v_s5_b_im, v_s5_c_re, v_s5_c_im, v_s5_d, v_s5_w_glu, v_w_out, v_final_g):
    weights = dict(c_ctx=c_ctx, norm_g=norm_g, w_ada=w_ada, b_ada=b_ada, ffn_w1=ffn_w1, ffn_w3=ffn_w3, ffn_w2=ffn_w2,
                   w_in=w_in, pool_w=pool_w, pool_scale=pool_scale, s5_a_re=s5_a_re, s5_a_im=s5_a_im,
                   s5_log_dt=s5_log_dt, s5_b_re=s5_b_re, s5_b_im=s5_b_im, s5_c_re=s5_c_re, s5_c_im=s5_c_im, s5_d=s5_d,
                   s5_w_glu=s5_w_glu, w_out=w_out, final_g=final_g)
    mom_m = dict(c_ctx=m_c_ctx, norm_g=m_norm_g, w_ada=m_w_ada, b_ada=m_b_ada, ffn_w1=m_ffn_w1, ffn_w3=m_ffn_w3,
                 ffn_w2=m_ffn_w2, w_in=m_w_in, pool_w=m_pool_w, pool_scale=m_pool_scale, s5_a_re=m_s5_a_re,
                 s5_a_im=m_s5_a_im, s5_log_dt=m_s5_log_dt, s5_b_re=m_s5_b_re, s5_b_im=m_s5_b_im, s5_c_re=m_s5_c_re,
                 s5_c_im=m_s5_c_im, s5_d=m_s5_d, s5_w_glu=m_s5_w_glu, w_out=m_w_out, final_g=m_final_g)
    mom_v = dict(c_ctx=v_c_ctx, norm_g=v_norm_g, w_ada=v_w_ada, b_ada=v_b_ada, ffn_w1=v_ffn_w1, ffn_w3=v_ffn_w3,
                 ffn_w2=v_ffn_w2, w_in=v_w_in, pool_w=v_pool_w, pool_scale=v_pool_scale, s5_a_re=v_s5_a_re,
                 s5_a_im=v_s5_a_im, s5_log_dt=v_s5_log_dt, s5_b_re=v_s5_b_re, s5_b_im=v_s5_b_im, s5_c_re=v_s5_c_re,
                 s5_c_im=v_s5_c_im, s5_d=v_s5_d, s5_w_glu=v_s5_w_glu, w_out=v_w_out, final_g=v_final_g)

    l, d = x.shape[1], x.shape[2]
    lc = ctx.shape[1]
    t = l + TM
    rows = l // GRID_W
    fblk = ffn_w1.shape[-1]
    ngrp, gp = s5_a_re.shape[2], s5_a_re.shape[3]
    gc = s5_b_re.shape[3]
    hm = ngrp * gc
    nsb = 4
    gsb = ngrp // nsb
    assert lc == TC and TM % TC == 0 and l % TM == 0 and TM == TML and hm == 4 * LANE and ngrp * gp == nsb * 512
    me = 4 * lax.axis_index("x") + 2 * lax.axis_index("y") + lax.axis_index("c")

    padr = ((0, FPAD - fblk), (0, 0))
    w1_t, w3_t = jnp.swapaxes(ffn_w1, 2, 3), jnp.swapaxes(ffn_w3, 2, 3)
    w13_loc = [jnp.concatenate([jnp.pad(w1_t[0, k], padr), jnp.pad(w3_t[0, k], padr)], axis=0).astype(BF16)
               for k in range(2)]
    w2_loc = [jnp.pad(ffn_w2[0, k], padr).astype(BF16) for k in range(2)]
    gath = _exchange([(w13_loc[0], GATHER_REL), (w2_loc[0], GATHER_REL), (c, GATHER), (norm_g[0], GATHER)],
                     "exchange_weights")
    w13_0 = gath[0]
    w2_0 = gath[1].reshape(NDEV // 2, 2 * FPAD, d)
    c_all = gath[2].reshape(NDEV, d)
    g_all = gath[3].transpose(1, 0, 2).reshape(3, d)

    wa = w_ada.shape[-1]
    c16 = jnp.concatenate([c_all, c_ctx.reshape(1, d), jnp.zeros((7, d), F32)], axis=0)
    b_blk = lax.dynamic_slice_in_dim(b_ada, me * wa, wa, axis=1)
    mod_sl = _ada_fwd(c16, w_ada[0], b_blk)
    gath = _exchange([(mod_sl, GATHER)], "exchange_mod")
    mod_full = gath[0].transpose(1, 0, 2).reshape(16, NDEV * wa)
    mod_l = lax.dynamic_index_in_dim(mod_full, me, axis=0, keepdims=False).reshape(9, d)
    modtab = jnp.stack([mod_full[8].reshape(9, d), mod_l])

    x0 = [jnp.pad(ctx[0], ((0, TM - lc), (0, 0))), x[0]]
    g0, g1, g2 = g_all[0:1], g_all[1:2], g_all[2:3]
    x1, h0, ab0, o0, w13_1, w2_1, w_in_f, w_out_f, w_glu_f = _ffn_fwd(
        x0, modtab, g0, w13_0, w2_0, 0, True, "ffn0_fwd",
        xchg=[(w13_loc[1], GATHER_REL), (w2_loc[1], GATHER_REL), (w_in[0].astype(BF16), GATHER),
              (w_out[0].astype(BF16), GATHER), (s5_w_glu[0].astype(BF16), GATHER)])
    w2_1 = w2_1.reshape(NDEV // 2, 2 * FPAD, d)
    w_in_f = w_in_f.reshape(d, -1)
    w_out_f = w_out_f.reshape(-1, d)
    w_glu_f = w_glu_f.reshape(hm, hm)
    h1, u_pool, up = _mix_in_fwd(x1, modtab, g1, w_in_f)

    dpool = _pool(u_pool, rows, False, "pool_fwd")
    w_pool_bd = _blockdiag(pool_w[0]).astype(BF16)

    q = 2 * ngrp
    kvec = np.concatenate([np.arange(1, SEG + 1), SEG * np.arange(1, 9), SEG * np.arange(8, 0, -1)]).astype(np.float32)
    a_re3, a_im3 = s5_a_re[0].reshape(q, 1, gp), s5_a_im[0].reshape(q, 1, gp)
    ldt3 = jnp.broadcast_to(s5_log_dt[0].reshape(q, 1, 1), (q, 1, gp))
    c_re3, c_im3 = s5_c_re[0].reshape(q, gc, gp), s5_c_im[0].reshape(q, gc, gp)
    cp_re, cp_im, tab_re, tab_im = _s5_prep(a_re3, a_im3, ldt3, c_re3, c_im3, jnp.broadcast_to(jnp.asarray(kvec).reshape(1, NTAB, 1), (1, NTAB, gp)))
    tabs = [(tab_re.reshape(2, ngrp, NTAB, gp)[k].transpose(1, 0, 2).reshape(NTAB, ngrp * gp),
             tab_im.reshape(2, ngrp, NTAB, gp)[k].transpose(1, 0, 2).reshape(NTAB, ngrp * gp)) for k in range(2)]
    b_t = lambda b: _blockdiag(b[0].reshape(nsb, gsb, gp, gc).transpose(0, 1, 3, 2)).astype(BF16)
    bre, bim = b_t(s5_b_re), b_t(s5_b_im)
    cp_t = lambda cp: _blockdiag(cp.reshape(2, nsb, gsb, gc, gp)).astype(BF16)
    ctr, cti = cp_t(cp_re), cp_t(cp_im)
    cmr, cmi = jnp.swapaxes(ctr, -1, -2), jnp.swapaxes(cti, -1, -2)
    dskip = s5_d

    lch = l // TC
    nproc = lch + 1
    order_f = lambda i: jnp.where(i == 0, lch, i - 1)
    order_b = lambda i: jnp.where(i == 0, lch, lch - i)
    rorder_f = lambda i: jnp.where(i == lch, lch, lch - 1 - i)
    rorder_b = lambda i: i
    y_f, hr_f, hi_f, cinr_f, cini_f = _s5_fwd(up, bre, bim, cmr[0], cmi[0], tabs[0][0], tabs[0][1], dskip,
                                               order_f, nproc, False, True, "s5_fwd_f")
    y_b, hr_b, hi_b, cinr_b, cini_b = _s5_fwd(up, bre, bim, cmr[1], cmi[1], tabs[1][0], tabs[1][1], dskip,
                                              order_b, nproc, True, False, "s5_fwd_b")

    x2, ypre, cat, mix = _mix_out_fwd(y_f, y_b, dpool, x1, modtab, w_pool_bd, pool_scale, w_glu_f, w_out_f)
    dx3, h2, ab2, o2, loss_part, dfinal_g = _ffn_fwd([x2], modtab, g2, w13_1, w2_1, 6, False, "ffn1_fwd",
                                                     head=(loss_target[0], final_g.reshape(1, d)))

    dhp2, d13_1, d2_1 = _ffn_bwd(dx3, modtab, h2, ab2, w13_1, w2_1, 6, False, "ffn1_bwd")
    dx2, dmod_678, dg2 = _ffn_bwd_norm(dx3, [dhp2], [x2], o2, modtab, g2, 6, False, "ffn1_bwd_norm")
    dypre, ddpool, dw_out, dw_glu, dw_pool_bd, dpscale, dgate5 = _mix_out_bwd(
        dx2, mix, cat, ypre, dpool, modtab, w_pool_bd, pool_scale, w_glu_f, w_out_f)

    du_f, dar_f, dai_f, dcr_f, dci_f, dbr_f, dbi_f, dd_skip, r13_1, r2_1 = _s5_bwd(
        dypre, up, hr_f, hi_f, cinr_f, cini_f, bre, bim, ctr[0], cti[0], tabs[0][0], tabs[0][1], dskip,
        rorder_f, nproc, False, True, "s5_bwd_f",
        xchg=[(d13_1, A2A_REL), (d2_1.reshape(NDEV, FPAD, d), A2A_REL)])
    du_b, dar_b, dai_b, dcr_b, dci_b, dbr_b, dbi_b, _, r_out, r_glu = _s5_bwd(
        dypre, up, hr_b, hi_b, cinr_b, cini_b, bre, bim, ctr[1], cti[1], tabs[1][0], tabs[1][1], dskip,
        rorder_b, nproc, True, False, "s5_bwd_b",
        xchg=[(dw_out.reshape(NDEV, -1, d), A2A), (dw_glu.reshape(NDEV, hm // NDEV, hm), A2A)])

    dab_r = jnp.stack([dar_f, dar_b]).reshape(q, 1, gp)
    dab_i = jnp.stack([dai_f, dai_b]).reshape(q, 1, gp)
    dcp_r = _diagblocks(jnp.stack([dcr_f, dcr_b]), gsb).reshape(q, gc, gp)
    dcp_i = _diagblocks(jnp.stack([dci_f, dci_b]), gsb).reshape(q, gc, gp)
    ga_re, ga_im, gldt, gc_re, gc_im = _s5_param_bwd(a_re3, a_im3, ldt3, c_re3, c_im3, dab_r, dab_i, dcp_r, dcp_i)
    gb_re = (_diagblocks(dbr_f, gsb) + _diagblocks(dbr_b, gsb)).transpose(0, 1, 3, 2).reshape(ngrp, gp, gc)
    gb_im = (_diagblocks(dbi_f, gsb) + _diagblocks(dbi_b, gsb)).transpose(0, 1, 3, 2).reshape(ngrp, gp, gc)

    def pack(small):
        offs, pieces, off = {}, [], 0
        for k_, a_ in small.items():
            p_ = _pad128(a_.astype(F32))
            offs[k_] = (off, int(np.prod(a_.shape)))
            off += p_.shape[0]
            pieces.append(p_)
        return jnp.concatenate(pieces).reshape(1, off), offs

    gw_pool = _diagblocks(dw_pool_bd, 4)
    bundle_a, offs = pack(dict(pool_w=gw_pool, pool_scale=dpscale, s5_a_re=ga_re, s5_a_im=ga_im,
                               s5_log_dt=gldt[:, 0, 0], s5_b_re=gb_re, s5_b_im=gb_im, s5_c_re=gc_re, s5_c_im=gc_im,
                               s5_d=dd_skip, final_g=dfinal_g))

    du_pool = _pool(ddpool, rows, True, "pool_bwd")
    dx1, dw_in, dmod_34, dg1 = _mix_in_bwd(du_pool, du_f, du_b, dx2, h1, x1, modtab, g1, w_in_f, lc)
    dhp0a, d13_0a, d2_0a, bund_a, r_in = _ffn_bwd(
        dx1, modtab, h0, ab0, w13_0, w2_0, 0, True, "ffn0_bwd_a", pair0=0, npair=2,
        xchg=[(bundle_a, GATHER), (dw_in.reshape(NDEV, d // NDEV, -1), A2A)])
    dhp0b, d13_0b, d2_0b, r13_0a, r2_0a = _ffn_bwd(
        dx1, modtab, h0, ab0, w13_0, w2_0, 0, True, "ffn0_bwd_b", pair0=2, npair=2,
        xchg=[(d13_0a, A2A_SAME), (d2_0a.reshape(NDEV // 2, FPAD, d), A2A_SAME)])
    dx0, dmod_012, dg0, r13_0b, r2_0b = _ffn_bwd_norm(
        dx1, [dhp0a, dhp0b], x0, o0, modtab, g0, 0, True, "ffn0_bwd_norm",
        xchg=[(d13_0b, A2A_OTHER), (d2_0b.reshape(NDEV // 2, FPAD, d), A2A_OTHER)])
    grad_x = dx0.reshape(1, l, d)

    dmod_c = jnp.concatenate([dmod_012[0], dmod_34[0], jnp.zeros((4, d), F32)], axis=0)
    dmod_l = jnp.concatenate([dmod_012[1], dmod_34[1], dgate5, dmod_678[0]], axis=0)
    bundle_b, offs_b = pack(dict(norm_g=jnp.concatenate([dg0, dg1, dg2], axis=0), dmod_l=dmod_l, dmod_c=dmod_c,
                                 loss=loss_part[:, :1]))
    bund_b = _exchange([(bundle_b, GATHER)], "exchange_small")[0]
    bunds = {**{k_: (bund_a.reshape(NDEV, -1), v_) for k_, v_ in offs.items()},
             **{k_: (bund_b.reshape(NDEV, -1), v_) for k_, v_ in offs_b.items()}}
    r13, r2 = [(r13_0a, r13_0b), (r13_1,)], [(r2_0a, r2_0b), (r2_1,)]

    def piece(name):
        b_, (o_, n_) = bunds[name]
        return b_[:, o_:o_ + n_]

    loss = jnp.sum(piece("loss")[:, 0])

    dl_all = lax.dynamic_slice_in_dim(piece("dmod_l"), me * wa, wa, axis=1)
    dc_all = lax.dynamic_slice_in_dim(piece("dmod_c"), me * wa, wa, axis=1)
    g_wada, gc_part = _ada_bwd(c16, w_ada[0], dl_all, dc_all)
    gc_all = _exchange([(gc_part, GATHER)], "exchange_cctx")[0]

    parts = {
        "c_ctx": gc_all.reshape(NDEV, d),
        "norm_g": lax.dynamic_slice_in_dim(piece("norm_g").reshape(NDEV, 3, d), me * (d // NDEV), d // NDEV,
                                           axis=2).reshape((NDEV,) + norm_g.shape),
        "w_ada": g_wada.reshape((1,) + w_ada.shape),
        "b_ada": jnp.concatenate([piece("dmod_l"), piece("dmod_c")], axis=0).reshape((2 * NDEV,) + b_ada.shape),
        "w_in": r_in.reshape((NDEV,) + w_in.shape),
        "s5_w_glu": r_glu.reshape((NDEV,) + s5_w_glu.shape),
        "w_out": r_out.reshape((NDEV,) + w_out.shape),
    }
    for k_ in ("pool_w", "pool_scale", "s5_a_re", "s5_a_im", "s5_log_dt", "s5_b_re", "s5_b_im", "s5_c_re", "s5_c_im",
               "s5_d", "final_g"):
        parts[k_] = piece(k_).reshape((NDEV,) + weights[k_].shape)

    grads, deltas, new_m, new_v = [], [], [], []
    ffn_parts = {"ffn_w1": (r13, 0, True), "ffn_w3": (r13, FPAD, True), "ffn_w2": (r2, 0, False)}
    for k_ in weights:
        if k_ in ffn_parts:
            shards, row0, transposed = ffn_parts[k_]
            flip = (lambda a: jnp.swapaxes(a, 2, 3)) if transposed else (lambda a: a)
            outs = _adamw_ffn(shards, row0, flip(weights[k_]), flip(mom_m[k_]), flip(mom_v[k_]), "adamw_" + k_)
            g_, d_, m_, v_ = (flip(o_) for o_ in outs)
        else:
            g_, d_, m_, v_ = _adamw(parts[k_], weights[k_], mom_m[k_], mom_v[k_], "adamw_" + k_)
        grads.append(g_)
        deltas.append(d_)
        new_m.append(m_)
        new_v.append(v_)
    return (loss, grad_x, *grads, *deltas, *new_m, *new_v)
```

```python
import functools
import math

import numpy as np
import jax
import jax.numpy as jnp
from jax import lax
from jax.experimental import pallas as pl
from jax.experimental.pallas import tpu as pltpu

F32 = jnp.float32
BF16 = jnp.bfloat16
AXES = ("x", "y", "c")
NDEV = 8
EPS = 1e-6
TM = 512
TML = 512
TC = 256
SEG = TC // 8
NTAB = SEG + 16
S5_UNROLL = True
GRID_W = 64
POOL_WINDOWS = (2, 4, 8, 16)
LANE = 128
FPAD = 384
VMEM_LIMIT = 56 * 1024 * 1024
ADAM_BLOCK_BYTES = 1024 * 1024
ADAM_FFN_LANES = 256
ADAM_LR, ADAM_B1, ADAM_B2, ADAM_EPS, ADAM_WD, ADAM_STEP = 0.001, 0.9, 0.999, 1e-08, 0.01, 10


def _raw_call(body, kw):
    return pl.pallas_call(body, **kw)


def _pcall(body, xchg=None, edges=None, **kw):
    extra = ()
    if xchg:
        arrs, kinds = [a for a, _ in xchg], [k for _, k in xchg]
        n = len(arrs)
        n_in, n_out, n_scr = len(kw["in_specs"]), len(kw["out_shape"]), len(kw.get("scratch_shapes", ()))
        inner = body

        def hosted(*refs):
            a, b = n_in, n_in + n
            c_, d_ = b + n_out, b + n_out + n
            e = d_ + n_scr
            first, mid, last = edges()

            @pl.when(first)
            def _():
                _xchg_start(refs[a:b], refs[c_:d_], refs[e:], kinds)

            inner(*refs[:a], *refs[b:c_], *refs[d_:e])

            if any(k == GATHER_REL for k in kinds):
                @pl.when(mid)
                def _():
                    _xchg_relay(refs[a:b], refs[c_:d_], refs[e:], kinds)

            @pl.when(last)
            def _():
                _xchg_finish(refs[a:b], refs[c_:d_], refs[e:], kinds)

        body = hosted
        any_spec = pl.BlockSpec(memory_space=pl.ANY)
        kw = dict(kw, in_specs=list(kw["in_specs"]) + [any_spec] * n,
                  out_shape=tuple(kw["out_shape"]) + _xchg_shapes(arrs, kinds),
                  out_specs=tuple(kw["out_specs"]) + (any_spec,) * n,
                  scratch_shapes=list(kw.get("scratch_shapes", ())) + _xchg_sems(n))
        extra = tuple(arrs)
    call = _raw_call(body, kw)
    return (lambda *args: call(*args, *extra)) if extra else call


def _params(sem):
    return pltpu.CompilerParams(dimension_semantics=sem, vmem_limit_bytes=VMEM_LIMIT)


def _dot(a, b):
    return jnp.dot(a, b, preferred_element_type=F32)


def _dot_nt(a, b):
    return lax.dot_general(a, b, (((1,), (1,)), ((), ())), preferred_element_type=F32)


def _dot_tn(a, b):
    return lax.dot_general(a, b, (((0,), (0,)), ((), ())), preferred_element_type=F32)


def _sigmoid(x):
    return 1.0 / (1.0 + jnp.exp(-x))


def _colsum(a):
    return jnp.sum(a, axis=0, keepdims=True)


def _resident(shape):
    nd = len(shape)
    return pl.BlockSpec(shape, lambda *_: (0,) * nd, pipeline_mode=pl.Buffered(1))


def _const(shape):
    nd = len(shape)
    return pl.BlockSpec(shape, lambda *_: (0,) * nd)


GATHER = "gather"
GATHER_REL = "gather_rel"
A2A = "a2a"
A2A_REL = "a2a_rel"
A2A_SAME = "a2a_same"
A2A_OTHER = "a2a_other"


def _exchange(items, name):
    arrs, kinds = [a for a, _ in items], [k for _, k in items]
    n = len(arrs)

    def body(*refs):
        _xchg_start(refs[:n], refs[n:2 * n], refs[2 * n:], kinds)
        _xchg_relay(refs[:n], refs[n:2 * n], refs[2 * n:], kinds)
        _xchg_finish(refs[:n], refs[n:2 * n], refs[2 * n:], kinds)

    any_spec = pl.BlockSpec(memory_space=pl.ANY)
    outs = _pcall(
        body, name=name, out_shape=_xchg_shapes(arrs, kinds), in_specs=[any_spec] * n, out_specs=[any_spec] * n,
        scratch_shapes=_xchg_sems(n),
    )(*arrs)
    return list(outs)


def _edges1(n0):
    return lambda: (pl.program_id(0) == 0, pl.program_id(0) == (7 * n0) // 8, pl.program_id(0) == n0 - 1)


def _edges2(n0, n1):
    def edges():
        step = pl.program_id(0) * n1 + pl.program_id(1)
        return step == 0, step == (7 * n0 * n1) // 8, step == n0 * n1 - 1
    return edges


def _xchg_shapes(arrs, kinds):
    return tuple(jax.ShapeDtypeStruct(((NDEV,) if k in (GATHER, GATHER_REL) else ()) + tuple(a.shape), a.dtype)
                 for a, k in zip(arrs, kinds))


def _xchg_sems(n):
    return [pltpu.SemaphoreType.DMA((n * (NDEV - 1),)), pltpu.SemaphoreType.DMA((n * (NDEV - 1),)),
            pltpu.SemaphoreType.DMA((n,))]


def _xchg_plan(in_refs, out_refs, sems, kinds):
    send_sems, recv_sems, loc_sems = sems
    x, y, c = (lax.axis_index(a) for a in AXES)
    my_abs, my_chip = 4 * x + 2 * y + c, 2 * x + y

    def peer(p):
        px = 1 - x if p & 4 else x
        py = 1 - y if p & 2 else y
        pc = 1 - c if p & 1 else c
        return (px, py, pc), 4 * px + 2 * py + pc, 2 * px + py

    starts, relays, recvs = [], [], []
    for k, kind in enumerate(kinds):
        src, dst = in_refs[k], out_refs[k]

        def remote(src_ref, row, p, pair, dst=dst, k=k):
            dev, sem = peer(p)[0], k * (NDEV - 1) + pair - 1
            return lambda: pltpu.make_async_remote_copy(
                src_ref=src_ref, dst_ref=dst.at[row], send_sem=send_sems.at[sem], recv_sem=recv_sems.at[sem],
                device_id=dev, device_id_type=pl.DeviceIdType.MESH)

        def local(src_ref, row, dst=dst, k=k):
            return lambda: pltpu.make_async_copy(src_ref, dst.at[row], loc_sems.at[k])

        if kind == GATHER:
            for p in range(1, NDEV):
                starts.append((remote(src, my_abs, p, p), False))
                recvs.append(remote(src, peer(p)[1], p, p))
            starts.append((local(src, my_abs), True))
        elif kind == GATHER_REL:
            for p in (1, 4, 2, 6):
                starts.append((remote(src, 4 * (p & 1) + my_chip, p, p), False))
            for q in (4, 2, 6):
                pchip = peer(q)[2]
                relays.append((remote(src, pchip, q, q), remote(dst.at[pchip], 4 + pchip, 1, q | 1)))
                recvs.append(remote(src, 4 + pchip, 1, q | 1))
            recvs.append(remote(src, 4 + my_chip, 1, 1))
            starts.append((local(src, my_chip), True))
        elif kind in (A2A, A2A_REL):
            for p in range(1, NDEV):
                _, pabs, pchip = peer(p)
                theirs, mine = (pabs, my_abs) if kind == A2A else (4 * (p & 1) + pchip, 4 * (p & 1) + my_chip)
                starts.append((remote(src.at[theirs], mine, p, p), False))
                recvs.append(remote(src.at[theirs], theirs, p, p))
            own = my_abs if kind == A2A else my_chip
            starts.append((local(src.at[own], own), True))
        else:
            for p in ((4, 2, 6) if kind == A2A_SAME else (1, 5, 3, 7)):
                pchip = peer(p)[2]
                starts.append((remote(src.at[pchip], my_chip, p, p), False))
                recvs.append(remote(src.at[pchip], pchip, p, p))
            if kind == A2A_SAME:
                starts.append((local(src.at[my_chip], my_chip), True))
    return starts, relays, recvs


def _xchg_start(in_refs, out_refs, sems, kinds):
    for make, _ in _xchg_plan(in_refs, out_refs, sems, kinds)[0]:
        make().start()


def _xchg_relay(in_refs, out_refs, sems, kinds):
    for arrival, forward in _xchg_plan(in_refs, out_refs, sems, kinds)[1]:
        arrival().wait_recv()
        forward().start()


def _xchg_finish(in_refs, out_refs, sems, kinds):
    starts, relays, recvs = _xchg_plan(in_refs, out_refs, sems, kinds)
    for make in recvs:
        make().wait_recv()
    for make, is_local in starts:
        if is_local:
            make().wait()
        else:
            make().wait_send()
    for _, forward in relays:
        forward().wait_send()


def _ada_fwd(c16, w_blk, b_blk):
    def body(c_ref, w_ref, b_ref, o_ref):
        cc = c_ref[...]
        s = (cc * _sigmoid(cc)).astype(BF16)
        o_ref[...] = _dot(s, w_ref[...].astype(BF16)) + b_ref[...]

    return _pcall(body, name="ada_fwd", out_shape=jax.ShapeDtypeStruct((16, w_blk.shape[1]), F32),
                  compiler_params=_params(None))(c16, w_blk, b_blk)


def _ada_bwd(c16, w_blk, dl, dc):
    d, w = w_blk.shape

    def body(c_ref, w_ref, dl_ref, dc_ref, gw_ref, gc_ref):
        cc = c_ref[...]
        sg = _sigmoid(cc)
        s = (cc * sg).astype(BF16)
        dctot = _colsum(dc_ref[...])
        dm = jnp.concatenate([dl_ref[...], jnp.broadcast_to(dctot, (8, w))], axis=0)
        rows = lax.broadcasted_iota(jnp.int32, (16, w), 0)
        dm = jnp.where(rows <= 8, dm, 0.0).astype(BF16)
        gw_ref[...] = _dot_tn(s, dm)
        t = _dot_nt(jnp.broadcast_to(dctot, (8, w)).astype(BF16), w_ref[...].astype(BF16))[0:1]
        c8, s8 = cc[8:9], sg[8:9]
        gc_ref[...] = t * (s8 * (1.0 + c8 * (1.0 - s8)))

    return _pcall(body, name="ada_bwd",
                  out_shape=(jax.ShapeDtypeStruct((d, w), F32), jax.ShapeDtypeStruct((1, d), F32)),
                  compiler_params=_params(None))(c16, w_blk, dl, dc)


def _norm_fwd(x, g, shift, scale):
    rstd = lax.rsqrt(jnp.mean(x * x, axis=-1, keepdims=True) + EPS)
    xhat = x * rstd
    xn = xhat * g
    return rstd, xhat, xn, xn * (1.0 + scale) + shift


def _norm_bwd(dh, rstd, xhat, xn, g, scale):
    dxn = dh * (1.0 + scale)
    dxhat = dxn * g
    dx = rstd * (dxhat - xhat * jnp.mean(dxhat * xhat, axis=-1, keepdims=True))
    return dx, _colsum(dh), _colsum(dh * xn), _colsum(dxn * xhat)


def _mod_spec(d, has_ctx):
    if has_ctx:
        return pl.BlockSpec((1, 9, d), lambda *ids: (jnp.minimum(ids[-1], 1), 0, 0))
    return pl.BlockSpec((1, 9, d), lambda *ids: (1, 0, 0))


def _row_specs(nx, tm, d):
    if nx == 1:
        return [pl.BlockSpec((tm, d), lambda i: (i, 0))]
    return [pl.BlockSpec((tm, d), lambda i: (0, 0)), pl.BlockSpec((tm, d), lambda i: (jnp.maximum(i - 1, 0), 0))]


def _rows(refs):
    if len(refs) == 1:
        return refs[0][...]
    return jnp.where(pl.program_id(0) == 0, refs[0][...], refs[1][...])


def _tile(has_ctx):
    return TM if has_ctx else TML


def _tile_of(nt, has_ctx):
    if has_ctx:
        return lambda i: jnp.where(i == 0, nt - 1, i - 1)
    return lambda i: i


def _dmod_spec(nrow, d, has_ctx):
    if has_ctx:
        return pl.BlockSpec((1, nrow, d), lambda i: (jnp.minimum(i, 1), 0, 0))
    return pl.BlockSpec((1, nrow, d), lambda i: (0, 0, 0))


def _ffn_fwd(x, modtab, g, w13, w2, k0, has_ctx, name, xchg=None, head=None):
    t, d = sum(a.shape[0] for a in x), x[0].shape[1]
    nb, w, _ = w13.shape
    fp = w // 2
    tm = _tile(has_ctx)
    nx = len(x)

    nhead = 0 if head is None else 2

    def body(*refs):
        mod_ref, g_ref, w13_ref, w2_ref = refs[nx:nx + 4]
        xo_ref, h_ref, ab_ref, o_ref = refs[nx + 4 + nhead:nx + 8 + nhead]
        xx = _rows(refs[:nx])
        shift, scale, gate = mod_ref[0, k0:k0 + 1, :], mod_ref[0, k0 + 1:k0 + 2, :], mod_ref[0, k0 + 2:k0 + 3, :]
        _, _, _, h = _norm_fwd(xx, g_ref[...], shift, scale)
        hb = h.astype(BF16)
        h_ref[...] = hb
        acc = jnp.zeros((tm, d), F32)
        for p in range(nb // 2):
            zs = []
            for q in range(2):
                blk = 2 * p + q
                ab = _dot_nt(hb, w13_ref[blk])
                ab_ref[:, blk * w:(blk + 1) * w] = ab.astype(BF16)
                a, b = ab[:, :fp], ab[:, fp:]
                zs.append((a * _sigmoid(a) * b).astype(BF16))
            acc = acc + _dot(jnp.concatenate(zs, axis=1), w2_ref[p])
        o_ref[...] = acc.astype(BF16)
        xo = xx + (0.5 * gate) * acc
        if head is None:
            xo_ref[...] = xo
        else:
            tgt_ref, fg_ref = refs[nx + 4:nx + 6]
            loss_ref, dfg_ref = refs[nx + 8 + nhead:]

            @pl.when(pl.program_id(0) == 0)
            def _():
                loss_ref[...] = jnp.zeros_like(loss_ref)
                dfg_ref[...] = jnp.zeros_like(dfg_ref)

            xo_ref[...] = _loss_head(xo, tgt_ref[...], fg_ref[...], loss_ref, dfg_ref)

    tile = _tile_of(t // tm, has_ctx)
    row = lambda i: (tile(i), 0)
    head_in = [] if head is None else [pl.BlockSpec((tm, d), row), _const((1, d))]
    head_shape = () if head is None else (jax.ShapeDtypeStruct((1, LANE), F32), jax.ShapeDtypeStruct((1, d), F32))
    head_out = () if head is None else (_const((1, LANE)), _const((1, d)))
    return _pcall(
        body, name=name, grid=(t // tm,), xchg=xchg, edges=_edges1(t // tm),
        out_shape=(jax.ShapeDtypeStruct((t, d), F32), jax.ShapeDtypeStruct((t, d), BF16),
                   jax.ShapeDtypeStruct((t, nb * w), BF16), jax.ShapeDtypeStruct((t, d), BF16)) + head_shape,
        in_specs=_row_specs(nx, tm, d) + [_mod_spec(d, has_ctx), _const((1, d)),
                                          _resident(w13.shape), _resident(w2.shape)] + head_in,
        out_specs=(pl.BlockSpec((tm, d), row), pl.BlockSpec((tm, d), row), pl.BlockSpec((tm, nb * w), row),
                   pl.BlockSpec((tm, d), row)) + head_out,
        compiler_params=_params(("arbitrary",)),
    )(*x, modtab, g, w13, w2, *(head or ()))


def _ffn_bwd(dy, modtab, h, ab, w13, w2, k0, has_ctx, name, pair0=0, npair=None, xchg=None):
    t, d = dy.shape
    _, w, _ = w13.shape
    fp = w // 2
    npair = w13.shape[0] // 2 if npair is None else npair
    nb = 2 * npair
    tm = _tile(has_ctx)
    nt = t // tm

    def body(dy_ref, mod_ref, h_ref, ab_ref, w13_ref, w2_ref, dh_ref, d13_ref, d2_ref, acc13, acc2):
        i = pl.program_id(1)

        @pl.when(i == 0)
        def _():
            acc13[...] = jnp.zeros_like(acc13)
            acc2[...] = jnp.zeros_like(acc2)

        gate = mod_ref[0, k0 + 2:k0 + 3, :]
        do = (dy_ref[...] * (0.5 * gate)).astype(BF16)
        dz = _dot_nt(do, w2_ref[0])
        hb = h_ref[...]
        dh = jnp.zeros((tm, d), F32)
        zs = []
        for q in range(2):
            ab = ab_ref[:, q * w:(q + 1) * w].astype(F32)
            a, b = ab[:, :fp], ab[:, fp:]
            sg = _sigmoid(a)
            sa = a * sg
            dzq = dz[:, q * fp:(q + 1) * fp]
            da = dzq * b * (sg * (1.0 + a * (1.0 - sg)))
            db = dzq * sa
            dab = jnp.concatenate([da, db], axis=1).astype(BF16)
            dh = dh + _dot(dab, w13_ref[q])
            acc13[q] += _dot_tn(dab, hb)
            zs.append((sa * b).astype(BF16))
        acc2[...] += _dot_tn(jnp.concatenate(zs, axis=1), do)
        dh_ref[0] = dh.astype(BF16)

        @pl.when(i == nt - 1)
        def _():
            d13_ref[...] = acc13[...].astype(BF16)
            d2_ref[0] = acc2[...].astype(BF16)

    mod_spec = _mod_spec(d, has_ctx)
    tile = _tile_of(nt, has_ctx)
    return _pcall(
        body, name=name, grid=(npair, nt), xchg=xchg, edges=_edges2(npair, nt),
        out_shape=(jax.ShapeDtypeStruct((npair, t, d), BF16), jax.ShapeDtypeStruct((nb, w, d), BF16),
                   jax.ShapeDtypeStruct((npair, 2 * fp, d), BF16)),
        in_specs=[pl.BlockSpec((tm, d), lambda p, i: (tile(i), 0)), mod_spec,
                  pl.BlockSpec((tm, d), lambda p, i: (tile(i), 0)),
                  pl.BlockSpec((tm, 2 * w), lambda p, i: (tile(i), pair0 + p)),
                  pl.BlockSpec((2, w, d), lambda p, i: (pair0 + p, 0, 0)),
                  pl.BlockSpec((1, 2 * fp, d), lambda p, i: (pair0 + p, 0, 0))],
        out_specs=(pl.BlockSpec((1, tm, d), lambda p, i: (p, tile(i), 0)),
                   pl.BlockSpec((2, w, d), lambda p, i: (p, 0, 0)),
                   pl.BlockSpec((1, 2 * fp, d), lambda p, i: (p, 0, 0))),
        scratch_shapes=[pltpu.VMEM((2, w, d), F32), pltpu.VMEM((2 * fp, d), F32)],
        compiler_params=_params(("arbitrary", "arbitrary")),
    )(dy, modtab, h, ab, w13, w2)


def _ffn_bwd_norm(dy, dhps, x, o, modtab, g, k0, has_ctx, name, xchg=None):
    t, d = dy.shape
    ngrp = 2 if has_ctx else 1
    tm = _tile(has_ctx)
    ndh, nx = len(dhps), len(x)
    lat0 = ngrp - 1

    def body(dy_ref, *rest):
        dhp_refs, x_refs = rest[:ndh], rest[ndh:ndh + nx]
        o_ref, mod_ref, g_ref, dx_ref, dmod_ref, dg_ref = rest[ndh + nx:]
        i = pl.program_id(0)

        @pl.when(i == 0)
        def _():
            dg_ref[...] = jnp.zeros_like(dg_ref)

        @pl.when((i == 0) | (i == ngrp - 1))
        def _():
            dmod_ref[...] = jnp.zeros_like(dmod_ref)

        dh = None
        for ref in dhp_refs:
            for p in range(ref.shape[0]):
                dh = ref[p].astype(F32) if dh is None else dh + ref[p].astype(F32)
        scale = mod_ref[0, k0 + 1:k0 + 2, :]
        gg = g_ref[...]
        rstd, xhat, xn, _ = _norm_fwd(_rows(x_refs), gg, 0.0, scale)
        dxn, dshift, dscale, dg = _norm_bwd(dh, rstd, xhat, xn, gg, scale)
        dyv = dy_ref[...]
        dx_ref[...] = dyv + dxn
        dmod_ref[0, 0:1, :] += dshift
        dmod_ref[0, 1:2, :] += dscale
        dmod_ref[0, 2:3, :] += _colsum(0.5 * dyv * o_ref[...].astype(F32))
        dg_ref[...] += dg

    tile = _tile_of(t // tm, has_ctx)
    row = lambda i: (tile(i), 0)
    return _pcall(
        body, name=name, grid=(t // tm,), xchg=xchg, edges=_edges1(t // tm),
        out_shape=(jax.ShapeDtypeStruct((t - lat0 * tm, d), F32), jax.ShapeDtypeStruct((ngrp, 3, d), F32),
                   jax.ShapeDtypeStruct((1, d), F32)),
        in_specs=[pl.BlockSpec((tm, d), row)]
        + [pl.BlockSpec((a.shape[0], tm, d), lambda i: (0, tile(i), 0)) for a in dhps]
        + _row_specs(nx, tm, d) + [pl.BlockSpec((tm, d), row), _mod_spec(d, has_ctx), _const((1, d))],
        out_specs=(pl.BlockSpec((tm, d), lambda i: (jnp.maximum(i - lat0, 0), 0)), _dmod_spec(3, d, has_ctx),
                   _const((1, d))),
        compiler_params=_params(("arbitrary",)),
    )(dy, *dhps, *x, o, modtab, g)


def _mix_in_fwd(x1, modtab, g, w_in):
    t, d = x1.shape
    hm = w_in.shape[1] // 2

    def body(x_ref, mod_ref, g_ref, w_ref, h_ref, up_ref, us_ref):
        _, _, _, h = _norm_fwd(x_ref[...], g_ref[...], mod_ref[0, 3:4, :], mod_ref[0, 4:5, :])
        hb = h.astype(BF16)
        h_ref[...] = hb
        u = _dot(hb, w_ref[...])
        up_ref[...] = u[:, :hm]
        us_ref[...] = u[:, hm:]

    tile = _tile_of(t // TM, True)
    row = lambda i: (tile(i), 0)
    return _pcall(
        body, name="mix_in_fwd", grid=(t // TM,),
        out_shape=(jax.ShapeDtypeStruct((t, d), BF16), jax.ShapeDtypeStruct((t, hm), F32),
                   jax.ShapeDtypeStruct((t, hm), F32)),
        in_specs=[pl.BlockSpec((TM, d), row), _mod_spec(d, True), _const((1, d)), _resident(w_in.shape)],
        out_specs=(pl.BlockSpec((TM, d), row), pl.BlockSpec((TM, hm), row), pl.BlockSpec((TM, hm), row)),
        compiler_params=_params(("arbitrary",)),
    )(x1, modtab, g, w_in)


def _mix_in_bwd(du_pool, du_f, du_b, dx2, h1, x1, modtab, g, w_in, lc):
    t, d = x1.shape
    m = w_in.shape[1]
    hm = m // 2
    nt = t // TM

    def body(dup_ref, duf_ref, dub_ref, dx2_ref, h_ref, x_ref, mod_ref, g_ref, w_ref, dx_ref, dw_ref, dmod_ref, dg_ref, acc):
        i = pl.program_id(0)

        @pl.when(i == 0)
        def _():
            acc[...] = jnp.zeros_like(acc)
            dg_ref[...] = jnp.zeros_like(dg_ref)

        @pl.when(i <= 1)
        def _():
            dmod_ref[...] = jnp.zeros_like(dmod_ref)

        lat = (i > 0).astype(F32)
        valid = (i > 0) | (lax.broadcasted_iota(jnp.int32, (TM, 1), 0) < lc)
        du_s5 = jnp.where(valid, duf_ref[...] + dub_ref[...], 0.0)
        du = jnp.concatenate([dup_ref[...] * lat, du_s5], axis=1).astype(BF16)
        dh = _dot_nt(du, w_ref[...])
        acc[...] += _dot_tn(h_ref[...], du)
        scale = mod_ref[0, 4:5, :]
        gg = g_ref[...]
        rstd, xhat, xn, _ = _norm_fwd(x_ref[...], gg, 0.0, scale)
        dxn, dshift, dscale, dg = _norm_bwd(dh, rstd, xhat, xn, gg, scale)
        dx_ref[...] = dx2_ref[...] * lat + dxn
        dmod_ref[0, 0:1, :] += dshift
        dmod_ref[0, 1:2, :] += dscale
        dg_ref[...] += dg

        @pl.when(i == nt - 1)
        def _():
            dw_ref[...] = acc[...].astype(BF16)

    tile = _tile_of(nt, True)
    row = lambda i: (tile(i), 0)
    lrow = lambda i: (jnp.maximum(i - 1, 0), 0)
    return _pcall(
        body, name="mix_in_bwd", grid=(nt,),
        out_shape=(jax.ShapeDtypeStruct((t, d), F32), jax.ShapeDtypeStruct((d, m), BF16),
                   jax.ShapeDtypeStruct((2, 2, d), F32), jax.ShapeDtypeStruct((1, d), F32)),
        in_specs=[pl.BlockSpec((TM, hm), lrow), pl.BlockSpec((TM, hm), row), pl.BlockSpec((TM, hm), row),
                  pl.BlockSpec((TM, d), lrow), pl.BlockSpec((TM, d), row), pl.BlockSpec((TM, d), row),
                  _mod_spec(d, True), _const((1, d)), _resident(w_in.shape)],
        out_specs=(pl.BlockSpec((TM, d), row), _const((d, m)), _dmod_spec(2, d, True), _const((1, d))),
        scratch_shapes=[pltpu.VMEM((d, m), F32)],
        compiler_params=_params(("arbitrary",)),
    )(du_pool, du_f, du_b, dx2, h1, x1, modtab, g, w_in)


def _pool(v, rows, transpose, name):
    l, n = rows * GRID_W, GRID_W * LANE
    ngrp = v.shape[1] // LANE
    nchunk = 4
    cw = n // nchunk

    def rowsum(val, lo, hi):
        ri = lax.broadcasted_iota(jnp.int32, (rows, rows), 0)
        ci = lax.broadcasted_iota(jnp.int32, (rows, rows), 1)
        sel = (((ci - ri) >= -lo) & ((ci - ri) <= hi)).astype(BF16)
        v1 = val.astype(BF16)
        r1 = val - v1.astype(F32)
        v2 = r1.astype(BF16)
        v3 = (r1 - v2.astype(F32)).astype(BF16)
        return _dot(sel, v1) + _dot(sel, v2) + _dot(sel, v3)

    def one(v_ref, o_ref, g_scr, r_scr, a_scr, win):
        for c in range(GRID_W):
            g_scr[:, c * LANE:(c + 1) * LANE] = v_ref[pl.ds(c, rows, stride=GRID_W), :]
        lo = win // 2
        hi = win - 1 - lo
        rlo, rhi = (hi, lo) if transpose else (lo, hi)
        ridx = lax.broadcasted_iota(jnp.int32, (rows, 1), 0)
        cnt_r = (jnp.minimum(ridx + hi + 1, rows) - jnp.maximum(ridx - lo, 0)).astype(F32)
        cidx = lax.broadcasted_iota(jnp.int32, (1, n), 1) // LANE
        cnt_c = (jnp.minimum(cidx + hi + 1, GRID_W) - jnp.maximum(cidx - lo, 0)).astype(F32)
        if not transpose:
            for k in range(nchunk):
                sl = slice(k * cw, (k + 1) * cw)
                r_scr[:, sl] = rowsum(g_scr[:, sl], rlo, rhi) / cnt_r
        else:
            r_scr[...] = g_scr[...] / cnt_c
        a_scr[...] = r_scr[...]
        for j in range(-rlo, rhi + 1):
            if j == 0:
                continue
            c0, c1 = max(0, -j), min(GRID_W, GRID_W - j)
            a_scr[:, c0 * LANE:c1 * LANE] += r_scr[:, (c0 + j) * LANE:(c1 + j) * LANE]
        if not transpose:
            r_scr[...] = a_scr[...] / cnt_c - g_scr[...]
        else:
            for k in range(nchunk):
                sl = slice(k * cw, (k + 1) * cw)
                r_scr[:, sl] = rowsum(a_scr[:, sl] / cnt_r, rlo, rhi) - g_scr[:, sl]
        for c in range(GRID_W):
            o_ref[pl.ds(c, rows, stride=GRID_W), :] = r_scr[:, c * LANE:(c + 1) * LANE]

    def body(v_ref, o_ref, g_scr, r_scr, a_scr):
        grp = pl.program_id(0)
        for k, win in enumerate(POOL_WINDOWS):
            @pl.when(grp == k)
            def _(win=win):
                one(v_ref, o_ref, g_scr, r_scr, a_scr, win)

    spec = pl.BlockSpec((l, LANE), lambda k: (0, k))
    return _pcall(
        body, name=name, grid=(ngrp,), out_shape=jax.ShapeDtypeStruct((l, ngrp * LANE), F32),
        in_specs=[spec], out_specs=spec,
        scratch_shapes=[pltpu.VMEM((rows, n), F32), pltpu.VMEM((rows, n), F32), pltpu.VMEM((rows, n), F32)],
        compiler_params=_params(("arbitrary",)),
    )(v)


def _cmul(ar, ai, br, bi):
    return ar * br - ai * bi, ar * bi + ai * br


def _s5_prep(a_re, a_im, log_dt, c_re, c_im, kvec):
    q, nc, p = c_re.shape

    def body(ar_ref, ai_ref, ldt_ref, cr_ref, ci_ref, k_ref, cpr_ref, cpi_ref, tr_ref, ti_ref):
        lr, li = ar_ref[...], ai_ref[...]
        dt = jnp.exp(ldt_ref[...])
        kk = k_ref[...]
        mag = jnp.exp(kk * (lr * dt))
        ph = kk * (li * dt)
        tr_ref[...] = mag * jnp.cos(ph)
        ti_ref[...] = mag * jnp.sin(ph)
        m1 = jnp.exp(lr * dt)
        abr, abi = m1 * jnp.cos(li * dt), m1 * jnp.sin(li * dt)
        den = lr * lr + li * li
        xr, xi = abr - 1.0, abi
        cfr, cfi = (xr * lr + xi * li) / den, (xi * lr - xr * li) / den
        pr, pi_ = _cmul(cr_ref[...], ci_ref[...], cfr, cfi)
        cpr_ref[...] = pr
        cpi_ref[...] = pi_

    return _pcall(
        body, name="s5_prep",
        out_shape=(jax.ShapeDtypeStruct((q, nc, p), F32), jax.ShapeDtypeStruct((q, nc, p), F32),
                   jax.ShapeDtypeStruct((q, NTAB, p), F32), jax.ShapeDtypeStruct((q, NTAB, p), F32)),
        compiler_params=_params(None),
    )(a_re, a_im, log_dt, c_re, c_im, kvec)


def _s5_param_bwd(a_re, a_im, log_dt, c_re, c_im, dab_r, dab_i, dcp_r, dcp_i):
    q, nc, p = c_re.shape

    def body(ar_ref, ai_ref, ldt_ref, cr_ref, ci_ref, dar_ref, dai_ref, dcr_ref, dci_ref,
             gar_ref, gai_ref, gdt_ref, gcr_ref, gci_ref):
        lr, li = ar_ref[...], ai_ref[...]
        dt = jnp.exp(ldt_ref[...])
        m1 = jnp.exp(lr * dt)
        abr, abi = m1 * jnp.cos(li * dt), m1 * jnp.sin(li * dt)
        den = lr * lr + li * li
        xr, xi = abr - 1.0, abi
        cfr, cfi = (xr * lr + xi * li) / den, (xi * lr - xr * li) / den
        cr, ci = cr_ref[...], ci_ref[...]
        dcr, dci = dcr_ref[...], dci_ref[...]
        gcr, gci = _cmul(dcr, dci, cfr, -cfi)
        gcr_ref[...] = gcr
        gci_ref[...] = gci
        t_r, t_i = _cmul(cr, -ci, dcr, dci)
        dcf_r = jnp.sum(t_r, axis=1, keepdims=True)
        dcf_i = jnp.sum(t_i, axis=1, keepdims=True)
        ilr, ili = lr / den, -li / den
        u_r, u_i = _cmul(dcf_r, dcf_i, ilr, -ili)
        dab_r_, dab_i_ = dar_ref[...] + u_r, dai_ref[...] + u_i
        v_r, v_i = _cmul(dab_r_, dab_i_, abr, -abi)
        cl_r, cl_i = _cmul(cfr, cfi, ilr, ili)
        w_r, w_i = _cmul(dcf_r, dcf_i, cl_r, -cl_i)
        gar_ref[...] = v_r * dt - w_r
        gai_ref[...] = v_i * dt - w_i
        la_r, la_i = _cmul(lr, li, abr, abi)
        ddt = la_r * dab_r_ + la_i * dab_i_
        gdt_ref[...] = dt * jnp.sum(ddt, axis=2, keepdims=True)

    return _pcall(
        body, name="s5_param_bwd",
        out_shape=(jax.ShapeDtypeStruct((q, 1, p), F32), jax.ShapeDtypeStruct((q, 1, p), F32),
                   jax.ShapeDtypeStruct((q, 1, p), F32), jax.ShapeDtypeStruct((q, nc, p), F32),
                   jax.ShapeDtypeStruct((q, nc, p), F32)),
        compiler_params=_params(None),
    )(a_re, a_im, log_dt, c_re, c_im, dab_r, dab_i, dcp_r, dcp_i)


def _bcast8(row, c):
    return jnp.broadcast_to(row, (8, c))


def _slab(j):
    return j * 8 if isinstance(j, int) else pl.multiple_of(j * 8, 8)


def _scan_head(xr, xi, tre_ref, tim_ref, car_r, car_i, desc, conj, cs):
    sg = -1.0 if conj else 1.0
    ar = _bcast8(tre_ref[0:1, :], cs)
    ai = sg * _bcast8(tim_ref[0:1, :], cs)

    def p1(jj, carry):
        hr, hi = carry
        off = _slab(SEG - 1 - jj if desc else jj)
        pr, pi_ = _cmul(ar, ai, hr, hi)
        nr = pr + xr[pl.ds(off, 8), :]
        ni = pi_ + xi[pl.ds(off, 8), :]
        xr[pl.ds(off, 8), :] = nr
        xi[pl.ds(off, 8), :] = ni
        return nr, ni

    zero = jnp.zeros((8, cs), F32)
    ir, ii = lax.fori_loop(0, SEG, p1, (zero, zero), unroll=S5_UNROLL)
    cin_r, cin_i = car_r[...], car_i[...]
    rows = lax.broadcasted_iota(jnp.int32, (8, cs), 0)
    for s, krow in ((1, SEG), (2, SEG + 1), (4, SEG + 3)):
        keep, sh = (rows < 8 - s, 8 - s) if desc else (rows >= s, s)
        sr = jnp.where(keep, pltpu.roll(ir, sh, 0), 0.0)
        si = jnp.where(keep, pltpu.roll(ii, sh, 0), 0.0)
        pr, pi_ = _cmul(tre_ref[krow:krow + 1, :], sg * tim_ref[krow:krow + 1, :], sr, si)
        ir, ii = ir + pr, ii + pi_
    q0 = SEG + 8 if desc else SEG
    pr, pi_ = _cmul(tre_ref[q0:q0 + 8, :], sg * tim_ref[q0:q0 + 8, :], cin_r, cin_i)
    fr, fi = ir + pr, ii + pi_
    keep, sh, edge = (rows < 7, 7, 0) if desc else (rows >= 1, 1, 7)
    cs_r = jnp.where(keep, pltpu.roll(fr, sh, 0), cin_r)
    cs_i = jnp.where(keep, pltpu.roll(fi, sh, 0), cin_i)
    car_r[...] = _bcast8(fr[edge:edge + 1, :], cs)
    car_i[...] = _bcast8(fi[edge:edge + 1, :], cs)
    return cs_r, cs_i, cin_r, cin_i


def _pow_row(tre_ref, tim_ref, j, desc, conj, cs):
    k = SEG - 1 - j if desc else j
    sg = -1.0 if conj else 1.0
    return _bcast8(tre_ref[pl.ds(k, 1), :], cs), sg * _bcast8(tim_ref[pl.ds(k, 1), :], cs)


def _to_segments(val, dst):
    for r in range(8):
        dst[pl.ds(r, SEG, stride=8), :] = val[r * SEG:(r + 1) * SEG, :]


def _from_segments(src):
    return jnp.concatenate([src[pl.ds(r, SEG, stride=8), :] for r in range(8)], axis=0)


def _s5_fwd(up, bre, bim, cmr, cmi, tre, tim, dskip, order, nproc, desc, with_skip, name):
    t, cu = up.shape
    nsb = bre.shape[0]
    cb = cu // nsb
    cs = bre.shape[2]
    nch = t // TC

    def body(u_ref, bre_ref, bim_ref, cmr_ref, cmi_ref, tre_ref, tim_ref, dsk_ref,
             y_ref, hr_ref, hi_ref, cinr_ref, cini_ref, car_r, car_i, seg_scr):
        i = pl.program_id(0)

        @pl.when(i == 0)
        def _():
            car_r[...] = jnp.zeros_like(car_r)
            car_i[...] = jnp.zeros_like(car_i)

        for sb in range(nsb):
            col, ch = pl.ds(sb * cs, cs), slice(sb * cb, (sb + 1) * cb)
            hr_v, hi_v = hr_ref.at[:, col], hi_ref.at[:, col]
            tre_v, tim_v = tre_ref.at[:, col], tim_ref.at[:, col]
            useg, yseg = seg_scr.at[0, sb], seg_scr.at[1, sb]
            _to_segments(u_ref[:, ch], useg)
            u = useg[...]
            ub = u.astype(BF16)
            hr_v[...] = _dot(ub, bre_ref[sb])
            hi_v[...] = _dot(ub, bim_ref[sb])
            cs_r, cs_i, cin_r, cin_i = _scan_head(hr_v, hi_v, tre_v, tim_v, car_r.at[sb], car_i.at[sb], desc, False, cs)
            cinr_ref[:, col] = cin_r
            cini_ref[:, col] = cin_i

            def p2(j, _, hr_v=hr_v, hi_v=hi_v, tre_v=tre_v, tim_v=tim_v, cs_r=cs_r, cs_i=cs_i):
                off = _slab(j)
                pw_r, pw_i = _pow_row(tre_v, tim_v, j, desc, False, cs)
                pr, pi_ = _cmul(pw_r, pw_i, cs_r, cs_i)
                hr_v[pl.ds(off, 8), :] = hr_v[pl.ds(off, 8), :] + pr
                hi_v[pl.ds(off, 8), :] = hi_v[pl.ds(off, 8), :] + pi_
                return 0

            lax.fori_loop(0, SEG, p2, 0, unroll=S5_UNROLL)
            y = _dot(hr_v[...].astype(BF16), cmr_ref[sb]) - _dot(hi_v[...].astype(BF16), cmi_ref[sb])
            if with_skip:
                y = y + dsk_ref[:, ch] * u
            yseg[...] = y
            y_ref[:, ch] = _from_segments(yseg)

    blk = lambda i: (order(i), 0)
    return _pcall(
        body, name=name, grid=(nproc,),
        out_shape=(jax.ShapeDtypeStruct((t, cu), F32), jax.ShapeDtypeStruct((t, nsb * cs), F32),
                   jax.ShapeDtypeStruct((t, nsb * cs), F32), jax.ShapeDtypeStruct((nch * 8, nsb * cs), F32),
                   jax.ShapeDtypeStruct((nch * 8, nsb * cs), F32)),
        in_specs=[pl.BlockSpec((TC, cu), blk), _const(bre.shape), _const(bim.shape), _const(cmr.shape),
                  _const(cmi.shape), _const(tre.shape), _const(tim.shape), _const(dskip.shape)],
        out_specs=(pl.BlockSpec((TC, cu), blk), pl.BlockSpec((TC, nsb * cs), blk), pl.BlockSpec((TC, nsb * cs), blk),
                   pl.BlockSpec((8, nsb * cs), blk), pl.BlockSpec((8, nsb * cs), blk)),
        scratch_shapes=[pltpu.VMEM((nsb, 8, cs), F32), pltpu.VMEM((nsb, 8, cs), F32),
                        pltpu.VMEM((2, nsb, TC, cb), F32)],
        compiler_params=_params(("arbitrary",)),
    )(up, bre, bim, cmr, cmi, tre, tim, dskip)


def _s5_bwd(dy_lat, up, hr, hi, cinr, cini, bre, bim, ctr, cti, tre, tim, dskip, order, nproc, desc, with_skip, name,
            xchg=None):
    t, cu = up.shape
    nsb = bre.shape[0]
    cb = cu // nsb
    cs = bre.shape[2]
    lch = dy_lat.shape[0] // TC
    adesc = not desc

    def body(dy_ref, u_ref, hr_ref, hi_ref, cinr_ref, cini_ref, bre_ref, bim_ref, ctr_ref, cti_ref, tre_ref, tim_ref,
             dsk_ref, du_ref, dar_ref, dai_ref, dcr_ref, dci_ref, dbr_ref, dbi_ref, dd_ref,
             car_r, car_i, mr_all, mi_all, acc_r, acc_i, seg_scr):
        i = pl.program_id(0)
        latent = (order(i) < lch).astype(F32)

        @pl.when(i == 0)
        def _():
            for ref in (car_r, car_i, acc_r, acc_i, dcr_ref, dci_ref, dbr_ref, dbi_ref, dd_ref):
                ref[...] = jnp.zeros_like(ref)

        rows = lax.broadcasted_iota(jnp.int32, (8, cs), 0)
        for sb in range(nsb):
            col, ch = pl.ds(sb * cs, cs), slice(sb * cb, (sb + 1) * cb)
            hr_v, hi_v = hr_ref.at[:, col], hi_ref.at[:, col]
            tre_v, tim_v = tre_ref.at[:, col], tim_ref.at[:, col]
            mr, mi = mr_all.at[sb % 2], mi_all.at[sb % 2]
            useg, dyseg, duseg = seg_scr.at[0, sb], seg_scr.at[1, sb], seg_scr.at[2, sb]
            _to_segments(dy_ref[:, ch] * latent, dyseg)
            _to_segments(u_ref[:, ch], useg)
            dy = dyseg[...]
            dyb = dy.astype(BF16)
            u = useg[...]
            ub = u.astype(BF16)
            mr[...] = _dot(dyb, ctr_ref[sb])
            mi[...] = -_dot(dyb, cti_ref[sb])
            cs_r, cs_i, _, _ = _scan_head(mr, mi, tre_v, tim_v, car_r.at[sb], car_i.at[sb], adesc, True, cs)

            def fix(j, hp_r, hp_i, acc, mr=mr, mi=mi, tre_v=tre_v, tim_v=tim_v, cs_r=cs_r, cs_i=cs_i):
                off = _slab(j)
                pw_r, pw_i = _pow_row(tre_v, tim_v, j, adesc, True, cs)
                pr, pi_ = _cmul(pw_r, pw_i, cs_r, cs_i)
                m_r = mr[pl.ds(off, 8), :] + pr
                m_i = mi[pl.ds(off, 8), :] + pi_
                mr[pl.ds(off, 8), :] = m_r
                mi[pl.ds(off, 8), :] = m_i
                a_r, a_i = acc
                return a_r + hp_r * m_r + hp_i * m_i, a_i + hp_r * m_i - hp_i * m_r

            edge_j, src, keep, sh = (SEG - 1, 0, rows < 7, 7) if desc else (0, (SEG - 1) * 8, rows >= 1, 1)
            h0r = jnp.where(keep, pltpu.roll(hr_v[src:src + 8, :], sh, 0), cinr_ref[:, col])
            h0i = jnp.where(keep, pltpu.roll(hi_v[src:src + 8, :], sh, 0), cini_ref[:, col])
            acc = fix(edge_j, h0r, h0i, (acc_r[sb], acc_i[sb]))

            def p2(jj, acc, fix=fix, hr_v=hr_v, hi_v=hi_v):
                j = jj if desc else jj + 1
                offp = _slab(j + 1 if desc else j - 1)
                return fix(j, hr_v[pl.ds(offp, 8), :], hi_v[pl.ds(offp, 8), :], acc)

            a_r, a_i = lax.fori_loop(0, SEG - 1, p2, acc, unroll=S5_UNROLL)
            acc_r[sb] = a_r
            acc_i[sb] = a_i

            mrb, mib = mr[...].astype(BF16), mi[...].astype(BF16)
            du = _dot_nt(mrb, bre_ref[sb]) + _dot_nt(mib, bim_ref[sb])
            if with_skip:
                du = du + dsk_ref[:, ch] * dy
                dd_ref[:, ch] += _colsum(dy * u)
            duseg[...] = du
            du_ref[:, ch] = _from_segments(duseg)
            dbr_ref[sb] += _dot_tn(ub, mrb)
            dbi_ref[sb] += _dot_tn(ub, mib)
            dcr_ref[sb] += _dot_tn(dyb, hr_v[...].astype(BF16))
            dci_ref[sb] -= _dot_tn(dyb, hi_v[...].astype(BF16))

            @pl.when(i == nproc - 1)
            def _(sb=sb, a_r=a_r, a_i=a_i):
                dar_ref[sb] = _colsum(a_r)
                dai_ref[sb] = _colsum(a_i)

    blk = lambda i: (order(i), 0)
    blk_dy = lambda i: (jnp.minimum(order(i), lch - 1), 0)
    mat = (nsb, cb, cs)
    return _pcall(
        body, name=name, grid=(nproc,), xchg=xchg, edges=_edges1(nproc),
        out_shape=(jax.ShapeDtypeStruct((t, cu), F32),
                   jax.ShapeDtypeStruct((nsb, 1, cs), F32), jax.ShapeDtypeStruct((nsb, 1, cs), F32),
                   jax.ShapeDtypeStruct(mat, F32), jax.ShapeDtypeStruct(mat, F32),
                   jax.ShapeDtypeStruct(mat, F32), jax.ShapeDtypeStruct(mat, F32),
                   jax.ShapeDtypeStruct((1, cu), F32)),
        in_specs=[pl.BlockSpec((TC, cu), blk_dy), pl.BlockSpec((TC, cu), blk), pl.BlockSpec((TC, nsb * cs), blk),
                  pl.BlockSpec((TC, nsb * cs), blk), pl.BlockSpec((8, nsb * cs), blk), pl.BlockSpec((8, nsb * cs), blk),
                  _const(mat), _const(mat), _const(mat), _const(mat), _const(tre.shape), _const(tim.shape),
                  _const((1, cu))],
        out_specs=(pl.BlockSpec((TC, cu), blk), _const((nsb, 1, cs)), _const((nsb, 1, cs)),
                   _const(mat), _const(mat), _const(mat), _const(mat), _const((1, cu))),
        scratch_shapes=[pltpu.VMEM((nsb, 8, cs), F32), pltpu.VMEM((nsb, 8, cs), F32), pltpu.VMEM((2, TC, cs), F32),
                        pltpu.VMEM((2, TC, cs), F32), pltpu.VMEM((nsb, 8, cs), F32), pltpu.VMEM((nsb, 8, cs), F32),
                        pltpu.VMEM((3, nsb, TC, cb), F32)],
        compiler_params=_params(("arbitrary",)),
    )(dy_lat, up, hr, hi, cinr, cini, bre, bim, ctr, cti, tre, tim, dskip)


def _gelu(x):
    k = math.sqrt(2.0 / math.pi)
    return 0.5 * x * (1.0 + jnp.tanh(k * (x + 0.044715 * (x * x * x))))


def _gelu_grad(x):
    k = math.sqrt(2.0 / math.pi)
    th = jnp.tanh(k * (x + 0.044715 * (x * x * x)))
    return 0.5 * (1.0 + th) + 0.5 * x * (1.0 - th * th) * (k * (1.0 + 3.0 * 0.044715 * (x * x)))


def _mix_out_fwd(yf, yb, dpool, x1, modtab, w_pool, pscale, w_glu, w_out):
    l, hm = dpool.shape
    d = x1.shape[1]

    def body(yf_ref, yb_ref, dp_ref, x_ref, mod_ref, wp_ref, ps_ref, wg_ref, wo_ref, x2_ref, yp_ref, cat_ref, mix_ref):
        ypre = yf_ref[...] + yb_ref[...]
        yp_ref[...] = ypre
        yg = _gelu(ypre)
        y2 = yg * _sigmoid(_dot(yg.astype(BF16), wg_ref[...]))
        po = _dot(dp_ref[...].astype(BF16), wp_ref[...]) * ps_ref[...]
        cat = jnp.concatenate([po, y2], axis=1).astype(BF16)
        cat_ref[...] = cat
        mix = _dot(cat, wo_ref[...])
        mix_ref[...] = mix.astype(BF16)
        x2_ref[...] = x_ref[...] + mod_ref[0, 5:6, :] * mix

    row = lambda i: (i, 0)
    lrow = row
    return _pcall(
        body, name="mix_out_fwd", grid=(l // TM,),
        out_shape=(jax.ShapeDtypeStruct((l, d), F32), jax.ShapeDtypeStruct((l, hm), F32),
                   jax.ShapeDtypeStruct((l, d), BF16), jax.ShapeDtypeStruct((l, d), BF16)),
        in_specs=[pl.BlockSpec((TM, hm), lrow), pl.BlockSpec((TM, hm), lrow), pl.BlockSpec((TM, hm), row),
                  pl.BlockSpec((TM, d), lrow), _mod_spec(d, False), _const(w_pool.shape), _const((1, hm)),
                  _const(w_glu.shape), _resident(w_out.shape)],
        out_specs=(pl.BlockSpec((TM, d), row), pl.BlockSpec((TM, hm), row), pl.BlockSpec((TM, d), row),
                   pl.BlockSpec((TM, d), row)),
        compiler_params=_params(("arbitrary",)),
    )(yf, yb, dpool, x1, modtab, w_pool, pscale, w_glu, w_out)


def _mix_out_bwd(dx2, mix, cat, ypre, dpool, modtab, w_pool, pscale, w_glu, w_out):
    l, hm = dpool.shape
    d = dx2.shape[1]
    nt = l // TML

    def body(dx_ref, mix_ref, cat_ref, yp_ref, dp_ref, mod_ref, wp_ref, ps_ref, wg_ref, wo_ref,
             dyp_ref, ddp_ref, dwo_ref, dwg_ref, dwp_ref, dps_ref, dgate_ref, acc_o, acc_g):
        i = pl.program_id(0)

        @pl.when(i == 0)
        def _():
            for ref in (acc_o, acc_g, dwp_ref, dps_ref, dgate_ref):
                ref[...] = jnp.zeros_like(ref)

        dx = dx_ref[...]
        dgate_ref[...] += _colsum(dx * mix_ref[...].astype(F32))
        dmix = (dx * mod_ref[0, 5:6, :]).astype(BF16)
        dcat = _dot_nt(dmix, wo_ref[...])
        acc_o[...] += _dot_tn(cat_ref[...], dmix)
        dpo, dy2 = dcat[:, :hm], dcat[:, hm:]
        dpb = dp_ref[...].astype(BF16)
        pp = _dot(dpb, wp_ref[...])
        dps_ref[...] += _colsum(dpo * pp)
        dpp = (dpo * ps_ref[...]).astype(BF16)
        ddp_ref[...] = _dot_nt(dpp, wp_ref[...])
        dwp_ref[...] += _dot_tn(dpb, dpp)
        ypre = yp_ref[...]
        yg = _gelu(ypre)
        ygb = yg.astype(BF16)
        s = _sigmoid(_dot(ygb, wg_ref[...]))
        dq = (dy2 * yg * s * (1.0 - s)).astype(BF16)
        dyg = dy2 * s + _dot_nt(dq, wg_ref[...])
        acc_g[...] += _dot_tn(ygb, dq)
        dyp_ref[...] = dyg * _gelu_grad(ypre)

        @pl.when(i == nt - 1)
        def _():
            dwo_ref[...] = acc_o[...].astype(BF16)
            dwg_ref[...] = acc_g[...].astype(BF16)

    row = lambda i: (i, 0)
    return _pcall(
        body, name="mix_out_bwd", grid=(nt,),
        out_shape=(jax.ShapeDtypeStruct((l, hm), F32), jax.ShapeDtypeStruct((l, hm), F32),
                   jax.ShapeDtypeStruct((d, d), BF16), jax.ShapeDtypeStruct((hm, hm), BF16),
                   jax.ShapeDtypeStruct((hm, hm), F32), jax.ShapeDtypeStruct((1, hm), F32),
                   jax.ShapeDtypeStruct((1, d), F32)),
        in_specs=[pl.BlockSpec((TML, d), row), pl.BlockSpec((TML, d), row), pl.BlockSpec((TML, d), row),
                  pl.BlockSpec((TML, hm), row), pl.BlockSpec((TML, hm), row), _mod_spec(d, False),
                  _const(w_pool.shape), _const((1, hm)), _const(w_glu.shape), _resident(w_out.shape)],
        out_specs=(pl.BlockSpec((TML, hm), row), pl.BlockSpec((TML, hm), row), _const((d, d)), _const((hm, hm)),
                   _const((hm, hm)), _const((1, hm)), _const((1, d))),
        scratch_shapes=[pltpu.VMEM((d, d), F32), pltpu.VMEM((hm, hm), F32)],
        compiler_params=_params(("arbitrary",)),
    )(dx2, mix, cat, ypre, dpool, modtab, w_pool, pscale, w_glu, w_out)


def _loss_head(xx, target, gg, loss_ref, dg_ref):
    d = xx.shape[-1]
    rstd = lax.rsqrt(jnp.mean(xx * xx, axis=-1, keepdims=True) + EPS)
    xhat = xx * rstd
    err = xhat * gg - target
    row_loss = jnp.mean(err * err, axis=-1, keepdims=True)
    loss_ref[...] += 0.5 * jnp.sum(row_loss, axis=0, keepdims=True)
    dout = err * (1.0 / d)
    dg_ref[...] += _colsum(dout * xhat)
    dxhat = dout * gg
    return rstd * (dxhat - xhat * jnp.mean(dxhat * xhat, axis=-1, keepdims=True))


def _adamw(parts, w, m, v, name):
    shape = w.shape
    c = shape[-1]
    r = int(np.prod(shape)) // c
    npart = parts.shape[0]
    tr = r
    for cand in (1024, 512, 256, 128, 64, 32, 16):
        if r % cand == 0 and cand * max(c, LANE) * 4 <= ADAM_BLOCK_BYTES:
            tr = cand
            break
    c1 = 1.0 - ADAM_B1
    c2 = 1.0 - ADAM_B2
    bc1 = 1.0 - ADAM_B1 ** ADAM_STEP
    bc2 = 1.0 - ADAM_B2 ** ADAM_STEP

    def body(p_ref, w_ref, m_ref, v_ref, g_ref, d_ref, mo_ref, vo_ref):
        g = p_ref[0].astype(F32)
        for k in range(1, npart):
            g = g + p_ref[k].astype(F32)
        mn = ADAM_B1 * m_ref[...] + c1 * g
        vn = ADAM_B2 * v_ref[...] + c2 * (g * g)
        m_hat = mn / bc1
        v_hat = vn / bc2
        g_ref[...] = g
        mo_ref[...] = mn
        vo_ref[...] = vn
        d_ref[...] = -ADAM_LR * (m_hat / (jnp.sqrt(v_hat) + ADAM_EPS) + ADAM_WD * w_ref[...])

    spec = pl.BlockSpec((tr, c), lambda i: (i, 0))
    outs = _pcall(
        body, name=name, grid=(r // tr,),
        out_shape=tuple(jax.ShapeDtypeStruct((r, c), F32) for _ in range(4)),
        in_specs=[pl.BlockSpec((npart, tr, c), lambda i: (0, i, 0)), spec, spec, spec],
        out_specs=(spec, spec, spec, spec),
        compiler_params=_params(("arbitrary",)),
    )(parts.reshape(npart, r, c), w.reshape(r, c), m.reshape(r, c), v.reshape(r, c))
    return tuple(o.reshape(shape) for o in outs)


def _adamw_ffn(parts, row0, w, m, v, name):
    _, nl, r, c = w.shape
    rp = parts[0][0].shape[1]
    tc = ADAM_FFN_LANES
    grid = (c // tc,)
    p_spec = lambda a: pl.BlockSpec((a.shape[0], rp, tc), lambda i: (0, 0, i))
    w_spec = pl.BlockSpec((1, nl, r, tc), lambda i: (0, 0, 0, i))
    take = lambda ref, k: ref[k, row0:row0 + r, :]
    flat = [a for layer in parts for a in layer]
    counts = [len(layer) for layer in parts]
    c1 = 1.0 - ADAM_B1
    c2 = 1.0 - ADAM_B2
    bc1 = 1.0 - ADAM_B1 ** ADAM_STEP
    bc2 = 1.0 - ADAM_B2 ** ADAM_STEP

    def body(*refs):
        w_ref, m_ref, v_ref, g_ref, d_ref, mo_ref, vo_ref = refs[len(flat):]
        first = 0
        for lyr in range(nl):
            g = None
            for ref in refs[first:first + counts[lyr]]:
                for k in range(ref.shape[0]):
                    term = take(ref, k).astype(F32)
                    g = term if g is None else g + term
            first += counts[lyr]
            mn = ADAM_B1 * m_ref[0, lyr] + c1 * g
            vn = ADAM_B2 * v_ref[0, lyr] + c2 * (g * g)
            g_ref[0, lyr] = g
            mo_ref[0, lyr] = mn
            vo_ref[0, lyr] = vn
            d_ref[0, lyr] = -ADAM_LR * ((mn / bc1) / (jnp.sqrt(vn / bc2) + ADAM_EPS) + ADAM_WD * w_ref[0, lyr])

    return _pcall(
        body, name=name, grid=grid, out_shape=tuple(jax.ShapeDtypeStruct(w.shape, F32) for _ in range(4)),
        in_specs=[p_spec(a) for a in flat] + [w_spec] * 3, out_specs=(w_spec,) * 4,
        compiler_params=_params(("arbitrary",)),
    )(*flat, w, m, v)


def _blockdiag(xb):
    n, a, b = xb.shape[-3:]
    eye = jnp.eye(n, dtype=xb.dtype)
    out = xb[..., :, :, None, :] * eye[:, None, :, None]
    return out.reshape(xb.shape[:-3] + (n * a, n * b))


def _diagblocks(mat, n):
    lead = mat.shape[:-2]
    a, b = mat.shape[-2] // n, mat.shape[-1] // n
    m5 = mat.reshape(lead + (n, a, n, b))
    nl = len(lead)
    dg = jnp.diagonal(m5, axis1=nl, axis2=nl + 2)
    return jnp.moveaxis(dg, -1, nl)


def _pad128(a):
    flat = a.reshape(-1)
    pad = (-flat.shape[0]) % LANE
    return jnp.pad(flat, (0, pad)) if pad else flat


def kernel(x, c, ctx, c_ctx, norm_g, w_ada, b_ada, ffn_w1, ffn_w3, ffn_w2, w_in, pool_w, pool_scale, s5_a_re, s5_a_im, s5_log_dt, s5_b_re, s5_b_im, s5_c_re, s5_c_im, s5_d, s5_w_glu, w_out, final_g, loss_target, m_c_ctx, m_norm_g, m_w_ada, m_b_ada, m_ffn_w1, m_ffn_w3, m_ffn_w2, m_w_in, m_pool_w, m_pool_scale, m_s5_a_re, m_s5_a_im, m_s5_log_dt, m_s5_b_re, m_s5_b_im, m_s5_c_re, m_s5_c_im, m_s5_d, m_s5_w_glu, m_w_out, m_final_g, v_c_ctx, v_norm_g, v_w_ada, v_b_ada, v_ffn_w1, v_ffn_w3, v_ffn_w2, v_w_in, v_pool_w, v_pool_scale, v_s5_a_re, v_s5_a_im, v_s5_log_dt, v_s5_b_re, v_s5_b_im, v_s5_c_re, v_s5_c_im, v_s5_d, v_s5_w_glu, v_w_out, v_final_g):
    weights = dict(c_ctx=c_ctx, norm_g=norm_g, w_ada=w_ada, b_ada=b_ada, ffn_w1=ffn_w1, ffn_w3=ffn_w3, ffn_w2=ffn_w2,
                   w_in=w_in, pool_w=pool_w, pool_scale=pool_scale, s5_a_re=s5_a_re, s5_a_im=s5_a_im,
                   s5_log_dt=s5_log_dt, s5_b_re=s5_b_re, s5_b_im=s5_b_im, s5_c_re=s5_c_re, s5_c_im=s5_c_im, s5_d=s5_d,
                   s5_w_glu=s5_w_glu, w_out=w_out, final_g=final_g)
    mom_m = dict(c_ctx=m_c_ctx, norm_g=m_norm_g, w_ada=m_w_ada, b_ada=m_b_ada, ffn_w1=m_ffn_w1, ffn_w3=m_ffn_w3,
                 ffn_w2=m_ffn_w2, w_in=m_w_in, pool_w=m_pool_w, pool_scale=m_pool_scale, s5_a_re=m_s5_a_re,
                 s5_a_im=m_s5_a_im, s5_log_dt=m_s5_log_dt, s5_b_re=m_s5_b_re, s5_b_im=m_s5_b_im, s5_c_re=m_s5_c_re,
                 s5_c_im=m_s5_c_im, s5_d=m_s5_d, s5_w_glu=m_s5_w_glu, w_out=m_w_out, final_g=m_final_g)
    mom_v = dict(c_ctx=v_c_ctx, norm_g=v_norm_g, w_ada=v_w_ada, b_ada=v_b_ada, ffn_w1=v_ffn_w1, ffn_w3=v_ffn_w3,
                 ffn_w2=v_ffn_w2, w_in=v_w_in, pool_w=v_pool_w, pool_scale=v_pool_scale, s5_a_re=v_s5_a_re,
                 s5_a_im=v_s5_a_im, s5_log_dt=v_s5_log_dt, s5_b_re=v_s5_b_re, s5_b_im=v_s5_b_im, s5_c_re=v_s5_c_re,
                 s5_c_im=v_s5_c_im, s5_d=v_s5_d, s5_w_glu=v_s5_w_glu, w_out=v_w_out, final_g=v_final_g)

    l, d = x.shape[1], x.shape[2]
    lc = ctx.shape[1]
    t = l + TM
    rows = l // GRID_W
    fblk = ffn_w1.shape[-1]
    ngrp, gp = s5_a_re.shape[2], s5_a_re.shape[3]
    gc = s5_b_re.shape[3]
    hm = ngrp * gc
    nsb = 4
    gsb = ngrp // nsb
    assert lc == TC and TM % TC == 0 and l % TM == 0 and TM == TML and hm == 4 * LANE and ngrp * gp == nsb * 512
    me = 4 * lax.axis_index("x") + 2 * lax.axis_index("y") + lax.axis_index("c")

    padr = ((0, FPAD - fblk), (0, 0))
    w1_t, w3_t = jnp.swapaxes(ffn_w1, 2, 3), jnp.swapaxes(ffn_w3, 2, 3)
    w13_loc = [jnp.concatenate([jnp.pad(w1_t[0, k], padr), jnp.pad(w3_t[0, k], padr)], axis=0).astype(BF16)
               for k in range(2)]
    w2_loc = [jnp.pad(ffn_w2[0, k], padr).astype(BF16) for k in range(2)]
    gath = _exchange([(w13_loc[0], GATHER_REL), (w2_loc[0], GATHER_REL), (c, GATHER), (norm_g[0], GATHER)],
                     "exchange_weights")
    w13_0 = gath[0]
    w2_0 = gath[1].reshape(NDEV // 2, 2 * FPAD, d)
    c_all = gath[2].reshape(NDEV, d)
    g_all = gath[3].transpose(1, 0, 2).reshape(3, d)

    wa = w_ada.shape[-1]
    c16 = jnp.concatenate([c_all, c_ctx.reshape(1, d), jnp.zeros((7, d), F32)], axis=0)
    b_blk = lax.dynamic_slice_in_dim(b_ada, me * wa, wa, axis=1)
    mod_sl = _ada_fwd(c16, w_ada[0], b_blk)
    gath = _exchange([(mod_sl, GATHER)], "exchange_mod")
    mod_full = gath[0].transpose(1, 0, 2).reshape(16, NDEV * wa)
    mod_l = lax.dynamic_index_in_dim(mod_full, me, axis=0, keepdims=False).reshape(9, d)
    modtab = jnp.stack([mod_full[8].reshape(9, d), mod_l])

    x0 = [jnp.pad(ctx[0], ((0, TM - lc), (0, 0))), x[0]]
    g0, g1, g2 = g_all[0:1], g_all[1:2], g_all[2:3]
    x1, h0, ab0, o0, w13_1, w2_1, w_in_f, w_out_f, w_glu_f = _ffn_fwd(
        x0, modtab, g0, w13_0, w2_0, 0, True, "ffn0_fwd",
        xchg=[(w13_loc[1], GATHER_REL), (w2_loc[1], GATHER_REL), (w_in[0].astype(BF16), GATHER),
              (w_out[0].astype(BF16), GATHER), (s5_w_glu[0].astype(BF16), GATHER)])
    w2_1 = w2_1.reshape(NDEV // 2, 2 * FPAD, d)
    w_in_f = w_in_f.reshape(d, -1)
    w_out_f = w_out_f.reshape(-1, d)
    w_glu_f = w_glu_f.reshape(hm, hm)
    h1, u_pool, up = _mix_in_fwd(x1, modtab, g1, w_in_f)

    dpool = _pool(u_pool, rows, False, "pool_fwd")
    w_pool_bd = _blockdiag(pool_w[0]).astype(BF16)

    q = 2 * ngrp
    kvec = np.concatenate([np.arange(1, SEG + 1), SEG * np.arange(1, 9), SEG * np.arange(8, 0, -1)]).astype(np.float32)
    a_re3, a_im3 = s5_a_re[0].reshape(q, 1, gp), s5_a_im[0].reshape(q, 1, gp)
    ldt3 = jnp.broadcast_to(s5_log_dt[0].reshape(q, 1, 1), (q, 1, gp))
    c_re3, c_im3 = s5_c_re[0].reshape(q, gc, gp), s5_c_im[0].reshape(q, gc, gp)
    cp_re, cp_im, tab_re, tab_im = _s5_prep(a_re3, a_im3, ldt3, c_re3, c_im3, jnp.broadcast_to(jnp.asarray(kvec).reshape(1, NTAB, 1), (1, NTAB, gp)))
    tabs = [(tab_re.reshape(2, ngrp, NTAB, gp)[k].transpose(1, 0, 2).reshape(NTAB, ngrp * gp),
             tab_im.reshape(2, ngrp, NTAB, gp)[k].transpose(1, 0, 2).reshape(NTAB, ngrp * gp)) for k in range(2)]
    b_t = lambda b: _blockdiag(b[0].reshape(nsb, gsb, gp, gc).transpose(0, 1, 3, 2)).astype(BF16)
    bre, bim = b_t(s5_b_re), b_t(s5_b_im)
    cp_t = lambda cp: _blockdiag(cp.reshape(2, nsb, gsb, gc, gp)).astype(BF16)
    ctr, cti = cp_t(cp_re), cp_t(cp_im)
    cmr, cmi = jnp.swapaxes(ctr, -1, -2), jnp.swapaxes(cti, -1, -2)
    dskip = s5_d

    lch = l // TC
    nproc = lch + 1
    order_f = lambda i: jnp.where(i == 0, lch, i - 1)
    order_b = lambda i: jnp.where(i == 0, lch, lch - i)
    rorder_f = lambda i: jnp.where(i == lch, lch, lch - 1 - i)
    rorder_b = lambda i: i
    y_f, hr_f, hi_f, cinr_f, cini_f = _s5_fwd(up, bre, bim, cmr[0], cmi[0], tabs[0][0], tabs[0][1], dskip,
                                               order_f, nproc, False, True, "s5_fwd_f")
    y_b, hr_b, hi_b, cinr_b, cini_b = _s5_fwd(up, bre, bim, cmr[1], cmi[1], tabs[1][0], tabs[1][1], dskip,
                                              order_b, nproc, True, False, "s5_fwd_b")

    x2, ypre, cat, mix = _mix_out_fwd(y_f, y_b, dpool, x1, modtab, w_pool_bd, pool_scale, w_glu_f, w_out_f)
    dx3, h2, ab2, o2, loss_part, dfinal_g = _ffn_fwd([x2], modtab, g2, w13_1, w2_1, 6, False, "ffn1_fwd",
                                                     head=(loss_target[0], final_g.reshape(1, d)))

    dhp2, d13_1, d2_1 = _ffn_bwd(dx3, modtab, h2, ab2, w13_1, w2_1, 6, False, "ffn1_bwd")
    dx2, dmod_678, dg2 = _ffn_bwd_norm(dx3, [dhp2], [x2], o2, modtab, g2, 6, False, "ffn1_bwd_norm")
    dypre, ddpool, dw_out, dw_glu, dw_pool_bd, dpscale, dgate5 = _mix_out_bwd(
        dx2, mix, cat, ypre, dpool, modtab, w_pool_bd, pool_scale, w_glu_f, w_out_f)

    du_f, dar_f, dai_f, dcr_f, dci_f, dbr_f, dbi_f, dd_skip, r13_1, r2_1 = _s5_bwd(
        dypre, up, hr_f, hi_f, cinr_f, cini_f, bre, bim, ctr[0], cti[0], tabs[0][0], tabs[0][1], dskip,
        rorder_f, nproc, False, True, "s5_bwd_f",
        xchg=[(d13_1, A2A_REL), (d2_1.reshape(NDEV, FPAD, d), A2A_REL)])
    du_b, dar_b, dai_b, dcr_b, dci_b, dbr_b, dbi_b, _, r_out, r_glu = _s5_bwd(
        dypre, up, hr_b, hi_b, cinr_b, cini_b, bre, bim, ctr[1], cti[1], tabs[1][0], tabs[1][1], dskip,
        rorder_b, nproc, True, False, "s5_bwd_b",
        xchg=[(dw_out.reshape(NDEV, -1, d), A2A), (dw_glu.reshape(NDEV, hm // NDEV, hm), A2A)])

    dab_r = jnp.stack([dar_f, dar_b]).reshape(q, 1, gp)
    dab_i = jnp.stack([dai_f, dai_b]).reshape(q, 1, gp)
    dcp_r = _diagblocks(jnp.stack([dcr_f, dcr_b]), gsb).reshape(q, gc, gp)
    dcp_i = _diagblocks(jnp.stack([dci_f, dci_b]), gsb).reshape(q, gc, gp)
    ga_re, ga_im, gldt, gc_re, gc_im = _s5_param_bwd(a_re3, a_im3, ldt3, c_re3, c_im3, dab_r, dab_i, dcp_r, dcp_i)
    gb_re = (_diagblocks(dbr_f, gsb) + _diagblocks(dbr_b, gsb)).transpose(0, 1, 3, 2).reshape(ngrp, gp, gc)
    gb_im = (_diagblocks(dbi_f, gsb) + _diagblocks(dbi_b, gsb)).transpose(0, 1, 3, 2).reshape(ngrp, gp, gc)

    def pack(small):
        offs, pieces, off = {}, [], 0
        for k_, a_ in small.items():
            p_ = _pad128(a_.astype(F32))
            offs[k_] = (off, int(np.prod(a_.shape)))
            off += p_.shape[0]
            pieces.append(p_)
        return jnp.concatenate(pieces).reshape(1, off), offs

    gw_pool = _diagblocks(dw_pool_bd, 4)
    bundle_a, offs = pack(dict(pool_w=gw_pool, pool_scale=dpscale, s5_a_re=ga_re, s5_a_im=ga_im,
                               s5_log_dt=gldt[:, 0, 0], s5_b_re=gb_re, s5_b_im=gb_im, s5_c_re=gc_re, s5_c_im=gc_im,
                               s5_d=dd_skip, final_g=dfinal_g))

    du_pool = _pool(ddpool, rows, True, "pool_bwd")
    dx1, dw_in, dmod_34, dg1 = _mix_in_bwd(du_pool, du_f, du_b, dx2, h1, x1, modtab, g1, w_in_f, lc)
    dhp0a, d13_0a, d2_0a, bund_a, r_in = _ffn_bwd(
        dx1, modtab, h0, ab0, w13_0, w2_0, 0, True, "ffn0_bwd_a", pair0=0, npair=2,
        xchg=[(bundle_a, GATHER), (dw_in.reshape(NDEV, d // NDEV, -1), A2A)])
    dhp0b, d13_0b, d2_0b, r13_0a, r2_0a = _ffn_bwd(
        dx1, modtab, h0, ab0, w13_0, w2_0, 0, True, "ffn0_bwd_b", pair0=2, npair=2,
        xchg=[(d13_0a, A2A_SAME), (d2_0a.reshape(NDEV // 2, FPAD, d), A2A_SAME)])
    dx0, dmod_012, dg0, r13_0b, r2_0b = _ffn_bwd_norm(
        dx1, [dhp0a, dhp0b], x0, o0, modtab, g0, 0, True, "ffn0_bwd_norm",
        xchg=[(d13_0b, A2A_OTHER), (d2_0b.reshape(NDEV // 2, FPAD, d), A2A_OTHER)])
    grad_x = dx0.reshape(1, l, d)

    dmod_c = jnp.concatenate([dmod_012[0], dmod_34[0], jnp.zeros((4, d), F32)], axis=0)
    dmod_l = jnp.concatenate([dmod_012[1], dmod_34[1], dgate5, dmod_678[0]], axis=0)
    bundle_b, offs_b = pack(dict(norm_g=jnp.concatenate([dg0, dg1, dg2], axis=0), dmod_l=dmod_l, dmod_c=dmod_c,
                                 loss=loss_part[:, :1]))
    bund_b = _exchange([(bundle_b, GATHER)], "exchange_small")[0]
    bunds = {**{k_: (bund_a.reshape(NDEV, -1), v_) for k_, v_ in offs.items()},
             **{k_: (bund_b.reshape(NDEV, -1), v_) for k_, v_ in offs_b.items()}}
    r13, r2 = [(r13_0a, r13_0b), (r13_1,)], [(r2_0a, r2_0b), (r2_1,)]

    def piece(name):
        b_, (o_, n_) = bunds[name]
        return b_[:, o_:o_ + n_]

    loss = jnp.sum(piece("loss")[:, 0])

    dl_all = lax.dynamic_slice_in_dim(piece("dmod_l"), me * wa, wa, axis=1)
    dc_all = lax.dynamic_slice_in_dim(piece("dmod_c"), me * wa, wa, axis=1)
    g_wada, gc_part = _ada_bwd(c16, w_ada[0], dl_all, dc_all)
    gc_all = _exchange([(gc_part, GATHER)], "exchange_cctx")[0]

    parts = {
        "c_ctx": gc_all.reshape(NDEV, d),
        "norm_g": lax.dynamic_slice_in_dim(piece("norm_g").reshape(NDEV, 3, d), me * (d // NDEV), d // NDEV,
                                           axis=2).reshape((NDEV,) + norm_g.shape),
        "w_ada": g_wada.reshape((1,) + w_ada.shape),
        "b_ada": jnp.concatenate([piece("dmod_l"), piece("dmod_c")], axis=0).reshape((2 * NDEV,) + b_ada.shape),
        "w_in": r_in.reshape((NDEV,) + w_in.shape),
        "s5_w_glu": r_glu.reshape((NDEV,) + s5_w_glu.shape),
        "w_out": r_out.reshape((NDEV,) + w_out.shape),
    }
    for k_ in ("pool_w", "pool_scale", "s5_a_re", "s5_a_im", "s5_log_dt", "s5_b_re", "s5_b_im", "s5_c_re", "s5_c_im",
               "s5_d", "final_g"):
        parts[k_] = piece(k_).reshape((NDEV,) + weights[k_].shape)

    grads, deltas, new_m, new_v = [], [], [], []
    ffn_parts = {"ffn_w1": (r13, 0, True), "ffn_w3": (r13, FPAD, True), "ffn_w2": (r2, 0, False)}
    for k_ in weights:
        if k_ in ffn_parts:
            shards, row0, transposed = ffn_parts[k_]
            flip = (lambda a: jnp.swapaxes(a, 2, 3)) if transposed else (lambda a: a)
            outs = _adamw_ffn(shards, row0, flip(weights[k_]), flip(mom_m[k_]), flip(mom_v[k_]), "adamw_" + k_)
            g_, d_, m_, v_ = (flip(o_) for o_ in outs)
        else:
            g_, d_, m_, v_ = _adamw(parts[k_], weights[k_], mom_m[k_], mom_v[k_], "adamw_" + k_)
        grads.append(g_)
        deltas.append(d_)
        new_m.append(m_)
        new_v.append(v_)
    return (loss, grad_x, *grads, *deltas, *new_m, *new_v)
```

```python
import functools
import math

import numpy as np
import jax
import jax.numpy as jnp
from jax import lax
from jax.experimental import pallas as pl
from jax.experimental.pallas import tpu as pltpu

F32 = jnp.float32
BF16 = jnp.bfloat16
AXES = ("x", "y", "c")
NDEV = 8
EPS = 1e-6
TM = 512
TML = 512
TC = 256
SEG = TC // 8
NTAB = SEG + 16
S5_UNROLL = True
GRID_W = 64
POOL_WINDOWS = (2, 4, 8, 16)
LANE = 128
FPAD = 384
VMEM_LIMIT = 56 * 1024 * 1024
ADAM_BLOCK_BYTES = 1024 * 1024
ADAM_FFN_LANES = 256
ADAM_LR, ADAM_B1, ADAM_B2, ADAM_EPS, ADAM_WD, ADAM_STEP = 0.001, 0.9, 0.999, 1e-08, 0.01, 10


def _raw_call(body, kw):
    return pl.pallas_call(body, **kw)


def _pcall(body, xchg=None, edges=None, **kw):
    extra = ()
    if xchg:
        arrs, kinds = [a for a, _ in xchg], [k for _, k in xchg]
        n = len(arrs)
        n_in, n_out, n_scr = len(kw["in_specs"]), len(kw["out_shape"]), len(kw.get("scratch_shapes", ()))
        inner = body

        def hosted(*refs):
            a, b = n_in, n_in + n
            c_, d_ = b + n_out, b + n_out + n
            e = d_ + n_scr
            first, mid, last = edges()

            @pl.when(first)
            def _():
                _xchg_start(refs[a:b], refs[c_:d_], refs[e:], kinds)

            inner(*refs[:a], *refs[b:c_], *refs[d_:e])

            if any(k == GATHER_REL for k in kinds):
                @pl.when(mid)
                def _():
                    _xchg_relay(refs[a:b], refs[c_:d_], refs[e:], kinds)

            @pl.when(last)
            def _():
                _xchg_finish(refs[a:b], refs[c_:d_], refs[e:], kinds)

        body = hosted
        any_spec = pl.BlockSpec(memory_space=pl.ANY)
        kw = dict(kw, in_specs=list(kw["in_specs"]) + [any_spec] * n,
                  out_shape=tuple(kw["out_shape"]) + _xchg_shapes(arrs, kinds),
                  out_specs=tuple(kw["out_specs"]) + (any_spec,) * n,
                  scratch_shapes=list(kw.get("scratch_shapes", ())) + _xchg_sems(n))
        extra = tuple(arrs)
    call = _raw_call(body, kw)
    return (lambda *args: call(*args, *extra)) if extra else call


def _params(sem):
    return pltpu.CompilerParams(dimension_semantics=sem, vmem_limit_bytes=VMEM_LIMIT)


def _dot(a, b):
    return jnp.dot(a, b, preferred_element_type=F32)


def _dot_nt(a, b):
    return lax.dot_general(a, b, (((1,), (1,)), ((), ())), preferred_element_type=F32)


def _dot_tn(a, b):
    return lax.dot_general(a, b, (((0,), (0,)), ((), ())), preferred_element_type=F32)


def _sigmoid(x):
    return 1.0 / (1.0 + jnp.exp(-x))


def _colsum(a):
    return jnp.sum(a, axis=0, keepdims=True)


def _resident(shape):
    nd = len(shape)
    return pl.BlockSpec(shape, lambda *_: (0,) * nd, pipeline_mode=pl.Buffered(1))


def _const(shape):
    nd = len(shape)
    return pl.BlockSpec(shape, lambda *_: (0,) * nd)


GATHER = "gather"
GATHER_REL = "gather_rel"
A2A = "a2a"
A2A_REL = "a2a_rel"
A2A_SAME = "a2a_same"
A2A_OTHER = "a2a_other"


def _exchange(items, name):
    arrs, kinds = [a for a, _ in items], [k for _, k in items]
    n = len(arrs)

    def body(*refs):
        _xchg_start(refs[:n], refs[n:2 * n], refs[2 * n:], kinds)
        _xchg_relay(refs[:n], refs[n:2 * n], refs[2 * n:], kinds)
        _xchg_finish(refs[:n], refs[n:2 * n], refs[2 * n:], kinds)

    any_spec = pl.BlockSpec(memory_space=pl.ANY)
    outs = _pcall(
        body, name=name, out_shape=_xchg_shapes(arrs, kinds), in_specs=[any_spec] * n, out_specs=[any_spec] * n,
        scratch_shapes=_xchg_sems(n),
    )(*arrs)
    return list(outs)


def _edges1(n0):
    return lambda: (pl.program_id(0) == 0, pl.program_id(0) == (7 * n0) // 8, pl.program_id(0) == n0 - 1)


def _edges2(n0, n1):
    def edges():
        step = pl.program_id(0) * n1 + pl.program_id(1)
        return step == 0, step == (7 * n0 * n1) // 8, step == n0 * n1 - 1
    return edges


def _xchg_shapes(arrs, kinds):
    return tuple(jax.ShapeDtypeStruct(((NDEV,) if k in (GATHER, GATHER_REL) else ()) + tuple(a.shape), a.dtype)
                 for a, k in zip(arrs, kinds))


def _xchg_sems(n):
    return [pltpu.SemaphoreType.DMA((n * (NDEV - 1),)), pltpu.SemaphoreType.DMA((n * (NDEV - 1),)),
            pltpu.SemaphoreType.DMA((n,))]


def _xchg_plan(in_refs, out_refs, sems, kinds):
    send_sems, recv_sems, loc_sems = sems
    x, y, c = (lax.axis_index(a) for a in AXES)
    my_abs, my_chip = 4 * x + 2 * y + c, 2 * x + y

    def peer(p):
        px = 1 - x if p & 4 else x
        py = 1 - y if p & 2 else y
        pc = 1 - c if p & 1 else c
        return (px, py, pc), 4 * px + 2 * py + pc, 2 * px + py

    starts, relays, recvs = [], [], []
    for k, kind in enumerate(kinds):
        src, dst = in_refs[k], out_refs[k]

        def remote(src_ref, row, p, pair, dst=dst, k=k):
            dev, sem = peer(p)[0], k * (NDEV - 1) + pair - 1
            return lambda: pltpu.make_async_remote_copy(
                src_ref=src_ref, dst_ref=dst.at[row], send_sem=send_sems.at[sem], recv_sem=recv_sems.at[sem],
                device_id=dev, device_id_type=pl.DeviceIdType.MESH)

        def local(src_ref, row, dst=dst, k=k):
            return lambda: pltpu.make_async_copy(src_ref, dst.at[row], loc_sems.at[k])

        if kind == GATHER:
            for p in range(1, NDEV):
                starts.append((remote(src, my_abs, p, p), False))
                recvs.append(remote(src, peer(p)[1], p, p))
            starts.append((local(src, my_abs), True))
        elif kind == GATHER_REL:
            for p in (1, 4, 2, 6):
                starts.append((remote(src, 4 * (p & 1) + my_chip, p, p), False))
            for q in (4, 2, 6):
                pchip = peer(q)[2]
                relays.append((remote(src, pchip, q, q), remote(dst.at[pchip], 4 + pchip, 1, q | 1)))
                recvs.append(remote(src, 4 + pchip, 1, q | 1))
            recvs.append(remote(src, 4 + my_chip, 1, 1))
            starts.append((local(src, my_chip), True))
        elif kind in (A2A, A2A_REL):
            for p in range(1, NDEV):
                _, pabs, pchip = peer(p)
                theirs, mine = (pabs, my_abs) if kind == A2A else (4 * (p & 1) + pchip, 4 * (p & 1) + my_chip)
                starts.append((remote(src.at[theirs], mine, p, p), False))
                recvs.append(remote(src.at[theirs], theirs, p, p))
            own = my_abs if kind == A2A else my_chip
            starts.append((local(src.at[own], own), True))
        else:
            for p in ((4, 2, 6) if kind == A2A_SAME else (1, 5, 3, 7)):
                pchip = peer(p)[2]
                starts.append((remote(src.at[pchip], my_chip, p, p), False))
                recvs.append(remote(src.at[pchip], pchip, p, p))
            if kind == A2A_SAME:
                starts.append((local(src.at[my_chip], my_chip), True))
    return starts, relays, recvs


def _xchg_start(in_refs, out_refs, sems, kinds):
    for make, _ in _xchg_plan(in_refs, out_refs, sems, kinds)[0]:
        make().start()


def _xchg_relay(in_refs, out_refs, sems, kinds):
    for arrival, forward in _xchg_plan(in_refs, out_refs, sems, kinds)[1]:
        arrival().wait_recv()
        forward().start()


def _xchg_finish(in_refs, out_refs, sems, kinds):
    starts, relays, recvs = _xchg_plan(in_refs, out_refs, sems, kinds)
    for make in recvs:
        make().wait_recv()
    for make, is_local in starts:
        if is_local:
            make().wait()
        else:
            make().wait_send()
    for _, forward in relays:
        forward().wait_send()


def _front(w13, w2, ng, c, c_ctx, w_blk, b_blk):
    d, wa = w_blk.shape
    big, big_kinds = [w13, w2, ng], [GATHER_REL, GATHER_REL, GATHER]

    def body(w13_ref, w2_ref, ng_ref, c_ref, cctx_ref, w_ref, b_ref, g13_ref, g2_ref, gng_ref, gc_ref, msl_ref, gm_ref,
             c16, mod, loc_sems, *sems):
        big_io = ((w13_ref, w2_ref, ng_ref), (g13_ref, g2_ref, gng_ref), sems[0:3], big_kinds)
        c_io = ((c_ref,), (gc_ref,), sems[3:6], [GATHER])
        mod_io = ((msl_ref,), (gm_ref,), sems[6:9], [GATHER])
        _xchg_start(*big_io)
        _xchg_start(*c_io)
        _xchg_finish(*c_io)
        rows = [pltpu.make_async_copy(gc_ref.at[k], c16.at[pl.ds(k, 1), :], loc_sems.at[k]) for k in range(NDEV)]
        for cp in rows:
            cp.start()
        c16[NDEV:NDEV + 1, :] = cctx_ref[...]
        c16[NDEV + 1:, :] = jnp.zeros((16 - NDEV - 1, d), F32)
        for cp in rows:
            cp.wait()
        cc = c16[...]
        mod[...] = _dot((cc * _sigmoid(cc)).astype(BF16), w_ref[...].astype(BF16)) + b_ref[...]
        out = pltpu.make_async_copy(mod, msl_ref, loc_sems.at[NDEV])
        out.start()
        out.wait()
        _xchg_start(*mod_io)
        _xchg_relay(*big_io)
        _xchg_finish(*big_io)
        _xchg_finish(*mod_io)

    any_spec = pl.BlockSpec(memory_space=pl.ANY)
    vmem = pl.BlockSpec(memory_space=pltpu.VMEM)
    outs = _pcall(
        body, name="exchange_weights",
        out_shape=_xchg_shapes(big + [c], big_kinds + [GATHER]) + (jax.ShapeDtypeStruct((16, wa), F32),
                                                                   jax.ShapeDtypeStruct((NDEV, 16, wa), F32)),
        in_specs=[any_spec] * 4 + [vmem] * 3, out_specs=[any_spec] * 6,
        scratch_shapes=[pltpu.VMEM((16, d), F32), pltpu.VMEM((16, wa), F32), pltpu.SemaphoreType.DMA((NDEV + 1,))]
        + _xchg_sems(3) + _xchg_sems(1) + _xchg_sems(1),
        compiler_params=_params(None),
    )(w13, w2, ng, c, c_ctx, w_blk, b_blk)
    return outs[0], outs[1], outs[2], outs[3], outs[5]


def _ada_bwd(c16, w_blk, dl, dc):
    d, w = w_blk.shape

    def body(c_ref, w_ref, dl_ref, dc_ref, gw_ref, gc_ref):
        cc = c_ref[...]
        sg = _sigmoid(cc)
        s = (cc * sg).astype(BF16)
        dctot = _colsum(dc_ref[...])
        dm = jnp.concatenate([dl_ref[...], jnp.broadcast_to(dctot, (8, w))], axis=0)
        rows = lax.broadcasted_iota(jnp.int32, (16, w), 0)
        dm = jnp.where(rows <= 8, dm, 0.0).astype(BF16)
        gw_ref[...] = _dot_tn(s, dm)
        t = _dot_nt(jnp.broadcast_to(dctot, (8, w)).astype(BF16), w_ref[...].astype(BF16))[0:1]
        c8, s8 = cc[8:9], sg[8:9]
        gc_ref[...] = t * (s8 * (1.0 + c8 * (1.0 - s8)))

    return _pcall(body, name="ada_bwd",
                  out_shape=(jax.ShapeDtypeStruct((d, w), F32), jax.ShapeDtypeStruct((1, d), F32)),
                  compiler_params=_params(None))(c16, w_blk, dl, dc)


def _norm_fwd(x, g, shift, scale):
    rstd = lax.rsqrt(jnp.mean(x * x, axis=-1, keepdims=True) + EPS)
    xhat = x * rstd
    xn = xhat * g
    return rstd, xhat, xn, xn * (1.0 + scale) + shift


def _norm_bwd(dh, rstd, xhat, xn, g, scale):
    dxn = dh * (1.0 + scale)
    dxhat = dxn * g
    dx = rstd * (dxhat - xhat * jnp.mean(dxhat * xhat, axis=-1, keepdims=True))
    return dx, _colsum(dh), _colsum(dh * xn), _colsum(dxn * xhat)


def _mod_spec(d, has_ctx):
    if has_ctx:
        return pl.BlockSpec((1, 9, d), lambda *ids: (jnp.minimum(ids[-1], 1), 0, 0))
    return pl.BlockSpec((1, 9, d), lambda *ids: (1, 0, 0))


def _row_specs(nx, tm, d):
    if nx == 1:
        return [pl.BlockSpec((tm, d), lambda i: (i, 0))]
    return [pl.BlockSpec((tm, d), lambda i: (0, 0)), pl.BlockSpec((tm, d), lambda i: (jnp.maximum(i - 1, 0), 0))]


def _rows(refs):
    if len(refs) == 1:
        return refs[0][...]
    return jnp.where(pl.program_id(0) == 0, refs[0][...], refs[1][...])


def _tile(has_ctx):
    return TM if has_ctx else TML


def _tile_of(nt, has_ctx):
    if has_ctx:
        return lambda i: jnp.where(i == 0, nt - 1, i - 1)
    return lambda i: i


def _dmod_spec(nrow, d, has_ctx):
    if has_ctx:
        return pl.BlockSpec((1, nrow, d), lambda i: (jnp.minimum(i, 1), 0, 0))
    return pl.BlockSpec((1, nrow, d), lambda i: (0, 0, 0))


def _ffn_fwd(x, modtab, g, w13, w2, k0, has_ctx, name, xchg=None, head=None):
    t, d = sum(a.shape[0] for a in x), x[0].shape[1]
    nb, w, _ = w13.shape
    fp = w // 2
    tm = _tile(has_ctx)
    nx = len(x)

    nhead = 0 if head is None else 2

    def body(*refs):
        mod_ref, g_ref, w13_ref, w2_ref = refs[nx:nx + 4]
        xo_ref, h_ref, ab_ref, o_ref = refs[nx + 4 + nhead:nx + 8 + nhead]
        xx = _rows(refs[:nx])
        shift, scale, gate = mod_ref[0, k0:k0 + 1, :], mod_ref[0, k0 + 1:k0 + 2, :], mod_ref[0, k0 + 2:k0 + 3, :]
        _, _, _, h = _norm_fwd(xx, g_ref[...], shift, scale)
        hb = h.astype(BF16)
        h_ref[...] = hb
        acc = jnp.zeros((tm, d), F32)
        for p in range(nb // 2):
            zs = []
            for q in range(2):
                blk = 2 * p + q
                ab = _dot_nt(hb, w13_ref[blk])
                ab_ref[:, blk * w:(blk + 1) * w] = ab.astype(BF16)
                a, b = ab[:, :fp], ab[:, fp:]
                zs.append((a * _sigmoid(a) * b).astype(BF16))
            acc = acc + _dot(jnp.concatenate(zs, axis=1), w2_ref[p])
        o_ref[...] = acc.astype(BF16)
        xo = xx + (0.5 * gate) * acc
        if head is None:
            xo_ref[...] = xo
        else:
            tgt_ref, fg_ref = refs[nx + 4:nx + 6]
            loss_ref, dfg_ref = refs[nx + 8 + nhead:]

            @pl.when(pl.program_id(0) == 0)
            def _():
                loss_ref[...] = jnp.zeros_like(loss_ref)
                dfg_ref[...] = jnp.zeros_like(dfg_ref)

            xo_ref[...] = _loss_head(xo, tgt_ref[...], fg_ref[...], loss_ref, dfg_ref)

    tile = _tile_of(t // tm, has_ctx)
    row = lambda i: (tile(i), 0)
    head_in = [] if head is None else [pl.BlockSpec((tm, d), row), _const((1, d))]
    head_shape = () if head is None else (jax.ShapeDtypeStruct((1, LANE), F32), jax.ShapeDtypeStruct((1, d), F32))
    head_out = () if head is None else (_const((1, LANE)), _const((1, d)))
    return _pcall(
        body, name=name, grid=(t // tm,), xchg=xchg, edges=_edges1(t // tm),
        out_shape=(jax.ShapeDtypeStruct((t, d), F32), jax.ShapeDtypeStruct((t, d), BF16),
                   jax.ShapeDtypeStruct((t, nb * w), BF16), jax.ShapeDtypeStruct((t, d), BF16)) + head_shape,
        in_specs=_row_specs(nx, tm, d) + [_mod_spec(d, has_ctx), _const((1, d)),
                                          _resident(w13.shape), _resident(w2.shape)] + head_in,
        out_specs=(pl.BlockSpec((tm, d), row), pl.BlockSpec((tm, d), row), pl.BlockSpec((tm, nb * w), row),
                   pl.BlockSpec((tm, d), row)) + head_out,
        compiler_params=_params(("arbitrary",)),
    )(*x, modtab, g, w13, w2, *(head or ()))


def _ffn_bwd(dy, modtab, h, ab, w13, w2, k0, has_ctx, name, pair0=0, npair=None, xchg=None):
    t, d = dy.shape
    _, w, _ = w13.shape
    fp = w // 2
    npair = w13.shape[0] // 2 if npair is None else npair
    nb = 2 * npair
    tm = _tile(has_ctx)
    nt = t // tm

    def body(dy_ref, mod_ref, h_ref, ab_ref, w13_ref, w2_ref, dh_ref, d13_ref, d2_ref, acc13, acc2):
        i = pl.program_id(1)

        @pl.when(i == 0)
        def _():
            acc13[...] = jnp.zeros_like(acc13)
            acc2[...] = jnp.zeros_like(acc2)

        gate = mod_ref[0, k0 + 2:k0 + 3, :]
        do = (dy_ref[...] * (0.5 * gate)).astype(BF16)
        dz = _dot_nt(do, w2_ref[0])
        hb = h_ref[...]
        dh = jnp.zeros((tm, d), F32)
        zs = []
        for q in range(2):
            ab = ab_ref[:, q * w:(q + 1) * w].astype(F32)
            a, b = ab[:, :fp], ab[:, fp:]
            sg = _sigmoid(a)
            sa = a * sg
            dzq = dz[:, q * fp:(q + 1) * fp]
            da = dzq * b * (sg * (1.0 + a * (1.0 - sg)))
            db = dzq * sa
            dab = jnp.concatenate([da, db], axis=1).astype(BF16)
            dh = dh + _dot(dab, w13_ref[q])
            acc13[q] += _dot_tn(dab, hb)
            zs.append((sa * b).astype(BF16))
        acc2[...] += _dot_tn(jnp.concatenate(zs, axis=1), do)
        dh_ref[0] = dh.astype(BF16)

        @pl.when(i == nt - 1)
        def _():
            d13_ref[...] = acc13[...].astype(BF16)
            d2_ref[0] = acc2[...].astype(BF16)

    mod_spec = _mod_spec(d, has_ctx)
    tile = _tile_of(nt, has_ctx)
    return _pcall(
        body, name=name, grid=(npair, nt), xchg=xchg, edges=_edges2(npair, nt),
        out_shape=(jax.ShapeDtypeStruct((npair, t, d), BF16), jax.ShapeDtypeStruct((nb, w, d), BF16),
                   jax.ShapeDtypeStruct((npair, 2 * fp, d), BF16)),
        in_specs=[pl.BlockSpec((tm, d), lambda p, i: (tile(i), 0)), mod_spec,
                  pl.BlockSpec((tm, d), lambda p, i: (tile(i), 0)),
                  pl.BlockSpec((tm, 2 * w), lambda p, i: (tile(i), pair0 + p)),
                  pl.BlockSpec((2, w, d), lambda p, i: (pair0 + p, 0, 0)),
                  pl.BlockSpec((1, 2 * fp, d), lambda p, i: (pair0 + p, 0, 0))],
        out_specs=(pl.BlockSpec((1, tm, d), lambda p, i: (p, tile(i), 0)),
                   pl.BlockSpec((2, w, d), lambda p, i: (p, 0, 0)),
                   pl.BlockSpec((1, 2 * fp, d), lambda p, i: (p, 0, 0))),
        scratch_shapes=[pltpu.VMEM((2, w, d), F32), pltpu.VMEM((2 * fp, d), F32)],
        compiler_params=_params(("arbitrary", "arbitrary")),
    )(dy, modtab, h, ab, w13, w2)


def _ffn_bwd_norm(dy, dhps, x, o, modtab, g, k0, has_ctx, name, xchg=None):
    t, d = dy.shape
    ngrp = 2 if has_ctx else 1
    tm = _tile(has_ctx)
    ndh, nx = len(dhps), len(x)
    lat0 = ngrp - 1

    def body(dy_ref, *rest):
        dhp_refs, x_refs = rest[:ndh], rest[ndh:ndh + nx]
        o_ref, mod_ref, g_ref, dx_ref, dmod_ref, dg_ref = rest[ndh + nx:]
        i = pl.program_id(0)

        @pl.when(i == 0)
        def _():
            dg_ref[...] = jnp.zeros_like(dg_ref)

        @pl.when((i == 0) | (i == ngrp - 1))
        def _():
            dmod_ref[...] = jnp.zeros_like(dmod_ref)

        dh = None
        for ref in dhp_refs:
            for p in range(ref.shape[0]):
                dh = ref[p].astype(F32) if dh is None else dh + ref[p].astype(F32)
        scale = mod_ref[0, k0 + 1:k0 + 2, :]
        gg = g_ref[...]
        rstd, xhat, xn, _ = _norm_fwd(_rows(x_refs), gg, 0.0, scale)
        dxn, dshift, dscale, dg = _norm_bwd(dh, rstd, xhat, xn, gg, scale)
        dyv = dy_ref[...]
        dx_ref[...] = dyv + dxn
        dmod_ref[0, 0:1, :] += dshift
        dmod_ref[0, 1:2, :] += dscale
        dmod_ref[0, 2:3, :] += _colsum(0.5 * dyv * o_ref[...].astype(F32))
        dg_ref[...] += dg

    tile = _tile_of(t // tm, has_ctx)
    row = lambda i: (tile(i), 0)
    return _pcall(
        body, name=name, grid=(t // tm,), xchg=xchg, edges=_edges1(t // tm),
        out_shape=(jax.ShapeDtypeStruct((t - lat0 * tm, d), F32), jax.ShapeDtypeStruct((ngrp, 3, d), F32),
                   jax.ShapeDtypeStruct((1, d), F32)),
        in_specs=[pl.BlockSpec((tm, d), row)]
        + [pl.BlockSpec((a.shape[0], tm, d), lambda i: (0, tile(i), 0)) for a in dhps]
        + _row_specs(nx, tm, d) + [pl.BlockSpec((tm, d), row), _mod_spec(d, has_ctx), _const((1, d))],
        out_specs=(pl.BlockSpec((tm, d), lambda i: (jnp.maximum(i - lat0, 0), 0)), _dmod_spec(3, d, has_ctx),
                   _const((1, d))),
        compiler_params=_params(("arbitrary",)),
    )(dy, *dhps, *x, o, modtab, g)


def _mix_in_fwd(x1, modtab, g, w_in):
    t, d = x1.shape
    hm = w_in.shape[1] // 2

    def body(x_ref, mod_ref, g_ref, w_ref, h_ref, up_ref, us_ref):
        _, _, _, h = _norm_fwd(x_ref[...], g_ref[...], mod_ref[0, 3:4, :], mod_ref[0, 4:5, :])
        hb = h.astype(BF16)
        h_ref[...] = hb
        u = _dot(hb, w_ref[...])
        up_ref[...] = u[:, :hm]
        us_ref[...] = u[:, hm:]

    tile = _tile_of(t // TM, True)
    row = lambda i: (tile(i), 0)
    return _pcall(
        body, name="mix_in_fwd", grid=(t // TM,),
        out_shape=(jax.ShapeDtypeStruct((t, d), BF16), jax.ShapeDtypeStruct((t, hm), F32),
                   jax.ShapeDtypeStruct((t, hm), F32)),
        in_specs=[pl.BlockSpec((TM, d), row), _mod_spec(d, True), _const((1, d)), _resident(w_in.shape)],
        out_specs=(pl.BlockSpec((TM, d), row), pl.BlockSpec((TM, hm), row), pl.BlockSpec((TM, hm), row)),
        compiler_params=_params(("arbitrary",)),
    )(x1, modtab, g, w_in)


def _mix_in_bwd(du_pool, du_f, du_b, dx2, h1, x1, modtab, g, w_in, lc):
    t, d = x1.shape
    m = w_in.shape[1]
    hm = m // 2
    nt = t // TM

    def body(dup_ref, duf_ref, dub_ref, dx2_ref, h_ref, x_ref, mod_ref, g_ref, w_ref, dx_ref, dw_ref, dmod_ref, dg_ref, acc):
        i = pl.program_id(0)

        @pl.when(i == 0)
        def _():
            acc[...] = jnp.zeros_like(acc)
            dg_ref[...] = jnp.zeros_like(dg_ref)

        @pl.when(i <= 1)
        def _():
            dmod_ref[...] = jnp.zeros_like(dmod_ref)

        lat = (i > 0).astype(F32)
        valid = (i > 0) | (lax.broadcasted_iota(jnp.int32, (TM, 1), 0) < lc)
        du_s5 = jnp.where(valid, duf_ref[...] + dub_ref[...], 0.0)
        du = jnp.concatenate([dup_ref[...] * lat, du_s5], axis=1).astype(BF16)
        dh = _dot_nt(du, w_ref[...])
        acc[...] += _dot_tn(h_ref[...], du)
        scale = mod_ref[0, 4:5, :]
        gg = g_ref[...]
        rstd, xhat, xn, _ = _norm_fwd(x_ref[...], gg, 0.0, scale)
        dxn, dshift, dscale, dg = _norm_bwd(dh, rstd, xhat, xn, gg, scale)
        dx_ref[...] = dx2_ref[...] * lat + dxn
        dmod_ref[0, 0:1, :] += dshift
        dmod_ref[0, 1:2, :] += dscale
        dg_ref[...] += dg

        @pl.when(i == nt - 1)
        def _():
            dw_ref[...] = acc[...].astype(BF16)

    tile = _tile_of(nt, True)
    row = lambda i: (tile(i), 0)
    lrow = lambda i: (jnp.maximum(i - 1, 0), 0)
    return _pcall(
        body, name="mix_in_bwd", grid=(nt,),
        out_shape=(jax.ShapeDtypeStruct((t, d), F32), jax.ShapeDtypeStruct((d, m), BF16),
                   jax.ShapeDtypeStruct((2, 2, d), F32), jax.ShapeDtypeStruct((1, d), F32)),
        in_specs=[pl.BlockSpec((TM, hm), lrow), pl.BlockSpec((TM, hm), row), pl.BlockSpec((TM, hm), row),
                  pl.BlockSpec((TM, d), lrow), pl.BlockSpec((TM, d), row), pl.BlockSpec((TM, d), row),
                  _mod_spec(d, True), _const((1, d)), _resident(w_in.shape)],
        out_specs=(pl.BlockSpec((TM, d), row), _const((d, m)), _dmod_spec(2, d, True), _const((1, d))),
        scratch_shapes=[pltpu.VMEM((d, m), F32)],
        compiler_params=_params(("arbitrary",)),
    )(du_pool, du_f, du_b, dx2, h1, x1, modtab, g, w_in)


def _pool(v, rows, transpose, name):
    l, n = rows * GRID_W, GRID_W * LANE
    ngrp = v.shape[1] // LANE
    nchunk = 4
    cw = n // nchunk

    def rowsum(val, lo, hi):
        ri = lax.broadcasted_iota(jnp.int32, (rows, rows), 0)
        ci = lax.broadcasted_iota(jnp.int32, (rows, rows), 1)
        sel = (((ci - ri) >= -lo) & ((ci - ri) <= hi)).astype(BF16)
        v1 = val.astype(BF16)
        r1 = val - v1.astype(F32)
        v2 = r1.astype(BF16)
        v3 = (r1 - v2.astype(F32)).astype(BF16)
        return _dot(sel, v1) + _dot(sel, v2) + _dot(sel, v3)

    def one(v_ref, o_ref, g_scr, r_scr, a_scr, win):
        for c in range(GRID_W):
            g_scr[:, c * LANE:(c + 1) * LANE] = v_ref[pl.ds(c, rows, stride=GRID_W), :]
        lo = win // 2
        hi = win - 1 - lo
        rlo, rhi = (hi, lo) if transpose else (lo, hi)
        ridx = lax.broadcasted_iota(jnp.int32, (rows, 1), 0)
        cnt_r = (jnp.minimum(ridx + hi + 1, rows) - jnp.maximum(ridx - lo, 0)).astype(F32)
        cidx = lax.broadcasted_iota(jnp.int32, (1, n), 1) // LANE
        cnt_c = (jnp.minimum(cidx + hi + 1, GRID_W) - jnp.maximum(cidx - lo, 0)).astype(F32)
        if not transpose:
            for k in range(nchunk):
                sl = slice(k * cw, (k + 1) * cw)
                r_scr[:, sl] = rowsum(g_scr[:, sl], rlo, rhi) / cnt_r
        else:
            r_scr[...] = g_scr[...] / cnt_c
        a_scr[...] = r_scr[...]
        for j in range(-rlo, rhi + 1):
            if j == 0:
                continue
            c0, c1 = max(0, -j), min(GRID_W, GRID_W - j)
            a_scr[:, c0 * LANE:c1 * LANE] += r_scr[:, (c0 + j) * LANE:(c1 + j) * LANE]
        if not transpose:
            r_scr[...] = a_scr[...] / cnt_c - g_scr[...]
        else:
            for k in range(nchunk):
                sl = slice(k * cw, (k + 1) * cw)
                r_scr[:, sl] = rowsum(a_scr[:, sl] / cnt_r, rlo, rhi) - g_scr[:, sl]
        for c in range(GRID_W):
            o_ref[pl.ds(c, rows, stride=GRID_W), :] = r_scr[:, c * LANE:(c + 1) * LANE]

    def body(v_ref, o_ref, g_scr, r_scr, a_scr):
        grp = pl.program_id(0)
        for k, win in enumerate(POOL_WINDOWS):
            @pl.when(grp == k)
            def _(win=win):
                one(v_ref, o_ref, g_scr, r_scr, a_scr, win)

    spec = pl.BlockSpec((l, LANE), lambda k: (0, k))
    return _pcall(
        body, name=name, grid=(ngrp,), out_shape=jax.ShapeDtypeStruct((l, ngrp * LANE), F32),
        in_specs=[spec], out_specs=spec,
        scratch_shapes=[pltpu.VMEM((rows, n), F32), pltpu.VMEM((rows, n), F32), pltpu.VMEM((rows, n), F32)],
        compiler_params=_params(("arbitrary",)),
    )(v)


def _cmul(ar, ai, br, bi):
    return ar * br - ai * bi, ar * bi + ai * br


def _s5_prep(a_re, a_im, log_dt, c_re, c_im, kvec):
    q, nc, p = c_re.shape

    def body(ar_ref, ai_ref, ldt_ref, cr_ref, ci_ref, k_ref, cpr_ref, cpi_ref, tr_ref, ti_ref):
        lr, li = ar_ref[...], ai_ref[...]
        dt = jnp.exp(ldt_ref[...])
        kk = k_ref[...]
        mag = jnp.exp(kk * (lr * dt))
        ph = kk * (li * dt)
        tr_ref[...] = mag * jnp.cos(ph)
        ti_ref[...] = mag * jnp.sin(ph)
        m1 = jnp.exp(lr * dt)
        abr, abi = m1 * jnp.cos(li * dt), m1 * jnp.sin(li * dt)
        den = lr * lr + li * li
        xr, xi = abr - 1.0, abi
        cfr, cfi = (xr * lr + xi * li) / den, (xi * lr - xr * li) / den
        pr, pi_ = _cmul(cr_ref[...], ci_ref[...], cfr, cfi)
        cpr_ref[...] = pr
        cpi_ref[...] = pi_

    return _pcall(
        body, name="s5_prep",
        out_shape=(jax.ShapeDtypeStruct((q, nc, p), F32), jax.ShapeDtypeStruct((q, nc, p), F32),
                   jax.ShapeDtypeStruct((q, NTAB, p), F32), jax.ShapeDtypeStruct((q, NTAB, p), F32)),
        compiler_params=_params(None),
    )(a_re, a_im, log_dt, c_re, c_im, kvec)


def _s5_param_bwd(a_re, a_im, log_dt, c_re, c_im, dab_r, dab_i, dcp_r, dcp_i):
    q, nc, p = c_re.shape

    def body(ar_ref, ai_ref, ldt_ref, cr_ref, ci_ref, dar_ref, dai_ref, dcr_ref, dci_ref,
             gar_ref, gai_ref, gdt_ref, gcr_ref, gci_ref):
        lr, li = ar_ref[...], ai_ref[...]
        dt = jnp.exp(ldt_ref[...])
        m1 = jnp.exp(lr * dt)
        abr, abi = m1 * jnp.cos(li * dt), m1 * jnp.sin(li * dt)
        den = lr * lr + li * li
        xr, xi = abr - 1.0, abi
        cfr, cfi = (xr * lr + xi * li) / den, (xi * lr - xr * li) / den
        cr, ci = cr_ref[...], ci_ref[...]
        dcr, dci = dcr_ref[...], dci_ref[...]
        gcr, gci = _cmul(dcr, dci, cfr, -cfi)
        gcr_ref[...] = gcr
        gci_ref[...] = gci
        t_r, t_i = _cmul(cr, -ci, dcr, dci)
        dcf_r = jnp.sum(t_r, axis=1, keepdims=True)
        dcf_i = jnp.sum(t_i, axis=1, keepdims=True)
        ilr, ili = lr / den, -li / den
        u_r, u_i = _cmul(dcf_r, dcf_i, ilr, -ili)
        dab_r_, dab_i_ = dar_ref[...] + u_r, dai_ref[...] + u_i
        v_r, v_i = _cmul(dab_r_, dab_i_, abr, -abi)
        cl_r, cl_i = _cmul(cfr, cfi, ilr, ili)
        w_r, w_i = _cmul(dcf_r, dcf_i, cl_r, -cl_i)
        gar_ref[...] = v_r * dt - w_r
        gai_ref[...] = v_i * dt - w_i
        la_r, la_i = _cmul(lr, li, abr, abi)
        ddt = la_r * dab_r_ + la_i * dab_i_
        gdt_ref[...] = dt * jnp.sum(ddt, axis=2, keepdims=True)

    return _pcall(
        body, name="s5_param_bwd",
        out_shape=(jax.ShapeDtypeStruct((q, 1, p), F32), jax.ShapeDtypeStruct((q, 1, p), F32),
                   jax.ShapeDtypeStruct((q, 1, p), F32), jax.ShapeDtypeStruct((q, nc, p), F32),
                   jax.ShapeDtypeStruct((q, nc, p), F32)),
        compiler_params=_params(None),
    )(a_re, a_im, log_dt, c_re, c_im, dab_r, dab_i, dcp_r, dcp_i)


def _bcast8(row, c):
    return jnp.broadcast_to(row, (8, c))


def _slab(j):
    return j * 8 if isinstance(j, int) else pl.multiple_of(j * 8, 8)


def _scan_head(xr, xi, tre_ref, tim_ref, car_r, car_i, desc, conj, cs):
    sg = -1.0 if conj else 1.0
    ar = _bcast8(tre_ref[0:1, :], cs)
    ai = sg * _bcast8(tim_ref[0:1, :], cs)

    def p1(jj, carry):
        hr, hi = carry
        off = _slab(SEG - 1 - jj if desc else jj)
        pr, pi_ = _cmul(ar, ai, hr, hi)
        nr = pr + xr[pl.ds(off, 8), :]
        ni = pi_ + xi[pl.ds(off, 8), :]
        xr[pl.ds(off, 8), :] = nr
        xi[pl.ds(off, 8), :] = ni
        return nr, ni

    zero = jnp.zeros((8, cs), F32)
    ir, ii = lax.fori_loop(0, SEG, p1, (zero, zero), unroll=S5_UNROLL)
    cin_r, cin_i = car_r[...], car_i[...]
    rows = lax.broadcasted_iota(jnp.int32, (8, cs), 0)
    for s, krow in ((1, SEG), (2, SEG + 1), (4, SEG + 3)):
        keep, sh = (rows < 8 - s, 8 - s) if desc else (rows >= s, s)
        sr = jnp.where(keep, pltpu.roll(ir, sh, 0), 0.0)
        si = jnp.where(keep, pltpu.roll(ii, sh, 0), 0.0)
        pr, pi_ = _cmul(tre_ref[krow:krow + 1, :], sg * tim_ref[krow:krow + 1, :], sr, si)
        ir, ii = ir + pr, ii + pi_
    q0 = SEG + 8 if desc else SEG
    pr, pi_ = _cmul(tre_ref[q0:q0 + 8, :], sg * tim_ref[q0:q0 + 8, :], cin_r, cin_i)
    fr, fi = ir + pr, ii + pi_
    keep, sh, edge = (rows < 7, 7, 0) if desc else (rows >= 1, 1, 7)
    cs_r = jnp.where(keep, pltpu.roll(fr, sh, 0), cin_r)
    cs_i = jnp.where(keep, pltpu.roll(fi, sh, 0), cin_i)
    car_r[...] = _bcast8(fr[edge:edge + 1, :], cs)
    car_i[...] = _bcast8(fi[edge:edge + 1, :], cs)
    return cs_r, cs_i, cin_r, cin_i


def _pow_row(tre_ref, tim_ref, j, desc, conj, cs):
    k = SEG - 1 - j if desc else j
    sg = -1.0 if conj else 1.0
    return _bcast8(tre_ref[pl.ds(k, 1), :], cs), sg * _bcast8(tim_ref[pl.ds(k, 1), :], cs)


def _to_segments(val, dst):
    for r in range(8):
        dst[pl.ds(r, SEG, stride=8), :] = val[r * SEG:(r + 1) * SEG, :]


def _from_segments(src):
    return jnp.concatenate([src[pl.ds(r, SEG, stride=8), :] for r in range(8)], axis=0)


def _s5_fwd(up, bre, bim, cmr, cmi, tre, tim, dskip, order, nproc, desc, with_skip, name):
    t, cu = up.shape
    nsb = bre.shape[0]
    cb = cu // nsb
    cs = bre.shape[2]
    nch = t // TC

    def body(u_ref, bre_ref, bim_ref, cmr_ref, cmi_ref, tre_ref, tim_ref, dsk_ref,
             y_ref, hr_ref, hi_ref, cinr_ref, cini_ref, car_r, car_i, seg_scr):
        i = pl.program_id(0)

        @pl.when(i == 0)
        def _():
            car_r[...] = jnp.zeros_like(car_r)
            car_i[...] = jnp.zeros_like(car_i)

        for sb in range(nsb):
            col, ch = pl.ds(sb * cs, cs), slice(sb * cb, (sb + 1) * cb)
            hr_v, hi_v = hr_ref.at[:, col], hi_ref.at[:, col]
            tre_v, tim_v = tre_ref.at[:, col], tim_ref.at[:, col]
            useg, yseg = seg_scr.at[0, sb], seg_scr.at[1, sb]
            _to_segments(u_ref[:, ch], useg)
            u = useg[...]
            ub = u.astype(BF16)
            hr_v[...] = _dot(ub, bre_ref[sb])
            hi_v[...] = _dot(ub, bim_ref[sb])
            cs_r, cs_i, cin_r, cin_i = _scan_head(hr_v, hi_v, tre_v, tim_v, car_r.at[sb], car_i.at[sb], desc, False, cs)
            cinr_ref[:, col] = cin_r
            cini_ref[:, col] = cin_i

            def p2(j, _, hr_v=hr_v, hi_v=hi_v, tre_v=tre_v, tim_v=tim_v, cs_r=cs_r, cs_i=cs_i):
                off = _slab(j)
                pw_r, pw_i = _pow_row(tre_v, tim_v, j, desc, False, cs)
                pr, pi_ = _cmul(pw_r, pw_i, cs_r, cs_i)
                hr_v[pl.ds(off, 8), :] = hr_v[pl.ds(off, 8), :] + pr
                hi_v[pl.ds(off, 8), :] = hi_v[pl.ds(off, 8), :] + pi_
                return 0

            lax.fori_loop(0, SEG, p2, 0, unroll=S5_UNROLL)
            y = _dot(hr_v[...].astype(BF16), cmr_ref[sb]) - _dot(hi_v[...].astype(BF16), cmi_ref[sb])
            if with_skip:
                y = y + dsk_ref[:, ch] * u
            yseg[...] = y
            y_ref[:, ch] = _from_segments(yseg)

    blk = lambda i: (order(i), 0)
    return _pcall(
        body, name=name, grid=(nproc,),
        out_shape=(jax.ShapeDtypeStruct((t, cu), F32), jax.ShapeDtypeStruct((t, nsb * cs), F32),
                   jax.ShapeDtypeStruct((t, nsb * cs), F32), jax.ShapeDtypeStruct((nch * 8, nsb * cs), F32),
                   jax.ShapeDtypeStruct((nch * 8, nsb * cs), F32)),
        in_specs=[pl.BlockSpec((TC, cu), blk), _const(bre.shape), _const(bim.shape), _const(cmr.shape),
                  _const(cmi.shape), _const(tre.shape), _const(tim.shape), _const(dskip.shape)],
        out_specs=(pl.BlockSpec((TC, cu), blk), pl.BlockSpec((TC, nsb * cs), blk), pl.BlockSpec((TC, nsb * cs), blk),
                   pl.BlockSpec((8, nsb * cs), blk), pl.BlockSpec((8, nsb * cs), blk)),
        scratch_shapes=[pltpu.VMEM((nsb, 8, cs), F32), pltpu.VMEM((nsb, 8, cs), F32),
                        pltpu.VMEM((2, nsb, TC, cb), F32)],
        compiler_params=_params(("arbitrary",)),
    )(up, bre, bim, cmr, cmi, tre, tim, dskip)


def _s5_bwd(dy_lat, up, hr, hi, cinr, cini, bre, bim, ctr, cti, tre, tim, dskip, order, nproc, desc, with_skip, name,
            xchg=None):
    t, cu = up.shape
    nsb = bre.shape[0]
    cb = cu // nsb
    cs = bre.shape[2]
    lch = dy_lat.shape[0] // TC
    adesc = not desc

    def body(dy_ref, u_ref, hr_ref, hi_ref, cinr_ref, cini_ref, bre_ref, bim_ref, ctr_ref, cti_ref, tre_ref, tim_ref,
             dsk_ref, du_ref, dar_ref, dai_ref, dcr_ref, dci_ref, dbr_ref, dbi_ref, dd_ref,
             car_r, car_i, mr_all, mi_all, acc_r, acc_i, seg_scr):
        i = pl.program_id(0)
        latent = (order(i) < lch).astype(F32)

        @pl.when(i == 0)
        def _():
            for ref in (car_r, car_i, acc_r, acc_i, dcr_ref, dci_ref, dbr_ref, dbi_ref, dd_ref):
                ref[...] = jnp.zeros_like(ref)

        rows = lax.broadcasted_iota(jnp.int32, (8, cs), 0)
        for sb in range(nsb):
            col, ch = pl.ds(sb * cs, cs), slice(sb * cb, (sb + 1) * cb)
            hr_v, hi_v = hr_ref.at[:, col], hi_ref.at[:, col]
            tre_v, tim_v = tre_ref.at[:, col], tim_ref.at[:, col]
            mr, mi = mr_all.at[sb % 2], mi_all.at[sb % 2]
            useg, dyseg, duseg = seg_scr.at[0, sb], seg_scr.at[1, sb], seg_scr.at[2, sb]
            _to_segments(dy_ref[:, ch] * latent, dyseg)
            _to_segments(u_ref[:, ch], useg)
            dy = dyseg[...]
            dyb = dy.astype(BF16)
            u = useg[...]
            ub = u.astype(BF16)
            mr[...] = _dot(dyb, ctr_ref[sb])
            mi[...] = -_dot(dyb, cti_ref[sb])
            cs_r, cs_i, _, _ = _scan_head(mr, mi, tre_v, tim_v, car_r.at[sb], car_i.at[sb], adesc, True, cs)

            def fix(j, hp_r, hp_i, acc, mr=mr, mi=mi, tre_v=tre_v, tim_v=tim_v, cs_r=cs_r, cs_i=cs_i):
                off = _slab(j)
                pw_r, pw_i = _pow_row(tre_v, tim_v, j, adesc, True, cs)
                pr, pi_ = _cmul(pw_r, pw_i, cs_r, cs_i)
                m_r = mr[pl.ds(off, 8), :] + pr
                m_i = mi[pl.ds(off, 8), :] + pi_
                mr[pl.ds(off, 8), :] = m_r
                mi[pl.ds(off, 8), :] = m_i
                a_r, a_i = acc
                return a_r + hp_r * m_r + hp_i * m_i, a_i + hp_r * m_i - hp_i * m_r

            edge_j, src, keep, sh = (SEG - 1, 0, rows < 7, 7) if desc else (0, (SEG - 1) * 8, rows >= 1, 1)
            h0r = jnp.where(keep, pltpu.roll(hr_v[src:src + 8, :], sh, 0), cinr_ref[:, col])
            h0i = jnp.where(keep, pltpu.roll(hi_v[src:src + 8, :], sh, 0), cini_ref[:, col])
            acc = fix(edge_j, h0r, h0i, (acc_r[sb], acc_i[sb]))

            def p2(jj, acc, fix=fix, hr_v=hr_v, hi_v=hi_v):
                j = jj if desc else jj + 1
                offp = _slab(j + 1 if desc else j - 1)
                return fix(j, hr_v[pl.ds(offp, 8), :], hi_v[pl.ds(offp, 8), :], acc)

            a_r, a_i = lax.fori_loop(0, SEG - 1, p2, acc, unroll=S5_UNROLL)
            acc_r[sb] = a_r
            acc_i[sb] = a_i

            mrb, mib = mr[...].astype(BF16), mi[...].astype(BF16)
            du = _dot_nt(mrb, bre_ref[sb]) + _dot_nt(mib, bim_ref[sb])
            if with_skip:
                du = du + dsk_ref[:, ch] * dy
                dd_ref[:, ch] += _colsum(dy * u)
            duseg[...] = du
            du_ref[:, ch] = _from_segments(duseg)
            dbr_ref[sb] += _dot_tn(ub, mrb)
            dbi_ref[sb] += _dot_tn(ub, mib)
            dcr_ref[sb] += _dot_tn(dyb, hr_v[...].astype(BF16))
            dci_ref[sb] -= _dot_tn(dyb, hi_v[...].astype(BF16))

            @pl.when(i == nproc - 1)
            def _(sb=sb, a_r=a_r, a_i=a_i):
                dar_ref[sb] = _colsum(a_r)
                dai_ref[sb] = _colsum(a_i)

    blk = lambda i: (order(i), 0)
    blk_dy = lambda i: (jnp.minimum(order(i), lch - 1), 0)
    mat = (nsb, cb, cs)
    return _pcall(
        body, name=name, grid=(nproc,), xchg=xchg, edges=_edges1(nproc),
        out_shape=(jax.ShapeDtypeStruct((t, cu), F32),
                   jax.ShapeDtypeStruct((nsb, 1, cs), F32), jax.ShapeDtypeStruct((nsb, 1, cs), F32),
                   jax.ShapeDtypeStruct(mat, F32), jax.ShapeDtypeStruct(mat, F32),
                   jax.ShapeDtypeStruct(mat, F32), jax.ShapeDtypeStruct(mat, F32),
                   jax.ShapeDtypeStruct((1, cu), F32)),
        in_specs=[pl.BlockSpec((TC, cu), blk_dy), pl.BlockSpec((TC, cu), blk), pl.BlockSpec((TC, nsb * cs), blk),
                  pl.BlockSpec((TC, nsb * cs), blk), pl.BlockSpec((8, nsb * cs), blk), pl.BlockSpec((8, nsb * cs), blk),
                  _const(mat), _const(mat), _const(mat), _const(mat), _const(tre.shape), _const(tim.shape),
                  _const((1, cu))],
        out_specs=(pl.BlockSpec((TC, cu), blk), _const((nsb, 1, cs)), _const((nsb, 1, cs)),
                   _const(mat), _const(mat), _const(mat), _const(mat), _const((1, cu))),
        scratch_shapes=[pltpu.VMEM((nsb, 8, cs), F32), pltpu.VMEM((nsb, 8, cs), F32), pltpu.VMEM((2, TC, cs), F32),
                        pltpu.VMEM((2, TC, cs), F32), pltpu.VMEM((nsb, 8, cs), F32), pltpu.VMEM((nsb, 8, cs), F32),
                        pltpu.VMEM((3, nsb, TC, cb), F32)],
        compiler_params=_params(("arbitrary",)),
    )(dy_lat, up, hr, hi, cinr, cini, bre, bim, ctr, cti, tre, tim, dskip)


def _gelu(x):
    k = math.sqrt(2.0 / math.pi)
    return 0.5 * x * (1.0 + jnp.tanh(k * (x + 0.044715 * (x * x * x))))


def _gelu_grad(x):
    k = math.sqrt(2.0 / math.pi)
    th = jnp.tanh(k * (x + 0.044715 * (x * x * x)))
    return 0.5 * (1.0 + th) + 0.5 * x * (1.0 - th * th) * (k * (1.0 + 3.0 * 0.044715 * (x * x)))


def _mix_out_fwd(yf, yb, dpool, x1, modtab, w_pool, pscale, w_glu, w_out):
    l, hm = dpool.shape
    d = x1.shape[1]

    def body(yf_ref, yb_ref, dp_ref, x_ref, mod_ref, wp_ref, ps_ref, wg_ref, wo_ref, x2_ref, yp_ref, cat_ref, mix_ref):
        ypre = yf_ref[...] + yb_ref[...]
        yp_ref[...] = ypre
        yg = _gelu(ypre)
        y2 = yg * _sigmoid(_dot(yg.astype(BF16), wg_ref[...]))
        po = _dot(dp_ref[...].astype(BF16), wp_ref[...]) * ps_ref[...]
        cat = jnp.concatenate([po, y2], axis=1).astype(BF16)
        cat_ref[...] = cat
        mix = _dot(cat, wo_ref[...])
        mix_ref[...] = mix.astype(BF16)
        x2_ref[...] = x_ref[...] + mod_ref[0, 5:6, :] * mix

    row = lambda i: (i, 0)
    lrow = row
    return _pcall(
        body, name="mix_out_fwd", grid=(l // TM,),
        out_shape=(jax.ShapeDtypeStruct((l, d), F32), jax.ShapeDtypeStruct((l, hm), F32),
                   jax.ShapeDtypeStruct((l, d), BF16), jax.ShapeDtypeStruct((l, d), BF16)),
        in_specs=[pl.BlockSpec((TM, hm), lrow), pl.BlockSpec((TM, hm), lrow), pl.BlockSpec((TM, hm), row),
                  pl.BlockSpec((TM, d), lrow), _mod_spec(d, False), _const(w_pool.shape), _const((1, hm)),
                  _const(w_glu.shape), _resident(w_out.shape)],
        out_specs=(pl.BlockSpec((TM, d), row), pl.BlockSpec((TM, hm), row), pl.BlockSpec((TM, d), row),
                   pl.BlockSpec((TM, d), row)),
        compiler_params=_params(("arbitrary",)),
    )(yf, yb, dpool, x1, modtab, w_pool, pscale, w_glu, w_out)


def _mix_out_bwd(dx2, mix, cat, ypre, dpool, modtab, w_pool, pscale, w_glu, w_out):
    l, hm = dpool.shape
    d = dx2.shape[1]
    nt = l // TML

    def body(dx_ref, mix_ref, cat_ref, yp_ref, dp_ref, mod_ref, wp_ref, ps_ref, wg_ref, wo_ref,
             dyp_ref, ddp_ref, dwo_ref, dwg_ref, dwp_ref, dps_ref, dgate_ref, acc_o, acc_g):
        i = pl.program_id(0)

        @pl.when(i == 0)
        def _():
            for ref in (acc_o, acc_g, dwp_ref, dps_ref, dgate_ref):
                ref[...] = jnp.zeros_like(ref)

        dx = dx_ref[...]
        dgate_ref[...] += _colsum(dx * mix_ref[...].astype(F32))
        dmix = (dx * mod_ref[0, 5:6, :]).astype(BF16)
        dcat = _dot_nt(dmix, wo_ref[...])
        acc_o[...] += _dot_tn(cat_ref[...], dmix)
        dpo, dy2 = dcat[:, :hm], dcat[:, hm:]
        dpb = dp_ref[...].astype(BF16)
        pp = _dot(dpb, wp_ref[...])
        dps_ref[...] += _colsum(dpo * pp)
        dpp = (dpo * ps_ref[...]).astype(BF16)
        ddp_ref[...] = _dot_nt(dpp, wp_ref[...])
        dwp_ref[...] += _dot_tn(dpb, dpp)
        ypre = yp_ref[...]
        yg = _gelu(ypre)
        ygb = yg.astype(BF16)
        s = _sigmoid(_dot(ygb, wg_ref[...]))
        dq = (dy2 * yg * s * (1.0 - s)).astype(BF16)
        dyg = dy2 * s + _dot_nt(dq, wg_ref[...])
        acc_g[...] += _dot_tn(ygb, dq)
        dyp_ref[...] = dyg * _gelu_grad(ypre)

        @pl.when(i == nt - 1)
        def _():
            dwo_ref[...] = acc_o[...].astype(BF16)
            dwg_ref[...] = acc_g[...].astype(BF16)

    row = lambda i: (i, 0)
    return _pcall(
        body, name="mix_out_bwd", grid=(nt,),
        out_shape=(jax.ShapeDtypeStruct((l, hm), F32), jax.ShapeDtypeStruct((l, hm), F32),
                   jax.ShapeDtypeStruct((d, d), BF16), jax.ShapeDtypeStruct((hm, hm), BF16),
                   jax.ShapeDtypeStruct((hm, hm), F32), jax.ShapeDtypeStruct((1, hm), F32),
                   jax.ShapeDtypeStruct((1, d), F32)),
        in_specs=[pl.BlockSpec((TML, d), row), pl.BlockSpec((TML, d), row), pl.BlockSpec((TML, d), row),
                  pl.BlockSpec((TML, hm), row), pl.BlockSpec((TML, hm), row), _mod_spec(d, False),
                  _const(w_pool.shape), _const((1, hm)), _const(w_glu.shape), _resident(w_out.shape)],
        out_specs=(pl.BlockSpec((TML, hm), row), pl.BlockSpec((TML, hm), row), _const((d, d)), _const((hm, hm)),
                   _const((hm, hm)), _const((1, hm)), _const((1, d))),
        scratch_shapes=[pltpu.VMEM((d, d), F32), pltpu.VMEM((hm, hm), F32)],
        compiler_params=_params(("arbitrary",)),
    )(dx2, mix, cat, ypre, dpool, modtab, w_pool, pscale, w_glu, w_out)


def _loss_head(xx, target, gg, loss_ref, dg_ref):
    d = xx.shape[-1]
    rstd = lax.rsqrt(jnp.mean(xx * xx, axis=-1, keepdims=True) + EPS)
    xhat = xx * rstd
    err = xhat * gg - target
    row_loss = jnp.mean(err * err, axis=-1, keepdims=True)
    loss_ref[...] += 0.5 * jnp.sum(row_loss, axis=0, keepdims=True)
    dout = err * (1.0 / d)
    dg_ref[...] += _colsum(dout * xhat)
    dxhat = dout * gg
    return rstd * (dxhat - xhat * jnp.mean(dxhat * xhat, axis=-1, keepdims=True))


def _adamw(parts, w, m, v, name):
    shape = w.shape
    c = shape[-1]
    r = int(np.prod(shape)) // c
    npart = parts.shape[0]
    tr = r
    for cand in (1024, 512, 256, 128, 64, 32, 16):
        if r % cand == 0 and cand * max(c, LANE) * 4 <= ADAM_BLOCK_BYTES:
            tr = cand
            break
    c1 = 1.0 - ADAM_B1
    c2 = 1.0 - ADAM_B2
    bc1 = 1.0 - ADAM_B1 ** ADAM_STEP
    bc2 = 1.0 - ADAM_B2 ** ADAM_STEP

    def body(p_ref, w_ref, m_ref, v_ref, g_ref, d_ref, mo_ref, vo_ref):
        g = p_ref[0].astype(F32)
        for k in range(1, npart):
            g = g + p_ref[k].astype(F32)
        mn = ADAM_B1 * m_ref[...] + c1 * g
        vn = ADAM_B2 * v_ref[...] + c2 * (g * g)
        m_hat = mn / bc1
        v_hat = vn / bc2
        g_ref[...] = g
        mo_ref[...] = mn
        vo_ref[...] = vn
        d_ref[...] = -ADAM_LR * (m_hat / (jnp.sqrt(v_hat) + ADAM_EPS) + ADAM_WD * w_ref[...])

    spec = pl.BlockSpec((tr, c), lambda i: (i, 0))
    outs = _pcall(
        body, name=name, grid=(r // tr,),
        out_shape=tuple(jax.ShapeDtypeStruct((r, c), F32) for _ in range(4)),
        in_specs=[pl.BlockSpec((npart, tr, c), lambda i: (0, i, 0)), spec, spec, spec],
        out_specs=(spec, spec, spec, spec),
        compiler_params=_params(("arbitrary",)),
    )(parts.reshape(npart, r, c), w.reshape(r, c), m.reshape(r, c), v.reshape(r, c))
    return tuple(o.reshape(shape) for o in outs)


def _adamw_ffn(parts, row0, w, m, v, name):
    _, nl, r, c = w.shape
    rp = parts[0][0].shape[1]
    tc = ADAM_FFN_LANES
    grid = (c // tc,)
    p_spec = lambda a: pl.BlockSpec((a.shape[0], rp, tc), lambda i: (0, 0, i))
    w_spec = pl.BlockSpec((1, nl, r, tc), lambda i: (0, 0, 0, i))
    take = lambda ref, k: ref[k, row0:row0 + r, :]
    flat = [a for layer in parts for a in layer]
    counts = [len(layer) for layer in parts]
    c1 = 1.0 - ADAM_B1
    c2 = 1.0 - ADAM_B2
    bc1 = 1.0 - ADAM_B1 ** ADAM_STEP
    bc2 = 1.0 - ADAM_B2 ** ADAM_STEP

    def body(*refs):
        w_ref, m_ref, v_ref, g_ref, d_ref, mo_ref, vo_ref = refs[len(flat):]
        first = 0
        for lyr in range(nl):
            g = None
            for ref in refs[first:first + counts[lyr]]:
                for k in range(ref.shape[0]):
                    term = take(ref, k).astype(F32)
                    g = term if g is None else g + term
            first += counts[lyr]
            mn = ADAM_B1 * m_ref[0, lyr] + c1 * g
            vn = ADAM_B2 * v_ref[0, lyr] + c2 * (g * g)
            g_ref[0, lyr] = g
            mo_ref[0, lyr] = mn
            vo_ref[0, lyr] = vn
            d_ref[0, lyr] = -ADAM_LR * ((mn / bc1) / (jnp.sqrt(vn / bc2) + ADAM_EPS) + ADAM_WD * w_ref[0, lyr])

    return _pcall(
        body, name=name, grid=grid, out_shape=tuple(jax.ShapeDtypeStruct(w.shape, F32) for _ in range(4)),
        in_specs=[p_spec(a) for a in flat] + [w_spec] * 3, out_specs=(w_spec,) * 4,
        compiler_params=_params(("arbitrary",)),
    )(*flat, w, m, v)


def _blockdiag(xb):
    n, a, b = xb.shape[-3:]
    eye = jnp.eye(n, dtype=xb.dtype)
    out = xb[..., :, :, None, :] * eye[:, None, :, None]
    return out.reshape(xb.shape[:-3] + (n * a, n * b))


def _diagblocks(mat, n):
    lead = mat.shape[:-2]
    a, b = mat.shape[-2] // n, mat.shape[-1] // n
    m5 = mat.reshape(lead + (n, a, n, b))
    nl = len(lead)
    dg = jnp.diagonal(m5, axis1=nl, axis2=nl + 2)
    return jnp.moveaxis(dg, -1, nl)


def _pad128(a):
    flat = a.reshape(-1)
    pad = (-flat.shape[0]) % LANE
    return jnp.pad(flat, (0, pad)) if pad else flat


def kernel(x, c, ctx, c_ctx, norm_g, w_ada, b_ada, ffn_w1, ffn_w3, ffn_w2, w_in, pool_w, pool_scale, s5_a_re, s5_a_im, s5_log_dt, s5_b_re, s5_b_im, s5_c_re, s5_c_im, s5_d, s5_w_glu, w_out, final_g, loss_target, m_c_ctx, m_norm_g, m_w_ada, m_b_ada, m_ffn_w1, m_ffn_w3, m_ffn_w2, m_w_in, m_pool_w, m_pool_scale, m_s5_a_re, m_s5_a_im, m_s5_log_dt, m_s5_b_re, m_s5_b_im, m_s5_c_re, m_s5_c_im, m_s5_d, m_s5_w_glu, m_w_out, m_final_g, v_c_ctx, v_norm_g, v_w_ada, v_b_ada, v_ffn_w1, v_ffn_w3, v_ffn_w2, v_w_in, v_pool_w, v_pool_scale, v_s5_a_re, v_s5_a_im, v_s5_log_dt, v_s5_b_re, v_s5_b_im, v_s5_c_re, v_s5_c_im, v_s5_d, v_s5_w_glu, v_w_out, v_final_g):
    weights = dict(c_ctx=c_ctx, norm_g=norm_g, w_ada=w_ada, b_ada=b_ada, ffn_w1=ffn_w1, ffn_w3=ffn_w3, ffn_w2=ffn_w2,
                   w_in=w_in, pool_w=pool_w, pool_scale=pool_scale, s5_a_re=s5_a_re, s5_a_im=s5_a_im,
                   s5_log_dt=s5_log_dt, s5_b_re=s5_b_re, s5_b_im=s5_b_im, s5_c_re=s5_c_re, s5_c_im=s5_c_im, s5_d=s5_d,
                   s5_w_glu=s5_w_glu, w_out=w_out, final_g=final_g)
    mom_m = dict(c_ctx=m_c_ctx, norm_g=m_norm_g, w_ada=m_w_ada, b_ada=m_b_ada, ffn_w1=m_ffn_w1, ffn_w3=m_ffn_w3,
                 ffn_w2=m_ffn_w2, w_in=m_w_in, pool_w=m_pool_w, pool_scale=m_pool_scale, s5_a_re=m_s5_a_re,
                 s5_a_im=m_s5_a_im, s5_log_dt=m_s5_log_dt, s5_b_re=m_s5_b_re, s5_b_im=m_s5_b_im, s5_c_re=m_s5_c_re,
                 s5_c_im=m_s5_c_im, s5_d=m_s5_d, s5_w_glu=m_s5_w_glu, w_out=m_w_out, final_g=m_final_g)
    mom_v = dict(c_ctx=v_c_ctx, norm_g=v_norm_g, w_ada=v_w_ada, b_ada=v_b_ada, ffn_w1=v_ffn_w1, ffn_w3=v_ffn_w3,
                 ffn_w2=v_ffn_w2, w_in=v_w_in, pool_w=v_pool_w, pool_scale=v_pool_scale, s5_a_re=v_s5_a_re,
                 s5_a_im=v_s5_a_im, s5_log_dt=v_s5_log_dt, s5_b_re=v_s5_b_re, s5_b_im=v_s5_b_im, s5_c_re=v_s5_c_re,
                 s5_c_im=v_s5_c_im, s5_d=v_s5_d, s5_w_glu=v_s5_w_glu, w_out=v_w_out, final_g=v_final_g)

    l, d = x.shape[1], x.shape[2]
    lc = ctx.shape[1]
    t = l + TM
    rows = l // GRID_W
    fblk = ffn_w1.shape[-1]
    ngrp, gp = s5_a_re.shape[2], s5_a_re.shape[3]
    gc = s5_b_re.shape[3]
    hm = ngrp * gc
    nsb = 4
    gsb = ngrp // nsb
    assert lc == TC and TM % TC == 0 and l % TM == 0 and TM == TML and hm == 4 * LANE and ngrp * gp == nsb * 512
    me = 4 * lax.axis_index("x") + 2 * lax.axis_index("y") + lax.axis_index("c")

    padr = ((0, FPAD - fblk), (0, 0))
    w1_t, w3_t = jnp.swapaxes(ffn_w1, 2, 3), jnp.swapaxes(ffn_w3, 2, 3)
    w13_loc = [jnp.concatenate([jnp.pad(w1_t[0, k], padr), jnp.pad(w3_t[0, k], padr)], axis=0).astype(BF16)
               for k in range(2)]
    w2_loc = [jnp.pad(ffn_w2[0, k], padr).astype(BF16) for k in range(2)]
    wa = w_ada.shape[-1]
    b_blk = lax.dynamic_slice_in_dim(b_ada, me * wa, wa, axis=1)
    w13_0, w2_0, g_all, c_all, mod_all = _front(w13_loc[0], w2_loc[0], norm_g[0], c, c_ctx.reshape(1, d), w_ada[0],
                                                b_blk)
    w2_0 = w2_0.reshape(NDEV // 2, 2 * FPAD, d)
    c_all = c_all.reshape(NDEV, d)
    g_all = g_all.transpose(1, 0, 2).reshape(3, d)

    c16 = jnp.concatenate([c_all, c_ctx.reshape(1, d), jnp.zeros((7, d), F32)], axis=0)
    mod_full = mod_all.transpose(1, 0, 2).reshape(16, NDEV * wa)
    mod_l = lax.dynamic_index_in_dim(mod_full, me, axis=0, keepdims=False).reshape(9, d)
    modtab = jnp.stack([mod_full[8].reshape(9, d), mod_l])

    x0 = [jnp.pad(ctx[0], ((0, TM - lc), (0, 0))), x[0]]
    g0, g1, g2 = g_all[0:1], g_all[1:2], g_all[2:3]
    x1, h0, ab0, o0, w13_1, w2_1, w_in_f, w_out_f, w_glu_f = _ffn_fwd(
        x0, modtab, g0, w13_0, w2_0, 0, True, "ffn0_fwd",
        xchg=[(w13_loc[1], GATHER_REL), (w2_loc[1], GATHER_REL), (w_in[0].astype(BF16), GATHER),
              (w_out[0].astype(BF16), GATHER), (s5_w_glu[0].astype(BF16), GATHER)])
    w2_1 = w2_1.reshape(NDEV // 2, 2 * FPAD, d)
    w_in_f = w_in_f.reshape(d, -1)
    w_out_f = w_out_f.reshape(-1, d)
    w_glu_f = w_glu_f.reshape(hm, hm)
    h1, u_pool, up = _mix_in_fwd(x1, modtab, g1, w_in_f)

    dpool = _pool(u_pool, rows, False, "pool_fwd")
    w_pool_bd = _blockdiag(pool_w[0]).astype(BF16)

    q = 2 * ngrp
    kvec = np.concatenate([np.arange(1, SEG + 1), SEG * np.arange(1, 9), SEG * np.arange(8, 0, -1)]).astype(np.float32)
    a_re3, a_im3 = s5_a_re[0].reshape(q, 1, gp), s5_a_im[0].reshape(q, 1, gp)
    ldt3 = jnp.broadcast_to(s5_log_dt[0].reshape(q, 1, 1), (q, 1, gp))
    c_re3, c_im3 = s5_c_re[0].reshape(q, gc, gp), s5_c_im[0].reshape(q, gc, gp)
    cp_re, cp_im, tab_re, tab_im = _s5_prep(a_re3, a_im3, ldt3, c_re3, c_im3, jnp.broadcast_to(jnp.asarray(kvec).reshape(1, NTAB, 1), (1, NTAB, gp)))
    tabs = [(tab_re.reshape(2, ngrp, NTAB, gp)[k].transpose(1, 0, 2).reshape(NTAB, ngrp * gp),
             tab_im.reshape(2, ngrp, NTAB, gp)[k].transpose(1, 0, 2).reshape(NTAB, ngrp * gp)) for k in range(2)]
    b_t = lambda b: _blockdiag(b[0].reshape(nsb, gsb, gp, gc).transpose(0, 1, 3, 2)).astype(BF16)
    bre, bim = b_t(s5_b_re), b_t(s5_b_im)
    cp_t = lambda cp: _blockdiag(cp.reshape(2, nsb, gsb, gc, gp)).astype(BF16)
    ctr, cti = cp_t(cp_re), cp_t(cp_im)
    cmr, cmi = jnp.swapaxes(ctr, -1, -2), jnp.swapaxes(cti, -1, -2)
    dskip = s5_d

    lch = l // TC
    nproc = lch + 1
    order_f = lambda i: jnp.where(i == 0, lch, i - 1)
    order_b = lambda i: jnp.where(i == 0, lch, lch - i)
    rorder_f = lambda i: jnp.where(i == lch, lch, lch - 1 - i)
    rorder_b = lambda i: i
    y_f, hr_f, hi_f, cinr_f, cini_f = _s5_fwd(up, bre, bim, cmr[0], cmi[0], tabs[0][0], tabs[0][1], dskip,
                                               order_f, nproc, False, True, "s5_fwd_f")
    y_b, hr_b, hi_b, cinr_b, cini_b = _s5_fwd(up, bre, bim, cmr[1], cmi[1], tabs[1][0], tabs[1][1], dskip,
                                              order_b, nproc, True, False, "s5_fwd_b")

    x2, ypre, cat, mix = _mix_out_fwd(y_f, y_b, dpool, x1, modtab, w_pool_bd, pool_scale, w_glu_f, w_out_f)
    dx3, h2, ab2, o2, loss_part, dfinal_g = _ffn_fwd([x2], modtab, g2, w13_1, w2_1, 6, False, "ffn1_fwd",
                                                     head=(loss_target[0], final_g.reshape(1, d)))

    dhp2, d13_1, d2_1 = _ffn_bwd(dx3, modtab, h2, ab2, w13_1, w2_1, 6, False, "ffn1_bwd")
    dx2, dmod_678, dg2 = _ffn_bwd_norm(dx3, [dhp2], [x2], o2, modtab, g2, 6, False, "ffn1_bwd_norm")
    dypre, ddpool, dw_out, dw_glu, dw_pool_bd, dpscale, dgate5 = _mix_out_bwd(
        dx2, mix, cat, ypre, dpool, modtab, w_pool_bd, pool_scale, w_glu_f, w_out_f)

    du_f, dar_f, dai_f, dcr_f, dci_f, dbr_f, dbi_f, dd_skip, r13_1, r2_1 = _s5_bwd(
        dypre, up, hr_f, hi_f, cinr_f, cini_f, bre, bim, ctr[0], cti[0], tabs[0][0], tabs[0][1], dskip,
        rorder_f, nproc, False, True, "s5_bwd_f",
        xchg=[(d13_1, A2A_REL), (d2_1.reshape(NDEV, FPAD, d), A2A_REL)])
    du_b, dar_b, dai_b, dcr_b, dci_b, dbr_b, dbi_b, _, r_out, r_glu = _s5_bwd(
        dypre, up, hr_b, hi_b, cinr_b, cini_b, bre, bim, ctr[1], cti[1], tabs[1][0], tabs[1][1], dskip,
        rorder_b, nproc, True, False, "s5_bwd_b",
        xchg=[(dw_out.reshape(NDEV, -1, d), A2A), (dw_glu.reshape(NDEV, hm // NDEV, hm), A2A)])

    dab_r = jnp.stack([dar_f, dar_b]).reshape(q, 1, gp)
    dab_i = jnp.stack([dai_f, dai_b]).reshape(q, 1, gp)
    dcp_r = _diagblocks(jnp.stack([dcr_f, dcr_b]), gsb).reshape(q, gc, gp)
    dcp_i = _diagblocks(jnp.stack([dci_f, dci_b]), gsb).reshape(q, gc, gp)
    ga_re, ga_im, gldt, gc_re, gc_im = _s5_param_bwd(a_re3, a_im3, ldt3, c_re3, c_im3, dab_r, dab_i, dcp_r, dcp_i)
    gb_re = (_diagblocks(dbr_f, gsb) + _diagblocks(dbr_b, gsb)).transpose(0, 1, 3, 2).reshape(ngrp, gp, gc)
    gb_im = (_diagblocks(dbi_f, gsb) + _diagblocks(dbi_b, gsb)).transpose(0, 1, 3, 2).reshape(ngrp, gp, gc)

    def pack(small):
        offs, pieces, off = {}, [], 0
        for k_, a_ in small.items():
            p_ = _pad128(a_.astype(F32))
            offs[k_] = (off, int(np.prod(a_.shape)))
            off += p_.shape[0]
            pieces.append(p_)
        return jnp.concatenate(pieces).reshape(1, off), offs

    gw_pool = _diagblocks(dw_pool_bd, 4)
    bundle_a, offs = pack(dict(pool_w=gw_pool, pool_scale=dpscale, s5_a_re=ga_re, s5_a_im=ga_im,
                               s5_log_dt=gldt[:, 0, 0], s5_b_re=gb_re, s5_b_im=gb_im, s5_c_re=gc_re, s5_c_im=gc_im,
                               s5_d=dd_skip, final_g=dfinal_g))

    du_pool = _pool(ddpool, rows, True, "pool_bwd")
    dx1, dw_in, dmod_34, dg1 = _mix_in_bwd(du_pool, du_f, du_b, dx2, h1, x1, modtab, g1, w_in_f, lc)
    dhp0a, d13_0a, d2_0a, bund_a, r_in = _ffn_bwd(
        dx1, modtab, h0, ab0, w13_0, w2_0, 0, True, "ffn0_bwd_a", pair0=0, npair=2,
        xchg=[(bundle_a, GATHER), (dw_in.reshape(NDEV, d // NDEV, -1), A2A)])
    dhp0b, d13_0b, d2_0b, r13_0a, r2_0a = _ffn_bwd(
        dx1, modtab, h0, ab0, w13_0, w2_0, 0, True, "ffn0_bwd_b", pair0=2, npair=2,
        xchg=[(d13_0a, A2A_SAME), (d2_0a.reshape(NDEV // 2, FPAD, d), A2A_SAME)])
    dx0, dmod_012, dg0, r13_0b, r2_0b = _ffn_bwd_norm(
        dx1, [dhp0a, dhp0b], x0, o0, modtab, g0, 0, True, "ffn0_bwd_norm",
        xchg=[(d13_0b, A2A_OTHER), (d2_0b.reshape(NDEV // 2, FPAD, d), A2A_OTHER)])
    grad_x = dx0.reshape(1, l, d)

    dmod_c = jnp.concatenate([dmod_012[0], dmod_34[0], jnp.zeros((4, d), F32)], axis=0)
    dmod_l = jnp.concatenate([dmod_012[1], dmod_34[1], dgate5, dmod_678[0]], axis=0)
    bundle_b, offs_b = pack(dict(norm_g=jnp.concatenate([dg0, dg1, dg2], axis=0), dmod_l=dmod_l, dmod_c=dmod_c,
                                 loss=loss_part[:, :1]))
    bund_b = _exchange([(bundle_b, GATHER)], "exchange_small")[0]
    bunds = {**{k_: (bund_a.reshape(NDEV, -1), v_) for k_, v_ in offs.items()},
             **{k_: (bund_b.reshape(NDEV, -1), v_) for k_, v_ in offs_b.items()}}
    r13, r2 = [(r13_0a, r13_0b), (r13_1,)], [(r2_0a, r2_0b), (r2_1,)]

    def piece(name):
        b_, (o_, n_) = bunds[name]
        return b_[:, o_:o_ + n_]

    loss = jnp.sum(piece("loss")[:, 0])

    dl_all = lax.dynamic_slice_in_dim(piece("dmod_l"), me * wa, wa, axis=1)
    dc_all = lax.dynamic_slice_in_dim(piece("dmod_c"), me * wa, wa, axis=1)
    g_wada, gc_part = _ada_bwd(c16, w_ada[0], dl_all, dc_all)
    gc_all = _exchange([(gc_part, GATHER)], "exchange_cctx")[0]

    parts = {
        "c_ctx": gc_all.reshape(NDEV, d),
        "norm_g": lax.dynamic_slice_in_dim(piece("norm_g").reshape(NDEV, 3, d), me * (d // NDEV), d // NDEV,
                                           axis=2).reshape((NDEV,) + norm_g.shape),
        "w_ada": g_wada.reshape((1,) + w_ada.shape),
        "b_ada": jnp.concatenate([piece("dmod_l"), piece("dmod_c")], axis=0).reshape((2 * NDEV,) + b_ada.shape),
        "w_in": r_in.reshape((NDEV,) + w_in.shape),
        "s5_w_glu": r_glu.reshape((NDEV,) + s5_w_glu.shape),
        "w_out": r_out.reshape((NDEV,) + w_out.shape),
    }
    for k_ in ("pool_w", "pool_scale", "s5_a_re", "s5_a_im", "s5_log_dt", "s5_b_re", "s5_b_im", "s5_c_re", "s5_c_im",
               "s5_d", "final_g"):
        parts[k_] = piece(k_).reshape((NDEV,) + weights[k_].shape)

    grads, deltas, new_m, new_v = [], [], [], []
    ffn_parts = {"ffn_w1": (r13, 0, True), "ffn_w3": (r13, FPAD, True), "ffn_w2": (r2, 0, False)}
    for k_ in weights:
        if k_ in ffn_parts:
            shards, row0, transposed = ffn_parts[k_]
            flip = (lambda a: jnp.swapaxes(a, 2, 3)) if transposed else (lambda a: a)
            outs = _adamw_ffn(shards, row0, flip(weights[k_]), flip(mom_m[k_]), flip(mom_v[k_]), "adamw_" + k_)
            g_, d_, m_, v_ = (flip(o_) for o_ in outs)
        else:
            g_, d_, m_, v_ = _adamw(parts[k_], weights[k_], mom_m[k_], mom_v[k_], "adamw_" + k_)
        grads.append(g_)
        deltas.append(d_)
        new_m.append(m_)
        new_v.append(v_)
    return (loss, grad_x, *grads, *deltas, *new_m, *new_v)
```

```python
import functools
import math

import numpy as np
import jax
import jax.numpy as jnp
from jax import lax
from jax.experimental import pallas as pl
from jax.experimental.pallas import tpu as pltpu

F32 = jnp.float32
BF16 = jnp.bfloat16
AXES = ("x", "y", "c")
NDEV = 8
EPS = 1e-6
TM = 512
TML = 512
TC = 256
SEG = TC // 8
NTAB = SEG + 16
S5_UNROLL = True
GRID_W = 64
POOL_WINDOWS = (2, 4, 8, 16)
LANE = 128
FPAD = 384
VMEM_LIMIT = 56 * 1024 * 1024
ADAM_BLOCK_BYTES = 1024 * 1024
ADAM_FFN_LANES = 256
ADAM_LR, ADAM_B1, ADAM_B2, ADAM_EPS, ADAM_WD, ADAM_STEP = 0.001, 0.9, 0.999, 1e-08, 0.01, 10


def _raw_call(body, kw):
    return pl.pallas_call(body, **kw)


def _pcall(body, xchg=None, edges=None, **kw):
    extra = ()
    if xchg:
        arrs, kinds = [a for a, _ in xchg], [k for _, k in xchg]
        n = len(arrs)
        n_in, n_out, n_scr = len(kw["in_specs"]), len(kw["out_shape"]), len(kw.get("scratch_shapes", ()))
        inner = body

        def hosted(*refs):
            a, b = n_in, n_in + n
            c_, d_ = b + n_out, b + n_out + n
            e = d_ + n_scr
            first, mid, last = edges()

            @pl.when(first)
            def _():
                _xchg_start(refs[a:b], refs[c_:d_], refs[e:], kinds)

            inner(*refs[:a], *refs[b:c_], *refs[d_:e])

            if any(k == GATHER_REL for k in kinds):
                @pl.when(mid)
                def _():
                    _xchg_relay(refs[a:b], refs[c_:d_], refs[e:], kinds)

            @pl.when(last)
            def _():
                _xchg_finish(refs[a:b], refs[c_:d_], refs[e:], kinds)

        body = hosted
        any_spec = pl.BlockSpec(memory_space=pl.ANY)
        kw = dict(kw, in_specs=list(kw["in_specs"]) + [any_spec] * n,
                  out_shape=tuple(kw["out_shape"]) + _xchg_shapes(arrs, kinds),
                  out_specs=tuple(kw["out_specs"]) + (any_spec,) * n,
                  scratch_shapes=list(kw.get("scratch_shapes", ())) + _xchg_sems(n))
        extra = tuple(arrs)
    call = _raw_call(body, kw)
    return (lambda *args: call(*args, *extra)) if extra else call


def _params(sem):
    return pltpu.CompilerParams(dimension_semantics=sem, vmem_limit_bytes=VMEM_LIMIT)


def _dot(a, b):
    return jnp.dot(a, b, preferred_element_type=F32)


def _dot_nt(a, b):
    return lax.dot_general(a, b, (((1,), (1,)), ((), ())), preferred_element_type=F32)


def _dot_tn(a, b):
    return lax.dot_general(a, b, (((0,), (0,)), ((), ())), preferred_element_type=F32)


def _sigmoid(x):
    return 1.0 / (1.0 + jnp.exp(-x))


def _colsum(a):
    return jnp.sum(a, axis=0, keepdims=True)


def _resident(shape):
    nd = len(shape)
    return pl.BlockSpec(shape, lambda *_: (0,) * nd, pipeline_mode=pl.Buffered(1))


def _const(shape):
    nd = len(shape)
    return pl.BlockSpec(shape, lambda *_: (0,) * nd)


GATHER = "gather"
GATHER_REL = "gather_rel"
A2A = "a2a"
A2A_REL = "a2a_rel"
A2A_SAME = "a2a_same"
A2A_OTHER = "a2a_other"


def _exchange(items, name):
    arrs, kinds = [a for a, _ in items], [k for _, k in items]
    n = len(arrs)

    def body(*refs):
        _xchg_start(refs[:n], refs[n:2 * n], refs[2 * n:], kinds)
        _xchg_relay(refs[:n], refs[n:2 * n], refs[2 * n:], kinds)
        _xchg_finish(refs[:n], refs[n:2 * n], refs[2 * n:], kinds)

    any_spec = pl.BlockSpec(memory_space=pl.ANY)
    outs = _pcall(
        body, name=name, out_shape=_xchg_shapes(arrs, kinds), in_specs=[any_spec] * n, out_specs=[any_spec] * n,
        scratch_shapes=_xchg_sems(n),
    )(*arrs)
    return list(outs)


def _edges1(n0):
    return lambda: (pl.program_id(0) == 0, pl.program_id(0) == (7 * n0) // 8, pl.program_id(0) == n0 - 1)


def _edges2(n0, n1):
    def edges():
        step = pl.program_id(0) * n1 + pl.program_id(1)
        return step == 0, step == (7 * n0 * n1) // 8, step == n0 * n1 - 1
    return edges


def _xchg_shapes(arrs, kinds):
    return tuple(jax.ShapeDtypeStruct(((NDEV,) if k in (GATHER, GATHER_REL) else ()) + tuple(a.shape), a.dtype)
                 for a, k in zip(arrs, kinds))


def _xchg_sems(n):
    return [pltpu.SemaphoreType.DMA((n * (NDEV - 1),)), pltpu.SemaphoreType.DMA((n * (NDEV - 1),)),
            pltpu.SemaphoreType.DMA((n,))]


def _xchg_plan(in_refs, out_refs, sems, kinds):
    send_sems, recv_sems, loc_sems = sems
    x, y, c = (lax.axis_index(a) for a in AXES)
    my_abs, my_chip = 4 * x + 2 * y + c, 2 * x + y

    def peer(p):
        px = 1 - x if p & 4 else x
        py = 1 - y if p & 2 else y
        pc = 1 - c if p & 1 else c
        return (px, py, pc), 4 * px + 2 * py + pc, 2 * px + py

    starts, relays, recvs = [], [], []
    for k, kind in enumerate(kinds):
        src, dst = in_refs[k], out_refs[k]

        def remote(src_ref, row, p, pair, dst=dst, k=k):
            dev, sem = peer(p)[0], k * (NDEV - 1) + pair - 1
            return lambda: pltpu.make_async_remote_copy(
                src_ref=src_ref, dst_ref=dst.at[row], send_sem=send_sems.at[sem], recv_sem=recv_sems.at[sem],
                device_id=dev, device_id_type=pl.DeviceIdType.MESH)

        def local(src_ref, row, dst=dst, k=k):
            return lambda: pltpu.make_async_copy(src_ref, dst.at[row], loc_sems.at[k])

        if kind == GATHER:
            for p in range(1, NDEV):
                starts.append((remote(src, my_abs, p, p), False))
                recvs.append(remote(src, peer(p)[1], p, p))
            starts.append((local(src, my_abs), True))
        elif kind == GATHER_REL:
            for p in (1, 4, 2, 6):
                starts.append((remote(src, 4 * (p & 1) + my_chip, p, p), False))
            for q in (4, 2, 6):
                pchip = peer(q)[2]
                relays.append((remote(src, pchip, q, q), remote(dst.at[pchip], 4 + pchip, 1, q | 1)))
                recvs.append(remote(src, 4 + pchip, 1, q | 1))
            recvs.append(remote(src, 4 + my_chip, 1, 1))
            starts.append((local(src, my_chip), True))
        elif kind in (A2A, A2A_REL):
            for p in range(1, NDEV):
                _, pabs, pchip = peer(p)
                theirs, mine = (pabs, my_abs) if kind == A2A else (4 * (p & 1) + pchip, 4 * (p & 1) + my_chip)
                starts.append((remote(src.at[theirs], mine, p, p), False))
                recvs.append(remote(src.at[theirs], theirs, p, p))
            own = my_abs if kind == A2A else my_chip
            starts.append((local(src.at[own], own), True))
        else:
            for p in ((4, 2, 6) if kind == A2A_SAME else (1, 5, 3, 7)):
                pchip = peer(p)[2]
                starts.append((remote(src.at[pchip], my_chip, p, p), False))
                recvs.append(remote(src.at[pchip], pchip, p, p))
            if kind == A2A_SAME:
                starts.append((local(src.at[my_chip], my_chip), True))
    return starts, relays, recvs


def _xchg_start(in_refs, out_refs, sems, kinds):
    for make, _ in _xchg_plan(in_refs, out_refs, sems, kinds)[0]:
        make().start()


def _xchg_relay(in_refs, out_refs, sems, kinds):
    for arrival, forward in _xchg_plan(in_refs, out_refs, sems, kinds)[1]:
        arrival().wait_recv()
        forward().start()


def _xchg_finish(in_refs, out_refs, sems, kinds):
    starts, relays, recvs = _xchg_plan(in_refs, out_refs, sems, kinds)
    for make in recvs:
        make().wait_recv()
    for make, is_local in starts:
        if is_local:
            make().wait()
        else:
            make().wait_send()
    for _, forward in relays:
        forward().wait_send()


def _front(w13, w2, ng, c, c_ctx, w_blk, b_blk):
    d, wa = w_blk.shape
    big, big_kinds = [w13, w2, ng], [GATHER_REL, GATHER_REL, GATHER]

    def body(w13_ref, w2_ref, ng_ref, c_ref, cctx_ref, w_ref, b_ref, g13_ref, g2_ref, gng_ref, gc_ref, msl_ref, gm_ref,
             c16, mod, loc_sems, *sems):
        big_io = ((w13_ref, w2_ref, ng_ref), (g13_ref, g2_ref, gng_ref), sems[0:3], big_kinds)
        c_io = ((c_ref,), (gc_ref,), sems[3:6], [GATHER])
        mod_io = ((msl_ref,), (gm_ref,), sems[6:9], [GATHER])
        _xchg_start(*c_io)
        _xchg_start(*big_io)
        _xchg_finish(*c_io)
        rows = [pltpu.make_async_copy(gc_ref.at[k], c16.at[pl.ds(k, 1), :], loc_sems.at[k]) for k in range(NDEV)]
        for cp in rows:
            cp.start()
        c16[NDEV:NDEV + 1, :] = cctx_ref[...]
        c16[NDEV + 1:, :] = jnp.zeros((16 - NDEV - 1, d), F32)
        for cp in rows:
            cp.wait()
        cc = c16[...]
        mod[...] = _dot((cc * _sigmoid(cc)).astype(BF16), w_ref[...].astype(BF16)) + b_ref[...]
        out = pltpu.make_async_copy(mod, msl_ref, loc_sems.at[NDEV])
        out.start()
        out.wait()
        _xchg_start(*mod_io)
        _xchg_relay(*big_io)
        _xchg_finish(*big_io)
        _xchg_finish(*mod_io)

    any_spec = pl.BlockSpec(memory_space=pl.ANY)
    vmem = pl.BlockSpec(memory_space=pltpu.VMEM)
    outs = _pcall(
        body, name="exchange_weights",
        out_shape=_xchg_shapes(big + [c], big_kinds + [GATHER]) + (jax.ShapeDtypeStruct((16, wa), F32),
                                                                   jax.ShapeDtypeStruct((NDEV, 16, wa), F32)),
        in_specs=[any_spec] * 4 + [vmem] * 3, out_specs=[any_spec] * 6,
        scratch_shapes=[pltpu.VMEM((16, d), F32), pltpu.VMEM((16, wa), F32), pltpu.SemaphoreType.DMA((NDEV + 1,))]
        + _xchg_sems(3) + _xchg_sems(1) + _xchg_sems(1),
        compiler_params=_params(None),
    )(w13, w2, ng, c, c_ctx, w_blk, b_blk)
    return outs[0], outs[1], outs[2], outs[3], outs[5]


def _ada_bwd(c16, w_blk, dl, dc):
    d, w = w_blk.shape

    def body(c_ref, w_ref, dl_ref, dc_ref, gw_ref, gc_ref):
        cc = c_ref[...]
        sg = _sigmoid(cc)
        s = (cc * sg).astype(BF16)
        dctot = _colsum(dc_ref[...])
        dm = jnp.concatenate([dl_ref[...], jnp.broadcast_to(dctot, (8, w))], axis=0)
        rows = lax.broadcasted_iota(jnp.int32, (16, w), 0)
        dm = jnp.where(rows <= 8, dm, 0.0).astype(BF16)
        gw_ref[...] = _dot_tn(s, dm)
        t = _dot_nt(jnp.broadcast_to(dctot, (8, w)).astype(BF16), w_ref[...].astype(BF16))[0:1]
        c8, s8 = cc[8:9], sg[8:9]
        gc_ref[...] = t * (s8 * (1.0 + c8 * (1.0 - s8)))

    return _pcall(body, name="ada_bwd",
                  out_shape=(jax.ShapeDtypeStruct((d, w), F32), jax.ShapeDtypeStruct((1, d), F32)),
                  compiler_params=_params(None))(c16, w_blk, dl, dc)


def _norm_fwd(x, g, shift, scale):
    rstd = lax.rsqrt(jnp.mean(x * x, axis=-1, keepdims=True) + EPS)
    xhat = x * rstd
    xn = xhat * g
    return rstd, xhat, xn, xn * (1.0 + scale) + shift


def _norm_bwd(dh, rstd, xhat, xn, g, scale):
    dxn = dh * (1.0 + scale)
    dxhat = dxn * g
    dx = rstd * (dxhat - xhat * jnp.mean(dxhat * xhat, axis=-1, keepdims=True))
    return dx, _colsum(dh), _colsum(dh * xn), _colsum(dxn * xhat)


def _mod_spec(d, has_ctx):
    if has_ctx:
        return pl.BlockSpec((1, 9, d), lambda *ids: (jnp.minimum(ids[-1], 1), 0, 0))
    return pl.BlockSpec((1, 9, d), lambda *ids: (1, 0, 0))


def _row_specs(nx, tm, d):
    if nx == 1:
        return [pl.BlockSpec((tm, d), lambda i: (i, 0))]
    return [pl.BlockSpec((tm, d), lambda i: (0, 0)), pl.BlockSpec((tm, d), lambda i: (jnp.maximum(i - 1, 0), 0))]


def _rows(refs):
    if len(refs) == 1:
        return refs[0][...]
    return jnp.where(pl.program_id(0) == 0, refs[0][...], refs[1][...])


def _tile(has_ctx):
    return TM if has_ctx else TML


def _tile_of(nt, has_ctx):
    if has_ctx:
        return lambda i: jnp.where(i == 0, nt - 1, i - 1)
    return lambda i: i


def _dmod_spec(nrow, d, has_ctx):
    if has_ctx:
        return pl.BlockSpec((1, nrow, d), lambda i: (jnp.minimum(i, 1), 0, 0))
    return pl.BlockSpec((1, nrow, d), lambda i: (0, 0, 0))


def _ffn_fwd(x, modtab, g, w13, w2, k0, has_ctx, name, xchg=None, head=None):
    t, d = sum(a.shape[0] for a in x), x[0].shape[1]
    nb, w, _ = w13.shape
    fp = w // 2
    tm = _tile(has_ctx)
    nx = len(x)

    nhead = 0 if head is None else 2

    def body(*refs):
        mod_ref, g_ref, w13_ref, w2_ref = refs[nx:nx + 4]
        xo_ref, h_ref, ab_ref, o_ref = refs[nx + 4 + nhead:nx + 8 + nhead]
        xx = _rows(refs[:nx])
        shift, scale, gate = mod_ref[0, k0:k0 + 1, :], mod_ref[0, k0 + 1:k0 + 2, :], mod_ref[0, k0 + 2:k0 + 3, :]
        _, _, _, h = _norm_fwd(xx, g_ref[...], shift, scale)
        hb = h.astype(BF16)
        h_ref[...] = hb
        acc = jnp.zeros((tm, d), F32)
        for p in range(nb // 2):
            zs = []
            for q in range(2):
                blk = 2 * p + q
                ab = _dot_nt(hb, w13_ref[blk])
                ab_ref[:, blk * w:(blk + 1) * w] = ab.astype(BF16)
                a, b = ab[:, :fp], ab[:, fp:]
                zs.append((a * _sigmoid(a) * b).astype(BF16))
            acc = acc + _dot(jnp.concatenate(zs, axis=1), w2_ref[p])
        o_ref[...] = acc.astype(BF16)
        xo = xx + (0.5 * gate) * acc
        if head is None:
            xo_ref[...] = xo
        else:
            tgt_ref, fg_ref = refs[nx + 4:nx + 6]
            loss_ref, dfg_ref = refs[nx + 8 + nhead:]

            @pl.when(pl.program_id(0) == 0)
            def _():
                loss_ref[...] = jnp.zeros_like(loss_ref)
                dfg_ref[...] = jnp.zeros_like(dfg_ref)

            xo_ref[...] = _loss_head(xo, tgt_ref[...], fg_ref[...], loss_ref, dfg_ref)

    tile = _tile_of(t // tm, has_ctx)
    row = lambda i: (tile(i), 0)
    head_in = [] if head is None else [pl.BlockSpec((tm, d), row), _const((1, d))]
    head_shape = () if head is None else (jax.ShapeDtypeStruct((1, LANE), F32), jax.ShapeDtypeStruct((1, d), F32))
    head_out = () if head is None else (_const((1, LANE)), _const((1, d)))
    return _pcall(
        body, name=name, grid=(t // tm,), xchg=xchg, edges=_edges1(t // tm),
        out_shape=(jax.ShapeDtypeStruct((t, d), F32), jax.ShapeDtypeStruct((t, d), BF16),
                   jax.ShapeDtypeStruct((t, nb * w), BF16), jax.ShapeDtypeStruct((t, d), BF16)) + head_shape,
        in_specs=_row_specs(nx, tm, d) + [_mod_spec(d, has_ctx), _const((1, d)),
                                          _resident(w13.shape), _resident(w2.shape)] + head_in,
        out_specs=(pl.BlockSpec((tm, d), row), pl.BlockSpec((tm, d), row), pl.BlockSpec((tm, nb * w), row),
                   pl.BlockSpec((tm, d), row)) + head_out,
        compiler_params=_params(("arbitrary",)),
    )(*x, modtab, g, w13, w2, *(head or ()))


def _ffn_bwd(dy, modtab, h, ab, w13, w2, k0, has_ctx, name, pair0=0, npair=None, xchg=None):
    t, d = dy.shape
    _, w, _ = w13.shape
    fp = w // 2
    npair = w13.shape[0] // 2 if npair is None else npair
    nb = 2 * npair
    tm = _tile(has_ctx)
    nt = t // tm

    def body(dy_ref, mod_ref, h_ref, ab_ref, w13_ref, w2_ref, dh_ref, d13_ref, d2_ref, acc13, acc2):
        i = pl.program_id(1)

        @pl.when(i == 0)
        def _():
            acc13[...] = jnp.zeros_like(acc13)
            acc2[...] = jnp.zeros_like(acc2)

        gate = mod_ref[0, k0 + 2:k0 + 3, :]
        do = (dy_ref[...] * (0.5 * gate)).astype(BF16)
        dz = _dot_nt(do, w2_ref[0])
        hb = h_ref[...]
        dh = jnp.zeros((tm, d), F32)
        zs = []
        for q in range(2):
            ab = ab_ref[:, q * w:(q + 1) * w].astype(F32)
            a, b = ab[:, :fp], ab[:, fp:]
            sg = _sigmoid(a)
            sa = a * sg
            dzq = dz[:, q * fp:(q + 1) * fp]
            da = dzq * b * (sg * (1.0 + a * (1.0 - sg)))
            db = dzq * sa
            dab = jnp.concatenate([da, db], axis=1).astype(BF16)
            dh = dh + _dot(dab, w13_ref[q])
            acc13[q] += _dot_tn(dab, hb)
            zs.append((sa * b).astype(BF16))
        acc2[...] += _dot_tn(jnp.concatenate(zs, axis=1), do)
        dh_ref[0] = dh.astype(BF16)

        @pl.when(i == nt - 1)
        def _():
            d13_ref[...] = acc13[...].astype(BF16)
            d2_ref[0] = acc2[...].astype(BF16)

    mod_spec = _mod_spec(d, has_ctx)
    tile = _tile_of(nt, has_ctx)
    return _pcall(
        body, name=name, grid=(npair, nt), xchg=xchg, edges=_edges2(npair, nt),
        out_shape=(jax.ShapeDtypeStruct((npair, t, d), BF16), jax.ShapeDtypeStruct((nb, w, d), BF16),
                   jax.ShapeDtypeStruct((npair, 2 * fp, d), BF16)),
        in_specs=[pl.BlockSpec((tm, d), lambda p, i: (tile(i), 0)), mod_spec,
                  pl.BlockSpec((tm, d), lambda p, i: (tile(i), 0)),
                  pl.BlockSpec((tm, 2 * w), lambda p, i: (tile(i), pair0 + p)),
                  pl.BlockSpec((2, w, d), lambda p, i: (pair0 + p, 0, 0)),
                  pl.BlockSpec((1, 2 * fp, d), lambda p, i: (pair0 + p, 0, 0))],
        out_specs=(pl.BlockSpec((1, tm, d), lambda p, i: (p, tile(i), 0)),
                   pl.BlockSpec((2, w, d), lambda p, i: (p, 0, 0)),
                   pl.BlockSpec((1, 2 * fp, d), lambda p, i: (p, 0, 0))),
        scratch_shapes=[pltpu.VMEM((2, w, d), F32), pltpu.VMEM((2 * fp, d), F32)],
        compiler_params=_params(("arbitrary", "arbitrary")),
    )(dy, modtab, h, ab, w13, w2)


def _ffn_bwd_norm(dy, dhps, x, o, modtab, g, k0, has_ctx, name, xchg=None):
    t, d = dy.shape
    ngrp = 2 if has_ctx else 1
    tm = _tile(has_ctx)
    ndh, nx = len(dhps), len(x)
    lat0 = ngrp - 1

    def body(dy_ref, *rest):
        dhp_refs, x_refs = rest[:ndh], rest[ndh:ndh + nx]
        o_ref, mod_ref, g_ref, dx_ref, dmod_ref, dg_ref = rest[ndh + nx:]
        i = pl.program_id(0)

        @pl.when(i == 0)
        def _():
            dg_ref[...] = jnp.zeros_like(dg_ref)

        @pl.when((i == 0) | (i == ngrp - 1))
        def _():
            dmod_ref[...] = jnp.zeros_like(dmod_ref)

        dh = None
        for ref in dhp_refs:
            for p in range(ref.shape[0]):
                dh = ref[p].astype(F32) if dh is None else dh + ref[p].astype(F32)
        scale = mod_ref[0, k0 + 1:k0 + 2, :]
        gg = g_ref[...]
        rstd, xhat, xn, _ = _norm_fwd(_rows(x_refs), gg, 0.0, scale)
        dxn, dshift, dscale, dg = _norm_bwd(dh, rstd, xhat, xn, gg, scale)
        dyv = dy_ref[...]
        dx_ref[...] = dyv + dxn
        dmod_ref[0, 0:1, :] += dshift
        dmod_ref[0, 1:2, :] += dscale
        dmod_ref[0, 2:3, :] += _colsum(0.5 * dyv * o_ref[...].astype(F32))
        dg_ref[...] += dg

    tile = _tile_of(t // tm, has_ctx)
    row = lambda i: (tile(i), 0)
    return _pcall(
        body, name=name, grid=(t // tm,), xchg=xchg, edges=_edges1(t // tm),
        out_shape=(jax.ShapeDtypeStruct((t - lat0 * tm, d), F32), jax.ShapeDtypeStruct((ngrp, 3, d), F32),
                   jax.ShapeDtypeStruct((1, d), F32)),
        in_specs=[pl.BlockSpec((tm, d), row)]
        + [pl.BlockSpec((a.shape[0], tm, d), lambda i: (0, tile(i), 0)) for a in dhps]
        + _row_specs(nx, tm, d) + [pl.BlockSpec((tm, d), row), _mod_spec(d, has_ctx), _const((1, d))],
        out_specs=(pl.BlockSpec((tm, d), lambda i: (jnp.maximum(i - lat0, 0), 0)), _dmod_spec(3, d, has_ctx),
                   _const((1, d))),
        compiler_params=_params(("arbitrary",)),
    )(dy, *dhps, *x, o, modtab, g)


def _mix_in_fwd(x1, modtab, g, w_in):
    t, d = x1.shape
    hm = w_in.shape[1] // 2

    def body(x_ref, mod_ref, g_ref, w_ref, h_ref, up_ref, us_ref):
        _, _, _, h = _norm_fwd(x_ref[...], g_ref[...], mod_ref[0, 3:4, :], mod_ref[0, 4:5, :])
        hb = h.astype(BF16)
        h_ref[...] = hb
        u = _dot(hb, w_ref[...])
        up_ref[...] = u[:, :hm]
        us_ref[...] = u[:, hm:]

    tile = _tile_of(t // TM, True)
    row = lambda i: (tile(i), 0)
    return _pcall(
        body, name="mix_in_fwd", grid=(t // TM,),
        out_shape=(jax.ShapeDtypeStruct((t, d), BF16), jax.ShapeDtypeStruct((t, hm), F32),
                   jax.ShapeDtypeStruct((t, hm), F32)),
        in_specs=[pl.BlockSpec((TM, d), row), _mod_spec(d, True), _const((1, d)), _resident(w_in.shape)],
        out_specs=(pl.BlockSpec((TM, d), row), pl.BlockSpec((TM, hm), row), pl.BlockSpec((TM, hm), row)),
        compiler_params=_params(("arbitrary",)),
    )(x1, modtab, g, w_in)


def _mix_in_bwd(du_pool, du_f, du_b, dx2, h1, x1, modtab, g, w_in, lc):
    t, d = x1.shape
    m = w_in.shape[1]
    hm = m // 2
    nt = t // TM

    def body(dup_ref, duf_ref, dub_ref, dx2_ref, h_ref, x_ref, mod_ref, g_ref, w_ref, dx_ref, dw_ref, dmod_ref, dg_ref, acc):
        i = pl.program_id(0)

        @pl.when(i == 0)
        def _():
            acc[...] = jnp.zeros_like(acc)
            dg_ref[...] = jnp.zeros_like(dg_ref)

        @pl.when(i <= 1)
        def _():
            dmod_ref[...] = jnp.zeros_like(dmod_ref)

        lat = (i > 0).astype(F32)
        valid = (i > 0) | (lax.broadcasted_iota(jnp.int32, (TM, 1), 0) < lc)
        du_s5 = jnp.where(valid, duf_ref[...] + dub_ref[...], 0.0)
        du = jnp.concatenate([dup_ref[...] * lat, du_s5], axis=1).astype(BF16)
        dh = _dot_nt(du, w_ref[...])
        acc[...] += _dot_tn(h_ref[...], du)
        scale = mod_ref[0, 4:5, :]
        gg = g_ref[...]
        rstd, xhat, xn, _ = _norm_fwd(x_ref[...], gg, 0.0, scale)
        dxn, dshift, dscale, dg = _norm_bwd(dh, rstd, xhat, xn, gg, scale)
        dx_ref[...] = dx2_ref[...] * lat + dxn
        dmod_ref[0, 0:1, :] += dshift
        dmod_ref[0, 1:2, :] += dscale
        dg_ref[...] += dg

        @pl.when(i == nt - 1)
        def _():
            dw_ref[...] = acc[...].astype(BF16)

    tile = _tile_of(nt, True)
    row = lambda i: (tile(i), 0)
    lrow = lambda i: (jnp.maximum(i - 1, 0), 0)
    return _pcall(
        body, name="mix_in_bwd", grid=(nt,),
        out_shape=(jax.ShapeDtypeStruct((t, d), F32), jax.ShapeDtypeStruct((d, m), BF16),
                   jax.ShapeDtypeStruct((2, 2, d), F32), jax.ShapeDtypeStruct((1, d), F32)),
        in_specs=[pl.BlockSpec((TM, hm), lrow), pl.BlockSpec((TM, hm), row), pl.BlockSpec((TM, hm), row),
                  pl.BlockSpec((TM, d), lrow), pl.BlockSpec((TM, d), row), pl.BlockSpec((TM, d), row),
                  _mod_spec(d, True), _const((1, d)), _resident(w_in.shape)],
        out_specs=(pl.BlockSpec((TM, d), row), _const((d, m)), _dmod_spec(2, d, True), _const((1, d))),
        scratch_shapes=[pltpu.VMEM((d, m), F32)],
        compiler_params=_params(("arbitrary",)),
    )(du_pool, du_f, du_b, dx2, h1, x1, modtab, g, w_in)


def _pool(v, rows, transpose, name):
    l, n = rows * GRID_W, GRID_W * LANE
    ngrp = v.shape[1] // LANE
    nchunk = 4
    cw = n // nchunk

    def rowsum(val, lo, hi):
        ri = lax.broadcasted_iota(jnp.int32, (rows, rows), 0)
        ci = lax.broadcasted_iota(jnp.int32, (rows, rows), 1)
        sel = (((ci - ri) >= -lo) & ((ci - ri) <= hi)).astype(BF16)
        v1 = val.astype(BF16)
        r1 = val - v1.astype(F32)
        v2 = r1.astype(BF16)
        v3 = (r1 - v2.astype(F32)).astype(BF16)
        return _dot(sel, v1) + _dot(sel, v2) + _dot(sel, v3)

    def one(v_ref, o_ref, g_scr, r_scr, a_scr, win):
        for c in range(GRID_W):
            g_scr[:, c * LANE:(c + 1) * LANE] = v_ref[pl.ds(c, rows, stride=GRID_W), :]
        lo = win // 2
        hi = win - 1 - lo
        rlo, rhi = (hi, lo) if transpose else (lo, hi)
        ridx = lax.broadcasted_iota(jnp.int32, (rows, 1), 0)
        cnt_r = (jnp.minimum(ridx + hi + 1, rows) - jnp.maximum(ridx - lo, 0)).astype(F32)
        cidx = lax.broadcasted_iota(jnp.int32, (1, n), 1) // LANE
        cnt_c = (jnp.minimum(cidx + hi + 1, GRID_W) - jnp.maximum(cidx - lo, 0)).astype(F32)
        if not transpose:
            for k in range(nchunk):
                sl = slice(k * cw, (k + 1) * cw)
                r_scr[:, sl] = rowsum(g_scr[:, sl], rlo, rhi) / cnt_r
        else:
            r_scr[...] = g_scr[...] / cnt_c
        a_scr[...] = r_scr[...]
        for j in range(-rlo, rhi + 1):
            if j == 0:
                continue
            c0, c1 = max(0, -j), min(GRID_W, GRID_W - j)
            a_scr[:, c0 * LANE:c1 * LANE] += r_scr[:, (c0 + j) * LANE:(c1 + j) * LANE]
        if not transpose:
            r_scr[...] = a_scr[...] / cnt_c - g_scr[...]
        else:
            for k in range(nchunk):
                sl = slice(k * cw, (k + 1) * cw)
                r_scr[:, sl] = rowsum(a_scr[:, sl] / cnt_r, rlo, rhi) - g_scr[:, sl]
        for c in range(GRID_W):
            o_ref[pl.ds(c, rows, stride=GRID_W), :] = r_scr[:, c * LANE:(c + 1) * LANE]

    def body(v_ref, o_ref, g_scr, r_scr, a_scr):
        grp = pl.program_id(0)
        for k, win in enumerate(POOL_WINDOWS):
            @pl.when(grp == k)
            def _(win=win):
                one(v_ref, o_ref, g_scr, r_scr, a_scr, win)

    spec = pl.BlockSpec((l, LANE), lambda k: (0, k))
    return _pcall(
        body, name=name, grid=(ngrp,), out_shape=jax.ShapeDtypeStruct((l, ngrp * LANE), F32),
        in_specs=[spec], out_specs=spec,
        scratch_shapes=[pltpu.VMEM((rows, n), F32), pltpu.VMEM((rows, n), F32), pltpu.VMEM((rows, n), F32)],
        compiler_params=_params(("arbitrary",)),
    )(v)


def _cmul(ar, ai, br, bi):
    return ar * br - ai * bi, ar * bi + ai * br


def _s5_prep(a_re, a_im, log_dt, c_re, c_im, kvec):
    q, nc, p = c_re.shape

    def body(ar_ref, ai_ref, ldt_ref, cr_ref, ci_ref, k_ref, cpr_ref, cpi_ref, tr_ref, ti_ref):
        lr, li = ar_ref[...], ai_ref[...]
        dt = jnp.exp(ldt_ref[...])
        kk = k_ref[...]
        mag = jnp.exp(kk * (lr * dt))
        ph = kk * (li * dt)
        tr_ref[...] = mag * jnp.cos(ph)
        ti_ref[...] = mag * jnp.sin(ph)
        m1 = jnp.exp(lr * dt)
        abr, abi = m1 * jnp.cos(li * dt), m1 * jnp.sin(li * dt)
        den = lr * lr + li * li
        xr, xi = abr - 1.0, abi
        cfr, cfi = (xr * lr + xi * li) / den, (xi * lr - xr * li) / den
        pr, pi_ = _cmul(cr_ref[...], ci_ref[...], cfr, cfi)
        cpr_ref[...] = pr
        cpi_ref[...] = pi_

    return _pcall(
        body, name="s5_prep",
        out_shape=(jax.ShapeDtypeStruct((q, nc, p), F32), jax.ShapeDtypeStruct((q, nc, p), F32),
                   jax.ShapeDtypeStruct((q, NTAB, p), F32), jax.ShapeDtypeStruct((q, NTAB, p), F32)),
        compiler_params=_params(None),
    )(a_re, a_im, log_dt, c_re, c_im, kvec)


def _s5_param_bwd(a_re, a_im, log_dt, c_re, c_im, dab_r, dab_i, dcp_r, dcp_i):
    q, nc, p = c_re.shape

    def body(ar_ref, ai_ref, ldt_ref, cr_ref, ci_ref, dar_ref, dai_ref, dcr_ref, dci_ref,
             gar_ref, gai_ref, gdt_ref, gcr_ref, gci_ref):
        lr, li = ar_ref[...], ai_ref[...]
        dt = jnp.exp(ldt_ref[...])
        m1 = jnp.exp(lr * dt)
        abr, abi = m1 * jnp.cos(li * dt), m1 * jnp.sin(li * dt)
        den = lr * lr + li * li
        xr, xi = abr - 1.0, abi
        cfr, cfi = (xr * lr + xi * li) / den, (xi * lr - xr * li) / den
        cr, ci = cr_ref[...], ci_ref[...]
        dcr, dci = dcr_ref[...], dci_ref[...]
        gcr, gci = _cmul(dcr, dci, cfr, -cfi)
        gcr_ref[...] = gcr
        gci_ref[...] = gci
        t_r, t_i = _cmul(cr, -ci, dcr, dci)
        dcf_r = jnp.sum(t_r, axis=1, keepdims=True)
        dcf_i = jnp.sum(t_i, axis=1, keepdims=True)
        ilr, ili = lr / den, -li / den
        u_r, u_i = _cmul(dcf_r, dcf_i, ilr, -ili)
        dab_r_, dab_i_ = dar_ref[...] + u_r, dai_ref[...] + u_i
        v_r, v_i = _cmul(dab_r_, dab_i_, abr, -abi)
        cl_r, cl_i = _cmul(cfr, cfi, ilr, ili)
        w_r, w_i = _cmul(dcf_r, dcf_i, cl_r, -cl_i)
        gar_ref[...] = v_r * dt - w_r
        gai_ref[...] = v_i * dt - w_i
        la_r, la_i = _cmul(lr, li, abr, abi)
        ddt = la_r * dab_r_ + la_i * dab_i_
        gdt_ref[...] = dt * jnp.sum(ddt, axis=2, keepdims=True)

    return _pcall(
        body, name="s5_param_bwd",
        out_shape=(jax.ShapeDtypeStruct((q, 1, p), F32), jax.ShapeDtypeStruct((q, 1, p), F32),
                   jax.ShapeDtypeStruct((q, 1, p), F32), jax.ShapeDtypeStruct((q, nc, p), F32),
                   jax.ShapeDtypeStruct((q, nc, p), F32)),
        compiler_params=_params(None),
    )(a_re, a_im, log_dt, c_re, c_im, dab_r, dab_i, dcp_r, dcp_i)


def _bcast8(row, c):
    return jnp.broadcast_to(row, (8, c))


def _slab(j):
    return j * 8 if isinstance(j, int) else pl.multiple_of(j * 8, 8)


def _scan_head(xr, xi, tre_ref, tim_ref, car_r, car_i, desc, conj, cs):
    sg = -1.0 if conj else 1.0
    ar = _bcast8(tre_ref[0:1, :], cs)
    ai = sg * _bcast8(tim_ref[0:1, :], cs)

    def p1(jj, carry):
        hr, hi = carry
        off = _slab(SEG - 1 - jj if desc else jj)
        pr, pi_ = _cmul(ar, ai, hr, hi)
        nr = pr + xr[pl.ds(off, 8), :]
        ni = pi_ + xi[pl.ds(off, 8), :]
        xr[pl.ds(off, 8), :] = nr
        xi[pl.ds(off, 8), :] = ni
        return nr, ni

    zero = jnp.zeros((8, cs), F32)
    ir, ii = lax.fori_loop(0, SEG, p1, (zero, zero), unroll=S5_UNROLL)
    cin_r, cin_i = car_r[...], car_i[...]
    rows = lax.broadcasted_iota(jnp.int32, (8, cs), 0)
    for s, krow in ((1, SEG), (2, SEG + 1), (4, SEG + 3)):
        keep, sh = (rows < 8 - s, 8 - s) if desc else (rows >= s, s)
        sr = jnp.where(keep, pltpu.roll(ir, sh, 0), 0.0)
        si = jnp.where(keep, pltpu.roll(ii, sh, 0), 0.0)
        pr, pi_ = _cmul(tre_ref[krow:krow + 1, :], sg * tim_ref[krow:krow + 1, :], sr, si)
        ir, ii = ir + pr, ii + pi_
    q0 = SEG + 8 if desc else SEG
    pr, pi_ = _cmul(tre_ref[q0:q0 + 8, :], sg * tim_ref[q0:q0 + 8, :], cin_r, cin_i)
    fr, fi = ir + pr, ii + pi_
    keep, sh, edge = (rows < 7, 7, 0) if desc else (rows >= 1, 1, 7)
    cs_r = jnp.where(keep, pltpu.roll(fr, sh, 0), cin_r)
    cs_i = jnp.where(keep, pltpu.roll(fi, sh, 0), cin_i)
    car_r[...] = _bcast8(fr[edge:edge + 1, :], cs)
    car_i[...] = _bcast8(fi[edge:edge + 1, :], cs)
    return cs_r, cs_i, cin_r, cin_i


def _pow_row(tre_ref, tim_ref, j, desc, conj, cs):
    k = SEG - 1 - j if desc else j
    sg = -1.0 if conj else 1.0
    return _bcast8(tre_ref[pl.ds(k, 1), :], cs), sg * _bcast8(tim_ref[pl.ds(k, 1), :], cs)


def _to_segments(val, dst):
    for r in range(8):
        dst[pl.ds(r, SEG, stride=8), :] = val[r * SEG:(r + 1) * SEG, :]


def _from_segments(src):
    return jnp.concatenate([src[pl.ds(r, SEG, stride=8), :] for r in range(8)], axis=0)


def _s5_fwd(up, bre, bim, cmr, cmi, tre, tim, dskip, order, nproc, desc, with_skip, name):
    t, cu = up.shape
    nsb = bre.shape[0]
    cb = cu // nsb
    cs = bre.shape[2]
    nch = t // TC

    def body(u_ref, bre_ref, bim_ref, cmr_ref, cmi_ref, tre_ref, tim_ref, dsk_ref,
             y_ref, hr_ref, hi_ref, cinr_ref, cini_ref, car_r, car_i, seg_scr):
        i = pl.program_id(0)

        @pl.when(i == 0)
        def _():
            car_r[...] = jnp.zeros_like(car_r)
            car_i[...] = jnp.zeros_like(car_i)

        for sb in range(nsb):
            col, ch = pl.ds(sb * cs, cs), slice(sb * cb, (sb + 1) * cb)
            hr_v, hi_v = hr_ref.at[:, col], hi_ref.at[:, col]
            tre_v, tim_v = tre_ref.at[:, col], tim_ref.at[:, col]
            useg, yseg = seg_scr.at[0, sb], seg_scr.at[1, sb]
            _to_segments(u_ref[:, ch], useg)
            u = useg[...]
            ub = u.astype(BF16)
            hr_v[...] = _dot(ub, bre_ref[sb])
            hi_v[...] = _dot(ub, bim_ref[sb])
            cs_r, cs_i, cin_r, cin_i = _scan_head(hr_v, hi_v, tre_v, tim_v, car_r.at[sb], car_i.at[sb], desc, False, cs)
            cinr_ref[:, col] = cin_r
            cini_ref[:, col] = cin_i

            def p2(j, _, hr_v=hr_v, hi_v=hi_v, tre_v=tre_v, tim_v=tim_v, cs_r=cs_r, cs_i=cs_i):
                off = _slab(j)
                pw_r, pw_i = _pow_row(tre_v, tim_v, j, desc, False, cs)
                pr, pi_ = _cmul(pw_r, pw_i, cs_r, cs_i)
                hr_v[pl.ds(off, 8), :] = hr_v[pl.ds(off, 8), :] + pr
                hi_v[pl.ds(off, 8), :] = hi_v[pl.ds(off, 8), :] + pi_
                return 0

            lax.fori_loop(0, SEG, p2, 0, unroll=S5_UNROLL)
            y = _dot(hr_v[...].astype(BF16), cmr_ref[sb]) - _dot(hi_v[...].astype(BF16), cmi_ref[sb])
            if with_skip:
                y = y + dsk_ref[:, ch] * u
            yseg[...] = y
            y_ref[:, ch] = _from_segments(yseg)

    blk = lambda i: (order(i), 0)
    return _pcall(
        body, name=name, grid=(nproc,),
        out_shape=(jax.ShapeDtypeStruct((t, cu), F32), jax.ShapeDtypeStruct((t, nsb * cs), F32),
                   jax.ShapeDtypeStruct((t, nsb * cs), F32), jax.ShapeDtypeStruct((nch * 8, nsb * cs), F32),
                   jax.ShapeDtypeStruct((nch * 8, nsb * cs), F32)),
        in_specs=[pl.BlockSpec((TC, cu), blk), _const(bre.shape), _const(bim.shape), _const(cmr.shape),
                  _const(cmi.shape), _const(tre.shape), _const(tim.shape), _const(dskip.shape)],
        out_specs=(pl.BlockSpec((TC, cu), blk), pl.BlockSpec((TC, nsb * cs), blk), pl.BlockSpec((TC, nsb * cs), blk),
                   pl.BlockSpec((8, nsb * cs), blk), pl.BlockSpec((8, nsb * cs), blk)),
        scratch_shapes=[pltpu.VMEM((nsb, 8, cs), F32), pltpu.VMEM((nsb, 8, cs), F32),
                        pltpu.VMEM((2, nsb, TC, cb), F32)],
        compiler_params=_params(("arbitrary",)),
    )(up, bre, bim, cmr, cmi, tre, tim, dskip)


def _s5_bwd(dy_lat, up, hr, hi, cinr, cini, bre, bim, ctr, cti, tre, tim, dskip, order, nproc, desc, with_skip, name,
            xchg=None):
    t, cu = up.shape
    nsb = bre.shape[0]
    cb = cu // nsb
    cs = bre.shape[2]
    lch = dy_lat.shape[0] // TC
    adesc = not desc

    def body(dy_ref, u_ref, hr_ref, hi_ref, cinr_ref, cini_ref, bre_ref, bim_ref, ctr_ref, cti_ref, tre_ref, tim_ref,
             dsk_ref, du_ref, dar_ref, dai_ref, dcr_ref, dci_ref, dbr_ref, dbi_ref, dd_ref,
             car_r, car_i, mr_all, mi_all, acc_r, acc_i, seg_scr):
        i = pl.program_id(0)
        latent = (order(i) < lch).astype(F32)

        @pl.when(i == 0)
        def _():
            for ref in (car_r, car_i, acc_r, acc_i, dcr_ref, dci_ref, dbr_ref, dbi_ref, dd_ref):
                ref[...] = jnp.zeros_like(ref)

        rows = lax.broadcasted_iota(jnp.int32, (8, cs), 0)
        for sb in range(nsb):
            col, ch = pl.ds(sb * cs, cs), slice(sb * cb, (sb + 1) * cb)
            hr_v, hi_v = hr_ref.at[:, col], hi_ref.at[:, col]
            tre_v, tim_v = tre_ref.at[:, col], tim_ref.at[:, col]
            mr, mi = mr_all.at[sb % 2], mi_all.at[sb % 2]
            useg, dyseg, duseg = seg_scr.at[0, sb], seg_scr.at[1, sb], seg_scr.at[2, sb]
            _to_segments(dy_ref[:, ch] * latent, dyseg)
            _to_segments(u_ref[:, ch], useg)
            dy = dyseg[...]
            dyb = dy.astype(BF16)
            u = useg[...]
            ub = u.astype(BF16)
            mr[...] = _dot(dyb, ctr_ref[sb])
            mi[...] = -_dot(dyb, cti_ref[sb])
            cs_r, cs_i, _, _ = _scan_head(mr, mi, tre_v, tim_v, car_r.at[sb], car_i.at[sb], adesc, True, cs)

            def fix(j, hp_r, hp_i, acc, mr=mr, mi=mi, tre_v=tre_v, tim_v=tim_v, cs_r=cs_r, cs_i=cs_i):
                off = _slab(j)
                pw_r, pw_i = _pow_row(tre_v, tim_v, j, adesc, True, cs)
                pr, pi_ = _cmul(pw_r, pw_i, cs_r, cs_i)
                m_r = mr[pl.ds(off, 8), :] + pr
                m_i = mi[pl.ds(off, 8), :] + pi_
                mr[pl.ds(off, 8), :] = m_r
                mi[pl.ds(off, 8), :] = m_i
                a_r, a_i = acc
                return a_r + hp_r * m_r + hp_i * m_i, a_i + hp_r * m_i - hp_i * m_r

            edge_j, src, keep, sh = (SEG - 1, 0, rows < 7, 7) if desc else (0, (SEG - 1) * 8, rows >= 1, 1)
            h0r = jnp.where(keep, pltpu.roll(hr_v[src:src + 8, :], sh, 0), cinr_ref[:, col])
            h0i = jnp.where(keep, pltpu.roll(hi_v[src:src + 8, :], sh, 0), cini_ref[:, col])
            acc = fix(edge_j, h0r, h0i, (acc_r[sb], acc_i[sb]))

            def p2(jj, acc, fix=fix, hr_v=hr_v, hi_v=hi_v):
                j = jj if desc else jj + 1
                offp = _slab(j + 1 if desc else j - 1)
                return fix(j, hr_v[pl.ds(offp, 8), :], hi_v[pl.ds(offp, 8), :], acc)

            a_r, a_i = lax.fori_loop(0, SEG - 1, p2, acc, unroll=S5_UNROLL)
            acc_r[sb] = a_r
            acc_i[sb] = a_i

            mrb, mib = mr[...].astype(BF16), mi[...].astype(BF16)
            du = _dot_nt(mrb, bre_ref[sb]) + _dot_nt(mib, bim_ref[sb])
            if with_skip:
                du = du + dsk_ref[:, ch] * dy
                dd_ref[:, ch] += _colsum(dy * u)
            duseg[...] = du
            du_ref[:, ch] = _from_segments(duseg)
            dbr_ref[sb] += _dot_tn(ub, mrb)
            dbi_ref[sb] += _dot_tn(ub, mib)
            dcr_ref[sb] += _dot_tn(dyb, hr_v[...].astype(BF16))
            dci_ref[sb] -= _dot_tn(dyb, hi_v[...].astype(BF16))

            @pl.when(i == nproc - 1)
            def _(sb=sb, a_r=a_r, a_i=a_i):
                dar_ref[sb] = _colsum(a_r)
                dai_ref[sb] = _colsum(a_i)

    blk = lambda i: (order(i), 0)
    blk_dy = lambda i: (jnp.minimum(order(i), lch - 1), 0)
    mat = (nsb, cb, cs)
    return _pcall(
        body, name=name, grid=(nproc,), xchg=xchg, edges=_edges1(nproc),
        out_shape=(jax.ShapeDtypeStruct((t, cu), F32),
                   jax.ShapeDtypeStruct((nsb, 1, cs), F32), jax.ShapeDtypeStruct((nsb, 1, cs), F32),
                   jax.ShapeDtypeStruct(mat, F32), jax.ShapeDtypeStruct(mat, F32),
                   jax.ShapeDtypeStruct(mat, F32), jax.ShapeDtypeStruct(mat, F32),
                   jax.ShapeDtypeStruct((1, cu), F32)),
        in_specs=[pl.BlockSpec((TC, cu), blk_dy), pl.BlockSpec((TC, cu), blk), pl.BlockSpec((TC, nsb * cs), blk),
                  pl.BlockSpec((TC, nsb * cs), blk), pl.BlockSpec((8, nsb * cs), blk), pl.BlockSpec((8, nsb * cs), blk),
                  _const(mat), _const(mat), _const(mat), _const(mat), _const(tre.shape), _const(tim.shape),
                  _const((1, cu))],
        out_specs=(pl.BlockSpec((TC, cu), blk), _const((nsb, 1, cs)), _const((nsb, 1, cs)),
                   _const(mat), _const(mat), _const(mat), _const(mat), _const((1, cu))),
        scratch_shapes=[pltpu.VMEM((nsb, 8, cs), F32), pltpu.VMEM((nsb, 8, cs), F32), pltpu.VMEM((2, TC, cs), F32),
                        pltpu.VMEM((2, TC, cs), F32), pltpu.VMEM((nsb, 8, cs), F32), pltpu.VMEM((nsb, 8, cs), F32),
                        pltpu.VMEM((3, nsb, TC, cb), F32)],
        compiler_params=_params(("arbitrary",)),
    )(dy_lat, up, hr, hi, cinr, cini, bre, bim, ctr, cti, tre, tim, dskip)


def _gelu(x):
    k = math.sqrt(2.0 / math.pi)
    return 0.5 * x * (1.0 + jnp.tanh(k * (x + 0.044715 * (x * x * x))))


def _gelu_grad(x):
    k = math.sqrt(2.0 / math.pi)
    th = jnp.tanh(k * (x + 0.044715 * (x * x * x)))
    return 0.5 * (1.0 + th) + 0.5 * x * (1.0 - th * th) * (k * (1.0 + 3.0 * 0.044715 * (x * x)))


def _mix_out_fwd(yf, yb, dpool, x1, modtab, w_pool, pscale, w_glu, w_out):
    l, hm = dpool.shape
    d = x1.shape[1]

    def body(yf_ref, yb_ref, dp_ref, x_ref, mod_ref, wp_ref, ps_ref, wg_ref, wo_ref, x2_ref, yp_ref, cat_ref, mix_ref):
        ypre = yf_ref[...] + yb_ref[...]
        yp_ref[...] = ypre
        yg = _gelu(ypre)
        y2 = yg * _sigmoid(_dot(yg.astype(BF16), wg_ref[...]))
        po = _dot(dp_ref[...].astype(BF16), wp_ref[...]) * ps_ref[...]
        cat = jnp.concatenate([po, y2], axis=1).astype(BF16)
        cat_ref[...] = cat
        mix = _dot(cat, wo_ref[...])
        mix_ref[...] = mix.astype(BF16)
        x2_ref[...] = x_ref[...] + mod_ref[0, 5:6, :] * mix

    row = lambda i: (i, 0)
    lrow = row
    return _pcall(
        body, name="mix_out_fwd", grid=(l // TM,),
        out_shape=(jax.ShapeDtypeStruct((l, d), F32), jax.ShapeDtypeStruct((l, hm), F32),
                   jax.ShapeDtypeStruct((l, d), BF16), jax.ShapeDtypeStruct((l, d), BF16)),
        in_specs=[pl.BlockSpec((TM, hm), lrow), pl.BlockSpec((TM, hm), lrow), pl.BlockSpec((TM, hm), row),
                  pl.BlockSpec((TM, d), lrow), _mod_spec(d, False), _const(w_pool.shape), _const((1, hm)),
                  _const(w_glu.shape), _resident(w_out.shape)],
        out_specs=(pl.BlockSpec((TM, d), row), pl.BlockSpec((TM, hm), row), pl.BlockSpec((TM, d), row),
                   pl.BlockSpec((TM, d), row)),
        compiler_params=_params(("arbitrary",)),
    )(yf, yb, dpool, x1, modtab, w_pool, pscale, w_glu, w_out)


def _mix_out_bwd(dx2, mix, cat, ypre, dpool, modtab, w_pool, pscale, w_glu, w_out):
    l, hm = dpool.shape
    d = dx2.shape[1]
    nt = l // TML

    def body(dx_ref, mix_ref, cat_ref, yp_ref, dp_ref, mod_ref, wp_ref, ps_ref, wg_ref, wo_ref,
             dyp_ref, ddp_ref, dwo_ref, dwg_ref, dwp_ref, dps_ref, dgate_ref, acc_o, acc_g):
        i = pl.program_id(0)

        @pl.when(i == 0)
        def _():
            for ref in (acc_o, acc_g, dwp_ref, dps_ref, dgate_ref):
                ref[...] = jnp.zeros_like(ref)

        dx = dx_ref[...]
        dgate_ref[...] += _colsum(dx * mix_ref[...].astype(F32))
        dmix = (dx * mod_ref[0, 5:6, :]).astype(BF16)
        dcat = _dot_nt(dmix, wo_ref[...])
        acc_o[...] += _dot_tn(cat_ref[...], dmix)
        dpo, dy2 = dcat[:, :hm], dcat[:, hm:]
        dpb = dp_ref[...].astype(BF16)
        pp = _dot(dpb, wp_ref[...])
        dps_ref[...] += _colsum(dpo * pp)
        dpp = (dpo * ps_ref[...]).astype(BF16)
        ddp_ref[...] = _dot_nt(dpp, wp_ref[...])
        dwp_ref[...] += _dot_tn(dpb, dpp)
        ypre = yp_ref[...]
        yg = _gelu(ypre)
        ygb = yg.astype(BF16)
        s = _sigmoid(_dot(ygb, wg_ref[...]))
        dq = (dy2 * yg * s * (1.0 - s)).astype(BF16)
        dyg = dy2 * s + _dot_nt(dq, wg_ref[...])
        acc_g[...] += _dot_tn(ygb, dq)
        dyp_ref[...] = dyg * _gelu_grad(ypre)

        @pl.when(i == nt - 1)
        def _():
            dwo_ref[...] = acc_o[...].astype(BF16)
            dwg_ref[...] = acc_g[...].astype(BF16)

    row = lambda i: (i, 0)
    return _pcall(
        body, name="mix_out_bwd", grid=(nt,),
        out_shape=(jax.ShapeDtypeStruct((l, hm), F32), jax.ShapeDtypeStruct((l, hm), F32),
                   jax.ShapeDtypeStruct((d, d), BF16), jax.ShapeDtypeStruct((hm, hm), BF16),
                   jax.ShapeDtypeStruct((hm, hm), F32), jax.ShapeDtypeStruct((1, hm), F32),
                   jax.ShapeDtypeStruct((1, d), F32)),
        in_specs=[pl.BlockSpec((TML, d), row), pl.BlockSpec((TML, d), row), pl.BlockSpec((TML, d), row),
                  pl.BlockSpec((TML, hm), row), pl.BlockSpec((TML, hm), row), _mod_spec(d, False),
                  _const(w_pool.shape), _const((1, hm)), _const(w_glu.shape), _resident(w_out.shape)],
        out_specs=(pl.BlockSpec((TML, hm), row), pl.BlockSpec((TML, hm), row), _const((d, d)), _const((hm, hm)),
                   _const((hm, hm)), _const((1, hm)), _const((1, d))),
        scratch_shapes=[pltpu.VMEM((d, d), F32), pltpu.VMEM((hm, hm), F32)],
        compiler_params=_params(("arbitrary",)),
    )(dx2, mix, cat, ypre, dpool, modtab, w_pool, pscale, w_glu, w_out)


def _loss_head(xx, target, gg, loss_ref, dg_ref):
    d = xx.shape[-1]
    rstd = lax.rsqrt(jnp.mean(xx * xx, axis=-1, keepdims=True) + EPS)
    xhat = xx * rstd
    err = xhat * gg - target
    row_loss = jnp.mean(err * err, axis=-1, keepdims=True)
    loss_ref[...] += 0.5 * jnp.sum(row_loss, axis=0, keepdims=True)
    dout = err * (1.0 / d)
    dg_ref[...] += _colsum(dout * xhat)
    dxhat = dout * gg
    return rstd * (dxhat - xhat * jnp.mean(dxhat * xhat, axis=-1, keepdims=True))


def _adamw(parts, w, m, v, name):
    shape = w.shape
    c = shape[-1]
    r = int(np.prod(shape)) // c
    npart = parts.shape[0]
    tr = r
    for cand in (1024, 512, 256, 128, 64, 32, 16):
        if r % cand == 0 and cand * max(c, LANE) * 4 <= ADAM_BLOCK_BYTES:
            tr = cand
            break
    c1 = 1.0 - ADAM_B1
    c2 = 1.0 - ADAM_B2
    bc1 = 1.0 - ADAM_B1 ** ADAM_STEP
    bc2 = 1.0 - ADAM_B2 ** ADAM_STEP

    def body(p_ref, w_ref, m_ref, v_ref, g_ref, d_ref, mo_ref, vo_ref):
        g = p_ref[0].astype(F32)
        for k in range(1, npart):
            g = g + p_ref[k].astype(F32)
        mn = ADAM_B1 * m_ref[...] + c1 * g
        vn = ADAM_B2 * v_ref[...] + c2 * (g * g)
        m_hat = mn / bc1
        v_hat = vn / bc2
        g_ref[...] = g
        mo_ref[...] = mn
        vo_ref[...] = vn
        d_ref[...] = -ADAM_LR * (m_hat / (jnp.sqrt(v_hat) + ADAM_EPS) + ADAM_WD * w_ref[...])

    spec = pl.BlockSpec((tr, c), lambda i: (i, 0))
    outs = _pcall(
        body, name=name, grid=(r // tr,),
        out_shape=tuple(jax.ShapeDtypeStruct((r, c), F32) for _ in range(4)),
        in_specs=[pl.BlockSpec((npart, tr, c), lambda i: (0, i, 0)), spec, spec, spec],
        out_specs=(spec, spec, spec, spec),
        compiler_params=_params(("arbitrary",)),
    )(parts.reshape(npart, r, c), w.reshape(r, c), m.reshape(r, c), v.reshape(r, c))
    return tuple(o.reshape(shape) for o in outs)


def _adamw_ffn(parts, row0, w, m, v, name):
    _, nl, r, c = w.shape
    rp = parts[0][0].shape[1]
    tc = ADAM_FFN_LANES
    grid = (c // tc,)
    p_spec = lambda a: pl.BlockSpec((a.shape[0], rp, tc), lambda i: (0, 0, i))
    w_spec = pl.BlockSpec((1, nl, r, tc), lambda i: (0, 0, 0, i))
    take = lambda ref, k: ref[k, row0:row0 + r, :]
    flat = [a for layer in parts for a in layer]
    counts = [len(layer) for layer in parts]
    c1 = 1.0 - ADAM_B1
    c2 = 1.0 - ADAM_B2
    bc1 = 1.0 - ADAM_B1 ** ADAM_STEP
    bc2 = 1.0 - ADAM_B2 ** ADAM_STEP

    def body(*refs):
        w_ref, m_ref, v_ref, g_ref, d_ref, mo_ref, vo_ref = refs[len(flat):]
        first = 0
        for lyr in range(nl):
            g = None
            for ref in refs[first:first + counts[lyr]]:
                for k in range(ref.shape[0]):
                    term = take(ref, k).astype(F32)
                    g = term if g is None else g + term
            first += counts[lyr]
            mn = ADAM_B1 * m_ref[0, lyr] + c1 * g
            vn = ADAM_B2 * v_ref[0, lyr] + c2 * (g * g)
            g_ref[0, lyr] = g
            mo_ref[0, lyr] = mn
            vo_ref[0, lyr] = vn
            d_ref[0, lyr] = -ADAM_LR * ((mn / bc1) / (jnp.sqrt(vn / bc2) + ADAM_EPS) + ADAM_WD * w_ref[0, lyr])

    return _pcall(
        body, name=name, grid=grid, out_shape=tuple(jax.ShapeDtypeStruct(w.shape, F32) for _ in range(4)),
        in_specs=[p_spec(a) for a in flat] + [w_spec] * 3, out_specs=(w_spec,) * 4,
        compiler_params=_params(("arbitrary",)),
    )(*flat, w, m, v)


def _blockdiag(xb):
    n, a, b = xb.shape[-3:]
    eye = jnp.eye(n, dtype=xb.dtype)
    out = xb[..., :, :, None, :] * eye[:, None, :, None]
    return out.reshape(xb.shape[:-3] + (n * a, n * b))


def _diagblocks(mat, n):
    lead = mat.shape[:-2]
    a, b = mat.shape[-2] // n, mat.shape[-1] // n
    m5 = mat.reshape(lead + (n, a, n, b))
    nl = len(lead)
    dg = jnp.diagonal(m5, axis1=nl, axis2=nl + 2)
    return jnp.moveaxis(dg, -1, nl)


def _pad128(a):
    flat = a.reshape(-1)
    pad = (-flat.shape[0]) % LANE
    return jnp.pad(flat, (0, pad)) if pad else flat


def kernel(x, c, ctx, c_ctx, norm_g, w_ada, b_ada, ffn_w1, ffn_w3, ffn_w2, w_in, pool_w, pool_scale, s5_a_re, s5_a_im, s5_log_dt, s5_b_re, s5_b_im, s5_c_re, s5_c_im, s5_d, s5_w_glu, w_out, final_g, loss_target, m_c_ctx, m_norm_g, m_w_ada, m_b_ada, m_ffn_w1, m_ffn_w3, m_ffn_w2, m_w_in, m_pool_w, m_pool_scale, m_s5_a_re, m_s5_a_im, m_s5_log_dt, m_s5_b_re, m_s5_b_im, m_s5_c_re, m_s5_c_im, m_s5_d, m_s5_w_glu, m_w_out, m_final_g, v_c_ctx, v_norm_g, v_w_ada, v_b_ada, v_ffn_w1, v_ffn_w3, v_ffn_w2, v_w_in, v_pool_w, v_pool_scale, v_s5_a_re, v_s5_a_im, v_s5_log_dt, v_s5_b_re, v_s5_b_im, v_s5_c_re, v_s5_c_im, v_s5_d, v_s5_w_glu, v_w_out, v_final_g):
    weights = dict(c_ctx=c_ctx, norm_g=norm_g, w_ada=w_ada, b_ada=b_ada, ffn_w1=ffn_w1, ffn_w3=ffn_w3, ffn_w2=ffn_w2,
                   w_in=w_in, pool_w=pool_w, pool_scale=pool_scale, s5_a_re=s5_a_re, s5_a_im=s5_a_im,
                   s5_log_dt=s5_log_dt, s5_b_re=s5_b_re, s5_b_im=s5_b_im, s5_c_re=s5_c_re, s5_c_im=s5_c_im, s5_d=s5_d,
                   s5_w_glu=s5_w_glu, w_out=w_out, final_g=final_g)
    mom_m = dict(c_ctx=m_c_ctx, norm_g=m_norm_g, w_ada=m_w_ada, b_ada=m_b_ada, ffn_w1=m_ffn_w1, ffn_w3=m_ffn_w3,
                 ffn_w2=m_ffn_w2, w_in=m_w_in, pool_w=m_pool_w, pool_scale=m_pool_scale, s5_a_re=m_s5_a_re,
                 s5_a_im=m_s5_a_im, s5_log_dt=m_s5_log_dt, s5_b_re=m_s5_b_re, s5_b_im=m_s5_b_im, s5_c_re=m_s5_c_re,
                 s5_c_im=m_s5_c_im, s5_d=m_s5_d, s5_w_glu=m_s5_w_glu, w_out=m_w_out, final_g=m_final_g)
    mom_v = dict(c_ctx=v_c_ctx, norm_g=v_norm_g, w_ada=v_w_ada, b_ada=v_b_ada, ffn_w1=v_ffn_w1, ffn_w3=v_ffn_w3,
                 ffn_w2=v_ffn_w2, w_in=v_w_in, pool_w=v_pool_w, pool_scale=v_pool_scale, s5_a_re=v_s5_a_re,
                 s5_a_im=v_s5_a_im, s5_log_dt=v_s5_log_dt, s5_b_re=v_s5_b_re, s5_b_im=v_s5_b_im, s5_c_re=v_s5_c_re,
                 s5_c_im=v_s5_c_im, s5_d=v_s5_d, s5_w_glu=v_s5_w_glu, w_out=v_w_out, final_g=v_final_g)

    l, d = x.shape[1], x.shape[2]
    lc = ctx.shape[1]
    t = l + TM
    rows = l // GRID_W
    fblk = ffn_w1.shape[-1]
    ngrp, gp = s5_a_re.shape[2], s5_a_re.shape[3]
    gc = s5_b_re.shape[3]
    hm = ngrp * gc
    nsb = 4
    gsb = ngrp // nsb
    assert lc == TC and TM % TC == 0 and l % TM == 0 and TM == TML and hm == 4 * LANE and ngrp * gp == nsb * 512
    me = 4 * lax.axis_index("x") + 2 * lax.axis_index("y") + lax.axis_index("c")

    padr = ((0, FPAD - fblk), (0, 0))
    w1_t, w3_t = jnp.swapaxes(ffn_w1, 2, 3), jnp.swapaxes(ffn_w3, 2, 3)
    w13_loc = [jnp.concatenate([jnp.pad(w1_t[0, k], padr), jnp.pad(w3_t[0, k], padr)], axis=0).astype(BF16)
               for k in range(2)]
    w2_loc = [jnp.pad(ffn_w2[0, k], padr).astype(BF16) for k in range(2)]
    wa = w_ada.shape[-1]
    b_blk = lax.dynamic_slice_in_dim(b_ada, me * wa, wa, axis=1)
    w13_0, w2_0, g_all, c_all, mod_all = _front(w13_loc[0], w2_loc[0], norm_g[0], c, c_ctx.reshape(1, d), w_ada[0],
                                                b_blk)
    w2_0 = w2_0.reshape(NDEV // 2, 2 * FPAD, d)
    c_all = c_all.reshape(NDEV, d)
    g_all = g_all.transpose(1, 0, 2).reshape(3, d)

    c16 = jnp.concatenate([c_all, c_ctx.reshape(1, d), jnp.zeros((7, d), F32)], axis=0)
    mod_full = mod_all.transpose(1, 0, 2).reshape(16, NDEV * wa)
    mod_l = lax.dynamic_index_in_dim(mod_full, me, axis=0, keepdims=False).reshape(9, d)
    modtab = jnp.stack([mod_full[8].reshape(9, d), mod_l])

    x0 = [jnp.pad(ctx[0], ((0, TM - lc), (0, 0))), x[0]]
    g0, g1, g2 = g_all[0:1], g_all[1:2], g_all[2:3]
    x1, h0, ab0, o0, w13_1, w2_1, w_in_f, w_out_f, w_glu_f = _ffn_fwd(
        x0, modtab, g0, w13_0, w2_0, 0, True, "ffn0_fwd",
        xchg=[(w13_loc[1], GATHER_REL), (w2_loc[1], GATHER_REL), (w_in[0].astype(BF16), GATHER),
              (w_out[0].astype(BF16), GATHER), (s5_w_glu[0].astype(BF16), GATHER)])
    w2_1 = w2_1.reshape(NDEV // 2, 2 * FPAD, d)
    w_in_f = w_in_f.reshape(d, -1)
    w_out_f = w_out_f.reshape(-1, d)
    w_glu_f = w_glu_f.reshape(hm, hm)
    h1, u_pool, up = _mix_in_fwd(x1, modtab, g1, w_in_f)

    dpool = _pool(u_pool, rows, False, "pool_fwd")
    w_pool_bd = _blockdiag(pool_w[0]).astype(BF16)

    q = 2 * ngrp
    kvec = np.concatenate([np.arange(1, SEG + 1), SEG * np.arange(1, 9), SEG * np.arange(8, 0, -1)]).astype(np.float32)
    a_re3, a_im3 = s5_a_re[0].reshape(q, 1, gp), s5_a_im[0].reshape(q, 1, gp)
    ldt3 = jnp.broadcast_to(s5_log_dt[0].reshape(q, 1, 1), (q, 1, gp))
    c_re3, c_im3 = s5_c_re[0].reshape(q, gc, gp), s5_c_im[0].reshape(q, gc, gp)
    cp_re, cp_im, tab_re, tab_im = _s5_prep(a_re3, a_im3, ldt3, c_re3, c_im3, jnp.broadcast_to(jnp.asarray(kvec).reshape(1, NTAB, 1), (1, NTAB, gp)))
    tabs = [(tab_re.reshape(2, ngrp, NTAB, gp)[k].transpose(1, 0, 2).reshape(NTAB, ngrp * gp),
             tab_im.reshape(2, ngrp, NTAB, gp)[k].transpose(1, 0, 2).reshape(NTAB, ngrp * gp)) for k in range(2)]
    b_t = lambda b: _blockdiag(b[0].reshape(nsb, gsb, gp, gc).transpose(0, 1, 3, 2)).astype(BF16)
    bre, bim = b_t(s5_b_re), b_t(s5_b_im)
    cp_t = lambda cp: _blockdiag(cp.reshape(2, nsb, gsb, gc, gp)).astype(BF16)
    ctr, cti = cp_t(cp_re), cp_t(cp_im)
    cmr, cmi = jnp.swapaxes(ctr, -1, -2), jnp.swapaxes(cti, -1, -2)
    dskip = s5_d

    lch = l // TC
    nproc = lch + 1
    order_f = lambda i: jnp.where(i == 0, lch, i - 1)
    order_b = lambda i: jnp.where(i == 0, lch, lch - i)
    rorder_f = lambda i: jnp.where(i == lch, lch, lch - 1 - i)
    rorder_b = lambda i: i
    y_f, hr_f, hi_f, cinr_f, cini_f = _s5_fwd(up, bre, bim, cmr[0], cmi[0], tabs[0][0], tabs[0][1], dskip,
                                               order_f, nproc, False, True, "s5_fwd_f")
    y_b, hr_b, hi_b, cinr_b, cini_b = _s5_fwd(up, bre, bim, cmr[1], cmi[1], tabs[1][0], tabs[1][1], dskip,
                                              order_b, nproc, True, False, "s5_fwd_b")

    x2, ypre, cat, mix = _mix_out_fwd(y_f, y_b, dpool, x1, modtab, w_pool_bd, pool_scale, w_glu_f, w_out_f)
    dx3, h2, ab2, o2, loss_part, dfinal_g = _ffn_fwd([x2], modtab, g2, w13_1, w2_1, 6, False, "ffn1_fwd",
                                                     head=(loss_target[0], final_g.reshape(1, d)))

    dhp2, d13_1, d2_1 = _ffn_bwd(dx3, modtab, h2, ab2, w13_1, w2_1, 6, False, "ffn1_bwd")
    dx2, dmod_678, dg2 = _ffn_bwd_norm(dx3, [dhp2], [x2], o2, modtab, g2, 6, False, "ffn1_bwd_norm")
    dypre, ddpool, dw_out, dw_glu, dw_pool_bd, dpscale, dgate5 = _mix_out_bwd(
        dx2, mix, cat, ypre, dpool, modtab, w_pool_bd, pool_scale, w_glu_f, w_out_f)

    du_f, dar_f, dai_f, dcr_f, dci_f, dbr_f, dbi_f, dd_skip, r13_1, r2_1 = _s5_bwd(
        dypre, up, hr_f, hi_f, cinr_f, cini_f, bre, bim, ctr[0], cti[0], tabs[0][0], tabs[0][1], dskip,
        rorder_f, nproc, False, True, "s5_bwd_f",
        xchg=[(d13_1, A2A_REL), (d2_1.reshape(NDEV, FPAD, d), A2A_REL)])
    du_b, dar_b, dai_b, dcr_b, dci_b, dbr_b, dbi_b, _, r_out, r_glu = _s5_bwd(
        dypre, up, hr_b, hi_b, cinr_b, cini_b, bre, bim, ctr[1], cti[1], tabs[1][0], tabs[1][1], dskip,
        rorder_b, nproc, True, False, "s5_bwd_b",
        xchg=[(dw_out.reshape(NDEV, -1, d), A2A), (dw_glu.reshape(NDEV, hm // NDEV, hm), A2A)])

    dab_r = jnp.stack([dar_f, dar_b]).reshape(q, 1, gp)
    dab_i = jnp.stack([dai_f, dai_b]).reshape(q, 1, gp)
    dcp_r = _diagblocks(jnp.stack([dcr_f, dcr_b]), gsb).reshape(q, gc, gp)
    dcp_i = _diagblocks(jnp.stack([dci_f, dci_b]), gsb).reshape(q, gc, gp)
    ga_re, ga_im, gldt, gc_re, gc_im = _s5_param_bwd(a_re3, a_im3, ldt3, c_re3, c_im3, dab_r, dab_i, dcp_r, dcp_i)
    gb_re = (_diagblocks(dbr_f, gsb) + _diagblocks(dbr_b, gsb)).transpose(0, 1, 3, 2).reshape(ngrp, gp, gc)
    gb_im = (_diagblocks(dbi_f, gsb) + _diagblocks(dbi_b, gsb)).transpose(0, 1, 3, 2).reshape(ngrp, gp, gc)

    def pack(small):
        offs, pieces, off = {}, [], 0
        for k_, a_ in small.items():
            p_ = _pad128(a_.astype(F32))
            offs[k_] = (off, int(np.prod(a_.shape)))
            off += p_.shape[0]
            pieces.append(p_)
        return jnp.concatenate(pieces).reshape(1, off), offs

    gw_pool = _diagblocks(dw_pool_bd, 4)
    bundle_a, offs = pack(dict(pool_w=gw_pool, pool_scale=dpscale, s5_a_re=ga_re, s5_a_im=ga_im,
                               s5_log_dt=gldt[:, 0, 0], s5_b_re=gb_re, s5_b_im=gb_im, s5_c_re=gc_re, s5_c_im=gc_im,
                               s5_d=dd_skip, final_g=dfinal_g))

    du_pool = _pool(ddpool, rows, True, "pool_bwd")
    dx1, dw_in, dmod_34, dg1 = _mix_in_bwd(du_pool, du_f, du_b, dx2, h1, x1, modtab, g1, w_in_f, lc)
    dhp0a, d13_0a, d2_0a, bund_a, r_in = _ffn_bwd(
        dx1, modtab, h0, ab0, w13_0, w2_0, 0, True, "ffn0_bwd_a", pair0=0, npair=2,
        xchg=[(bundle_a, GATHER), (dw_in.reshape(NDEV, d // NDEV, -1), A2A)])
    dhp0b, d13_0b, d2_0b, r13_0a, r2_0a = _ffn_bwd(
        dx1, modtab, h0, ab0, w13_0, w2_0, 0, True, "ffn0_bwd_b", pair0=2, npair=2,
        xchg=[(d13_0a, A2A_SAME), (d2_0a.reshape(NDEV // 2, FPAD, d), A2A_SAME)])
    dx0, dmod_012, dg0, r13_0b, r2_0b = _ffn_bwd_norm(
        dx1, [dhp0a, dhp0b], x0, o0, modtab, g0, 0, True, "ffn0_bwd_norm",
        xchg=[(d13_0b, A2A_OTHER), (d2_0b.reshape(NDEV // 2, FPAD, d), A2A_OTHER)])
    grad_x = dx0.reshape(1, l, d)

    dmod_c = jnp.concatenate([dmod_012[0], dmod_34[0], jnp.zeros((4, d), F32)], axis=0)
    dmod_l = jnp.concatenate([dmod_012[1], dmod_34[1], dgate5, dmod_678[0]], axis=0)
    bundle_b, offs_b = pack(dict(norm_g=jnp.concatenate([dg0, dg1, dg2], axis=0), dmod_l=dmod_l, dmod_c=dmod_c,
                                 loss=loss_part[:, :1]))
    bund_b = _exchange([(bundle_b, GATHER)], "exchange_small")[0]
    bunds = {**{k_: (bund_a.reshape(NDEV, -1), v_) for k_, v_ in offs.items()},
             **{k_: (bund_b.reshape(NDEV, -1), v_) for k_, v_ in offs_b.items()}}
    r13, r2 = [(r13_0a, r13_0b), (r13_1,)], [(r2_0a, r2_0b), (r2_1,)]

    def piece(name):
        b_, (o_, n_) = bunds[name]
        return b_[:, o_:o_ + n_]

    loss = jnp.sum(piece("loss")[:, 0])

    dl_all = lax.dynamic_slice_in_dim(piece("dmod_l"), me * wa, wa, axis=1)
    dc_all = lax.dynamic_slice_in_dim(piece("dmod_c"), me * wa, wa, axis=1)
    g_wada, gc_part = _ada_bwd(c16, w_ada[0], dl_all, dc_all)
    gc_all = _exchange([(gc_part, GATHER)], "exchange_cctx")[0]

    parts = {
        "c_ctx": gc_all.reshape(NDEV, d),
        "norm_g": lax.dynamic_slice_in_dim(piece("norm_g").reshape(NDEV, 3, d), me * (d // NDEV), d // NDEV,
                                           axis=2).reshape((NDEV,) + norm_g.shape),
        "w_ada": g_wada.reshape((1,) + w_ada.shape),
        "b_ada": jnp.concatenate([piece("dmod_l"), piece("dmod_c")], axis=0).reshape((2 * NDEV,) + b_ada.shape),
        "w_in": r_in.reshape((NDEV,) + w_in.shape),
        "s5_w_glu": r_glu.reshape((NDEV,) + s5_w_glu.shape),
        "w_out": r_out.reshape((NDEV,) + w_out.shape),
    }
    for k_ in ("pool_w", "pool_scale", "s5_a_re", "s5_a_im", "s5_log_dt", "s5_b_re", "s5_b_im", "s5_c_re", "s5_c_im",
               "s5_d", "final_g"):
        parts[k_] = piece(k_).reshape((NDEV,) + weights[k_].shape)

    grads, deltas, new_m, new_v = [], [], [], []
    ffn_parts = {"ffn_w1": (r13, 0, True), "ffn_w3": (r13, FPAD, True), "ffn_w2": (r2, 0, False)}
    for k_ in weights:
        if k_ in ffn_parts:
            shards, row0, transposed = ffn_parts[k_]
            flip = (lambda a: jnp.swapaxes(a, 2, 3)) if transposed else (lambda a: a)
            outs = _adamw_ffn(shards, row0, flip(weights[k_]), flip(mom_m[k_]), flip(mom_v[k_]), "adamw_" + k_)
            g_, d_, m_, v_ = (flip(o_) for o_ in outs)
        else:
            g_, d_, m_, v_ = _adamw(parts[k_], weights[k_], mom_m[k_], mom_v[k_], "adamw_" + k_)
        grads.append(g_)
        deltas.append(d_)
        new_m.append(m_)
        new_v.append(v_)
    return (loss, grad_x, *grads, *deltas, *new_m, *new_v)
```

```python
import functools
import math

import numpy as np
import jax
import jax.numpy as jnp
from jax import lax
from jax.experimental import pallas as pl
from jax.experimental.pallas import tpu as pltpu

F32 = jnp.float32
BF16 = jnp.bfloat16
AXES = ("x", "y", "c")
NDEV = 8
EPS = 1e-6
TM = 512
TML = 512
TC = 256
SEG = TC // 8
NTAB = SEG + 16
S5_UNROLL = True
GRID_W = 64
POOL_WINDOWS = (2, 4, 8, 16)
LANE = 128
FPAD = 384
VMEM_LIMIT = 56 * 1024 * 1024
ADAM_BLOCK_BYTES = 1024 * 1024
ADAM_FFN_LANES = 256
ADAM_LR, ADAM_B1, ADAM_B2, ADAM_EPS, ADAM_WD, ADAM_STEP = 0.001, 0.9, 0.999, 1e-08, 0.01, 10


def _raw_call(body, kw):
    return pl.pallas_call(body, **kw)


def _pcall(body, xchg=None, edges=None, **kw):
    extra = ()
    if xchg:
        arrs, kinds = [a for a, _ in xchg], [k for _, k in xchg]
        n = len(arrs)
        n_in, n_out, n_scr = len(kw["in_specs"]), len(kw["out_shape"]), len(kw.get("scratch_shapes", ()))
        inner = body

        def hosted(*refs):
            a, b = n_in, n_in + n
            c_, d_ = b + n_out, b + n_out + n
            e = d_ + n_scr
            first, mid, last = edges()

            @pl.when(first)
            def _():
                _xchg_start(refs[a:b], refs[c_:d_], refs[e:], kinds)

            inner(*refs[:a], *refs[b:c_], *refs[d_:e])

            if any(k == GATHER_REL for k in kinds):
                @pl.when(mid)
                def _():
                    _xchg_relay(refs[a:b], refs[c_:d_], refs[e:], kinds)

            @pl.when(last)
            def _():
                _xchg_finish(refs[a:b], refs[c_:d_], refs[e:], kinds)

        body = hosted
        any_spec = pl.BlockSpec(memory_space=pl.ANY)
        kw = dict(kw, in_specs=list(kw["in_specs"]) + [any_spec] * n,
                  out_shape=tuple(kw["out_shape"]) + _xchg_shapes(arrs, kinds),
                  out_specs=tuple(kw["out_specs"]) + (any_spec,) * n,
                  scratch_shapes=list(kw.get("scratch_shapes", ())) + _xchg_sems(n))
        extra = tuple(arrs)
    call = _raw_call(body, kw)
    return (lambda *args: call(*args, *extra)) if extra else call


def _params(sem):
    return pltpu.CompilerParams(dimension_semantics=sem, vmem_limit_bytes=VMEM_LIMIT)


def _dot(a, b):
    return jnp.dot(a, b, preferred_element_type=F32)


def _dot_nt(a, b):
    return lax.dot_general(a, b, (((1,), (1,)), ((), ())), preferred_element_type=F32)


def _dot_tn(a, b):
    return lax.dot_general(a, b, (((0,), (0,)), ((), ())), preferred_element_type=F32)


def _sigmoid(x):
    return 1.0 / (1.0 + jnp.exp(-x))


def _colsum(a):
    return jnp.sum(a, axis=0, keepdims=True)


def _resident(shape):
    nd = len(shape)
    return pl.BlockSpec(shape, lambda *_: (0,) * nd, pipeline_mode=pl.Buffered(1))


def _const(shape):
    nd = len(shape)
    return pl.BlockSpec(shape, lambda *_: (0,) * nd)


GATHER = "gather"
GATHER_REL = "gather_rel"
A2A = "a2a"
A2A_REL = "a2a_rel"
A2A_SAME = "a2a_same"
A2A_OTHER = "a2a_other"


def _exchange(items, name):
    arrs, kinds = [a for a, _ in items], [k for _, k in items]
    n = len(arrs)

    def body(*refs):
        _xchg_start(refs[:n], refs[n:2 * n], refs[2 * n:], kinds)
        _xchg_relay(refs[:n], refs[n:2 * n], refs[2 * n:], kinds)
        _xchg_finish(refs[:n], refs[n:2 * n], refs[2 * n:], kinds)

    any_spec = pl.BlockSpec(memory_space=pl.ANY)
    outs = _pcall(
        body, name=name, out_shape=_xchg_shapes(arrs, kinds), in_specs=[any_spec] * n, out_specs=[any_spec] * n,
        scratch_shapes=_xchg_sems(n),
    )(*arrs)
    return list(outs)


def _edges1(n0):
    return lambda: (pl.program_id(0) == 0, pl.program_id(0) == (7 * n0) // 8, pl.program_id(0) == n0 - 1)


def _edges2(n0, n1):
    def edges():
        step = pl.program_id(0) * n1 + pl.program_id(1)
        return step == 0, step == (7 * n0 * n1) // 8, step == n0 * n1 - 1
    return edges


def _xchg_shapes(arrs, kinds):
    return tuple(jax.ShapeDtypeStruct(((NDEV,) if k in (GATHER, GATHER_REL) else ()) + tuple(a.shape), a.dtype)
                 for a, k in zip(arrs, kinds))


def _xchg_sems(n):
    return [pltpu.SemaphoreType.DMA((n * (NDEV - 1),)), pltpu.SemaphoreType.DMA((n * (NDEV - 1),)),
            pltpu.SemaphoreType.DMA((n,))]


def _xchg_plan(in_refs, out_refs, sems, kinds):
    send_sems, recv_sems, loc_sems = sems
    x, y, c = (lax.axis_index(a) for a in AXES)
    my_abs, my_chip = 4 * x + 2 * y + c, 2 * x + y

    def peer(p):
        px = 1 - x if p & 4 else x
        py = 1 - y if p & 2 else y
        pc = 1 - c if p & 1 else c
        return (px, py, pc), 4 * px + 2 * py + pc, 2 * px + py

    starts, relays, recvs = [], [], []
    for k, kind in enumerate(kinds):
        src, dst = in_refs[k], out_refs[k]

        def remote(src_ref, row, p, pair, dst=dst, k=k):
            dev, sem = peer(p)[0], k * (NDEV - 1) + pair - 1
            return lambda: pltpu.make_async_remote_copy(
                src_ref=src_ref, dst_ref=dst.at[row], send_sem=send_sems.at[sem], recv_sem=recv_sems.at[sem],
                device_id=dev, device_id_type=pl.DeviceIdType.MESH)

        def local(src_ref, row, dst=dst, k=k):
            return lambda: pltpu.make_async_copy(src_ref, dst.at[row], loc_sems.at[k])

        if kind == GATHER:
            for p in range(1, NDEV):
                starts.append((remote(src, my_abs, p, p), False))
                recvs.append(remote(src, peer(p)[1], p, p))
            starts.append((local(src, my_abs), True))
        elif kind == GATHER_REL:
            for p in (1, 4, 2, 6):
                starts.append((remote(src, 4 * (p & 1) + my_chip, p, p), False))
            for q in (4, 2, 6):
                pchip = peer(q)[2]
                relays.append((remote(src, pchip, q, q), remote(dst.at[pchip], 4 + pchip, 1, q | 1)))
                recvs.append(remote(src, 4 + pchip, 1, q | 1))
            recvs.append(remote(src, 4 + my_chip, 1, 1))
            starts.append((local(src, my_chip), True))
        elif kind in (A2A, A2A_REL):
            for p in range(1, NDEV):
                _, pabs, pchip = peer(p)
                theirs, mine = (pabs, my_abs) if kind == A2A else (4 * (p & 1) + pchip, 4 * (p & 1) + my_chip)
                starts.append((remote(src.at[theirs], mine, p, p), False))
                recvs.append(remote(src.at[theirs], theirs, p, p))
            own = my_abs if kind == A2A else my_chip
            starts.append((local(src.at[own], own), True))
        else:
            for p in ((4, 2, 6) if kind == A2A_SAME else (1, 5, 3, 7)):
                pchip = peer(p)[2]
                starts.append((remote(src.at[pchip], my_chip, p, p), False))
                recvs.append(remote(src.at[pchip], pchip, p, p))
            if kind == A2A_SAME:
                starts.append((local(src.at[my_chip], my_chip), True))
    return starts, relays, recvs


def _xchg_start(in_refs, out_refs, sems, kinds):
    for make, _ in _xchg_plan(in_refs, out_refs, sems, kinds)[0]:
        make().start()


def _xchg_relay(in_refs, out_refs, sems, kinds):
    for arrival, forward in _xchg_plan(in_refs, out_refs, sems, kinds)[1]:
        arrival().wait_recv()
        forward().start()


def _xchg_finish(in_refs, out_refs, sems, kinds):
    starts, relays, recvs = _xchg_plan(in_refs, out_refs, sems, kinds)
    for make in recvs:
        make().wait_recv()
    for make, is_local in starts:
        if is_local:
            make().wait()
        else:
            make().wait_send()
    for _, forward in relays:
        forward().wait_send()


def _front(w13, w2, ng, c, c_ctx, w_blk, b_blk):
    d, wa = w_blk.shape
    big, big_kinds = [w13, w2, ng], [GATHER_REL, GATHER_REL, GATHER]

    def body(w13_ref, w2_ref, ng_ref, c_ref, cctx_ref, w_ref, b_ref, g13_ref, g2_ref, gng_ref, gc_ref, msl_ref, gm_ref,
             c16, mod, loc_sems, *sems):
        big_io = ((w13_ref, w2_ref, ng_ref), (g13_ref, g2_ref, gng_ref), sems[0:3], big_kinds)
        c_io = ((c_ref,), (gc_ref,), sems[3:6], [GATHER])
        mod_io = ((msl_ref,), (gm_ref,), sems[6:9], [GATHER])
        _xchg_start(*c_io)
        _xchg_start(*big_io)
        _xchg_finish(*c_io)
        rows = [pltpu.make_async_copy(gc_ref.at[k], c16.at[pl.ds(k, 1), :], loc_sems.at[k]) for k in range(NDEV)]
        for cp in rows:
            cp.start()
        c16[NDEV:NDEV + 1, :] = cctx_ref[...]
        c16[NDEV + 1:, :] = jnp.zeros((16 - NDEV - 1, d), F32)
        for cp in rows:
            cp.wait()
        cc = c16[...]
        mod[...] = _dot((cc * _sigmoid(cc)).astype(BF16), w_ref[...].astype(BF16)) + b_ref[...]
        out = pltpu.make_async_copy(mod, msl_ref, loc_sems.at[NDEV])
        out.start()
        out.wait()
        _xchg_start(*mod_io)
        _xchg_relay(*big_io)
        _xchg_finish(*big_io)
        _xchg_finish(*mod_io)

    any_spec = pl.BlockSpec(memory_space=pl.ANY)
    vmem = pl.BlockSpec(memory_space=pltpu.VMEM)
    outs = _pcall(
        body, name="exchange_weights",
        out_shape=_xchg_shapes(big + [c], big_kinds + [GATHER]) + (jax.ShapeDtypeStruct((16, wa), F32),
                                                                   jax.ShapeDtypeStruct((NDEV, 16, wa), F32)),
        in_specs=[any_spec] * 4 + [vmem] * 3, out_specs=[any_spec] * 6,
        scratch_shapes=[pltpu.VMEM((16, d), F32), pltpu.VMEM((16, wa), F32), pltpu.SemaphoreType.DMA((NDEV + 1,))]
        + _xchg_sems(3) + _xchg_sems(1) + _xchg_sems(1),
        compiler_params=_params(None),
    )(w13, w2, ng, c, c_ctx, w_blk, b_blk)
    return outs[0], outs[1], outs[2], outs[3], outs[5]


def _ada_bwd(c16, w_blk, dl, dc):
    d, w = w_blk.shape

    def body(c_ref, w_ref, dl_ref, dc_ref, gw_ref, gc_ref):
        cc = c_ref[...]
        sg = _sigmoid(cc)
        s = (cc * sg).astype(BF16)
        dctot = _colsum(dc_ref[...])
        dm = jnp.concatenate([dl_ref[...], jnp.broadcast_to(dctot, (8, w))], axis=0)
        rows = lax.broadcasted_iota(jnp.int32, (16, w), 0)
        dm = jnp.where(rows <= 8, dm, 0.0).astype(BF16)
        gw_ref[...] = _dot_tn(s, dm)
        t = _dot_nt(jnp.broadcast_to(dctot, (8, w)).astype(BF16), w_ref[...].astype(BF16))[0:1]
        c8, s8 = cc[8:9], sg[8:9]
        gc_ref[...] = t * (s8 * (1.0 + c8 * (1.0 - s8)))

    return _pcall(body, name="ada_bwd",
                  out_shape=(jax.ShapeDtypeStruct((d, w), F32), jax.ShapeDtypeStruct((1, d), F32)),
                  compiler_params=_params(None))(c16, w_blk, dl, dc)


def _norm_fwd(x, g, shift, scale):
    rstd = lax.rsqrt(jnp.mean(x * x, axis=-1, keepdims=True) + EPS)
    xhat = x * rstd
    xn = xhat * g
    return rstd, xhat, xn, xn * (1.0 + scale) + shift


def _norm_bwd(dh, rstd, xhat, xn, g, scale):
    dxn = dh * (1.0 + scale)
    dxhat = dxn * g
    dx = rstd * (dxhat - xhat * jnp.mean(dxhat * xhat, axis=-1, keepdims=True))
    return dx, _colsum(dh), _colsum(dh * xn), _colsum(dxn * xhat)


def _mod_spec(d, has_ctx):
    if has_ctx:
        return pl.BlockSpec((1, 9, d), lambda *ids: (jnp.minimum(ids[-1], 1), 0, 0))
    return pl.BlockSpec((1, 9, d), lambda *ids: (1, 0, 0))


def _row_specs(nx, tm, d):
    if nx == 1:
        return [pl.BlockSpec((tm, d), lambda i: (i, 0))]
    return [pl.BlockSpec((tm, d), lambda i: (0, 0)), pl.BlockSpec((tm, d), lambda i: (jnp.maximum(i - 1, 0), 0))]


def _rows(refs):
    if len(refs) == 1:
        return refs[0][...]
    return jnp.where(pl.program_id(0) == 0, refs[0][...], refs[1][...])


def _tile(has_ctx):
    return TM if has_ctx else TML


def _tile_of(nt, has_ctx):
    if has_ctx:
        return lambda i: jnp.where(i == 0, nt - 1, i - 1)
    return lambda i: i


def _dmod_spec(nrow, d, has_ctx):
    if has_ctx:
        return pl.BlockSpec((1, nrow, d), lambda i: (jnp.minimum(i, 1), 0, 0))
    return pl.BlockSpec((1, nrow, d), lambda i: (0, 0, 0))


def _ffn_fwd(x, modtab, g, w13, w2, k0, has_ctx, name, xchg=None, head=None):
    t, d = sum(a.shape[0] for a in x), x[0].shape[1]
    nb, w, _ = w13.shape
    fp = w // 2
    tm = _tile(has_ctx)
    nx = len(x)

    nhead = 0 if head is None else 2

    def body(*refs):
        mod_ref, g_ref, w13_ref, w2_ref = refs[nx:nx + 4]
        xo_ref, h_ref, ab_ref, o_ref = refs[nx + 4 + nhead:nx + 8 + nhead]
        xx = _rows(refs[:nx])
        shift, scale, gate = mod_ref[0, k0:k0 + 1, :], mod_ref[0, k0 + 1:k0 + 2, :], mod_ref[0, k0 + 2:k0 + 3, :]
        _, _, _, h = _norm_fwd(xx, g_ref[...], shift, scale)
        hb = h.astype(BF16)
        h_ref[...] = hb
        acc = jnp.zeros((tm, d), F32)
        for p in range(nb // 2):
            zs = []
            for q in range(2):
                blk = 2 * p + q
                ab = _dot_nt(hb, w13_ref[blk])
                ab_ref[:, blk * w:(blk + 1) * w] = ab.astype(BF16)
                a, b = ab[:, :fp], ab[:, fp:]
                zs.append((a * _sigmoid(a) * b).astype(BF16))
            acc = acc + _dot(jnp.concatenate(zs, axis=1), w2_ref[p])
        o_ref[...] = acc.astype(BF16)
        xo = xx + (0.5 * gate) * acc
        if head is None:
            xo_ref[...] = xo
        else:
            tgt_ref, fg_ref = refs[nx + 4:nx + 6]
            loss_ref, dfg_ref = refs[nx + 8 + nhead:]

            @pl.when(pl.program_id(0) == 0)
            def _():
                loss_ref[...] = jnp.zeros_like(loss_ref)
                dfg_ref[...] = jnp.zeros_like(dfg_ref)

            xo_ref[...] = _loss_head(xo, tgt_ref[...], fg_ref[...], loss_ref, dfg_ref)

    tile = _tile_of(t // tm, has_ctx)
    row = lambda i: (tile(i), 0)
    head_in = [] if head is None else [pl.BlockSpec((tm, d), row), _const((1, d))]
    head_shape = () if head is None else (jax.ShapeDtypeStruct((1, LANE), F32), jax.ShapeDtypeStruct((1, d), F32))
    head_out = () if head is None else (_const((1, LANE)), _const((1, d)))
    return _pcall(
        body, name=name, grid=(t // tm,), xchg=xchg, edges=_edges1(t // tm),
        out_shape=(jax.ShapeDtypeStruct((t, d), F32), jax.ShapeDtypeStruct((t, d), BF16),
                   jax.ShapeDtypeStruct((t, nb * w), BF16), jax.ShapeDtypeStruct((t, d), BF16)) + head_shape,
        in_specs=_row_specs(nx, tm, d) + [_mod_spec(d, has_ctx), _const((1, d)),
                                          _resident(w13.shape), _resident(w2.shape)] + head_in,
        out_specs=(pl.BlockSpec((tm, d), row), pl.BlockSpec((tm, d), row), pl.BlockSpec((tm, nb * w), row),
                   pl.BlockSpec((tm, d), row)) + head_out,
        compiler_params=_params(("arbitrary",)),
    )(*x, modtab, g, w13, w2, *(head or ()))


def _ffn_bwd(dy, modtab, h, ab, w13, w2, k0, has_ctx, name, pair0=0, npair=None, xchg=None):
    t, d = dy.shape
    _, w, _ = w13.shape
    fp = w // 2
    npair = w13.shape[0] // 2 if npair is None else npair
    nb = 2 * npair
    tm = _tile(has_ctx)
    nt = t // tm

    def body(dy_ref, mod_ref, h_ref, ab_ref, w13_ref, w2_ref, dh_ref, d13_ref, d2_ref, acc13, acc2):
        i = pl.program_id(1)

        @pl.when(i == 0)
        def _():
            acc13[...] = jnp.zeros_like(acc13)
            acc2[...] = jnp.zeros_like(acc2)

        gate = mod_ref[0, k0 + 2:k0 + 3, :]
        do = (dy_ref[...] * (0.5 * gate)).astype(BF16)
        dz = _dot_nt(do, w2_ref[0])
        hb = h_ref[...]
        dh = jnp.zeros((tm, d), F32)
        zs = []
        for q in range(2):
            ab = ab_ref[:, q * w:(q + 1) * w].astype(F32)
            a, b = ab[:, :fp], ab[:, fp:]
            sg = _sigmoid(a)
            sa = a * sg
            dzq = dz[:, q * fp:(q + 1) * fp]
            da = dzq * b * (sg * (1.0 + a * (1.0 - sg)))
            db = dzq * sa
            dab = jnp.concatenate([da, db], axis=1).astype(BF16)
            dh = dh + _dot(dab, w13_ref[q])
            acc13[q] += _dot_tn(dab, hb)
            zs.append((sa * b).astype(BF16))
        acc2[...] += _dot_tn(jnp.concatenate(zs, axis=1), do)
        dh_ref[0] = dh.astype(BF16)

        @pl.when(i == nt - 1)
        def _():
            d13_ref[...] = acc13[...].astype(BF16)
            d2_ref[0] = acc2[...].astype(BF16)

    mod_spec = _mod_spec(d, has_ctx)
    tile = _tile_of(nt, has_ctx)
    return _pcall(
        body, name=name, grid=(npair, nt), xchg=xchg, edges=_edges2(npair, nt),
        out_shape=(jax.ShapeDtypeStruct((npair, t, d), BF16), jax.ShapeDtypeStruct((nb, w, d), BF16),
                   jax.ShapeDtypeStruct((npair, 2 * fp, d), BF16)),
        in_specs=[pl.BlockSpec((tm, d), lambda p, i: (tile(i), 0)), mod_spec,
                  pl.BlockSpec((tm, d), lambda p, i: (tile(i), 0)),
                  pl.BlockSpec((tm, 2 * w), lambda p, i: (tile(i), pair0 + p)),
                  pl.BlockSpec((2, w, d), lambda p, i: (pair0 + p, 0, 0)),
                  pl.BlockSpec((1, 2 * fp, d), lambda p, i: (pair0 + p, 0, 0))],
        out_specs=(pl.BlockSpec((1, tm, d), lambda p, i: (p, tile(i), 0)),
                   pl.BlockSpec((2, w, d), lambda p, i: (p, 0, 0)),
                   pl.BlockSpec((1, 2 * fp, d), lambda p, i: (p, 0, 0))),
        scratch_shapes=[pltpu.VMEM((2, w, d), F32), pltpu.VMEM((2 * fp, d), F32)],
        compiler_params=_params(("arbitrary", "arbitrary")),
    )(dy, modtab, h, ab, w13, w2)


def _ffn_bwd_norm(dy, dhps, x, o, modtab, g, k0, has_ctx, name, xchg=None):
    t, d = dy.shape
    ngrp = 2 if has_ctx else 1
    tm = _tile(has_ctx)
    ndh, nx = len(dhps), len(x)
    lat0 = ngrp - 1

    def body(dy_ref, *rest):
        dhp_refs, x_refs = rest[:ndh], rest[ndh:ndh + nx]
        o_ref, mod_ref, g_ref, dx_ref, dmod_ref, dg_ref = rest[ndh + nx:]
        i = pl.program_id(0)

        @pl.when(i == 0)
        def _():
            dg_ref[...] = jnp.zeros_like(dg_ref)

        @pl.when((i == 0) | (i == ngrp - 1))
        def _():
            dmod_ref[...] = jnp.zeros_like(dmod_ref)

        dh = None
        for ref in dhp_refs:
            for p in range(ref.shape[0]):
                dh = ref[p].astype(F32) if dh is None else dh + ref[p].astype(F32)
        scale = mod_ref[0, k0 + 1:k0 + 2, :]
        gg = g_ref[...]
        rstd, xhat, xn, _ = _norm_fwd(_rows(x_refs), gg, 0.0, scale)
        dxn, dshift, dscale, dg = _norm_bwd(dh, rstd, xhat, xn, gg, scale)
        dyv = dy_ref[...]
        dx_ref[...] = dyv + dxn
        dmod_ref[0, 0:1, :] += dshift
        dmod_ref[0, 1:2, :] += dscale
        dmod_ref[0, 2:3, :] += _colsum(0.5 * dyv * o_ref[...].astype(F32))
        dg_ref[...] += dg

    tile = _tile_of(t // tm, has_ctx)
    row = lambda i: (tile(i), 0)
    return _pcall(
        body, name=name, grid=(t // tm,), xchg=xchg, edges=_edges1(t // tm),
        out_shape=(jax.ShapeDtypeStruct((t - lat0 * tm, d), F32), jax.ShapeDtypeStruct((ngrp, 3, d), F32),
                   jax.ShapeDtypeStruct((1, d), F32)),
        in_specs=[pl.BlockSpec((tm, d), row)]
        + [pl.BlockSpec((a.shape[0], tm, d), lambda i: (0, tile(i), 0)) for a in dhps]
        + _row_specs(nx, tm, d) + [pl.BlockSpec((tm, d), row), _mod_spec(d, has_ctx), _const((1, d))],
        out_specs=(pl.BlockSpec((tm, d), lambda i: (jnp.maximum(i - lat0, 0), 0)), _dmod_spec(3, d, has_ctx),
                   _const((1, d))),
        compiler_params=_params(("arbitrary",)),
    )(dy, *dhps, *x, o, modtab, g)


def _mix_in_fwd(x1, modtab, g, w_in):
    t, d = x1.shape
    hm = w_in.shape[1] // 2

    def body(x_ref, mod_ref, g_ref, w_ref, h_ref, up_ref, us_ref):
        _, _, _, h = _norm_fwd(x_ref[...], g_ref[...], mod_ref[0, 3:4, :], mod_ref[0, 4:5, :])
        hb = h.astype(BF16)
        h_ref[...] = hb
        u = _dot(hb, w_ref[...])
        up_ref[...] = u[:, :hm]
        us_ref[...] = u[:, hm:]

    tile = _tile_of(t // TM, True)
    row = lambda i: (tile(i), 0)
    return _pcall(
        body, name="mix_in_fwd", grid=(t // TM,),
        out_shape=(jax.ShapeDtypeStruct((t, d), BF16), jax.ShapeDtypeStruct((t, hm), F32),
                   jax.ShapeDtypeStruct((t, hm), F32)),
        in_specs=[pl.BlockSpec((TM, d), row), _mod_spec(d, True), _const((1, d)), _resident(w_in.shape)],
        out_specs=(pl.BlockSpec((TM, d), row), pl.BlockSpec((TM, hm), row), pl.BlockSpec((TM, hm), row)),
        compiler_params=_params(("arbitrary",)),
    )(x1, modtab, g, w_in)


def _mix_in_bwd(du_pool, du_f, du_b, dx2, h1, x1, modtab, g, w_in, lc):
    t, d = x1.shape
    m = w_in.shape[1]
    hm = m // 2
    nt = t // TM

    def body(dup_ref, duf_ref, dub_ref, dx2_ref, h_ref, x_ref, mod_ref, g_ref, w_ref, dx_ref, dw_ref, dmod_ref, dg_ref, acc):
        i = pl.program_id(0)

        @pl.when(i == 0)
        def _():
            acc[...] = jnp.zeros_like(acc)
            dg_ref[...] = jnp.zeros_like(dg_ref)

        @pl.when(i <= 1)
        def _():
            dmod_ref[...] = jnp.zeros_like(dmod_ref)

        lat = (i > 0).astype(F32)
        valid = (i > 0) | (lax.broadcasted_iota(jnp.int32, (TM, 1), 0) < lc)
        du_s5 = jnp.where(valid, duf_ref[...] + dub_ref[...], 0.0)
        du = jnp.concatenate([dup_ref[...] * lat, du_s5], axis=1).astype(BF16)
        dh = _dot_nt(du, w_ref[...])
        acc[...] += _dot_tn(h_ref[...], du)
        scale = mod_ref[0, 4:5, :]
        gg = g_ref[...]
        rstd, xhat, xn, _ = _norm_fwd(x_ref[...], gg, 0.0, scale)
        dxn, dshift, dscale, dg = _norm_bwd(dh, rstd, xhat, xn, gg, scale)
        dx_ref[...] = dx2_ref[...] * lat + dxn
        dmod_ref[0, 0:1, :] += dshift
        dmod_ref[0, 1:2, :] += dscale
        dg_ref[...] += dg

        @pl.when(i == nt - 1)
        def _():
            dw_ref[...] = acc[...].astype(BF16)

    tile = _tile_of(nt, True)
    row = lambda i: (tile(i), 0)
    lrow = lambda i: (jnp.maximum(i - 1, 0), 0)
    return _pcall(
        body, name="mix_in_bwd", grid=(nt,),
        out_shape=(jax.ShapeDtypeStruct((t, d), F32), jax.ShapeDtypeStruct((d, m), BF16),
                   jax.ShapeDtypeStruct((2, 2, d), F32), jax.ShapeDtypeStruct((1, d), F32)),
        in_specs=[pl.BlockSpec((TM, hm), lrow), pl.BlockSpec((TM, hm), row), pl.BlockSpec((TM, hm), row),
                  pl.BlockSpec((TM, d), lrow), pl.BlockSpec((TM, d), row), pl.BlockSpec((TM, d), row),
                  _mod_spec(d, True), _const((1, d)), _resident(w_in.shape)],
        out_specs=(pl.BlockSpec((TM, d), row), _const((d, m)), _dmod_spec(2, d, True), _const((1, d))),
        scratch_shapes=[pltpu.VMEM((d, m), F32)],
        compiler_params=_params(("arbitrary",)),
    )(du_pool, du_f, du_b, dx2, h1, x1, modtab, g, w_in)


def _pool(v, rows, transpose, name):
    l, n = rows * GRID_W, GRID_W * LANE
    ngrp = v.shape[1] // LANE
    nchunk = 4
    cw = n // nchunk

    def rowsum(val, lo, hi):
        ri = lax.broadcasted_iota(jnp.int32, (rows, rows), 0)
        ci = lax.broadcasted_iota(jnp.int32, (rows, rows), 1)
        sel = (((ci - ri) >= -lo) & ((ci - ri) <= hi)).astype(BF16)
        v1 = val.astype(BF16)
        r1 = val - v1.astype(F32)
        v2 = r1.astype(BF16)
        v3 = (r1 - v2.astype(F32)).astype(BF16)
        return _dot(sel, v1) + _dot(sel, v2) + _dot(sel, v3)

    def one(v_ref, o_ref, g_scr, r_scr, a_scr, win):
        for c in range(GRID_W):
            g_scr[:, c * LANE:(c + 1) * LANE] = v_ref[pl.ds(c, rows, stride=GRID_W), :]
        lo = win // 2
        hi = win - 1 - lo
        rlo, rhi = (hi, lo) if transpose else (lo, hi)
        ridx = lax.broadcasted_iota(jnp.int32, (rows, 1), 0)
        cnt_r = (jnp.minimum(ridx + hi + 1, rows) - jnp.maximum(ridx - lo, 0)).astype(F32)
        cidx = lax.broadcasted_iota(jnp.int32, (1, n), 1) // LANE
        cnt_c = (jnp.minimum(cidx + hi + 1, GRID_W) - jnp.maximum(cidx - lo, 0)).astype(F32)
        if not transpose:
            for k in range(nchunk):
                sl = slice(k * cw, (k + 1) * cw)
                r_scr[:, sl] = rowsum(g_scr[:, sl], rlo, rhi) / cnt_r
        else:
            r_scr[...] = g_scr[...] / cnt_c
        a_scr[...] = r_scr[...]
        for j in range(-rlo, rhi + 1):
            if j == 0:
                continue
            c0, c1 = max(0, -j), min(GRID_W, GRID_W - j)
            a_scr[:, c0 * LANE:c1 * LANE] += r_scr[:, (c0 + j) * LANE:(c1 + j) * LANE]
        if not transpose:
            r_scr[...] = a_scr[...] / cnt_c - g_scr[...]
        else:
            for k in range(nchunk):
                sl = slice(k * cw, (k + 1) * cw)
                r_scr[:, sl] = rowsum(a_scr[:, sl] / cnt_r, rlo, rhi) - g_scr[:, sl]
        for c in range(GRID_W):
            o_ref[pl.ds(c, rows, stride=GRID_W), :] = r_scr[:, c * LANE:(c + 1) * LANE]

    def body(v_ref, o_ref, g_scr, r_scr, a_scr):
        grp = pl.program_id(0)
        for k, win in enumerate(POOL_WINDOWS):
            @pl.when(grp == k)
            def _(win=win):
                one(v_ref, o_ref, g_scr, r_scr, a_scr, win)

    spec = pl.BlockSpec((l, LANE), lambda k: (0, k))
    return _pcall(
        body, name=name, grid=(ngrp,), out_shape=jax.ShapeDtypeStruct((l, ngrp * LANE), F32),
        in_specs=[spec], out_specs=spec,
        scratch_shapes=[pltpu.VMEM((rows, n), F32), pltpu.VMEM((rows, n), F32), pltpu.VMEM((rows, n), F32)],
        compiler_params=_params(("arbitrary",)),
    )(v)


def _cmul(ar, ai, br, bi):
    return ar * br - ai * bi, ar * bi + ai * br


def _s5_prep(a_re, a_im, log_dt, c_re, c_im, kvec):
    q, nc, p = c_re.shape

    def body(ar_ref, ai_ref, ldt_ref, cr_ref, ci_ref, k_ref, cpr_ref, cpi_ref, tr_ref, ti_ref):
        lr, li = ar_ref[...], ai_ref[...]
        dt = jnp.exp(ldt_ref[...])
        kk = k_ref[...]
        mag = jnp.exp(kk * (lr * dt))
        ph = kk * (li * dt)
        tr_ref[...] = mag * jnp.cos(ph)
        ti_ref[...] = mag * jnp.sin(ph)
        m1 = jnp.exp(lr * dt)
        abr, abi = m1 * jnp.cos(li * dt), m1 * jnp.sin(li * dt)
        den = lr * lr + li * li
        xr, xi = abr - 1.0, abi
        cfr, cfi = (xr * lr + xi * li) / den, (xi * lr - xr * li) / den
        pr, pi_ = _cmul(cr_ref[...], ci_ref[...], cfr, cfi)
        cpr_ref[...] = pr
        cpi_ref[...] = pi_

    return _pcall(
        body, name="s5_prep",
        out_shape=(jax.ShapeDtypeStruct((q, nc, p), F32), jax.ShapeDtypeStruct((q, nc, p), F32),
                   jax.ShapeDtypeStruct((q, NTAB, p), F32), jax.ShapeDtypeStruct((q, NTAB, p), F32)),
        compiler_params=_params(None),
    )(a_re, a_im, log_dt, c_re, c_im, kvec)


def _s5_param_bwd(a_re, a_im, log_dt, c_re, c_im, dab_r, dab_i, dcp_r, dcp_i):
    q, nc, p = c_re.shape

    def body(ar_ref, ai_ref, ldt_ref, cr_ref, ci_ref, dar_ref, dai_ref, dcr_ref, dci_ref,
             gar_ref, gai_ref, gdt_ref, gcr_ref, gci_ref):
        lr, li = ar_ref[...], ai_ref[...]
        dt = jnp.exp(ldt_ref[...])
        m1 = jnp.exp(lr * dt)
        abr, abi = m1 * jnp.cos(li * dt), m1 * jnp.sin(li * dt)
        den = lr * lr + li * li
        xr, xi = abr - 1.0, abi
        cfr, cfi = (xr * lr + xi * li) / den, (xi * lr - xr * li) / den
        cr, ci = cr_ref[...], ci_ref[...]
        dcr, dci = dcr_ref[...], dci_ref[...]
        gcr, gci = _cmul(dcr, dci, cfr, -cfi)
        gcr_ref[...] = gcr
        gci_ref[...] = gci
        t_r, t_i = _cmul(cr, -ci, dcr, dci)
        dcf_r = jnp.sum(t_r, axis=1, keepdims=True)
        dcf_i = jnp.sum(t_i, axis=1, keepdims=True)
        ilr, ili = lr / den, -li / den
        u_r, u_i = _cmul(dcf_r, dcf_i, ilr, -ili)
        dab_r_, dab_i_ = dar_ref[...] + u_r, dai_ref[...] + u_i
        v_r, v_i = _cmul(dab_r_, dab_i_, abr, -abi)
        cl_r, cl_i = _cmul(cfr, cfi, ilr, ili)
        w_r, w_i = _cmul(dcf_r, dcf_i, cl_r, -cl_i)
        gar_ref[...] = v_r * dt - w_r
        gai_ref[...] = v_i * dt - w_i
        la_r, la_i = _cmul(lr, li, abr, abi)
        ddt = la_r * dab_r_ + la_i * dab_i_
        gdt_ref[...] = dt * jnp.sum(ddt, axis=2, keepdims=True)

    return _pcall(
        body, name="s5_param_bwd",
        out_shape=(jax.ShapeDtypeStruct((q, 1, p), F32), jax.ShapeDtypeStruct((q, 1, p), F32),
                   jax.ShapeDtypeStruct((q, 1, p), F32), jax.ShapeDtypeStruct((q, nc, p), F32),
                   jax.ShapeDtypeStruct((q, nc, p), F32)),
        compiler_params=_params(None),
    )(a_re, a_im, log_dt, c_re, c_im, dab_r, dab_i, dcp_r, dcp_i)


def _bcast8(row, c):
    return jnp.broadcast_to(row, (8, c))


def _slab(j):
    return j * 8 if isinstance(j, int) else pl.multiple_of(j * 8, 8)


def _scan_head(xr, xi, tre_ref, tim_ref, car_r, car_i, desc, conj, cs):
    sg = -1.0 if conj else 1.0
    ar = _bcast8(tre_ref[0:1, :], cs)
    ai = sg * _bcast8(tim_ref[0:1, :], cs)

    def p1(jj, carry):
        hr, hi = carry
        off = _slab(SEG - 1 - jj if desc else jj)
        pr, pi_ = _cmul(ar, ai, hr, hi)
        nr = pr + xr[pl.ds(off, 8), :]
        ni = pi_ + xi[pl.ds(off, 8), :]
        xr[pl.ds(off, 8), :] = nr
        xi[pl.ds(off, 8), :] = ni
        return nr, ni

    zero = jnp.zeros((8, cs), F32)
    ir, ii = lax.fori_loop(0, SEG, p1, (zero, zero), unroll=S5_UNROLL)
    cin_r, cin_i = car_r[...], car_i[...]
    rows = lax.broadcasted_iota(jnp.int32, (8, cs), 0)
    for s, krow in ((1, SEG), (2, SEG + 1), (4, SEG + 3)):
        keep, sh = (rows < 8 - s, 8 - s) if desc else (rows >= s, s)
        sr = jnp.where(keep, pltpu.roll(ir, sh, 0), 0.0)
        si = jnp.where(keep, pltpu.roll(ii, sh, 0), 0.0)
        pr, pi_ = _cmul(tre_ref[krow:krow + 1, :], sg * tim_ref[krow:krow + 1, :], sr, si)
        ir, ii = ir + pr, ii + pi_
    q0 = SEG + 8 if desc else SEG
    pr, pi_ = _cmul(tre_ref[q0:q0 + 8, :], sg * tim_ref[q0:q0 + 8, :], cin_r, cin_i)
    fr, fi = ir + pr, ii + pi_
    keep, sh, edge = (rows < 7, 7, 0) if desc else (rows >= 1, 1, 7)
    cs_r = jnp.where(keep, pltpu.roll(fr, sh, 0), cin_r)
    cs_i = jnp.where(keep, pltpu.roll(fi, sh, 0), cin_i)
    car_r[...] = _bcast8(fr[edge:edge + 1, :], cs)
    car_i[...] = _bcast8(fi[edge:edge + 1, :], cs)
    return cs_r, cs_i, cin_r, cin_i


def _pow_row(tre_ref, tim_ref, j, desc, conj, cs):
    k = SEG - 1 - j if desc else j
    sg = -1.0 if conj else 1.0
    return _bcast8(tre_ref[pl.ds(k, 1), :], cs), sg * _bcast8(tim_ref[pl.ds(k, 1), :], cs)


def _to_segments(val, dst):
    for r in range(8):
        dst[pl.ds(r, SEG, stride=8), :] = val[r * SEG:(r + 1) * SEG, :]


def _from_segments(src):
    return jnp.concatenate([src[pl.ds(r, SEG, stride=8), :] for r in range(8)], axis=0)


def _s5_fwd(up, bre, bim, cmr, cmi, tre, tim, dskip, order, nproc, desc, with_skip, name):
    t, cu = up.shape
    nsb = bre.shape[0]
    cb = cu // nsb
    cs = bre.shape[2]
    nch = t // TC

    def body(u_ref, bre_ref, bim_ref, cmr_ref, cmi_ref, tre_ref, tim_ref, dsk_ref,
             y_ref, hr_ref, hi_ref, cinr_ref, cini_ref, car_r, car_i, seg_scr):
        i = pl.program_id(0)

        @pl.when(i == 0)
        def _():
            car_r[...] = jnp.zeros_like(car_r)
            car_i[...] = jnp.zeros_like(car_i)

        for sb in range(nsb):
            col, ch = pl.ds(sb * cs, cs), slice(sb * cb, (sb + 1) * cb)
            hr_v, hi_v = hr_ref.at[:, col], hi_ref.at[:, col]
            tre_v, tim_v = tre_ref.at[:, col], tim_ref.at[:, col]
            useg, yseg = seg_scr.at[0, sb], seg_scr.at[1, sb]
            _to_segments(u_ref[:, ch], useg)
            u = useg[...]
            ub = u.astype(BF16)
            hr_v[...] = _dot(ub, bre_ref[sb])
            hi_v[...] = _dot(ub, bim_ref[sb])
            cs_r, cs_i, cin_r, cin_i = _scan_head(hr_v, hi_v, tre_v, tim_v, car_r.at[sb], car_i.at[sb], desc, False, cs)
            cinr_ref[:, col] = cin_r
            cini_ref[:, col] = cin_i

            def p2(j, _, hr_v=hr_v, hi_v=hi_v, tre_v=tre_v, tim_v=tim_v, cs_r=cs_r, cs_i=cs_i):
                off = _slab(j)
                pw_r, pw_i = _pow_row(tre_v, tim_v, j, desc, False, cs)
                pr, pi_ = _cmul(pw_r, pw_i, cs_r, cs_i)
                hr_v[pl.ds(off, 8), :] = hr_v[pl.ds(off, 8), :] + pr
                hi_v[pl.ds(off, 8), :] = hi_v[pl.ds(off, 8), :] + pi_
                return 0

            lax.fori_loop(0, SEG, p2, 0, unroll=S5_UNROLL)
            y = _dot(hr_v[...].astype(BF16), cmr_ref[sb]) - _dot(hi_v[...].astype(BF16), cmi_ref[sb])
            if with_skip:
                y = y + dsk_ref[:, ch] * u
            yseg[...] = y
            y_ref[:, ch] = _from_segments(yseg)

    blk = lambda i: (order(i), 0)
    return _pcall(
        body, name=name, grid=(nproc,),
        out_shape=(jax.ShapeDtypeStruct((t, cu), F32), jax.ShapeDtypeStruct((t, nsb * cs), F32),
                   jax.ShapeDtypeStruct((t, nsb * cs), F32), jax.ShapeDtypeStruct((nch * 8, nsb * cs), F32),
                   jax.ShapeDtypeStruct((nch * 8, nsb * cs), F32)),
        in_specs=[pl.BlockSpec((TC, cu), blk), _const(bre.shape), _const(bim.shape), _const(cmr.shape),
                  _const(cmi.shape), _const(tre.shape), _const(tim.shape), _const(dskip.shape)],
        out_specs=(pl.BlockSpec((TC, cu), blk), pl.BlockSpec((TC, nsb * cs), blk), pl.BlockSpec((TC, nsb * cs), blk),
                   pl.BlockSpec((8, nsb * cs), blk), pl.BlockSpec((8, nsb * cs), blk)),
        scratch_shapes=[pltpu.VMEM((nsb, 8, cs), F32), pltpu.VMEM((nsb, 8, cs), F32),
                        pltpu.VMEM((2, nsb, TC, cb), F32)],
        compiler_params=_params(("arbitrary",)),
    )(up, bre, bim, cmr, cmi, tre, tim, dskip)


def _s5_bwd(dy_lat, up, hr, hi, cinr, cini, bre, bim, ctr, cti, tre, tim, dskip, order, nproc, desc, with_skip, name,
            xchg=None):
    t, cu = up.shape
    nsb = bre.shape[0]
    cb = cu // nsb
    cs = bre.shape[2]
    lch = dy_lat.shape[0] // TC
    adesc = not desc

    def body(dy_ref, u_ref, hr_ref, hi_ref, cinr_ref, cini_ref, bre_ref, bim_ref, ctr_ref, cti_ref, tre_ref, tim_ref,
             dsk_ref, du_ref, dar_ref, dai_ref, dcr_ref, dci_ref, dbr_ref, dbi_ref, dd_ref,
             car_r, car_i, mr_all, mi_all, acc_r, acc_i, seg_scr):
        i = pl.program_id(0)
        latent = (order(i) < lch).astype(F32)

        @pl.when(i == 0)
        def _():
            for ref in (car_r, car_i, acc_r, acc_i, dcr_ref, dci_ref, dbr_ref, dbi_ref, dd_ref):
                ref[...] = jnp.zeros_like(ref)

        rows = lax.broadcasted_iota(jnp.int32, (8, cs), 0)
        for sb in range(nsb):
            col, ch = pl.ds(sb * cs, cs), slice(sb * cb, (sb + 1) * cb)
            hr_v, hi_v = hr_ref.at[:, col], hi_ref.at[:, col]
            tre_v, tim_v = tre_ref.at[:, col], tim_ref.at[:, col]
            mr, mi = mr_all.at[sb % 2], mi_all.at[sb % 2]
            useg, dyseg, duseg = seg_scr.at[0, sb], seg_scr.at[1, sb], seg_scr.at[2, sb]
            _to_segments(dy_ref[:, ch] * latent, dyseg)
            _to_segments(u_ref[:, ch], useg)
            dy = dyseg[...]
            dyb = dy.astype(BF16)
            u = useg[...]
            ub = u.astype(BF16)
            mr[...] = _dot(dyb, ctr_ref[sb])
            mi[...] = -_dot(dyb, cti_ref[sb])
            cs_r, cs_i, _, _ = _scan_head(mr, mi, tre_v, tim_v, car_r.at[sb], car_i.at[sb], adesc, True, cs)

            def fix(j, hp_r, hp_i, acc, mr=mr, mi=mi, tre_v=tre_v, tim_v=tim_v, cs_r=cs_r, cs_i=cs_i):
                off = _slab(j)
                pw_r, pw_i = _pow_row(tre_v, tim_v, j, adesc, True, cs)
                pr, pi_ = _cmul(pw_r, pw_i, cs_r, cs_i)
                m_r = mr[pl.ds(off, 8), :] + pr
                m_i = mi[pl.ds(off, 8), :] + pi_
                mr[pl.ds(off, 8), :] = m_r
                mi[pl.ds(off, 8), :] = m_i
                a_r, a_i = acc
                return a_r + hp_r * m_r + hp_i * m_i, a_i + hp_r * m_i - hp_i * m_r

            edge_j, src, keep, sh = (SEG - 1, 0, rows < 7, 7) if desc else (0, (SEG - 1) * 8, rows >= 1, 1)
            h0r = jnp.where(keep, pltpu.roll(hr_v[src:src + 8, :], sh, 0), cinr_ref[:, col])
            h0i = jnp.where(keep, pltpu.roll(hi_v[src:src + 8, :], sh, 0), cini_ref[:, col])
            acc = fix(edge_j, h0r, h0i, (acc_r[sb], acc_i[sb]))

            def p2(jj, acc, fix=fix, hr_v=hr_v, hi_v=hi_v):
                j = jj if desc else jj + 1
                offp = _slab(j + 1 if desc else j - 1)
                return fix(j, hr_v[pl.ds(offp, 8), :], hi_v[pl.ds(offp, 8), :], acc)

            a_r, a_i = lax.fori_loop(0, SEG - 1, p2, acc, unroll=S5_UNROLL)
            acc_r[sb] = a_r
            acc_i[sb] = a_i

            mrb, mib = mr[...].astype(BF16), mi[...].astype(BF16)
            du = _dot_nt(mrb, bre_ref[sb]) + _dot_nt(mib, bim_ref[sb])
            if with_skip:
                du = du + dsk_ref[:, ch] * dy
                dd_ref[:, ch] += _colsum(dy * u)
            duseg[...] = du
            du_ref[:, ch] = _from_segments(duseg)
            dbr_ref[sb] += _dot_tn(ub, mrb)
            dbi_ref[sb] += _dot_tn(ub, mib)
            dcr_ref[sb] += _dot_tn(dyb, hr_v[...].astype(BF16))
            dci_ref[sb] -= _dot_tn(dyb, hi_v[...].astype(BF16))

            @pl.when(i == nproc - 1)
            def _(sb=sb, a_r=a_r, a_i=a_i):
                dar_ref[sb] = _colsum(a_r)
                dai_ref[sb] = _colsum(a_i)

    blk = lambda i: (order(i), 0)
    blk_dy = lambda i: (jnp.minimum(order(i), lch - 1), 0)
    mat = (nsb, cb, cs)
    return _pcall(
        body, name=name, grid=(nproc,), xchg=xchg, edges=_edges1(nproc),
        out_shape=(jax.ShapeDtypeStruct((t, cu), F32),
                   jax.ShapeDtypeStruct((nsb, 1, cs), F32), jax.ShapeDtypeStruct((nsb, 1, cs), F32),
                   jax.ShapeDtypeStruct(mat, F32), jax.ShapeDtypeStruct(mat, F32),
                   jax.ShapeDtypeStruct(mat, F32), jax.ShapeDtypeStruct(mat, F32),
                   jax.ShapeDtypeStruct((1, cu), F32)),
        in_specs=[pl.BlockSpec((TC, cu), blk_dy), pl.BlockSpec((TC, cu), blk), pl.BlockSpec((TC, nsb * cs), blk),
                  pl.BlockSpec((TC, nsb * cs), blk), pl.BlockSpec((8, nsb * cs), blk), pl.BlockSpec((8, nsb * cs), blk),
                  _const(mat), _const(mat), _const(mat), _const(mat), _const(tre.shape), _const(tim.shape),
                  _const((1, cu))],
        out_specs=(pl.BlockSpec((TC, cu), blk), _const((nsb, 1, cs)), _const((nsb, 1, cs)),
                   _const(mat), _const(mat), _const(mat), _const(mat), _const((1, cu))),
        scratch_shapes=[pltpu.VMEM((nsb, 8, cs), F32), pltpu.VMEM((nsb, 8, cs), F32), pltpu.VMEM((2, TC, cs), F32),
                        pltpu.VMEM((2, TC, cs), F32), pltpu.VMEM((nsb, 8, cs), F32), pltpu.VMEM((nsb, 8, cs), F32),
                        pltpu.VMEM((3, nsb, TC, cb), F32)],
        compiler_params=_params(("arbitrary",)),
    )(dy_lat, up, hr, hi, cinr, cini, bre, bim, ctr, cti, tre, tim, dskip)


def _gelu(x):
    k = math.sqrt(2.0 / math.pi)
    return 0.5 * x * (1.0 + jnp.tanh(k * (x + 0.044715 * (x * x * x))))


def _gelu_grad(x):
    k = math.sqrt(2.0 / math.pi)
    th = jnp.tanh(k * (x + 0.044715 * (x * x * x)))
    return 0.5 * (1.0 + th) + 0.5 * x * (1.0 - th * th) * (k * (1.0 + 3.0 * 0.044715 * (x * x)))


def _mix_out_fwd(yf, yb, dpool, x1, modtab, w_pool, pscale, w_glu, w_out):
    l, hm = dpool.shape
    d = x1.shape[1]

    def body(yf_ref, yb_ref, dp_ref, x_ref, mod_ref, wp_ref, ps_ref, wg_ref, wo_ref, x2_ref, yp_ref, cat_ref, mix_ref):
        ypre = yf_ref[...] + yb_ref[...]
        yp_ref[...] = ypre
        yg = _gelu(ypre)
        y2 = yg * _sigmoid(_dot(yg.astype(BF16), wg_ref[...]))
        po = _dot(dp_ref[...].astype(BF16), wp_ref[...]) * ps_ref[...]
        cat = jnp.concatenate([po, y2], axis=1).astype(BF16)
        cat_ref[...] = cat
        mix = _dot(cat, wo_ref[...])
        mix_ref[...] = mix.astype(BF16)
        x2_ref[...] = x_ref[...] + mod_ref[0, 5:6, :] * mix

    row = lambda i: (i, 0)
    lrow = row
    return _pcall(
        body, name="mix_out_fwd", grid=(l // TM,),
        out_shape=(jax.ShapeDtypeStruct((l, d), F32), jax.ShapeDtypeStruct((l, hm), F32),
                   jax.ShapeDtypeStruct((l, d), BF16), jax.ShapeDtypeStruct((l, d), BF16)),
        in_specs=[pl.BlockSpec((TM, hm), lrow), pl.BlockSpec((TM, hm), lrow), pl.BlockSpec((TM, hm), row),
                  pl.BlockSpec((TM, d), lrow), _mod_spec(d, False), _const(w_pool.shape), _const((1, hm)),
                  _const(w_glu.shape), _resident(w_out.shape)],
        out_specs=(pl.BlockSpec((TM, d), row), pl.BlockSpec((TM, hm), row), pl.BlockSpec((TM, d), row),
                   pl.BlockSpec((TM, d), row)),
        compiler_params=_params(("arbitrary",)),
    )(yf, yb, dpool, x1, modtab, w_pool, pscale, w_glu, w_out)


def _mix_out_bwd(dx2, mix, cat, ypre, dpool, modtab, w_pool, pscale, w_glu, w_out):
    l, hm = dpool.shape
    d = dx2.shape[1]
    nt = l // TML

    def body(dx_ref, mix_ref, cat_ref, yp_ref, dp_ref, mod_ref, wp_ref, ps_ref, wg_ref, wo_ref,
             dyp_ref, ddp_ref, dwo_ref, dwg_ref, dwp_ref, dps_ref, dgate_ref, acc_o, acc_g):
        i = pl.program_id(0)

        @pl.when(i == 0)
        def _():
            for ref in (acc_o, acc_g, dwp_ref, dps_ref, dgate_ref):
                ref[...] = jnp.zeros_like(ref)

        dx = dx_ref[...]
        dgate_ref[...] += _colsum(dx * mix_ref[...].astype(F32))
        dmix = (dx * mod_ref[0, 5:6, :]).astype(BF16)
        dcat = _dot_nt(dmix, wo_ref[...])
        acc_o[...] += _dot_tn(cat_ref[...], dmix)
        dpo, dy2 = dcat[:, :hm], dcat[:, hm:]
        dpb = dp_ref[...].astype(BF16)
        pp = _dot(dpb, wp_ref[...])
        dps_ref[...] += _colsum(dpo * pp)
        dpp = (dpo * ps_ref[...]).astype(BF16)
        ddp_ref[...] = _dot_nt(dpp, wp_ref[...])
        dwp_ref[...] += _dot_tn(dpb, dpp)
        ypre = yp_ref[...]
        yg = _gelu(ypre)
        ygb = yg.astype(BF16)
        s = _sigmoid(_dot(ygb, wg_ref[...]))
        dq = (dy2 * yg * s * (1.0 - s)).astype(BF16)
        dyg = dy2 * s + _dot_nt(dq, wg_ref[...])
        acc_g[...] += _dot_tn(ygb, dq)
        dyp_ref[...] = dyg * _gelu_grad(ypre)

        @pl.when(i == nt - 1)
        def _():
            dwo_ref[...] = acc_o[...].astype(BF16)
            dwg_ref[...] = acc_g[...].astype(BF16)

    row = lambda i: (i, 0)
    return _pcall(
        body, name="mix_out_bwd", grid=(nt,),
        out_shape=(jax.ShapeDtypeStruct((l, hm), F32), jax.ShapeDtypeStruct((l, hm), F32),
                   jax.ShapeDtypeStruct((d, d), BF16), jax.ShapeDtypeStruct((hm, hm), BF16),
                   jax.ShapeDtypeStruct((hm, hm), F32), jax.ShapeDtypeStruct((1, hm), F32),
                   jax.ShapeDtypeStruct((1, d), F32)),
        in_specs=[pl.BlockSpec((TML, d), row), pl.BlockSpec((TML, d), row), pl.BlockSpec((TML, d), row),
                  pl.BlockSpec((TML, hm), row), pl.BlockSpec((TML, hm), row), _mod_spec(d, False),
                  _const(w_pool.shape), _const((1, hm)), _const(w_glu.shape), _resident(w_out.shape)],
        out_specs=(pl.BlockSpec((TML, hm), row), pl.BlockSpec((TML, hm), row), _const((d, d)), _const((hm, hm)),
                   _const((hm, hm)), _const((1, hm)), _const((1, d))),
        scratch_shapes=[pltpu.VMEM((d, d), F32), pltpu.VMEM((hm, hm), F32)],
        compiler_params=_params(("arbitrary",)),
    )(dx2, mix, cat, ypre, dpool, modtab, w_pool, pscale, w_glu, w_out)


def _loss_head(xx, target, gg, loss_ref, dg_ref):
    d = xx.shape[-1]
    rstd = lax.rsqrt(jnp.mean(xx * xx, axis=-1, keepdims=True) + EPS)
    xhat = xx * rstd
    err = xhat * gg - target
    row_loss = jnp.mean(err * err, axis=-1, keepdims=True)
    loss_ref[...] += 0.5 * jnp.sum(row_loss, axis=0, keepdims=True)
    dout = err * (1.0 / d)
    dg_ref[...] += _colsum(dout * xhat)
    dxhat = dout * gg
    return rstd * (dxhat - xhat * jnp.mean(dxhat * xhat, axis=-1, keepdims=True))


def _adamw(parts, w, m, v, name):
    shape = w.shape
    c = shape[-1]
    r = int(np.prod(shape)) // c
    npart = parts.shape[0]
    tr = r
    for cand in (1024, 512, 256, 128, 64, 32, 16):
        if r % cand == 0 and cand * max(c, LANE) * 4 <= ADAM_BLOCK_BYTES:
            tr = cand
            break
    c1 = 1.0 - ADAM_B1
    c2 = 1.0 - ADAM_B2
    bc1 = 1.0 - ADAM_B1 ** ADAM_STEP
    bc2 = 1.0 - ADAM_B2 ** ADAM_STEP

    def body(p_ref, w_ref, m_ref, v_ref, g_ref, d_ref, mo_ref, vo_ref):
        g = p_ref[0].astype(F32)
        for k in range(1, npart):
            g = g + p_ref[k].astype(F32)
        mn = ADAM_B1 * m_ref[...] + c1 * g
        vn = ADAM_B2 * v_ref[...] + c2 * (g * g)
        m_hat = mn / bc1
        v_hat = vn / bc2
        g_ref[...] = g
        mo_ref[...] = mn
        vo_ref[...] = vn
        d_ref[...] = -ADAM_LR * (m_hat / (jnp.sqrt(v_hat) + ADAM_EPS) + ADAM_WD * w_ref[...])

    spec = pl.BlockSpec((tr, c), lambda i: (i, 0))
    outs = _pcall(
        body, name=name, grid=(r // tr,),
        out_shape=tuple(jax.ShapeDtypeStruct((r, c), F32) for _ in range(4)),
        in_specs=[pl.BlockSpec((npart, tr, c), lambda i: (0, i, 0)), spec, spec, spec],
        out_specs=(spec, spec, spec, spec),
        compiler_params=_params(("arbitrary",)),
    )(parts.reshape(npart, r, c), w.reshape(r, c), m.reshape(r, c), v.reshape(r, c))
    return tuple(o.reshape(shape) for o in outs)


def _adamw_ffn(parts, row0, w, m, v, name):
    _, nl, r, c = w.shape
    rp = parts[0][0].shape[1]
    tc = ADAM_FFN_LANES
    grid = (c // tc,)
    p_spec = lambda a: pl.BlockSpec((a.shape[0], rp, tc), lambda i: (0, 0, i))
    w_spec = pl.BlockSpec((1, nl, r, tc), lambda i: (0, 0, 0, i))
    take = lambda ref, k: ref[k, row0:row0 + r, :]
    flat = [a for layer in parts for a in layer]
    counts = [len(layer) for layer in parts]
    c1 = 1.0 - ADAM_B1
    c2 = 1.0 - ADAM_B2
    bc1 = 1.0 - ADAM_B1 ** ADAM_STEP
    bc2 = 1.0 - ADAM_B2 ** ADAM_STEP

    def body(*refs):
        w_ref, m_ref, v_ref, g_ref, d_ref, mo_ref, vo_ref = refs[len(flat):]
        first = 0
        for lyr in range(nl):
            g = None
            for ref in refs[first:first + counts[lyr]]:
                for k in range(ref.shape[0]):
                    term = take(ref, k).astype(F32)
                    g = term if g is None else g + term
            first += counts[lyr]
            mn = ADAM_B1 * m_ref[0, lyr] + c1 * g
            vn = ADAM_B2 * v_ref[0, lyr] + c2 * (g * g)
            g_ref[0, lyr] = g
            mo_ref[0, lyr] = mn
            vo_ref[0, lyr] = vn
            d_ref[0, lyr] = -ADAM_LR * ((mn / bc1) / (jnp.sqrt(vn / bc2) + ADAM_EPS) + ADAM_WD * w_ref[0, lyr])

    return _pcall(
        body, name=name, grid=grid, out_shape=tuple(jax.ShapeDtypeStruct(w.shape, F32) for _ in range(4)),
        in_specs=[p_spec(a) for a in flat] + [w_spec] * 3, out_specs=(w_spec,) * 4,
        compiler_params=_params(("arbitrary",)),
    )(*flat, w, m, v)


def _blockdiag(xb):
    n, a, b = xb.shape[-3:]
    eye = jnp.eye(n, dtype=xb.dtype)
    out = xb[..., :, :, None, :] * eye[:, None, :, None]
    return out.reshape(xb.shape[:-3] + (n * a, n * b))


def _diagblocks(mat, n):
    lead = mat.shape[:-2]
    a, b = mat.shape[-2] // n, mat.shape[-1] // n
    m5 = mat.reshape(lead + (n, a, n, b))
    nl = len(lead)
    dg = jnp.diagonal(m5, axis1=nl, axis2=nl + 2)
    return jnp.moveaxis(dg, -1, nl)


def _pad128(a):
    flat = a.reshape(-1)
    pad = (-flat.shape[0]) % LANE
    return jnp.pad(flat, (0, pad)) if pad else flat


def kernel(x, c, ctx, c_ctx, norm_g, w_ada, b_ada, ffn_w1, ffn_w3, ffn_w2, w_in, pool_w, pool_scale, s5_a_re, s5_a_im, s5_log_dt, s5_b_re, s5_b_im, s5_c_re, s5_c_im, s5_d, s5_w_glu, w_out, final_g, loss_target, m_c_ctx, m_norm_g, m_w_ada, m_b_ada, m_ffn_w1, m_ffn_w3, m_ffn_w2, m_w_in, m_pool_w, m_pool_scale, m_s5_a_re, m_s5_a_im, m_s5_log_dt, m_s5_b_re, m_s5_b_im, m_s5_c_re, m_s5_c_im, m_s5_d, m_s5_w_glu, m_w_out, m_final_g, v_c_ctx, v_norm_g, v_w_ada, v_b_ada, v_ffn_w1, v_ffn_w3, v_ffn_w2, v_w_in, v_pool_w, v_pool_scale, v_s5_a_re, v_s5_a_im, v_s5_log_dt, v_s5_b_re, v_s5_b_im, v_s5_c_re, v_s5_c_im, v_s5_d, v_s5_w_glu, v_w_out, v_final_g):
    weights = dict(c_ctx=c_ctx, norm_g=norm_g, w_ada=w_ada, b_ada=b_ada, ffn_w1=ffn_w1, ffn_w3=ffn_w3, ffn_w2=ffn_w2,
                   w_in=w_in, pool_w=pool_w, pool_scale=pool_scale, s5_a_re=s5_a_re, s5_a_im=s5_a_im,
                   s5_log_dt=s5_log_dt, s5_b_re=s5_b_re, s5_b_im=s5_b_im, s5_c_re=s5_c_re, s5_c_im=s5_c_im, s5_d=s5_d,
                   s5_w_glu=s5_w_glu, w_out=w_out, final_g=final_g)
    mom_m = dict(c_ctx=m_c_ctx, norm_g=m_norm_g, w_ada=m_w_ada, b_ada=m_b_ada, ffn_w1=m_ffn_w1, ffn_w3=m_ffn_w3,
                 ffn_w2=m_ffn_w2, w_in=m_w_in, pool_w=m_pool_w, pool_scale=m_pool_scale, s5_a_re=m_s5_a_re,
                 s5_a_im=m_s5_a_im, s5_log_dt=m_s5_log_dt, s5_b_re=m_s5_b_re, s5_b_im=m_s5_b_im, s5_c_re=m_s5_c_re,
                 s5_c_im=m_s5_c_im, s5_d=m_s5_d, s5_w_glu=m_s5_w_glu, w_out=m_w_out, final_g=m_final_g)
    mom_v = dict(c_ctx=v_c_ctx, norm_g=v_norm_g, w_ada=v_w_ada, b_ada=v_b_ada, ffn_w1=v_ffn_w1, ffn_w3=v_ffn_w3,
                 ffn_w2=v_ffn_w2, w_in=v_w_in, pool_w=v_pool_w, pool_scale=v_pool_scale, s5_a_re=v_s5_a_re,
                 s5_a_im=v_s5_a_im, s5_log_dt=v_s5_log_dt, s5_b_re=v_s5_b_re, s5_b_im=v_s5_b_im, s5_c_re=v_s5_c_re,
                 s5_c_im=v_s5_c_im, s5_d=v_s5_d, s5_w_glu=v_s5_w_glu, w_out=v_w_out, final_g=v_final_g)

    l, d = x.shape[1], x.shape[2]
    lc = ctx.shape[1]
    t = l + TM
    rows = l // GRID_W
    fblk = ffn_w1.shape[-1]
    ngrp, gp = s5_a_re.shape[2], s5_a_re.shape[3]
    gc = s5_b_re.shape[3]
    hm = ngrp * gc
    nsb = 4
    gsb = ngrp // nsb
    assert lc == TC and TM % TC == 0 and l % TM == 0 and TM == TML and hm == 4 * LANE and ngrp * gp == nsb * 512
    me = 4 * lax.axis_index("x") + 2 * lax.axis_index("y") + lax.axis_index("c")

    padr = ((0, FPAD - fblk), (0, 0))
    w1_t, w3_t = jnp.swapaxes(ffn_w1, 2, 3), jnp.swapaxes(ffn_w3, 2, 3)
    w13_loc = [jnp.concatenate([jnp.pad(w1_t[0, k], padr), jnp.pad(w3_t[0, k], padr)], axis=0).astype(BF16)
               for k in range(2)]
    w2_loc = [jnp.pad(ffn_w2[0, k], padr).astype(BF16) for k in range(2)]
    wa = w_ada.shape[-1]
    b_blk = lax.dynamic_slice_in_dim(b_ada, me * wa, wa, axis=1)
    w13_0, w2_0, g_all, c_all, mod_all = _front(w13_loc[0], w2_loc[0], norm_g[0], c, c_ctx.reshape(1, d), w_ada[0],
                                                b_blk)
    w2_0 = w2_0.reshape(NDEV // 2, 2 * FPAD, d)
    c_all = c_all.reshape(NDEV, d)
    g_all = g_all.transpose(1, 0, 2).reshape(3, d)

    c16 = jnp.concatenate([c_all, c_ctx.reshape(1, d), jnp.zeros((7, d), F32)], axis=0)
    mod_full = mod_all.transpose(1, 0, 2).reshape(16, NDEV * wa)
    mod_l = lax.dynamic_index_in_dim(mod_full, me, axis=0, keepdims=False).reshape(9, d)
    modtab = jnp.stack([mod_full[8].reshape(9, d), mod_l])

    x0 = [jnp.pad(ctx[0], ((0, TM - lc), (0, 0))), x[0]]
    g0, g1, g2 = g_all[0:1], g_all[1:2], g_all[2:3]
    x1, h0, ab0, o0, w13_1, w2_1, w_in_f, w_out_f, w_glu_f = _ffn_fwd(
        x0, modtab, g0, w13_0, w2_0, 0, True, "ffn0_fwd",
        xchg=[(w13_loc[1], GATHER_REL), (w2_loc[1], GATHER_REL), (w_in[0].astype(BF16), GATHER),
              (w_out[0].astype(BF16), GATHER), (s5_w_glu[0].astype(BF16), GATHER)])
    w2_1 = w2_1.reshape(NDEV // 2, 2 * FPAD, d)
    w_in_f = w_in_f.reshape(d, -1)
    w_out_f = w_out_f.reshape(-1, d)
    w_glu_f = w_glu_f.reshape(hm, hm)
    h1, u_pool, up = _mix_in_fwd(x1, modtab, g1, w_in_f)

    dpool = _pool(u_pool, rows, False, "pool_fwd")
    w_pool_bd = _blockdiag(pool_w[0]).astype(BF16)

    q = 2 * ngrp
    kvec = np.concatenate([np.arange(1, SEG + 1), SEG * np.arange(1, 9), SEG * np.arange(8, 0, -1)]).astype(np.float32)
    a_re3, a_im3 = s5_a_re[0].reshape(q, 1, gp), s5_a_im[0].reshape(q, 1, gp)
    ldt3 = jnp.broadcast_to(s5_log_dt[0].reshape(q, 1, 1), (q, 1, gp))
    c_re3, c_im3 = s5_c_re[0].reshape(q, gc, gp), s5_c_im[0].reshape(q, gc, gp)
    cp_re, cp_im, tab_re, tab_im = _s5_prep(a_re3, a_im3, ldt3, c_re3, c_im3, jnp.broadcast_to(jnp.asarray(kvec).reshape(1, NTAB, 1), (1, NTAB, gp)))
    tabs = [(tab_re.reshape(2, ngrp, NTAB, gp)[k].transpose(1, 0, 2).reshape(NTAB, ngrp * gp),
             tab_im.reshape(2, ngrp, NTAB, gp)[k].transpose(1, 0, 2).reshape(NTAB, ngrp * gp)) for k in range(2)]
    b_t = lambda b: _blockdiag(b[0].reshape(nsb, gsb, gp, gc).transpose(0, 1, 3, 2)).astype(BF16)
    bre, bim = b_t(s5_b_re), b_t(s5_b_im)
    cp_t = lambda cp: _blockdiag(cp.reshape(2, nsb, gsb, gc, gp)).astype(BF16)
    ctr, cti = cp_t(cp_re), cp_t(cp_im)
    cmr, cmi = jnp.swapaxes(ctr, -1, -2), jnp.swapaxes(cti, -1, -2)
    dskip = s5_d

    lch = l // TC
    nproc = lch + 1
    order_f = lambda i: jnp.where(i == 0, lch, i - 1)
    order_b = lambda i: jnp.where(i == 0, lch, lch - i)
    rorder_f = lambda i: jnp.where(i == lch, lch, lch - 1 - i)
    rorder_b = lambda i: i
    y_f, hr_f, hi_f, cinr_f, cini_f = _s5_fwd(up, bre, bim, cmr[0], cmi[0], tabs[0][0], tabs[0][1], dskip,
                                               order_f, nproc, False, True, "s5_fwd_f")
    y_b, hr_b, hi_b, cinr_b, cini_b = _s5_fwd(up, bre, bim, cmr[1], cmi[1], tabs[1][0], tabs[1][1], dskip,
                                              order_b, nproc, True, False, "s5_fwd_b")

    x2, ypre, cat, mix = _mix_out_fwd(y_f, y_b, dpool, x1, modtab, w_pool_bd, pool_scale, w_glu_f, w_out_f)
    dx3, h2, ab2, o2, loss_part, dfinal_g = _ffn_fwd([x2], modtab, g2, w13_1, w2_1, 6, False, "ffn1_fwd",
                                                     head=(loss_target[0], final_g.reshape(1, d)))

    dhp2, d13_1, d2_1 = _ffn_bwd(dx3, modtab, h2, ab2, w13_1, w2_1, 6, False, "ffn1_bwd")
    dx2, dmod_678, dg2 = _ffn_bwd_norm(dx3, [dhp2], [x2], o2, modtab, g2, 6, False, "ffn1_bwd_norm")
    dypre, ddpool, dw_out, dw_glu, dw_pool_bd, dpscale, dgate5 = _mix_out_bwd(
        dx2, mix, cat, ypre, dpool, modtab, w_pool_bd, pool_scale, w_glu_f, w_out_f)

    du_f, dar_f, dai_f, dcr_f, dci_f, dbr_f, dbi_f, dd_skip, r13_1 = _s5_bwd(
        dypre, up, hr_f, hi_f, cinr_f, cini_f, bre, bim, ctr[0], cti[0], tabs[0][0], tabs[0][1], dskip,
        rorder_f, nproc, False, True, "s5_bwd_f",
        xchg=[(d13_1, A2A_REL)])
    du_b, dar_b, dai_b, dcr_b, dci_b, dbr_b, dbi_b, _, r2_1, r_out, r_glu = _s5_bwd(
        dypre, up, hr_b, hi_b, cinr_b, cini_b, bre, bim, ctr[1], cti[1], tabs[1][0], tabs[1][1], dskip,
        rorder_b, nproc, True, False, "s5_bwd_b",
        xchg=[(d2_1.reshape(NDEV, FPAD, d), A2A_REL), (dw_out.reshape(NDEV, -1, d), A2A),
              (dw_glu.reshape(NDEV, hm // NDEV, hm), A2A)])

    dab_r = jnp.stack([dar_f, dar_b]).reshape(q, 1, gp)
    dab_i = jnp.stack([dai_f, dai_b]).reshape(q, 1, gp)
    dcp_r = _diagblocks(jnp.stack([dcr_f, dcr_b]), gsb).reshape(q, gc, gp)
    dcp_i = _diagblocks(jnp.stack([dci_f, dci_b]), gsb).reshape(q, gc, gp)
    ga_re, ga_im, gldt, gc_re, gc_im = _s5_param_bwd(a_re3, a_im3, ldt3, c_re3, c_im3, dab_r, dab_i, dcp_r, dcp_i)
    gb_re = (_diagblocks(dbr_f, gsb) + _diagblocks(dbr_b, gsb)).transpose(0, 1, 3, 2).reshape(ngrp, gp, gc)
    gb_im = (_diagblocks(dbi_f, gsb) + _diagblocks(dbi_b, gsb)).transpose(0, 1, 3, 2).reshape(ngrp, gp, gc)

    def pack(small):
        offs, pieces, off = {}, [], 0
        for k_, a_ in small.items():
            p_ = _pad128(a_.astype(F32))
            offs[k_] = (off, int(np.prod(a_.shape)))
            off += p_.shape[0]
            pieces.append(p_)
        return jnp.concatenate(pieces).reshape(1, off), offs

    gw_pool = _diagblocks(dw_pool_bd, 4)
    bundle_a, offs = pack(dict(pool_w=gw_pool, pool_scale=dpscale, s5_a_re=ga_re, s5_a_im=ga_im,
                               s5_log_dt=gldt[:, 0, 0], s5_b_re=gb_re, s5_b_im=gb_im, s5_c_re=gc_re, s5_c_im=gc_im,
                               s5_d=dd_skip, final_g=dfinal_g))

    du_pool = _pool(ddpool, rows, True, "pool_bwd")
    dx1, dw_in, dmod_34, dg1 = _mix_in_bwd(du_pool, du_f, du_b, dx2, h1, x1, modtab, g1, w_in_f, lc)
    dhp0a, d13_0a, d2_0a, bund_a, r_in = _ffn_bwd(
        dx1, modtab, h0, ab0, w13_0, w2_0, 0, True, "ffn0_bwd_a", pair0=0, npair=2,
        xchg=[(bundle_a, GATHER), (dw_in.reshape(NDEV, d // NDEV, -1), A2A)])
    dhp0b, d13_0b, d2_0b, r13_0a, r2_0a = _ffn_bwd(
        dx1, modtab, h0, ab0, w13_0, w2_0, 0, True, "ffn0_bwd_b", pair0=2, npair=2,
        xchg=[(d13_0a, A2A_SAME), (d2_0a.reshape(NDEV // 2, FPAD, d), A2A_SAME)])
    dx0, dmod_012, dg0, r13_0b, r2_0b = _ffn_bwd_norm(
        dx1, [dhp0a, dhp0b], x0, o0, modtab, g0, 0, True, "ffn0_bwd_norm",
        xchg=[(d13_0b, A2A_OTHER), (d2_0b.reshape(NDEV // 2, FPAD, d), A2A_OTHER)])
    grad_x = dx0.reshape(1, l, d)

    dmod_c = jnp.concatenate([dmod_012[0], dmod_34[0], jnp.zeros((4, d), F32)], axis=0)
    dmod_l = jnp.concatenate([dmod_012[1], dmod_34[1], dgate5, dmod_678[0]], axis=0)
    bundle_b, offs_b = pack(dict(norm_g=jnp.concatenate([dg0, dg1, dg2], axis=0), dmod_l=dmod_l, dmod_c=dmod_c,
                                 loss=loss_part[:, :1]))
    bund_b = _exchange([(bundle_b, GATHER)], "exchange_small")[0]
    bunds = {**{k_: (bund_a.reshape(NDEV, -1), v_) for k_, v_ in offs.items()},
             **{k_: (bund_b.reshape(NDEV, -1), v_) for k_, v_ in offs_b.items()}}
    r13, r2 = [(r13_0a, r13_0b), (r13_1,)], [(r2_0a, r2_0b), (r2_1,)]

    def piece(name):
        b_, (o_, n_) = bunds[name]
        return b_[:, o_:o_ + n_]

    loss = jnp.sum(piece("loss")[:, 0])

    dl_all = lax.dynamic_slice_in_dim(piece("dmod_l"), me * wa, wa, axis=1)
    dc_all = lax.dynamic_slice_in_dim(piece("dmod_c"), me * wa, wa, axis=1)
    g_wada, gc_part = _ada_bwd(c16, w_ada[0], dl_all, dc_all)
    gc_all = _exchange([(gc_part, GATHER)], "exchange_cctx")[0]

    parts = {
        "c_ctx": gc_all.reshape(NDEV, d),
        "norm_g": lax.dynamic_slice_in_dim(piece("norm_g").reshape(NDEV, 3, d), me * (d // NDEV), d // NDEV,
                                           axis=2).reshape((NDEV,) + norm_g.shape),
        "w_ada": g_wada.reshape((1,) + w_ada.shape),
        "b_ada": jnp.concatenate([piece("dmod_l"), piece("dmod_c")], axis=0).reshape((2 * NDEV,) + b_ada.shape),
        "w_in": r_in.reshape((NDEV,) + w_in.shape),
        "s5_w_glu": r_glu.reshape((NDEV,) + s5_w_glu.shape),
        "w_out": r_out.reshape((NDEV,) + w_out.shape),
    }
    for k_ in ("pool_w", "pool_scale", "s5_a_re", "s5_a_im", "s5_log_dt", "s5_b_re", "s5_b_im", "s5_c_re", "s5_c_im",
               "s5_d", "final_g"):
        parts[k_] = piece(k_).reshape((NDEV,) + weights[k_].shape)

    grads, deltas, new_m, new_v = [], [], [], []
    ffn_parts = {"ffn_w1": (r13, 0, True), "ffn_w3": (r13, FPAD, True), "ffn_w2": (r2, 0, False)}
    for k_ in weights:
        if k_ in ffn_parts:
            shards, row0, transposed = ffn_parts[k_]
            flip = (lambda a: jnp.swapaxes(a, 2, 3)) if transposed else (lambda a: a)
            outs = _adamw_ffn(shards, row0, flip(weights[k_]), flip(mom_m[k_]), flip(mom_v[k_]), "adamw_" + k_)
            g_, d_, m_, v_ = (flip(o_) for o_ in outs)
        else:
            g_, d_, m_, v_ = _adamw(parts[k_], weights[k_], mom_m[k_], mom_v[k_], "adamw_" + k_)
        grads.append(g_)
        deltas.append(d_)
        new_m.append(m_)
        new_v.append(v_)
    return (loss, grad_x, *grads, *deltas, *new_m, *new_v)
```

```python
import functools
import math

import numpy as np
import jax
import jax.numpy as jnp
from jax import lax
from jax.experimental import pallas as pl
from jax.experimental.pallas import tpu as pltpu
from jax.experimental.pallas import tpu_sc as plsc

F32 = jnp.float32
BF16 = jnp.bfloat16
AXES = ("x", "y", "c")
NDEV = 8
EPS = 1e-6
TM = 512
TML = 512
TC = 256
SEG = TC // 8
NTAB = SEG + 16
S5_UNROLL = True
GRID_W = 64
POOL_WINDOWS = (2, 4, 8, 16)
LANE = 128
FPAD = 384
VMEM_LIMIT = 56 * 1024 * 1024
ADAM_BLOCK_BYTES = 1024 * 1024
ADAM_FFN_LANES = 256
SEQ_COLLECTIVE_ID = 0
ADAM_LR, ADAM_B1, ADAM_B2, ADAM_EPS, ADAM_WD, ADAM_STEP = 0.001, 0.9, 0.999, 1e-08, 0.01, 10


def _raw_call(body, kw):
    return pl.pallas_call(body, **kw)


def _pcall(body, xchg=None, edges=None, **kw):
    extra = ()
    if xchg:
        arrs, kinds = [a for a, _ in xchg], [k for _, k in xchg]
        n = len(arrs)
        n_in, n_out, n_scr = len(kw["in_specs"]), len(kw["out_shape"]), len(kw.get("scratch_shapes", ()))
        inner = body

        def hosted(*refs):
            a, b = n_in, n_in + n
            c_, d_ = b + n_out, b + n_out + n
            e = d_ + n_scr
            first, mid, last = edges()

            @pl.when(first)
            def _():
                _xchg_start(refs[a:b], refs[c_:d_], refs[e:], kinds)

            inner(*refs[:a], *refs[b:c_], *refs[d_:e])

            if any(k == GATHER_REL for k in kinds):
                @pl.when(mid)
                def _():
                    _xchg_relay(refs[a:b], refs[c_:d_], refs[e:], kinds)

            @pl.when(last)
            def _():
                _xchg_finish(refs[a:b], refs[c_:d_], refs[e:], kinds)

        body = hosted
        any_spec = pl.BlockSpec(memory_space=pl.ANY)
        kw = dict(kw, in_specs=list(kw["in_specs"]) + [any_spec] * n,
                  out_shape=tuple(kw["out_shape"]) + _xchg_shapes(arrs, kinds),
                  out_specs=tuple(kw["out_specs"]) + (any_spec,) * n,
                  scratch_shapes=list(kw.get("scratch_shapes", ())) + _xchg_sems(n))
        extra = tuple(arrs)
    call = _raw_call(body, kw)
    return (lambda *args: call(*args, *extra)) if extra else call


def _params(sem):
    return pltpu.CompilerParams(dimension_semantics=sem, vmem_limit_bytes=VMEM_LIMIT)


def _dot(a, b):
    return jnp.dot(a, b, preferred_element_type=F32)


def _dot_nt(a, b):
    return lax.dot_general(a, b, (((1,), (1,)), ((), ())), preferred_element_type=F32)


def _dot_tn(a, b):
    return lax.dot_general(a, b, (((0,), (0,)), ((), ())), preferred_element_type=F32)


def _sigmoid(x):
    return 1.0 / (1.0 + jnp.exp(-x))


def _colsum(a):
    return jnp.sum(a, axis=0, keepdims=True)


def _resident(shape):
    nd = len(shape)
    return pl.BlockSpec(shape, lambda *_: (0,) * nd, pipeline_mode=pl.Buffered(1))


def _const(shape):
    nd = len(shape)
    return pl.BlockSpec(shape, lambda *_: (0,) * nd)


GATHER = "gather"
GATHER_REL = "gather_rel"
A2A = "a2a"
A2A_REL = "a2a_rel"
A2A_SAME = "a2a_same"
A2A_OTHER = "a2a_other"


def _exchange(items, name):
    arrs, kinds = [a for a, _ in items], [k for _, k in items]
    n = len(arrs)

    def body(*refs):
        _xchg_start(refs[:n], refs[n:2 * n], refs[2 * n:], kinds)
        _xchg_relay(refs[:n], refs[n:2 * n], refs[2 * n:], kinds)
        _xchg_finish(refs[:n], refs[n:2 * n], refs[2 * n:], kinds)

    any_spec = pl.BlockSpec(memory_space=pl.ANY)
    outs = _pcall(
        body, name=name, out_shape=_xchg_shapes(arrs, kinds), in_specs=[any_spec] * n, out_specs=[any_spec] * n,
        scratch_shapes=_xchg_sems(n),
    )(*arrs)
    return list(outs)


def _sequencer_a2a_other(arrs, name):
    hbm = pltpu.MemorySpace.HBM
    srcs = [jax.new_ref(a, memory_space=hbm) for a in arrs]
    dsts = [jax.empty_ref(jax.ShapeDtypeStruct(a.shape, a.dtype), memory_space=hbm) for a in arrs]
    masks = (1, 5, 3, 7)
    nsem = len(arrs) * len(masks)

    @pl.kernel(mesh=plsc.ScalarSubcoreMesh(axis_name="seq", num_cores=1), name=name,
               scratch_types=(pltpu.SemaphoreType.DMA,) * (2 * nsem),
               compiler_params=pltpu.CompilerParams(collective_id=SEQ_COLLECTIVE_ID))
    def launch(*sems):
        x, y, c = (lax.axis_index(a) for a in AXES)
        my_chip = 2 * x + y
        peers = []
        for p in masks:
            px = 1 - x if p & 4 else x
            py = 1 - y if p & 2 else y
            peers.append(((px, py, 1 - c), 2 * px + py))
        barrier = pltpu.get_barrier_semaphore()
        for dev, _ in peers:
            pl.semaphore_signal(barrier, inc=1, device_id=dev, device_id_type=pl.DeviceIdType.MESH)
        pl.semaphore_wait(barrier, len(peers))
        copies = []
        for k in range(len(arrs)):
            for q, (dev, pchip) in enumerate(peers):
                i = k * len(masks) + q
                send = pltpu.make_async_remote_copy(
                    src_ref=srcs[k].at[pchip], dst_ref=dsts[k].at[my_chip], send_sem=sems[i], recv_sem=sems[nsem + i],
                    device_id=dev, device_id_type=pl.DeviceIdType.MESH)
                recv = pltpu.make_async_remote_copy(
                    src_ref=srcs[k].at[pchip], dst_ref=dsts[k].at[pchip], send_sem=sems[i], recv_sem=sems[nsem + i],
                    device_id=dev, device_id_type=pl.DeviceIdType.MESH)
                send.start()
                copies.append((send, recv))
        for send, recv in copies:
            recv.wait_recv()
            send.wait_send()

    launch()
    return [d[...] for d in dsts]


def _edges1(n0):
    return lambda: (pl.program_id(0) == 0, pl.program_id(0) == (7 * n0) // 8, pl.program_id(0) == n0 - 1)


def _edges2(n0, n1):
    def edges():
        step = pl.program_id(0) * n1 + pl.program_id(1)
        return step == 0, step == (7 * n0 * n1) // 8, step == n0 * n1 - 1
    return edges


def _xchg_shapes(arrs, kinds):
    return tuple(jax.ShapeDtypeStruct(((NDEV,) if k in (GATHER, GATHER_REL) else ()) + tuple(a.shape), a.dtype)
                 for a, k in zip(arrs, kinds))


def _xchg_sems(n):
    return [pltpu.SemaphoreType.DMA((n * (NDEV - 1),)), pltpu.SemaphoreType.DMA((n * (NDEV - 1),)),
            pltpu.SemaphoreType.DMA((n,))]


def _xchg_plan(in_refs, out_refs, sems, kinds):
    send_sems, recv_sems, loc_sems = sems
    x, y, c = (lax.axis_index(a) for a in AXES)
    my_abs, my_chip = 4 * x + 2 * y + c, 2 * x + y

    def peer(p):
        px = 1 - x if p & 4 else x
        py = 1 - y if p & 2 else y
        pc = 1 - c if p & 1 else c
        return (px, py, pc), 4 * px + 2 * py + pc, 2 * px + py

    starts, relays, recvs = [], [], []
    for k, kind in enumerate(kinds):
        src, dst = in_refs[k], out_refs[k]

        def remote(src_ref, row, p, pair, dst=dst, k=k):
            dev, sem = peer(p)[0], k * (NDEV - 1) + pair - 1
            return lambda: pltpu.make_async_remote_copy(
                src_ref=src_ref, dst_ref=dst.at[row], send_sem=send_sems.at[sem], recv_sem=recv_sems.at[sem],
                device_id=dev, device_id_type=pl.DeviceIdType.MESH)

        def local(src_ref, row, dst=dst, k=k):
            return lambda: pltpu.make_async_copy(src_ref, dst.at[row], loc_sems.at[k])

        if kind == GATHER:
            for p in range(1, NDEV):
                starts.append((remote(src, my_abs, p, p), False))
                recvs.append(remote(src, peer(p)[1], p, p))
            starts.append((local(src, my_abs), True))
        elif kind == GATHER_REL:
            for p in (1, 4, 2, 6):
                starts.append((remote(src, 4 * (p & 1) + my_chip, p, p), False))
            for q in (4, 2, 6):
                pchip = peer(q)[2]
                relays.append((remote(src, pchip, q, q), remote(dst.at[pchip], 4 + pchip, 1, q | 1)))
                recvs.append(remote(src, 4 + pchip, 1, q | 1))
            recvs.append(remote(src, 4 + my_chip, 1, 1))
            starts.append((local(src, my_chip), True))
        elif kind in (A2A, A2A_REL):
            for p in range(1, NDEV):
                _, pabs, pchip = peer(p)
                theirs, mine = (pabs, my_abs) if kind == A2A else (4 * (p & 1) + pchip, 4 * (p & 1) + my_chip)
                starts.append((remote(src.at[theirs], mine, p, p), False))
                recvs.append(remote(src.at[theirs], theirs, p, p))
            own = my_abs if kind == A2A else my_chip
            starts.append((local(src.at[own], own), True))
        else:
            for p in ((4, 2, 6) if kind == A2A_SAME else (1, 5, 3, 7)):
                pchip = peer(p)[2]
                starts.append((remote(src.at[pchip], my_chip, p, p), False))
                recvs.append(remote(src.at[pchip], pchip, p, p))
            if kind == A2A_SAME:
                starts.append((local(src.at[my_chip], my_chip), True))
    return starts, relays, recvs


def _xchg_start(in_refs, out_refs, sems, kinds):
    for make, _ in _xchg_plan(in_refs, out_refs, sems, kinds)[0]:
        make().start()


def _xchg_relay(in_refs, out_refs, sems, kinds):
    for arrival, forward in _xchg_plan(in_refs, out_refs, sems, kinds)[1]:
        arrival().wait_recv()
        forward().start()


def _xchg_finish(in_refs, out_refs, sems, kinds):
    starts, relays, recvs = _xchg_plan(in_refs, out_refs, sems, kinds)
    for make in recvs:
        make().wait_recv()
    for make, is_local in starts:
        if is_local:
            make().wait()
        else:
            make().wait_send()
    for _, forward in relays:
        forward().wait_send()


def _front(w13, w2, ng, c, c_ctx, w_blk, b_blk):
    d, wa = w_blk.shape
    big, big_kinds = [w13, w2, ng], [GATHER_REL, GATHER_REL, GATHER]

    def body(w13_ref, w2_ref, ng_ref, c_ref, cctx_ref, w_ref, b_ref, g13_ref, g2_ref, gng_ref, gc_ref, msl_ref, gm_ref,
             c16, mod, loc_sems, *sems):
        big_io = ((w13_ref, w2_ref, ng_ref), (g13_ref, g2_ref, gng_ref), sems[0:3], big_kinds)
        c_io = ((c_ref,), (gc_ref,), sems[3:6], [GATHER])
        mod_io = ((msl_ref,), (gm_ref,), sems[6:9], [GATHER])
        _xchg_start(*c_io)
        _xchg_start(*big_io)
        _xchg_finish(*c_io)
        rows = [pltpu.make_async_copy(gc_ref.at[k], c16.at[pl.ds(k, 1), :], loc_sems.at[k]) for k in range(NDEV)]
        for cp in rows:
            cp.start()
        c16[NDEV:NDEV + 1, :] = cctx_ref[...]
        c16[NDEV + 1:, :] = jnp.zeros((16 - NDEV - 1, d), F32)
        for cp in rows:
            cp.wait()
        cc = c16[...]
        mod[...] = _dot((cc * _sigmoid(cc)).astype(BF16), w_ref[...].astype(BF16)) + b_ref[...]
        out = pltpu.make_async_copy(mod, msl_ref, loc_sems.at[NDEV])
        out.start()
        out.wait()
        _xchg_start(*mod_io)
        _xchg_relay(*big_io)
        _xchg_finish(*big_io)
        _xchg_finish(*mod_io)

    any_spec = pl.BlockSpec(memory_space=pl.ANY)
    vmem = pl.BlockSpec(memory_space=pltpu.VMEM)
    outs = _pcall(
        body, name="exchange_weights",
        out_shape=_xchg_shapes(big + [c], big_kinds + [GATHER]) + (jax.ShapeDtypeStruct((16, wa), F32),
                                                                   jax.ShapeDtypeStruct((NDEV, 16, wa), F32)),
        in_specs=[any_spec] * 4 + [vmem] * 3, out_specs=[any_spec] * 6,
        scratch_shapes=[pltpu.VMEM((16, d), F32), pltpu.VMEM((16, wa), F32), pltpu.SemaphoreType.DMA((NDEV + 1,))]
        + _xchg_sems(3) + _xchg_sems(1) + _xchg_sems(1),
        compiler_params=_params(None),
    )(w13, w2, ng, c, c_ctx, w_blk, b_blk)
    return outs[0], outs[1], outs[2], outs[3], outs[5]


def _ada_bwd(c16, w_blk, dl, dc):
    d, w = w_blk.shape

    def body(c_ref, w_ref, dl_ref, dc_ref, gw_ref, gc_ref):
        cc = c_ref[...]
        sg = _sigmoid(cc)
        s = (cc * sg).astype(BF16)
        dctot = _colsum(dc_ref[...])
        dm = jnp.concatenate([dl_ref[...], jnp.broadcast_to(dctot, (8, w))], axis=0)
        rows = lax.broadcasted_iota(jnp.int32, (16, w), 0)
        dm = jnp.where(rows <= 8, dm, 0.0).astype(BF16)
        gw_ref[...] = _dot_tn(s, dm)
        t = _dot_nt(jnp.broadcast_to(dctot, (8, w)).astype(BF16), w_ref[...].astype(BF16))[0:1]
        c8, s8 = cc[8:9], sg[8:9]
        gc_ref[...] = t * (s8 * (1.0 + c8 * (1.0 - s8)))

    return _pcall(body, name="ada_bwd",
                  out_shape=(jax.ShapeDtypeStruct((d, w), F32), jax.ShapeDtypeStruct((1, d), F32)),
                  compiler_params=_params(None))(c16, w_blk, dl, dc)


def _norm_fwd(x, g, shift, scale):
    rstd = lax.rsqrt(jnp.mean(x * x, axis=-1, keepdims=True) + EPS)
    xhat = x * rstd
    xn = xhat * g
    return rstd, xhat, xn, xn * (1.0 + scale) + shift


def _norm_bwd(dh, rstd, xhat, xn, g, scale):
    dxn = dh * (1.0 + scale)
    dxhat = dxn * g
    dx = rstd * (dxhat - xhat * jnp.mean(dxhat * xhat, axis=-1, keepdims=True))
    return dx, _colsum(dh), _colsum(dh * xn), _colsum(dxn * xhat)


def _mod_spec(d, has_ctx):
    if has_ctx:
        return pl.BlockSpec((1, 9, d), lambda *ids: (jnp.minimum(ids[-1], 1), 0, 0))
    return pl.BlockSpec((1, 9, d), lambda *ids: (1, 0, 0))


def _row_specs(nx, tm, d):
    if nx == 1:
        return [pl.BlockSpec((tm, d), lambda i: (i, 0))]
    return [pl.BlockSpec((tm, d), lambda i: (0, 0)), pl.BlockSpec((tm, d), lambda i: (jnp.maximum(i - 1, 0), 0))]


def _rows(refs):
    if len(refs) == 1:
        return refs[0][...]
    return jnp.where(pl.program_id(0) == 0, refs[0][...], refs[1][...])


def _tile(has_ctx):
    return TM if has_ctx else TML


def _tile_of(nt, has_ctx):
    if has_ctx:
        return lambda i: jnp.where(i == 0, nt - 1, i - 1)
    return lambda i: i


def _dmod_spec(nrow, d, has_ctx):
    if has_ctx:
        return pl.BlockSpec((1, nrow, d), lambda i: (jnp.minimum(i, 1), 0, 0))
    return pl.BlockSpec((1, nrow, d), lambda i: (0, 0, 0))


def _ffn_fwd(x, modtab, g, w13, w2, k0, has_ctx, name, xchg=None, head=None):
    t, d = sum(a.shape[0] for a in x), x[0].shape[1]
    nb, w, _ = w13.shape
    fp = w // 2
    tm = _tile(has_ctx)
    nx = len(x)

    nhead = 0 if head is None else 2

    def body(*refs):
        mod_ref, g_ref, w13_ref, w2_ref = refs[nx:nx + 4]
        xo_ref, h_ref, ab_ref, o_ref = refs[nx + 4 + nhead:nx + 8 + nhead]
        xx = _rows(refs[:nx])
        shift, scale, gate = mod_ref[0, k0:k0 + 1, :], mod_ref[0, k0 + 1:k0 + 2, :], mod_ref[0, k0 + 2:k0 + 3, :]
        _, _, _, h = _norm_fwd(xx, g_ref[...], shift, scale)
        hb = h.astype(BF16)
        h_ref[...] = hb
        acc = jnp.zeros((tm, d), F32)
        for p in range(nb // 2):
            zs = []
            for q in range(2):
                blk = 2 * p + q
                ab = _dot_nt(hb, w13_ref[blk])
                ab_ref[:, blk * w:(blk + 1) * w] = ab.astype(BF16)
                a, b = ab[:, :fp], ab[:, fp:]
                zs.append((a * _sigmoid(a) * b).astype(BF16))
            acc = acc + _dot(jnp.concatenate(zs, axis=1), w2_ref[p])
        o_ref[...] = acc.astype(BF16)
        xo = xx + (0.5 * gate) * acc
        if head is None:
            xo_ref[...] = xo
        else:
            tgt_ref, fg_ref = refs[nx + 4:nx + 6]
            loss_ref, dfg_ref = refs[nx + 8 + nhead:]

            @pl.when(pl.program_id(0) == 0)
            def _():
                loss_ref[...] = jnp.zeros_like(loss_ref)
                dfg_ref[...] = jnp.zeros_like(dfg_ref)

            xo_ref[...] = _loss_head(xo, tgt_ref[...], fg_ref[...], loss_ref, dfg_ref)

    tile = _tile_of(t // tm, has_ctx)
    row = lambda i: (tile(i), 0)
    head_in = [] if head is None else [pl.BlockSpec((tm, d), row), _const((1, d))]
    head_shape = () if head is None else (jax.ShapeDtypeStruct((1, LANE), F32), jax.ShapeDtypeStruct((1, d), F32))
    head_out = () if head is None else (_const((1, LANE)), _const((1, d)))
    return _pcall(
        body, name=name, grid=(t // tm,), xchg=xchg, edges=_edges1(t // tm),
        out_shape=(jax.ShapeDtypeStruct((t, d), F32), jax.ShapeDtypeStruct((t, d), BF16),
                   jax.ShapeDtypeStruct((t, nb * w), BF16), jax.ShapeDtypeStruct((t, d), BF16)) + head_shape,
        in_specs=_row_specs(nx, tm, d) + [_mod_spec(d, has_ctx), _const((1, d)),
                                          _resident(w13.shape), _resident(w2.shape)] + head_in,
        out_specs=(pl.BlockSpec((tm, d), row), pl.BlockSpec((tm, d), row), pl.BlockSpec((tm, nb * w), row),
                   pl.BlockSpec((tm, d), row)) + head_out,
        compiler_params=_params(("arbitrary",)),
    )(*x, modtab, g, w13, w2, *(head or ()))


def _ffn_bwd(dy, modtab, h, ab, w13, w2, k0, has_ctx, name, pair0=0, npair=None, xchg=None):
    t, d = dy.shape
    _, w, _ = w13.shape
    fp = w // 2
    npair = w13.shape[0] // 2 if npair is None else npair
    nb = 2 * npair
    tm = _tile(has_ctx)
    nt = t // tm

    def body(dy_ref, mod_ref, h_ref, ab_ref, w13_ref, w2_ref, dh_ref, d13_ref, d2_ref, acc13, acc2):
        i = pl.program_id(1)

        @pl.when(i == 0)
        def _():
            acc13[...] = jnp.zeros_like(acc13)
            acc2[...] = jnp.zeros_like(acc2)

        gate = mod_ref[0, k0 + 2:k0 + 3, :]
        do = (dy_ref[...] * (0.5 * gate)).astype(BF16)
        dz = _dot_nt(do, w2_ref[0])
        hb = h_ref[...]
        dh = jnp.zeros((tm, d), F32)
        zs = []
        for q in range(2):
            ab = ab_ref[:, q * w:(q + 1) * w].astype(F32)
            a, b = ab[:, :fp], ab[:, fp:]
            sg = _sigmoid(a)
            sa = a * sg
            dzq = dz[:, q * fp:(q + 1) * fp]
            da = dzq * b * (sg * (1.0 + a * (1.0 - sg)))
            db = dzq * sa
            dab = jnp.concatenate([da, db], axis=1).astype(BF16)
            dh = dh + _dot(dab, w13_ref[q])
            acc13[q] += _dot_tn(dab, hb)
            zs.append((sa * b).astype(BF16))
        acc2[...] += _dot_tn(jnp.concatenate(zs, axis=1), do)
        dh_ref[0] = dh.astype(BF16)

        @pl.when(i == nt - 1)
        def _():
            d13_ref[...] = acc13[...].astype(BF16)
            d2_ref[0] = acc2[...].astype(BF16)

    mod_spec = _mod_spec(d, has_ctx)
    tile = _tile_of(nt, has_ctx)
    return _pcall(
        body, name=name, grid=(npair, nt), xchg=xchg, edges=_edges2(npair, nt),
        out_shape=(jax.ShapeDtypeStruct((npair, t, d), BF16), jax.ShapeDtypeStruct((nb, w, d), BF16),
                   jax.ShapeDtypeStruct((npair, 2 * fp, d), BF16)),
        in_specs=[pl.BlockSpec((tm, d), lambda p, i: (tile(i), 0)), mod_spec,
                  pl.BlockSpec((tm, d), lambda p, i: (tile(i), 0)),
                  pl.BlockSpec((tm, 2 * w), lambda p, i: (tile(i), pair0 + p)),
                  pl.BlockSpec((2, w, d), lambda p, i: (pair0 + p, 0, 0)),
                  pl.BlockSpec((1, 2 * fp, d), lambda p, i: (pair0 + p, 0, 0))],
        out_specs=(pl.BlockSpec((1, tm, d), lambda p, i: (p, tile(i), 0)),
                   pl.BlockSpec((2, w, d), lambda p, i: (p, 0, 0)),
                   pl.BlockSpec((1, 2 * fp, d), lambda p, i: (p, 0, 0))),
        scratch_shapes=[pltpu.VMEM((2, w, d), F32), pltpu.VMEM((2 * fp, d), F32)],
        compiler_params=_params(("arbitrary", "arbitrary")),
    )(dy, modtab, h, ab, w13, w2)


def _ffn_bwd_norm(dy, dhps, x, o, modtab, g, k0, has_ctx, name, xchg=None):
    t, d = dy.shape
    ngrp = 2 if has_ctx else 1
    tm = _tile(has_ctx)
    ndh, nx = len(dhps), len(x)
    lat0 = ngrp - 1

    def body(dy_ref, *rest):
        dhp_refs, x_refs = rest[:ndh], rest[ndh:ndh + nx]
        o_ref, mod_ref, g_ref, dx_ref, dmod_ref, dg_ref = rest[ndh + nx:]
        i = pl.program_id(0)

        @pl.when(i == 0)
        def _():
            dg_ref[...] = jnp.zeros_like(dg_ref)

        @pl.when((i == 0) | (i == ngrp - 1))
        def _():
            dmod_ref[...] = jnp.zeros_like(dmod_ref)

        dh = None
        for ref in dhp_refs:
            for p in range(ref.shape[0]):
                dh = ref[p].astype(F32) if dh is None else dh + ref[p].astype(F32)
        scale = mod_ref[0, k0 + 1:k0 + 2, :]
        gg = g_ref[...]
        rstd, xhat, xn, _ = _norm_fwd(_rows(x_refs), gg, 0.0, scale)
        dxn, dshift, dscale, dg = _norm_bwd(dh, rstd, xhat, xn, gg, scale)
        dyv = dy_ref[...]
        dx_ref[...] = dyv + dxn
        dmod_ref[0, 0:1, :] += dshift
        dmod_ref[0, 1:2, :] += dscale
        dmod_ref[0, 2:3, :] += _colsum(0.5 * dyv * o_ref[...].astype(F32))
        dg_ref[...] += dg

    tile = _tile_of(t // tm, has_ctx)
    row = lambda i: (tile(i), 0)
    return _pcall(
        body, name=name, grid=(t // tm,), xchg=xchg, edges=_edges1(t // tm),
        out_shape=(jax.ShapeDtypeStruct((t - lat0 * tm, d), F32), jax.ShapeDtypeStruct((ngrp, 3, d), F32),
                   jax.ShapeDtypeStruct((1, d), F32)),
        in_specs=[pl.BlockSpec((tm, d), row)]
        + [pl.BlockSpec((a.shape[0], tm, d), lambda i: (0, tile(i), 0)) for a in dhps]
        + _row_specs(nx, tm, d) + [pl.BlockSpec((tm, d), row), _mod_spec(d, has_ctx), _const((1, d))],
        out_specs=(pl.BlockSpec((tm, d), lambda i: (jnp.maximum(i - lat0, 0), 0)), _dmod_spec(3, d, has_ctx),
                   _const((1, d))),
        compiler_params=_params(("arbitrary",)),
    )(dy, *dhps, *x, o, modtab, g)


def _mix_in_fwd(x1, modtab, g, w_in):
    t, d = x1.shape
    hm = w_in.shape[1] // 2

    def body(x_ref, mod_ref, g_ref, w_ref, h_ref, up_ref, us_ref):
        _, _, _, h = _norm_fwd(x_ref[...], g_ref[...], mod_ref[0, 3:4, :], mod_ref[0, 4:5, :])
        hb = h.astype(BF16)
        h_ref[...] = hb
        u = _dot(hb, w_ref[...])
        up_ref[...] = u[:, :hm]
        us_ref[...] = u[:, hm:]

    tile = _tile_of(t // TM, True)
    row = lambda i: (tile(i), 0)
    return _pcall(
        body, name="mix_in_fwd", grid=(t // TM,),
        out_shape=(jax.ShapeDtypeStruct((t, d), BF16), jax.ShapeDtypeStruct((t, hm), F32),
                   jax.ShapeDtypeStruct((t, hm), F32)),
        in_specs=[pl.BlockSpec((TM, d), row), _mod_spec(d, True), _const((1, d)), _resident(w_in.shape)],
        out_specs=(pl.BlockSpec((TM, d), row), pl.BlockSpec((TM, hm), row), pl.BlockSpec((TM, hm), row)),
        compiler_params=_params(("arbitrary",)),
    )(x1, modtab, g, w_in)


def _mix_in_bwd(du_pool, du_f, du_b, dx2, h1, x1, modtab, g, w_in, lc):
    t, d = x1.shape
    m = w_in.shape[1]
    hm = m // 2
    nt = t // TM

    def body(dup_ref, duf_ref, dub_ref, dx2_ref, h_ref, x_ref, mod_ref, g_ref, w_ref, dx_ref, dw_ref, dmod_ref, dg_ref, acc):
        i = pl.program_id(0)

        @pl.when(i == 0)
        def _():
            acc[...] = jnp.zeros_like(acc)
            dg_ref[...] = jnp.zeros_like(dg_ref)

        @pl.when(i <= 1)
        def _():
            dmod_ref[...] = jnp.zeros_like(dmod_ref)

        lat = (i > 0).astype(F32)
        valid = (i > 0) | (lax.broadcasted_iota(jnp.int32, (TM, 1), 0) < lc)
        du_s5 = jnp.where(valid, duf_ref[...] + dub_ref[...], 0.0)
        du = jnp.concatenate([dup_ref[...] * lat, du_s5], axis=1).astype(BF16)
        dh = _dot_nt(du, w_ref[...])
        acc[...] += _dot_tn(h_ref[...], du)
        scale = mod_ref[0, 4:5, :]
        gg = g_ref[...]
        rstd, xhat, xn, _ = _norm_fwd(x_ref[...], gg, 0.0, scale)
        dxn, dshift, dscale, dg = _norm_bwd(dh, rstd, xhat, xn, gg, scale)
        dx_ref[...] = dx2_ref[...] * lat + dxn
        dmod_ref[0, 0:1, :] += dshift
        dmod_ref[0, 1:2, :] += dscale
        dg_ref[...] += dg

        @pl.when(i == nt - 1)
        def _():
            dw_ref[...] = acc[...].astype(BF16)

    tile = _tile_of(nt, True)
    row = lambda i: (tile(i), 0)
    lrow = lambda i: (jnp.maximum(i - 1, 0), 0)
    return _pcall(
        body, name="mix_in_bwd", grid=(nt,),
        out_shape=(jax.ShapeDtypeStruct((t, d), F32), jax.ShapeDtypeStruct((d, m), BF16),
                   jax.ShapeDtypeStruct((2, 2, d), F32), jax.ShapeDtypeStruct((1, d), F32)),
        in_specs=[pl.BlockSpec((TM, hm), lrow), pl.BlockSpec((TM, hm), row), pl.BlockSpec((TM, hm), row),
                  pl.BlockSpec((TM, d), lrow), pl.BlockSpec((TM, d), row), pl.BlockSpec((TM, d), row),
                  _mod_spec(d, True), _const((1, d)), _resident(w_in.shape)],
        out_specs=(pl.BlockSpec((TM, d), row), _const((d, m)), _dmod_spec(2, d, True), _const((1, d))),
        scratch_shapes=[pltpu.VMEM((d, m), F32)],
        compiler_params=_params(("arbitrary",)),
    )(du_pool, du_f, du_b, dx2, h1, x1, modtab, g, w_in)


def _pool(v, rows, transpose, name):
    l, n = rows * GRID_W, GRID_W * LANE
    ngrp = v.shape[1] // LANE
    nchunk = 4
    cw = n // nchunk

    def rowsum(val, lo, hi):
        ri = lax.broadcasted_iota(jnp.int32, (rows, rows), 0)
        ci = lax.broadcasted_iota(jnp.int32, (rows, rows), 1)
        sel = (((ci - ri) >= -lo) & ((ci - ri) <= hi)).astype(BF16)
        v1 = val.astype(BF16)
        r1 = val - v1.astype(F32)
        v2 = r1.astype(BF16)
        v3 = (r1 - v2.astype(F32)).astype(BF16)
        return _dot(sel, v1) + _dot(sel, v2) + _dot(sel, v3)

    def one(v_ref, o_ref, g_scr, r_scr, a_scr, win):
        for c in range(GRID_W):
            g_scr[:, c * LANE:(c + 1) * LANE] = v_ref[pl.ds(c, rows, stride=GRID_W), :]
        lo = win // 2
        hi = win - 1 - lo
        rlo, rhi = (hi, lo) if transpose else (lo, hi)
        ridx = lax.broadcasted_iota(jnp.int32, (rows, 1), 0)
        cnt_r = (jnp.minimum(ridx + hi + 1, rows) - jnp.maximum(ridx - lo, 0)).astype(F32)
        cidx = lax.broadcasted_iota(jnp.int32, (1, n), 1) // LANE
        cnt_c = (jnp.minimum(cidx + hi + 1, GRID_W) - jnp.maximum(cidx - lo, 0)).astype(F32)
        if not transpose:
            for k in range(nchunk):
                sl = slice(k * cw, (k + 1) * cw)
                r_scr[:, sl] = rowsum(g_scr[:, sl], rlo, rhi) / cnt_r
        else:
            r_scr[...] = g_scr[...] / cnt_c
        a_scr[...] = r_scr[...]
        for j in range(-rlo, rhi + 1):
            if j == 0:
                continue
            c0, c1 = max(0, -j), min(GRID_W, GRID_W - j)
            a_scr[:, c0 * LANE:c1 * LANE] += r_scr[:, (c0 + j) * LANE:(c1 + j) * LANE]
        if not transpose:
            r_scr[...] = a_scr[...] / cnt_c - g_scr[...]
        else:
            for k in range(nchunk):
                sl = slice(k * cw, (k + 1) * cw)
                r_scr[:, sl] = rowsum(a_scr[:, sl] / cnt_r, rlo, rhi) - g_scr[:, sl]
        for c in range(GRID_W):
            o_ref[pl.ds(c, rows, stride=GRID_W), :] = r_scr[:, c * LANE:(c + 1) * LANE]

    def body(v_ref, o_ref, g_scr, r_scr, a_scr):
        grp = pl.program_id(0)
        for k, win in enumerate(POOL_WINDOWS):
            @pl.when(grp == k)
            def _(win=win):
                one(v_ref, o_ref, g_scr, r_scr, a_scr, win)

    spec = pl.BlockSpec((l, LANE), lambda k: (0, k))
    return _pcall(
        body, name=name, grid=(ngrp,), out_shape=jax.ShapeDtypeStruct((l, ngrp * LANE), F32),
        in_specs=[spec], out_specs=spec,
        scratch_shapes=[pltpu.VMEM((rows, n), F32), pltpu.VMEM((rows, n), F32), pltpu.VMEM((rows, n), F32)],
        compiler_params=_params(("arbitrary",)),
    )(v)


def _cmul(ar, ai, br, bi):
    return ar * br - ai * bi, ar * bi + ai * br


def _s5_prep(a_re, a_im, log_dt, c_re, c_im, kvec):
    q, nc, p = c_re.shape

    def body(ar_ref, ai_ref, ldt_ref, cr_ref, ci_ref, k_ref, cpr_ref, cpi_ref, tr_ref, ti_ref):
        lr, li = ar_ref[...], ai_ref[...]
        dt = jnp.exp(ldt_ref[...])
        kk = k_ref[...]
        mag = jnp.exp(kk * (lr * dt))
        ph = kk * (li * dt)
        tr_ref[...] = mag * jnp.cos(ph)
        ti_ref[...] = mag * jnp.sin(ph)
        m1 = jnp.exp(lr * dt)
        abr, abi = m1 * jnp.cos(li * dt), m1 * jnp.sin(li * dt)
        den = lr * lr + li * li
        xr, xi = abr - 1.0, abi
        cfr, cfi = (xr * lr + xi * li) / den, (xi * lr - xr * li) / den
        pr, pi_ = _cmul(cr_ref[...], ci_ref[...], cfr, cfi)
        cpr_ref[...] = pr
        cpi_ref[...] = pi_

    return _pcall(
        body, name="s5_prep",
        out_shape=(jax.ShapeDtypeStruct((q, nc, p), F32), jax.ShapeDtypeStruct((q, nc, p), F32),
                   jax.ShapeDtypeStruct((q, NTAB, p), F32), jax.ShapeDtypeStruct((q, NTAB, p), F32)),
        compiler_params=_params(None),
    )(a_re, a_im, log_dt, c_re, c_im, kvec)


def _s5_param_bwd(a_re, a_im, log_dt, c_re, c_im, dab_r, dab_i, dcp_r, dcp_i):
    q, nc, p = c_re.shape

    def body(ar_ref, ai_ref, ldt_ref, cr_ref, ci_ref, dar_ref, dai_ref, dcr_ref, dci_ref,
             gar_ref, gai_ref, gdt_ref, gcr_ref, gci_ref):
        lr, li = ar_ref[...], ai_ref[...]
        dt = jnp.exp(ldt_ref[...])
        m1 = jnp.exp(lr * dt)
        abr, abi = m1 * jnp.cos(li * dt), m1 * jnp.sin(li * dt)
        den = lr * lr + li * li
        xr, xi = abr - 1.0, abi
        cfr, cfi = (xr * lr + xi * li) / den, (xi * lr - xr * li) / den
        cr, ci = cr_ref[...], ci_ref[...]
        dcr, dci = dcr_ref[...], dci_ref[...]
        gcr, gci = _cmul(dcr, dci, cfr, -cfi)
        gcr_ref[...] = gcr
        gci_ref[...] = gci
        t_r, t_i = _cmul(cr, -ci, dcr, dci)
        dcf_r = jnp.sum(t_r, axis=1, keepdims=True)
        dcf_i = jnp.sum(t_i, axis=1, keepdims=True)
        ilr, ili = lr / den, -li / den
        u_r, u_i = _cmul(dcf_r, dcf_i, ilr, -ili)
        dab_r_, dab_i_ = dar_ref[...] + u_r, dai_ref[...] + u_i
        v_r, v_i = _cmul(dab_r_, dab_i_, abr, -abi)
        cl_r, cl_i = _cmul(cfr, cfi, ilr, ili)
        w_r, w_i = _cmul(dcf_r, dcf_i, cl_r, -cl_i)
        gar_ref[...] = v_r * dt - w_r
        gai_ref[...] = v_i * dt - w_i
        la_r, la_i = _cmul(lr, li, abr, abi)
        ddt = la_r * dab_r_ + la_i * dab_i_
        gdt_ref[...] = dt * jnp.sum(ddt, axis=2, keepdims=True)

    return _pcall(
        body, name="s5_param_bwd",
        out_shape=(jax.ShapeDtypeStruct((q, 1, p), F32), jax.ShapeDtypeStruct((q, 1, p), F32),
                   jax.ShapeDtypeStruct((q, 1, p), F32), jax.ShapeDtypeStruct((q, nc, p), F32),
                   jax.ShapeDtypeStruct((q, nc, p), F32)),
        compiler_params=_params(None),
    )(a_re, a_im, log_dt, c_re, c_im, dab_r, dab_i, dcp_r, dcp_i)


def _bcast8(row, c):
    return jnp.broadcast_to(row, (8, c))


def _slab(j):
    return j * 8 if isinstance(j, int) else pl.multiple_of(j * 8, 8)


def _scan_head(xr, xi, tre_ref, tim_ref, car_r, car_i, desc, conj, cs):
    sg = -1.0 if conj else 1.0
    ar = _bcast8(tre_ref[0:1, :], cs)
    ai = sg * _bcast8(tim_ref[0:1, :], cs)

    def p1(jj, carry):
        hr, hi = carry
        off = _slab(SEG - 1 - jj if desc else jj)
        pr, pi_ = _cmul(ar, ai, hr, hi)
        nr = pr + xr[pl.ds(off, 8), :]
        ni = pi_ + xi[pl.ds(off, 8), :]
        xr[pl.ds(off, 8), :] = nr
        xi[pl.ds(off, 8), :] = ni
        return nr, ni

    zero = jnp.zeros((8, cs), F32)
    ir, ii = lax.fori_loop(0, SEG, p1, (zero, zero), unroll=S5_UNROLL)
    cin_r, cin_i = car_r[...], car_i[...]
    rows = lax.broadcasted_iota(jnp.int32, (8, cs), 0)
    for s, krow in ((1, SEG), (2, SEG + 1), (4, SEG + 3)):
        keep, sh = (rows < 8 - s, 8 - s) if desc else (rows >= s, s)
        sr = jnp.where(keep, pltpu.roll(ir, sh, 0), 0.0)
        si = jnp.where(keep, pltpu.roll(ii, sh, 0), 0.0)
        pr, pi_ = _cmul(tre_ref[krow:krow + 1, :], sg * tim_ref[krow:krow + 1, :], sr, si)
        ir, ii = ir + pr, ii + pi_
    q0 = SEG + 8 if desc else SEG
    pr, pi_ = _cmul(tre_ref[q0:q0 + 8, :], sg * tim_ref[q0:q0 + 8, :], cin_r, cin_i)
    fr, fi = ir + pr, ii + pi_
    keep, sh, edge = (rows < 7, 7, 0) if desc else (rows >= 1, 1, 7)
    cs_r = jnp.where(keep, pltpu.roll(fr, sh, 0), cin_r)
    cs_i = jnp.where(keep, pltpu.roll(fi, sh, 0), cin_i)
    car_r[...] = _bcast8(fr[edge:edge + 1, :], cs)
    car_i[...] = _bcast8(fi[edge:edge + 1, :], cs)
    return cs_r, cs_i, cin_r, cin_i


def _pow_row(tre_ref, tim_ref, j, desc, conj, cs):
    k = SEG - 1 - j if desc else j
    sg = -1.0 if conj else 1.0
    return _bcast8(tre_ref[pl.ds(k, 1), :], cs), sg * _bcast8(tim_ref[pl.ds(k, 1), :], cs)


def _to_segments(val, dst):
    for r in range(8):
        dst[pl.ds(r, SEG, stride=8), :] = val[r * SEG:(r + 1) * SEG, :]


def _from_segments(src):
    return jnp.concatenate([src[pl.ds(r, SEG, stride=8), :] for r in range(8)], axis=0)


def _s5_fwd(up, bre, bim, cmr, cmi, tre, tim, dskip, order, nproc, desc, with_skip, name):
    t, cu = up.shape
    nsb = bre.shape[0]
    cb = cu // nsb
    cs = bre.shape[2]
    nch = t // TC

    def body(u_ref, bre_ref, bim_ref, cmr_ref, cmi_ref, tre_ref, tim_ref, dsk_ref,
             y_ref, hr_ref, hi_ref, cinr_ref, cini_ref, car_r, car_i, seg_scr):
        i = pl.program_id(0)

        @pl.when(i == 0)
        def _():
            car_r[...] = jnp.zeros_like(car_r)
            car_i[...] = jnp.zeros_like(car_i)

        for sb in range(nsb):
            col, ch = pl.ds(sb * cs, cs), slice(sb * cb, (sb + 1) * cb)
            hr_v, hi_v = hr_ref.at[:, col], hi_ref.at[:, col]
            tre_v, tim_v = tre_ref.at[:, col], tim_ref.at[:, col]
            useg, yseg = seg_scr.at[0, sb], seg_scr.at[1, sb]
            _to_segments(u_ref[:, ch], useg)
            u = useg[...]
            ub = u.astype(BF16)
            hr_v[...] = _dot(ub, bre_ref[sb])
            hi_v[...] = _dot(ub, bim_ref[sb])
            cs_r, cs_i, cin_r, cin_i = _scan_head(hr_v, hi_v, tre_v, tim_v, car_r.at[sb], car_i.at[sb], desc, False, cs)
            cinr_ref[:, col] = cin_r
            cini_ref[:, col] = cin_i

            def p2(j, _, hr_v=hr_v, hi_v=hi_v, tre_v=tre_v, tim_v=tim_v, cs_r=cs_r, cs_i=cs_i):
                off = _slab(j)
                pw_r, pw_i = _pow_row(tre_v, tim_v, j, desc, False, cs)
                pr, pi_ = _cmul(pw_r, pw_i, cs_r, cs_i)
                hr_v[pl.ds(off, 8), :] = hr_v[pl.ds(off, 8), :] + pr
                hi_v[pl.ds(off, 8), :] = hi_v[pl.ds(off, 8), :] + pi_
                return 0

            lax.fori_loop(0, SEG, p2, 0, unroll=S5_UNROLL)
            y = _dot(hr_v[...].astype(BF16), cmr_ref[sb]) - _dot(hi_v[...].astype(BF16), cmi_ref[sb])
            if with_skip:
                y = y + dsk_ref[:, ch] * u
            yseg[...] = y
            y_ref[:, ch] = _from_segments(yseg)

    blk = lambda i: (order(i), 0)
    return _pcall(
        body, name=name, grid=(nproc,),
        out_shape=(jax.ShapeDtypeStruct((t, cu), F32), jax.ShapeDtypeStruct((t, nsb * cs), F32),
                   jax.ShapeDtypeStruct((t, nsb * cs), F32), jax.ShapeDtypeStruct((nch * 8, nsb * cs), F32),
                   jax.ShapeDtypeStruct((nch * 8, nsb * cs), F32)),
        in_specs=[pl.BlockSpec((TC, cu), blk), _const(bre.shape), _const(bim.shape), _const(cmr.shape),
                  _const(cmi.shape), _const(tre.shape), _const(tim.shape), _const(dskip.shape)],
        out_specs=(pl.BlockSpec((TC, cu), blk), pl.BlockSpec((TC, nsb * cs), blk), pl.BlockSpec((TC, nsb * cs), blk),
                   pl.BlockSpec((8, nsb * cs), blk), pl.BlockSpec((8, nsb * cs), blk)),
        scratch_shapes=[pltpu.VMEM((nsb, 8, cs), F32), pltpu.VMEM((nsb, 8, cs), F32),
                        pltpu.VMEM((2, nsb, TC, cb), F32)],
        compiler_params=_params(("arbitrary",)),
    )(up, bre, bim, cmr, cmi, tre, tim, dskip)


def _s5_bwd(dy_lat, up, hr, hi, cinr, cini, bre, bim, ctr, cti, tre, tim, dskip, order, nproc, desc, with_skip, name,
            xchg=None):
    t, cu = up.shape
    nsb = bre.shape[0]
    cb = cu // nsb
    cs = bre.shape[2]
    lch = dy_lat.shape[0] // TC
    adesc = not desc

    def body(dy_ref, u_ref, hr_ref, hi_ref, cinr_ref, cini_ref, bre_ref, bim_ref, ctr_ref, cti_ref, tre_ref, tim_ref,
             dsk_ref, du_ref, dar_ref, dai_ref, dcr_ref, dci_ref, dbr_ref, dbi_ref, dd_ref,
             car_r, car_i, mr_all, mi_all, acc_r, acc_i, seg_scr):
        i = pl.program_id(0)
        latent = (order(i) < lch).astype(F32)

        @pl.when(i == 0)
        def _():
            for ref in (car_r, car_i, acc_r, acc_i, dcr_ref, dci_ref, dbr_ref, dbi_ref, dd_ref):
                ref[...] = jnp.zeros_like(ref)

        rows = lax.broadcasted_iota(jnp.int32, (8, cs), 0)
        for sb in range(nsb):
            col, ch = pl.ds(sb * cs, cs), slice(sb * cb, (sb + 1) * cb)
            hr_v, hi_v = hr_ref.at[:, col], hi_ref.at[:, col]
            tre_v, tim_v = tre_ref.at[:, col], tim_ref.at[:, col]
            mr, mi = mr_all.at[sb % 2], mi_all.at[sb % 2]
            useg, dyseg, duseg = seg_scr.at[0, sb], seg_scr.at[1, sb], seg_scr.at[2, sb]
            _to_segments(dy_ref[:, ch] * latent, dyseg)
            _to_segments(u_ref[:, ch], useg)
            dy = dyseg[...]
            dyb = dy.astype(BF16)
            u = useg[...]
            ub = u.astype(BF16)
            mr[...] = _dot(dyb, ctr_ref[sb])
            mi[...] = -_dot(dyb, cti_ref[sb])
            cs_r, cs_i, _, _ = _scan_head(mr, mi, tre_v, tim_v, car_r.at[sb], car_i.at[sb], adesc, True, cs)

            def fix(j, hp_r, hp_i, acc, mr=mr, mi=mi, tre_v=tre_v, tim_v=tim_v, cs_r=cs_r, cs_i=cs_i):
                off = _slab(j)
                pw_r, pw_i = _pow_row(tre_v, tim_v, j, adesc, True, cs)
                pr, pi_ = _cmul(pw_r, pw_i, cs_r, cs_i)
                m_r = mr[pl.ds(off, 8), :] + pr
                m_i = mi[pl.ds(off, 8), :] + pi_
                mr[pl.ds(off, 8), :] = m_r
                mi[pl.ds(off, 8), :] = m_i
                a_r, a_i = acc
                return a_r + hp_r * m_r + hp_i * m_i, a_i + hp_r * m_i - hp_i * m_r

            edge_j, src, keep, sh = (SEG - 1, 0, rows < 7, 7) if desc else (0, (SEG - 1) * 8, rows >= 1, 1)
            h0r = jnp.where(keep, pltpu.roll(hr_v[src:src + 8, :], sh, 0), cinr_ref[:, col])
            h0i = jnp.where(keep, pltpu.roll(hi_v[src:src + 8, :], sh, 0), cini_ref[:, col])
            acc = fix(edge_j, h0r, h0i, (acc_r[sb], acc_i[sb]))

            def p2(jj, acc, fix=fix, hr_v=hr_v, hi_v=hi_v):
                j = jj if desc else jj + 1
                offp = _slab(j + 1 if desc else j - 1)
                return fix(j, hr_v[pl.ds(offp, 8), :], hi_v[pl.ds(offp, 8), :], acc)

            a_r, a_i = lax.fori_loop(0, SEG - 1, p2, acc, unroll=S5_UNROLL)
            acc_r[sb] = a_r
            acc_i[sb] = a_i

            mrb, mib = mr[...].astype(BF16), mi[...].astype(BF16)
            du = _dot_nt(mrb, bre_ref[sb]) + _dot_nt(mib, bim_ref[sb])
            if with_skip:
                du = du + dsk_ref[:, ch] * dy
                dd_ref[:, ch] += _colsum(dy * u)
            duseg[...] = du
            du_ref[:, ch] = _from_segments(duseg)
            dbr_ref[sb] += _dot_tn(ub, mrb)
            dbi_ref[sb] += _dot_tn(ub, mib)
            dcr_ref[sb] += _dot_tn(dyb, hr_v[...].astype(BF16))
            dci_ref[sb] -= _dot_tn(dyb, hi_v[...].astype(BF16))

            @pl.when(i == nproc - 1)
            def _(sb=sb, a_r=a_r, a_i=a_i):
                dar_ref[sb] = _colsum(a_r)
                dai_ref[sb] = _colsum(a_i)

    blk = lambda i: (order(i), 0)
    blk_dy = lambda i: (jnp.minimum(order(i), lch - 1), 0)
    mat = (nsb, cb, cs)
    return _pcall(
        body, name=name, grid=(nproc,), xchg=xchg, edges=_edges1(nproc),
        out_shape=(jax.ShapeDtypeStruct((t, cu), F32),
                   jax.ShapeDtypeStruct((nsb, 1, cs), F32), jax.ShapeDtypeStruct((nsb, 1, cs), F32),
                   jax.ShapeDtypeStruct(mat, F32), jax.ShapeDtypeStruct(mat, F32),
                   jax.ShapeDtypeStruct(mat, F32), jax.ShapeDtypeStruct(mat, F32),
                   jax.ShapeDtypeStruct((1, cu), F32)),
        in_specs=[pl.BlockSpec((TC, cu), blk_dy), pl.BlockSpec((TC, cu), blk), pl.BlockSpec((TC, nsb * cs), blk),
                  pl.BlockSpec((TC, nsb * cs), blk), pl.BlockSpec((8, nsb * cs), blk), pl.BlockSpec((8, nsb * cs), blk),
                  _const(mat), _const(mat), _const(mat), _const(mat), _const(tre.shape), _const(tim.shape),
                  _const((1, cu))],
        out_specs=(pl.BlockSpec((TC, cu), blk), _const((nsb, 1, cs)), _const((nsb, 1, cs)),
                   _const(mat), _const(mat), _const(mat), _const(mat), _const((1, cu))),
        scratch_shapes=[pltpu.VMEM((nsb, 8, cs), F32), pltpu.VMEM((nsb, 8, cs), F32), pltpu.VMEM((2, TC, cs), F32),
                        pltpu.VMEM((2, TC, cs), F32), pltpu.VMEM((nsb, 8, cs), F32), pltpu.VMEM((nsb, 8, cs), F32),
                        pltpu.VMEM((3, nsb, TC, cb), F32)],
        compiler_params=_params(("arbitrary",)),
    )(dy_lat, up, hr, hi, cinr, cini, bre, bim, ctr, cti, tre, tim, dskip)


def _gelu(x):
    k = math.sqrt(2.0 / math.pi)
    return 0.5 * x * (1.0 + jnp.tanh(k * (x + 0.044715 * (x * x * x))))


def _gelu_grad(x):
    k = math.sqrt(2.0 / math.pi)
    th = jnp.tanh(k * (x + 0.044715 * (x * x * x)))
    return 0.5 * (1.0 + th) + 0.5 * x * (1.0 - th * th) * (k * (1.0 + 3.0 * 0.044715 * (x * x)))


def _mix_out_fwd(yf, yb, dpool, x1, modtab, w_pool, pscale, w_glu, w_out):
    l, hm = dpool.shape
    d = x1.shape[1]

    def body(yf_ref, yb_ref, dp_ref, x_ref, mod_ref, wp_ref, ps_ref, wg_ref, wo_ref, x2_ref, yp_ref, cat_ref, mix_ref):
        ypre = yf_ref[...] + yb_ref[...]
        yp_ref[...] = ypre
        yg = _gelu(ypre)
        y2 = yg * _sigmoid(_dot(yg.astype(BF16), wg_ref[...]))
        po = _dot(dp_ref[...].astype(BF16), wp_ref[...]) * ps_ref[...]
        cat = jnp.concatenate([po, y2], axis=1).astype(BF16)
        cat_ref[...] = cat
        mix = _dot(cat, wo_ref[...])
        mix_ref[...] = mix.astype(BF16)
        x2_ref[...] = x_ref[...] + mod_ref[0, 5:6, :] * mix

    row = lambda i: (i, 0)
    lrow = row
    return _pcall(
        body, name="mix_out_fwd", grid=(l // TM,),
        out_shape=(jax.ShapeDtypeStruct((l, d), F32), jax.ShapeDtypeStruct((l, hm), F32),
                   jax.ShapeDtypeStruct((l, d), BF16), jax.ShapeDtypeStruct((l, d), BF16)),
        in_specs=[pl.BlockSpec((TM, hm), lrow), pl.BlockSpec((TM, hm), lrow), pl.BlockSpec((TM, hm), row),
                  pl.BlockSpec((TM, d), lrow), _mod_spec(d, False), _const(w_pool.shape), _const((1, hm)),
                  _const(w_glu.shape), _resident(w_out.shape)],
        out_specs=(pl.BlockSpec((TM, d), row), pl.BlockSpec((TM, hm), row), pl.BlockSpec((TM, d), row),
                   pl.BlockSpec((TM, d), row)),
        compiler_params=_params(("arbitrary",)),
    )(yf, yb, dpool, x1, modtab, w_pool, pscale, w_glu, w_out)


def _mix_out_bwd(dx2, mix, cat, ypre, dpool, modtab, w_pool, pscale, w_glu, w_out):
    l, hm = dpool.shape
    d = dx2.shape[1]
    nt = l // TML

    def body(dx_ref, mix_ref, cat_ref, yp_ref, dp_ref, mod_ref, wp_ref, ps_ref, wg_ref, wo_ref,
             dyp_ref, ddp_ref, dwo_ref, dwg_ref, dwp_ref, dps_ref, dgate_ref, acc_o, acc_g):
        i = pl.program_id(0)

        @pl.when(i == 0)
        def _():
            for ref in (acc_o, acc_g, dwp_ref, dps_ref, dgate_ref):
                ref[...] = jnp.zeros_like(ref)

        dx = dx_ref[...]
        dgate_ref[...] += _colsum(dx * mix_ref[...].astype(F32))
        dmix = (dx * mod_ref[0, 5:6, :]).astype(BF16)
        dcat = _dot_nt(dmix, wo_ref[...])
        acc_o[...] += _dot_tn(cat_ref[...], dmix)
        dpo, dy2 = dcat[:, :hm], dcat[:, hm:]
        dpb = dp_ref[...].astype(BF16)
        pp = _dot(dpb, wp_ref[...])
        dps_ref[...] += _colsum(dpo * pp)
        dpp = (dpo * ps_ref[...]).astype(BF16)
        ddp_ref[...] = _dot_nt(dpp, wp_ref[...])
        dwp_ref[...] += _dot_tn(dpb, dpp)
        ypre = yp_ref[...]
        yg = _gelu(ypre)
        ygb = yg.astype(BF16)
        s = _sigmoid(_dot(ygb, wg_ref[...]))
        dq = (dy2 * yg * s * (1.0 - s)).astype(BF16)
        dyg = dy2 * s + _dot_nt(dq, wg_ref[...])
        acc_g[...] += _dot_tn(ygb, dq)
        dyp_ref[...] = dyg * _gelu_grad(ypre)

        @pl.when(i == nt - 1)
        def _():
            dwo_ref[...] = acc_o[...].astype(BF16)
            dwg_ref[...] = acc_g[...].astype(BF16)

    row = lambda i: (i, 0)
    return _pcall(
        body, name="mix_out_bwd", grid=(nt,),
        out_shape=(jax.ShapeDtypeStruct((l, hm), F32), jax.ShapeDtypeStruct((l, hm), F32),
                   jax.ShapeDtypeStruct((d, d), BF16), jax.ShapeDtypeStruct((hm, hm), BF16),
                   jax.ShapeDtypeStruct((hm, hm), F32), jax.ShapeDtypeStruct((1, hm), F32),
                   jax.ShapeDtypeStruct((1, d), F32)),
        in_specs=[pl.BlockSpec((TML, d), row), pl.BlockSpec((TML, d), row), pl.BlockSpec((TML, d), row),
                  pl.BlockSpec((TML, hm), row), pl.BlockSpec((TML, hm), row), _mod_spec(d, False),
                  _const(w_pool.shape), _const((1, hm)), _const(w_glu.shape), _resident(w_out.shape)],
        out_specs=(pl.BlockSpec((TML, hm), row), pl.BlockSpec((TML, hm), row), _const((d, d)), _const((hm, hm)),
                   _const((hm, hm)), _const((1, hm)), _const((1, d))),
        scratch_shapes=[pltpu.VMEM((d, d), F32), pltpu.VMEM((hm, hm), F32)],
        compiler_params=_params(("arbitrary",)),
    )(dx2, mix, cat, ypre, dpool, modtab, w_pool, pscale, w_glu, w_out)


def _loss_head(xx, target, gg, loss_ref, dg_ref):
    d = xx.shape[-1]
    rstd = lax.rsqrt(jnp.mean(xx * xx, axis=-1, keepdims=True) + EPS)
    xhat = xx * rstd
    err = xhat * gg - target
    row_loss = jnp.mean(err * err, axis=-1, keepdims=True)
    loss_ref[...] += 0.5 * jnp.sum(row_loss, axis=0, keepdims=True)
    dout = err * (1.0 / d)
    dg_ref[...] += _colsum(dout * xhat)
    dxhat = dout * gg
    return rstd * (dxhat - xhat * jnp.mean(dxhat * xhat, axis=-1, keepdims=True))


def _adamw(parts, w, m, v, name):
    shape = w.shape
    c = shape[-1]
    r = int(np.prod(shape)) // c
    npart = parts.shape[0]
    tr = r
    for cand in (1024, 512, 256, 128, 64, 32, 16):
        if r % cand == 0 and cand * max(c, LANE) * 4 <= ADAM_BLOCK_BYTES:
            tr = cand
            break
    c1 = 1.0 - ADAM_B1
    c2 = 1.0 - ADAM_B2
    bc1 = 1.0 - ADAM_B1 ** ADAM_STEP
    bc2 = 1.0 - ADAM_B2 ** ADAM_STEP

    def body(p_ref, w_ref, m_ref, v_ref, g_ref, d_ref, mo_ref, vo_ref):
        g = p_ref[0].astype(F32)
        for k in range(1, npart):
            g = g + p_ref[k].astype(F32)
        mn = ADAM_B1 * m_ref[...] + c1 * g
        vn = ADAM_B2 * v_ref[...] + c2 * (g * g)
        m_hat = mn / bc1
        v_hat = vn / bc2
        g_ref[...] = g
        mo_ref[...] = mn
        vo_ref[...] = vn
        d_ref[...] = -ADAM_LR * (m_hat / (jnp.sqrt(v_hat) + ADAM_EPS) + ADAM_WD * w_ref[...])

    spec = pl.BlockSpec((tr, c), lambda i: (i, 0))
    outs = _pcall(
        body, name=name, grid=(r // tr,),
        out_shape=tuple(jax.ShapeDtypeStruct((r, c), F32) for _ in range(4)),
        in_specs=[pl.BlockSpec((npart, tr, c), lambda i: (0, i, 0)), spec, spec, spec],
        out_specs=(spec, spec, spec, spec),
        compiler_params=_params(("arbitrary",)),
    )(parts.reshape(npart, r, c), w.reshape(r, c), m.reshape(r, c), v.reshape(r, c))
    return tuple(o.reshape(shape) for o in outs)


def _adamw_ffn(parts, row0, w, m, v, name):
    _, nl, r, c = w.shape
    rp = parts[0][0].shape[1]
    tc = ADAM_FFN_LANES
    grid = (c // tc,)
    p_spec = lambda a: pl.BlockSpec((a.shape[0], rp, tc), lambda i: (0, 0, i))
    w_spec = pl.BlockSpec((1, nl, r, tc), lambda i: (0, 0, 0, i))
    take = lambda ref, k: ref[k, row0:row0 + r, :]
    flat = [a for layer in parts for a in layer]
    counts = [len(layer) for layer in parts]
    c1 = 1.0 - ADAM_B1
    c2 = 1.0 - ADAM_B2
    bc1 = 1.0 - ADAM_B1 ** ADAM_STEP
    bc2 = 1.0 - ADAM_B2 ** ADAM_STEP

    def body(*refs):
        w_ref, m_ref, v_ref, g_ref, d_ref, mo_ref, vo_ref = refs[len(flat):]
        first = 0
        for lyr in range(nl):
            g = None
            for ref in refs[first:first + counts[lyr]]:
                for k in range(ref.shape[0]):
                    term = take(ref, k).astype(F32)
                    g = term if g is None else g + term
            first += counts[lyr]
            mn = ADAM_B1 * m_ref[0, lyr] + c1 * g
            vn = ADAM_B2 * v_ref[0, lyr] + c2 * (g * g)
            g_ref[0, lyr] = g
            mo_ref[0, lyr] = mn
            vo_ref[0, lyr] = vn
            d_ref[0, lyr] = -ADAM_LR * ((mn / bc1) / (jnp.sqrt(vn / bc2) + ADAM_EPS) + ADAM_WD * w_ref[0, lyr])

    return _pcall(
        body, name=name, grid=grid, out_shape=tuple(jax.ShapeDtypeStruct(w.shape, F32) for _ in range(4)),
        in_specs=[p_spec(a) for a in flat] + [w_spec] * 3, out_specs=(w_spec,) * 4,
        compiler_params=_params(("arbitrary",)),
    )(*flat, w, m, v)


def _blockdiag(xb):
    n, a, b = xb.shape[-3:]
    eye = jnp.eye(n, dtype=xb.dtype)
    out = xb[..., :, :, None, :] * eye[:, None, :, None]
    return out.reshape(xb.shape[:-3] + (n * a, n * b))


def _diagblocks(mat, n):
    lead = mat.shape[:-2]
    a, b = mat.shape[-2] // n, mat.shape[-1] // n
    m5 = mat.reshape(lead + (n, a, n, b))
    nl = len(lead)
    dg = jnp.diagonal(m5, axis1=nl, axis2=nl + 2)
    return jnp.moveaxis(dg, -1, nl)


def _pad128(a):
    flat = a.reshape(-1)
    pad = (-flat.shape[0]) % LANE
    return jnp.pad(flat, (0, pad)) if pad else flat


def kernel(x, c, ctx, c_ctx, norm_g, w_ada, b_ada, ffn_w1, ffn_w3, ffn_w2, w_in, pool_w, pool_scale, s5_a_re, s5_a_im, s5_log_dt, s5_b_re, s5_b_im, s5_c_re, s5_c_im, s5_d, s5_w_glu, w_out, final_g, loss_target, m_c_ctx, m_norm_g, m_w_ada, m_b_ada, m_ffn_w1, m_ffn_w3, m_ffn_w2, m_w_in, m_pool_w, m_pool_scale, m_s5_a_re, m_s5_a_im, m_s5_log_dt, m_s5_b_re, m_s5_b_im, m_s5_c_re, m_s5_c_im, m_s5_d, m_s5_w_glu, m_w_out, m_final_g, v_c_ctx, v_norm_g, v_w_ada, v_b_ada, v_ffn_w1, v_ffn_w3, v_ffn_w2, v_w_in, v_pool_w, v_pool_scale, v_s5_a_re, v_s5_a_im, v_s5_log_dt, v_s5_b_re, v_s5_b_im, v_s5_c_re, v_s5_c_im, v_s5_d, v_s5_w_glu, v_w_out, v_final_g):
    weights = dict(c_ctx=c_ctx, norm_g=norm_g, w_ada=w_ada, b_ada=b_ada, ffn_w1=ffn_w1, ffn_w3=ffn_w3, ffn_w2=ffn_w2,
                   w_in=w_in, pool_w=pool_w, pool_scale=pool_scale, s5_a_re=s5_a_re, s5_a_im=s5_a_im,
                   s5_log_dt=s5_log_dt, s5_b_re=s5_b_re, s5_b_im=s5_b_im, s5_c_re=s5_c_re, s5_c_im=s5_c_im, s5_d=s5_d,
                   s5_w_glu=s5_w_glu, w_out=w_out, final_g=final_g)
    mom_m = dict(c_ctx=m_c_ctx, norm_g=m_norm_g, w_ada=m_w_ada, b_ada=m_b_ada, ffn_w1=m_ffn_w1, ffn_w3=m_ffn_w3,
                 ffn_w2=m_ffn_w2, w_in=m_w_in, pool_w=m_pool_w, pool_scale=m_pool_scale, s5_a_re=m_s5_a_re,
                 s5_a_im=m_s5_a_im, s5_log_dt=m_s5_log_dt, s5_b_re=m_s5_b_re, s5_b_im=m_s5_b_im, s5_c_re=m_s5_c_re,
                 s5_c_im=m_s5_c_im, s5_d=m_s5_d, s5_w_glu=m_s5_w_glu, w_out=m_w_out, final_g=m_final_g)
    mom_v = dict(c_ctx=v_c_ctx, norm_g=v_norm_g, w_ada=v_w_ada, b_ada=v_b_ada, ffn_w1=v_ffn_w1, ffn_w3=v_ffn_w3,
                 ffn_w2=v_ffn_w2, w_in=v_w_in, pool_w=v_pool_w, pool_scale=v_pool_scale, s5_a_re=v_s5_a_re,
                 s5_a_im=v_s5_a_im, s5_log_dt=v_s5_log_dt, s5_b_re=v_s5_b_re, s5_b_im=v_s5_b_im, s5_c_re=v_s5_c_re,
                 s5_c_im=v_s5_c_im, s5_d=v_s5_d, s5_w_glu=v_s5_w_glu, w_out=v_w_out, final_g=v_final_g)

    l, d = x.shape[1], x.shape[2]
    lc = ctx.shape[1]
    t = l + TM
    rows = l // GRID_W
    fblk = ffn_w1.shape[-1]
    ngrp, gp = s5_a_re.shape[2], s5_a_re.shape[3]
    gc = s5_b_re.shape[3]
    hm = ngrp * gc
    nsb = 4
    gsb = ngrp // nsb
    assert lc == TC and TM % TC == 0 and l % TM == 0 and TM == TML and hm == 4 * LANE and ngrp * gp == nsb * 512
    me = 4 * lax.axis_index("x") + 2 * lax.axis_index("y") + lax.axis_index("c")

    padr = ((0, FPAD - fblk), (0, 0))
    w1_t, w3_t = jnp.swapaxes(ffn_w1, 2, 3), jnp.swapaxes(ffn_w3, 2, 3)
    w13_loc = [jnp.concatenate([jnp.pad(w1_t[0, k], padr), jnp.pad(w3_t[0, k], padr)], axis=0).astype(BF16)
               for k in range(2)]
    w2_loc = [jnp.pad(ffn_w2[0, k], padr).astype(BF16) for k in range(2)]
    wa = w_ada.shape[-1]
    b_blk = lax.dynamic_slice_in_dim(b_ada, me * wa, wa, axis=1)
    w13_0, w2_0, g_all, c_all, mod_all = _front(w13_loc[0], w2_loc[0], norm_g[0], c, c_ctx.reshape(1, d), w_ada[0],
                                                b_blk)
    w2_0 = w2_0.reshape(NDEV // 2, 2 * FPAD, d)
    c_all = c_all.reshape(NDEV, d)
    g_all = g_all.transpose(1, 0, 2).reshape(3, d)

    c16 = jnp.concatenate([c_all, c_ctx.reshape(1, d), jnp.zeros((7, d), F32)], axis=0)
    mod_full = mod_all.transpose(1, 0, 2).reshape(16, NDEV * wa)
    mod_l = lax.dynamic_index_in_dim(mod_full, me, axis=0, keepdims=False).reshape(9, d)
    modtab = jnp.stack([mod_full[8].reshape(9, d), mod_l])

    x0 = [jnp.pad(ctx[0], ((0, TM - lc), (0, 0))), x[0]]
    g0, g1, g2 = g_all[0:1], g_all[1:2], g_all[2:3]
    x1, h0, ab0, o0, w13_1, w2_1, w_in_f, w_out_f, w_glu_f = _ffn_fwd(
        x0, modtab, g0, w13_0, w2_0, 0, True, "ffn0_fwd",
        xchg=[(w13_loc[1], GATHER_REL), (w2_loc[1], GATHER_REL), (w_in[0].astype(BF16), GATHER),
              (w_out[0].astype(BF16), GATHER), (s5_w_glu[0].astype(BF16), GATHER)])
    w2_1 = w2_1.reshape(NDEV // 2, 2 * FPAD, d)
    w_in_f = w_in_f.reshape(d, -1)
    w_out_f = w_out_f.reshape(-1, d)
    w_glu_f = w_glu_f.reshape(hm, hm)
    h1, u_pool, up = _mix_in_fwd(x1, modtab, g1, w_in_f)

    dpool = _pool(u_pool, rows, False, "pool_fwd")
    w_pool_bd = _blockdiag(pool_w[0]).astype(BF16)

    q = 2 * ngrp
    kvec = np.concatenate([np.arange(1, SEG + 1), SEG * np.arange(1, 9), SEG * np.arange(8, 0, -1)]).astype(np.float32)
    a_re3, a_im3 = s5_a_re[0].reshape(q, 1, gp), s5_a_im[0].reshape(q, 1, gp)
    ldt3 = jnp.broadcast_to(s5_log_dt[0].reshape(q, 1, 1), (q, 1, gp))
    c_re3, c_im3 = s5_c_re[0].reshape(q, gc, gp), s5_c_im[0].reshape(q, gc, gp)
    cp_re, cp_im, tab_re, tab_im = _s5_prep(a_re3, a_im3, ldt3, c_re3, c_im3, jnp.broadcast_to(jnp.asarray(kvec).reshape(1, NTAB, 1), (1, NTAB, gp)))
    tabs = [(tab_re.reshape(2, ngrp, NTAB, gp)[k].transpose(1, 0, 2).reshape(NTAB, ngrp * gp),
             tab_im.reshape(2, ngrp, NTAB, gp)[k].transpose(1, 0, 2).reshape(NTAB, ngrp * gp)) for k in range(2)]
    b_t = lambda b: _blockdiag(b[0].reshape(nsb, gsb, gp, gc).transpose(0, 1, 3, 2)).astype(BF16)
    bre, bim = b_t(s5_b_re), b_t(s5_b_im)
    cp_t = lambda cp: _blockdiag(cp.reshape(2, nsb, gsb, gc, gp)).astype(BF16)
    ctr, cti = cp_t(cp_re), cp_t(cp_im)
    cmr, cmi = jnp.swapaxes(ctr, -1, -2), jnp.swapaxes(cti, -1, -2)
    dskip = s5_d

    lch = l // TC
    nproc = lch + 1
    order_f = lambda i: jnp.where(i == 0, lch, i - 1)
    order_b = lambda i: jnp.where(i == 0, lch, lch - i)
    rorder_f = lambda i: jnp.where(i == lch, lch, lch - 1 - i)
    rorder_b = lambda i: i
    y_f, hr_f, hi_f, cinr_f, cini_f = _s5_fwd(up, bre, bim, cmr[0], cmi[0], tabs[0][0], tabs[0][1], dskip,
                                               order_f, nproc, False, True, "s5_fwd_f")
    y_b, hr_b, hi_b, cinr_b, cini_b = _s5_fwd(up, bre, bim, cmr[1], cmi[1], tabs[1][0], tabs[1][1], dskip,
                                              order_b, nproc, True, False, "s5_fwd_b")

    x2, ypre, cat, mix = _mix_out_fwd(y_f, y_b, dpool, x1, modtab, w_pool_bd, pool_scale, w_glu_f, w_out_f)
    dx3, h2, ab2, o2, loss_part, dfinal_g = _ffn_fwd([x2], modtab, g2, w13_1, w2_1, 6, False, "ffn1_fwd",
                                                     head=(loss_target[0], final_g.reshape(1, d)))

    dhp2, d13_1, d2_1 = _ffn_bwd(dx3, modtab, h2, ab2, w13_1, w2_1, 6, False, "ffn1_bwd")
    dx2, dmod_678, dg2 = _ffn_bwd_norm(dx3, [dhp2], [x2], o2, modtab, g2, 6, False, "ffn1_bwd_norm")
    dypre, ddpool, dw_out, dw_glu, dw_pool_bd, dpscale, dgate5 = _mix_out_bwd(
        dx2, mix, cat, ypre, dpool, modtab, w_pool_bd, pool_scale, w_glu_f, w_out_f)

    du_f, dar_f, dai_f, dcr_f, dci_f, dbr_f, dbi_f, dd_skip, r13_1 = _s5_bwd(
        dypre, up, hr_f, hi_f, cinr_f, cini_f, bre, bim, ctr[0], cti[0], tabs[0][0], tabs[0][1], dskip,
        rorder_f, nproc, False, True, "s5_bwd_f",
        xchg=[(d13_1, A2A_REL)])
    du_b, dar_b, dai_b, dcr_b, dci_b, dbr_b, dbi_b, _, r2_1, r_out, r_glu = _s5_bwd(
        dypre, up, hr_b, hi_b, cinr_b, cini_b, bre, bim, ctr[1], cti[1], tabs[1][0], tabs[1][1], dskip,
        rorder_b, nproc, True, False, "s5_bwd_b",
        xchg=[(d2_1.reshape(NDEV, FPAD, d), A2A_REL), (dw_out.reshape(NDEV, -1, d), A2A),
              (dw_glu.reshape(NDEV, hm // NDEV, hm), A2A)])

    dab_r = jnp.stack([dar_f, dar_b]).reshape(q, 1, gp)
    dab_i = jnp.stack([dai_f, dai_b]).reshape(q, 1, gp)
    dcp_r = _diagblocks(jnp.stack([dcr_f, dcr_b]), gsb).reshape(q, gc, gp)
    dcp_i = _diagblocks(jnp.stack([dci_f, dci_b]), gsb).reshape(q, gc, gp)
    ga_re, ga_im, gldt, gc_re, gc_im = _s5_param_bwd(a_re3, a_im3, ldt3, c_re3, c_im3, dab_r, dab_i, dcp_r, dcp_i)
    gb_re = (_diagblocks(dbr_f, gsb) + _diagblocks(dbr_b, gsb)).transpose(0, 1, 3, 2).reshape(ngrp, gp, gc)
    gb_im = (_diagblocks(dbi_f, gsb) + _diagblocks(dbi_b, gsb)).transpose(0, 1, 3, 2).reshape(ngrp, gp, gc)

    def pack(small):
        offs, pieces, off = {}, [], 0
        for k_, a_ in small.items():
            p_ = _pad128(a_.astype(F32))
            offs[k_] = (off, int(np.prod(a_.shape)))
            off += p_.shape[0]
            pieces.append(p_)
        return jnp.concatenate(pieces).reshape(1, off), offs

    gw_pool = _diagblocks(dw_pool_bd, 4)
    bundle_a, offs = pack(dict(pool_w=gw_pool, pool_scale=dpscale, s5_a_re=ga_re, s5_a_im=ga_im,
                               s5_log_dt=gldt[:, 0, 0], s5_b_re=gb_re, s5_b_im=gb_im, s5_c_re=gc_re, s5_c_im=gc_im,
                               s5_d=dd_skip, final_g=dfinal_g))

    du_pool = _pool(ddpool, rows, True, "pool_bwd")
    dx1, dw_in, dmod_34, dg1 = _mix_in_bwd(du_pool, du_f, du_b, dx2, h1, x1, modtab, g1, w_in_f, lc)
    dhp0a, d13_0a, d2_0a, bund_a, r_in = _ffn_bwd(
        dx1, modtab, h0, ab0, w13_0, w2_0, 0, True, "ffn0_bwd_a", pair0=0, npair=2,
        xchg=[(bundle_a, GATHER), (dw_in.reshape(NDEV, d // NDEV, -1), A2A)])
    dhp0b, d13_0b, d2_0b, r13_0a, r2_0a = _ffn_bwd(
        dx1, modtab, h0, ab0, w13_0, w2_0, 0, True, "ffn0_bwd_b", pair0=2, npair=2,
        xchg=[(d13_0a, A2A_SAME), (d2_0a.reshape(NDEV // 2, FPAD, d), A2A_SAME)])
    r13_0b, r2_0b = _sequencer_a2a_other([d13_0b, d2_0b.reshape(NDEV // 2, FPAD, d)], "seq_a2a_ffn0")
    dx0, dmod_012, dg0 = _ffn_bwd_norm(dx1, [dhp0a, dhp0b], x0, o0, modtab, g0, 0, True, "ffn0_bwd_norm")
    grad_x = dx0.reshape(1, l, d)

    dmod_c = jnp.concatenate([dmod_012[0], dmod_34[0], jnp.zeros((4, d), F32)], axis=0)
    dmod_l = jnp.concatenate([dmod_012[1], dmod_34[1], dgate5, dmod_678[0]], axis=0)
    bundle_b, offs_b = pack(dict(norm_g=jnp.concatenate([dg0, dg1, dg2], axis=0), dmod_l=dmod_l, dmod_c=dmod_c,
                                 loss=loss_part[:, :1]))
    bund_b = _exchange([(bundle_b, GATHER)], "exchange_small")[0]
    bunds = {**{k_: (bund_a.reshape(NDEV, -1), v_) for k_, v_ in offs.items()},
             **{k_: (bund_b.reshape(NDEV, -1), v_) for k_, v_ in offs_b.items()}}
    r13, r2 = [(r13_0a, r13_0b), (r13_1,)], [(r2_0a, r2_0b), (r2_1,)]

    def piece(name):
        b_, (o_, n_) = bunds[name]
        return b_[:, o_:o_ + n_]

    loss = jnp.sum(piece("loss")[:, 0])

    dl_all = lax.dynamic_slice_in_dim(piece("dmod_l"), me * wa, wa, axis=1)
    dc_all = lax.dynamic_slice_in_dim(piece("dmod_c"), me * wa, wa, axis=1)
    g_wada, gc_part = _ada_bwd(c16, w_ada[0], dl_all, dc_all)
    gc_all = _exchange([(gc_part, GATHER)], "exchange_cctx")[0]

    parts = {
        "c_ctx": gc_all.reshape(NDEV, d),
        "norm_g": lax.dynamic_slice_in_dim(piece("norm_g").reshape(NDEV, 3, d), me * (d // NDEV), d // NDEV,
                                           axis=2).reshape((NDEV,) + norm_g.shape),
        "w_ada": g_wada.reshape((1,) + w_ada.shape),
        "b_ada": jnp.concatenate([piece("dmod_l"), piece("dmod_c")], axis=0).reshape((2 * NDEV,) + b_ada.shape),
        "w_in": r_in.reshape((NDEV,) + w_in.shape),
        "s5_w_glu": r_glu.reshape((NDEV,) + s5_w_glu.shape),
        "w_out": r_out.reshape((NDEV,) + w_out.shape),
    }
    for k_ in ("pool_w", "pool_scale", "s5_a_re", "s5_a_im", "s5_log_dt", "s5_b_re", "s5_b_im", "s5_c_re", "s5_c_im",
               "s5_d", "final_g"):
        parts[k_] = piece(k_).reshape((NDEV,) + weights[k_].shape)

    grads, deltas, new_m, new_v = [], [], [], []
    ffn_parts = {"ffn_w1": (r13, 0, True), "ffn_w3": (r13, FPAD, True), "ffn_w2": (r2, 0, False)}
    for k_ in weights:
        if k_ in ffn_parts:
            shards, row0, transposed = ffn_parts[k_]
            flip = (lambda a: jnp.swapaxes(a, 2, 3)) if transposed else (lambda a: a)
            outs = _adamw_ffn(shards, row0, flip(weights[k_]), flip(mom_m[k_]), flip(mom_v[k_]), "adamw_" + k_)
            g_, d_, m_, v_ = (flip(o_) for o_ in outs)
        else:
            g_, d_, m_, v_ = _adamw(parts[k_], weights[k_], mom_m[k_], mom_v[k_], "adamw_" + k_)
        grads.append(g_)
        deltas.append(d_)
        new_m.append(m_)
        new_v.append(v_)
    return (loss, grad_x, *grads, *deltas, *new_m, *new_v)
```

```python
import functools
import math

import numpy as np
import jax
import jax.numpy as jnp
from jax import lax
from jax.experimental import pallas as pl
from jax.experimental.pallas import tpu as pltpu
from jax.experimental.pallas import tpu_sc as plsc

F32 = jnp.float32
BF16 = jnp.bfloat16
AXES = ("x", "y", "c")
NDEV = 8
EPS = 1e-6
TM = 512
TML = 512
TC = 256
SEG = TC // 8
NTAB = SEG + 16
S5_UNROLL = True
GRID_W = 64
POOL_WINDOWS = (2, 4, 8, 16)
LANE = 128
FPAD = 384
VMEM_LIMIT = 56 * 1024 * 1024
ADAM_BLOCK_BYTES = 1024 * 1024
ADAM_FFN_LANES = 256
SEQ_COLLECTIVE_ID = 0
SEQ_GATHER_COLLECTIVE_ID = 1
ADAM_LR, ADAM_B1, ADAM_B2, ADAM_EPS, ADAM_WD, ADAM_STEP = 0.001, 0.9, 0.999, 1e-08, 0.01, 10


def _raw_call(body, kw):
    return pl.pallas_call(body, **kw)


def _pcall(body, xchg=None, edges=None, **kw):
    extra = ()
    if xchg:
        arrs, kinds = [a for a, _ in xchg], [k for _, k in xchg]
        n = len(arrs)
        n_in, n_out, n_scr = len(kw["in_specs"]), len(kw["out_shape"]), len(kw.get("scratch_shapes", ()))
        inner = body

        def hosted(*refs):
            a, b = n_in, n_in + n
            c_, d_ = b + n_out, b + n_out + n
            e = d_ + n_scr
            first, mid, last = edges()

            @pl.when(first)
            def _():
                _xchg_start(refs[a:b], refs[c_:d_], refs[e:], kinds)

            inner(*refs[:a], *refs[b:c_], *refs[d_:e])

            if any(k == GATHER_REL for k in kinds):
                @pl.when(mid)
                def _():
                    _xchg_relay(refs[a:b], refs[c_:d_], refs[e:], kinds)

            @pl.when(last)
            def _():
                _xchg_finish(refs[a:b], refs[c_:d_], refs[e:], kinds)

        body = hosted
        any_spec = pl.BlockSpec(memory_space=pl.ANY)
        kw = dict(kw, in_specs=list(kw["in_specs"]) + [any_spec] * n,
                  out_shape=tuple(kw["out_shape"]) + _xchg_shapes(arrs, kinds),
                  out_specs=tuple(kw["out_specs"]) + (any_spec,) * n,
                  scratch_shapes=list(kw.get("scratch_shapes", ())) + _xchg_sems(n))
        extra = tuple(arrs)
    call = _raw_call(body, kw)
    return (lambda *args: call(*args, *extra)) if extra else call


def _params(sem):
    return pltpu.CompilerParams(dimension_semantics=sem, vmem_limit_bytes=VMEM_LIMIT)


def _dot(a, b):
    return jnp.dot(a, b, preferred_element_type=F32)


def _dot_nt(a, b):
    return lax.dot_general(a, b, (((1,), (1,)), ((), ())), preferred_element_type=F32)


def _dot_tn(a, b):
    return lax.dot_general(a, b, (((0,), (0,)), ((), ())), preferred_element_type=F32)


def _sigmoid(x):
    return 1.0 / (1.0 + jnp.exp(-x))


def _colsum(a):
    return jnp.sum(a, axis=0, keepdims=True)


def _resident(shape):
    nd = len(shape)
    return pl.BlockSpec(shape, lambda *_: (0,) * nd, pipeline_mode=pl.Buffered(1))


def _const(shape):
    nd = len(shape)
    return pl.BlockSpec(shape, lambda *_: (0,) * nd)


GATHER = "gather"
GATHER_REL = "gather_rel"
A2A = "a2a"
A2A_REL = "a2a_rel"
A2A_SAME = "a2a_same"
A2A_OTHER = "a2a_other"


def _exchange(items, name):
    arrs, kinds = [a for a, _ in items], [k for _, k in items]
    n = len(arrs)

    def body(*refs):
        _xchg_start(refs[:n], refs[n:2 * n], refs[2 * n:], kinds)
        _xchg_relay(refs[:n], refs[n:2 * n], refs[2 * n:], kinds)
        _xchg_finish(refs[:n], refs[n:2 * n], refs[2 * n:], kinds)

    any_spec = pl.BlockSpec(memory_space=pl.ANY)
    outs = _pcall(
        body, name=name, out_shape=_xchg_shapes(arrs, kinds), in_specs=[any_spec] * n, out_specs=[any_spec] * n,
        scratch_shapes=_xchg_sems(n),
    )(*arrs)
    return list(outs)


def _sequencer_a2a_other(arrs, name):
    hbm = pltpu.MemorySpace.HBM
    srcs = [jax.new_ref(a, memory_space=hbm) for a in arrs]
    dsts = [jax.empty_ref(jax.ShapeDtypeStruct(a.shape, a.dtype), memory_space=hbm) for a in arrs]
    masks = (1, 5, 3, 7)
    nsem = len(arrs) * len(masks)

    @pl.kernel(mesh=plsc.ScalarSubcoreMesh(axis_name="seq", num_cores=1), name=name,
               scratch_types=(pltpu.SemaphoreType.DMA,) * (2 * nsem),
               compiler_params=pltpu.CompilerParams(collective_id=SEQ_COLLECTIVE_ID))
    def launch(*sems):
        x, y, c = (lax.axis_index(a) for a in AXES)
        my_chip = 2 * x + y
        peers = []
        for p in masks:
            px = 1 - x if p & 4 else x
            py = 1 - y if p & 2 else y
            peers.append(((px, py, 1 - c), 2 * px + py))
        barrier = pltpu.get_barrier_semaphore()
        for dev, _ in peers:
            pl.semaphore_signal(barrier, inc=1, device_id=dev, device_id_type=pl.DeviceIdType.MESH)
        pl.semaphore_wait(barrier, len(peers))
        copies = []
        for k in range(len(arrs)):
            for q, (dev, pchip) in enumerate(peers):
                i = k * len(masks) + q
                send = pltpu.make_async_remote_copy(
                    src_ref=srcs[k].at[pchip], dst_ref=dsts[k].at[my_chip], send_sem=sems[i], recv_sem=sems[nsem + i],
                    device_id=dev, device_id_type=pl.DeviceIdType.MESH)
                recv = pltpu.make_async_remote_copy(
                    src_ref=srcs[k].at[pchip], dst_ref=dsts[k].at[pchip], send_sem=sems[i], recv_sem=sems[nsem + i],
                    device_id=dev, device_id_type=pl.DeviceIdType.MESH)
                send.start()
                copies.append((send, recv))
        for send, recv in copies:
            recv.wait_recv()
            send.wait_send()

    launch()
    return [d[...] for d in dsts]


def _sequencer_gather(block, name):
    hbm = pltpu.MemorySpace.HBM
    src = jax.new_ref(block, memory_space=hbm)
    dst = jax.empty_ref(jax.ShapeDtypeStruct((NDEV,) + block.shape, block.dtype), memory_space=hbm)
    npeer = NDEV - 1

    @pl.kernel(mesh=plsc.ScalarSubcoreMesh(axis_name="seq", num_cores=1), name=name,
               scratch_types=(pltpu.SemaphoreType.DMA,) * (2 * npeer),
               compiler_params=pltpu.CompilerParams(collective_id=SEQ_GATHER_COLLECTIVE_ID))
    def launch(*sems):
        x, y, c = (lax.axis_index(a) for a in AXES)
        my_abs = 4 * x + 2 * y + c
        peers = []
        for p in range(1, NDEV):
            px = 1 - x if p & 4 else x
            py = 1 - y if p & 2 else y
            pc = 1 - c if p & 1 else c
            peers.append(((px, py, pc), 4 * px + 2 * py + pc))
        barrier = pltpu.get_barrier_semaphore()
        for dev, _ in peers:
            pl.semaphore_signal(barrier, inc=1, device_id=dev, device_id_type=pl.DeviceIdType.MESH)
        pl.semaphore_wait(barrier, npeer)
        copies = []
        for i, (dev, pabs) in enumerate(peers):
            send = pltpu.make_async_remote_copy(
                src_ref=src, dst_ref=dst.at[my_abs], send_sem=sems[i], recv_sem=sems[npeer + i],
                device_id=dev, device_id_type=pl.DeviceIdType.MESH)
            recv = pltpu.make_async_remote_copy(
                src_ref=src, dst_ref=dst.at[pabs], send_sem=sems[i], recv_sem=sems[npeer + i],
                device_id=dev, device_id_type=pl.DeviceIdType.MESH)
            send.start()
            copies.append((send, recv))
        for send, recv in copies:
            recv.wait_recv()
            send.wait_send()

    launch()
    return dst[...]


def _edges1(n0):
    return lambda: (pl.program_id(0) == 0, pl.program_id(0) == (7 * n0) // 8, pl.program_id(0) == n0 - 1)


def _edges2(n0, n1):
    def edges():
        step = pl.program_id(0) * n1 + pl.program_id(1)
        return step == 0, step == (7 * n0 * n1) // 8, step == n0 * n1 - 1
    return edges


def _xchg_shapes(arrs, kinds):
    return tuple(jax.ShapeDtypeStruct(((NDEV,) if k in (GATHER, GATHER_REL) else ()) + tuple(a.shape), a.dtype)
                 for a, k in zip(arrs, kinds))


def _xchg_sems(n):
    return [pltpu.SemaphoreType.DMA((n * (NDEV - 1),)), pltpu.SemaphoreType.DMA((n * (NDEV - 1),)),
            pltpu.SemaphoreType.DMA((n,))]


def _xchg_plan(in_refs, out_refs, sems, kinds):
    send_sems, recv_sems, loc_sems = sems
    x, y, c = (lax.axis_index(a) for a in AXES)
    my_abs, my_chip = 4 * x + 2 * y + c, 2 * x + y

    def peer(p):
        px = 1 - x if p & 4 else x
        py = 1 - y if p & 2 else y
        pc = 1 - c if p & 1 else c
        return (px, py, pc), 4 * px + 2 * py + pc, 2 * px + py

    starts, relays, recvs = [], [], []
    for k, kind in enumerate(kinds):
        src, dst = in_refs[k], out_refs[k]

        def remote(src_ref, row, p, pair, dst=dst, k=k):
            dev, sem = peer(p)[0], k * (NDEV - 1) + pair - 1
            return lambda: pltpu.make_async_remote_copy(
                src_ref=src_ref, dst_ref=dst.at[row], send_sem=send_sems.at[sem], recv_sem=recv_sems.at[sem],
                device_id=dev, device_id_type=pl.DeviceIdType.MESH)

        def local(src_ref, row, dst=dst, k=k):
            return lambda: pltpu.make_async_copy(src_ref, dst.at[row], loc_sems.at[k])

        if kind == GATHER:
            for p in range(1, NDEV):
                starts.append((remote(src, my_abs, p, p), False))
                recvs.append(remote(src, peer(p)[1], p, p))
            starts.append((local(src, my_abs), True))
        elif kind == GATHER_REL:
            for p in (1, 4, 2, 6):
                starts.append((remote(src, 4 * (p & 1) + my_chip, p, p), False))
            for q in (4, 2, 6):
                pchip = peer(q)[2]
                relays.append((remote(src, pchip, q, q), remote(dst.at[pchip], 4 + pchip, 1, q | 1)))
                recvs.append(remote(src, 4 + pchip, 1, q | 1))
            recvs.append(remote(src, 4 + my_chip, 1, 1))
            starts.append((local(src, my_chip), True))
        elif kind in (A2A, A2A_REL):
            for p in range(1, NDEV):
                _, pabs, pchip = peer(p)
                theirs, mine = (pabs, my_abs) if kind == A2A else (4 * (p & 1) + pchip, 4 * (p & 1) + my_chip)
                starts.append((remote(src.at[theirs], mine, p, p), False))
                recvs.append(remote(src.at[theirs], theirs, p, p))
            own = my_abs if kind == A2A else my_chip
            starts.append((local(src.at[own], own), True))
        else:
            for p in ((4, 2, 6) if kind == A2A_SAME else (1, 5, 3, 7)):
                pchip = peer(p)[2]
                starts.append((remote(src.at[pchip], my_chip, p, p), False))
                recvs.append(remote(src.at[pchip], pchip, p, p))
            if kind == A2A_SAME:
                starts.append((local(src.at[my_chip], my_chip), True))
    return starts, relays, recvs


def _xchg_start(in_refs, out_refs, sems, kinds):
    for make, _ in _xchg_plan(in_refs, out_refs, sems, kinds)[0]:
        make().start()


def _xchg_relay(in_refs, out_refs, sems, kinds):
    for arrival, forward in _xchg_plan(in_refs, out_refs, sems, kinds)[1]:
        arrival().wait_recv()
        forward().start()


def _xchg_finish(in_refs, out_refs, sems, kinds):
    starts, relays, recvs = _xchg_plan(in_refs, out_refs, sems, kinds)
    for make in recvs:
        make().wait_recv()
    for make, is_local in starts:
        if is_local:
            make().wait()
        else:
            make().wait_send()
    for _, forward in relays:
        forward().wait_send()


def _front(w13, w2, ng, c, c_ctx, w_blk, b_blk):
    d, wa = w_blk.shape
    big, big_kinds = [w13, w2, ng], [GATHER_REL, GATHER_REL, GATHER]

    def body(w13_ref, w2_ref, ng_ref, c_ref, cctx_ref, w_ref, b_ref, g13_ref, g2_ref, gng_ref, gc_ref, msl_ref, gm_ref,
             c16, mod, loc_sems, *sems):
        big_io = ((w13_ref, w2_ref, ng_ref), (g13_ref, g2_ref, gng_ref), sems[0:3], big_kinds)
        c_io = ((c_ref,), (gc_ref,), sems[3:6], [GATHER])
        mod_io = ((msl_ref,), (gm_ref,), sems[6:9], [GATHER])
        _xchg_start(*c_io)
        _xchg_start(*big_io)
        _xchg_finish(*c_io)
        rows = [pltpu.make_async_copy(gc_ref.at[k], c16.at[pl.ds(k, 1), :], loc_sems.at[k]) for k in range(NDEV)]
        for cp in rows:
            cp.start()
        c16[NDEV:NDEV + 1, :] = cctx_ref[...]
        c16[NDEV + 1:, :] = jnp.zeros((16 - NDEV - 1, d), F32)
        for cp in rows:
            cp.wait()
        cc = c16[...]
        mod[...] = _dot((cc * _sigmoid(cc)).astype(BF16), w_ref[...].astype(BF16)) + b_ref[...]
        out = pltpu.make_async_copy(mod, msl_ref, loc_sems.at[NDEV])
        out.start()
        out.wait()
        _xchg_start(*mod_io)
        _xchg_relay(*big_io)
        _xchg_finish(*big_io)
        _xchg_finish(*mod_io)

    any_spec = pl.BlockSpec(memory_space=pl.ANY)
    vmem = pl.BlockSpec(memory_space=pltpu.VMEM)
    outs = _pcall(
        body, name="exchange_weights",
        out_shape=_xchg_shapes(big + [c], big_kinds + [GATHER]) + (jax.ShapeDtypeStruct((16, wa), F32),
                                                                   jax.ShapeDtypeStruct((NDEV, 16, wa), F32)),
        in_specs=[any_spec] * 4 + [vmem] * 3, out_specs=[any_spec] * 6,
        scratch_shapes=[pltpu.VMEM((16, d), F32), pltpu.VMEM((16, wa), F32), pltpu.SemaphoreType.DMA((NDEV + 1,))]
        + _xchg_sems(3) + _xchg_sems(1) + _xchg_sems(1),
        compiler_params=_params(None),
    )(w13, w2, ng, c, c_ctx, w_blk, b_blk)
    return outs[0], outs[1], outs[2], outs[3], outs[5]


def _ada_bwd(c16, w_blk, dl, dc):
    d, w = w_blk.shape

    def body(c_ref, w_ref, dl_ref, dc_ref, gw_ref, gc_ref):
        cc = c_ref[...]
        sg = _sigmoid(cc)
        s = (cc * sg).astype(BF16)
        dctot = _colsum(dc_ref[...])
        dm = jnp.concatenate([dl_ref[...], jnp.broadcast_to(dctot, (8, w))], axis=0)
        rows = lax.broadcasted_iota(jnp.int32, (16, w), 0)
        dm = jnp.where(rows <= 8, dm, 0.0).astype(BF16)
        gw_ref[...] = _dot_tn(s, dm)
        t = _dot_nt(jnp.broadcast_to(dctot, (8, w)).astype(BF16), w_ref[...].astype(BF16))[0:1]
        c8, s8 = cc[8:9], sg[8:9]
        gc_ref[...] = t * (s8 * (1.0 + c8 * (1.0 - s8)))

    return _pcall(body, name="ada_bwd",
                  out_shape=(jax.ShapeDtypeStruct((d, w), F32), jax.ShapeDtypeStruct((1, d), F32)),
                  compiler_params=_params(None))(c16, w_blk, dl, dc)


def _norm_fwd(x, g, shift, scale):
    rstd = lax.rsqrt(jnp.mean(x * x, axis=-1, keepdims=True) + EPS)
    xhat = x * rstd
    xn = xhat * g
    return rstd, xhat, xn, xn * (1.0 + scale) + shift


def _norm_bwd(dh, rstd, xhat, xn, g, scale):
    dxn = dh * (1.0 + scale)
    dxhat = dxn * g
    dx = rstd * (dxhat - xhat * jnp.mean(dxhat * xhat, axis=-1, keepdims=True))
    return dx, _colsum(dh), _colsum(dh * xn), _colsum(dxn * xhat)


def _mod_spec(d, has_ctx):
    if has_ctx:
        return pl.BlockSpec((1, 9, d), lambda *ids: (jnp.minimum(ids[-1], 1), 0, 0))
    return pl.BlockSpec((1, 9, d), lambda *ids: (1, 0, 0))


def _row_specs(nx, tm, d):
    if nx == 1:
        return [pl.BlockSpec((tm, d), lambda i: (i, 0))]
    return [pl.BlockSpec((tm, d), lambda i: (0, 0)), pl.BlockSpec((tm, d), lambda i: (jnp.maximum(i - 1, 0), 0))]


def _rows(refs):
    if len(refs) == 1:
        return refs[0][...]
    return jnp.where(pl.program_id(0) == 0, refs[0][...], refs[1][...])


def _tile(has_ctx):
    return TM if has_ctx else TML


def _tile_of(nt, has_ctx):
    if has_ctx:
        return lambda i: jnp.where(i == 0, nt - 1, i - 1)
    return lambda i: i


def _dmod_spec(nrow, d, has_ctx):
    if has_ctx:
        return pl.BlockSpec((1, nrow, d), lambda i: (jnp.minimum(i, 1), 0, 0))
    return pl.BlockSpec((1, nrow, d), lambda i: (0, 0, 0))


def _ffn_fwd(x, modtab, g, w13, w2, k0, has_ctx, name, xchg=None, head=None):
    t, d = sum(a.shape[0] for a in x), x[0].shape[1]
    nb, w, _ = w13.shape
    fp = w // 2
    tm = _tile(has_ctx)
    nx = len(x)

    nhead = 0 if head is None else 2

    def body(*refs):
        mod_ref, g_ref, w13_ref, w2_ref = refs[nx:nx + 4]
        xo_ref, h_ref, ab_ref, o_ref = refs[nx + 4 + nhead:nx + 8 + nhead]
        xx = _rows(refs[:nx])
        shift, scale, gate = mod_ref[0, k0:k0 + 1, :], mod_ref[0, k0 + 1:k0 + 2, :], mod_ref[0, k0 + 2:k0 + 3, :]
        _, _, _, h = _norm_fwd(xx, g_ref[...], shift, scale)
        hb = h.astype(BF16)
        h_ref[...] = hb
        acc = jnp.zeros((tm, d), F32)
        for p in range(nb // 2):
            zs = []
            for q in range(2):
                blk = 2 * p + q
                ab = _dot_nt(hb, w13_ref[blk])
                ab_ref[:, blk * w:(blk + 1) * w] = ab.astype(BF16)
                a, b = ab[:, :fp], ab[:, fp:]
                zs.append((a * _sigmoid(a) * b).astype(BF16))
            acc = acc + _dot(jnp.concatenate(zs, axis=1), w2_ref[p])
        o_ref[...] = acc.astype(BF16)
        xo = xx + (0.5 * gate) * acc
        if head is None:
            xo_ref[...] = xo
        else:
            tgt_ref, fg_ref = refs[nx + 4:nx + 6]
            loss_ref, dfg_ref = refs[nx + 8 + nhead:]

            @pl.when(pl.program_id(0) == 0)
            def _():
                loss_ref[...] = jnp.zeros_like(loss_ref)
                dfg_ref[...] = jnp.zeros_like(dfg_ref)

            xo_ref[...] = _loss_head(xo, tgt_ref[...], fg_ref[...], loss_ref, dfg_ref)

    tile = _tile_of(t // tm, has_ctx)
    row = lambda i: (tile(i), 0)
    head_in = [] if head is None else [pl.BlockSpec((tm, d), row), _const((1, d))]
    head_shape = () if head is None else (jax.ShapeDtypeStruct((1, LANE), F32), jax.ShapeDtypeStruct((1, d), F32))
    head_out = () if head is None else (_const((1, LANE)), _const((1, d)))
    return _pcall(
        body, name=name, grid=(t // tm,), xchg=xchg, edges=_edges1(t // tm),
        out_shape=(jax.ShapeDtypeStruct((t, d), F32), jax.ShapeDtypeStruct((t, d), BF16),
                   jax.ShapeDtypeStruct((t, nb * w), BF16), jax.ShapeDtypeStruct((t, d), BF16)) + head_shape,
        in_specs=_row_specs(nx, tm, d) + [_mod_spec(d, has_ctx), _const((1, d)),
                                          _resident(w13.shape), _resident(w2.shape)] + head_in,
        out_specs=(pl.BlockSpec((tm, d), row), pl.BlockSpec((tm, d), row), pl.BlockSpec((tm, nb * w), row),
                   pl.BlockSpec((tm, d), row)) + head_out,
        compiler_params=_params(("arbitrary",)),
    )(*x, modtab, g, w13, w2, *(head or ()))


def _ffn_bwd(dy, modtab, h, ab, w13, w2, k0, has_ctx, name, pair0=0, npair=None, xchg=None):
    t, d = dy.shape
    _, w, _ = w13.shape
    fp = w // 2
    npair = w13.shape[0] // 2 if npair is None else npair
    nb = 2 * npair
    tm = _tile(has_ctx)
    nt = t // tm

    def body(dy_ref, mod_ref, h_ref, ab_ref, w13_ref, w2_ref, dh_ref, d13_ref, d2_ref, acc13, acc2):
        i = pl.program_id(1)

        @pl.when(i == 0)
        def _():
            acc13[...] = jnp.zeros_like(acc13)
            acc2[...] = jnp.zeros_like(acc2)

        gate = mod_ref[0, k0 + 2:k0 + 3, :]
        do = (dy_ref[...] * (0.5 * gate)).astype(BF16)
        dz = _dot_nt(do, w2_ref[0])
        hb = h_ref[...]
        dh = jnp.zeros((tm, d), F32)
        zs = []
        for q in range(2):
            ab = ab_ref[:, q * w:(q + 1) * w].astype(F32)
            a, b = ab[:, :fp], ab[:, fp:]
            sg = _sigmoid(a)
            sa = a * sg
            dzq = dz[:, q * fp:(q + 1) * fp]
            da = dzq * b * (sg * (1.0 + a * (1.0 - sg)))
            db = dzq * sa
            dab = jnp.concatenate([da, db], axis=1).astype(BF16)
            dh = dh + _dot(dab, w13_ref[q])
            acc13[q] += _dot_tn(dab, hb)
            zs.append((sa * b).astype(BF16))
        acc2[...] += _dot_tn(jnp.concatenate(zs, axis=1), do)
        dh_ref[0] = dh.astype(BF16)

        @pl.when(i == nt - 1)
        def _():
            d13_ref[...] = acc13[...].astype(BF16)
            d2_ref[0] = acc2[...].astype(BF16)

    mod_spec = _mod_spec(d, has_ctx)
    tile = _tile_of(nt, has_ctx)
    return _pcall(
        body, name=name, grid=(npair, nt), xchg=xchg, edges=_edges2(npair, nt),
        out_shape=(jax.ShapeDtypeStruct((npair, t, d), BF16), jax.ShapeDtypeStruct((nb, w, d), BF16),
                   jax.ShapeDtypeStruct((npair, 2 * fp, d), BF16)),
        in_specs=[pl.BlockSpec((tm, d), lambda p, i: (tile(i), 0)), mod_spec,
                  pl.BlockSpec((tm, d), lambda p, i: (tile(i), 0)),
                  pl.BlockSpec((tm, 2 * w), lambda p, i: (tile(i), pair0 + p)),
                  pl.BlockSpec((2, w, d), lambda p, i: (pair0 + p, 0, 0)),
                  pl.BlockSpec((1, 2 * fp, d), lambda p, i: (pair0 + p, 0, 0))],
        out_specs=(pl.BlockSpec((1, tm, d), lambda p, i: (p, tile(i), 0)),
                   pl.BlockSpec((2, w, d), lambda p, i: (p, 0, 0)),
                   pl.BlockSpec((1, 2 * fp, d), lambda p, i: (p, 0, 0))),
        scratch_shapes=[pltpu.VMEM((2, w, d), F32), pltpu.VMEM((2 * fp, d), F32)],
        compiler_params=_params(("arbitrary", "arbitrary")),
    )(dy, modtab, h, ab, w13, w2)


def _ffn_bwd_norm(dy, dhps, x, o, modtab, g, k0, has_ctx, name, xchg=None):
    t, d = dy.shape
    ngrp = 2 if has_ctx else 1
    tm = _tile(has_ctx)
    ndh, nx = len(dhps), len(x)
    lat0 = ngrp - 1

    def body(dy_ref, *rest):
        dhp_refs, x_refs = rest[:ndh], rest[ndh:ndh + nx]
        o_ref, mod_ref, g_ref, dx_ref, dmod_ref, dg_ref = rest[ndh + nx:]
        i = pl.program_id(0)

        @pl.when(i == 0)
        def _():
            dg_ref[...] = jnp.zeros_like(dg_ref)

        @pl.when((i == 0) | (i == ngrp - 1))
        def _():
            dmod_ref[...] = jnp.zeros_like(dmod_ref)

        dh = None
        for ref in dhp_refs:
            for p in range(ref.shape[0]):
                dh = ref[p].astype(F32) if dh is None else dh + ref[p].astype(F32)
        scale = mod_ref[0, k0 + 1:k0 + 2, :]
        gg = g_ref[...]
        rstd, xhat, xn, _ = _norm_fwd(_rows(x_refs), gg, 0.0, scale)
        dxn, dshift, dscale, dg = _norm_bwd(dh, rstd, xhat, xn, gg, scale)
        dyv = dy_ref[...]
        dx_ref[...] = dyv + dxn
        dmod_ref[0, 0:1, :] += dshift
        dmod_ref[0, 1:2, :] += dscale
        dmod_ref[0, 2:3, :] += _colsum(0.5 * dyv * o_ref[...].astype(F32))
        dg_ref[...] += dg

    tile = _tile_of(t // tm, has_ctx)
    row = lambda i: (tile(i), 0)
    return _pcall(
        body, name=name, grid=(t // tm,), xchg=xchg, edges=_edges1(t // tm),
        out_shape=(jax.ShapeDtypeStruct((t - lat0 * tm, d), F32), jax.ShapeDtypeStruct((ngrp, 3, d), F32),
                   jax.ShapeDtypeStruct((1, d), F32)),
        in_specs=[pl.BlockSpec((tm, d), row)]
        + [pl.BlockSpec((a.shape[0], tm, d), lambda i: (0, tile(i), 0)) for a in dhps]
        + _row_specs(nx, tm, d) + [pl.BlockSpec((tm, d), row), _mod_spec(d, has_ctx), _const((1, d))],
        out_specs=(pl.BlockSpec((tm, d), lambda i: (jnp.maximum(i - lat0, 0), 0)), _dmod_spec(3, d, has_ctx),
                   _const((1, d))),
        compiler_params=_params(("arbitrary",)),
    )(dy, *dhps, *x, o, modtab, g)


def _mix_in_fwd(x1, modtab, g, w_in):
    t, d = x1.shape
    hm = w_in.shape[1] // 2

    def body(x_ref, mod_ref, g_ref, w_ref, h_ref, up_ref, us_ref):
        _, _, _, h = _norm_fwd(x_ref[...], g_ref[...], mod_ref[0, 3:4, :], mod_ref[0, 4:5, :])
        hb = h.astype(BF16)
        h_ref[...] = hb
        u = _dot(hb, w_ref[...])
        up_ref[...] = u[:, :hm]
        us_ref[...] = u[:, hm:]

    tile = _tile_of(t // TM, True)
    row = lambda i: (tile(i), 0)
    return _pcall(
        body, name="mix_in_fwd", grid=(t // TM,),
        out_shape=(jax.ShapeDtypeStruct((t, d), BF16), jax.ShapeDtypeStruct((t, hm), F32),
                   jax.ShapeDtypeStruct((t, hm), F32)),
        in_specs=[pl.BlockSpec((TM, d), row), _mod_spec(d, True), _const((1, d)), _resident(w_in.shape)],
        out_specs=(pl.BlockSpec((TM, d), row), pl.BlockSpec((TM, hm), row), pl.BlockSpec((TM, hm), row)),
        compiler_params=_params(("arbitrary",)),
    )(x1, modtab, g, w_in)


def _mix_in_bwd(du_pool, du_f, du_b, dx2, h1, x1, modtab, g, w_in, lc):
    t, d = x1.shape
    m = w_in.shape[1]
    hm = m // 2
    nt = t // TM

    def body(dup_ref, duf_ref, dub_ref, dx2_ref, h_ref, x_ref, mod_ref, g_ref, w_ref, dx_ref, dw_ref, dmod_ref, dg_ref, acc):
        i = pl.program_id(0)

        @pl.when(i == 0)
        def _():
            acc[...] = jnp.zeros_like(acc)
            dg_ref[...] = jnp.zeros_like(dg_ref)

        @pl.when(i <= 1)
        def _():
            dmod_ref[...] = jnp.zeros_like(dmod_ref)

        lat = (i > 0).astype(F32)
        valid = (i > 0) | (lax.broadcasted_iota(jnp.int32, (TM, 1), 0) < lc)
        du_s5 = jnp.where(valid, duf_ref[...] + dub_ref[...], 0.0)
        du = jnp.concatenate([dup_ref[...] * lat, du_s5], axis=1).astype(BF16)
        dh = _dot_nt(du, w_ref[...])
        acc[...] += _dot_tn(h_ref[...], du)
        scale = mod_ref[0, 4:5, :]
        gg = g_ref[...]
        rstd, xhat, xn, _ = _norm_fwd(x_ref[...], gg, 0.0, scale)
        dxn, dshift, dscale, dg = _norm_bwd(dh, rstd, xhat, xn, gg, scale)
        dx_ref[...] = dx2_ref[...] * lat + dxn
        dmod_ref[0, 0:1, :] += dshift
        dmod_ref[0, 1:2, :] += dscale
        dg_ref[...] += dg

        @pl.when(i == nt - 1)
        def _():
            dw_ref[...] = acc[...].astype(BF16)

    tile = _tile_of(nt, True)
    row = lambda i: (tile(i), 0)
    lrow = lambda i: (jnp.maximum(i - 1, 0), 0)
    return _pcall(
        body, name="mix_in_bwd", grid=(nt,),
        out_shape=(jax.ShapeDtypeStruct((t, d), F32), jax.ShapeDtypeStruct((d, m), BF16),
                   jax.ShapeDtypeStruct((2, 2, d), F32), jax.ShapeDtypeStruct((1, d), F32)),
        in_specs=[pl.BlockSpec((TM, hm), lrow), pl.BlockSpec((TM, hm), row), pl.BlockSpec((TM, hm), row),
                  pl.BlockSpec((TM, d), lrow), pl.BlockSpec((TM, d), row), pl.BlockSpec((TM, d), row),
                  _mod_spec(d, True), _const((1, d)), _resident(w_in.shape)],
        out_specs=(pl.BlockSpec((TM, d), row), _const((d, m)), _dmod_spec(2, d, True), _const((1, d))),
        scratch_shapes=[pltpu.VMEM((d, m), F32)],
        compiler_params=_params(("arbitrary",)),
    )(du_pool, du_f, du_b, dx2, h1, x1, modtab, g, w_in)


def _pool(v, rows, transpose, name):
    l, n = rows * GRID_W, GRID_W * LANE
    ngrp = v.shape[1] // LANE
    nchunk = 4
    cw = n // nchunk

    def rowsum(val, lo, hi):
        ri = lax.broadcasted_iota(jnp.int32, (rows, rows), 0)
        ci = lax.broadcasted_iota(jnp.int32, (rows, rows), 1)
        sel = (((ci - ri) >= -lo) & ((ci - ri) <= hi)).astype(BF16)
        v1 = val.astype(BF16)
        r1 = val - v1.astype(F32)
        v2 = r1.astype(BF16)
        v3 = (r1 - v2.astype(F32)).astype(BF16)
        return _dot(sel, v1) + _dot(sel, v2) + _dot(sel, v3)

    def one(v_ref, o_ref, g_scr, r_scr, a_scr, win):
        for c in range(GRID_W):
            g_scr[:, c * LANE:(c + 1) * LANE] = v_ref[pl.ds(c, rows, stride=GRID_W), :]
        lo = win // 2
        hi = win - 1 - lo
        rlo, rhi = (hi, lo) if transpose else (lo, hi)
        ridx = lax.broadcasted_iota(jnp.int32, (rows, 1), 0)
        cnt_r = (jnp.minimum(ridx + hi + 1, rows) - jnp.maximum(ridx - lo, 0)).astype(F32)
        cidx = lax.broadcasted_iota(jnp.int32, (1, n), 1) // LANE
        cnt_c = (jnp.minimum(cidx + hi + 1, GRID_W) - jnp.maximum(cidx - lo, 0)).astype(F32)
        if not transpose:
            for k in range(nchunk):
                sl = slice(k * cw, (k + 1) * cw)
                r_scr[:, sl] = rowsum(g_scr[:, sl], rlo, rhi) / cnt_r
        else:
            r_scr[...] = g_scr[...] / cnt_c
        a_scr[...] = r_scr[...]
        for j in range(-rlo, rhi + 1):
            if j == 0:
                continue
            c0, c1 = max(0, -j), min(GRID_W, GRID_W - j)
            a_scr[:, c0 * LANE:c1 * LANE] += r_scr[:, (c0 + j) * LANE:(c1 + j) * LANE]
        if not transpose:
            r_scr[...] = a_scr[...] / cnt_c - g_scr[...]
        else:
            for k in range(nchunk):
                sl = slice(k * cw, (k + 1) * cw)
                r_scr[:, sl] = rowsum(a_scr[:, sl] / cnt_r, rlo, rhi) - g_scr[:, sl]
        for c in range(GRID_W):
            o_ref[pl.ds(c, rows, stride=GRID_W), :] = r_scr[:, c * LANE:(c + 1) * LANE]

    def body(v_ref, o_ref, g_scr, r_scr, a_scr):
        grp = pl.program_id(0)
        for k, win in enumerate(POOL_WINDOWS):
            @pl.when(grp == k)
            def _(win=win):
                one(v_ref, o_ref, g_scr, r_scr, a_scr, win)

    spec = pl.BlockSpec((l, LANE), lambda k: (0, k))
    return _pcall(
        body, name=name, grid=(ngrp,), out_shape=jax.ShapeDtypeStruct((l, ngrp * LANE), F32),
        in_specs=[spec], out_specs=spec,
        scratch_shapes=[pltpu.VMEM((rows, n), F32), pltpu.VMEM((rows, n), F32), pltpu.VMEM((rows, n), F32)],
        compiler_params=_params(("arbitrary",)),
    )(v)


def _cmul(ar, ai, br, bi):
    return ar * br - ai * bi, ar * bi + ai * br


def _s5_prep(a_re, a_im, log_dt, c_re, c_im, kvec):
    q, nc, p = c_re.shape

    def body(ar_ref, ai_ref, ldt_ref, cr_ref, ci_ref, k_ref, cpr_ref, cpi_ref, tr_ref, ti_ref):
        lr, li = ar_ref[...], ai_ref[...]
        dt = jnp.exp(ldt_ref[...])
        kk = k_ref[...]
        mag = jnp.exp(kk * (lr * dt))
        ph = kk * (li * dt)
        tr_ref[...] = mag * jnp.cos(ph)
        ti_ref[...] = mag * jnp.sin(ph)
        m1 = jnp.exp(lr * dt)
        abr, abi = m1 * jnp.cos(li * dt), m1 * jnp.sin(li * dt)
        den = lr * lr + li * li
        xr, xi = abr - 1.0, abi
        cfr, cfi = (xr * lr + xi * li) / den, (xi * lr - xr * li) / den
        pr, pi_ = _cmul(cr_ref[...], ci_ref[...], cfr, cfi)
        cpr_ref[...] = pr
        cpi_ref[...] = pi_

    return _pcall(
        body, name="s5_prep",
        out_shape=(jax.ShapeDtypeStruct((q, nc, p), F32), jax.ShapeDtypeStruct((q, nc, p), F32),
                   jax.ShapeDtypeStruct((q, NTAB, p), F32), jax.ShapeDtypeStruct((q, NTAB, p), F32)),
        compiler_params=_params(None),
    )(a_re, a_im, log_dt, c_re, c_im, kvec)


def _s5_param_bwd(a_re, a_im, log_dt, c_re, c_im, dab_r, dab_i, dcp_r, dcp_i):
    q, nc, p = c_re.shape

    def body(ar_ref, ai_ref, ldt_ref, cr_ref, ci_ref, dar_ref, dai_ref, dcr_ref, dci_ref,
             gar_ref, gai_ref, gdt_ref, gcr_ref, gci_ref):
        lr, li = ar_ref[...], ai_ref[...]
        dt = jnp.exp(ldt_ref[...])
        m1 = jnp.exp(lr * dt)
        abr, abi = m1 * jnp.cos(li * dt), m1 * jnp.sin(li * dt)
        den = lr * lr + li * li
        xr, xi = abr - 1.0, abi
        cfr, cfi = (xr * lr + xi * li) / den, (xi * lr - xr * li) / den
        cr, ci = cr_ref[...], ci_ref[...]
        dcr, dci = dcr_ref[...], dci_ref[...]
        gcr, gci = _cmul(dcr, dci, cfr, -cfi)
        gcr_ref[...] = gcr
        gci_ref[...] = gci
        t_r, t_i = _cmul(cr, -ci, dcr, dci)
        dcf_r = jnp.sum(t_r, axis=1, keepdims=True)
        dcf_i = jnp.sum(t_i, axis=1, keepdims=True)
        ilr, ili = lr / den, -li / den
        u_r, u_i = _cmul(dcf_r, dcf_i, ilr, -ili)
        dab_r_, dab_i_ = dar_ref[...] + u_r, dai_ref[...] + u_i
        v_r, v_i = _cmul(dab_r_, dab_i_, abr, -abi)
        cl_r, cl_i = _cmul(cfr, cfi, ilr, ili)
        w_r, w_i = _cmul(dcf_r, dcf_i, cl_r, -cl_i)
        gar_ref[...] = v_r * dt - w_r
        gai_ref[...] = v_i * dt - w_i
        la_r, la_i = _cmul(lr, li, abr, abi)
        ddt = la_r * dab_r_ + la_i * dab_i_
        gdt_ref[...] = dt * jnp.sum(ddt, axis=2, keepdims=True)

    return _pcall(
        body, name="s5_param_bwd",
        out_shape=(jax.ShapeDtypeStruct((q, 1, p), F32), jax.ShapeDtypeStruct((q, 1, p), F32),
                   jax.ShapeDtypeStruct((q, 1, p), F32), jax.ShapeDtypeStruct((q, nc, p), F32),
                   jax.ShapeDtypeStruct((q, nc, p), F32)),
        compiler_params=_params(None),
    )(a_re, a_im, log_dt, c_re, c_im, dab_r, dab_i, dcp_r, dcp_i)


def _bcast8(row, c):
    return jnp.broadcast_to(row, (8, c))


def _slab(j):
    return j * 8 if isinstance(j, int) else pl.multiple_of(j * 8, 8)


def _scan_head(xr, xi, tre_ref, tim_ref, car_r, car_i, desc, conj, cs):
    sg = -1.0 if conj else 1.0
    ar = _bcast8(tre_ref[0:1, :], cs)
    ai = sg * _bcast8(tim_ref[0:1, :], cs)

    def p1(jj, carry):
        hr, hi = carry
        off = _slab(SEG - 1 - jj if desc else jj)
        pr, pi_ = _cmul(ar, ai, hr, hi)
        nr = pr + xr[pl.ds(off, 8), :]
        ni = pi_ + xi[pl.ds(off, 8), :]
        xr[pl.ds(off, 8), :] = nr
        xi[pl.ds(off, 8), :] = ni
        return nr, ni

    zero = jnp.zeros((8, cs), F32)
    ir, ii = lax.fori_loop(0, SEG, p1, (zero, zero), unroll=S5_UNROLL)
    cin_r, cin_i = car_r[...], car_i[...]
    rows = lax.broadcasted_iota(jnp.int32, (8, cs), 0)
    for s, krow in ((1, SEG), (2, SEG + 1), (4, SEG + 3)):
        keep, sh = (rows < 8 - s, 8 - s) if desc else (rows >= s, s)
        sr = jnp.where(keep, pltpu.roll(ir, sh, 0), 0.0)
        si = jnp.where(keep, pltpu.roll(ii, sh, 0), 0.0)
        pr, pi_ = _cmul(tre_ref[krow:krow + 1, :], sg * tim_ref[krow:krow + 1, :], sr, si)
        ir, ii = ir + pr, ii + pi_
    q0 = SEG + 8 if desc else SEG
    pr, pi_ = _cmul(tre_ref[q0:q0 + 8, :], sg * tim_ref[q0:q0 + 8, :], cin_r, cin_i)
    fr, fi = ir + pr, ii + pi_
    keep, sh, edge = (rows < 7, 7, 0) if desc else (rows >= 1, 1, 7)
    cs_r = jnp.where(keep, pltpu.roll(fr, sh, 0), cin_r)
    cs_i = jnp.where(keep, pltpu.roll(fi, sh, 0), cin_i)
    car_r[...] = _bcast8(fr[edge:edge + 1, :], cs)
    car_i[...] = _bcast8(fi[edge:edge + 1, :], cs)
    return cs_r, cs_i, cin_r, cin_i


def _pow_row(tre_ref, tim_ref, j, desc, conj, cs):
    k = SEG - 1 - j if desc else j
    sg = -1.0 if conj else 1.0
    return _bcast8(tre_ref[pl.ds(k, 1), :], cs), sg * _bcast8(tim_ref[pl.ds(k, 1), :], cs)


def _to_segments(val, dst):
    for r in range(8):
        dst[pl.ds(r, SEG, stride=8), :] = val[r * SEG:(r + 1) * SEG, :]


def _from_segments(src):
    return jnp.concatenate([src[pl.ds(r, SEG, stride=8), :] for r in range(8)], axis=0)


def _s5_fwd(up, bre, bim, cmr, cmi, tre, tim, dskip, order, nproc, desc, with_skip, name):
    t, cu = up.shape
    nsb = bre.shape[0]
    cb = cu // nsb
    cs = bre.shape[2]
    nch = t // TC

    def body(u_ref, bre_ref, bim_ref, cmr_ref, cmi_ref, tre_ref, tim_ref, dsk_ref,
             y_ref, hr_ref, hi_ref, cinr_ref, cini_ref, car_r, car_i, seg_scr):
        i = pl.program_id(0)

        @pl.when(i == 0)
        def _():
            car_r[...] = jnp.zeros_like(car_r)
            car_i[...] = jnp.zeros_like(car_i)

        for sb in range(nsb):
            col, ch = pl.ds(sb * cs, cs), slice(sb * cb, (sb + 1) * cb)
            hr_v, hi_v = hr_ref.at[:, col], hi_ref.at[:, col]
            tre_v, tim_v = tre_ref.at[:, col], tim_ref.at[:, col]
            useg, yseg = seg_scr.at[0, sb], seg_scr.at[1, sb]
            _to_segments(u_ref[:, ch], useg)
            u = useg[...]
            ub = u.astype(BF16)
            hr_v[...] = _dot(ub, bre_ref[sb])
            hi_v[...] = _dot(ub, bim_ref[sb])
            cs_r, cs_i, cin_r, cin_i = _scan_head(hr_v, hi_v, tre_v, tim_v, car_r.at[sb], car_i.at[sb], desc, False, cs)
            cinr_ref[:, col] = cin_r
            cini_ref[:, col] = cin_i

            def p2(j, _, hr_v=hr_v, hi_v=hi_v, tre_v=tre_v, tim_v=tim_v, cs_r=cs_r, cs_i=cs_i):
                off = _slab(j)
                pw_r, pw_i = _pow_row(tre_v, tim_v, j, desc, False, cs)
                pr, pi_ = _cmul(pw_r, pw_i, cs_r, cs_i)
                hr_v[pl.ds(off, 8), :] = hr_v[pl.ds(off, 8), :] + pr
                hi_v[pl.ds(off, 8), :] = hi_v[pl.ds(off, 8), :] + pi_
                return 0

            lax.fori_loop(0, SEG, p2, 0, unroll=S5_UNROLL)
            y = _dot(hr_v[...].astype(BF16), cmr_ref[sb]) - _dot(hi_v[...].astype(BF16), cmi_ref[sb])
            if with_skip:
                y = y + dsk_ref[:, ch] * u
            yseg[...] = y
            y_ref[:, ch] = _from_segments(yseg)

    blk = lambda i: (order(i), 0)
    return _pcall(
        body, name=name, grid=(nproc,),
        out_shape=(jax.ShapeDtypeStruct((t, cu), F32), jax.ShapeDtypeStruct((t, nsb * cs), F32),
                   jax.ShapeDtypeStruct((t, nsb * cs), F32), jax.ShapeDtypeStruct((nch * 8, nsb * cs), F32),
                   jax.ShapeDtypeStruct((nch * 8, nsb * cs), F32)),
        in_specs=[pl.BlockSpec((TC, cu), blk), _const(bre.shape), _const(bim.shape), _const(cmr.shape),
                  _const(cmi.shape), _const(tre.shape), _const(tim.shape), _const(dskip.shape)],
        out_specs=(pl.BlockSpec((TC, cu), blk), pl.BlockSpec((TC, nsb * cs), blk), pl.BlockSpec((TC, nsb * cs), blk),
                   pl.BlockSpec((8, nsb * cs), blk), pl.BlockSpec((8, nsb * cs), blk)),
        scratch_shapes=[pltpu.VMEM((nsb, 8, cs), F32), pltpu.VMEM((nsb, 8, cs), F32),
                        pltpu.VMEM((2, nsb, TC, cb), F32)],
        compiler_params=_params(("arbitrary",)),
    )(up, bre, bim, cmr, cmi, tre, tim, dskip)


def _s5_bwd(dy_lat, up, hr, hi, cinr, cini, bre, bim, ctr, cti, tre, tim, dskip, order, nproc, desc, with_skip, name,
            xchg=None):
    t, cu = up.shape
    nsb = bre.shape[0]
    cb = cu // nsb
    cs = bre.shape[2]
    lch = dy_lat.shape[0] // TC
    adesc = not desc

    def body(dy_ref, u_ref, hr_ref, hi_ref, cinr_ref, cini_ref, bre_ref, bim_ref, ctr_ref, cti_ref, tre_ref, tim_ref,
             dsk_ref, du_ref, dar_ref, dai_ref, dcr_ref, dci_ref, dbr_ref, dbi_ref, dd_ref,
             car_r, car_i, mr_all, mi_all, acc_r, acc_i, seg_scr):
        i = pl.program_id(0)
        latent = (order(i) < lch).astype(F32)

        @pl.when(i == 0)
        def _():
            for ref in (car_r, car_i, acc_r, acc_i, dcr_ref, dci_ref, dbr_ref, dbi_ref, dd_ref):
                ref[...] = jnp.zeros_like(ref)

        rows = lax.broadcasted_iota(jnp.int32, (8, cs), 0)
        for sb in range(nsb):
            col, ch = pl.ds(sb * cs, cs), slice(sb * cb, (sb + 1) * cb)
            hr_v, hi_v = hr_ref.at[:, col], hi_ref.at[:, col]
            tre_v, tim_v = tre_ref.at[:, col], tim_ref.at[:, col]
            mr, mi = mr_all.at[sb % 2], mi_all.at[sb % 2]
            useg, dyseg, duseg = seg_scr.at[0, sb], seg_scr.at[1, sb], seg_scr.at[2, sb]
            _to_segments(dy_ref[:, ch] * latent, dyseg)
            _to_segments(u_ref[:, ch], useg)
            dy = dyseg[...]
            dyb = dy.astype(BF16)
            u = useg[...]
            ub = u.astype(BF16)
            mr[...] = _dot(dyb, ctr_ref[sb])
            mi[...] = -_dot(dyb, cti_ref[sb])
            cs_r, cs_i, _, _ = _scan_head(mr, mi, tre_v, tim_v, car_r.at[sb], car_i.at[sb], adesc, True, cs)

            def fix(j, hp_r, hp_i, acc, mr=mr, mi=mi, tre_v=tre_v, tim_v=tim_v, cs_r=cs_r, cs_i=cs_i):
                off = _slab(j)
                pw_r, pw_i = _pow_row(tre_v, tim_v, j, adesc, True, cs)
                pr, pi_ = _cmul(pw_r, pw_i, cs_r, cs_i)
                m_r = mr[pl.ds(off, 8), :] + pr
                m_i = mi[pl.ds(off, 8), :] + pi_
                mr[pl.ds(off, 8), :] = m_r
                mi[pl.ds(off, 8), :] = m_i
                a_r, a_i = acc
                return a_r + hp_r * m_r + hp_i * m_i, a_i + hp_r * m_i - hp_i * m_r

            edge_j, src, keep, sh = (SEG - 1, 0, rows < 7, 7) if desc else (0, (SEG - 1) * 8, rows >= 1, 1)
            h0r = jnp.where(keep, pltpu.roll(hr_v[src:src + 8, :], sh, 0), cinr_ref[:, col])
            h0i = jnp.where(keep, pltpu.roll(hi_v[src:src + 8, :], sh, 0), cini_ref[:, col])
            acc = fix(edge_j, h0r, h0i, (acc_r[sb], acc_i[sb]))

            def p2(jj, acc, fix=fix, hr_v=hr_v, hi_v=hi_v):
                j = jj if desc else jj + 1
                offp = _slab(j + 1 if desc else j - 1)
                return fix(j, hr_v[pl.ds(offp, 8), :], hi_v[pl.ds(offp, 8), :], acc)

            a_r, a_i = lax.fori_loop(0, SEG - 1, p2, acc, unroll=S5_UNROLL)
            acc_r[sb] = a_r
            acc_i[sb] = a_i

            mrb, mib = mr[...].astype(BF16), mi[...].astype(BF16)
            du = _dot_nt(mrb, bre_ref[sb]) + _dot_nt(mib, bim_ref[sb])
            if with_skip:
                du = du + dsk_ref[:, ch] * dy
                dd_ref[:, ch] += _colsum(dy * u)
            duseg[...] = du
            du_ref[:, ch] = _from_segments(duseg)
            dbr_ref[sb] += _dot_tn(ub, mrb)
            dbi_ref[sb] += _dot_tn(ub, mib)
            dcr_ref[sb] += _dot_tn(dyb, hr_v[...].astype(BF16))
            dci_ref[sb] -= _dot_tn(dyb, hi_v[...].astype(BF16))

            @pl.when(i == nproc - 1)
            def _(sb=sb, a_r=a_r, a_i=a_i):
                dar_ref[sb] = _colsum(a_r)
                dai_ref[sb] = _colsum(a_i)

    blk = lambda i: (order(i), 0)
    blk_dy = lambda i: (jnp.minimum(order(i), lch - 1), 0)
    mat = (nsb, cb, cs)
    return _pcall(
        body, name=name, grid=(nproc,), xchg=xchg, edges=_edges1(nproc),
        out_shape=(jax.ShapeDtypeStruct((t, cu), F32),
                   jax.ShapeDtypeStruct((nsb, 1, cs), F32), jax.ShapeDtypeStruct((nsb, 1, cs), F32),
                   jax.ShapeDtypeStruct(mat, F32), jax.ShapeDtypeStruct(mat, F32),
                   jax.ShapeDtypeStruct(mat, F32), jax.ShapeDtypeStruct(mat, F32),
                   jax.ShapeDtypeStruct((1, cu), F32)),
        in_specs=[pl.BlockSpec((TC, cu), blk_dy), pl.BlockSpec((TC, cu), blk), pl.BlockSpec((TC, nsb * cs), blk),
                  pl.BlockSpec((TC, nsb * cs), blk), pl.BlockSpec((8, nsb * cs), blk), pl.BlockSpec((8, nsb * cs), blk),
                  _const(mat), _const(mat), _const(mat), _const(mat), _const(tre.shape), _const(tim.shape),
                  _const((1, cu))],
        out_specs=(pl.BlockSpec((TC, cu), blk), _const((nsb, 1, cs)), _const((nsb, 1, cs)),
                   _const(mat), _const(mat), _const(mat), _const(mat), _const((1, cu))),
        scratch_shapes=[pltpu.VMEM((nsb, 8, cs), F32), pltpu.VMEM((nsb, 8, cs), F32), pltpu.VMEM((2, TC, cs), F32),
                        pltpu.VMEM((2, TC, cs), F32), pltpu.VMEM((nsb, 8, cs), F32), pltpu.VMEM((nsb, 8, cs), F32),
                        pltpu.VMEM((3, nsb, TC, cb), F32)],
        compiler_params=_params(("arbitrary",)),
    )(dy_lat, up, hr, hi, cinr, cini, bre, bim, ctr, cti, tre, tim, dskip)


def _gelu(x):
    k = math.sqrt(2.0 / math.pi)
    return 0.5 * x * (1.0 + jnp.tanh(k * (x + 0.044715 * (x * x * x))))


def _gelu_grad(x):
    k = math.sqrt(2.0 / math.pi)
    th = jnp.tanh(k * (x + 0.044715 * (x * x * x)))
    return 0.5 * (1.0 + th) + 0.5 * x * (1.0 - th * th) * (k * (1.0 + 3.0 * 0.044715 * (x * x)))


def _mix_out_fwd(yf, yb, dpool, x1, modtab, w_pool, pscale, w_glu, w_out):
    l, hm = dpool.shape
    d = x1.shape[1]

    def body(yf_ref, yb_ref, dp_ref, x_ref, mod_ref, wp_ref, ps_ref, wg_ref, wo_ref, x2_ref, yp_ref, cat_ref, mix_ref):
        ypre = yf_ref[...] + yb_ref[...]
        yp_ref[...] = ypre
        yg = _gelu(ypre)
        y2 = yg * _sigmoid(_dot(yg.astype(BF16), wg_ref[...]))
        po = _dot(dp_ref[...].astype(BF16), wp_ref[...]) * ps_ref[...]
        cat = jnp.concatenate([po, y2], axis=1).astype(BF16)
        cat_ref[...] = cat
        mix = _dot(cat, wo_ref[...])
        mix_ref[...] = mix.astype(BF16)
        x2_ref[...] = x_ref[...] + mod_ref[0, 5:6, :] * mix

    row = lambda i: (i, 0)
    lrow = row
    return _pcall(
        body, name="mix_out_fwd", grid=(l // TM,),
        out_shape=(jax.ShapeDtypeStruct((l, d), F32), jax.ShapeDtypeStruct((l, hm), F32),
                   jax.ShapeDtypeStruct((l, d), BF16), jax.ShapeDtypeStruct((l, d), BF16)),
        in_specs=[pl.BlockSpec((TM, hm), lrow), pl.BlockSpec((TM, hm), lrow), pl.BlockSpec((TM, hm), row),
                  pl.BlockSpec((TM, d), lrow), _mod_spec(d, False), _const(w_pool.shape), _const((1, hm)),
                  _const(w_glu.shape), _resident(w_out.shape)],
        out_specs=(pl.BlockSpec((TM, d), row), pl.BlockSpec((TM, hm), row), pl.BlockSpec((TM, d), row),
                   pl.BlockSpec((TM, d), row)),
        compiler_params=_params(("arbitrary",)),
    )(yf, yb, dpool, x1, modtab, w_pool, pscale, w_glu, w_out)


def _mix_out_bwd(dx2, mix, cat, ypre, dpool, modtab, w_pool, pscale, w_glu, w_out):
    l, hm = dpool.shape
    d = dx2.shape[1]
    nt = l // TML

    def body(dx_ref, mix_ref, cat_ref, yp_ref, dp_ref, mod_ref, wp_ref, ps_ref, wg_ref, wo_ref,
             dyp_ref, ddp_ref, dwo_ref, dwg_ref, dwp_ref, dps_ref, dgate_ref, acc_o, acc_g):
        i = pl.program_id(0)

        @pl.when(i == 0)
        def _():
            for ref in (acc_o, acc_g, dwp_ref, dps_ref, dgate_ref):
                ref[...] = jnp.zeros_like(ref)

        dx = dx_ref[...]
        dgate_ref[...] += _colsum(dx * mix_ref[...].astype(F32))
        dmix = (dx * mod_ref[0, 5:6, :]).astype(BF16)
        dcat = _dot_nt(dmix, wo_ref[...])
        acc_o[...] += _dot_tn(cat_ref[...], dmix)
        dpo, dy2 = dcat[:, :hm], dcat[:, hm:]
        dpb = dp_ref[...].astype(BF16)
        pp = _dot(dpb, wp_ref[...])
        dps_ref[...] += _colsum(dpo * pp)
        dpp = (dpo * ps_ref[...]).astype(BF16)
        ddp_ref[...] = _dot_nt(dpp, wp_ref[...])
        dwp_ref[...] += _dot_tn(dpb, dpp)
        ypre = yp_ref[...]
        yg = _gelu(ypre)
        ygb = yg.astype(BF16)
        s = _sigmoid(_dot(ygb, wg_ref[...]))
        dq = (dy2 * yg * s * (1.0 - s)).astype(BF16)
        dyg = dy2 * s + _dot_nt(dq, wg_ref[...])
        acc_g[...] += _dot_tn(ygb, dq)
        dyp_ref[...] = dyg * _gelu_grad(ypre)

        @pl.when(i == nt - 1)
        def _():
            dwo_ref[...] = acc_o[...].astype(BF16)
            dwg_ref[...] = acc_g[...].astype(BF16)

    row = lambda i: (i, 0)
    return _pcall(
        body, name="mix_out_bwd", grid=(nt,),
        out_shape=(jax.ShapeDtypeStruct((l, hm), F32), jax.ShapeDtypeStruct((l, hm), F32),
                   jax.ShapeDtypeStruct((d, d), BF16), jax.ShapeDtypeStruct((hm, hm), BF16),
                   jax.ShapeDtypeStruct((hm, hm), F32), jax.ShapeDtypeStruct((1, hm), F32),
                   jax.ShapeDtypeStruct((1, d), F32)),
        in_specs=[pl.BlockSpec((TML, d), row), pl.BlockSpec((TML, d), row), pl.BlockSpec((TML, d), row),
                  pl.BlockSpec((TML, hm), row), pl.BlockSpec((TML, hm), row), _mod_spec(d, False),
                  _const(w_pool.shape), _const((1, hm)), _const(w_glu.shape), _resident(w_out.shape)],
        out_specs=(pl.BlockSpec((TML, hm), row), pl.BlockSpec((TML, hm), row), _const((d, d)), _const((hm, hm)),
                   _const((hm, hm)), _const((1, hm)), _const((1, d))),
        scratch_shapes=[pltpu.VMEM((d, d), F32), pltpu.VMEM((hm, hm), F32)],
        compiler_params=_params(("arbitrary",)),
    )(dx2, mix, cat, ypre, dpool, modtab, w_pool, pscale, w_glu, w_out)


def _loss_head(xx, target, gg, loss_ref, dg_ref):
    d = xx.shape[-1]
    rstd = lax.rsqrt(jnp.mean(xx * xx, axis=-1, keepdims=True) + EPS)
    xhat = xx * rstd
    err = xhat * gg - target
    row_loss = jnp.mean(err * err, axis=-1, keepdims=True)
    loss_ref[...] += 0.5 * jnp.sum(row_loss, axis=0, keepdims=True)
    dout = err * (1.0 / d)
    dg_ref[...] += _colsum(dout * xhat)
    dxhat = dout * gg
    return rstd * (dxhat - xhat * jnp.mean(dxhat * xhat, axis=-1, keepdims=True))


def _adamw(parts, w, m, v, name):
    shape = w.shape
    c = shape[-1]
    r = int(np.prod(shape)) // c
    npart = parts.shape[0]
    tr = r
    for cand in (1024, 512, 256, 128, 64, 32, 16):
        if r % cand == 0 and cand * max(c, LANE) * 4 <= ADAM_BLOCK_BYTES:
            tr = cand
            break
    c1 = 1.0 - ADAM_B1
    c2 = 1.0 - ADAM_B2
    bc1 = 1.0 - ADAM_B1 ** ADAM_STEP
    bc2 = 1.0 - ADAM_B2 ** ADAM_STEP

    def body(p_ref, w_ref, m_ref, v_ref, g_ref, d_ref, mo_ref, vo_ref):
        g = p_ref[0].astype(F32)
        for k in range(1, npart):
            g = g + p_ref[k].astype(F32)
        mn = ADAM_B1 * m_ref[...] + c1 * g
        vn = ADAM_B2 * v_ref[...] + c2 * (g * g)
        m_hat = mn / bc1
        v_hat = vn / bc2
        g_ref[...] = g
        mo_ref[...] = mn
        vo_ref[...] = vn
        d_ref[...] = -ADAM_LR * (m_hat / (jnp.sqrt(v_hat) + ADAM_EPS) + ADAM_WD * w_ref[...])

    spec = pl.BlockSpec((tr, c), lambda i: (i, 0))
    outs = _pcall(
        body, name=name, grid=(r // tr,),
        out_shape=tuple(jax.ShapeDtypeStruct((r, c), F32) for _ in range(4)),
        in_specs=[pl.BlockSpec((npart, tr, c), lambda i: (0, i, 0)), spec, spec, spec],
        out_specs=(spec, spec, spec, spec),
        compiler_params=_params(("arbitrary",)),
    )(parts.reshape(npart, r, c), w.reshape(r, c), m.reshape(r, c), v.reshape(r, c))
    return tuple(o.reshape(shape) for o in outs)


def _adamw_ffn(parts, row0, w, m, v, name):
    _, nl, r, c = w.shape
    rp = parts[0][0].shape[1]
    tc = ADAM_FFN_LANES
    grid = (c // tc,)
    p_spec = lambda a: pl.BlockSpec((a.shape[0], rp, tc), lambda i: (0, 0, i))
    w_spec = pl.BlockSpec((1, nl, r, tc), lambda i: (0, 0, 0, i))
    take = lambda ref, k: ref[k, row0:row0 + r, :]
    flat = [a for layer in parts for a in layer]
    counts = [len(layer) for layer in parts]
    c1 = 1.0 - ADAM_B1
    c2 = 1.0 - ADAM_B2
    bc1 = 1.0 - ADAM_B1 ** ADAM_STEP
    bc2 = 1.0 - ADAM_B2 ** ADAM_STEP

    def body(*refs):
        w_ref, m_ref, v_ref, g_ref, d_ref, mo_ref, vo_ref = refs[len(flat):]
        first = 0
        for lyr in range(nl):
            g = None
            for ref in refs[first:first + counts[lyr]]:
                for k in range(ref.shape[0]):
                    term = take(ref, k).astype(F32)
                    g = term if g is None else g + term
            first += counts[lyr]
            mn = ADAM_B1 * m_ref[0, lyr] + c1 * g
            vn = ADAM_B2 * v_ref[0, lyr] + c2 * (g * g)
            g_ref[0, lyr] = g
            mo_ref[0, lyr] = mn
            vo_ref[0, lyr] = vn
            d_ref[0, lyr] = -ADAM_LR * ((mn / bc1) / (jnp.sqrt(vn / bc2) + ADAM_EPS) + ADAM_WD * w_ref[0, lyr])

    return _pcall(
        body, name=name, grid=grid, out_shape=tuple(jax.ShapeDtypeStruct(w.shape, F32) for _ in range(4)),
        in_specs=[p_spec(a) for a in flat] + [w_spec] * 3, out_specs=(w_spec,) * 4,
        compiler_params=_params(("arbitrary",)),
    )(*flat, w, m, v)


def _blockdiag(xb):
    n, a, b = xb.shape[-3:]
    eye = jnp.eye(n, dtype=xb.dtype)
    out = xb[..., :, :, None, :] * eye[:, None, :, None]
    return out.reshape(xb.shape[:-3] + (n * a, n * b))


def _diagblocks(mat, n):
    lead = mat.shape[:-2]
    a, b = mat.shape[-2] // n, mat.shape[-1] // n
    m5 = mat.reshape(lead + (n, a, n, b))
    nl = len(lead)
    dg = jnp.diagonal(m5, axis1=nl, axis2=nl + 2)
    return jnp.moveaxis(dg, -1, nl)


def _pad128(a):
    flat = a.reshape(-1)
    pad = (-flat.shape[0]) % LANE
    return jnp.pad(flat, (0, pad)) if pad else flat


def kernel(x, c, ctx, c_ctx, norm_g, w_ada, b_ada, ffn_w1, ffn_w3, ffn_w2, w_in, pool_w, pool_scale, s5_a_re, s5_a_im, s5_log_dt, s5_b_re, s5_b_im, s5_c_re, s5_c_im, s5_d, s5_w_glu, w_out, final_g, loss_target, m_c_ctx, m_norm_g, m_w_ada, m_b_ada, m_ffn_w1, m_ffn_w3, m_ffn_w2, m_w_in, m_pool_w, m_pool_scale, m_s5_a_re, m_s5_a_im, m_s5_log_dt, m_s5_b_re, m_s5_b_im, m_s5_c_re, m_s5_c_im, m_s5_d, m_s5_w_glu, m_w_out, m_final_g, v_c_ctx, v_norm_g, v_w_ada, v_b_ada, v_ffn_w1, v_ffn_w3, v_ffn_w2, v_w_in, v_pool_w, v_pool_scale, v_s5_a_re, v_s5_a_im, v_s5_log_dt, v_s5_b_re, v_s5_b_im, v_s5_c_re, v_s5_c_im, v_s5_d, v_s5_w_glu, v_w_out, v_final_g):
    weights = dict(c_ctx=c_ctx, norm_g=norm_g, w_ada=w_ada, b_ada=b_ada, ffn_w1=ffn_w1, ffn_w3=ffn_w3, ffn_w2=ffn_w2,
                   w_in=w_in, pool_w=pool_w, pool_scale=pool_scale, s5_a_re=s5_a_re, s5_a_im=s5_a_im,
                   s5_log_dt=s5_log_dt, s5_b_re=s5_b_re, s5_b_im=s5_b_im, s5_c_re=s5_c_re, s5_c_im=s5_c_im, s5_d=s5_d,
                   s5_w_glu=s5_w_glu, w_out=w_out, final_g=final_g)
    mom_m = dict(c_ctx=m_c_ctx, norm_g=m_norm_g, w_ada=m_w_ada, b_ada=m_b_ada, ffn_w1=m_ffn_w1, ffn_w3=m_ffn_w3,
                 ffn_w2=m_ffn_w2, w_in=m_w_in, pool_w=m_pool_w, pool_scale=m_pool_scale, s5_a_re=m_s5_a_re,
                 s5_a_im=m_s5_a_im, s5_log_dt=m_s5_log_dt, s5_b_re=m_s5_b_re, s5_b_im=m_s5_b_im, s5_c_re=m_s5_c_re,
                 s5_c_im=m_s5_c_im, s5_d=m_s5_d, s5_w_glu=m_s5_w_glu, w_out=m_w_out, final_g=m_final_g)
    mom_v = dict(c_ctx=v_c_ctx, norm_g=v_norm_g, w_ada=v_w_ada, b_ada=v_b_ada, ffn_w1=v_ffn_w1, ffn_w3=v_ffn_w3,
                 ffn_w2=v_ffn_w2, w_in=v_w_in, pool_w=v_pool_w, pool_scale=v_pool_scale, s5_a_re=v_s5_a_re,
                 s5_a_im=v_s5_a_im, s5_log_dt=v_s5_log_dt, s5_b_re=v_s5_b_re, s5_b_im=v_s5_b_im, s5_c_re=v_s5_c_re,
                 s5_c_im=v_s5_c_im, s5_d=v_s5_d, s5_w_glu=v_s5_w_glu, w_out=v_w_out, final_g=v_final_g)

    l, d = x.shape[1], x.shape[2]
    lc = ctx.shape[1]
    t = l + TM
    rows = l // GRID_W
    fblk = ffn_w1.shape[-1]
    ngrp, gp = s5_a_re.shape[2], s5_a_re.shape[3]
    gc = s5_b_re.shape[3]
    hm = ngrp * gc
    nsb = 4
    gsb = ngrp // nsb
    assert lc == TC and TM % TC == 0 and l % TM == 0 and TM == TML and hm == 4 * LANE and ngrp * gp == nsb * 512
    me = 4 * lax.axis_index("x") + 2 * lax.axis_index("y") + lax.axis_index("c")

    padr = ((0, FPAD - fblk), (0, 0))
    w1_t, w3_t = jnp.swapaxes(ffn_w1, 2, 3), jnp.swapaxes(ffn_w3, 2, 3)
    w13_loc = [jnp.concatenate([jnp.pad(w1_t[0, k], padr), jnp.pad(w3_t[0, k], padr)], axis=0).astype(BF16)
               for k in range(2)]
    w2_loc = [jnp.pad(ffn_w2[0, k], padr).astype(BF16) for k in range(2)]
    wa = w_ada.shape[-1]
    b_blk = lax.dynamic_slice_in_dim(b_ada, me * wa, wa, axis=1)
    w13_0, w2_0, g_all, c_all, mod_all = _front(w13_loc[0], w2_loc[0], norm_g[0], c, c_ctx.reshape(1, d), w_ada[0],
                                                b_blk)
    w2_0 = w2_0.reshape(NDEV // 2, 2 * FPAD, d)
    c_all = c_all.reshape(NDEV, d)
    g_all = g_all.transpose(1, 0, 2).reshape(3, d)

    c16 = jnp.concatenate([c_all, c_ctx.reshape(1, d), jnp.zeros((7, d), F32)], axis=0)
    mod_full = mod_all.transpose(1, 0, 2).reshape(16, NDEV * wa)
    mod_l = lax.dynamic_index_in_dim(mod_full, me, axis=0, keepdims=False).reshape(9, d)
    modtab = jnp.stack([mod_full[8].reshape(9, d), mod_l])

    x0 = [jnp.pad(ctx[0], ((0, TM - lc), (0, 0))), x[0]]
    g0, g1, g2 = g_all[0:1], g_all[1:2], g_all[2:3]
    x1, h0, ab0, o0, w13_1, w2_1, w_in_f, w_out_f, w_glu_f = _ffn_fwd(
        x0, modtab, g0, w13_0, w2_0, 0, True, "ffn0_fwd",
        xchg=[(w13_loc[1], GATHER_REL), (w2_loc[1], GATHER_REL), (w_in[0].astype(BF16), GATHER),
              (w_out[0].astype(BF16), GATHER), (s5_w_glu[0].astype(BF16), GATHER)])
    w2_1 = w2_1.reshape(NDEV // 2, 2 * FPAD, d)
    w_in_f = w_in_f.reshape(d, -1)
    w_out_f = w_out_f.reshape(-1, d)
    w_glu_f = w_glu_f.reshape(hm, hm)
    h1, u_pool, up = _mix_in_fwd(x1, modtab, g1, w_in_f)

    dpool = _pool(u_pool, rows, False, "pool_fwd")
    w_pool_bd = _blockdiag(pool_w[0]).astype(BF16)

    q = 2 * ngrp
    kvec = np.concatenate([np.arange(1, SEG + 1), SEG * np.arange(1, 9), SEG * np.arange(8, 0, -1)]).astype(np.float32)
    a_re3, a_im3 = s5_a_re[0].reshape(q, 1, gp), s5_a_im[0].reshape(q, 1, gp)
    ldt3 = jnp.broadcast_to(s5_log_dt[0].reshape(q, 1, 1), (q, 1, gp))
    c_re3, c_im3 = s5_c_re[0].reshape(q, gc, gp), s5_c_im[0].reshape(q, gc, gp)
    cp_re, cp_im, tab_re, tab_im = _s5_prep(a_re3, a_im3, ldt3, c_re3, c_im3, jnp.broadcast_to(jnp.asarray(kvec).reshape(1, NTAB, 1), (1, NTAB, gp)))
    tabs = [(tab_re.reshape(2, ngrp, NTAB, gp)[k].transpose(1, 0, 2).reshape(NTAB, ngrp * gp),
             tab_im.reshape(2, ngrp, NTAB, gp)[k].transpose(1, 0, 2).reshape(NTAB, ngrp * gp)) for k in range(2)]
    b_t = lambda b: _blockdiag(b[0].reshape(nsb, gsb, gp, gc).transpose(0, 1, 3, 2)).astype(BF16)
    bre, bim = b_t(s5_b_re), b_t(s5_b_im)
    cp_t = lambda cp: _blockdiag(cp.reshape(2, nsb, gsb, gc, gp)).astype(BF16)
    ctr, cti = cp_t(cp_re), cp_t(cp_im)
    cmr, cmi = jnp.swapaxes(ctr, -1, -2), jnp.swapaxes(cti, -1, -2)
    dskip = s5_d

    lch = l // TC
    nproc = lch + 1
    order_f = lambda i: jnp.where(i == 0, lch, i - 1)
    order_b = lambda i: jnp.where(i == 0, lch, lch - i)
    rorder_f = lambda i: jnp.where(i == lch, lch, lch - 1 - i)
    rorder_b = lambda i: i
    y_f, hr_f, hi_f, cinr_f, cini_f = _s5_fwd(up, bre, bim, cmr[0], cmi[0], tabs[0][0], tabs[0][1], dskip,
                                               order_f, nproc, False, True, "s5_fwd_f")
    y_b, hr_b, hi_b, cinr_b, cini_b = _s5_fwd(up, bre, bim, cmr[1], cmi[1], tabs[1][0], tabs[1][1], dskip,
                                              order_b, nproc, True, False, "s5_fwd_b")

    x2, ypre, cat, mix = _mix_out_fwd(y_f, y_b, dpool, x1, modtab, w_pool_bd, pool_scale, w_glu_f, w_out_f)
    dx3, h2, ab2, o2, loss_part, dfinal_g = _ffn_fwd([x2], modtab, g2, w13_1, w2_1, 6, False, "ffn1_fwd",
                                                     head=(loss_target[0], final_g.reshape(1, d)))

    dhp2, d13_1, d2_1 = _ffn_bwd(dx3, modtab, h2, ab2, w13_1, w2_1, 6, False, "ffn1_bwd")
    dx2, dmod_678, dg2 = _ffn_bwd_norm(dx3, [dhp2], [x2], o2, modtab, g2, 6, False, "ffn1_bwd_norm")
    dypre, ddpool, dw_out, dw_glu, dw_pool_bd, dpscale, dgate5 = _mix_out_bwd(
        dx2, mix, cat, ypre, dpool, modtab, w_pool_bd, pool_scale, w_glu_f, w_out_f)

    du_f, dar_f, dai_f, dcr_f, dci_f, dbr_f, dbi_f, dd_skip, r13_1 = _s5_bwd(
        dypre, up, hr_f, hi_f, cinr_f, cini_f, bre, bim, ctr[0], cti[0], tabs[0][0], tabs[0][1], dskip,
        rorder_f, nproc, False, True, "s5_bwd_f",
        xchg=[(d13_1, A2A_REL)])
    du_b, dar_b, dai_b, dcr_b, dci_b, dbr_b, dbi_b, _, r2_1, r_out, r_glu = _s5_bwd(
        dypre, up, hr_b, hi_b, cinr_b, cini_b, bre, bim, ctr[1], cti[1], tabs[1][0], tabs[1][1], dskip,
        rorder_b, nproc, True, False, "s5_bwd_b",
        xchg=[(d2_1.reshape(NDEV, FPAD, d), A2A_REL), (dw_out.reshape(NDEV, -1, d), A2A),
              (dw_glu.reshape(NDEV, hm // NDEV, hm), A2A)])

    dab_r = jnp.stack([dar_f, dar_b]).reshape(q, 1, gp)
    dab_i = jnp.stack([dai_f, dai_b]).reshape(q, 1, gp)
    dcp_r = _diagblocks(jnp.stack([dcr_f, dcr_b]), gsb).reshape(q, gc, gp)
    dcp_i = _diagblocks(jnp.stack([dci_f, dci_b]), gsb).reshape(q, gc, gp)
    ga_re, ga_im, gldt, gc_re, gc_im = _s5_param_bwd(a_re3, a_im3, ldt3, c_re3, c_im3, dab_r, dab_i, dcp_r, dcp_i)
    gb_re = (_diagblocks(dbr_f, gsb) + _diagblocks(dbr_b, gsb)).transpose(0, 1, 3, 2).reshape(ngrp, gp, gc)
    gb_im = (_diagblocks(dbi_f, gsb) + _diagblocks(dbi_b, gsb)).transpose(0, 1, 3, 2).reshape(ngrp, gp, gc)

    def pack(small):
        offs, pieces, off = {}, [], 0
        for k_, a_ in small.items():
            p_ = _pad128(a_.astype(F32))
            offs[k_] = (off, int(np.prod(a_.shape)))
            off += p_.shape[0]
            pieces.append(p_)
        return jnp.concatenate(pieces).reshape(1, off), offs

    gw_pool = _diagblocks(dw_pool_bd, 4)
    bundle_a, offs = pack(dict(pool_w=gw_pool, pool_scale=dpscale, s5_a_re=ga_re, s5_a_im=ga_im,
                               s5_log_dt=gldt[:, 0, 0], s5_b_re=gb_re, s5_b_im=gb_im, s5_c_re=gc_re, s5_c_im=gc_im,
                               s5_d=dd_skip, final_g=dfinal_g))

    du_pool = _pool(ddpool, rows, True, "pool_bwd")
    dx1, dw_in, dmod_34, dg1 = _mix_in_bwd(du_pool, du_f, du_b, dx2, h1, x1, modtab, g1, w_in_f, lc)
    dhp0a, d13_0a, d2_0a, bund_a, r_in = _ffn_bwd(
        dx1, modtab, h0, ab0, w13_0, w2_0, 0, True, "ffn0_bwd_a", pair0=0, npair=2,
        xchg=[(bundle_a, GATHER), (dw_in.reshape(NDEV, d // NDEV, -1), A2A)])
    dhp0b, d13_0b, d2_0b, r13_0a, r2_0a = _ffn_bwd(
        dx1, modtab, h0, ab0, w13_0, w2_0, 0, True, "ffn0_bwd_b", pair0=2, npair=2,
        xchg=[(d13_0a, A2A_SAME), (d2_0a.reshape(NDEV // 2, FPAD, d), A2A_SAME)])
    r13_0b, r2_0b = _sequencer_a2a_other([d13_0b, d2_0b.reshape(NDEV // 2, FPAD, d)], "seq_a2a_ffn0")
    dx0, dmod_012, dg0 = _ffn_bwd_norm(dx1, [dhp0a, dhp0b], x0, o0, modtab, g0, 0, True, "ffn0_bwd_norm")
    grad_x = dx0.reshape(1, l, d)

    dmod_c = jnp.concatenate([dmod_012[0], dmod_34[0], jnp.zeros((4, d), F32)], axis=0)
    dmod_l = jnp.concatenate([dmod_012[1], dmod_34[1], dgate5, dmod_678[0]], axis=0)
    bundle_b, offs_b = pack(dict(norm_g=jnp.concatenate([dg0, dg1, dg2], axis=0), dmod_l=dmod_l, dmod_c=dmod_c,
                                 loss=loss_part[:, :1]))
    bund_b = lax.dynamic_update_slice(_sequencer_gather(bundle_b, "seq_gather_small"), bundle_b[None],
                                      (me, 0, 0))
    bunds = {**{k_: (bund_a.reshape(NDEV, -1), v_) for k_, v_ in offs.items()},
             **{k_: (bund_b.reshape(NDEV, -1), v_) for k_, v_ in offs_b.items()}}
    r13, r2 = [(r13_0a, r13_0b), (r13_1,)], [(r2_0a, r2_0b), (r2_1,)]

    def piece(name):
        b_, (o_, n_) = bunds[name]
        return b_[:, o_:o_ + n_]

    loss = jnp.sum(piece("loss")[:, 0])

    dl_all = lax.dynamic_slice_in_dim(piece("dmod_l"), me * wa, wa, axis=1)
    dc_all = lax.dynamic_slice_in_dim(piece("dmod_c"), me * wa, wa, axis=1)
    g_wada, gc_part = _ada_bwd(c16, w_ada[0], dl_all, dc_all)
    gc_all = _exchange([(gc_part, GATHER)], "exchange_cctx")[0]

    parts = {
        "c_ctx": gc_all.reshape(NDEV, d),
        "norm_g": lax.dynamic_slice_in_dim(piece("norm_g").reshape(NDEV, 3, d), me * (d // NDEV), d // NDEV,
                                           axis=2).reshape((NDEV,) + norm_g.shape),
        "w_ada": g_wada.reshape((1,) + w_ada.shape),
        "b_ada": jnp.concatenate([piece("dmod_l"), piece("dmod_c")], axis=0).reshape((2 * NDEV,) + b_ada.shape),
        "w_in": r_in.reshape((NDEV,) + w_in.shape),
        "s5_w_glu": r_glu.reshape((NDEV,) + s5_w_glu.shape),
        "w_out": r_out.reshape((NDEV,) + w_out.shape),
    }
    for k_ in ("pool_w", "pool_scale", "s5_a_re", "s5_a_im", "s5_log_dt", "s5_b_re", "s5_b_im", "s5_c_re", "s5_c_im",
               "s5_d", "final_g"):
        parts[k_] = piece(k_).reshape((NDEV,) + weights[k_].shape)

    grads, deltas, new_m, new_v = [], [], [], []
    ffn_parts = {"ffn_w1": (r13, 0, True), "ffn_w3": (r13, FPAD, True), "ffn_w2": (r2, 0, False)}
    for k_ in weights:
        if k_ in ffn_parts:
            shards, row0, transposed = ffn_parts[k_]
            flip = (lambda a: jnp.swapaxes(a, 2, 3)) if transposed else (lambda a: a)
            outs = _adamw_ffn(shards, row0, flip(weights[k_]), flip(mom_m[k_]), flip(mom_v[k_]), "adamw_" + k_)
            g_, d_, m_, v_ = (flip(o_) for o_ in outs)
        else:
            g_, d_, m_, v_ = _adamw(parts[k_], weights[k_], mom_m[k_], mom_v[k_], "adamw_" + k_)
        grads.append(g_)
        deltas.append(d_)
        new_m.append(m_)
        new_v.append(v_)
    return (loss, grad_x, *grads, *deltas, *new_m, *new_v)
```

```python
import functools
import math

import numpy as np
import jax
import jax.numpy as jnp
from jax import lax
from jax.experimental import pallas as pl
from jax.experimental.pallas import tpu as pltpu
from jax.experimental.pallas import tpu_sc as plsc

F32 = jnp.float32
BF16 = jnp.bfloat16
AXES = ("x", "y", "c")
NDEV = 8
EPS = 1e-6
TM = 512
TML = 512
TC = 256
SEG = TC // 8
NTAB = SEG + 16
S5_UNROLL = True
GRID_W = 64
POOL_WINDOWS = (2, 4, 8, 16)
LANE = 128
FPAD = 384
VMEM_LIMIT = 56 * 1024 * 1024
ADAM_BLOCK_BYTES = 1024 * 1024
ADAM_FFN_LANES = 256
SEQ_COLLECTIVE_ID = 0
SEQ_GATHER_COLLECTIVE_ID = 1
SEQ_GATHER2_COLLECTIVE_ID = 2
ADAM_LR, ADAM_B1, ADAM_B2, ADAM_EPS, ADAM_WD, ADAM_STEP = 0.001, 0.9, 0.999, 1e-08, 0.01, 10


def _raw_call(body, kw):
    return pl.pallas_call(body, **kw)


def _pcall(body, xchg=None, edges=None, **kw):
    extra = ()
    if xchg:
        arrs, kinds = [a for a, _ in xchg], [k for _, k in xchg]
        n = len(arrs)
        n_in, n_out, n_scr = len(kw["in_specs"]), len(kw["out_shape"]), len(kw.get("scratch_shapes", ()))
        inner = body

        def hosted(*refs):
            a, b = n_in, n_in + n
            c_, d_ = b + n_out, b + n_out + n
            e = d_ + n_scr
            first, mid, last = edges()

            @pl.when(first)
            def _():
                _xchg_start(refs[a:b], refs[c_:d_], refs[e:], kinds)

            inner(*refs[:a], *refs[b:c_], *refs[d_:e])

            if any(k == GATHER_REL for k in kinds):
                @pl.when(mid)
                def _():
                    _xchg_relay(refs[a:b], refs[c_:d_], refs[e:], kinds)

            @pl.when(last)
            def _():
                _xchg_finish(refs[a:b], refs[c_:d_], refs[e:], kinds)

        body = hosted
        any_spec = pl.BlockSpec(memory_space=pl.ANY)
        kw = dict(kw, in_specs=list(kw["in_specs"]) + [any_spec] * n,
                  out_shape=tuple(kw["out_shape"]) + _xchg_shapes(arrs, kinds),
                  out_specs=tuple(kw["out_specs"]) + (any_spec,) * n,
                  scratch_shapes=list(kw.get("scratch_shapes", ())) + _xchg_sems(n))
        extra = tuple(arrs)
    call = _raw_call(body, kw)
    return (lambda *args: call(*args, *extra)) if extra else call


def _params(sem):
    return pltpu.CompilerParams(dimension_semantics=sem, vmem_limit_bytes=VMEM_LIMIT)


def _dot(a, b):
    return jnp.dot(a, b, preferred_element_type=F32)


def _dot_nt(a, b):
    return lax.dot_general(a, b, (((1,), (1,)), ((), ())), preferred_element_type=F32)


def _dot_tn(a, b):
    return lax.dot_general(a, b, (((0,), (0,)), ((), ())), preferred_element_type=F32)


def _sigmoid(x):
    return 1.0 / (1.0 + jnp.exp(-x))


def _colsum(a):
    return jnp.sum(a, axis=0, keepdims=True)


def _resident(shape):
    nd = len(shape)
    return pl.BlockSpec(shape, lambda *_: (0,) * nd, pipeline_mode=pl.Buffered(1))


def _const(shape):
    nd = len(shape)
    return pl.BlockSpec(shape, lambda *_: (0,) * nd)


GATHER = "gather"
GATHER_REL = "gather_rel"
A2A = "a2a"
A2A_REL = "a2a_rel"
A2A_SAME = "a2a_same"
A2A_OTHER = "a2a_other"


def _exchange(items, name):
    arrs, kinds = [a for a, _ in items], [k for _, k in items]
    n = len(arrs)

    def body(*refs):
        _xchg_start(refs[:n], refs[n:2 * n], refs[2 * n:], kinds)
        _xchg_relay(refs[:n], refs[n:2 * n], refs[2 * n:], kinds)
        _xchg_finish(refs[:n], refs[n:2 * n], refs[2 * n:], kinds)

    any_spec = pl.BlockSpec(memory_space=pl.ANY)
    outs = _pcall(
        body, name=name, out_shape=_xchg_shapes(arrs, kinds), in_specs=[any_spec] * n, out_specs=[any_spec] * n,
        scratch_shapes=_xchg_sems(n),
    )(*arrs)
    return list(outs)


def _sequencer_a2a_other(arrs, name):
    hbm = pltpu.MemorySpace.HBM
    srcs = [jax.new_ref(a, memory_space=hbm) for a in arrs]
    dsts = [jax.empty_ref(jax.ShapeDtypeStruct(a.shape, a.dtype), memory_space=hbm) for a in arrs]
    masks = (1, 5, 3, 7)
    nsem = len(arrs) * len(masks)

    @pl.kernel(mesh=plsc.ScalarSubcoreMesh(axis_name="seq", num_cores=1), name=name,
               scratch_types=(pltpu.SemaphoreType.DMA,) * (2 * nsem),
               compiler_params=pltpu.CompilerParams(collective_id=SEQ_COLLECTIVE_ID))
    def launch(*sems):
        x, y, c = (lax.axis_index(a) for a in AXES)
        my_chip = 2 * x + y
        peers = []
        for p in masks:
            px = 1 - x if p & 4 else x
            py = 1 - y if p & 2 else y
            peers.append(((px, py, 1 - c), 2 * px + py))
        barrier = pltpu.get_barrier_semaphore()
        for dev, _ in peers:
            pl.semaphore_signal(barrier, inc=1, device_id=dev, device_id_type=pl.DeviceIdType.MESH)
        pl.semaphore_wait(barrier, len(peers))
        copies = []
        for k in range(len(arrs)):
            for q, (dev, pchip) in enumerate(peers):
                i = k * len(masks) + q
                send = pltpu.make_async_remote_copy(
                    src_ref=srcs[k].at[pchip], dst_ref=dsts[k].at[my_chip], send_sem=sems[i], recv_sem=sems[nsem + i],
                    device_id=dev, device_id_type=pl.DeviceIdType.MESH)
                recv = pltpu.make_async_remote_copy(
                    src_ref=srcs[k].at[pchip], dst_ref=dsts[k].at[pchip], send_sem=sems[i], recv_sem=sems[nsem + i],
                    device_id=dev, device_id_type=pl.DeviceIdType.MESH)
                send.start()
                copies.append((send, recv))
        for send, recv in copies:
            recv.wait_recv()
            send.wait_send()

    launch()
    return [d[...] for d in dsts]


def _sequencer_gather(block, name, collective_id):
    hbm = pltpu.MemorySpace.HBM
    src = jax.new_ref(block, memory_space=hbm)
    dst = jax.empty_ref(jax.ShapeDtypeStruct((NDEV,) + block.shape, block.dtype), memory_space=hbm)
    npeer = NDEV - 1

    @pl.kernel(mesh=plsc.ScalarSubcoreMesh(axis_name="seq", num_cores=1), name=name,
               scratch_types=(pltpu.SemaphoreType.DMA,) * (2 * npeer),
               compiler_params=pltpu.CompilerParams(collective_id=collective_id))
    def launch(*sems):
        x, y, c = (lax.axis_index(a) for a in AXES)
        my_abs = 4 * x + 2 * y + c
        peers = []
        for p in range(1, NDEV):
            px = 1 - x if p & 4 else x
            py = 1 - y if p & 2 else y
            pc = 1 - c if p & 1 else c
            peers.append(((px, py, pc), 4 * px + 2 * py + pc))
        barrier = pltpu.get_barrier_semaphore()
        for dev, _ in peers:
            pl.semaphore_signal(barrier, inc=1, device_id=dev, device_id_type=pl.DeviceIdType.MESH)
        pl.semaphore_wait(barrier, npeer)
        copies = []
        for i, (dev, pabs) in enumerate(peers):
            send = pltpu.make_async_remote_copy(
                src_ref=src, dst_ref=dst.at[my_abs], send_sem=sems[i], recv_sem=sems[npeer + i],
                device_id=dev, device_id_type=pl.DeviceIdType.MESH)
            recv = pltpu.make_async_remote_copy(
                src_ref=src, dst_ref=dst.at[pabs], send_sem=sems[i], recv_sem=sems[npeer + i],
                device_id=dev, device_id_type=pl.DeviceIdType.MESH)
            send.start()
            copies.append((send, recv))
        for send, recv in copies:
            recv.wait_recv()
            send.wait_send()

    launch()
    return dst[...]


def _edges1(n0):
    return lambda: (pl.program_id(0) == 0, pl.program_id(0) == (7 * n0) // 8, pl.program_id(0) == n0 - 1)


def _edges2(n0, n1):
    def edges():
        step = pl.program_id(0) * n1 + pl.program_id(1)
        return step == 0, step == (7 * n0 * n1) // 8, step == n0 * n1 - 1
    return edges


def _xchg_shapes(arrs, kinds):
    return tuple(jax.ShapeDtypeStruct(((NDEV,) if k in (GATHER, GATHER_REL) else ()) + tuple(a.shape), a.dtype)
                 for a, k in zip(arrs, kinds))


def _xchg_sems(n):
    return [pltpu.SemaphoreType.DMA((n * (NDEV - 1),)), pltpu.SemaphoreType.DMA((n * (NDEV - 1),)),
            pltpu.SemaphoreType.DMA((n,))]


def _xchg_plan(in_refs, out_refs, sems, kinds):
    send_sems, recv_sems, loc_sems = sems
    x, y, c = (lax.axis_index(a) for a in AXES)
    my_abs, my_chip = 4 * x + 2 * y + c, 2 * x + y

    def peer(p):
        px = 1 - x if p & 4 else x
        py = 1 - y if p & 2 else y
        pc = 1 - c if p & 1 else c
        return (px, py, pc), 4 * px + 2 * py + pc, 2 * px + py

    starts, relays, recvs = [], [], []
    for k, kind in enumerate(kinds):
        src, dst = in_refs[k], out_refs[k]

        def remote(src_ref, row, p, pair, dst=dst, k=k):
            dev, sem = peer(p)[0], k * (NDEV - 1) + pair - 1
            return lambda: pltpu.make_async_remote_copy(
                src_ref=src_ref, dst_ref=dst.at[row], send_sem=send_sems.at[sem], recv_sem=recv_sems.at[sem],
                device_id=dev, device_id_type=pl.DeviceIdType.MESH)

        def local(src_ref, row, dst=dst, k=k):
            return lambda: pltpu.make_async_copy(src_ref, dst.at[row], loc_sems.at[k])

        if kind == GATHER:
            for p in range(1, NDEV):
                starts.append((remote(src, my_abs, p, p), False))
                recvs.append(remote(src, peer(p)[1], p, p))
            starts.append((local(src, my_abs), True))
        elif kind == GATHER_REL:
            for p in (1, 4, 2, 6):
                starts.append((remote(src, 4 * (p & 1) + my_chip, p, p), False))
            for q in (4, 2, 6):
                pchip = peer(q)[2]
                relays.append((remote(src, pchip, q, q), remote(dst.at[pchip], 4 + pchip, 1, q | 1)))
                recvs.append(remote(src, 4 + pchip, 1, q | 1))
            recvs.append(remote(src, 4 + my_chip, 1, 1))
            starts.append((local(src, my_chip), True))
        elif kind in (A2A, A2A_REL):
            for p in range(1, NDEV):
                _, pabs, pchip = peer(p)
                theirs, mine = (pabs, my_abs) if kind == A2A else (4 * (p & 1) + pchip, 4 * (p & 1) + my_chip)
                starts.append((remote(src.at[theirs], mine, p, p), False))
                recvs.append(remote(src.at[theirs], theirs, p, p))
            own = my_abs if kind == A2A else my_chip
            starts.append((local(src.at[own], own), True))
        else:
            for p in ((4, 2, 6) if kind == A2A_SAME else (1, 5, 3, 7)):
                pchip = peer(p)[2]
                starts.append((remote(src.at[pchip], my_chip, p, p), False))
                recvs.append(remote(src.at[pchip], pchip, p, p))
            if kind == A2A_SAME:
                starts.append((local(src.at[my_chip], my_chip), True))
    return starts, relays, recvs


def _xchg_start(in_refs, out_refs, sems, kinds):
    for make, _ in _xchg_plan(in_refs, out_refs, sems, kinds)[0]:
        make().start()


def _xchg_relay(in_refs, out_refs, sems, kinds):
    for arrival, forward in _xchg_plan(in_refs, out_refs, sems, kinds)[1]:
        arrival().wait_recv()
        forward().start()


def _xchg_finish(in_refs, out_refs, sems, kinds):
    starts, relays, recvs = _xchg_plan(in_refs, out_refs, sems, kinds)
    for make in recvs:
        make().wait_recv()
    for make, is_local in starts:
        if is_local:
            make().wait()
        else:
            make().wait_send()
    for _, forward in relays:
        forward().wait_send()


def _front(w13, w2, ng, c, c_ctx, w_blk, b_blk):
    d, wa = w_blk.shape
    big, big_kinds = [w13, w2, ng], [GATHER_REL, GATHER_REL, GATHER]

    def body(w13_ref, w2_ref, ng_ref, c_ref, cctx_ref, w_ref, b_ref, g13_ref, g2_ref, gng_ref, gc_ref, msl_ref, gm_ref,
             c16, mod, loc_sems, *sems):
        big_io = ((w13_ref, w2_ref, ng_ref), (g13_ref, g2_ref, gng_ref), sems[0:3], big_kinds)
        c_io = ((c_ref,), (gc_ref,), sems[3:6], [GATHER])
        mod_io = ((msl_ref,), (gm_ref,), sems[6:9], [GATHER])
        _xchg_start(*c_io)
        _xchg_start(*big_io)
        _xchg_finish(*c_io)
        rows = [pltpu.make_async_copy(gc_ref.at[k], c16.at[pl.ds(k, 1), :], loc_sems.at[k]) for k in range(NDEV)]
        for cp in rows:
            cp.start()
        c16[NDEV:NDEV + 1, :] = cctx_ref[...]
        c16[NDEV + 1:, :] = jnp.zeros((16 - NDEV - 1, d), F32)
        for cp in rows:
            cp.wait()
        cc = c16[...]
        mod[...] = _dot((cc * _sigmoid(cc)).astype(BF16), w_ref[...].astype(BF16)) + b_ref[...]
        out = pltpu.make_async_copy(mod, msl_ref, loc_sems.at[NDEV])
        out.start()
        out.wait()
        _xchg_start(*mod_io)
        _xchg_relay(*big_io)
        _xchg_finish(*big_io)
        _xchg_finish(*mod_io)

    any_spec = pl.BlockSpec(memory_space=pl.ANY)
    vmem = pl.BlockSpec(memory_space=pltpu.VMEM)
    outs = _pcall(
        body, name="exchange_weights",
        out_shape=_xchg_shapes(big + [c], big_kinds + [GATHER]) + (jax.ShapeDtypeStruct((16, wa), F32),
                                                                   jax.ShapeDtypeStruct((NDEV, 16, wa), F32)),
        in_specs=[any_spec] * 4 + [vmem] * 3, out_specs=[any_spec] * 6,
        scratch_shapes=[pltpu.VMEM((16, d), F32), pltpu.VMEM((16, wa), F32), pltpu.SemaphoreType.DMA((NDEV + 1,))]
        + _xchg_sems(3) + _xchg_sems(1) + _xchg_sems(1),
        compiler_params=_params(None),
    )(w13, w2, ng, c, c_ctx, w_blk, b_blk)
    return outs[0], outs[1], outs[2], outs[3], outs[5]


def _ada_bwd(c16, w_blk, dl, dc):
    d, w = w_blk.shape

    def body(c_ref, w_ref, dl_ref, dc_ref, gw_ref, gc_ref):
        cc = c_ref[...]
        sg = _sigmoid(cc)
        s = (cc * sg).astype(BF16)
        dctot = _colsum(dc_ref[...])
        dm = jnp.concatenate([dl_ref[...], jnp.broadcast_to(dctot, (8, w))], axis=0)
        rows = lax.broadcasted_iota(jnp.int32, (16, w), 0)
        dm = jnp.where(rows <= 8, dm, 0.0).astype(BF16)
        gw_ref[...] = _dot_tn(s, dm)
        t = _dot_nt(jnp.broadcast_to(dctot, (8, w)).astype(BF16), w_ref[...].astype(BF16))[0:1]
        c8, s8 = cc[8:9], sg[8:9]
        gc_ref[...] = t * (s8 * (1.0 + c8 * (1.0 - s8)))

    return _pcall(body, name="ada_bwd",
                  out_shape=(jax.ShapeDtypeStruct((d, w), F32), jax.ShapeDtypeStruct((1, d), F32)),
                  compiler_params=_params(None))(c16, w_blk, dl, dc)


def _norm_fwd(x, g, shift, scale):
    rstd = lax.rsqrt(jnp.mean(x * x, axis=-1, keepdims=True) + EPS)
    xhat = x * rstd
    xn = xhat * g
    return rstd, xhat, xn, xn * (1.0 + scale) + shift


def _norm_bwd(dh, rstd, xhat, xn, g, scale):
    dxn = dh * (1.0 + scale)
    dxhat = dxn * g
    dx = rstd * (dxhat - xhat * jnp.mean(dxhat * xhat, axis=-1, keepdims=True))
    return dx, _colsum(dh), _colsum(dh * xn), _colsum(dxn * xhat)


def _mod_spec(d, has_ctx):
    if has_ctx:
        return pl.BlockSpec((1, 9, d), lambda *ids: (jnp.minimum(ids[-1], 1), 0, 0))
    return pl.BlockSpec((1, 9, d), lambda *ids: (1, 0, 0))


def _row_specs(nx, tm, d):
    if nx == 1:
        return [pl.BlockSpec((tm, d), lambda i: (i, 0))]
    return [pl.BlockSpec((tm, d), lambda i: (0, 0)), pl.BlockSpec((tm, d), lambda i: (jnp.maximum(i - 1, 0), 0))]


def _rows(refs):
    if len(refs) == 1:
        return refs[0][...]
    return jnp.where(pl.program_id(0) == 0, refs[0][...], refs[1][...])


def _tile(has_ctx):
    return TM if has_ctx else TML


def _tile_of(nt, has_ctx):
    if has_ctx:
        return lambda i: jnp.where(i == 0, nt - 1, i - 1)
    return lambda i: i


def _dmod_spec(nrow, d, has_ctx):
    if has_ctx:
        return pl.BlockSpec((1, nrow, d), lambda i: (jnp.minimum(i, 1), 0, 0))
    return pl.BlockSpec((1, nrow, d), lambda i: (0, 0, 0))


def _ffn_fwd(x, modtab, g, w13, w2, k0, has_ctx, name, xchg=None, head=None):
    t, d = sum(a.shape[0] for a in x), x[0].shape[1]
    nb, w, _ = w13.shape
    fp = w // 2
    tm = _tile(has_ctx)
    nx = len(x)

    nhead = 0 if head is None else 2

    def body(*refs):
        mod_ref, g_ref, w13_ref, w2_ref = refs[nx:nx + 4]
        xo_ref, h_ref, ab_ref, o_ref = refs[nx + 4 + nhead:nx + 8 + nhead]
        xx = _rows(refs[:nx])
        shift, scale, gate = mod_ref[0, k0:k0 + 1, :], mod_ref[0, k0 + 1:k0 + 2, :], mod_ref[0, k0 + 2:k0 + 3, :]
        _, _, _, h = _norm_fwd(xx, g_ref[...], shift, scale)
        hb = h.astype(BF16)
        h_ref[...] = hb
        acc = jnp.zeros((tm, d), F32)
        for p in range(nb // 2):
            zs = []
            for q in range(2):
                blk = 2 * p + q
                ab = _dot_nt(hb, w13_ref[blk])
                ab_ref[:, blk * w:(blk + 1) * w] = ab.astype(BF16)
                a, b = ab[:, :fp], ab[:, fp:]
                zs.append((a * _sigmoid(a) * b).astype(BF16))
            acc = acc + _dot(jnp.concatenate(zs, axis=1), w2_ref[p])
        o_ref[...] = acc.astype(BF16)
        xo = xx + (0.5 * gate) * acc
        if head is None:
            xo_ref[...] = xo
        else:
            tgt_ref, fg_ref = refs[nx + 4:nx + 6]
            loss_ref, dfg_ref = refs[nx + 8 + nhead:]

            @pl.when(pl.program_id(0) == 0)
            def _():
                loss_ref[...] = jnp.zeros_like(loss_ref)
                dfg_ref[...] = jnp.zeros_like(dfg_ref)

            xo_ref[...] = _loss_head(xo, tgt_ref[...], fg_ref[...], loss_ref, dfg_ref)

    tile = _tile_of(t // tm, has_ctx)
    row = lambda i: (tile(i), 0)
    head_in = [] if head is None else [pl.BlockSpec((tm, d), row), _const((1, d))]
    head_shape = () if head is None else (jax.ShapeDtypeStruct((1, LANE), F32), jax.ShapeDtypeStruct((1, d), F32))
    head_out = () if head is None else (_const((1, LANE)), _const((1, d)))
    return _pcall(
        body, name=name, grid=(t // tm,), xchg=xchg, edges=_edges1(t // tm),
        out_shape=(jax.ShapeDtypeStruct((t, d), F32), jax.ShapeDtypeStruct((t, d), BF16),
                   jax.ShapeDtypeStruct((t, nb * w), BF16), jax.ShapeDtypeStruct((t, d), BF16)) + head_shape,
        in_specs=_row_specs(nx, tm, d) + [_mod_spec(d, has_ctx), _const((1, d)),
                                          _resident(w13.shape), _resident(w2.shape)] + head_in,
        out_specs=(pl.BlockSpec((tm, d), row), pl.BlockSpec((tm, d), row), pl.BlockSpec((tm, nb * w), row),
                   pl.BlockSpec((tm, d), row)) + head_out,
        compiler_params=_params(("arbitrary",)),
    )(*x, modtab, g, w13, w2, *(head or ()))


def _ffn_bwd(dy, modtab, h, ab, w13, w2, k0, has_ctx, name, pair0=0, npair=None, xchg=None):
    t, d = dy.shape
    _, w, _ = w13.shape
    fp = w // 2
    npair = w13.shape[0] // 2 if npair is None else npair
    nb = 2 * npair
    tm = _tile(has_ctx)
    nt = t // tm

    def body(dy_ref, mod_ref, h_ref, ab_ref, w13_ref, w2_ref, dh_ref, d13_ref, d2_ref, acc13, acc2):
        i = pl.program_id(1)

        @pl.when(i == 0)
        def _():
            acc13[...] = jnp.zeros_like(acc13)
            acc2[...] = jnp.zeros_like(acc2)

        gate = mod_ref[0, k0 + 2:k0 + 3, :]
        do = (dy_ref[...] * (0.5 * gate)).astype(BF16)
        dz = _dot_nt(do, w2_ref[0])
        hb = h_ref[...]
        dh = jnp.zeros((tm, d), F32)
        zs = []
        for q in range(2):
            ab = ab_ref[:, q * w:(q + 1) * w].astype(F32)
            a, b = ab[:, :fp], ab[:, fp:]
            sg = _sigmoid(a)
            sa = a * sg
            dzq = dz[:, q * fp:(q + 1) * fp]
            da = dzq * b * (sg * (1.0 + a * (1.0 - sg)))
            db = dzq * sa
            dab = jnp.concatenate([da, db], axis=1).astype(BF16)
            dh = dh + _dot(dab, w13_ref[q])
            acc13[q] += _dot_tn(dab, hb)
            zs.append((sa * b).astype(BF16))
        acc2[...] += _dot_tn(jnp.concatenate(zs, axis=1), do)
        dh_ref[0] = dh.astype(BF16)

        @pl.when(i == nt - 1)
        def _():
            d13_ref[...] = acc13[...].astype(BF16)
            d2_ref[0] = acc2[...].astype(BF16)

    mod_spec = _mod_spec(d, has_ctx)
    tile = _tile_of(nt, has_ctx)
    return _pcall(
        body, name=name, grid=(npair, nt), xchg=xchg, edges=_edges2(npair, nt),
        out_shape=(jax.ShapeDtypeStruct((npair, t, d), BF16), jax.ShapeDtypeStruct((nb, w, d), BF16),
                   jax.ShapeDtypeStruct((npair, 2 * fp, d), BF16)),
        in_specs=[pl.BlockSpec((tm, d), lambda p, i: (tile(i), 0)), mod_spec,
                  pl.BlockSpec((tm, d), lambda p, i: (tile(i), 0)),
                  pl.BlockSpec((tm, 2 * w), lambda p, i: (tile(i), pair0 + p)),
                  pl.BlockSpec((2, w, d), lambda p, i: (pair0 + p, 0, 0)),
                  pl.BlockSpec((1, 2 * fp, d), lambda p, i: (pair0 + p, 0, 0))],
        out_specs=(pl.BlockSpec((1, tm, d), lambda p, i: (p, tile(i), 0)),
                   pl.BlockSpec((2, w, d), lambda p, i: (p, 0, 0)),
                   pl.BlockSpec((1, 2 * fp, d), lambda p, i: (p, 0, 0))),
        scratch_shapes=[pltpu.VMEM((2, w, d), F32), pltpu.VMEM((2 * fp, d), F32)],
        compiler_params=_params(("arbitrary", "arbitrary")),
    )(dy, modtab, h, ab, w13, w2)


def _ffn_bwd_norm(dy, dhps, x, o, modtab, g, k0, has_ctx, name, xchg=None):
    t, d = dy.shape
    ngrp = 2 if has_ctx else 1
    tm = _tile(has_ctx)
    ndh, nx = len(dhps), len(x)
    lat0 = ngrp - 1

    def body(dy_ref, *rest):
        dhp_refs, x_refs = rest[:ndh], rest[ndh:ndh + nx]
        o_ref, mod_ref, g_ref, dx_ref, dmod_ref, dg_ref = rest[ndh + nx:]
        i = pl.program_id(0)

        @pl.when(i == 0)
        def _():
            dg_ref[...] = jnp.zeros_like(dg_ref)

        @pl.when((i == 0) | (i == ngrp - 1))
        def _():
            dmod_ref[...] = jnp.zeros_like(dmod_ref)

        dh = None
        for ref in dhp_refs:
            for p in range(ref.shape[0]):
                dh = ref[p].astype(F32) if dh is None else dh + ref[p].astype(F32)
        scale = mod_ref[0, k0 + 1:k0 + 2, :]
        gg = g_ref[...]
        rstd, xhat, xn, _ = _norm_fwd(_rows(x_refs), gg, 0.0, scale)
        dxn, dshift, dscale, dg = _norm_bwd(dh, rstd, xhat, xn, gg, scale)
        dyv = dy_ref[...]
        dx_ref[...] = dyv + dxn
        dmod_ref[0, 0:1, :] += dshift
        dmod_ref[0, 1:2, :] += dscale
        dmod_ref[0, 2:3, :] += _colsum(0.5 * dyv * o_ref[...].astype(F32))
        dg_ref[...] += dg

    tile = _tile_of(t // tm, has_ctx)
    row = lambda i: (tile(i), 0)
    return _pcall(
        body, name=name, grid=(t // tm,), xchg=xchg, edges=_edges1(t // tm),
        out_shape=(jax.ShapeDtypeStruct((t - lat0 * tm, d), F32), jax.ShapeDtypeStruct((ngrp, 3, d), F32),
                   jax.ShapeDtypeStruct((1, d), F32)),
        in_specs=[pl.BlockSpec((tm, d), row)]
        + [pl.BlockSpec((a.shape[0], tm, d), lambda i: (0, tile(i), 0)) for a in dhps]
        + _row_specs(nx, tm, d) + [pl.BlockSpec((tm, d), row), _mod_spec(d, has_ctx), _const((1, d))],
        out_specs=(pl.BlockSpec((tm, d), lambda i: (jnp.maximum(i - lat0, 0), 0)), _dmod_spec(3, d, has_ctx),
                   _const((1, d))),
        compiler_params=_params(("arbitrary",)),
    )(dy, *dhps, *x, o, modtab, g)


def _mix_in_fwd(x1, modtab, g, w_in):
    t, d = x1.shape
    hm = w_in.shape[1] // 2

    def body(x_ref, mod_ref, g_ref, w_ref, h_ref, up_ref, us_ref):
        _, _, _, h = _norm_fwd(x_ref[...], g_ref[...], mod_ref[0, 3:4, :], mod_ref[0, 4:5, :])
        hb = h.astype(BF16)
        h_ref[...] = hb
        u = _dot(hb, w_ref[...])
        up_ref[...] = u[:, :hm]
        us_ref[...] = u[:, hm:]

    tile = _tile_of(t // TM, True)
    row = lambda i: (tile(i), 0)
    return _pcall(
        body, name="mix_in_fwd", grid=(t // TM,),
        out_shape=(jax.ShapeDtypeStruct((t, d), BF16), jax.ShapeDtypeStruct((t, hm), F32),
                   jax.ShapeDtypeStruct((t, hm), F32)),
        in_specs=[pl.BlockSpec((TM, d), row), _mod_spec(d, True), _const((1, d)), _resident(w_in.shape)],
        out_specs=(pl.BlockSpec((TM, d), row), pl.BlockSpec((TM, hm), row), pl.BlockSpec((TM, hm), row)),
        compiler_params=_params(("arbitrary",)),
    )(x1, modtab, g, w_in)


def _mix_in_bwd(du_pool, du_f, du_b, dx2, h1, x1, modtab, g, w_in, lc):
    t, d = x1.shape
    m = w_in.shape[1]
    hm = m // 2
    nt = t // TM

    def body(dup_ref, duf_ref, dub_ref, dx2_ref, h_ref, x_ref, mod_ref, g_ref, w_ref, dx_ref, dw_ref, dmod_ref, dg_ref, acc):
        i = pl.program_id(0)

        @pl.when(i == 0)
        def _():
            acc[...] = jnp.zeros_like(acc)
            dg_ref[...] = jnp.zeros_like(dg_ref)

        @pl.when(i <= 1)
        def _():
            dmod_ref[...] = jnp.zeros_like(dmod_ref)

        lat = (i > 0).astype(F32)
        valid = (i > 0) | (lax.broadcasted_iota(jnp.int32, (TM, 1), 0) < lc)
        du_s5 = jnp.where(valid, duf_ref[...] + dub_ref[...], 0.0)
        du = jnp.concatenate([dup_ref[...] * lat, du_s5], axis=1).astype(BF16)
        dh = _dot_nt(du, w_ref[...])
        acc[...] += _dot_tn(h_ref[...], du)
        scale = mod_ref[0, 4:5, :]
        gg = g_ref[...]
        rstd, xhat, xn, _ = _norm_fwd(x_ref[...], gg, 0.0, scale)
        dxn, dshift, dscale, dg = _norm_bwd(dh, rstd, xhat, xn, gg, scale)
        dx_ref[...] = dx2_ref[...] * lat + dxn
        dmod_ref[0, 0:1, :] += dshift
        dmod_ref[0, 1:2, :] += dscale
        dg_ref[...] += dg

        @pl.when(i == nt - 1)
        def _():
            dw_ref[...] = acc[...].astype(BF16)

    tile = _tile_of(nt, True)
    row = lambda i: (tile(i), 0)
    lrow = lambda i: (jnp.maximum(i - 1, 0), 0)
    return _pcall(
        body, name="mix_in_bwd", grid=(nt,),
        out_shape=(jax.ShapeDtypeStruct((t, d), F32), jax.ShapeDtypeStruct((d, m), BF16),
                   jax.ShapeDtypeStruct((2, 2, d), F32), jax.ShapeDtypeStruct((1, d), F32)),
        in_specs=[pl.BlockSpec((TM, hm), lrow), pl.BlockSpec((TM, hm), row), pl.BlockSpec((TM, hm), row),
                  pl.BlockSpec((TM, d), lrow), pl.BlockSpec((TM, d), row), pl.BlockSpec((TM, d), row),
                  _mod_spec(d, True), _const((1, d)), _resident(w_in.shape)],
        out_specs=(pl.BlockSpec((TM, d), row), _const((d, m)), _dmod_spec(2, d, True), _const((1, d))),
        scratch_shapes=[pltpu.VMEM((d, m), F32)],
        compiler_params=_params(("arbitrary",)),
    )(du_pool, du_f, du_b, dx2, h1, x1, modtab, g, w_in)


def _pool(v, rows, transpose, name):
    l, n = rows * GRID_W, GRID_W * LANE
    ngrp = v.shape[1] // LANE
    nchunk = 4
    cw = n // nchunk

    def rowsum(val, lo, hi):
        ri = lax.broadcasted_iota(jnp.int32, (rows, rows), 0)
        ci = lax.broadcasted_iota(jnp.int32, (rows, rows), 1)
        sel = (((ci - ri) >= -lo) & ((ci - ri) <= hi)).astype(BF16)
        v1 = val.astype(BF16)
        r1 = val - v1.astype(F32)
        v2 = r1.astype(BF16)
        v3 = (r1 - v2.astype(F32)).astype(BF16)
        return _dot(sel, v1) + _dot(sel, v2) + _dot(sel, v3)

    def one(v_ref, o_ref, g_scr, r_scr, a_scr, win):
        for c in range(GRID_W):
            g_scr[:, c * LANE:(c + 1) * LANE] = v_ref[pl.ds(c, rows, stride=GRID_W), :]
        lo = win // 2
        hi = win - 1 - lo
        rlo, rhi = (hi, lo) if transpose else (lo, hi)
        ridx = lax.broadcasted_iota(jnp.int32, (rows, 1), 0)
        cnt_r = (jnp.minimum(ridx + hi + 1, rows) - jnp.maximum(ridx - lo, 0)).astype(F32)
        cidx = lax.broadcasted_iota(jnp.int32, (1, n), 1) // LANE
        cnt_c = (jnp.minimum(cidx + hi + 1, GRID_W) - jnp.maximum(cidx - lo, 0)).astype(F32)
        if not transpose:
            for k in range(nchunk):
                sl = slice(k * cw, (k + 1) * cw)
                r_scr[:, sl] = rowsum(g_scr[:, sl], rlo, rhi) / cnt_r
        else:
            r_scr[...] = g_scr[...] / cnt_c
        a_scr[...] = r_scr[...]
        for j in range(-rlo, rhi + 1):
            if j == 0:
                continue
            c0, c1 = max(0, -j), min(GRID_W, GRID_W - j)
            a_scr[:, c0 * LANE:c1 * LANE] += r_scr[:, (c0 + j) * LANE:(c1 + j) * LANE]
        if not transpose:
            r_scr[...] = a_scr[...] / cnt_c - g_scr[...]
        else:
            for k in range(nchunk):
                sl = slice(k * cw, (k + 1) * cw)
                r_scr[:, sl] = rowsum(a_scr[:, sl] / cnt_r, rlo, rhi) - g_scr[:, sl]
        for c in range(GRID_W):
            o_ref[pl.ds(c, rows, stride=GRID_W), :] = r_scr[:, c * LANE:(c + 1) * LANE]

    def body(v_ref, o_ref, g_scr, r_scr, a_scr):
        grp = pl.program_id(0)
        for k, win in enumerate(POOL_WINDOWS):
            @pl.when(grp == k)
            def _(win=win):
                one(v_ref, o_ref, g_scr, r_scr, a_scr, win)

    spec = pl.BlockSpec((l, LANE), lambda k: (0, k))
    return _pcall(
        body, name=name, grid=(ngrp,), out_shape=jax.ShapeDtypeStruct((l, ngrp * LANE), F32),
        in_specs=[spec], out_specs=spec,
        scratch_shapes=[pltpu.VMEM((rows, n), F32), pltpu.VMEM((rows, n), F32), pltpu.VMEM((rows, n), F32)],
        compiler_params=_params(("arbitrary",)),
    )(v)


def _cmul(ar, ai, br, bi):
    return ar * br - ai * bi, ar * bi + ai * br


def _s5_prep(a_re, a_im, log_dt, c_re, c_im, kvec):
    q, nc, p = c_re.shape

    def body(ar_ref, ai_ref, ldt_ref, cr_ref, ci_ref, k_ref, cpr_ref, cpi_ref, tr_ref, ti_ref):
        lr, li = ar_ref[...], ai_ref[...]
        dt = jnp.exp(ldt_ref[...])
        kk = k_ref[...]
        mag = jnp.exp(kk * (lr * dt))
        ph = kk * (li * dt)
        tr_ref[...] = mag * jnp.cos(ph)
        ti_ref[...] = mag * jnp.sin(ph)
        m1 = jnp.exp(lr * dt)
        abr, abi = m1 * jnp.cos(li * dt), m1 * jnp.sin(li * dt)
        den = lr * lr + li * li
        xr, xi = abr - 1.0, abi
        cfr, cfi = (xr * lr + xi * li) / den, (xi * lr - xr * li) / den
        pr, pi_ = _cmul(cr_ref[...], ci_ref[...], cfr, cfi)
        cpr_ref[...] = pr
        cpi_ref[...] = pi_

    return _pcall(
        body, name="s5_prep",
        out_shape=(jax.ShapeDtypeStruct((q, nc, p), F32), jax.ShapeDtypeStruct((q, nc, p), F32),
                   jax.ShapeDtypeStruct((q, NTAB, p), F32), jax.ShapeDtypeStruct((q, NTAB, p), F32)),
        compiler_params=_params(None),
    )(a_re, a_im, log_dt, c_re, c_im, kvec)


def _s5_param_bwd(a_re, a_im, log_dt, c_re, c_im, dab_r, dab_i, dcp_r, dcp_i):
    q, nc, p = c_re.shape

    def body(ar_ref, ai_ref, ldt_ref, cr_ref, ci_ref, dar_ref, dai_ref, dcr_ref, dci_ref,
             gar_ref, gai_ref, gdt_ref, gcr_ref, gci_ref):
        lr, li = ar_ref[...], ai_ref[...]
        dt = jnp.exp(ldt_ref[...])
        m1 = jnp.exp(lr * dt)
        abr, abi = m1 * jnp.cos(li * dt), m1 * jnp.sin(li * dt)
        den = lr * lr + li * li
        xr, xi = abr - 1.0, abi
        cfr, cfi = (xr * lr + xi * li) / den, (xi * lr - xr * li) / den
        cr, ci = cr_ref[...], ci_ref[...]
        dcr, dci = dcr_ref[...], dci_ref[...]
        gcr, gci = _cmul(dcr, dci, cfr, -cfi)
        gcr_ref[...] = gcr
        gci_ref[...] = gci
        t_r, t_i = _cmul(cr, -ci, dcr, dci)
        dcf_r = jnp.sum(t_r, axis=1, keepdims=True)
        dcf_i = jnp.sum(t_i, axis=1, keepdims=True)
        ilr, ili = lr / den, -li / den
        u_r, u_i = _cmul(dcf_r, dcf_i, ilr, -ili)
        dab_r_, dab_i_ = dar_ref[...] + u_r, dai_ref[...] + u_i
        v_r, v_i = _cmul(dab_r_, dab_i_, abr, -abi)
        cl_r, cl_i = _cmul(cfr, cfi, ilr, ili)
        w_r, w_i = _cmul(dcf_r, dcf_i, cl_r, -cl_i)
        gar_ref[...] = v_r * dt - w_r
        gai_ref[...] = v_i * dt - w_i
        la_r, la_i = _cmul(lr, li, abr, abi)
        ddt = la_r * dab_r_ + la_i * dab_i_
        gdt_ref[...] = dt * jnp.sum(ddt, axis=2, keepdims=True)

    return _pcall(
        body, name="s5_param_bwd",
        out_shape=(jax.ShapeDtypeStruct((q, 1, p), F32), jax.ShapeDtypeStruct((q, 1, p), F32),
                   jax.ShapeDtypeStruct((q, 1, p), F32), jax.ShapeDtypeStruct((q, nc, p), F32),
                   jax.ShapeDtypeStruct((q, nc, p), F32)),
        compiler_params=_params(None),
    )(a_re, a_im, log_dt, c_re, c_im, dab_r, dab_i, dcp_r, dcp_i)


def _bcast8(row, c):
    return jnp.broadcast_to(row, (8, c))


def _slab(j):
    return j * 8 if isinstance(j, int) else pl.multiple_of(j * 8, 8)


def _scan_head(xr, xi, tre_ref, tim_ref, car_r, car_i, desc, conj, cs):
    sg = -1.0 if conj else 1.0
    ar = _bcast8(tre_ref[0:1, :], cs)
    ai = sg * _bcast8(tim_ref[0:1, :], cs)

    def p1(jj, carry):
        hr, hi = carry
        off = _slab(SEG - 1 - jj if desc else jj)
        pr, pi_ = _cmul(ar, ai, hr, hi)
        nr = pr + xr[pl.ds(off, 8), :]
        ni = pi_ + xi[pl.ds(off, 8), :]
        xr[pl.ds(off, 8), :] = nr
        xi[pl.ds(off, 8), :] = ni
        return nr, ni

    zero = jnp.zeros((8, cs), F32)
    ir, ii = lax.fori_loop(0, SEG, p1, (zero, zero), unroll=S5_UNROLL)
    cin_r, cin_i = car_r[...], car_i[...]
    rows = lax.broadcasted_iota(jnp.int32, (8, cs), 0)
    for s, krow in ((1, SEG), (2, SEG + 1), (4, SEG + 3)):
        keep, sh = (rows < 8 - s, 8 - s) if desc else (rows >= s, s)
        sr = jnp.where(keep, pltpu.roll(ir, sh, 0), 0.0)
        si = jnp.where(keep, pltpu.roll(ii, sh, 0), 0.0)
        pr, pi_ = _cmul(tre_ref[krow:krow + 1, :], sg * tim_ref[krow:krow + 1, :], sr, si)
        ir, ii = ir + pr, ii + pi_
    q0 = SEG + 8 if desc else SEG
    pr, pi_ = _cmul(tre_ref[q0:q0 + 8, :], sg * tim_ref[q0:q0 + 8, :], cin_r, cin_i)
    fr, fi = ir + pr, ii + pi_
    keep, sh, edge = (rows < 7, 7, 0) if desc else (rows >= 1, 1, 7)
    cs_r = jnp.where(keep, pltpu.roll(fr, sh, 0), cin_r)
    cs_i = jnp.where(keep, pltpu.roll(fi, sh, 0), cin_i)
    car_r[...] = _bcast8(fr[edge:edge + 1, :], cs)
    car_i[...] = _bcast8(fi[edge:edge + 1, :], cs)
    return cs_r, cs_i, cin_r, cin_i


def _pow_row(tre_ref, tim_ref, j, desc, conj, cs):
    k = SEG - 1 - j if desc else j
    sg = -1.0 if conj else 1.0
    return _bcast8(tre_ref[pl.ds(k, 1), :], cs), sg * _bcast8(tim_ref[pl.ds(k, 1), :], cs)


def _to_segments(val, dst):
    for r in range(8):
        dst[pl.ds(r, SEG, stride=8), :] = val[r * SEG:(r + 1) * SEG, :]


def _from_segments(src):
    return jnp.concatenate([src[pl.ds(r, SEG, stride=8), :] for r in range(8)], axis=0)


def _s5_fwd(up, bre, bim, cmr, cmi, tre, tim, dskip, order, nproc, desc, with_skip, name):
    t, cu = up.shape
    nsb = bre.shape[0]
    cb = cu // nsb
    cs = bre.shape[2]
    nch = t // TC

    def body(u_ref, bre_ref, bim_ref, cmr_ref, cmi_ref, tre_ref, tim_ref, dsk_ref,
             y_ref, hr_ref, hi_ref, cinr_ref, cini_ref, car_r, car_i, seg_scr):
        i = pl.program_id(0)

        @pl.when(i == 0)
        def _():
            car_r[...] = jnp.zeros_like(car_r)
            car_i[...] = jnp.zeros_like(car_i)

        for sb in range(nsb):
            col, ch = pl.ds(sb * cs, cs), slice(sb * cb, (sb + 1) * cb)
            hr_v, hi_v = hr_ref.at[:, col], hi_ref.at[:, col]
            tre_v, tim_v = tre_ref.at[:, col], tim_ref.at[:, col]
            useg, yseg = seg_scr.at[0, sb], seg_scr.at[1, sb]
            _to_segments(u_ref[:, ch], useg)
            u = useg[...]
            ub = u.astype(BF16)
            hr_v[...] = _dot(ub, bre_ref[sb])
            hi_v[...] = _dot(ub, bim_ref[sb])
            cs_r, cs_i, cin_r, cin_i = _scan_head(hr_v, hi_v, tre_v, tim_v, car_r.at[sb], car_i.at[sb], desc, False, cs)
            cinr_ref[:, col] = cin_r
            cini_ref[:, col] = cin_i

            def p2(j, _, hr_v=hr_v, hi_v=hi_v, tre_v=tre_v, tim_v=tim_v, cs_r=cs_r, cs_i=cs_i):
                off = _slab(j)
                pw_r, pw_i = _pow_row(tre_v, tim_v, j, desc, False, cs)
                pr, pi_ = _cmul(pw_r, pw_i, cs_r, cs_i)
                hr_v[pl.ds(off, 8), :] = hr_v[pl.ds(off, 8), :] + pr
                hi_v[pl.ds(off, 8), :] = hi_v[pl.ds(off, 8), :] + pi_
                return 0

            lax.fori_loop(0, SEG, p2, 0, unroll=S5_UNROLL)
            y = _dot(hr_v[...].astype(BF16), cmr_ref[sb]) - _dot(hi_v[...].astype(BF16), cmi_ref[sb])
            if with_skip:
                y = y + dsk_ref[:, ch] * u
            yseg[...] = y
            y_ref[:, ch] = _from_segments(yseg)

    blk = lambda i: (order(i), 0)
    return _pcall(
        body, name=name, grid=(nproc,),
        out_shape=(jax.ShapeDtypeStruct((t, cu), F32), jax.ShapeDtypeStruct((t, nsb * cs), F32),
                   jax.ShapeDtypeStruct((t, nsb * cs), F32), jax.ShapeDtypeStruct((nch * 8, nsb * cs), F32),
                   jax.ShapeDtypeStruct((nch * 8, nsb * cs), F32)),
        in_specs=[pl.BlockSpec((TC, cu), blk), _const(bre.shape), _const(bim.shape), _const(cmr.shape),
                  _const(cmi.shape), _const(tre.shape), _const(tim.shape), _const(dskip.shape)],
        out_specs=(pl.BlockSpec((TC, cu), blk), pl.BlockSpec((TC, nsb * cs), blk), pl.BlockSpec((TC, nsb * cs), blk),
                   pl.BlockSpec((8, nsb * cs), blk), pl.BlockSpec((8, nsb * cs), blk)),
        scratch_shapes=[pltpu.VMEM((nsb, 8, cs), F32), pltpu.VMEM((nsb, 8, cs), F32),
                        pltpu.VMEM((2, nsb, TC, cb), F32)],
        compiler_params=_params(("arbitrary",)),
    )(up, bre, bim, cmr, cmi, tre, tim, dskip)


def _s5_bwd(dy_lat, up, hr, hi, cinr, cini, bre, bim, ctr, cti, tre, tim, dskip, order, nproc, desc, with_skip, name,
            xchg=None):
    t, cu = up.shape
    nsb = bre.shape[0]
    cb = cu // nsb
    cs = bre.shape[2]
    lch = dy_lat.shape[0] // TC
    adesc = not desc

    def body(dy_ref, u_ref, hr_ref, hi_ref, cinr_ref, cini_ref, bre_ref, bim_ref, ctr_ref, cti_ref, tre_ref, tim_ref,
             dsk_ref, du_ref, dar_ref, dai_ref, dcr_ref, dci_ref, dbr_ref, dbi_ref, dd_ref,
             car_r, car_i, mr_all, mi_all, acc_r, acc_i, seg_scr):
        i = pl.program_id(0)
        latent = (order(i) < lch).astype(F32)

        @pl.when(i == 0)
        def _():
            for ref in (car_r, car_i, acc_r, acc_i, dcr_ref, dci_ref, dbr_ref, dbi_ref, dd_ref):
                ref[...] = jnp.zeros_like(ref)

        rows = lax.broadcasted_iota(jnp.int32, (8, cs), 0)
        for sb in range(nsb):
            col, ch = pl.ds(sb * cs, cs), slice(sb * cb, (sb + 1) * cb)
            hr_v, hi_v = hr_ref.at[:, col], hi_ref.at[:, col]
            tre_v, tim_v = tre_ref.at[:, col], tim_ref.at[:, col]
            mr, mi = mr_all.at[sb % 2], mi_all.at[sb % 2]
            useg, dyseg, duseg = seg_scr.at[0, sb], seg_scr.at[1, sb], seg_scr.at[2, sb]
            _to_segments(dy_ref[:, ch] * latent, dyseg)
            _to_segments(u_ref[:, ch], useg)
            dy = dyseg[...]
            dyb = dy.astype(BF16)
            u = useg[...]
            ub = u.astype(BF16)
            mr[...] = _dot(dyb, ctr_ref[sb])
            mi[...] = -_dot(dyb, cti_ref[sb])
            cs_r, cs_i, _, _ = _scan_head(mr, mi, tre_v, tim_v, car_r.at[sb], car_i.at[sb], adesc, True, cs)

            def fix(j, hp_r, hp_i, acc, mr=mr, mi=mi, tre_v=tre_v, tim_v=tim_v, cs_r=cs_r, cs_i=cs_i):
                off = _slab(j)
                pw_r, pw_i = _pow_row(tre_v, tim_v, j, adesc, True, cs)
                pr, pi_ = _cmul(pw_r, pw_i, cs_r, cs_i)
                m_r = mr[pl.ds(off, 8), :] + pr
                m_i = mi[pl.ds(off, 8), :] + pi_
                mr[pl.ds(off, 8), :] = m_r
                mi[pl.ds(off, 8), :] = m_i
                a_r, a_i = acc
                return a_r + hp_r * m_r + hp_i * m_i, a_i + hp_r * m_i - hp_i * m_r

            edge_j, src, keep, sh = (SEG - 1, 0, rows < 7, 7) if desc else (0, (SEG - 1) * 8, rows >= 1, 1)
            h0r = jnp.where(keep, pltpu.roll(hr_v[src:src + 8, :], sh, 0), cinr_ref[:, col])
            h0i = jnp.where(keep, pltpu.roll(hi_v[src:src + 8, :], sh, 0), cini_ref[:, col])
            acc = fix(edge_j, h0r, h0i, (acc_r[sb], acc_i[sb]))

            def p2(jj, acc, fix=fix, hr_v=hr_v, hi_v=hi_v):
                j = jj if desc else jj + 1
                offp = _slab(j + 1 if desc else j - 1)
                return fix(j, hr_v[pl.ds(offp, 8), :], hi_v[pl.ds(offp, 8), :], acc)

            a_r, a_i = lax.fori_loop(0, SEG - 1, p2, acc, unroll=S5_UNROLL)
            acc_r[sb] = a_r
            acc_i[sb] = a_i

            mrb, mib = mr[...].astype(BF16), mi[...].astype(BF16)
            du = _dot_nt(mrb, bre_ref[sb]) + _dot_nt(mib, bim_ref[sb])
            if with_skip:
                du = du + dsk_ref[:, ch] * dy
                dd_ref[:, ch] += _colsum(dy * u)
            duseg[...] = du
            du_ref[:, ch] = _from_segments(duseg)
            dbr_ref[sb] += _dot_tn(ub, mrb)
            dbi_ref[sb] += _dot_tn(ub, mib)
            dcr_ref[sb] += _dot_tn(dyb, hr_v[...].astype(BF16))
            dci_ref[sb] -= _dot_tn(dyb, hi_v[...].astype(BF16))

            @pl.when(i == nproc - 1)
            def _(sb=sb, a_r=a_r, a_i=a_i):
                dar_ref[sb] = _colsum(a_r)
                dai_ref[sb] = _colsum(a_i)

    blk = lambda i: (order(i), 0)
    blk_dy = lambda i: (jnp.minimum(order(i), lch - 1), 0)
    mat = (nsb, cb, cs)
    return _pcall(
        body, name=name, grid=(nproc,), xchg=xchg, edges=_edges1(nproc),
        out_shape=(jax.ShapeDtypeStruct((t, cu), F32),
                   jax.ShapeDtypeStruct((nsb, 1, cs), F32), jax.ShapeDtypeStruct((nsb, 1, cs), F32),
                   jax.ShapeDtypeStruct(mat, F32), jax.ShapeDtypeStruct(mat, F32),
                   jax.ShapeDtypeStruct(mat, F32), jax.ShapeDtypeStruct(mat, F32),
                   jax.ShapeDtypeStruct((1, cu), F32)),
        in_specs=[pl.BlockSpec((TC, cu), blk_dy), pl.BlockSpec((TC, cu), blk), pl.BlockSpec((TC, nsb * cs), blk),
                  pl.BlockSpec((TC, nsb * cs), blk), pl.BlockSpec((8, nsb * cs), blk), pl.BlockSpec((8, nsb * cs), blk),
                  _const(mat), _const(mat), _const(mat), _const(mat), _const(tre.shape), _const(tim.shape),
                  _const((1, cu))],
        out_specs=(pl.BlockSpec((TC, cu), blk), _const((nsb, 1, cs)), _const((nsb, 1, cs)),
                   _const(mat), _const(mat), _const(mat), _const(mat), _const((1, cu))),
        scratch_shapes=[pltpu.VMEM((nsb, 8, cs), F32), pltpu.VMEM((nsb, 8, cs), F32), pltpu.VMEM((2, TC, cs), F32),
                        pltpu.VMEM((2, TC, cs), F32), pltpu.VMEM((nsb, 8, cs), F32), pltpu.VMEM((nsb, 8, cs), F32),
                        pltpu.VMEM((3, nsb, TC, cb), F32)],
        compiler_params=_params(("arbitrary",)),
    )(dy_lat, up, hr, hi, cinr, cini, bre, bim, ctr, cti, tre, tim, dskip)


def _gelu(x):
    k = math.sqrt(2.0 / math.pi)
    return 0.5 * x * (1.0 + jnp.tanh(k * (x + 0.044715 * (x * x * x))))


def _gelu_grad(x):
    k = math.sqrt(2.0 / math.pi)
    th = jnp.tanh(k * (x + 0.044715 * (x * x * x)))
    return 0.5 * (1.0 + th) + 0.5 * x * (1.0 - th * th) * (k * (1.0 + 3.0 * 0.044715 * (x * x)))


def _mix_out_fwd(yf, yb, dpool, x1, modtab, w_pool, pscale, w_glu, w_out):
    l, hm = dpool.shape
    d = x1.shape[1]

    def body(yf_ref, yb_ref, dp_ref, x_ref, mod_ref, wp_ref, ps_ref, wg_ref, wo_ref, x2_ref, yp_ref, cat_ref, mix_ref):
        ypre = yf_ref[...] + yb_ref[...]
        yp_ref[...] = ypre
        yg = _gelu(ypre)
        y2 = yg * _sigmoid(_dot(yg.astype(BF16), wg_ref[...]))
        po = _dot(dp_ref[...].astype(BF16), wp_ref[...]) * ps_ref[...]
        cat = jnp.concatenate([po, y2], axis=1).astype(BF16)
        cat_ref[...] = cat
        mix = _dot(cat, wo_ref[...])
        mix_ref[...] = mix.astype(BF16)
        x2_ref[...] = x_ref[...] + mod_ref[0, 5:6, :] * mix

    row = lambda i: (i, 0)
    lrow = row
    return _pcall(
        body, name="mix_out_fwd", grid=(l // TM,),
        out_shape=(jax.ShapeDtypeStruct((l, d), F32), jax.ShapeDtypeStruct((l, hm), F32),
                   jax.ShapeDtypeStruct((l, d), BF16), jax.ShapeDtypeStruct((l, d), BF16)),
        in_specs=[pl.BlockSpec((TM, hm), lrow), pl.BlockSpec((TM, hm), lrow), pl.BlockSpec((TM, hm), row),
                  pl.BlockSpec((TM, d), lrow), _mod_spec(d, False), _const(w_pool.shape), _const((1, hm)),
                  _const(w_glu.shape), _resident(w_out.shape)],
        out_specs=(pl.BlockSpec((TM, d), row), pl.BlockSpec((TM, hm), row), pl.BlockSpec((TM, d), row),
                   pl.BlockSpec((TM, d), row)),
        compiler_params=_params(("arbitrary",)),
    )(yf, yb, dpool, x1, modtab, w_pool, pscale, w_glu, w_out)


def _mix_out_bwd(dx2, mix, cat, ypre, dpool, modtab, w_pool, pscale, w_glu, w_out):
    l, hm = dpool.shape
    d = dx2.shape[1]
    nt = l // TML

    def body(dx_ref, mix_ref, cat_ref, yp_ref, dp_ref, mod_ref, wp_ref, ps_ref, wg_ref, wo_ref,
             dyp_ref, ddp_ref, dwo_ref, dwg_ref, dwp_ref, dps_ref, dgate_ref, acc_o, acc_g):
        i = pl.program_id(0)

        @pl.when(i == 0)
        def _():
            for ref in (acc_o, acc_g, dwp_ref, dps_ref, dgate_ref):
                ref[...] = jnp.zeros_like(ref)

        dx = dx_ref[...]
        dgate_ref[...] += _colsum(dx * mix_ref[...].astype(F32))
        dmix = (dx * mod_ref[0, 5:6, :]).astype(BF16)
        dcat = _dot_nt(dmix, wo_ref[...])
        acc_o[...] += _dot_tn(cat_ref[...], dmix)
        dpo, dy2 = dcat[:, :hm], dcat[:, hm:]
        dpb = dp_ref[...].astype(BF16)
        pp = _dot(dpb, wp_ref[...])
        dps_ref[...] += _colsum(dpo * pp)
        dpp = (dpo * ps_ref[...]).astype(BF16)
        ddp_ref[...] = _dot_nt(dpp, wp_ref[...])
        dwp_ref[...] += _dot_tn(dpb, dpp)
        ypre = yp_ref[...]
        yg = _gelu(ypre)
        ygb = yg.astype(BF16)
        s = _sigmoid(_dot(ygb, wg_ref[...]))
        dq = (dy2 * yg * s * (1.0 - s)).astype(BF16)
        dyg = dy2 * s + _dot_nt(dq, wg_ref[...])
        acc_g[...] += _dot_tn(ygb, dq)
        dyp_ref[...] = dyg * _gelu_grad(ypre)

        @pl.when(i == nt - 1)
        def _():
            dwo_ref[...] = acc_o[...].astype(BF16)
            dwg_ref[...] = acc_g[...].astype(BF16)

    row = lambda i: (i, 0)
    return _pcall(
        body, name="mix_out_bwd", grid=(nt,),
        out_shape=(jax.ShapeDtypeStruct((l, hm), F32), jax.ShapeDtypeStruct((l, hm), F32),
                   jax.ShapeDtypeStruct((d, d), BF16), jax.ShapeDtypeStruct((hm, hm), BF16),
                   jax.ShapeDtypeStruct((hm, hm), F32), jax.ShapeDtypeStruct((1, hm), F32),
                   jax.ShapeDtypeStruct((1, d), F32)),
        in_specs=[pl.BlockSpec((TML, d), row), pl.BlockSpec((TML, d), row), pl.BlockSpec((TML, d), row),
                  pl.BlockSpec((TML, hm), row), pl.BlockSpec((TML, hm), row), _mod_spec(d, False),
                  _const(w_pool.shape), _const((1, hm)), _const(w_glu.shape), _resident(w_out.shape)],
        out_specs=(pl.BlockSpec((TML, hm), row), pl.BlockSpec((TML, hm), row), _const((d, d)), _const((hm, hm)),
                   _const((hm, hm)), _const((1, hm)), _const((1, d))),
        scratch_shapes=[pltpu.VMEM((d, d), F32), pltpu.VMEM((hm, hm), F32)],
        compiler_params=_params(("arbitrary",)),
    )(dx2, mix, cat, ypre, dpool, modtab, w_pool, pscale, w_glu, w_out)


def _loss_head(xx, target, gg, loss_ref, dg_ref):
    d = xx.shape[-1]
    rstd = lax.rsqrt(jnp.mean(xx * xx, axis=-1, keepdims=True) + EPS)
    xhat = xx * rstd
    err = xhat * gg - target
    row_loss = jnp.mean(err * err, axis=-1, keepdims=True)
    loss_ref[...] += 0.5 * jnp.sum(row_loss, axis=0, keepdims=True)
    dout = err * (1.0 / d)
    dg_ref[...] += _colsum(dout * xhat)
    dxhat = dout * gg
    return rstd * (dxhat - xhat * jnp.mean(dxhat * xhat, axis=-1, keepdims=True))


def _adamw(parts, w, m, v, name):
    shape = w.shape
    c = shape[-1]
    r = int(np.prod(shape)) // c
    npart = parts.shape[0]
    tr = r
    for cand in (1024, 512, 256, 128, 64, 32, 16):
        if r % cand == 0 and cand * max(c, LANE) * 4 <= ADAM_BLOCK_BYTES:
            tr = cand
            break
    c1 = 1.0 - ADAM_B1
    c2 = 1.0 - ADAM_B2
    bc1 = 1.0 - ADAM_B1 ** ADAM_STEP
    bc2 = 1.0 - ADAM_B2 ** ADAM_STEP

    def body(p_ref, w_ref, m_ref, v_ref, g_ref, d_ref, mo_ref, vo_ref):
        g = p_ref[0].astype(F32)
        for k in range(1, npart):
            g = g + p_ref[k].astype(F32)
        mn = ADAM_B1 * m_ref[...] + c1 * g
        vn = ADAM_B2 * v_ref[...] + c2 * (g * g)
        m_hat = mn / bc1
        v_hat = vn / bc2
        g_ref[...] = g
        mo_ref[...] = mn
        vo_ref[...] = vn
        d_ref[...] = -ADAM_LR * (m_hat / (jnp.sqrt(v_hat) + ADAM_EPS) + ADAM_WD * w_ref[...])

    spec = pl.BlockSpec((tr, c), lambda i: (i, 0))
    outs = _pcall(
        body, name=name, grid=(r // tr,),
        out_shape=tuple(jax.ShapeDtypeStruct((r, c), F32) for _ in range(4)),
        in_specs=[pl.BlockSpec((npart, tr, c), lambda i: (0, i, 0)), spec, spec, spec],
        out_specs=(spec, spec, spec, spec),
        compiler_params=_params(("arbitrary",)),
    )(parts.reshape(npart, r, c), w.reshape(r, c), m.reshape(r, c), v.reshape(r, c))
    return tuple(o.reshape(shape) for o in outs)


def _adamw_ffn(parts, row0, w, m, v, name):
    _, nl, r, c = w.shape
    rp = parts[0][0].shape[1]
    tc = ADAM_FFN_LANES
    grid = (c // tc,)
    p_spec = lambda a: pl.BlockSpec((a.shape[0], rp, tc), lambda i: (0, 0, i))
    w_spec = pl.BlockSpec((1, nl, r, tc), lambda i: (0, 0, 0, i))
    take = lambda ref, k: ref[k, row0:row0 + r, :]
    flat = [a for layer in parts for a in layer]
    counts = [len(layer) for layer in parts]
    c1 = 1.0 - ADAM_B1
    c2 = 1.0 - ADAM_B2
    bc1 = 1.0 - ADAM_B1 ** ADAM_STEP
    bc2 = 1.0 - ADAM_B2 ** ADAM_STEP

    def body(*refs):
        w_ref, m_ref, v_ref, g_ref, d_ref, mo_ref, vo_ref = refs[len(flat):]
        first = 0
        for lyr in range(nl):
            g = None
            for ref in refs[first:first + counts[lyr]]:
                for k in range(ref.shape[0]):
                    term = take(ref, k).astype(F32)
                    g = term if g is None else g + term
            first += counts[lyr]
            mn = ADAM_B1 * m_ref[0, lyr] + c1 * g
            vn = ADAM_B2 * v_ref[0, lyr] + c2 * (g * g)
            g_ref[0, lyr] = g
            mo_ref[0, lyr] = mn
            vo_ref[0, lyr] = vn
            d_ref[0, lyr] = -ADAM_LR * ((mn / bc1) / (jnp.sqrt(vn / bc2) + ADAM_EPS) + ADAM_WD * w_ref[0, lyr])

    return _pcall(
        body, name=name, grid=grid, out_shape=tuple(jax.ShapeDtypeStruct(w.shape, F32) for _ in range(4)),
        in_specs=[p_spec(a) for a in flat] + [w_spec] * 3, out_specs=(w_spec,) * 4,
        compiler_params=_params(("arbitrary",)),
    )(*flat, w, m, v)


def _blockdiag(xb):
    n, a, b = xb.shape[-3:]
    eye = jnp.eye(n, dtype=xb.dtype)
    out = xb[..., :, :, None, :] * eye[:, None, :, None]
    return out.reshape(xb.shape[:-3] + (n * a, n * b))


def _diagblocks(mat, n):
    lead = mat.shape[:-2]
    a, b = mat.shape[-2] // n, mat.shape[-1] // n
    m5 = mat.reshape(lead + (n, a, n, b))
    nl = len(lead)
    dg = jnp.diagonal(m5, axis1=nl, axis2=nl + 2)
    return jnp.moveaxis(dg, -1, nl)


def _pad128(a):
    flat = a.reshape(-1)
    pad = (-flat.shape[0]) % LANE
    return jnp.pad(flat, (0, pad)) if pad else flat


def kernel(x, c, ctx, c_ctx, norm_g, w_ada, b_ada, ffn_w1, ffn_w3, ffn_w2, w_in, pool_w, pool_scale, s5_a_re, s5_a_im, s5_log_dt, s5_b_re, s5_b_im, s5_c_re, s5_c_im, s5_d, s5_w_glu, w_out, final_g, loss_target, m_c_ctx, m_norm_g, m_w_ada, m_b_ada, m_ffn_w1, m_ffn_w3, m_ffn_w2, m_w_in, m_pool_w, m_pool_scale, m_s5_a_re, m_s5_a_im, m_s5_log_dt, m_s5_b_re, m_s5_b_im, m_s5_c_re, m_s5_c_im, m_s5_d, m_s5_w_glu, m_w_out, m_final_g, v_c_ctx, v_norm_g, v_w_ada, v_b_ada, v_ffn_w1, v_ffn_w3, v_ffn_w2, v_w_in, v_pool_w, v_pool_scale, v_s5_a_re, v_s5_a_im, v_s5_log_dt, v_s5_b_re, v_s5_b_im, v_s5_c_re, v_s5_c_im, v_s5_d, v_s5_w_glu, v_w_out, v_final_g):
    weights = dict(c_ctx=c_ctx, norm_g=norm_g, w_ada=w_ada, b_ada=b_ada, ffn_w1=ffn_w1, ffn_w3=ffn_w3, ffn_w2=ffn_w2,
                   w_in=w_in, pool_w=pool_w, pool_scale=pool_scale, s5_a_re=s5_a_re, s5_a_im=s5_a_im,
                   s5_log_dt=s5_log_dt, s5_b_re=s5_b_re, s5_b_im=s5_b_im, s5_c_re=s5_c_re, s5_c_im=s5_c_im, s5_d=s5_d,
                   s5_w_glu=s5_w_glu, w_out=w_out, final_g=final_g)
    mom_m = dict(c_ctx=m_c_ctx, norm_g=m_norm_g, w_ada=m_w_ada, b_ada=m_b_ada, ffn_w1=m_ffn_w1, ffn_w3=m_ffn_w3,
                 ffn_w2=m_ffn_w2, w_in=m_w_in, pool_w=m_pool_w, pool_scale=m_pool_scale, s5_a_re=m_s5_a_re,
                 s5_a_im=m_s5_a_im, s5_log_dt=m_s5_log_dt, s5_b_re=m_s5_b_re, s5_b_im=m_s5_b_im, s5_c_re=m_s5_c_re,
                 s5_c_im=m_s5_c_im, s5_d=m_s5_d, s5_w_glu=m_s5_w_glu, w_out=m_w_out, final_g=m_final_g)
    mom_v = dict(c_ctx=v_c_ctx, norm_g=v_norm_g, w_ada=v_w_ada, b_ada=v_b_ada, ffn_w1=v_ffn_w1, ffn_w3=v_ffn_w3,
                 ffn_w2=v_ffn_w2, w_in=v_w_in, pool_w=v_pool_w, pool_scale=v_pool_scale, s5_a_re=v_s5_a_re,
                 s5_a_im=v_s5_a_im, s5_log_dt=v_s5_log_dt, s5_b_re=v_s5_b_re, s5_b_im=v_s5_b_im, s5_c_re=v_s5_c_re,
                 s5_c_im=v_s5_c_im, s5_d=v_s5_d, s5_w_glu=v_s5_w_glu, w_out=v_w_out, final_g=v_final_g)

    l, d = x.shape[1], x.shape[2]
    lc = ctx.shape[1]
    t = l + TM
    rows = l // GRID_W
    fblk = ffn_w1.shape[-1]
    ngrp, gp = s5_a_re.shape[2], s5_a_re.shape[3]
    gc = s5_b_re.shape[3]
    hm = ngrp * gc
    nsb = 4
    gsb = ngrp // nsb
    assert lc == TC and TM % TC == 0 and l % TM == 0 and TM == TML and hm == 4 * LANE and ngrp * gp == nsb * 512
    me = 4 * lax.axis_index("x") + 2 * lax.axis_index("y") + lax.axis_index("c")

    padr = ((0, FPAD - fblk), (0, 0))
    w1_t, w3_t = jnp.swapaxes(ffn_w1, 2, 3), jnp.swapaxes(ffn_w3, 2, 3)
    w13_loc = [jnp.concatenate([jnp.pad(w1_t[0, k], padr), jnp.pad(w3_t[0, k], padr)], axis=0).astype(BF16)
               for k in range(2)]
    w2_loc = [jnp.pad(ffn_w2[0, k], padr).astype(BF16) for k in range(2)]
    wa = w_ada.shape[-1]
    b_blk = lax.dynamic_slice_in_dim(b_ada, me * wa, wa, axis=1)
    w13_0, w2_0, g_all, c_all, mod_all = _front(w13_loc[0], w2_loc[0], norm_g[0], c, c_ctx.reshape(1, d), w_ada[0],
                                                b_blk)
    w2_0 = w2_0.reshape(NDEV // 2, 2 * FPAD, d)
    c_all = c_all.reshape(NDEV, d)
    g_all = g_all.transpose(1, 0, 2).reshape(3, d)

    c16 = jnp.concatenate([c_all, c_ctx.reshape(1, d), jnp.zeros((7, d), F32)], axis=0)
    mod_full = mod_all.transpose(1, 0, 2).reshape(16, NDEV * wa)
    mod_l = lax.dynamic_index_in_dim(mod_full, me, axis=0, keepdims=False).reshape(9, d)
    modtab = jnp.stack([mod_full[8].reshape(9, d), mod_l])

    x0 = [jnp.pad(ctx[0], ((0, TM - lc), (0, 0))), x[0]]
    g0, g1, g2 = g_all[0:1], g_all[1:2], g_all[2:3]
    x1, h0, ab0, o0, w13_1, w2_1, w_in_f, w_out_f, w_glu_f = _ffn_fwd(
        x0, modtab, g0, w13_0, w2_0, 0, True, "ffn0_fwd",
        xchg=[(w13_loc[1], GATHER_REL), (w2_loc[1], GATHER_REL), (w_in[0].astype(BF16), GATHER),
              (w_out[0].astype(BF16), GATHER), (s5_w_glu[0].astype(BF16), GATHER)])
    w2_1 = w2_1.reshape(NDEV // 2, 2 * FPAD, d)
    w_in_f = w_in_f.reshape(d, -1)
    w_out_f = w_out_f.reshape(-1, d)
    w_glu_f = w_glu_f.reshape(hm, hm)
    h1, u_pool, up = _mix_in_fwd(x1, modtab, g1, w_in_f)

    dpool = _pool(u_pool, rows, False, "pool_fwd")
    w_pool_bd = _blockdiag(pool_w[0]).astype(BF16)

    q = 2 * ngrp
    kvec = np.concatenate([np.arange(1, SEG + 1), SEG * np.arange(1, 9), SEG * np.arange(8, 0, -1)]).astype(np.float32)
    a_re3, a_im3 = s5_a_re[0].reshape(q, 1, gp), s5_a_im[0].reshape(q, 1, gp)
    ldt3 = jnp.broadcast_to(s5_log_dt[0].reshape(q, 1, 1), (q, 1, gp))
    c_re3, c_im3 = s5_c_re[0].reshape(q, gc, gp), s5_c_im[0].reshape(q, gc, gp)
    cp_re, cp_im, tab_re, tab_im = _s5_prep(a_re3, a_im3, ldt3, c_re3, c_im3, jnp.broadcast_to(jnp.asarray(kvec).reshape(1, NTAB, 1), (1, NTAB, gp)))
    tabs = [(tab_re.reshape(2, ngrp, NTAB, gp)[k].transpose(1, 0, 2).reshape(NTAB, ngrp * gp),
             tab_im.reshape(2, ngrp, NTAB, gp)[k].transpose(1, 0, 2).reshape(NTAB, ngrp * gp)) for k in range(2)]
    b_t = lambda b: _blockdiag(b[0].reshape(nsb, gsb, gp, gc).transpose(0, 1, 3, 2)).astype(BF16)
    bre, bim = b_t(s5_b_re), b_t(s5_b_im)
    cp_t = lambda cp: _blockdiag(cp.reshape(2, nsb, gsb, gc, gp)).astype(BF16)
    ctr, cti = cp_t(cp_re), cp_t(cp_im)
    cmr, cmi = jnp.swapaxes(ctr, -1, -2), jnp.swapaxes(cti, -1, -2)
    dskip = s5_d

    lch = l // TC
    nproc = lch + 1
    order_f = lambda i: jnp.where(i == 0, lch, i - 1)
    order_b = lambda i: jnp.where(i == 0, lch, lch - i)
    rorder_f = lambda i: jnp.where(i == lch, lch, lch - 1 - i)
    rorder_b = lambda i: i
    y_f, hr_f, hi_f, cinr_f, cini_f = _s5_fwd(up, bre, bim, cmr[0], cmi[0], tabs[0][0], tabs[0][1], dskip,
                                               order_f, nproc, False, True, "s5_fwd_f")
    y_b, hr_b, hi_b, cinr_b, cini_b = _s5_fwd(up, bre, bim, cmr[1], cmi[1], tabs[1][0], tabs[1][1], dskip,
                                              order_b, nproc, True, False, "s5_fwd_b")

    x2, ypre, cat, mix = _mix_out_fwd(y_f, y_b, dpool, x1, modtab, w_pool_bd, pool_scale, w_glu_f, w_out_f)
    dx3, h2, ab2, o2, loss_part, dfinal_g = _ffn_fwd([x2], modtab, g2, w13_1, w2_1, 6, False, "ffn1_fwd",
                                                     head=(loss_target[0], final_g.reshape(1, d)))

    dhp2, d13_1, d2_1 = _ffn_bwd(dx3, modtab, h2, ab2, w13_1, w2_1, 6, False, "ffn1_bwd")
    dx2, dmod_678, dg2 = _ffn_bwd_norm(dx3, [dhp2], [x2], o2, modtab, g2, 6, False, "ffn1_bwd_norm")
    dypre, ddpool, dw_out, dw_glu, dw_pool_bd, dpscale, dgate5 = _mix_out_bwd(
        dx2, mix, cat, ypre, dpool, modtab, w_pool_bd, pool_scale, w_glu_f, w_out_f)

    du_f, dar_f, dai_f, dcr_f, dci_f, dbr_f, dbi_f, dd_skip, r13_1 = _s5_bwd(
        dypre, up, hr_f, hi_f, cinr_f, cini_f, bre, bim, ctr[0], cti[0], tabs[0][0], tabs[0][1], dskip,
        rorder_f, nproc, False, True, "s5_bwd_f",
        xchg=[(d13_1, A2A_REL)])
    du_b, dar_b, dai_b, dcr_b, dci_b, dbr_b, dbi_b, _, r2_1, r_out, r_glu = _s5_bwd(
        dypre, up, hr_b, hi_b, cinr_b, cini_b, bre, bim, ctr[1], cti[1], tabs[1][0], tabs[1][1], dskip,
        rorder_b, nproc, True, False, "s5_bwd_b",
        xchg=[(d2_1.reshape(NDEV, FPAD, d), A2A_REL), (dw_out.reshape(NDEV, -1, d), A2A),
              (dw_glu.reshape(NDEV, hm // NDEV, hm), A2A)])

    dab_r = jnp.stack([dar_f, dar_b]).reshape(q, 1, gp)
    dab_i = jnp.stack([dai_f, dai_b]).reshape(q, 1, gp)
    dcp_r = _diagblocks(jnp.stack([dcr_f, dcr_b]), gsb).reshape(q, gc, gp)
    dcp_i = _diagblocks(jnp.stack([dci_f, dci_b]), gsb).reshape(q, gc, gp)
    ga_re, ga_im, gldt, gc_re, gc_im = _s5_param_bwd(a_re3, a_im3, ldt3, c_re3, c_im3, dab_r, dab_i, dcp_r, dcp_i)
    gb_re = (_diagblocks(dbr_f, gsb) + _diagblocks(dbr_b, gsb)).transpose(0, 1, 3, 2).reshape(ngrp, gp, gc)
    gb_im = (_diagblocks(dbi_f, gsb) + _diagblocks(dbi_b, gsb)).transpose(0, 1, 3, 2).reshape(ngrp, gp, gc)

    def pack(small):
        offs, pieces, off = {}, [], 0
        for k_, a_ in small.items():
            p_ = _pad128(a_.astype(F32))
            offs[k_] = (off, int(np.prod(a_.shape)))
            off += p_.shape[0]
            pieces.append(p_)
        return jnp.concatenate(pieces).reshape(1, off), offs

    gw_pool = _diagblocks(dw_pool_bd, 4)
    bundle_a, offs = pack(dict(pool_w=gw_pool, pool_scale=dpscale, s5_a_re=ga_re, s5_a_im=ga_im,
                               s5_log_dt=gldt[:, 0, 0], s5_b_re=gb_re, s5_b_im=gb_im, s5_c_re=gc_re, s5_c_im=gc_im,
                               s5_d=dd_skip, final_g=dfinal_g))

    du_pool = _pool(ddpool, rows, True, "pool_bwd")
    dx1, dw_in, dmod_34, dg1 = _mix_in_bwd(du_pool, du_f, du_b, dx2, h1, x1, modtab, g1, w_in_f, lc)
    dhp0a, d13_0a, d2_0a, bund_a, r_in = _ffn_bwd(
        dx1, modtab, h0, ab0, w13_0, w2_0, 0, True, "ffn0_bwd_a", pair0=0, npair=2,
        xchg=[(bundle_a, GATHER), (dw_in.reshape(NDEV, d // NDEV, -1), A2A)])
    dhp0b, d13_0b, d2_0b, r13_0a, r2_0a = _ffn_bwd(
        dx1, modtab, h0, ab0, w13_0, w2_0, 0, True, "ffn0_bwd_b", pair0=2, npair=2,
        xchg=[(d13_0a, A2A_SAME), (d2_0a.reshape(NDEV // 2, FPAD, d), A2A_SAME)])
    r13_0b, r2_0b = _sequencer_a2a_other([d13_0b, d2_0b.reshape(NDEV // 2, FPAD, d)], "seq_a2a_ffn0")
    dx0, dmod_012, dg0 = _ffn_bwd_norm(dx1, [dhp0a, dhp0b], x0, o0, modtab, g0, 0, True, "ffn0_bwd_norm")
    grad_x = dx0.reshape(1, l, d)

    dmod_c = jnp.concatenate([dmod_012[0], dmod_34[0], jnp.zeros((4, d), F32)], axis=0)
    dmod_l = jnp.concatenate([dmod_012[1], dmod_34[1], dgate5, dmod_678[0]], axis=0)
    bundle_b, offs_b = pack(dict(norm_g=jnp.concatenate([dg0, dg1, dg2], axis=0), dmod_l=dmod_l, dmod_c=dmod_c,
                                 loss=loss_part[:, :1]))
    bund_b = lax.dynamic_update_slice(_sequencer_gather(bundle_b, "seq_gather_small", SEQ_GATHER_COLLECTIVE_ID),
                                      bundle_b[None], (me, 0, 0))
    bunds = {**{k_: (bund_a.reshape(NDEV, -1), v_) for k_, v_ in offs.items()},
             **{k_: (bund_b.reshape(NDEV, -1), v_) for k_, v_ in offs_b.items()}}
    r13, r2 = [(r13_0a, r13_0b), (r13_1,)], [(r2_0a, r2_0b), (r2_1,)]

    def piece(name):
        b_, (o_, n_) = bunds[name]
        return b_[:, o_:o_ + n_]

    loss = jnp.sum(piece("loss")[:, 0])

    dl_all = lax.dynamic_slice_in_dim(piece("dmod_l"), me * wa, wa, axis=1)
    dc_all = lax.dynamic_slice_in_dim(piece("dmod_c"), me * wa, wa, axis=1)
    g_wada, gc_part = _ada_bwd(c16, w_ada[0], dl_all, dc_all)
    gc_all = lax.dynamic_update_slice(_sequencer_gather(gc_part, "seq_gather_cctx", SEQ_GATHER2_COLLECTIVE_ID),
                                      gc_part[None], (me, 0, 0))

    parts = {
        "c_ctx": gc_all.reshape(NDEV, d),
        "norm_g": lax.dynamic_slice_in_dim(piece("norm_g").reshape(NDEV, 3, d), me * (d // NDEV), d // NDEV,
                                           axis=2).reshape((NDEV,) + norm_g.shape),
        "w_ada": g_wada.reshape((1,) + w_ada.shape),
        "b_ada": jnp.concatenate([piece("dmod_l"), piece("dmod_c")], axis=0).reshape((2 * NDEV,) + b_ada.shape),
        "w_in": r_in.reshape((NDEV,) + w_in.shape),
        "s5_w_glu": r_glu.reshape((NDEV,) + s5_w_glu.shape),
        "w_out": r_out.reshape((NDEV,) + w_out.shape),
    }
    for k_ in ("pool_w", "pool_scale", "s5_a_re", "s5_a_im", "s5_log_dt", "s5_b_re", "s5_b_im", "s5_c_re", "s5_c_im",
               "s5_d", "final_g"):
        parts[k_] = piece(k_).reshape((NDEV,) + weights[k_].shape)

    grads, deltas, new_m, new_v = [], [], [], []
    ffn_parts = {"ffn_w1": (r13, 0, True), "ffn_w3": (r13, FPAD, True), "ffn_w2": (r2, 0, False)}
    for k_ in weights:
        if k_ in ffn_parts:
            shards, row0, transposed = ffn_parts[k_]
            flip = (lambda a: jnp.swapaxes(a, 2, 3)) if transposed else (lambda a: a)
            outs = _adamw_ffn(shards, row0, flip(weights[k_]), flip(mom_m[k_]), flip(mom_v[k_]), "adamw_" + k_)
            g_, d_, m_, v_ = (flip(o_) for o_ in outs)
        else:
            g_, d_, m_, v_ = _adamw(parts[k_], weights[k_], mom_m[k_], mom_v[k_], "adamw_" + k_)
        grads.append(g_)
        deltas.append(d_)
        new_m.append(m_)
        new_v.append(v_)
    return (loss, grad_x, *grads, *deltas, *new_m, *new_v)
```
